```python
import math
import jax
import jax.numpy as jnp
from jax import lax
import numpy as np

D_MODEL = 1024
BATCH = 4
SEQ = 8192
DEPTH = 4

GRID_W = 64
CTX_LEN = 256
HEAD_DIM = 64
ROPE_THETA = 10000.0
NORM_EPS = 1e-6
Q_BLOCK = 128
D_FF = 4 * D_MODEL
N_MOD = 6
MIX_HALF = D_MODEL // 2

GLA_HEADS = 4
GLA_DV = MIX_HALF // GLA_HEADS
GLA_DK = GLA_DV // 2
GLA_RANK = 16
GLA_NORMALIZER = 16.0
GLA_CHUNK = 64
GLA_QK = GLA_HEADS * GLA_DK
GLA_V = GLA_HEADS * GLA_DV

GQA_HEADS = MIX_HALF // HEAD_DIM
GQA_KV_HEADS = 2

HY_CH = MIX_HALF
HY_ORDER = 2
HY_SHORT = 3
HY_BANDS = 16
HY_EMB = 2 * HY_BANDS + 1
HY_FFN = 64
HY_TARGET = 1e-2
HY_FAST_PCT = 0.3
HY_SLOW_PCT = 1.5
HY_COLS = (HY_ORDER + 1) * HY_CH

DIFF_HEADS = MIX_HALF // (2 * HEAD_DIM)
DIFF_QK = DIFF_HEADS * 2 * HEAD_DIM
DIFF_V = DIFF_HEADS * 2 * HEAD_DIM

EVEN_COLS = 2 * GLA_QK + 2 * GLA_V + 2 * GLA_RANK + (GQA_HEADS + 2 * GQA_KV_HEADS) * HEAD_DIM
EVEN_MIX = GLA_V + GQA_HEADS * HEAD_DIM
ODD_COLS = HY_COLS + 2 * DIFF_QK + DIFF_V
ODD_MIX = HY_CH + DIFF_V

kernel_name = "hybrid_gla_gqa_hyena_diffattn_dit"


def rmsnorm(x, g):
    xf = x.astype(jnp.float32)
    y = xf * lax.rsqrt(jnp.mean(xf * xf, axis=-1, keepdims=True) + NORM_EPS)
    return (y * g.astype(jnp.float32)).astype(x.dtype)


def modulate(h, shift, scale):
    return h * (1 + scale) + shift


def split_cols(t, sizes):
    return jnp.split(t, np.cumsum(sizes)[:-1].tolist(), axis=-1)


def sq_relu_mlp(h, w1, w2):
    return jnp.square(jax.nn.relu(h @ w1)) @ w2


def axial_rope_tables(rows):
    r = jnp.broadcast_to(jnp.arange(rows, dtype=jnp.float32)[:, None], (rows, GRID_W)).reshape(-1)
    col = jnp.broadcast_to(jnp.arange(GRID_W, dtype=jnp.float32)[None, :], (rows, GRID_W)).reshape(-1)
    n_pairs = HEAD_DIM // 4
    inv = ROPE_THETA ** (-jnp.arange(n_pairs, dtype=jnp.float32) / n_pairs)
    ang = jnp.concatenate([r[:, None] * inv, col[:, None] * inv], axis=-1)
    return jnp.cos(ang), jnp.sin(ang)


def apply_rope(t, cos, sin):
    L = t.shape[1]
    bshape = (L,) + (1,) * (t.ndim - 3) + (HEAD_DIM // 2,)
    cs, sn = cos.reshape(bshape), sin.reshape(bshape)
    tp = t.astype(jnp.float32).reshape(t.shape[:-1] + (HEAD_DIM // 2, 2))
    x1, x2 = tp[..., 0], tp[..., 1]
    out = jnp.stack([x1 * cs - x2 * sn, x1 * sn + x2 * cs], axis=-1)
    return out.reshape(t.shape).astype(t.dtype)


def gla_chunked(q, k, v, la, s0):
    B, L, H, dk = q.shape
    dv = v.shape[-1]
    n = L // GLA_CHUNK

    def blocks(t):
        return t.astype(jnp.float32).reshape(B, n, GLA_CHUNK, H, t.shape[-1]).transpose(1, 0, 3, 2, 4)

    q, k, v, la = blocks(q), blocks(k), blocks(v), blocks(la)
    b = jnp.cumsum(la, axis=3)
    b_end = b[:, :, :, -1:]
    q_in = q * jnp.exp(b)
    k_in = k * jnp.exp(-b)
    k_end = k * jnp.exp(b_end - b)
    mask = jnp.tril(jnp.ones((GLA_CHUNK, GLA_CHUNK), dtype=bool))
    a = jnp.where(mask, jnp.einsum('nbhid,nbhjd->nbhij', q_in, k_in), 0.0)
    o_intra = jnp.einsum('nbhij,nbhjv->nbhiv', a, v)

    def step(s, inp):
        q_c, k_c, v_c, dec = inp
        o_c = jnp.einsum('bhid,bhdv->bhiv', q_c, s)
        s = s * dec[..., None] + jnp.einsum('bhjd,bhjv->bhdv', k_c, v_c)
        return s, o_c

    s_fin, o_inter = lax.scan(step, s0, (q_in, k_end, v, jnp.exp(b_end[:, :, :, 0])))
    o = (o_intra + o_inter).transpose(1, 0, 3, 2, 4).reshape(B, L, H, dv)
    return o, s_fin


def gla_final_state(k, v, la, s0):
    b = jnp.cumsum(la, axis=1)
    tot = b[:, -1]
    kw = k.astype(jnp.float32) * jnp.exp(tot[:, None] - b)
    return s0 * jnp.exp(tot)[..., None] + jnp.einsum('blhd,blhv->bhdv', kw, v.astype(jnp.float32))


def gla_mixer(p, pc, w_lr, b_lr, gain, ctx_out):
    def prep(q, k, v, g, lr):
        B, L, _ = q.shape
        gk = jnp.einsum('blzr,zrk->blzk', lr.reshape(B, L, 2, GLA_RANK), w_lr) + b_lr
        la = (jax.nn.log_sigmoid(gk.astype(jnp.float32)) / GLA_NORMALIZER).reshape(B, L, 2, GLA_HEADS, GLA_DK)
        return (q.reshape(B, L, GLA_HEADS, GLA_DK) * GLA_DK ** -0.5,
                k.reshape(B, L, GLA_HEADS, GLA_DK),
                v.reshape(B, L, GLA_HEADS, GLA_DV),
                g.reshape(B, L, GLA_HEADS, GLA_DV),
                la)

    q, k, v, g, la = prep(*p)
    qc, kc, vc, gc, lac = prep(*pc)
    s0 = jnp.zeros((q.shape[0], GLA_HEADS, GLA_DK, GLA_DV), jnp.float32)
    o, oc = 0.0, 0.0
    for d in range(2):
        rev = (lambda t: t[:, ::-1]) if d == 1 else (lambda t: t)
        if ctx_out:
            oc_d, s_ctx = gla_chunked(rev(qc), rev(kc), rev(vc), rev(lac[:, :, d]), s0)
            oc = oc + rev(oc_d)
        else:
            s_ctx = gla_final_state(rev(kc), rev(vc), rev(lac[:, :, d]), s0)
        o_d, _ = gla_chunked(rev(q), rev(k), rev(v), rev(la[:, :, d]), s_ctx)
        o = o + rev(o_d)

    def out(o, g):
        B, L = o.shape[:2]
        return (rmsnorm(o.astype(g.dtype), gain) * jax.nn.silu(g)).reshape(B, L, GLA_V)

    return out(o, g), (out(oc, gc) if ctx_out else None)


def gqa_attend(q, keys, vals):
    B, L, hq, hd = q.shape
    hkv = keys.shape[2]
    qb = q.reshape(B, L // Q_BLOCK, Q_BLOCK, hkv, hq // hkv, hd).transpose(1, 0, 2, 3, 4, 5)

    def one(qblk):
        s = jnp.einsum('bqkgd,bskd->bkgqs', qblk, keys).astype(jnp.float32) * hd ** -0.5
        pr = jax.nn.softmax(s, axis=-1).astype(vals.dtype)
        return jnp.einsum('bkgqs,bskd->bqkgd', pr, vals)

    o = lax.map(one, qb)
    return o.transpose(1, 0, 2, 3, 4, 5).reshape(B, L, hq * hd)


def gqa_mixer(p, pc, qk_g, cos, sin, ctx_out):
    def heads(q, k, v):
        B, L, _ = q.shape
        q = rmsnorm(q.reshape(B, L, GQA_HEADS, HEAD_DIM), qk_g[0])
        k = rmsnorm(k.reshape(B, L, GQA_KV_HEADS, HEAD_DIM), qk_g[1])
        return q, k, v.reshape(B, L, GQA_KV_HEADS, HEAD_DIM)

    q, k, v = heads(*p)
    qc, kc, vc = heads(*pc)
    q, k = apply_rope(q, cos, sin), apply_rope(k, cos, sin)
    y = gqa_attend(q, jnp.concatenate([kc, k], axis=1), jnp.concatenate([vc, v], axis=1))
    yc = gqa_attend(qc, kc, vc) if ctx_out else None
    return y, yc


def even_mixer(h, hc, w_in, w_lr, b_lr, gla_g, qk_g, w_out, cos, sin, ctx_out):
    sizes = (GLA_QK, GLA_QK, GLA_V, GLA_V, 2 * GLA_RANK,
             GQA_HEADS * HEAD_DIM, GQA_KV_HEADS * HEAD_DIM, GQA_KV_HEADS * HEAD_DIM)
    p = split_cols(h @ w_in, sizes)
    pc = split_cols(hc @ w_in, sizes)
    y_a, yc_a = gla_mixer(p[:5], pc[:5], w_lr, b_lr, gla_g, ctx_out)
    y_b, yc_b = gqa_mixer(p[5:], pc[5:], qk_g, cos, sin, ctx_out)
    y = jnp.concatenate([y_a, y_b], axis=-1) @ w_out
    yc = (jnp.concatenate([yc_a, yc_b], axis=-1) @ w_out) if ctx_out else None
    return y, yc


def short_conv3(u, w, b):
    up = jnp.pad(u, ((0, 0), (1, 1), (0, 0)))
    return up[:, :-2] * w[0] + up[:, 1:-1] * w[1] + up[:, 2:] * w[2] + b


def hyena_filters(L, w1, b1, w2, b2, w3):
    pos = jnp.arange(L, dtype=jnp.float32)
    t = pos / max(L - 1, 1)
    w = 2 * math.pi * pos / L
    bands = jnp.linspace(1e-4, HY_BANDS - 1, HY_BANDS, dtype=jnp.float32)
    z = jnp.concatenate([t[:, None], jnp.cos(w[:, None] * bands), -jnp.sin(w[:, None] * bands)], axis=-1)
    z = z.astype(w1.dtype)
    hid = jnp.sin(z @ w1 + b1)
    hid = jnp.sin(hid @ w2 + b2)
    h = (hid @ w3).astype(jnp.float32).reshape(L, HY_ORDER, 2, HY_CH)
    deltas = jnp.abs(jnp.linspace(math.log(HY_TARGET) / HY_FAST_PCT, math.log(HY_TARGET) / HY_SLOW_PCT,
                                  HY_CH, dtype=jnp.float32))
    return h * jnp.exp(-t[:, None] * deltas)[:, None, None, :]


def bidir_fft_conv(u, h_fwd, h_bwd):
    L, C = h_fwd.shape
    k = jnp.concatenate([h_fwd, jnp.zeros((1, C), h_fwd.dtype), h_bwd[1:][::-1]], axis=0)
    uf = jnp.fft.rfft(u.astype(jnp.float32), n=2 * L, axis=1)
    kf = jnp.fft.rfft(k, n=2 * L, axis=0)
    y = jnp.fft.irfft(uf * kf[None], n=2 * L, axis=1)[:, :L]
    return y.astype(u.dtype)


def hyena_mixer(u, uc, conv_w, conv_b, f_w1, f_b1, f_w2, f_b2, f_w3, hy_bias, ctx_out):
    def run(u):
        L = u.shape[1]
        v, x1, x2 = jnp.split(short_conv3(u, conv_w, conv_b), HY_ORDER + 1, axis=-1)
        h = hyena_filters(L, f_w1, f_b1, f_w2, f_b2, f_w3)
        z = v
        for o, gate in enumerate((x1, x2)):
            z = gate * (bidir_fft_conv(z, h[:, o, 0], h[:, o, 1]) + z * hy_bias[o])
        return z

    return run(u), (run(uc) if ctx_out else None)


def diff_attend(q, keys, vals, lam):
    B, L, H, _, hd = q.shape
    qb = q.reshape(B, L // Q_BLOCK, Q_BLOCK, H, 2, hd).transpose(1, 0, 2, 3, 4, 5)

    def one(qblk):
        s = jnp.einsum('bqhcd,bshcd->bhcqs', qblk, keys).astype(jnp.float32) * hd ** -0.5
        pr = jax.nn.softmax(s, axis=-1)
        a = (pr[:, :, 0] - lam * pr[:, :, 1]).astype(vals.dtype)
        return jnp.einsum('bhqs,bshv->bqhv', a, vals)

    o = lax.map(one, qb)
    return o.transpose(1, 0, 2, 3, 4).reshape(B, L, H, vals.shape[-1])


def diff_mixer(p, pc, lam_p, gain, layer_idx, cos, sin, ctx_out):
    lam_init = 0.8 - 0.6 * math.exp(-0.3 * layer_idx)
    lp = lam_p.astype(jnp.float32)
    lam = jnp.exp(jnp.sum(lp[0] * lp[1])) - jnp.exp(jnp.sum(lp[2] * lp[3])) + lam_init

    def heads(q, k, v):
        B, L, _ = q.shape
        return (q.reshape(B, L, DIFF_HEADS, 2, HEAD_DIM), k.reshape(B, L, DIFF_HEADS, 2, HEAD_DIM),
                v.reshape(B, L, DIFF_HEADS, 2 * HEAD_DIM))

    q, k, v = heads(*p)
    qc, kc, vc = heads(*pc)
    q, k = apply_rope(q, cos, sin), apply_rope(k, cos, sin)

    def out(o):
        B, L = o.shape[:2]
        return (rmsnorm(o, gain) * (1 - lam_init)).reshape(B, L, DIFF_V)

    y = out(diff_attend(q, jnp.concatenate([kc, k], axis=1), jnp.concatenate([vc, v], axis=1), lam))
    yc = out(diff_attend(qc, kc, vc, lam)) if ctx_out else None
    return y, yc


def odd_mixer(h, hc, w_in, conv_w, conv_b, f_w1, f_b1, f_w2, f_b2, f_w3, hy_bias, lam_p, diff_g, w_out,
              cos, sin, layer_idx, ctx_out):
    sizes = (HY_COLS, DIFF_QK, DIFF_QK, DIFF_V)
    p = split_cols(h @ w_in, sizes)
    pc = split_cols(hc @ w_in, sizes)
    y_c, yc_c = hyena_mixer(p[0], pc[0], conv_w, conv_b, f_w1, f_b1, f_w2, f_b2, f_w3, hy_bias, ctx_out)
    y_d, yc_d = diff_mixer(p[1:], pc[1:], lam_p, diff_g, layer_idx, cos, sin, ctx_out)
    y = jnp.concatenate([y_c, y_d], axis=-1) @ w_out
    yc = (jnp.concatenate([yc_c, yc_d], axis=-1) @ w_out) if ctx_out else None
    return y, yc


def setup_inputs(seed: int = 0) -> dict:
    key = jax.random.key(seed)
    ks = jax.random.split(key, 28)
    ne, no = (DEPTH + 1) // 2, DEPTH // 2
    f32 = jnp.float32

    def nrm(k, shape, fan_in, scale=1.0):
        return jax.random.normal(k, shape, f32) * (scale * fan_in ** -0.5)

    def gain(k, shape):
        return 1.0 + 0.05 * jax.random.normal(k, shape, f32)

    def small(k, shape, s):
        return s * jax.random.normal(k, shape, f32)

    return {
        "x": jax.random.normal(ks[0], (BATCH, SEQ, D_MODEL), f32),
        "c": jax.random.normal(ks[1], (BATCH, D_MODEL), f32),
        "ctx": jax.random.normal(ks[2], (BATCH, CTX_LEN, D_MODEL), f32),
        "c_ctx": jax.random.normal(ks[3], (D_MODEL,), f32),
        "w_ada": nrm(ks[4], (DEPTH, D_MODEL, N_MOD * D_MODEL), D_MODEL),
        "b_ada": small(ks[5], (DEPTH, N_MOD * D_MODEL), 0.02),
        "norm_g": gain(ks[6], (DEPTH, 4, D_MODEL)),
        "w_mlp_in": nrm(ks[7], (DEPTH, D_MODEL, D_FF), D_MODEL),
        "w_mlp_out": nrm(ks[8], (DEPTH, D_FF, D_MODEL), D_FF),
        "ev_w_in": nrm(ks[9], (ne, D_MODEL, EVEN_COLS), D_MODEL),
        "ev_w_lr": nrm(ks[10], (ne, 2, GLA_RANK, GLA_QK), GLA_RANK),
        "ev_b_lr": small(ks[11], (ne, 2, GLA_QK), 0.1),
        "ev_gla_g": gain(ks[12], (ne, GLA_DV)),
        "ev_qk_g": gain(ks[13], (ne, 2, HEAD_DIM)),
        "ev_w_out": nrm(ks[14], (ne, EVEN_MIX, D_MODEL), EVEN_MIX),
        "od_w_in": nrm(ks[15], (no, D_MODEL, ODD_COLS), D_MODEL),
        "od_conv_w": nrm(ks[16], (no, HY_SHORT, HY_COLS), HY_SHORT),
        "od_conv_b": small(ks[17], (no, HY_COLS), 0.02),
        "od_f_w1": nrm(ks[18], (no, HY_EMB, HY_FFN), HY_EMB),
        "od_f_b1": small(ks[19], (no, HY_FFN), 0.1),
        "od_f_w2": nrm(ks[20], (no, HY_FFN, HY_FFN), HY_FFN),
        "od_f_b2": small(ks[21], (no, HY_FFN), 0.1),
        "od_f_w3": nrm(ks[22], (no, HY_FFN, HY_ORDER * 2 * HY_CH), HY_FFN, 0.05),
        "od_hy_bias": small(ks[23], (no, HY_ORDER, HY_CH), 0.5),
        "od_lam": small(ks[24], (no, 4, HEAD_DIM), 0.1),
        "od_diff_g": gain(ks[25], (no, 2 * HEAD_DIM)),
        "od_w_out": nrm(ks[26], (no, ODD_MIX, D_MODEL), ODD_MIX),
    }


def reference(x, c, ctx, c_ctx, w_ada, b_ada, norm_g, w_mlp_in, w_mlp_out,
              ev_w_in, ev_w_lr, ev_b_lr, ev_gla_g, ev_qk_g, ev_w_out,
              od_w_in, od_conv_w, od_conv_b, od_f_w1, od_f_b1, od_f_w2, od_f_b2, od_f_w3,
              od_hy_bias, od_lam, od_diff_g, od_w_out):
    ROWS = x.shape[1] // GRID_W
    cos, sin = axial_rope_tables(ROWS)
    sc = jax.nn.silu(c)
    scc = jax.nn.silu(c_ctx)
    for i in range(DEPTH):
        last = i == DEPTH - 1
        j = i // 2
        g = norm_g[i]
        m = [t[:, None, :] for t in jnp.split(sc @ w_ada[i] + b_ada[i], N_MOD, axis=-1)]
        mc = jnp.split(scc @ w_ada[i] + b_ada[i], N_MOD, axis=-1)
        h = modulate(rmsnorm(x, g[0]), m[0], m[1])
        hc = modulate(rmsnorm(ctx, g[0]), mc[0], mc[1])
        if i % 2 == 0:
            y, yc = even_mixer(h, hc, ev_w_in[j], ev_w_lr[j], ev_b_lr[j], ev_gla_g[j], ev_qk_g[j], ev_w_out[j],
                               cos, sin, not last)
        else:
            y, yc = odd_mixer(h, hc, od_w_in[j], od_conv_w[j], od_conv_b[j], od_f_w1[j], od_f_b1[j], od_f_w2[j],
                              od_f_b2[j], od_f_w3[j], od_hy_bias[j], od_lam[j], od_diff_g[j], od_w_out[j],
                              cos, sin, i, not last)
        x = x + m[2] * rmsnorm(y, g[1])
        x = x + m[5] * rmsnorm(sq_relu_mlp(modulate(rmsnorm(x, g[2]), m[3], m[4]), w_mlp_in[i], w_mlp_out[i]), g[3])
        if not last:
            ctx = ctx + mc[2] * rmsnorm(yc, g[1])
            ctx = ctx + mc[5] * rmsnorm(
                sq_relu_mlp(modulate(rmsnorm(ctx, g[2]), mc[3], mc[4]), w_mlp_in[i], w_mlp_out[i]), g[3])
    return x
```

```python
import functools
import math

import jax
import jax.numpy as jnp
import numpy as np
from jax import lax
from jax.experimental import pallas as pl
from jax.experimental.pallas import tpu as pltpu

F32 = jnp.float32
BF16 = jnp.bfloat16

TM = 256
GRID_W = 64
HEAD_DIM = 64
ROPE_THETA = 10000.0
NORM_EPS = 1e-6
N_MOD = 6
VMEM_LIMIT = 56 * 1024 * 1024

GLA_HEADS = 4
GLA_DK = 64
GLA_DV = 128
GLA_RANK = 16
GLA_NORMALIZER = 16.0
GLA_CHUNK = 64
GLA_QK = GLA_HEADS * GLA_DK
GLA_V = GLA_HEADS * GLA_DV
GQA_HEADS = 8
GQA_KV_HEADS = 2
HY_CH = 512
HY_ORDER = 2
HY_BANDS = 16
HY_TARGET = 1e-2
HY_FAST_PCT = 0.3
HY_SLOW_PCT = 1.5
HY_COLS = (HY_ORDER + 1) * HY_CH
DIFF_HEADS = 4
DIFF_QK = DIFF_HEADS * 2 * HEAD_DIM
DIFF_V = DIFF_HEADS * 2 * HEAD_DIM


def _params(*sem):
    return pltpu.CompilerParams(dimension_semantics=sem, vmem_limit_bytes=VMEM_LIMIT)


def _const_spec(shape):
    nd = len(shape)
    return pl.BlockSpec(shape, lambda *_: (0,) * nd, pipeline_mode=pl.Buffered(1))


def _rms(x, g):
    return x * lax.rsqrt(jnp.mean(x * x, axis=-1, keepdims=True) + NORM_EPS) * g


def _mod_row(b, i):
    return jnp.where(i == 0, 4, b)


def _mods_kernel(cc_ref, w_ref, b_ref, o_ref):
    cc = cc_ref[...]
    s = cc / (1.0 + jnp.exp(-cc))
    o_ref[0] = jnp.dot(s.astype(BF16), w_ref[0].astype(BF16), preferred_element_type=F32) + b_ref[0]


def _mods(cc, w_ada, b_ada):
    depth, d, n = w_ada.shape
    nb = n // 4
    return pl.pallas_call(
        _mods_kernel,
        grid=(depth, n // nb),
        in_specs=[pl.BlockSpec((8, d), lambda l, j: (0, 0)),
                  pl.BlockSpec((1, d, nb), lambda l, j: (l, 0, j)),
                  pl.BlockSpec((1, 1, nb), lambda l, j: (l, 0, j))],
        out_specs=pl.BlockSpec((1, 8, nb), lambda l, j: (l, 0, j)),
        out_shape=jax.ShapeDtypeStruct((depth, 8, n), F32),
        compiler_params=_params("parallel", "parallel"),
        name="ada_mods",
    )(cc, w_ada, b_ada.reshape(depth, 1, n))


def _normed_input(xs_ref, mods_ref, g_ref):
    x = xs_ref[0]
    mods = mods_ref[0]
    h = _rms(x, g_ref[0:1]) * (1.0 + mods[1:2]) + mods[0:1]
    return h.astype(BF16)


def _rope_rows(y, cos, sin):
    x1, x2 = y[:32], y[32:]
    return x1 * cos - x2 * sin, x1 * sin + x2 * cos


def _log_sigmoid(x):
    return jnp.minimum(x, 0.0) - jnp.log(1.0 + jnp.exp(-jnp.abs(x)))


def _even_in_kernel(xs_ref, mods_ref, g_ref, wn_ref, wt_ref, wlr_ref, blr_ref, gq_ref, gk_ref,
                    cos_ref, sin_ref, gla_ref, la_ref, aq_ref, ak_ref, av_ref):
    h = _normed_input(xs_ref, mods_ref, g_ref)
    p = jnp.dot(h, wn_ref[...], preferred_element_type=F32)
    gla_ref[0] = p[:, :2 * GLA_QK + 2 * GLA_V]
    lr = p[:, 2 * GLA_QK + 2 * GLA_V:].astype(BF16)
    gk = jnp.dot(lr, wlr_ref[...], preferred_element_type=F32) + blr_ref[...]
    la_ref[0] = _log_sigmoid(gk) * (1.0 / GLA_NORMALIZER)

    pt = lax.dot_general(wt_ref[...], h, (((1,), (1,)), ((), ())), preferred_element_type=F32)
    cos, sin = cos_ref[...], sin_ref[...]
    gq, gk_gain = gq_ref[...], gk_ref[...]
    for hh in range(GQA_HEADS):
        blk = pt[hh * 64:(hh + 1) * 64]
        y = blk * lax.rsqrt(jnp.mean(blk * blk, axis=0, keepdims=True) + NORM_EPS) * gq
        o1, o2 = _rope_rows(y, cos, sin)
        aq_ref[0, 0, hh * 64:hh * 64 + 32, :] = (o1 * HEAD_DIM ** -0.5).astype(BF16)
        aq_ref[0, 0, hh * 64 + 32:hh * 64 + 64, :] = (o2 * HEAD_DIM ** -0.5).astype(BF16)
    kparts = []
    for hh in range(GQA_KV_HEADS):
        blk = pt[512 + hh * 64:512 + (hh + 1) * 64]
        y = blk * lax.rsqrt(jnp.mean(blk * blk, axis=0, keepdims=True) + NORM_EPS) * gk_gain
        o1, o2 = _rope_rows(y, cos, sin)
        kparts += [o1, o2]
    kt = jnp.concatenate(kparts, axis=0)
    ak_ref[0] = kt.T.astype(BF16)
    av_ref[0, 0] = pt[640:768].astype(BF16)


def _odd_in_kernel(xs_ref, mods_ref, g_ref, wn_ref, wt_ref, cos_ref, sin_ref,
                   u_ref, dq_ref, dk_ref, dv_ref):
    h = _normed_input(xs_ref, mods_ref, g_ref)
    u_ref[0] = jnp.dot(h, wn_ref[...], preferred_element_type=F32)
    pt = lax.dot_general(wt_ref[...], h, (((1,), (1,)), ((), ())), preferred_element_type=F32)
    cos, sin = cos_ref[...], sin_ref[...]
    kparts = []
    for hh in range(2 * DIFF_HEADS):
        o1, o2 = _rope_rows(pt[hh * 64:(hh + 1) * 64], cos, sin)
        dq_ref[0, 0, hh * 64:hh * 64 + 32, :] = (o1 * HEAD_DIM ** -0.5).astype(BF16)
        dq_ref[0, 0, hh * 64 + 32:hh * 64 + 64, :] = (o2 * HEAD_DIM ** -0.5).astype(BF16)
        k1, k2 = _rope_rows(pt[DIFF_QK + hh * 64:DIFF_QK + (hh + 1) * 64], cos, sin)
        kparts += [k1, k2]
    kt = jnp.concatenate(kparts, axis=0)
    dk_ref[0] = kt.T.astype(BF16)
    dv_ref[0, 0] = pt[2 * DIFF_QK:].astype(BF16)


def _pair_perm(n_heads):
    base = np.concatenate([np.arange(0, HEAD_DIM, 2), np.arange(1, HEAD_DIM, 2)])
    return np.concatenate([h * HEAD_DIM + base for h in range(n_heads)])


def _even_in(xs, mods_l, g0, w_in, w_lr, b_lr, qk_g, cos_t, sin_t):
    b, t, d = xs.shape
    nt = t // TM
    n_gla = 2 * GLA_QK + 2 * GLA_V
    c_lr = n_gla + 2 * GLA_RANK
    wn = jnp.concatenate([w_in[:, :c_lr], jnp.zeros((d, 128 - 2 * GLA_RANK), F32)], axis=1).astype(BF16)
    cq = c_lr + GQA_HEADS * HEAD_DIM
    ck = cq + GQA_KV_HEADS * HEAD_DIM
    wq = w_in[:, c_lr:cq][:, _pair_perm(GQA_HEADS)]
    wk = w_in[:, cq:ck][:, _pair_perm(GQA_KV_HEADS)]
    wt = jnp.concatenate([wq, wk, w_in[:, ck:]], axis=1).T.astype(BF16)
    wlr = jnp.zeros((128, 2 * GLA_QK), F32)
    wlr = wlr.at[:GLA_RANK, :GLA_QK].set(w_lr[0]).at[GLA_RANK:2 * GLA_RANK, GLA_QK:].set(w_lr[1]).astype(BF16)
    blr = b_lr.reshape(1, 2 * GLA_QK)
    perm = _pair_perm(1)
    gq = qk_g[0][perm].reshape(HEAD_DIM, 1)
    gk = qk_g[1][perm].reshape(HEAD_DIM, 1)
    tok = lambda n: pl.BlockSpec((1, TM, n), lambda bb, i: (bb, i, 0))
    tr = lambda n: pl.BlockSpec((1, 1, n, TM), lambda bb, i: (bb, i, 0, 0))
    return pl.pallas_call(
        _even_in_kernel,
        grid=(b, nt),
        in_specs=[tok(d),
                  pl.BlockSpec((1, N_MOD, d), lambda bb, i: (_mod_row(bb, i), 0, 0)),
                  _const_spec((1, d)),
                  _const_spec(wn.shape), _const_spec(wt.shape), _const_spec(wlr.shape), _const_spec(blr.shape),
                  _const_spec(gq.shape), _const_spec(gk.shape),
                  pl.BlockSpec((32, TM), lambda bb, i: (0, i)),
                  pl.BlockSpec((32, TM), lambda bb, i: (0, i))],
        out_specs=[tok(n_gla), tok(2 * GLA_QK), tr(512), tok(128), tr(128)],
        out_shape=[jax.ShapeDtypeStruct((b, t, n_gla), F32),
                   jax.ShapeDtypeStruct((b, t, 2 * GLA_QK), F32),
                   jax.ShapeDtypeStruct((b, nt, 512, TM), BF16),
                   jax.ShapeDtypeStruct((b, t, 128), BF16),
                   jax.ShapeDtypeStruct((b, nt, 128, TM), BF16)],
        compiler_params=_params("parallel", "parallel"),
        name="even_in",
    )(xs, mods_l, g0, wn, wt, wlr, blr, gq, gk, cos_t, sin_t)


def _odd_in(xs, mods_l, g0, w_in, cos_t, sin_t):
    b, t, d = xs.shape
    nt = t // TM
    wn = w_in[:, :HY_COLS].astype(BF16)
    perm = _pair_perm(2 * DIFF_HEADS)
    wq = w_in[:, HY_COLS:HY_COLS + DIFF_QK][:, perm]
    wk = w_in[:, HY_COLS + DIFF_QK:HY_COLS + 2 * DIFF_QK][:, perm]
    wt = jnp.concatenate([wq, wk, w_in[:, HY_COLS + 2 * DIFF_QK:]], axis=1).T.astype(BF16)
    tok = lambda n: pl.BlockSpec((1, TM, n), lambda bb, i: (bb, i, 0))
    tr = lambda n: pl.BlockSpec((1, 1, n, TM), lambda bb, i: (bb, i, 0, 0))
    return pl.pallas_call(
        _odd_in_kernel,
        grid=(b, nt),
        in_specs=[tok(d),
                  pl.BlockSpec((1, N_MOD, d), lambda bb, i: (_mod_row(bb, i), 0, 0)),
                  _const_spec((1, d)),
                  _const_spec(wn.shape), _const_spec(wt.shape),
                  pl.BlockSpec((32, TM), lambda bb, i: (0, i)),
                  pl.BlockSpec((32, TM), lambda bb, i: (0, i))],
        out_specs=[tok(HY_COLS), tr(DIFF_QK), tok(DIFF_QK), tr(DIFF_V)],
        out_shape=[jax.ShapeDtypeStruct((b, t, HY_COLS), F32),
                   jax.ShapeDtypeStruct((b, nt, DIFF_QK, TM), BF16),
                   jax.ShapeDtypeStruct((b, t, DIFF_QK), BF16),
                   jax.ShapeDtypeStruct((b, nt, DIFF_V, TM), BF16)],
        compiler_params=_params("parallel", "parallel"),
        name="odd_in",
    )(xs, mods_l, g0, wn, wt, cos_t, sin_t)


def _softmax_step(s, m_ref, l_ref, idx):
    m_old = m_ref[idx]
    m_new = jnp.maximum(m_old, jnp.max(s, axis=0, keepdims=True))
    p = jnp.exp(s - m_new)
    alpha = jnp.exp(m_old - m_new)
    l_ref[idx] = alpha * l_ref[idx] + jnp.sum(p, axis=0, keepdims=True)
    m_ref[idx] = m_new
    return p, alpha


def _gqa_kernel(q_ref, k_ref, v_ref, o_ref, qpad_ref, m_ref, l_ref, acc_ref, *, n_kv):
    i = pl.program_id(1)
    zeros = jnp.zeros((HEAD_DIM, TM), BF16)
    group = GQA_HEADS // GQA_KV_HEADS
    for hh in range(GQA_HEADS):
        qh = q_ref[0, 0, hh * 64:(hh + 1) * 64, :]
        qpad_ref[hh] = jnp.concatenate([qh, zeros] if hh // group == 0 else [zeros, qh], axis=0)
    m_ref[...] = jnp.full(m_ref.shape, -jnp.inf, F32)
    l_ref[...] = jnp.zeros(l_ref.shape, F32)
    acc_ref[...] = jnp.zeros(acc_ref.shape, F32)

    def body(j, carry):
        kc = k_ref[0, pl.ds(pl.multiple_of(j * TM, TM), TM), :]
        vc = v_ref[0, j]
        for hh in range(GQA_HEADS):
            g = hh // group
            s = jnp.dot(kc, qpad_ref[hh], preferred_element_type=F32)
            p, alpha = _softmax_step(s, m_ref, l_ref, hh)
            pv = jnp.dot(vc[g * 64:(g + 1) * 64], p.astype(BF16), preferred_element_type=F32)
            acc_ref[hh * 64:(hh + 1) * 64, :] = acc_ref[hh * 64:(hh + 1) * 64, :] * alpha + pv
        return carry

    lax.fori_loop(0, jnp.where(i == 0, 1, n_kv), body, 0)
    for hh in range(GQA_HEADS):
        o_ref[0, 0, hh * 64:(hh + 1) * 64, :] = (acc_ref[hh * 64:(hh + 1) * 64, :] / l_ref[hh]).astype(o_ref.dtype)


def _gqa_attention(aq, ak, av):
    b, nt, nq, _ = aq.shape
    t = nt * TM
    return pl.pallas_call(
        functools.partial(_gqa_kernel, n_kv=nt),
        grid=(b, nt),
        in_specs=[pl.BlockSpec((1, 1, nq, TM), lambda bb, i: (bb, i, 0, 0)),
                  pl.BlockSpec((1, t, 128), lambda bb, i: (bb, 0, 0)),
                  pl.BlockSpec((1, nt, 128, TM), lambda bb, i: (bb, 0, 0, 0))],
        out_specs=pl.BlockSpec((1, 1, nq, TM), lambda bb, i: (bb, i, 0, 0)),
        out_shape=jax.ShapeDtypeStruct((b, nt, nq, TM), BF16),
        scratch_shapes=[pltpu.VMEM((GQA_HEADS, 128, TM), BF16),
                        pltpu.VMEM((GQA_HEADS, 1, TM), F32),
                        pltpu.VMEM((GQA_HEADS, 1, TM), F32),
                        pltpu.VMEM((GQA_HEADS * HEAD_DIM, TM), F32)],
        compiler_params=_params("parallel", "arbitrary"),
        name="gqa_attn",
    )(aq, ak, av)


def _diff_kernel(q_ref, k_ref, v_ref, lam_ref, gain_ref, o_ref, qpad_ref, m_ref, l_ref, acc_ref,
                 *, n_kv, lam_init):
    i = pl.program_id(1)
    zeros = jnp.zeros((HEAD_DIM, TM), BF16)
    for hc in range(2 * DIFF_HEADS):
        qh = q_ref[0, 0, hc * 64:(hc + 1) * 64, :]
        qpad_ref[hc] = jnp.concatenate([qh, zeros] if hc % 2 == 0 else [zeros, qh], axis=0)
    m_ref[...] = jnp.full(m_ref.shape, -jnp.inf, F32)
    l_ref[...] = jnp.zeros(l_ref.shape, F32)
    acc_ref[...] = jnp.zeros(acc_ref.shape, F32)

    def body(j, carry):
        kc = k_ref[0, pl.ds(pl.multiple_of(j * TM, TM), TM), :]
        vc = v_ref[0, j]
        for hc in range(2 * DIFF_HEADS):
            hh = hc // 2
            s = jnp.dot(kc[:, hh * 128:(hh + 1) * 128], qpad_ref[hc], preferred_element_type=F32)
            p, alpha = _softmax_step(s, m_ref, l_ref, hc)
            pv = jnp.dot(vc[hh * 128:(hh + 1) * 128], p.astype(BF16), preferred_element_type=F32)
            acc_ref[hc] = acc_ref[hc] * alpha + pv
        return carry

    lax.fori_loop(0, jnp.where(i == 0, 1, n_kv), body, 0)
    lp = lam_ref[...]
    lam = (jnp.exp(jnp.sum(lp[0:1] * lp[1:2], axis=1, keepdims=True))
           - jnp.exp(jnp.sum(lp[2:3] * lp[3:4], axis=1, keepdims=True)) + lam_init)
    gain = gain_ref[...]
    for hh in range(DIFF_HEADS):
        o = acc_ref[2 * hh] / l_ref[2 * hh] - lam * (acc_ref[2 * hh + 1] / l_ref[2 * hh + 1])
        y = o * lax.rsqrt(jnp.mean(o * o, axis=0, keepdims=True) + NORM_EPS) * gain
        o_ref[0, 0, hh * 128:(hh + 1) * 128, :] = (y * (1.0 - lam_init)).astype(o_ref.dtype)


def _diff_attention(dq, dk, dv, lam_p, gain, layer_idx):
    b, nt, nq, _ = dq.shape
    t = nt * TM
    lam_init = 0.8 - 0.6 * math.exp(-0.3 * layer_idx)
    return pl.pallas_call(
        functools.partial(_diff_kernel, n_kv=nt, lam_init=lam_init),
        grid=(b, nt),
        in_specs=[pl.BlockSpec((1, 1, nq, TM), lambda bb, i: (bb, i, 0, 0)),
                  pl.BlockSpec((1, t, DIFF_QK), lambda bb, i: (bb, 0, 0)),
                  pl.BlockSpec((1, nt, DIFF_V, TM), lambda bb, i: (bb, 0, 0, 0)),
                  _const_spec(lam_p.shape),
                  _const_spec((2 * HEAD_DIM, 1))],
        out_specs=pl.BlockSpec((1, 1, DIFF_V, TM), lambda bb, i: (bb, i, 0, 0)),
        out_shape=jax.ShapeDtypeStruct((b, nt, DIFF_V, TM), BF16),
        scratch_shapes=[pltpu.VMEM((2 * DIFF_HEADS, 128, TM), BF16),
                        pltpu.VMEM((2 * DIFF_HEADS, 1, TM), F32),
                        pltpu.VMEM((2 * DIFF_HEADS, 1, TM), F32),
                        pltpu.VMEM((2 * DIFF_HEADS, 128, TM), F32)],
        compiler_params=_params("parallel", "arbitrary"),
        name="diff_attn",
    )(dq, dk, dv, lam_p, gain.reshape(2 * HEAD_DIM, 1))


def _post_kernel(xs_ref, ya_ref, yb_ref, mods_ref, g_ref, woa_ref, wob_ref, w1_ref, w2_ref, o_ref, *, ff_chunk):
    x = xs_ref[0]
    mods = mods_ref[0]
    g = g_ref[...]
    ya = ya_ref[0].astype(BF16)
    yb = yb_ref[0, 0].astype(F32).T.astype(BF16)
    y = (jnp.dot(ya, woa_ref[...], preferred_element_type=F32)
         + jnp.dot(yb, wob_ref[...], preferred_element_type=F32))
    x1 = x + mods[2:3] * _rms(y, g[1:2])
    h = (_rms(x1, g[2:3]) * (1.0 + mods[4:5]) + mods[3:4]).astype(BF16)
    d_ff = w1_ref.shape[1]
    acc = jnp.zeros(x.shape, F32)
    for c in range(d_ff // ff_chunk):
        u = jnp.dot(h, w1_ref[:, c * ff_chunk:(c + 1) * ff_chunk], preferred_element_type=F32)
        u = jnp.square(jnp.maximum(u, 0.0)).astype(BF16)
        acc = acc + jnp.dot(u, w2_ref[c * ff_chunk:(c + 1) * ff_chunk, :], preferred_element_type=F32)
    o_ref[0] = x1 + mods[5:6] * _rms(acc, g[3:4])


def _post(xs, ya, yb_t, mods_l, g, w_out, w1, w2, skip_ctx):
    b, t, d = xs.shape
    nt = t // TM
    half = w_out.shape[0] // 2
    woa = w_out[:half].astype(BF16)
    wob = w_out[half:].astype(BF16)
    w1 = w1.astype(BF16)
    w2 = w2.astype(BF16)
    off = 1 if skip_ctx else 0
    return pl.pallas_call(
        functools.partial(_post_kernel, ff_chunk=1024),
        grid=(b, nt - off),
        in_specs=[pl.BlockSpec((1, TM, d), lambda bb, i: (bb, i + off, 0)),
                  pl.BlockSpec((1, TM, half), lambda bb, i: (bb, i + off, 0)),
                  pl.BlockSpec((1, 1, half, TM), lambda bb, i: (bb, i + off, 0, 0)),
                  pl.BlockSpec((1, N_MOD, d), lambda bb, i: (_mod_row(bb, i + off), 0, 0)),
                  _const_spec(g.shape),
                  _const_spec(woa.shape), _const_spec(wob.shape), _const_spec(w1.shape), _const_spec(w2.shape)],
        out_specs=pl.BlockSpec((1, TM, d), lambda bb, i: (bb, i, 0)),
        out_shape=jax.ShapeDtypeStruct((b, t - off * TM, d), F32),
        compiler_params=_params("parallel", "parallel"),
        name="post",
    )(xs, ya, yb_t, mods_l, g, woa, wob, w1, w2)


def _rmsnorm_j(x, g):
    xf = x.astype(F32)
    return xf * lax.rsqrt(jnp.mean(xf * xf, axis=-1, keepdims=True) + NORM_EPS) * g


def _gla_chunked_j(q, k, v, la, s0):
    B, L, H, dk = q.shape
    dv = v.shape[-1]
    n = L // GLA_CHUNK

    def blocks(t):
        return t.reshape(B, n, GLA_CHUNK, H, t.shape[-1]).transpose(1, 0, 3, 2, 4)

    q, k, v, la = blocks(q), blocks(k), blocks(v), blocks(la)
    bb = jnp.cumsum(la, axis=3)
    b_end = bb[:, :, :, -1:]
    q_in = q * jnp.exp(bb)
    k_in = k * jnp.exp(-bb)
    k_end = k * jnp.exp(b_end - bb)
    mask = jnp.tril(jnp.ones((GLA_CHUNK, GLA_CHUNK), dtype=bool))
    a = jnp.where(mask, jnp.einsum('nbhid,nbhjd->nbhij', q_in, k_in), 0.0)
    o_intra = jnp.einsum('nbhij,nbhjv->nbhiv', a, v)

    def step(s, inp):
        q_c, k_c, v_c, dec = inp
        o_c = jnp.einsum('bhid,bhdv->bhiv', q_c, s)
        s = s * dec[..., None] + jnp.einsum('bhjd,bhjv->bhdv', k_c, v_c)
        return s, o_c

    s_fin, o_inter = lax.scan(step, s0, (q_in, k_end, v, jnp.exp(b_end[:, :, :, 0])))
    o = (o_intra + o_inter).transpose(1, 0, 3, 2, 4).reshape(B, L, H, dv)
    return o, s_fin


def _gla_j(gla, la, gain, lc):
    B, T, _ = gla.shape
    q = (gla[..., :GLA_QK] * GLA_DK ** -0.5).reshape(B, T, GLA_HEADS, GLA_DK)
    k = gla[..., GLA_QK:2 * GLA_QK].reshape(B, T, GLA_HEADS, GLA_DK)
    v = gla[..., 2 * GLA_QK:2 * GLA_QK + GLA_V].reshape(B, T, GLA_HEADS, GLA_DV)
    g = gla[..., 2 * GLA_QK + GLA_V:].reshape(B, T, GLA_HEADS, GLA_DV)
    la = la.reshape(B, T, 2, GLA_HEADS, GLA_DK)
    s0 = jnp.zeros((B, GLA_HEADS, GLA_DK, GLA_DV), F32)
    o_f, _ = _gla_chunked_j(q, k, v, la[:, :, 0], s0)
    rev = lambda t_: t_[:, ::-1]
    oc_b, s_ctx = _gla_chunked_j(rev(q[:, :lc]), rev(k[:, :lc]), rev(v[:, :lc]), rev(la[:, :lc, 1]), s0)
    ox_b, _ = _gla_chunked_j(rev(q[:, lc:]), rev(k[:, lc:]), rev(v[:, lc:]), rev(la[:, lc:, 1]), s_ctx)
    o = o_f + jnp.concatenate([rev(oc_b), rev(ox_b)], axis=1)
    return (_rmsnorm_j(o, gain) * jax.nn.silu(g)).reshape(B, T, GLA_V)


def _hyena_filters_j(L, w1, b1, w2, b2, w3):
    pos = jnp.arange(L, dtype=F32)
    t = pos / max(L - 1, 1)
    w = 2 * math.pi * pos / L
    bands = jnp.linspace(1e-4, HY_BANDS - 1, HY_BANDS, dtype=F32)
    z = jnp.concatenate([t[:, None], jnp.cos(w[:, None] * bands), -jnp.sin(w[:, None] * bands)], axis=-1)
    hid = jnp.sin(z @ w1 + b1)
    hid = jnp.sin(hid @ w2 + b2)
    h = (hid @ w3).reshape(L, HY_ORDER, 2, HY_CH)
    deltas = jnp.abs(jnp.linspace(math.log(HY_TARGET) / HY_FAST_PCT, math.log(HY_TARGET) / HY_SLOW_PCT,
                                  HY_CH, dtype=F32))
    return h * jnp.exp(-t[:, None] * deltas)[:, None, None, :]


def _bidir_fft_conv_j(u, h_fwd, h_bwd):
    L, C = h_fwd.shape
    k = jnp.concatenate([h_fwd, jnp.zeros((1, C), h_fwd.dtype), h_bwd[1:][::-1]], axis=0)
    uf = jnp.fft.rfft(u, n=2 * L, axis=1)
    kf = jnp.fft.rfft(k, n=2 * L, axis=0)
    return jnp.fft.irfft(uf * kf[None], n=2 * L, axis=1)[:, :L]


def _hyena_j(u, conv_w, conv_b, f_w1, f_b1, f_w2, f_b2, f_w3, hy_bias, lc):
    def run(u):
        L = u.shape[1]
        up = jnp.pad(u, ((0, 0), (1, 1), (0, 0)))
        s = up[:, :-2] * conv_w[0] + up[:, 1:-1] * conv_w[1] + up[:, 2:] * conv_w[2] + conv_b
        v, x1, x2 = jnp.split(s, HY_ORDER + 1, axis=-1)
        h = _hyena_filters_j(L, f_w1, f_b1, f_w2, f_b2, f_w3)
        z = v
        for o, gate in enumerate((x1, x2)):
            z = gate * (_bidir_fft_conv_j(z, h[:, o, 0], h[:, o, 1]) + z * hy_bias[o])
        return z

    return jnp.concatenate([run(u[:, :lc]), run(u[:, lc:])], axis=1)


def _rope_tables_t(seq, lc):
    rows = seq // GRID_W
    r = jnp.broadcast_to(jnp.arange(rows, dtype=F32)[:, None], (rows, GRID_W)).reshape(-1)
    col = jnp.broadcast_to(jnp.arange(GRID_W, dtype=F32)[None, :], (rows, GRID_W)).reshape(-1)
    n_pairs = HEAD_DIM // 4
    inv = ROPE_THETA ** (-jnp.arange(n_pairs, dtype=F32) / n_pairs)
    ang = jnp.concatenate([r[:, None] * inv, col[:, None] * inv], axis=-1)
    cos = jnp.concatenate([jnp.ones((lc, 2 * n_pairs), F32), jnp.cos(ang)], axis=0)
    sin = jnp.concatenate([jnp.zeros((lc, 2 * n_pairs), F32), jnp.sin(ang)], axis=0)
    return cos.T, sin.T


def kernel(x, c, ctx, c_ctx, w_ada, b_ada, norm_g, w_mlp_in, w_mlp_out, ev_w_in, ev_w_lr, ev_b_lr, ev_gla_g,
           ev_qk_g, ev_w_out, od_w_in, od_conv_w, od_conv_b, od_f_w1, od_f_b1, od_f_w2, od_f_b2, od_f_w3,
           od_hy_bias, od_lam, od_diff_g, od_w_out):
    bsz, seq, d = x.shape
    lc = ctx.shape[1]
    depth = w_ada.shape[0]
    assert lc == TM and seq % TM == 0 and bsz <= 4
    cos_t, sin_t = _rope_tables_t(seq, lc)
    cc = jnp.zeros((8, d), F32).at[:bsz].set(c).at[4].set(c_ctx)
    mods = _mods(cc, w_ada, b_ada).reshape(depth, 8, N_MOD, d)
    xs = jnp.concatenate([ctx, x], axis=1)
    for i in range(depth):
        last = i == depth - 1
        j = i // 2
        g = norm_g[i]
        if i % 2 == 0:
            gla, la, aq, ak, av = _even_in(xs, mods[i], g[0:1], ev_w_in[j], ev_w_lr[j], ev_b_lr[j], ev_qk_g[j],
                                           cos_t, sin_t)
            ya = _gla_j(gla, la, ev_gla_g[j], lc)
            yb = _gqa_attention(aq, ak, av)
            w_out = ev_w_out[j]
        else:
            u, dq, dk, dv = _odd_in(xs, mods[i], g[0:1], od_w_in[j], cos_t, sin_t)
            ya = _hyena_j(u, od_conv_w[j], od_conv_b[j], od_f_w1[j], od_f_b1[j], od_f_w2[j], od_f_b2[j],
                          od_f_w3[j], od_hy_bias[j], lc)
            yb = _diff_attention(dq, dk, dv, od_lam[j], od_diff_g[j], i)
            w_out = od_w_out[j]
        xs = _post(xs, ya, yb, mods[i], g, w_out, w_mlp_in[i], w_mlp_out[i], skip_ctx=last)
    return xs
```

```python
import functools
import math

import jax
import jax.numpy as jnp
import numpy as np
from jax import lax
from jax.experimental import pallas as pl
from jax.experimental.pallas import tpu as pltpu

F32 = jnp.float32
BF16 = jnp.bfloat16

TM = 256
GRID_W = 64
HEAD_DIM = 64
ROPE_THETA = 10000.0
NORM_EPS = 1e-6
N_MOD = 6
VMEM_LIMIT = 56 * 1024 * 1024

GLA_HEADS = 4
GLA_DK = 64
GLA_DV = 128
GLA_RANK = 16
GLA_NORMALIZER = 16.0
GLA_CHUNK = 64
GLA_QK = GLA_HEADS * GLA_DK
GLA_V = GLA_HEADS * GLA_DV
GQA_HEADS = 8
GQA_KV_HEADS = 2
HY_CH = 512
HY_ORDER = 2
HY_BANDS = 16
HY_TARGET = 1e-2
HY_FAST_PCT = 0.3
HY_SLOW_PCT = 1.5
HY_COLS = (HY_ORDER + 1) * HY_CH
DIFF_HEADS = 4
DIFF_QK = DIFF_HEADS * 2 * HEAD_DIM
DIFF_V = DIFF_HEADS * 2 * HEAD_DIM
ONES_ROWS = 16
N_SCORE_HEADS = 8
LOG2E = 1.4426950408889634
Q_SCALE = HEAD_DIM ** -0.5 * LOG2E
GQA_VROWS = HEAD_DIM + ONES_ROWS
DIFF_VROWS = 2 * HEAD_DIM + ONES_ROWS


def _params(*sem):
    return pltpu.CompilerParams(dimension_semantics=sem, vmem_limit_bytes=VMEM_LIMIT)


def _const_spec(shape):
    nd = len(shape)
    return pl.BlockSpec(shape, lambda *_: (0,) * nd, pipeline_mode=pl.Buffered(1))


def _rms(x, g):
    return x * lax.rsqrt(jnp.mean(x * x, axis=-1, keepdims=True) + NORM_EPS) * g


def _mod_row(b, i):
    return jnp.where(i == 0, 4, b)


def _mods_kernel(cc_ref, w_ref, b_ref, o_ref):
    cc = cc_ref[...]
    s = cc / (1.0 + jnp.exp(-cc))
    o_ref[0] = jnp.dot(s.astype(BF16), w_ref[0].astype(BF16), preferred_element_type=F32) + b_ref[0]


def _mods(cc, w_ada, b_ada):
    depth, d, n = w_ada.shape
    nb = n // 4
    return pl.pallas_call(
        _mods_kernel,
        grid=(depth, n // nb),
        in_specs=[pl.BlockSpec((8, d), lambda l, j: (0, 0)),
                  pl.BlockSpec((1, d, nb), lambda l, j: (l, 0, j)),
                  pl.BlockSpec((1, 1, nb), lambda l, j: (l, 0, j))],
        out_specs=pl.BlockSpec((1, 8, nb), lambda l, j: (l, 0, j)),
        out_shape=jax.ShapeDtypeStruct((depth, 8, n), F32),
        compiler_params=_params("parallel", "parallel"),
        name="ada_mods",
    )(cc, w_ada, b_ada.reshape(depth, 1, n))


def _normed_input(xs_ref, mods_ref, g_ref):
    x = xs_ref[0]
    mods = mods_ref[0]
    h = _rms(x, g_ref[0:1]) * (1.0 + mods[1:2]) + mods[0:1]
    return h.astype(BF16)


def _rope_rows(y, cos, sin):
    x1, x2 = y[:32], y[32:]
    return x1 * cos - x2 * sin, x1 * sin + x2 * cos


def _log_sigmoid(x):
    return jnp.minimum(x, 0.0) - jnp.log(1.0 + jnp.exp(-jnp.abs(x)))


def _even_in_kernel(xs_ref, mods_ref, g_ref, wn_ref, wt_ref, wlr_ref, blr_ref, gq_ref, gk_ref,
                    cos_ref, sin_ref, gla_ref, la_ref, aq_ref, ak_ref, av_ref):
    h = _normed_input(xs_ref, mods_ref, g_ref)
    p = jnp.dot(h, wn_ref[...], preferred_element_type=F32)
    gla_ref[0] = p[:, :2 * GLA_QK + 2 * GLA_V]
    lr = p[:, 2 * GLA_QK + 2 * GLA_V:].astype(BF16)
    gk = jnp.dot(lr, wlr_ref[...], preferred_element_type=F32) + blr_ref[...]
    la_ref[0] = _log_sigmoid(gk) * (1.0 / GLA_NORMALIZER)

    pt = lax.dot_general(wt_ref[...], h, (((1,), (1,)), ((), ())), preferred_element_type=F32)
    cos, sin = cos_ref[...], sin_ref[...]
    gq, gk_gain = gq_ref[...], gk_ref[...]
    for hh in range(GQA_HEADS):
        blk = pt[hh * 64:(hh + 1) * 64]
        y = blk * lax.rsqrt(jnp.mean(blk * blk, axis=0, keepdims=True) + NORM_EPS) * gq
        o1, o2 = _rope_rows(y, cos, sin)
        aq_ref[0, 0, hh * 64:hh * 64 + 32, :] = (o1 * Q_SCALE).astype(BF16)
        aq_ref[0, 0, hh * 64 + 32:hh * 64 + 64, :] = (o2 * Q_SCALE).astype(BF16)
    kparts = []
    for hh in range(GQA_KV_HEADS):
        blk = pt[512 + hh * 64:512 + (hh + 1) * 64]
        y = blk * lax.rsqrt(jnp.mean(blk * blk, axis=0, keepdims=True) + NORM_EPS) * gk_gain
        o1, o2 = _rope_rows(y, cos, sin)
        kparts += [o1, o2]
    kt = jnp.concatenate(kparts, axis=0)
    ak_ref[0] = kt.T.astype(BF16)
    ones = jnp.ones((ONES_ROWS, TM), BF16)
    for hh in range(GQA_KV_HEADS):
        av_ref[0, 0, hh * GQA_VROWS:hh * GQA_VROWS + 64, :] = pt[640 + hh * 64:640 + (hh + 1) * 64].astype(BF16)
        av_ref[0, 0, hh * GQA_VROWS + 64:(hh + 1) * GQA_VROWS, :] = ones


def _odd_in_kernel(xs_ref, mods_ref, g_ref, wn_ref, wt_ref, cos_ref, sin_ref,
                   u_ref, dq_ref, dk_ref, dv_ref):
    h = _normed_input(xs_ref, mods_ref, g_ref)
    u_ref[0] = jnp.dot(h, wn_ref[...], preferred_element_type=F32)
    pt = lax.dot_general(wt_ref[...], h, (((1,), (1,)), ((), ())), preferred_element_type=F32)
    cos, sin = cos_ref[...], sin_ref[...]
    kparts = []
    for hh in range(2 * DIFF_HEADS):
        o1, o2 = _rope_rows(pt[hh * 64:(hh + 1) * 64], cos, sin)
        dq_ref[0, 0, hh * 64:hh * 64 + 32, :] = (o1 * Q_SCALE).astype(BF16)
        dq_ref[0, 0, hh * 64 + 32:hh * 64 + 64, :] = (o2 * Q_SCALE).astype(BF16)
        k1, k2 = _rope_rows(pt[DIFF_QK + hh * 64:DIFF_QK + (hh + 1) * 64], cos, sin)
        kparts += [k1, k2]
    kt = jnp.concatenate(kparts, axis=0)
    dk_ref[0] = kt.T.astype(BF16)
    ones = jnp.ones((ONES_ROWS, TM), BF16)
    nv = 2 * HEAD_DIM
    for hh in range(DIFF_HEADS):
        dv_ref[0, 0, hh * DIFF_VROWS:hh * DIFF_VROWS + nv, :] = pt[2 * DIFF_QK + hh * nv:2 * DIFF_QK + (hh + 1) * nv].astype(BF16)
        dv_ref[0, 0, hh * DIFF_VROWS + nv:(hh + 1) * DIFF_VROWS, :] = ones


def _pair_perm(n_heads):
    base = np.concatenate([np.arange(0, HEAD_DIM, 2), np.arange(1, HEAD_DIM, 2)])
    return np.concatenate([h * HEAD_DIM + base for h in range(n_heads)])


def _even_in(xs, mods_l, g0, w_in, w_lr, b_lr, qk_g, cos_t, sin_t):
    b, t, d = xs.shape
    nt = t // TM
    n_gla = 2 * GLA_QK + 2 * GLA_V
    c_lr = n_gla + 2 * GLA_RANK
    wn = jnp.concatenate([w_in[:, :c_lr], jnp.zeros((d, 128 - 2 * GLA_RANK), F32)], axis=1).astype(BF16)
    cq = c_lr + GQA_HEADS * HEAD_DIM
    ck = cq + GQA_KV_HEADS * HEAD_DIM
    wq = w_in[:, c_lr:cq][:, _pair_perm(GQA_HEADS)]
    wk = w_in[:, cq:ck][:, _pair_perm(GQA_KV_HEADS)]
    wt = jnp.concatenate([wq, wk, w_in[:, ck:]], axis=1).T.astype(BF16)
    wlr = jnp.zeros((128, 2 * GLA_QK), F32)
    wlr = wlr.at[:GLA_RANK, :GLA_QK].set(w_lr[0]).at[GLA_RANK:2 * GLA_RANK, GLA_QK:].set(w_lr[1]).astype(BF16)
    blr = b_lr.reshape(1, 2 * GLA_QK)
    perm = _pair_perm(1)
    gq = qk_g[0][perm].reshape(HEAD_DIM, 1)
    gk = qk_g[1][perm].reshape(HEAD_DIM, 1)
    tok = lambda n: pl.BlockSpec((1, TM, n), lambda bb, i: (bb, i, 0))
    tr = lambda n: pl.BlockSpec((1, 1, n, TM), lambda bb, i: (bb, i, 0, 0))
    return pl.pallas_call(
        _even_in_kernel,
        grid=(b, nt),
        in_specs=[tok(d),
                  pl.BlockSpec((1, N_MOD, d), lambda bb, i: (_mod_row(bb, i), 0, 0)),
                  _const_spec((1, d)),
                  _const_spec(wn.shape), _const_spec(wt.shape), _const_spec(wlr.shape), _const_spec(blr.shape),
                  _const_spec(gq.shape), _const_spec(gk.shape),
                  pl.BlockSpec((32, TM), lambda bb, i: (0, i)),
                  pl.BlockSpec((32, TM), lambda bb, i: (0, i))],
        out_specs=[tok(n_gla), tok(2 * GLA_QK), tr(512), tok(128), tr(GQA_KV_HEADS * GQA_VROWS)],
        out_shape=[jax.ShapeDtypeStruct((b, t, n_gla), F32),
                   jax.ShapeDtypeStruct((b, t, 2 * GLA_QK), F32),
                   jax.ShapeDtypeStruct((b, nt, 512, TM), BF16),
                   jax.ShapeDtypeStruct((b, t, 128), BF16),
                   jax.ShapeDtypeStruct((b, nt, GQA_KV_HEADS * GQA_VROWS, TM), BF16)],
        compiler_params=_params("parallel", "parallel"),
        name="even_in",
    )(xs, mods_l, g0, wn, wt, wlr, blr, gq, gk, cos_t, sin_t)


def _odd_in(xs, mods_l, g0, w_in, cos_t, sin_t):
    b, t, d = xs.shape
    nt = t // TM
    wn = w_in[:, :HY_COLS].astype(BF16)
    perm = _pair_perm(2 * DIFF_HEADS)
    wq = w_in[:, HY_COLS:HY_COLS + DIFF_QK][:, perm]
    wk = w_in[:, HY_COLS + DIFF_QK:HY_COLS + 2 * DIFF_QK][:, perm]
    wt = jnp.concatenate([wq, wk, w_in[:, HY_COLS + 2 * DIFF_QK:]], axis=1).T.astype(BF16)
    tok = lambda n: pl.BlockSpec((1, TM, n), lambda bb, i: (bb, i, 0))
    tr = lambda n: pl.BlockSpec((1, 1, n, TM), lambda bb, i: (bb, i, 0, 0))
    return pl.pallas_call(
        _odd_in_kernel,
        grid=(b, nt),
        in_specs=[tok(d),
                  pl.BlockSpec((1, N_MOD, d), lambda bb, i: (_mod_row(bb, i), 0, 0)),
                  _const_spec((1, d)),
                  _const_spec(wn.shape), _const_spec(wt.shape),
                  pl.BlockSpec((32, TM), lambda bb, i: (0, i)),
                  pl.BlockSpec((32, TM), lambda bb, i: (0, i))],
        out_specs=[tok(HY_COLS), tr(DIFF_QK), tok(DIFF_QK), tr(DIFF_HEADS * DIFF_VROWS)],
        out_shape=[jax.ShapeDtypeStruct((b, t, HY_COLS), F32),
                   jax.ShapeDtypeStruct((b, nt, DIFF_QK, TM), BF16),
                   jax.ShapeDtypeStruct((b, t, DIFF_QK), BF16),
                   jax.ShapeDtypeStruct((b, nt, DIFF_HEADS * DIFF_VROWS, TM), BF16)],
        compiler_params=_params("parallel", "parallel"),
        name="odd_in",
    )(xs, mods_l, g0, wn, wt, cos_t, sin_t)


def _attn_core(q_ref, k_ref, v_ref, qpad_ref, m_ref, acc_ref, s_refs, c_refs, *, n_kv, k_col, q_slot, v_row, v_rows):
    i = pl.program_id(1)
    zeros = jnp.zeros((HEAD_DIM, TM), BF16)
    for hh in range(N_SCORE_HEADS):
        qh = q_ref[0, 0, hh * HEAD_DIM:(hh + 1) * HEAD_DIM, :]
        qpad_ref[hh] = jnp.concatenate([qh, zeros] if q_slot(hh) == 0 else [zeros, qh], axis=0)
    m_ref[...] = jnp.full(m_ref.shape, -jnp.inf, F32)
    acc_ref[...] = jnp.zeros(acc_ref.shape, F32)

    def scores(j, hh, slot):
        kc = k_ref[0, pl.ds(pl.multiple_of(j * TM, TM), TM), k_col(hh):k_col(hh) + 128]
        s = jnp.dot(kc, qpad_ref[hh], preferred_element_type=F32)
        s_refs[slot][hh] = s
        c_refs[slot][hh] = jnp.max(s, axis=0, keepdims=True)

    def consume(j, hh, slot):
        vc = v_ref[0, j, v_row(hh):v_row(hh) + v_rows, :]
        m_old = m_ref[hh]
        m_new = jnp.maximum(m_old, c_refs[slot][hh])
        p = jnp.exp2(s_refs[slot][hh] - m_new).astype(BF16)
        alpha = jnp.exp2(m_old - m_new)
        acc_ref[hh] = acc_ref[hh] * alpha + jnp.dot(vc, p, preferred_element_type=F32)
        m_ref[hh] = m_new

    for hh in range(N_SCORE_HEADS):
        scores(0, hh, 0)

    def body(jj, carry):
        j = 2 * jj
        for hh in range(N_SCORE_HEADS):
            consume(j, hh, 0)
            scores(j + 1, hh, 1)
        for hh in range(N_SCORE_HEADS):
            consume(j + 1, hh, 1)
            scores(j + 2, hh, 0)
        return carry

    n = jnp.where(i == 0, 1, n_kv)
    lax.fori_loop(0, (n - 1) // 2, body, 0)
    for hh in range(N_SCORE_HEADS):
        consume(n - 1, hh, 0)


def _attn_scratch(v_rows):
    return [pltpu.VMEM((N_SCORE_HEADS, 128, TM), BF16),
            pltpu.VMEM((N_SCORE_HEADS, 1, TM), F32),
            pltpu.VMEM((N_SCORE_HEADS, v_rows, TM), F32),
            pltpu.VMEM((N_SCORE_HEADS, TM, TM), F32), pltpu.VMEM((N_SCORE_HEADS, TM, TM), F32),
            pltpu.VMEM((N_SCORE_HEADS, 1, TM), F32), pltpu.VMEM((N_SCORE_HEADS, 1, TM), F32)]


def _gqa_kernel(q_ref, k_ref, v_ref, o_ref, qpad_ref, m_ref, acc_ref, s0_ref, s1_ref, c0_ref, c1_ref, *, n_kv):
    group = GQA_HEADS // GQA_KV_HEADS
    _attn_core(q_ref, k_ref, v_ref, qpad_ref, m_ref, acc_ref, (s0_ref, s1_ref), (c0_ref, c1_ref), n_kv=n_kv,
               k_col=lambda h: 0, q_slot=lambda h: h // group, v_row=lambda h: (h // group) * GQA_VROWS,
               v_rows=GQA_VROWS)
    for hh in range(GQA_HEADS):
        a = acc_ref[hh]
        o_ref[0, 0, hh * HEAD_DIM:(hh + 1) * HEAD_DIM, :] = (a[:HEAD_DIM] / a[HEAD_DIM:HEAD_DIM + 1]).astype(o_ref.dtype)


def _gqa_attention(aq, ak, av):
    b, nt, nq, _ = aq.shape
    t = nt * TM
    assert nt % 2 == 1
    return pl.pallas_call(
        functools.partial(_gqa_kernel, n_kv=nt),
        grid=(b, nt),
        in_specs=[pl.BlockSpec((1, 1, nq, TM), lambda bb, i: (bb, i, 0, 0)),
                  pl.BlockSpec((1, t, 128), lambda bb, i: (bb, 0, 0)),
                  pl.BlockSpec((1, nt, GQA_KV_HEADS * GQA_VROWS, TM), lambda bb, i: (bb, 0, 0, 0))],
        out_specs=pl.BlockSpec((1, 1, nq, TM), lambda bb, i: (bb, i, 0, 0)),
        out_shape=jax.ShapeDtypeStruct((b, nt, nq, TM), BF16),
        scratch_shapes=_attn_scratch(GQA_VROWS),
        compiler_params=_params("parallel", "arbitrary"),
        name="gqa_attn",
    )(aq, ak, av)


def _diff_kernel(q_ref, k_ref, v_ref, lam_ref, gain_ref, o_ref, qpad_ref, m_ref, acc_ref, s0_ref, s1_ref,
                 c0_ref, c1_ref, *, n_kv, lam_init):
    _attn_core(q_ref, k_ref, v_ref, qpad_ref, m_ref, acc_ref, (s0_ref, s1_ref), (c0_ref, c1_ref), n_kv=n_kv,
               k_col=lambda hc: (hc // 2) * 128, q_slot=lambda hc: hc % 2, v_row=lambda hc: (hc // 2) * DIFF_VROWS,
               v_rows=DIFF_VROWS)
    lp = lam_ref[...]
    lam = (jnp.exp(jnp.sum(lp[0:1] * lp[1:2], axis=1, keepdims=True))
           - jnp.exp(jnp.sum(lp[2:3] * lp[3:4], axis=1, keepdims=True)) + lam_init)
    gain = gain_ref[...]
    nv = 2 * HEAD_DIM
    for hh in range(DIFF_HEADS):
        a0, a1 = acc_ref[2 * hh], acc_ref[2 * hh + 1]
        o = a0[:nv] / a0[nv:nv + 1] - lam * (a1[:nv] / a1[nv:nv + 1])
        y = o * lax.rsqrt(jnp.mean(o * o, axis=0, keepdims=True) + NORM_EPS) * gain
        o_ref[0, 0, hh * nv:(hh + 1) * nv, :] = (y * (1.0 - lam_init)).astype(o_ref.dtype)


def _diff_attention(dq, dk, dv, lam_p, gain, layer_idx):
    b, nt, nq, _ = dq.shape
    t = nt * TM
    assert nt % 2 == 1
    lam_init = 0.8 - 0.6 * math.exp(-0.3 * layer_idx)
    return pl.pallas_call(
        functools.partial(_diff_kernel, n_kv=nt, lam_init=lam_init),
        grid=(b, nt),
        in_specs=[pl.BlockSpec((1, 1, nq, TM), lambda bb, i: (bb, i, 0, 0)),
                  pl.BlockSpec((1, t, DIFF_QK), lambda bb, i: (bb, 0, 0)),
                  pl.BlockSpec((1, nt, DIFF_HEADS * DIFF_VROWS, TM), lambda bb, i: (bb, 0, 0, 0)),
                  _const_spec(lam_p.shape),
                  _const_spec((2 * HEAD_DIM, 1))],
        out_specs=pl.BlockSpec((1, 1, DIFF_V, TM), lambda bb, i: (bb, i, 0, 0)),
        out_shape=jax.ShapeDtypeStruct((b, nt, DIFF_V, TM), BF16),
        scratch_shapes=_attn_scratch(DIFF_VROWS),
        compiler_params=_params("parallel", "arbitrary"),
        name="diff_attn",
    )(dq, dk, dv, lam_p, gain.reshape(2 * HEAD_DIM, 1))


def _post_kernel(xs_ref, ya_ref, yb_ref, mods_ref, g_ref, woa_ref, wob_ref, w1_ref, w2_ref, o_ref, *, ff_chunk):
    x = xs_ref[0]
    mods = mods_ref[0]
    g = g_ref[...]
    ya = ya_ref[0].astype(BF16)
    yb = yb_ref[0, 0].astype(F32).T.astype(BF16)
    y = (jnp.dot(ya, woa_ref[...], preferred_element_type=F32)
         + jnp.dot(yb, wob_ref[...], preferred_element_type=F32))
    x1 = x + mods[2:3] * _rms(y, g[1:2])
    h = (_rms(x1, g[2:3]) * (1.0 + mods[4:5]) + mods[3:4]).astype(BF16)
    d_ff = w1_ref.shape[1]
    acc = jnp.zeros(x.shape, F32)
    for c in range(d_ff // ff_chunk):
        u = jnp.dot(h, w1_ref[:, c * ff_chunk:(c + 1) * ff_chunk], preferred_element_type=F32)
        u = jnp.square(jnp.maximum(u, 0.0)).astype(BF16)
        acc = acc + jnp.dot(u, w2_ref[c * ff_chunk:(c + 1) * ff_chunk, :], preferred_element_type=F32)
    o_ref[0] = x1 + mods[5:6] * _rms(acc, g[3:4])


def _post(xs, ya, yb_t, mods_l, g, w_out, w1, w2, skip_ctx):
    b, t, d = xs.shape
    nt = t // TM
    half = w_out.shape[0] // 2
    woa = w_out[:half].astype(BF16)
    wob = w_out[half:].astype(BF16)
    w1 = w1.astype(BF16)
    w2 = w2.astype(BF16)
    off = 1 if skip_ctx else 0
    return pl.pallas_call(
        functools.partial(_post_kernel, ff_chunk=1024),
        grid=(b, nt - off),
        in_specs=[pl.BlockSpec((1, TM, d), lambda bb, i: (bb, i + off, 0)),
                  pl.BlockSpec((1, TM, half), lambda bb, i: (bb, i + off, 0)),
                  pl.BlockSpec((1, 1, half, TM), lambda bb, i: (bb, i + off, 0, 0)),
                  pl.BlockSpec((1, N_MOD, d), lambda bb, i: (_mod_row(bb, i + off), 0, 0)),
                  _const_spec(g.shape),
                  _const_spec(woa.shape), _const_spec(wob.shape), _const_spec(w1.shape), _const_spec(w2.shape)],
        out_specs=pl.BlockSpec((1, TM, d), lambda bb, i: (bb, i, 0)),
        out_shape=jax.ShapeDtypeStruct((b, t - off * TM, d), F32),
        compiler_params=_params("parallel", "parallel"),
        name="post",
    )(xs, ya, yb_t, mods_l, g, woa, wob, w1, w2)


def _rmsnorm_j(x, g):
    xf = x.astype(F32)
    return xf * lax.rsqrt(jnp.mean(xf * xf, axis=-1, keepdims=True) + NORM_EPS) * g


def _gla_chunked_j(q, k, v, la, s0):
    B, L, H, dk = q.shape
    dv = v.shape[-1]
    n = L // GLA_CHUNK

    def blocks(t):
        return t.reshape(B, n, GLA_CHUNK, H, t.shape[-1]).transpose(1, 0, 3, 2, 4)

    q, k, v, la = blocks(q), blocks(k), blocks(v), blocks(la)
    bb = jnp.cumsum(la, axis=3)
    b_end = bb[:, :, :, -1:]
    q_in = q * jnp.exp(bb)
    k_in = k * jnp.exp(-bb)
    k_end = k * jnp.exp(b_end - bb)
    mask = jnp.tril(jnp.ones((GLA_CHUNK, GLA_CHUNK), dtype=bool))
    a = jnp.where(mask, jnp.einsum('nbhid,nbhjd->nbhij', q_in, k_in), 0.0)
    o_intra = jnp.einsum('nbhij,nbhjv->nbhiv', a, v)

    def step(s, inp):
        q_c, k_c, v_c, dec = inp
        o_c = jnp.einsum('bhid,bhdv->bhiv', q_c, s)
        s = s * dec[..., None] + jnp.einsum('bhjd,bhjv->bhdv', k_c, v_c)
        return s, o_c

    s_fin, o_inter = lax.scan(step, s0, (q_in, k_end, v, jnp.exp(b_end[:, :, :, 0])))
    o = (o_intra + o_inter).transpose(1, 0, 3, 2, 4).reshape(B, L, H, dv)
    return o, s_fin


def _gla_j(gla, la, gain, lc):
    B, T, _ = gla.shape
    q = (gla[..., :GLA_QK] * GLA_DK ** -0.5).reshape(B, T, GLA_HEADS, GLA_DK)
    k = gla[..., GLA_QK:2 * GLA_QK].reshape(B, T, GLA_HEADS, GLA_DK)
    v = gla[..., 2 * GLA_QK:2 * GLA_QK + GLA_V].reshape(B, T, GLA_HEADS, GLA_DV)
    g = gla[..., 2 * GLA_QK + GLA_V:].reshape(B, T, GLA_HEADS, GLA_DV)
    la = la.reshape(B, T, 2, GLA_HEADS, GLA_DK)
    s0 = jnp.zeros((B, GLA_HEADS, GLA_DK, GLA_DV), F32)
    o_f, _ = _gla_chunked_j(q, k, v, la[:, :, 0], s0)
    rev = lambda t_: t_[:, ::-1]
    oc_b, s_ctx = _gla_chunked_j(rev(q[:, :lc]), rev(k[:, :lc]), rev(v[:, :lc]), rev(la[:, :lc, 1]), s0)
    ox_b, _ = _gla_chunked_j(rev(q[:, lc:]), rev(k[:, lc:]), rev(v[:, lc:]), rev(la[:, lc:, 1]), s_ctx)
    o = o_f + jnp.concatenate([rev(oc_b), rev(ox_b)], axis=1)
    return (_rmsnorm_j(o, gain) * jax.nn.silu(g)).reshape(B, T, GLA_V)


def _hyena_filters_j(L, w1, b1, w2, b2, w3):
    pos = jnp.arange(L, dtype=F32)
    t = pos / max(L - 1, 1)
    w = 2 * math.pi * pos / L
    bands = jnp.linspace(1e-4, HY_BANDS - 1, HY_BANDS, dtype=F32)
    z = jnp.concatenate([t[:, None], jnp.cos(w[:, None] * bands), -jnp.sin(w[:, None] * bands)], axis=-1)
    hid = jnp.sin(z @ w1 + b1)
    hid = jnp.sin(hid @ w2 + b2)
    h = (hid @ w3).reshape(L, HY_ORDER, 2, HY_CH)
    deltas = jnp.abs(jnp.linspace(math.log(HY_TARGET) / HY_FAST_PCT, math.log(HY_TARGET) / HY_SLOW_PCT,
                                  HY_CH, dtype=F32))
    return h * jnp.exp(-t[:, None] * deltas)[:, None, None, :]


def _bidir_fft_conv_j(u, h_fwd, h_bwd):
    L, C = h_fwd.shape
    k = jnp.concatenate([h_fwd, jnp.zeros((1, C), h_fwd.dtype), h_bwd[1:][::-1]], axis=0)
    uf = jnp.fft.rfft(u, n=2 * L, axis=1)
    kf = jnp.fft.rfft(k, n=2 * L, axis=0)
    return jnp.fft.irfft(uf * kf[None], n=2 * L, axis=1)[:, :L]


def _hyena_j(u, conv_w, conv_b, f_w1, f_b1, f_w2, f_b2, f_w3, hy_bias, lc):
    def run(u):
        L = u.shape[1]
        up = jnp.pad(u, ((0, 0), (1, 1), (0, 0)))
        s = up[:, :-2] * conv_w[0] + up[:, 1:-1] * conv_w[1] + up[:, 2:] * conv_w[2] + conv_b
        v, x1, x2 = jnp.split(s, HY_ORDER + 1, axis=-1)
        h = _hyena_filters_j(L, f_w1, f_b1, f_w2, f_b2, f_w3)
        z = v
        for o, gate in enumerate((x1, x2)):
            z = gate * (_bidir_fft_conv_j(z, h[:, o, 0], h[:, o, 1]) + z * hy_bias[o])
        return z

    return jnp.concatenate([run(u[:, :lc]), run(u[:, lc:])], axis=1)


def _rope_tables_t(seq, lc):
    rows = seq // GRID_W
    r = jnp.broadcast_to(jnp.arange(rows, dtype=F32)[:, None], (rows, GRID_W)).reshape(-1)
    col = jnp.broadcast_to(jnp.arange(GRID_W, dtype=F32)[None, :], (rows, GRID_W)).reshape(-1)
    n_pairs = HEAD_DIM // 4
    inv = ROPE_THETA ** (-jnp.arange(n_pairs, dtype=F32) / n_pairs)
    ang = jnp.concatenate([r[:, None] * inv, col[:, None] * inv], axis=-1)
    cos = jnp.concatenate([jnp.ones((lc, 2 * n_pairs), F32), jnp.cos(ang)], axis=0)
    sin = jnp.concatenate([jnp.zeros((lc, 2 * n_pairs), F32), jnp.sin(ang)], axis=0)
    return cos.T, sin.T


def kernel(x, c, ctx, c_ctx, w_ada, b_ada, norm_g, w_mlp_in, w_mlp_out, ev_w_in, ev_w_lr, ev_b_lr, ev_gla_g,
           ev_qk_g, ev_w_out, od_w_in, od_conv_w, od_conv_b, od_f_w1, od_f_b1, od_f_w2, od_f_b2, od_f_w3,
           od_hy_bias, od_lam, od_diff_g, od_w_out):
    bsz, seq, d = x.shape
    lc = ctx.shape[1]
    depth = w_ada.shape[0]
    assert lc == TM and seq % TM == 0 and bsz <= 4
    cos_t, sin_t = _rope_tables_t(seq, lc)
    cc = jnp.zeros((8, d), F32).at[:bsz].set(c).at[4].set(c_ctx)
    mods = _mods(cc, w_ada, b_ada).reshape(depth, 8, N_MOD, d)
    xs = jnp.concatenate([ctx, x], axis=1)
    for i in range(depth):
        last = i == depth - 1
        j = i // 2
        g = norm_g[i]
        if i % 2 == 0:
            gla, la, aq, ak, av = _even_in(xs, mods[i], g[0:1], ev_w_in[j], ev_w_lr[j], ev_b_lr[j], ev_qk_g[j],
                                           cos_t, sin_t)
            ya = _gla_j(gla, la, ev_gla_g[j], lc)
            yb = _gqa_attention(aq, ak, av)
            w_out = ev_w_out[j]
        else:
            u, dq, dk, dv = _odd_in(xs, mods[i], g[0:1], od_w_in[j], cos_t, sin_t)
            ya = _hyena_j(u, od_conv_w[j], od_conv_b[j], od_f_w1[j], od_f_b1[j], od_f_w2[j], od_f_b2[j],
                          od_f_w3[j], od_hy_bias[j], lc)
            yb = _diff_attention(dq, dk, dv, od_lam[j], od_diff_g[j], i)
            w_out = od_w_out[j]
        xs = _post(xs, ya, yb, mods[i], g, w_out, w_mlp_in[i], w_mlp_out[i], skip_ctx=last)
    return xs
```

```python
import functools
import math

import jax
import jax.numpy as jnp
import numpy as np
from jax import lax
from jax.experimental import pallas as pl
from jax.experimental.pallas import tpu as pltpu

F32 = jnp.float32
BF16 = jnp.bfloat16

TM = 256
GRID_W = 64
HEAD_DIM = 64
ROPE_THETA = 10000.0
NORM_EPS = 1e-6
N_MOD = 6
VMEM_LIMIT = 56 * 1024 * 1024

GLA_HEADS = 4
GLA_DK = 64
GLA_DV = 128
GLA_RANK = 16
GLA_NORMALIZER = 16.0
GLA_CHUNK = 64
GLA_QK = GLA_HEADS * GLA_DK
GLA_V = GLA_HEADS * GLA_DV
GQA_HEADS = 8
GQA_KV_HEADS = 2
HY_CH = 512
HY_ORDER = 2
HY_BANDS = 16
HY_TARGET = 1e-2
HY_FAST_PCT = 0.3
HY_SLOW_PCT = 1.5
HY_COLS = (HY_ORDER + 1) * HY_CH
DIFF_HEADS = 4
DIFF_QK = DIFF_HEADS * 2 * HEAD_DIM
DIFF_V = DIFF_HEADS * 2 * HEAD_DIM
ONES_ROWS = 16
N_SCORE_HEADS = 8
LOG2E = 1.4426950408889634
Q_SCALE = HEAD_DIM ** -0.5 * LOG2E
GQA_VROWS = HEAD_DIM + ONES_ROWS
DIFF_VROWS = 2 * HEAD_DIM + ONES_ROWS


def _params(*sem):
    return pltpu.CompilerParams(dimension_semantics=sem, vmem_limit_bytes=VMEM_LIMIT)


def _const_spec(shape):
    nd = len(shape)
    return pl.BlockSpec(shape, lambda *_: (0,) * nd, pipeline_mode=pl.Buffered(1))


def _rms(x, g):
    return x * lax.rsqrt(jnp.mean(x * x, axis=-1, keepdims=True) + NORM_EPS) * g


def _mod_row(b, i):
    return jnp.where(i == 0, 4, b)


def _mods_kernel(cc_ref, w_ref, b_ref, o_ref):
    cc = cc_ref[...]
    s = cc / (1.0 + jnp.exp(-cc))
    o_ref[0] = jnp.dot(s.astype(BF16), w_ref[0].astype(BF16), preferred_element_type=F32) + b_ref[0]


def _mods(cc, w_ada, b_ada):
    depth, d, n = w_ada.shape
    nb = n // 4
    return pl.pallas_call(
        _mods_kernel,
        grid=(depth, n // nb),
        in_specs=[pl.BlockSpec((8, d), lambda l, j: (0, 0)),
                  pl.BlockSpec((1, d, nb), lambda l, j: (l, 0, j)),
                  pl.BlockSpec((1, 1, nb), lambda l, j: (l, 0, j))],
        out_specs=pl.BlockSpec((1, 8, nb), lambda l, j: (l, 0, j)),
        out_shape=jax.ShapeDtypeStruct((depth, 8, n), F32),
        compiler_params=_params("parallel", "parallel"),
        name="ada_mods",
    )(cc, w_ada, b_ada.reshape(depth, 1, n))


def _normed_input(xs_ref, mods_ref, g_ref):
    x = xs_ref[0]
    mods = mods_ref[0]
    h = _rms(x, g_ref[0:1]) * (1.0 + mods[1:2]) + mods[0:1]
    return h.astype(BF16)


def _rope_rows(y, cos, sin):
    x1, x2 = y[:32], y[32:]
    return x1 * cos - x2 * sin, x1 * sin + x2 * cos


def _log_sigmoid(x):
    return jnp.minimum(x, 0.0) - jnp.log(1.0 + jnp.exp(-jnp.abs(x)))


def _even_in_kernel(xs_ref, mods_ref, g_ref, wn_ref, wt_ref, wlr_ref, blr_ref, gq_ref, gk_ref,
                    cos_ref, sin_ref, qk_ref, la_ref, gv_ref, gg_ref, aq_ref, ak_ref, av_ref):
    h = _normed_input(xs_ref, mods_ref, g_ref)
    p = jnp.dot(h, wn_ref[...], preferred_element_type=F32)
    qk_ref[0, :, :GLA_QK] = p[:, :GLA_QK] * GLA_DK ** -0.5
    qk_ref[0, :, GLA_QK:] = p[:, GLA_QK:2 * GLA_QK]
    lr = p[:, 2 * GLA_QK:].astype(BF16)
    gk = jnp.dot(lr, wlr_ref[...], preferred_element_type=F32) + blr_ref[...]
    la_ref[0] = _log_sigmoid(gk) * (1.0 / GLA_NORMALIZER)

    pt_all = lax.dot_general(wt_ref[...], h, (((1,), (1,)), ((), ())), preferred_element_type=F32)
    gv_ref[0, 0] = pt_all[:GLA_V].astype(BF16)
    gg_ref[0, 0] = pt_all[GLA_V:2 * GLA_V]
    pt = pt_all[2 * GLA_V:]
    cos, sin = cos_ref[...], sin_ref[...]
    gq, gk_gain = gq_ref[...], gk_ref[...]
    for hh in range(GQA_HEADS):
        blk = pt[hh * 64:(hh + 1) * 64]
        y = blk * lax.rsqrt(jnp.mean(blk * blk, axis=0, keepdims=True) + NORM_EPS) * gq
        o1, o2 = _rope_rows(y, cos, sin)
        aq_ref[0, 0, hh * 64:hh * 64 + 32, :] = (o1 * Q_SCALE).astype(BF16)
        aq_ref[0, 0, hh * 64 + 32:hh * 64 + 64, :] = (o2 * Q_SCALE).astype(BF16)
    kparts = []
    for hh in range(GQA_KV_HEADS):
        blk = pt[512 + hh * 64:512 + (hh + 1) * 64]
        y = blk * lax.rsqrt(jnp.mean(blk * blk, axis=0, keepdims=True) + NORM_EPS) * gk_gain
        o1, o2 = _rope_rows(y, cos, sin)
        kparts += [o1, o2]
    kt = jnp.concatenate(kparts, axis=0)
    ak_ref[0] = kt.T.astype(BF16)
    ones = jnp.ones((ONES_ROWS, TM), BF16)
    for hh in range(GQA_KV_HEADS):
        av_ref[0, 0, hh * GQA_VROWS:hh * GQA_VROWS + 64, :] = pt[640 + hh * 64:640 + (hh + 1) * 64].astype(BF16)
        av_ref[0, 0, hh * GQA_VROWS + 64:(hh + 1) * GQA_VROWS, :] = ones


def _odd_in_kernel(xs_ref, mods_ref, g_ref, wn_ref, wt_ref, cos_ref, sin_ref,
                   u_ref, dq_ref, dk_ref, dv_ref):
    h = _normed_input(xs_ref, mods_ref, g_ref)
    u_ref[0] = jnp.dot(h, wn_ref[...], preferred_element_type=F32)
    pt = lax.dot_general(wt_ref[...], h, (((1,), (1,)), ((), ())), preferred_element_type=F32)
    cos, sin = cos_ref[...], sin_ref[...]
    kparts = []
    for hh in range(2 * DIFF_HEADS):
        o1, o2 = _rope_rows(pt[hh * 64:(hh + 1) * 64], cos, sin)
        dq_ref[0, 0, hh * 64:hh * 64 + 32, :] = (o1 * Q_SCALE).astype(BF16)
        dq_ref[0, 0, hh * 64 + 32:hh * 64 + 64, :] = (o2 * Q_SCALE).astype(BF16)
        k1, k2 = _rope_rows(pt[DIFF_QK + hh * 64:DIFF_QK + (hh + 1) * 64], cos, sin)
        kparts += [k1, k2]
    kt = jnp.concatenate(kparts, axis=0)
    dk_ref[0] = kt.T.astype(BF16)
    ones = jnp.ones((ONES_ROWS, TM), BF16)
    nv = 2 * HEAD_DIM
    for hh in range(DIFF_HEADS):
        dv_ref[0, 0, hh * DIFF_VROWS:hh * DIFF_VROWS + nv, :] = pt[2 * DIFF_QK + hh * nv:2 * DIFF_QK + (hh + 1) * nv].astype(BF16)
        dv_ref[0, 0, hh * DIFF_VROWS + nv:(hh + 1) * DIFF_VROWS, :] = ones


def _pair_perm(n_heads):
    base = np.concatenate([np.arange(0, HEAD_DIM, 2), np.arange(1, HEAD_DIM, 2)])
    return np.concatenate([h * HEAD_DIM + base for h in range(n_heads)])


def _even_in(xs, mods_l, g0, w_in, w_lr, b_lr, qk_g, cos_t, sin_t):
    b, t, d = xs.shape
    nt = t // TM
    c_v = 2 * GLA_QK
    c_g = c_v + GLA_V
    c_lr0 = c_g + GLA_V
    c_lr = c_lr0 + 2 * GLA_RANK
    wn = jnp.concatenate([w_in[:, :c_v], w_in[:, c_lr0:c_lr], jnp.zeros((d, 128 - 2 * GLA_RANK), F32)],
                         axis=1).astype(BF16)
    cq = c_lr + GQA_HEADS * HEAD_DIM
    ck = cq + GQA_KV_HEADS * HEAD_DIM
    wq = w_in[:, c_lr:cq][:, _pair_perm(GQA_HEADS)]
    wk = w_in[:, cq:ck][:, _pair_perm(GQA_KV_HEADS)]
    wt = jnp.concatenate([w_in[:, c_v:c_lr0], wq, wk, w_in[:, ck:]], axis=1).T.astype(BF16)
    wlr = jnp.zeros((128, 2 * GLA_QK), F32)
    wlr = wlr.at[:GLA_RANK, :GLA_QK].set(w_lr[0]).at[GLA_RANK:2 * GLA_RANK, GLA_QK:].set(w_lr[1]).astype(BF16)
    blr = b_lr.reshape(1, 2 * GLA_QK)
    perm = _pair_perm(1)
    gq = qk_g[0][perm].reshape(HEAD_DIM, 1)
    gk = qk_g[1][perm].reshape(HEAD_DIM, 1)
    tok = lambda n: pl.BlockSpec((1, TM, n), lambda bb, i: (bb, i, 0))
    tr = lambda n: pl.BlockSpec((1, 1, n, TM), lambda bb, i: (bb, i, 0, 0))
    return pl.pallas_call(
        _even_in_kernel,
        grid=(b, nt),
        in_specs=[tok(d),
                  pl.BlockSpec((1, N_MOD, d), lambda bb, i: (_mod_row(bb, i), 0, 0)),
                  _const_spec((1, d)),
                  _const_spec(wn.shape), _const_spec(wt.shape), _const_spec(wlr.shape), _const_spec(blr.shape),
                  _const_spec(gq.shape), _const_spec(gk.shape),
                  pl.BlockSpec((32, TM), lambda bb, i: (0, i)),
                  pl.BlockSpec((32, TM), lambda bb, i: (0, i))],
        out_specs=[tok(2 * GLA_QK), tok(2 * GLA_QK), tr(GLA_V), tr(GLA_V), tr(512), tok(128),
                   tr(GQA_KV_HEADS * GQA_VROWS)],
        out_shape=[jax.ShapeDtypeStruct((b, t, 2 * GLA_QK), F32),
                   jax.ShapeDtypeStruct((b, t, 2 * GLA_QK), F32),
                   jax.ShapeDtypeStruct((b, nt, GLA_V, TM), BF16),
                   jax.ShapeDtypeStruct((b, nt, GLA_V, TM), F32),
                   jax.ShapeDtypeStruct((b, nt, 512, TM), BF16),
                   jax.ShapeDtypeStruct((b, t, 128), BF16),
                   jax.ShapeDtypeStruct((b, nt, GQA_KV_HEADS * GQA_VROWS, TM), BF16)],
        compiler_params=_params("parallel", "parallel"),
        name="even_in",
    )(xs, mods_l, g0, wn, wt, wlr, blr, gq, gk, cos_t, sin_t)


def _odd_in(xs, mods_l, g0, w_in, cos_t, sin_t):
    b, t, d = xs.shape
    nt = t // TM
    wn = w_in[:, :HY_COLS].astype(BF16)
    perm = _pair_perm(2 * DIFF_HEADS)
    wq = w_in[:, HY_COLS:HY_COLS + DIFF_QK][:, perm]
    wk = w_in[:, HY_COLS + DIFF_QK:HY_COLS + 2 * DIFF_QK][:, perm]
    wt = jnp.concatenate([wq, wk, w_in[:, HY_COLS + 2 * DIFF_QK:]], axis=1).T.astype(BF16)
    tok = lambda n: pl.BlockSpec((1, TM, n), lambda bb, i: (bb, i, 0))
    tr = lambda n: pl.BlockSpec((1, 1, n, TM), lambda bb, i: (bb, i, 0, 0))
    return pl.pallas_call(
        _odd_in_kernel,
        grid=(b, nt),
        in_specs=[tok(d),
                  pl.BlockSpec((1, N_MOD, d), lambda bb, i: (_mod_row(bb, i), 0, 0)),
                  _const_spec((1, d)),
                  _const_spec(wn.shape), _const_spec(wt.shape),
                  pl.BlockSpec((32, TM), lambda bb, i: (0, i)),
                  pl.BlockSpec((32, TM), lambda bb, i: (0, i))],
        out_specs=[tok(HY_COLS), tr(DIFF_QK), tok(DIFF_QK), tr(DIFF_HEADS * DIFF_VROWS)],
        out_shape=[jax.ShapeDtypeStruct((b, t, HY_COLS), F32),
                   jax.ShapeDtypeStruct((b, nt, DIFF_QK, TM), BF16),
                   jax.ShapeDtypeStruct((b, t, DIFF_QK), BF16),
                   jax.ShapeDtypeStruct((b, nt, DIFF_HEADS * DIFF_VROWS, TM), BF16)],
        compiler_params=_params("parallel", "parallel"),
        name="odd_in",
    )(xs, mods_l, g0, wn, wt, cos_t, sin_t)


def _attn_core(q_ref, k_ref, v_ref, qpad_ref, m_ref, acc_ref, s_refs, c_refs, *, n_kv, k_col, q_slot, v_row, v_rows):
    i = pl.program_id(1)
    zeros = jnp.zeros((HEAD_DIM, TM), BF16)
    for hh in range(N_SCORE_HEADS):
        qh = q_ref[0, 0, hh * HEAD_DIM:(hh + 1) * HEAD_DIM, :]
        qpad_ref[hh] = jnp.concatenate([qh, zeros] if q_slot(hh) == 0 else [zeros, qh], axis=0)
    m_ref[...] = jnp.full(m_ref.shape, -jnp.inf, F32)
    acc_ref[...] = jnp.zeros(acc_ref.shape, F32)

    def scores(j, hh, slot):
        kc = k_ref[0, pl.ds(pl.multiple_of(j * TM, TM), TM), k_col(hh):k_col(hh) + 128]
        s = jnp.dot(kc, qpad_ref[hh], preferred_element_type=F32)
        s_refs[slot][hh] = s
        c_refs[slot][hh] = jnp.max(s, axis=0, keepdims=True)

    def consume(j, hh, slot):
        vc = v_ref[0, j, v_row(hh):v_row(hh) + v_rows, :]
        m_old = m_ref[hh]
        m_new = jnp.maximum(m_old, c_refs[slot][hh])
        p = jnp.exp2(s_refs[slot][hh] - m_new).astype(BF16)
        alpha = jnp.exp2(m_old - m_new)
        acc_ref[hh] = acc_ref[hh] * alpha + jnp.dot(vc, p, preferred_element_type=F32)
        m_ref[hh] = m_new

    for hh in range(N_SCORE_HEADS):
        scores(0, hh, 0)

    def body(jj, carry):
        j = 2 * jj
        for hh in range(N_SCORE_HEADS):
            consume(j, hh, 0)
            scores(j + 1, hh, 1)
        for hh in range(N_SCORE_HEADS):
            consume(j + 1, hh, 1)
            scores(j + 2, hh, 0)
        return carry

    n = jnp.where(i == 0, 1, n_kv)
    lax.fori_loop(0, (n - 1) // 2, body, 0)
    for hh in range(N_SCORE_HEADS):
        consume(n - 1, hh, 0)


def _attn_scratch(v_rows):
    return [pltpu.VMEM((N_SCORE_HEADS, 128, TM), BF16),
            pltpu.VMEM((N_SCORE_HEADS, 1, TM), F32),
            pltpu.VMEM((N_SCORE_HEADS, v_rows, TM), F32),
            pltpu.VMEM((N_SCORE_HEADS, TM, TM), F32), pltpu.VMEM((N_SCORE_HEADS, TM, TM), F32),
            pltpu.VMEM((N_SCORE_HEADS, 1, TM), F32), pltpu.VMEM((N_SCORE_HEADS, 1, TM), F32)]


def _gqa_kernel(q_ref, k_ref, v_ref, o_ref, qpad_ref, m_ref, acc_ref, s0_ref, s1_ref, c0_ref, c1_ref, *, n_kv):
    group = GQA_HEADS // GQA_KV_HEADS
    _attn_core(q_ref, k_ref, v_ref, qpad_ref, m_ref, acc_ref, (s0_ref, s1_ref), (c0_ref, c1_ref), n_kv=n_kv,
               k_col=lambda h: 0, q_slot=lambda h: h // group, v_row=lambda h: (h // group) * GQA_VROWS,
               v_rows=GQA_VROWS)
    for hh in range(GQA_HEADS):
        a = acc_ref[hh]
        o_ref[0, 0, hh * HEAD_DIM:(hh + 1) * HEAD_DIM, :] = (a[:HEAD_DIM] / a[HEAD_DIM:HEAD_DIM + 1]).astype(o_ref.dtype)


def _gqa_attention(aq, ak, av):
    b, nt, nq, _ = aq.shape
    t = nt * TM
    assert nt % 2 == 1
    return pl.pallas_call(
        functools.partial(_gqa_kernel, n_kv=nt),
        grid=(b, nt),
        in_specs=[pl.BlockSpec((1, 1, nq, TM), lambda bb, i: (bb, i, 0, 0)),
                  pl.BlockSpec((1, t, 128), lambda bb, i: (bb, 0, 0)),
                  pl.BlockSpec((1, nt, GQA_KV_HEADS * GQA_VROWS, TM), lambda bb, i: (bb, 0, 0, 0))],
        out_specs=pl.BlockSpec((1, 1, nq, TM), lambda bb, i: (bb, i, 0, 0)),
        out_shape=jax.ShapeDtypeStruct((b, nt, nq, TM), BF16),
        scratch_shapes=_attn_scratch(GQA_VROWS),
        compiler_params=_params("parallel", "arbitrary"),
        name="gqa_attn",
    )(aq, ak, av)


def _diff_kernel(q_ref, k_ref, v_ref, lam_ref, gain_ref, o_ref, qpad_ref, m_ref, acc_ref, s0_ref, s1_ref,
                 c0_ref, c1_ref, *, n_kv, lam_init):
    _attn_core(q_ref, k_ref, v_ref, qpad_ref, m_ref, acc_ref, (s0_ref, s1_ref), (c0_ref, c1_ref), n_kv=n_kv,
               k_col=lambda hc: (hc // 2) * 128, q_slot=lambda hc: hc % 2, v_row=lambda hc: (hc // 2) * DIFF_VROWS,
               v_rows=DIFF_VROWS)
    lp = lam_ref[...]
    lam = (jnp.exp(jnp.sum(lp[0:1] * lp[1:2], axis=1, keepdims=True))
           - jnp.exp(jnp.sum(lp[2:3] * lp[3:4], axis=1, keepdims=True)) + lam_init)
    gain = gain_ref[...]
    nv = 2 * HEAD_DIM
    for hh in range(DIFF_HEADS):
        a0, a1 = acc_ref[2 * hh], acc_ref[2 * hh + 1]
        o = a0[:nv] / a0[nv:nv + 1] - lam * (a1[:nv] / a1[nv:nv + 1])
        y = o * lax.rsqrt(jnp.mean(o * o, axis=0, keepdims=True) + NORM_EPS) * gain
        o_ref[0, 0, hh * nv:(hh + 1) * nv, :] = (y * (1.0 - lam_init)).astype(o_ref.dtype)


def _diff_attention(dq, dk, dv, lam_p, gain, layer_idx):
    b, nt, nq, _ = dq.shape
    t = nt * TM
    assert nt % 2 == 1
    lam_init = 0.8 - 0.6 * math.exp(-0.3 * layer_idx)
    return pl.pallas_call(
        functools.partial(_diff_kernel, n_kv=nt, lam_init=lam_init),
        grid=(b, nt),
        in_specs=[pl.BlockSpec((1, 1, nq, TM), lambda bb, i: (bb, i, 0, 0)),
                  pl.BlockSpec((1, t, DIFF_QK), lambda bb, i: (bb, 0, 0)),
                  pl.BlockSpec((1, nt, DIFF_HEADS * DIFF_VROWS, TM), lambda bb, i: (bb, 0, 0, 0)),
                  _const_spec(lam_p.shape),
                  _const_spec((2 * HEAD_DIM, 1))],
        out_specs=pl.BlockSpec((1, 1, DIFF_V, TM), lambda bb, i: (bb, i, 0, 0)),
        out_shape=jax.ShapeDtypeStruct((b, nt, DIFF_V, TM), BF16),
        scratch_shapes=_attn_scratch(DIFF_VROWS),
        compiler_params=_params("parallel", "arbitrary"),
        name="diff_attn",
    )(dq, dk, dv, lam_p, gain.reshape(2 * HEAD_DIM, 1))


def _split_bf16(x):
    hi = x.astype(BF16)
    return hi, (x - hi.astype(F32)).astype(BF16)


def _nt(a, b):
    return lax.dot_general(a, b, (((1,), (1,)), ((), ())), preferred_element_type=F32)


def _gla_kernel(qk_ref, la_ref, vt_ref, o_ref, st_ref, *, backward):
    @pl.when(pl.program_id(1) == 0)
    def _():
        st_ref[...] = jnp.zeros(st_ref.shape, F32)

    n_chunk = TM // GLA_CHUNK
    r = lax.broadcasted_iota(jnp.int32, (TM, TM), 0)
    c = lax.broadcasted_iota(jnp.int32, (TM, TM), 1)
    same = (r // GLA_CHUNK) == (c // GLA_CHUNK)
    causal = (c >= r) if backward else (c <= r)
    tri = jnp.where(same & causal, 1.0, 0.0).astype(BF16)
    blk = jnp.where(same, 1.0, 0.0).astype(BF16)
    la_hi, la_lo = _split_bf16(la_ref[0])
    cum = jnp.dot(tri, la_hi, preferred_element_type=F32) + jnp.dot(tri, la_lo, preferred_element_type=F32)
    tot = jnp.dot(blk, la_hi, preferred_element_type=F32) + jnp.dot(blk, la_lo, preferred_element_type=F32)
    q = qk_ref[0, :, :GLA_QK]
    k = qk_ref[0, :, GLA_QK:]
    q_in = (q * jnp.exp(cum)).astype(BF16)
    k_in = k * jnp.exp(-cum)
    k_end = k * jnp.exp(tot - cum)
    dec = jnp.exp(tot)
    lane = lax.broadcasted_iota(jnp.int32, (TM, 128), 1)
    row = lax.broadcasted_iota(jnp.int32, (TM, 128), 0)
    keep = same & ((r >= c) if backward else (r <= c))
    order = range(n_chunk - 1, -1, -1) if backward else range(n_chunk)
    for h in range(GLA_HEADS):
        pair = slice((h // 2) * 128, (h // 2 + 1) * 128)
        head_lanes = (lane // GLA_DK) == (h % 2)
        q_pair = q_in[:, pair]
        k_in_h = jnp.where(head_lanes, k_in[:, pair], 0.0).astype(BF16)
        k_end_h = jnp.where(head_lanes, k_end[:, pair], 0.0)
        vt = vt_ref[0, 0, h * GLA_DV:(h + 1) * GLA_DV, :]
        a_t = jnp.where(keep, _nt(k_in_h, q_pair), 0.0).astype(BF16)
        o_t = jnp.dot(vt, a_t, preferred_element_type=F32)
        s_t = st_ref[h]
        for ci in order:
            in_chunk = (row // GLA_CHUNK) == ci
            q_c = jnp.where(in_chunk, q_pair, jnp.zeros_like(q_pair))
            o_t = o_t + _nt(s_t.astype(BF16), q_c)
            k_c = jnp.where(in_chunk, k_end_h, 0.0).astype(BF16)
            s_t = s_t * dec[ci * GLA_CHUNK:ci * GLA_CHUNK + 1, pair] + jnp.dot(vt, k_c, preferred_element_type=F32)
        st_ref[h] = s_t
        o_ref[0, 0, h * GLA_DV:(h + 1) * GLA_DV, :] = o_t


def _gla_direction(qk, la, vt, backward):
    b, t, _ = qk.shape
    nt = t // TM
    d = 1 if backward else 0
    tile = (lambda i: jnp.where(i == 0, 0, nt - i)) if backward else (lambda i: i)
    return pl.pallas_call(
        functools.partial(_gla_kernel, backward=backward),
        grid=(b, nt),
        in_specs=[pl.BlockSpec((1, TM, 2 * GLA_QK), lambda bb, i: (bb, tile(i), 0)),
                  pl.BlockSpec((1, TM, GLA_QK), lambda bb, i: (bb, tile(i), d)),
                  pl.BlockSpec((1, 1, GLA_V, TM), lambda bb, i: (bb, tile(i), 0, 0))],
        out_specs=pl.BlockSpec((1, 1, GLA_V, TM), lambda bb, i: (bb, tile(i), 0, 0)),
        out_shape=jax.ShapeDtypeStruct((b, nt, GLA_V, TM), F32),
        scratch_shapes=[pltpu.VMEM((GLA_HEADS, GLA_DV, 128), F32)],
        compiler_params=_params("parallel", "arbitrary"),
        name="gla_bwd" if backward else "gla_fwd",
    )(qk, la, vt)


def _post_tail(x, mods, g, ya, yb_ref, woa_ref, wob_ref, w1_ref, w2_ref, o_ref, ff_chunk):
    yb = yb_ref[0, 0].astype(F32).T.astype(BF16)
    y = (jnp.dot(ya, woa_ref[...], preferred_element_type=F32)
         + jnp.dot(yb, wob_ref[...], preferred_element_type=F32))
    x1 = x + mods[2:3] * _rms(y, g[1:2])
    h = (_rms(x1, g[2:3]) * (1.0 + mods[4:5]) + mods[3:4]).astype(BF16)
    d_ff = w1_ref.shape[1]
    acc = jnp.zeros(x.shape, F32)
    for c in range(d_ff // ff_chunk):
        u = jnp.dot(h, w1_ref[:, c * ff_chunk:(c + 1) * ff_chunk], preferred_element_type=F32)
        u = jnp.square(jnp.maximum(u, 0.0)).astype(BF16)
        acc = acc + jnp.dot(u, w2_ref[c * ff_chunk:(c + 1) * ff_chunk, :], preferred_element_type=F32)
    o_ref[0] = x1 + mods[5:6] * _rms(acc, g[3:4])


def _post_gla_kernel(xs_ref, of_ref, ob_ref, gg_ref, gain_ref, yb_ref, mods_ref, g_ref, woa_ref, wob_ref, w1_ref,
                     w2_ref, o_ref, *, ff_chunk):
    gain = gain_ref[...]
    parts = []
    for hh in range(GLA_HEADS):
        rows = slice(hh * GLA_DV, (hh + 1) * GLA_DV)
        o = of_ref[0, 0, rows, :] + ob_ref[0, 0, rows, :]
        gt = gg_ref[0, 0, rows, :]
        y = o * lax.rsqrt(jnp.mean(o * o, axis=0, keepdims=True) + NORM_EPS) * gain
        parts.append(y * (gt / (1.0 + jnp.exp(-gt))))
    ya = jnp.concatenate(parts, axis=0).T.astype(BF16)
    _post_tail(xs_ref[0], mods_ref[0], g_ref[...], ya, yb_ref, woa_ref, wob_ref, w1_ref, w2_ref, o_ref, ff_chunk)


def _post_kernel(xs_ref, ya_ref, yb_ref, mods_ref, g_ref, woa_ref, wob_ref, w1_ref, w2_ref, o_ref, *, ff_chunk):
    _post_tail(xs_ref[0], mods_ref[0], g_ref[...], ya_ref[0].astype(BF16), yb_ref, woa_ref, wob_ref, w1_ref, w2_ref,
               o_ref, ff_chunk)


def _post(xs, ya, yb_t, mods_l, g, w_out, w1, w2, skip_ctx):
    b, t, d = xs.shape
    nt = t // TM
    half = w_out.shape[0] // 2
    woa = w_out[:half].astype(BF16)
    wob = w_out[half:].astype(BF16)
    w1 = w1.astype(BF16)
    w2 = w2.astype(BF16)
    off = 1 if skip_ctx else 0
    tok = lambda n: pl.BlockSpec((1, TM, n), lambda bb, i: (bb, i + off, 0))
    tr = lambda n: pl.BlockSpec((1, 1, n, TM), lambda bb, i: (bb, i + off, 0, 0))
    if isinstance(ya, tuple):
        o_f, o_b, gg, gain = ya
        body, name = _post_gla_kernel, "post_gla"
        a_specs = [tr(half), tr(half), tr(half), _const_spec((GLA_DV, 1))]
        a_args = [o_f, o_b, gg, gain.reshape(GLA_DV, 1)]
    else:
        body, name = _post_kernel, "post"
        a_specs, a_args = [tok(half)], [ya]
    return pl.pallas_call(
        functools.partial(body, ff_chunk=1024),
        grid=(b, nt - off),
        in_specs=[tok(d)] + a_specs + [
                  tr(half),
                  pl.BlockSpec((1, N_MOD, d), lambda bb, i: (_mod_row(bb, i + off), 0, 0)),
                  _const_spec(g.shape),
                  _const_spec(woa.shape), _const_spec(wob.shape), _const_spec(w1.shape), _const_spec(w2.shape)],
        out_specs=pl.BlockSpec((1, TM, d), lambda bb, i: (bb, i, 0)),
        out_shape=jax.ShapeDtypeStruct((b, t - off * TM, d), F32),
        compiler_params=_params("parallel", "parallel"),
        name=name,
    )(xs, *a_args, yb_t, mods_l, g, woa, wob, w1, w2)


def _hyena_filters_j(L, w1, b1, w2, b2, w3):
    pos = jnp.arange(L, dtype=F32)
    t = pos / max(L - 1, 1)
    w = 2 * math.pi * pos / L
    bands = jnp.linspace(1e-4, HY_BANDS - 1, HY_BANDS, dtype=F32)
    z = jnp.concatenate([t[:, None], jnp.cos(w[:, None] * bands), -jnp.sin(w[:, None] * bands)], axis=-1)
    hid = jnp.sin(z @ w1 + b1)
    hid = jnp.sin(hid @ w2 + b2)
    h = (hid @ w3).reshape(L, HY_ORDER, 2, HY_CH)
    deltas = jnp.abs(jnp.linspace(math.log(HY_TARGET) / HY_FAST_PCT, math.log(HY_TARGET) / HY_SLOW_PCT,
                                  HY_CH, dtype=F32))
    return h * jnp.exp(-t[:, None] * deltas)[:, None, None, :]


def _bidir_fft_conv_j(u, h_fwd, h_bwd):
    L, C = h_fwd.shape
    k = jnp.concatenate([h_fwd, jnp.zeros((1, C), h_fwd.dtype), h_bwd[1:][::-1]], axis=0)
    uf = jnp.fft.rfft(u, n=2 * L, axis=1)
    kf = jnp.fft.rfft(k, n=2 * L, axis=0)
    return jnp.fft.irfft(uf * kf[None], n=2 * L, axis=1)[:, :L]


def _hyena_j(u, conv_w, conv_b, f_w1, f_b1, f_w2, f_b2, f_w3, hy_bias, lc):
    def run(u):
        L = u.shape[1]
        up = jnp.pad(u, ((0, 0), (1, 1), (0, 0)))
        s = up[:, :-2] * conv_w[0] + up[:, 1:-1] * conv_w[1] + up[:, 2:] * conv_w[2] + conv_b
        v, x1, x2 = jnp.split(s, HY_ORDER + 1, axis=-1)
        h = _hyena_filters_j(L, f_w1, f_b1, f_w2, f_b2, f_w3)
        z = v
        for o, gate in enumerate((x1, x2)):
            z = gate * (_bidir_fft_conv_j(z, h[:, o, 0], h[:, o, 1]) + z * hy_bias[o])
        return z

    return jnp.concatenate([run(u[:, :lc]), run(u[:, lc:])], axis=1)


def _rope_tables_t(seq, lc):
    rows = seq // GRID_W
    r = jnp.broadcast_to(jnp.arange(rows, dtype=F32)[:, None], (rows, GRID_W)).reshape(-1)
    col = jnp.broadcast_to(jnp.arange(GRID_W, dtype=F32)[None, :], (rows, GRID_W)).reshape(-1)
    n_pairs = HEAD_DIM // 4
    inv = ROPE_THETA ** (-jnp.arange(n_pairs, dtype=F32) / n_pairs)
    ang = jnp.concatenate([r[:, None] * inv, col[:, None] * inv], axis=-1)
    cos = jnp.concatenate([jnp.ones((lc, 2 * n_pairs), F32), jnp.cos(ang)], axis=0)
    sin = jnp.concatenate([jnp.zeros((lc, 2 * n_pairs), F32), jnp.sin(ang)], axis=0)
    return cos.T, sin.T


def kernel(x, c, ctx, c_ctx, w_ada, b_ada, norm_g, w_mlp_in, w_mlp_out, ev_w_in, ev_w_lr, ev_b_lr, ev_gla_g,
           ev_qk_g, ev_w_out, od_w_in, od_conv_w, od_conv_b, od_f_w1, od_f_b1, od_f_w2, od_f_b2, od_f_w3,
           od_hy_bias, od_lam, od_diff_g, od_w_out):
    bsz, seq, d = x.shape
    lc = ctx.shape[1]
    depth = w_ada.shape[0]
    assert lc == TM and seq % TM == 0 and bsz <= 4
    cos_t, sin_t = _rope_tables_t(seq, lc)
    cc = jnp.zeros((8, d), F32).at[:bsz].set(c).at[4].set(c_ctx)
    mods = _mods(cc, w_ada, b_ada).reshape(depth, 8, N_MOD, d)
    xs = jnp.concatenate([ctx, x], axis=1)
    for i in range(depth):
        last = i == depth - 1
        j = i // 2
        g = norm_g[i]
        if i % 2 == 0:
            qk, la, gv, gg, aq, ak, av = _even_in(xs, mods[i], g[0:1], ev_w_in[j], ev_w_lr[j], ev_b_lr[j],
                                                  ev_qk_g[j], cos_t, sin_t)
            ya = (_gla_direction(qk, la, gv, False), _gla_direction(qk, la, gv, True), gg, ev_gla_g[j])
            yb = _gqa_attention(aq, ak, av)
            w_out = ev_w_out[j]
        else:
            u, dq, dk, dv = _odd_in(xs, mods[i], g[0:1], od_w_in[j], cos_t, sin_t)
            ya = _hyena_j(u, od_conv_w[j], od_conv_b[j], od_f_w1[j], od_f_b1[j], od_f_w2[j], od_f_b2[j],
                          od_f_w3[j], od_hy_bias[j], lc)
            yb = _diff_attention(dq, dk, dv, od_lam[j], od_diff_g[j], i)
            w_out = od_w_out[j]
        xs = _post(xs, ya, yb, mods[i], g, w_out, w_mlp_in[i], w_mlp_out[i], skip_ctx=last)
    return xs
```

```python
import functools
import math

import jax
import jax.numpy as jnp
import numpy as np
from jax import lax
from jax.experimental import pallas as pl
from jax.experimental.pallas import tpu as pltpu

F32 = jnp.float32
BF16 = jnp.bfloat16

TM = 256
GRID_W = 64
HEAD_DIM = 64
ROPE_THETA = 10000.0
NORM_EPS = 1e-6
N_MOD = 6
VMEM_LIMIT = 56 * 1024 * 1024

GLA_HEADS = 4
GLA_DK = 64
GLA_DV = 128
GLA_RANK = 16
GLA_NORMALIZER = 16.0
GLA_CHUNK = 64
GLA_QK = GLA_HEADS * GLA_DK
GLA_V = GLA_HEADS * GLA_DV
GQA_HEADS = 8
GQA_KV_HEADS = 2
HY_CH = 512
HY_ORDER = 2
HY_BANDS = 16
HY_TARGET = 1e-2
HY_FAST_PCT = 0.3
HY_SLOW_PCT = 1.5
HY_COLS = (HY_ORDER + 1) * HY_CH
DIFF_HEADS = 4
DIFF_QK = DIFF_HEADS * 2 * HEAD_DIM
DIFF_V = DIFF_HEADS * 2 * HEAD_DIM
ONES_ROWS = 16
N_SCORE_HEADS = 8
LOG2E = 1.4426950408889634
Q_SCALE = HEAD_DIM ** -0.5 * LOG2E
GQA_VROWS = HEAD_DIM + ONES_ROWS
DIFF_VROWS = 2 * HEAD_DIM + ONES_ROWS


def _params(*sem):
    return pltpu.CompilerParams(dimension_semantics=sem, vmem_limit_bytes=VMEM_LIMIT)


def _const_spec(shape):
    nd = len(shape)
    return pl.BlockSpec(shape, lambda *_: (0,) * nd, pipeline_mode=pl.Buffered(1))


def _rms(x, g):
    return x * lax.rsqrt(jnp.mean(x * x, axis=-1, keepdims=True) + NORM_EPS) * g


def _mod_row(b, i):
    return jnp.where(i == 0, 4, b)


def _mods_kernel(cc_ref, w_ref, b_ref, o_ref):
    cc = cc_ref[...]
    s = cc / (1.0 + jnp.exp(-cc))
    o_ref[0] = jnp.dot(s.astype(BF16), w_ref[0].astype(BF16), preferred_element_type=F32) + b_ref[0]


def _mods(cc, w_ada, b_ada):
    depth, d, n = w_ada.shape
    nb = n // 4
    return pl.pallas_call(
        _mods_kernel,
        grid=(depth, n // nb),
        in_specs=[pl.BlockSpec((8, d), lambda l, j: (0, 0)),
                  pl.BlockSpec((1, d, nb), lambda l, j: (l, 0, j)),
                  pl.BlockSpec((1, 1, nb), lambda l, j: (l, 0, j))],
        out_specs=pl.BlockSpec((1, 8, nb), lambda l, j: (l, 0, j)),
        out_shape=jax.ShapeDtypeStruct((depth, 8, n), F32),
        compiler_params=_params("parallel", "parallel"),
        name="ada_mods",
    )(cc, w_ada, b_ada.reshape(depth, 1, n))


def _normed_input(xs_ref, mods_ref, g_ref):
    x = xs_ref[0]
    mods = mods_ref[0]
    h = _rms(x, g_ref[0:1]) * (1.0 + mods[1:2]) + mods[0:1]
    return h.astype(BF16)


def _rope_rows(y, cos, sin):
    x1, x2 = y[:32], y[32:]
    return x1 * cos - x2 * sin, x1 * sin + x2 * cos


def _log_sigmoid(x):
    return jnp.minimum(x, 0.0) - jnp.log(1.0 + jnp.exp(-jnp.abs(x)))


def _even_in_kernel(xs_ref, mods_ref, g_ref, wn_ref, wt_ref, wlr_ref, blr_ref, gq_ref, gk_ref,
                    cos_ref, sin_ref, qk_ref, la_ref, gv_ref, gg_ref, aq_ref, ak_ref, av_ref):
    h = _normed_input(xs_ref, mods_ref, g_ref)
    p = jnp.dot(h, wn_ref[...], preferred_element_type=F32)
    qk_ref[0, :, :GLA_QK] = p[:, :GLA_QK] * GLA_DK ** -0.5
    qk_ref[0, :, GLA_QK:] = p[:, GLA_QK:2 * GLA_QK]
    lr = p[:, 2 * GLA_QK:].astype(BF16)
    gk = jnp.dot(lr, wlr_ref[...], preferred_element_type=F32) + blr_ref[...]
    la_ref[0] = _log_sigmoid(gk) * (1.0 / GLA_NORMALIZER)

    pt_all = lax.dot_general(wt_ref[...], h, (((1,), (1,)), ((), ())), preferred_element_type=F32)
    gv_ref[0, 0] = pt_all[:GLA_V].astype(BF16)
    gg_ref[0, 0] = pt_all[GLA_V:2 * GLA_V]
    pt = pt_all[2 * GLA_V:]
    cos, sin = cos_ref[...], sin_ref[...]
    gq, gk_gain = gq_ref[...], gk_ref[...]
    for hh in range(GQA_HEADS):
        blk = pt[hh * 64:(hh + 1) * 64]
        y = blk * lax.rsqrt(jnp.mean(blk * blk, axis=0, keepdims=True) + NORM_EPS) * gq
        o1, o2 = _rope_rows(y, cos, sin)
        aq_ref[0, 0, hh * 64:hh * 64 + 32, :] = (o1 * Q_SCALE).astype(BF16)
        aq_ref[0, 0, hh * 64 + 32:hh * 64 + 64, :] = (o2 * Q_SCALE).astype(BF16)
    kparts = []
    for hh in range(GQA_KV_HEADS):
        blk = pt[512 + hh * 64:512 + (hh + 1) * 64]
        y = blk * lax.rsqrt(jnp.mean(blk * blk, axis=0, keepdims=True) + NORM_EPS) * gk_gain
        o1, o2 = _rope_rows(y, cos, sin)
        kparts += [o1, o2]
    kt = jnp.concatenate(kparts, axis=0)
    ak_ref[0] = kt.T.astype(BF16)
    ones = jnp.ones((ONES_ROWS, TM), BF16)
    for hh in range(GQA_KV_HEADS):
        av_ref[0, 0, hh * GQA_VROWS:hh * GQA_VROWS + 64, :] = pt[640 + hh * 64:640 + (hh + 1) * 64].astype(BF16)
        av_ref[0, 0, hh * GQA_VROWS + 64:(hh + 1) * GQA_VROWS, :] = ones


def _odd_in_kernel(xs_ref, mods_ref, g_ref, wn_ref, wt_ref, cos_ref, sin_ref,
                   u_ref, dq_ref, dk_ref, dv_ref):
    h = _normed_input(xs_ref, mods_ref, g_ref)
    u_ref[0] = jnp.dot(h, wn_ref[...], preferred_element_type=F32)
    pt = lax.dot_general(wt_ref[...], h, (((1,), (1,)), ((), ())), preferred_element_type=F32)
    cos, sin = cos_ref[...], sin_ref[...]
    kparts = []
    for hh in range(2 * DIFF_HEADS):
        o1, o2 = _rope_rows(pt[hh * 64:(hh + 1) * 64], cos, sin)
        dq_ref[0, 0, hh * 64:hh * 64 + 32, :] = (o1 * Q_SCALE).astype(BF16)
        dq_ref[0, 0, hh * 64 + 32:hh * 64 + 64, :] = (o2 * Q_SCALE).astype(BF16)
        k1, k2 = _rope_rows(pt[DIFF_QK + hh * 64:DIFF_QK + (hh + 1) * 64], cos, sin)
        kparts += [k1, k2]
    kt = jnp.concatenate(kparts, axis=0)
    dk_ref[0] = kt.T.astype(BF16)
    ones = jnp.ones((ONES_ROWS, TM), BF16)
    nv = 2 * HEAD_DIM
    for hh in range(DIFF_HEADS):
        dv_ref[0, 0, hh * DIFF_VROWS:hh * DIFF_VROWS + nv, :] = pt[2 * DIFF_QK + hh * nv:2 * DIFF_QK + (hh + 1) * nv].astype(BF16)
        dv_ref[0, 0, hh * DIFF_VROWS + nv:(hh + 1) * DIFF_VROWS, :] = ones


def _pair_perm(n_heads):
    base = np.concatenate([np.arange(0, HEAD_DIM, 2), np.arange(1, HEAD_DIM, 2)])
    return np.concatenate([h * HEAD_DIM + base for h in range(n_heads)])


def _even_in(xs, mods_l, g0, w_in, w_lr, b_lr, qk_g, cos_t, sin_t):
    b, t, d = xs.shape
    nt = t // TM
    c_v = 2 * GLA_QK
    c_g = c_v + GLA_V
    c_lr0 = c_g + GLA_V
    c_lr = c_lr0 + 2 * GLA_RANK
    wn = jnp.concatenate([w_in[:, :c_v], w_in[:, c_lr0:c_lr], jnp.zeros((d, 128 - 2 * GLA_RANK), F32)],
                         axis=1).astype(BF16)
    cq = c_lr + GQA_HEADS * HEAD_DIM
    ck = cq + GQA_KV_HEADS * HEAD_DIM
    wq = w_in[:, c_lr:cq][:, _pair_perm(GQA_HEADS)]
    wk = w_in[:, cq:ck][:, _pair_perm(GQA_KV_HEADS)]
    wt = jnp.concatenate([w_in[:, c_v:c_lr0], wq, wk, w_in[:, ck:]], axis=1).T.astype(BF16)
    wlr = jnp.zeros((128, 2 * GLA_QK), F32)
    wlr = wlr.at[:GLA_RANK, :GLA_QK].set(w_lr[0]).at[GLA_RANK:2 * GLA_RANK, GLA_QK:].set(w_lr[1]).astype(BF16)
    blr = b_lr.reshape(1, 2 * GLA_QK)
    perm = _pair_perm(1)
    gq = qk_g[0][perm].reshape(HEAD_DIM, 1)
    gk = qk_g[1][perm].reshape(HEAD_DIM, 1)
    tok = lambda n: pl.BlockSpec((1, TM, n), lambda bb, i: (bb, i, 0))
    tr = lambda n: pl.BlockSpec((1, 1, n, TM), lambda bb, i: (bb, i, 0, 0))
    return pl.pallas_call(
        _even_in_kernel,
        grid=(b, nt),
        in_specs=[tok(d),
                  pl.BlockSpec((1, N_MOD, d), lambda bb, i: (_mod_row(bb, i), 0, 0)),
                  _const_spec((1, d)),
                  _const_spec(wn.shape), _const_spec(wt.shape), _const_spec(wlr.shape), _const_spec(blr.shape),
                  _const_spec(gq.shape), _const_spec(gk.shape),
                  pl.BlockSpec((32, TM), lambda bb, i: (0, i)),
                  pl.BlockSpec((32, TM), lambda bb, i: (0, i))],
        out_specs=[tok(2 * GLA_QK), tok(2 * GLA_QK), tr(GLA_V), tr(GLA_V), tr(512), tok(128),
                   tr(GQA_KV_HEADS * GQA_VROWS)],
        out_shape=[jax.ShapeDtypeStruct((b, t, 2 * GLA_QK), F32),
                   jax.ShapeDtypeStruct((b, t, 2 * GLA_QK), F32),
                   jax.ShapeDtypeStruct((b, nt, GLA_V, TM), BF16),
                   jax.ShapeDtypeStruct((b, nt, GLA_V, TM), F32),
                   jax.ShapeDtypeStruct((b, nt, 512, TM), BF16),
                   jax.ShapeDtypeStruct((b, t, 128), BF16),
                   jax.ShapeDtypeStruct((b, nt, GQA_KV_HEADS * GQA_VROWS, TM), BF16)],
        compiler_params=_params("parallel", "parallel"),
        name="even_in",
    )(xs, mods_l, g0, wn, wt, wlr, blr, gq, gk, cos_t, sin_t)


def _odd_in(xs, mods_l, g0, w_in, cos_t, sin_t):
    b, t, d = xs.shape
    nt = t // TM
    wn = w_in[:, :HY_COLS].astype(BF16)
    perm = _pair_perm(2 * DIFF_HEADS)
    wq = w_in[:, HY_COLS:HY_COLS + DIFF_QK][:, perm]
    wk = w_in[:, HY_COLS + DIFF_QK:HY_COLS + 2 * DIFF_QK][:, perm]
    wt = jnp.concatenate([wq, wk, w_in[:, HY_COLS + 2 * DIFF_QK:]], axis=1).T.astype(BF16)
    tok = lambda n: pl.BlockSpec((1, TM, n), lambda bb, i: (bb, i, 0))
    tr = lambda n: pl.BlockSpec((1, 1, n, TM), lambda bb, i: (bb, i, 0, 0))
    return pl.pallas_call(
        _odd_in_kernel,
        grid=(b, nt),
        in_specs=[tok(d),
                  pl.BlockSpec((1, N_MOD, d), lambda bb, i: (_mod_row(bb, i), 0, 0)),
                  _const_spec((1, d)),
                  _const_spec(wn.shape), _const_spec(wt.shape),
                  pl.BlockSpec((32, TM), lambda bb, i: (0, i)),
                  pl.BlockSpec((32, TM), lambda bb, i: (0, i))],
        out_specs=[tok(HY_COLS), tr(DIFF_QK), tok(DIFF_QK), tr(DIFF_HEADS * DIFF_VROWS)],
        out_shape=[jax.ShapeDtypeStruct((b, t, HY_COLS), F32),
                   jax.ShapeDtypeStruct((b, nt, DIFF_QK, TM), BF16),
                   jax.ShapeDtypeStruct((b, t, DIFF_QK), BF16),
                   jax.ShapeDtypeStruct((b, nt, DIFF_HEADS * DIFF_VROWS, TM), BF16)],
        compiler_params=_params("parallel", "parallel"),
        name="odd_in",
    )(xs, mods_l, g0, wn, wt, cos_t, sin_t)


def _attn_core(q_ref, k_ref, v_ref, qpad_ref, m_ref, acc_ref, s_refs, c_refs, *, n_kv, k_col, q_slot, v_row, v_rows):
    i = pl.program_id(1)
    zeros = jnp.zeros((HEAD_DIM, TM), BF16)
    for hh in range(N_SCORE_HEADS):
        qh = q_ref[0, 0, hh * HEAD_DIM:(hh + 1) * HEAD_DIM, :]
        qpad_ref[hh] = jnp.concatenate([qh, zeros] if q_slot(hh) == 0 else [zeros, qh], axis=0)
    m_ref[...] = jnp.full(m_ref.shape, -jnp.inf, F32)
    acc_ref[...] = jnp.zeros(acc_ref.shape, F32)

    def scores(j, hh, slot):
        kc = k_ref[0, pl.ds(pl.multiple_of(j * TM, TM), TM), k_col(hh):k_col(hh) + 128]
        s = jnp.dot(kc, qpad_ref[hh], preferred_element_type=F32)
        s_refs[slot][hh] = s
        c_refs[slot][hh] = jnp.max(s, axis=0, keepdims=True)

    def consume(j, hh, slot):
        vc = v_ref[0, j, v_row(hh):v_row(hh) + v_rows, :]
        m_old = m_ref[hh]
        m_new = jnp.maximum(m_old, c_refs[slot][hh])
        p = jnp.exp2(s_refs[slot][hh] - m_new).astype(BF16)
        alpha = jnp.exp2(m_old - m_new)
        acc_ref[hh] = acc_ref[hh] * alpha + jnp.dot(vc, p, preferred_element_type=F32)
        m_ref[hh] = m_new

    for hh in range(N_SCORE_HEADS):
        scores(0, hh, 0)

    def body(jj, carry):
        j = 2 * jj
        for hh in range(N_SCORE_HEADS):
            consume(j, hh, 0)
            scores(j + 1, hh, 1)
        for hh in range(N_SCORE_HEADS):
            consume(j + 1, hh, 1)
            scores(j + 2, hh, 0)
        return carry

    n = jnp.where(i == 0, 1, n_kv)
    lax.fori_loop(0, (n - 1) // 2, body, 0)
    for hh in range(N_SCORE_HEADS):
        consume(n - 1, hh, 0)


def _attn_scratch(v_rows):
    return [pltpu.VMEM((N_SCORE_HEADS, 128, TM), BF16),
            pltpu.VMEM((N_SCORE_HEADS, 1, TM), F32),
            pltpu.VMEM((N_SCORE_HEADS, v_rows, TM), F32),
            pltpu.VMEM((N_SCORE_HEADS, TM, TM), F32), pltpu.VMEM((N_SCORE_HEADS, TM, TM), F32),
            pltpu.VMEM((N_SCORE_HEADS, 1, TM), F32), pltpu.VMEM((N_SCORE_HEADS, 1, TM), F32)]


def _gqa_kernel(q_ref, k_ref, v_ref, o_ref, qpad_ref, m_ref, acc_ref, s0_ref, s1_ref, c0_ref, c1_ref, *, n_kv):
    group = GQA_HEADS // GQA_KV_HEADS
    _attn_core(q_ref, k_ref, v_ref, qpad_ref, m_ref, acc_ref, (s0_ref, s1_ref), (c0_ref, c1_ref), n_kv=n_kv,
               k_col=lambda h: 0, q_slot=lambda h: h // group, v_row=lambda h: (h // group) * GQA_VROWS,
               v_rows=GQA_VROWS)
    for hh in range(GQA_HEADS):
        a = acc_ref[hh]
        o_ref[0, 0, hh * HEAD_DIM:(hh + 1) * HEAD_DIM, :] = (a[:HEAD_DIM] / a[HEAD_DIM:HEAD_DIM + 1]).astype(o_ref.dtype)


def _gqa_attention(aq, ak, av):
    b, nt, nq, _ = aq.shape
    t = nt * TM
    assert nt % 2 == 1
    return pl.pallas_call(
        functools.partial(_gqa_kernel, n_kv=nt),
        grid=(b, nt),
        in_specs=[pl.BlockSpec((1, 1, nq, TM), lambda bb, i: (bb, i, 0, 0)),
                  pl.BlockSpec((1, t, 128), lambda bb, i: (bb, 0, 0)),
                  pl.BlockSpec((1, nt, GQA_KV_HEADS * GQA_VROWS, TM), lambda bb, i: (bb, 0, 0, 0))],
        out_specs=pl.BlockSpec((1, 1, nq, TM), lambda bb, i: (bb, i, 0, 0)),
        out_shape=jax.ShapeDtypeStruct((b, nt, nq, TM), BF16),
        scratch_shapes=_attn_scratch(GQA_VROWS),
        compiler_params=_params("parallel", "arbitrary"),
        name="gqa_attn",
    )(aq, ak, av)


def _diff_kernel(q_ref, k_ref, v_ref, lam_ref, gain_ref, o_ref, qpad_ref, m_ref, acc_ref, s0_ref, s1_ref,
                 c0_ref, c1_ref, *, n_kv, lam_init):
    _attn_core(q_ref, k_ref, v_ref, qpad_ref, m_ref, acc_ref, (s0_ref, s1_ref), (c0_ref, c1_ref), n_kv=n_kv,
               k_col=lambda hc: (hc // 2) * 128, q_slot=lambda hc: hc % 2, v_row=lambda hc: (hc // 2) * DIFF_VROWS,
               v_rows=DIFF_VROWS)
    lp = lam_ref[...]
    lam = (jnp.exp(jnp.sum(lp[0:1] * lp[1:2], axis=1, keepdims=True))
           - jnp.exp(jnp.sum(lp[2:3] * lp[3:4], axis=1, keepdims=True)) + lam_init)
    gain = gain_ref[...]
    nv = 2 * HEAD_DIM
    for hh in range(DIFF_HEADS):
        a0, a1 = acc_ref[2 * hh], acc_ref[2 * hh + 1]
        o = a0[:nv] / a0[nv:nv + 1] - lam * (a1[:nv] / a1[nv:nv + 1])
        y = o * lax.rsqrt(jnp.mean(o * o, axis=0, keepdims=True) + NORM_EPS) * gain
        o_ref[0, 0, hh * nv:(hh + 1) * nv, :] = (y * (1.0 - lam_init)).astype(o_ref.dtype)


def _diff_attention(dq, dk, dv, lam_p, gain, layer_idx):
    b, nt, nq, _ = dq.shape
    t = nt * TM
    assert nt % 2 == 1
    lam_init = 0.8 - 0.6 * math.exp(-0.3 * layer_idx)
    return pl.pallas_call(
        functools.partial(_diff_kernel, n_kv=nt, lam_init=lam_init),
        grid=(b, nt),
        in_specs=[pl.BlockSpec((1, 1, nq, TM), lambda bb, i: (bb, i, 0, 0)),
                  pl.BlockSpec((1, t, DIFF_QK), lambda bb, i: (bb, 0, 0)),
                  pl.BlockSpec((1, nt, DIFF_HEADS * DIFF_VROWS, TM), lambda bb, i: (bb, 0, 0, 0)),
                  _const_spec(lam_p.shape),
                  _const_spec((2 * HEAD_DIM, 1))],
        out_specs=pl.BlockSpec((1, 1, DIFF_V, TM), lambda bb, i: (bb, i, 0, 0)),
        out_shape=jax.ShapeDtypeStruct((b, nt, DIFF_V, TM), BF16),
        scratch_shapes=_attn_scratch(DIFF_VROWS),
        compiler_params=_params("parallel", "arbitrary"),
        name="diff_attn",
    )(dq, dk, dv, lam_p, gain.reshape(2 * HEAD_DIM, 1))


def _split_bf16(x):
    hi = x.astype(BF16)
    return hi, (x - hi.astype(F32)).astype(BF16)


def _nt(a, b):
    return lax.dot_general(a, b, (((1,), (1,)), ((), ())), preferred_element_type=F32)


def _gla_kernel(qk_ref, la_ref, vt_ref, o_ref, st_ref, *, backward):
    @pl.when(pl.program_id(1) == 0)
    def _():
        st_ref[...] = jnp.zeros(st_ref.shape, F32)

    n_chunk = TM // GLA_CHUNK
    r = lax.broadcasted_iota(jnp.int32, (TM, TM), 0)
    c = lax.broadcasted_iota(jnp.int32, (TM, TM), 1)
    same = (r // GLA_CHUNK) == (c // GLA_CHUNK)
    causal = (c >= r) if backward else (c <= r)
    tri = jnp.where(same & causal, 1.0, 0.0).astype(BF16)
    blk = jnp.where(same, 1.0, 0.0).astype(BF16)
    la_hi, la_lo = _split_bf16(la_ref[0])
    cum = jnp.dot(tri, la_hi, preferred_element_type=F32) + jnp.dot(tri, la_lo, preferred_element_type=F32)
    tot = jnp.dot(blk, la_hi, preferred_element_type=F32) + jnp.dot(blk, la_lo, preferred_element_type=F32)
    q = qk_ref[0, :, :GLA_QK]
    k = qk_ref[0, :, GLA_QK:]
    q_in = (q * jnp.exp(cum)).astype(BF16)
    k_in = k * jnp.exp(-cum)
    k_end = k * jnp.exp(tot - cum)
    dec = jnp.exp(tot)
    lane = lax.broadcasted_iota(jnp.int32, (TM, 128), 1)
    row = lax.broadcasted_iota(jnp.int32, (TM, 128), 0)
    keep = same & ((r >= c) if backward else (r <= c))
    order = range(n_chunk - 1, -1, -1) if backward else range(n_chunk)
    for h in range(GLA_HEADS):
        pair = slice((h // 2) * 128, (h // 2 + 1) * 128)
        head_lanes = (lane // GLA_DK) == (h % 2)
        q_pair = q_in[:, pair]
        k_in_h = jnp.where(head_lanes, k_in[:, pair], 0.0).astype(BF16)
        k_end_h = jnp.where(head_lanes, k_end[:, pair], 0.0)
        vt = vt_ref[0, 0, h * GLA_DV:(h + 1) * GLA_DV, :]
        a_t = jnp.where(keep, _nt(k_in_h, q_pair), 0.0).astype(BF16)
        o_t = jnp.dot(vt, a_t, preferred_element_type=F32)
        s_t = st_ref[h]
        for ci in order:
            in_chunk = (row // GLA_CHUNK) == ci
            q_c = jnp.where(in_chunk, q_pair, jnp.zeros_like(q_pair))
            o_t = o_t + _nt(s_t.astype(BF16), q_c)
            k_c = jnp.where(in_chunk, k_end_h, 0.0).astype(BF16)
            s_t = s_t * dec[ci * GLA_CHUNK:ci * GLA_CHUNK + 1, pair] + jnp.dot(vt, k_c, preferred_element_type=F32)
        st_ref[h] = s_t
        o_ref[0, 0, h * GLA_DV:(h + 1) * GLA_DV, :] = o_t


def _gla_direction(qk, la, vt, backward):
    b, t, _ = qk.shape
    nt = t // TM
    d = 1 if backward else 0
    tile = (lambda i: jnp.where(i == 0, 0, nt - i)) if backward else (lambda i: i)
    return pl.pallas_call(
        functools.partial(_gla_kernel, backward=backward),
        grid=(b, nt),
        in_specs=[pl.BlockSpec((1, TM, 2 * GLA_QK), lambda bb, i: (bb, tile(i), 0)),
                  pl.BlockSpec((1, TM, GLA_QK), lambda bb, i: (bb, tile(i), d)),
                  pl.BlockSpec((1, 1, GLA_V, TM), lambda bb, i: (bb, tile(i), 0, 0))],
        out_specs=pl.BlockSpec((1, 1, GLA_V, TM), lambda bb, i: (bb, tile(i), 0, 0)),
        out_shape=jax.ShapeDtypeStruct((b, nt, GLA_V, TM), F32),
        scratch_shapes=[pltpu.VMEM((GLA_HEADS, GLA_DV, 128), F32)],
        compiler_params=_params("parallel", "arbitrary"),
        name="gla_bwd" if backward else "gla_fwd",
    )(qk, la, vt)


def _post_tail(x, mods, g, ya, yb_ref, woa_ref, wob_ref, w1_ref, w2_ref, o_ref, ff_chunk):
    yb = yb_ref[0, 0].astype(F32).T.astype(BF16)
    y = (jnp.dot(ya, woa_ref[...], preferred_element_type=F32)
         + jnp.dot(yb, wob_ref[...], preferred_element_type=F32))
    x1 = x + mods[2:3] * _rms(y, g[1:2])
    h = (_rms(x1, g[2:3]) * (1.0 + mods[4:5]) + mods[3:4]).astype(BF16)
    d_ff = w1_ref.shape[1]
    acc = jnp.zeros(x.shape, F32)
    for c in range(d_ff // ff_chunk):
        u = jnp.dot(h, w1_ref[:, c * ff_chunk:(c + 1) * ff_chunk], preferred_element_type=F32)
        u = jnp.square(jnp.maximum(u, 0.0)).astype(BF16)
        acc = acc + jnp.dot(u, w2_ref[c * ff_chunk:(c + 1) * ff_chunk, :], preferred_element_type=F32)
    o_ref[0] = x1 + mods[5:6] * _rms(acc, g[3:4])


def _post_gla_kernel(xs_ref, of_ref, ob_ref, gg_ref, gain_ref, yb_ref, mods_ref, g_ref, woa_ref, wob_ref, w1_ref,
                     w2_ref, o_ref, *, ff_chunk):
    gain = gain_ref[...]
    parts = []
    for hh in range(GLA_HEADS):
        rows = slice(hh * GLA_DV, (hh + 1) * GLA_DV)
        o = of_ref[0, 0, rows, :] + ob_ref[0, 0, rows, :]
        gt = gg_ref[0, 0, rows, :]
        y = o * lax.rsqrt(jnp.mean(o * o, axis=0, keepdims=True) + NORM_EPS) * gain
        parts.append(y * (gt / (1.0 + jnp.exp(-gt))))
    ya = jnp.concatenate(parts, axis=0).T.astype(BF16)
    _post_tail(xs_ref[0], mods_ref[0], g_ref[...], ya, yb_ref, woa_ref, wob_ref, w1_ref, w2_ref, o_ref, ff_chunk)


def _post_kernel(xs_ref, ya_ref, yb_ref, mods_ref, g_ref, woa_ref, wob_ref, w1_ref, w2_ref, o_ref, *, ff_chunk):
    _post_tail(xs_ref[0], mods_ref[0], g_ref[...], ya_ref[0].astype(BF16), yb_ref, woa_ref, wob_ref, w1_ref, w2_ref,
               o_ref, ff_chunk)


def _post(xs, ya, yb_t, mods_l, g, w_out, w1, w2, skip_ctx):
    b, t, d = xs.shape
    nt = t // TM
    half = w_out.shape[0] // 2
    woa = w_out[:half].astype(BF16)
    wob = w_out[half:].astype(BF16)
    w1 = w1.astype(BF16)
    w2 = w2.astype(BF16)
    off = 1 if skip_ctx else 0
    tok = lambda n: pl.BlockSpec((1, TM, n), lambda bb, i: (bb, i + off, 0))
    tr = lambda n: pl.BlockSpec((1, 1, n, TM), lambda bb, i: (bb, i + off, 0, 0))
    if isinstance(ya, tuple):
        o_f, o_b, gg, gain = ya
        body, name = _post_gla_kernel, "post_gla"
        a_specs = [tr(half), tr(half), tr(half), _const_spec((GLA_DV, 1))]
        a_args = [o_f, o_b, gg, gain.reshape(GLA_DV, 1)]
    else:
        body, name = _post_kernel, "post"
        a_specs, a_args = [tok(half)], [ya]
    return pl.pallas_call(
        functools.partial(body, ff_chunk=1024),
        grid=(b, nt - off),
        in_specs=[tok(d)] + a_specs + [
                  tr(half),
                  pl.BlockSpec((1, N_MOD, d), lambda bb, i: (_mod_row(bb, i + off), 0, 0)),
                  _const_spec(g.shape),
                  _const_spec(woa.shape), _const_spec(wob.shape), _const_spec(w1.shape), _const_spec(w2.shape)],
        out_specs=pl.BlockSpec((1, TM, d), lambda bb, i: (bb, i, 0)),
        out_shape=jax.ShapeDtypeStruct((b, t - off * TM, d), F32),
        compiler_params=_params("parallel", "parallel"),
        name=name,
    )(xs, *a_args, yb_t, mods_l, g, woa, wob, w1, w2)


def _short_conv_kernel(u_ref, prev_ref, next_ref, w_ref, b_ref, v_ref, x1_ref, x2_ref, *, n_tiles):
    i = pl.program_id(1)
    u = u_ref[0]
    rows = lax.broadcasted_iota(jnp.int32, u.shape, 0)
    prev_row = jnp.where(i == 0, 0.0, prev_ref[0, 7:8, :])
    next_row = jnp.where(i == n_tiles - 1, 0.0, next_ref[0, 0:1, :])
    up = jnp.where(rows == 0, prev_row, pltpu.roll(u, 1, axis=0))
    dn = jnp.where(rows == TM - 1, next_row, pltpu.roll(u, TM - 1, axis=0))
    w = w_ref[...]
    s = up * w[0:1] + u * w[1:2] + dn * w[2:3] + b_ref[...]
    v_ref[0] = s[:, :HY_CH]
    x1_ref[0] = s[:, HY_CH:2 * HY_CH]
    x2_ref[0] = s[:, 2 * HY_CH:]


def _short_conv(u, conv_w, conv_b, first_tile, n_tiles):
    b, t, c = u.shape
    r8 = TM // 8
    last_blk = t // 8 - 1
    out = jax.ShapeDtypeStruct((b, n_tiles * TM, HY_CH), F32)
    ospec = pl.BlockSpec((1, TM, HY_CH), lambda bb, i: (bb, i, 0))
    return pl.pallas_call(
        functools.partial(_short_conv_kernel, n_tiles=n_tiles),
        grid=(b, n_tiles),
        in_specs=[pl.BlockSpec((1, TM, c), lambda bb, i: (bb, i + first_tile, 0)),
                  pl.BlockSpec((1, 8, c), lambda bb, i: (bb, jnp.maximum((i + first_tile) * r8 - 1, 0), 0)),
                  pl.BlockSpec((1, 8, c), lambda bb, i: (bb, jnp.minimum((i + first_tile + 1) * r8, last_blk), 0)),
                  _const_spec(conv_w.shape), _const_spec((1, c))],
        out_specs=[ospec, ospec, ospec],
        out_shape=[out, out, out],
        compiler_params=_params("parallel", "parallel"),
        name="hy_short",
    )(u, u, u, conv_w, conv_b.reshape(1, c))


def _dot3(a, b):
    ah, al = _split_bf16(a)
    bh, bl = _split_bf16(b)
    d = lambda x, y: jnp.dot(x, y, preferred_element_type=F32)
    return d(ah, bh) + (d(al, bh) + d(ah, bl))


def _filter_kernel(w1t_ref, w1c_ref, w1s_ref, b1_ref, w2_ref, b2_ref, w3_ref, delta_ref, o_ref, *, seq, rb):
    n = pl.program_id(1) * rb + lax.broadcasted_iota(jnp.int32, (rb, 1), 0)
    d = jnp.where(n < seq, n, 2 * seq - n).astype(F32)
    t = d / max(seq - 1, 1)
    w = (2 * math.pi / seq) * d
    band = lax.broadcasted_iota(jnp.int32, (1, HY_BANDS), 1).astype(F32)
    bands = 1e-4 + band * ((HY_BANDS - 1 - 1e-4) / (HY_BANDS - 1))
    ang = w * bands
    pre = t * w1t_ref[...] + _dot3(jnp.cos(ang), w1c_ref[...]) - _dot3(jnp.sin(ang), w1s_ref[...]) + b1_ref[...]
    hid = jnp.sin(pre)
    hid = jnp.sin(_dot3(hid, w2_ref[...]) + b2_ref[...])
    h = _dot3(hid, w3_ref[0]) * jnp.exp(-t * delta_ref[...])
    o_ref[0] = jnp.where(n == seq, 0.0, h).astype(o_ref.dtype)


def _hyena_kernels(seq, f_w1, f_b1, f_w2, f_b2, f_w3):
    rb = min(512, seq)
    nblk = 2 * seq // rb
    ffn = f_w2.shape[0]
    w3 = f_w3.reshape(ffn, HY_ORDER * 2, HY_CH).transpose(1, 0, 2)
    deltas = jnp.abs(jnp.linspace(math.log(HY_TARGET) / HY_FAST_PCT, math.log(HY_TARGET) / HY_SLOW_PCT,
                                  HY_CH, dtype=F32)).reshape(1, HY_CH)
    return pl.pallas_call(
        functools.partial(_filter_kernel, seq=seq, rb=rb),
        grid=(HY_ORDER, nblk),
        in_specs=[_const_spec((1, ffn)), _const_spec((HY_BANDS, ffn)), _const_spec((HY_BANDS, ffn)),
                  _const_spec((1, ffn)), _const_spec((ffn, ffn)), _const_spec((1, ffn)),
                  pl.BlockSpec((1, ffn, HY_CH), lambda o, j: (2 * o + (2 * j >= nblk), 0, 0)),
                  _const_spec((1, HY_CH))],
        out_specs=pl.BlockSpec((1, rb, HY_CH), lambda o, j: (o, j, 0)),
        out_shape=jax.ShapeDtypeStruct((HY_ORDER, 2 * seq, HY_CH), BF16),
        compiler_params=_params("parallel", "parallel"),
        name="hy_filter",
    )(f_w1[0:1], f_w1[1:1 + HY_BANDS], f_w1[1 + HY_BANDS:], f_b1.reshape(1, ffn), f_w2, f_b2.reshape(1, ffn), w3,
      deltas)


def _dft_tables(seq):
    n = 2 * seq
    r = math.isqrt(n)
    assert r * r == n
    idx = jnp.arange(r, dtype=jnp.int32)
    phi = (2 * math.pi / r) * ((idx[:, None] * idx[None, :]) % r).astype(F32)
    f1 = jnp.concatenate([jnp.cos(phi), -jnp.sin(phi)], axis=0).astype(BF16)
    m = (idx[:, None, None] + r * idx[None, :, None]) * idx[None, None, :]
    th = (2 * math.pi / n) * (m % n).astype(F32)
    c, s = jnp.cos(th), jnp.sin(th)
    blockm = lambda re, im: jnp.concatenate([jnp.concatenate([re, -im], axis=-1),
                                             jnp.concatenate([im, re], axis=-1)], axis=-2)
    g = blockm(c, -s).astype(BF16)
    ginv = blockm(jnp.swapaxes(c, 1, 2), jnp.swapaxes(s, 1, 2)).astype(BF16)
    finv = (jnp.concatenate([jnp.cos(phi.T), -jnp.sin(phi.T)], axis=1)[:r // 2] / n).astype(BF16)
    return r, f1, g, ginv, finv


def _dft1_kernel(f_ref, x_ref, o_ref):
    o_ref[0] = jnp.dot(f_ref[...], x_ref[0].astype(BF16), preferred_element_type=F32).astype(o_ref.dtype)


def _dft_stage1(f, x2d, nb):
    b, ka, cols = x2d.shape
    return pl.pallas_call(
        _dft1_kernel,
        grid=(b, cols // nb),
        in_specs=[_const_spec(f.shape), pl.BlockSpec((1, ka, nb), lambda bb, j: (bb, 0, j))],
        out_specs=pl.BlockSpec((1, f.shape[0], nb), lambda bb, j: (bb, 0, j)),
        out_shape=jax.ShapeDtypeStruct((b, f.shape[0], cols), BF16),
        compiler_params=_params("parallel", "parallel"),
        name="hy_dft1",
    )(f, x2d)


def _fspec_kernel(a_ref, g_ref, o_ref, *, kb, r):
    for kk in range(kb):
        a = a_ref[0, :, kk].reshape(2 * r, a_ref.shape[-1])
        o_ref[0, kk] = jnp.dot(g_ref[kk], a, preferred_element_type=F32)


def _filter_spectrum(a, g, kb):
    o, _, r, _, c = a.shape
    return pl.pallas_call(
        functools.partial(_fspec_kernel, kb=kb, r=r),
        grid=(o, r // kb),
        in_specs=[pl.BlockSpec((1, 2, kb, r, c), lambda oo, j: (oo, 0, j, 0, 0)),
                  pl.BlockSpec((kb, 2 * r, 2 * r), lambda oo, j: (j, 0, 0))],
        out_specs=pl.BlockSpec((1, kb, 2 * r, c), lambda oo, j: (oo, j, 0, 0)),
        out_shape=jax.ShapeDtypeStruct((o, r, 2 * r, c), F32),
        compiler_params=_params("parallel", "parallel"),
        name="hy_fspec",
    )(a, g)


def _dft2_kernel(a_ref, g_ref, gi_ref, k_ref, o_ref, *, kb, r):
    for kk in range(kb):
        a = a_ref[0, :, kk].reshape(2 * r, a_ref.shape[-1])
        x = jnp.dot(g_ref[kk], a, preferred_element_type=F32)
        ks = k_ref[0, kk]
        xr, xi, kr, ki = x[:r], x[r:], ks[:r], ks[r:]
        y = jnp.concatenate([xr * kr - xi * ki, xr * ki + xi * kr], axis=0).astype(BF16)
        bq = jnp.dot(gi_ref[kk], y, preferred_element_type=F32)
        o_ref[0, 0, kk] = bq[:r].astype(o_ref.dtype)
        o_ref[0, 1, kk] = bq[r:].astype(o_ref.dtype)


def _dft_stage2(a, g, ginv, kspec, order, kb):
    b, _, r, _, c = a.shape
    blk = pl.BlockSpec((1, 2, kb, r, c), lambda j, bb: (bb, 0, j, 0, 0))
    tab = pl.BlockSpec((kb, 2 * r, 2 * r), lambda j, bb: (j, 0, 0))
    return pl.pallas_call(
        functools.partial(_dft2_kernel, kb=kb, r=r),
        grid=(r // kb, b),
        in_specs=[blk, tab, tab, pl.BlockSpec((1, kb, 2 * r, c), lambda j, bb: (order, j, 0, 0))],
        out_specs=blk,
        out_shape=jax.ShapeDtypeStruct(a.shape, BF16),
        compiler_params=_params("parallel", "arbitrary"),
        name="hy_dft2",
    )(a, g, ginv, kspec)


def _dft3_kernel(f_ref, b_ref, gate_ref, z_ref, bias_ref, o_ref):
    y = jnp.dot(f_ref[...], b_ref[0], preferred_element_type=F32)
    o_ref[0] = gate_ref[0] * (y + z_ref[0] * bias_ref[...])


def _dft_stage3(finv, bq2d, gate2d, z2d, bias_row, nb):
    b, _, cols = bq2d.shape
    ra = finv.shape[0]
    io = pl.BlockSpec((1, ra, nb), lambda bb, j: (bb, 0, j))
    return pl.pallas_call(
        _dft3_kernel,
        grid=(b, cols // nb),
        in_specs=[_const_spec(finv.shape), pl.BlockSpec((1, finv.shape[1], nb), lambda bb, j: (bb, 0, j)), io, io,
                  _const_spec((1, nb))],
        out_specs=io,
        out_shape=jax.ShapeDtypeStruct((b, ra, cols), F32),
        compiler_params=_params("parallel", "parallel"),
        name="hy_dft3",
    )(finv, bq2d, gate2d, z2d, bias_row)


def _long_conv_gated(v, gates, kern, hy_bias):
    b, seq, c = v.shape
    r, f1, g, ginv, finv = _dft_tables(seq)
    nb = 8 * c
    kb = 8
    ka = _dft_stage1(f1, kern.reshape(HY_ORDER, r, r * c), nb)
    kspec = _filter_spectrum(ka.reshape(HY_ORDER, 2, r, r, c), g, kb)
    z = v
    for o in range(HY_ORDER):
        a = _dft_stage1(f1[:, :r // 2], z.reshape(b, r // 2, r * c), nb)
        bq = _dft_stage2(a.reshape(b, 2, r, r, c), g, ginv, kspec, o, kb)
        bias_row = jnp.tile(hy_bias[o], nb // c).reshape(1, nb)
        z = _dft_stage3(finv, bq.reshape(b, 2 * r, r * c), gates[o].reshape(b, r // 2, r * c),
                        z.reshape(b, r // 2, r * c), bias_row, nb).reshape(b, seq, c)
    return z


def _ctx_conv_kernel(v_ref, x1_ref, x2_ref, k_ref, ff_ref, fi_ref, bias_ref, o_ref):
    n = ff_ref.shape[1]
    z = v_ref[0]
    for o, gate_ref in enumerate((x1_ref, x2_ref)):
        ks = jnp.dot(ff_ref[...], k_ref[o], preferred_element_type=F32)
        x = jnp.dot(ff_ref[:, :TM], z.astype(BF16), preferred_element_type=F32)
        xr, xi, kr, ki = x[:n], x[n:], ks[:n], ks[n:]
        y = jnp.concatenate([xr * kr - xi * ki, xr * ki + xi * kr], axis=0).astype(BF16)
        conv = jnp.dot(fi_ref[...], y, preferred_element_type=F32)
        z = gate_ref[0] * (conv + z * bias_ref[o:o + 1])
    o_ref[0] = z


def _ctx_conv_gated(v, x1, x2, kern, hy_bias):
    b, seq, c = v.shape
    n = 2 * seq
    idx = jnp.arange(n, dtype=jnp.int32)
    th = (2 * math.pi / n) * ((idx[:, None] * idx[None, :]) % n).astype(F32)
    ff = jnp.concatenate([jnp.cos(th), -jnp.sin(th)], axis=0).astype(BF16)
    fi = (jnp.concatenate([jnp.cos(th), -jnp.sin(th)], axis=1)[:seq] / n).astype(BF16)
    io = pl.BlockSpec((1, seq, c), lambda bb: (bb, 0, 0))
    return pl.pallas_call(
        _ctx_conv_kernel,
        grid=(b,),
        in_specs=[io, io, io, _const_spec(kern.shape), _const_spec(ff.shape), _const_spec(fi.shape),
                  _const_spec(hy_bias.shape)],
        out_specs=io,
        out_shape=jax.ShapeDtypeStruct((b, seq, c), F32),
        compiler_params=_params("parallel"),
        name="hy_ctx",
    )(v, x1, x2, kern, ff, fi, hy_bias)


def _hyena(u, lc, conv_w, conv_b, f_w1, f_b1, f_w2, f_b2, f_w3, hy_bias):
    seq = u.shape[1] - lc
    vc, x1c, x2c = _short_conv(u, conv_w, conv_b, 0, lc // TM)
    vx, x1x, x2x = _short_conv(u, conv_w, conv_b, lc // TM, seq // TM)
    yc = _ctx_conv_gated(vc, x1c, x2c, _hyena_kernels(lc, f_w1, f_b1, f_w2, f_b2, f_w3), hy_bias)
    yx = _long_conv_gated(vx, (x1x, x2x), _hyena_kernels(seq, f_w1, f_b1, f_w2, f_b2, f_w3), hy_bias)
    return jnp.concatenate([yc, yx], axis=1)


def _rope_tables_t(seq, lc):
    rows = seq // GRID_W
    r = jnp.broadcast_to(jnp.arange(rows, dtype=F32)[:, None], (rows, GRID_W)).reshape(-1)
    col = jnp.broadcast_to(jnp.arange(GRID_W, dtype=F32)[None, :], (rows, GRID_W)).reshape(-1)
    n_pairs = HEAD_DIM // 4
    inv = ROPE_THETA ** (-jnp.arange(n_pairs, dtype=F32) / n_pairs)
    ang = jnp.concatenate([r[:, None] * inv, col[:, None] * inv], axis=-1)
    cos = jnp.concatenate([jnp.ones((lc, 2 * n_pairs), F32), jnp.cos(ang)], axis=0)
    sin = jnp.concatenate([jnp.zeros((lc, 2 * n_pairs), F32), jnp.sin(ang)], axis=0)
    return cos.T, sin.T


def kernel(x, c, ctx, c_ctx, w_ada, b_ada, norm_g, w_mlp_in, w_mlp_out, ev_w_in, ev_w_lr, ev_b_lr, ev_gla_g,
           ev_qk_g, ev_w_out, od_w_in, od_conv_w, od_conv_b, od_f_w1, od_f_b1, od_f_w2, od_f_b2, od_f_w3,
           od_hy_bias, od_lam, od_diff_g, od_w_out):
    bsz, seq, d = x.shape
    lc = ctx.shape[1]
    depth = w_ada.shape[0]
    assert lc == TM and seq % TM == 0 and bsz <= 4
    cos_t, sin_t = _rope_tables_t(seq, lc)
    cc = jnp.zeros((8, d), F32).at[:bsz].set(c).at[4].set(c_ctx)
    mods = _mods(cc, w_ada, b_ada).reshape(depth, 8, N_MOD, d)
    xs = jnp.concatenate([ctx, x], axis=1)
    for i in range(depth):
        last = i == depth - 1
        j = i // 2
        g = norm_g[i]
        if i % 2 == 0:
            qk, la, gv, gg, aq, ak, av = _even_in(xs, mods[i], g[0:1], ev_w_in[j], ev_w_lr[j], ev_b_lr[j],
                                                  ev_qk_g[j], cos_t, sin_t)
            ya = (_gla_direction(qk, la, gv, False), _gla_direction(qk, la, gv, True), gg, ev_gla_g[j])
            yb = _gqa_attention(aq, ak, av)
            w_out = ev_w_out[j]
        else:
            u, dq, dk, dv = _odd_in(xs, mods[i], g[0:1], od_w_in[j], cos_t, sin_t)
            ya = _hyena(u, lc, od_conv_w[j], od_conv_b[j], od_f_w1[j], od_f_b1[j], od_f_w2[j], od_f_b2[j],
                        od_f_w3[j], od_hy_bias[j])
            yb = _diff_attention(dq, dk, dv, od_lam[j], od_diff_g[j], i)
            w_out = od_w_out[j]
        xs = _post(xs, ya, yb, mods[i], g, w_out, w_mlp_in[i], w_mlp_out[i], skip_ctx=last)
    return xs
```

```python
import functools
import math

import jax
import jax.numpy as jnp
import numpy as np
from jax import lax
from jax.experimental import pallas as pl
from jax.experimental.pallas import tpu as pltpu

F32 = jnp.float32
BF16 = jnp.bfloat16

TM = 256
GRID_W = 64
HEAD_DIM = 64
ROPE_THETA = 10000.0
NORM_EPS = 1e-6
N_MOD = 6
VMEM_LIMIT = 56 * 1024 * 1024

GLA_HEADS = 4
GLA_DK = 64
GLA_DV = 128
GLA_RANK = 16
GLA_NORMALIZER = 16.0
GLA_CHUNK = 64
GLA_QK = GLA_HEADS * GLA_DK
GLA_V = GLA_HEADS * GLA_DV
GQA_HEADS = 8
GQA_KV_HEADS = 2
HY_CH = 512
HY_ORDER = 2
HY_BANDS = 16
HY_TARGET = 1e-2
HY_FAST_PCT = 0.3
HY_SLOW_PCT = 1.5
HY_COLS = (HY_ORDER + 1) * HY_CH
DIFF_HEADS = 4
DIFF_QK = DIFF_HEADS * 2 * HEAD_DIM
DIFF_V = DIFF_HEADS * 2 * HEAD_DIM
ONES_ROWS = 16
N_SCORE_HEADS = 8
LOG2E = 1.4426950408889634
Q_SCALE = HEAD_DIM ** -0.5 * LOG2E
GQA_VROWS = HEAD_DIM + ONES_ROWS
DIFF_VROWS = 2 * HEAD_DIM + ONES_ROWS


def _params(*sem):
    return pltpu.CompilerParams(dimension_semantics=sem, vmem_limit_bytes=VMEM_LIMIT)


def _const_spec(shape):
    nd = len(shape)
    return pl.BlockSpec(shape, lambda *_: (0,) * nd, pipeline_mode=pl.Buffered(1))


def _rms(x, g):
    return x * lax.rsqrt(jnp.mean(x * x, axis=-1, keepdims=True) + NORM_EPS) * g


def _mod_row(b, i):
    return jnp.where(i == 0, 4, b)


def _mods_kernel(cc_ref, w_ref, b_ref, o_ref):
    cc = cc_ref[...]
    s = cc / (1.0 + jnp.exp(-cc))
    o_ref[0] = jnp.dot(s.astype(BF16), w_ref[0].astype(BF16), preferred_element_type=F32) + b_ref[0]


def _mods(cc, w_ada, b_ada):
    depth, d, n = w_ada.shape
    nb = n // 4
    return pl.pallas_call(
        _mods_kernel,
        grid=(depth, n // nb),
        in_specs=[pl.BlockSpec((8, d), lambda l, j: (0, 0)),
                  pl.BlockSpec((1, d, nb), lambda l, j: (l, 0, j)),
                  pl.BlockSpec((1, 1, nb), lambda l, j: (l, 0, j))],
        out_specs=pl.BlockSpec((1, 8, nb), lambda l, j: (l, 0, j)),
        out_shape=jax.ShapeDtypeStruct((depth, 8, n), F32),
        compiler_params=_params("parallel", "parallel"),
        name="ada_mods",
    )(cc, w_ada, b_ada.reshape(depth, 1, n))


def _normed_input(xs_ref, mods_ref, g_ref):
    x = xs_ref[0]
    mods = mods_ref[0]
    h = _rms(x, g_ref[0:1]) * (1.0 + mods[1:2]) + mods[0:1]
    return h.astype(BF16)


def _rope_rows(y, cos, sin):
    x1, x2 = y[:32], y[32:]
    return x1 * cos - x2 * sin, x1 * sin + x2 * cos


def _log_sigmoid(x):
    return jnp.minimum(x, 0.0) - jnp.log(1.0 + jnp.exp(-jnp.abs(x)))


def _even_in_kernel(xs_ref, mods_ref, g_ref, wn_ref, wt_ref, wlr_ref, blr_ref, gq_ref, gk_ref,
                    cos_ref, sin_ref, qk_ref, la_ref, gv_ref, gg_ref, aq_ref, ak_ref, av_ref):
    h = _normed_input(xs_ref, mods_ref, g_ref)
    p = jnp.dot(h, wn_ref[...], preferred_element_type=F32)
    qk_ref[0, :, :GLA_QK] = p[:, :GLA_QK] * GLA_DK ** -0.5
    qk_ref[0, :, GLA_QK:] = p[:, GLA_QK:2 * GLA_QK]
    lr = p[:, 2 * GLA_QK:].astype(BF16)
    gk = jnp.dot(lr, wlr_ref[...], preferred_element_type=F32) + blr_ref[...]
    la_ref[0] = _log_sigmoid(gk) * (1.0 / GLA_NORMALIZER)

    pt_all = lax.dot_general(wt_ref[...], h, (((1,), (1,)), ((), ())), preferred_element_type=F32)
    gv_ref[0, 0] = pt_all[:GLA_V].astype(BF16)
    gg_ref[0, 0] = pt_all[GLA_V:2 * GLA_V]
    pt = pt_all[2 * GLA_V:]
    cos, sin = cos_ref[...], sin_ref[...]
    gq, gk_gain = gq_ref[...], gk_ref[...]
    for hh in range(GQA_HEADS):
        blk = pt[hh * 64:(hh + 1) * 64]
        y = blk * lax.rsqrt(jnp.mean(blk * blk, axis=0, keepdims=True) + NORM_EPS) * gq
        o1, o2 = _rope_rows(y, cos, sin)
        aq_ref[0, 0, hh * 64:hh * 64 + 32, :] = (o1 * Q_SCALE).astype(BF16)
        aq_ref[0, 0, hh * 64 + 32:hh * 64 + 64, :] = (o2 * Q_SCALE).astype(BF16)
    kparts = []
    for hh in range(GQA_KV_HEADS):
        blk = pt[512 + hh * 64:512 + (hh + 1) * 64]
        y = blk * lax.rsqrt(jnp.mean(blk * blk, axis=0, keepdims=True) + NORM_EPS) * gk_gain
        o1, o2 = _rope_rows(y, cos, sin)
        kparts += [o1, o2]
    kt = jnp.concatenate(kparts, axis=0)
    ak_ref[0] = kt.T.astype(BF16)
    ones = jnp.ones((ONES_ROWS, TM), BF16)
    for hh in range(GQA_KV_HEADS):
        av_ref[0, 0, hh * GQA_VROWS:hh * GQA_VROWS + 64, :] = pt[640 + hh * 64:640 + (hh + 1) * 64].astype(BF16)
        av_ref[0, 0, hh * GQA_VROWS + 64:(hh + 1) * GQA_VROWS, :] = ones


def _odd_in_kernel(xs_ref, mods_ref, g_ref, wn_ref, wt_ref, cos_ref, sin_ref,
                   u_ref, dq_ref, dk_ref, dv_ref):
    h = _normed_input(xs_ref, mods_ref, g_ref)
    u_ref[0] = jnp.dot(h, wn_ref[...], preferred_element_type=F32)
    pt = lax.dot_general(wt_ref[...], h, (((1,), (1,)), ((), ())), preferred_element_type=F32)
    cos, sin = cos_ref[...], sin_ref[...]
    kparts = []
    for hh in range(2 * DIFF_HEADS):
        o1, o2 = _rope_rows(pt[hh * 64:(hh + 1) * 64], cos, sin)
        dq_ref[0, 0, hh * 64:hh * 64 + 32, :] = (o1 * Q_SCALE).astype(BF16)
        dq_ref[0, 0, hh * 64 + 32:hh * 64 + 64, :] = (o2 * Q_SCALE).astype(BF16)
        k1, k2 = _rope_rows(pt[DIFF_QK + hh * 64:DIFF_QK + (hh + 1) * 64], cos, sin)
        kparts += [k1, k2]
    kt = jnp.concatenate(kparts, axis=0)
    dk_ref[0] = kt.T.astype(BF16)
    ones = jnp.ones((ONES_ROWS, TM), BF16)
    nv = 2 * HEAD_DIM
    for hh in range(DIFF_HEADS):
        dv_ref[0, 0, hh * DIFF_VROWS:hh * DIFF_VROWS + nv, :] = pt[2 * DIFF_QK + hh * nv:2 * DIFF_QK + (hh + 1) * nv].astype(BF16)
        dv_ref[0, 0, hh * DIFF_VROWS + nv:(hh + 1) * DIFF_VROWS, :] = ones


def _pair_perm(n_heads):
    base = np.concatenate([np.arange(0, HEAD_DIM, 2), np.arange(1, HEAD_DIM, 2)])
    return np.concatenate([h * HEAD_DIM + base for h in range(n_heads)])


def _even_in(xs, mods_l, g0, w_in, w_lr, b_lr, qk_g, cos_t, sin_t):
    b, t, d = xs.shape
    nt = t // TM
    c_v = 2 * GLA_QK
    c_g = c_v + GLA_V
    c_lr0 = c_g + GLA_V
    c_lr = c_lr0 + 2 * GLA_RANK
    wn = jnp.concatenate([w_in[:, :c_v], w_in[:, c_lr0:c_lr], jnp.zeros((d, 128 - 2 * GLA_RANK), F32)],
                         axis=1).astype(BF16)
    cq = c_lr + GQA_HEADS * HEAD_DIM
    ck = cq + GQA_KV_HEADS * HEAD_DIM
    wq = w_in[:, c_lr:cq][:, _pair_perm(GQA_HEADS)]
    wk = w_in[:, cq:ck][:, _pair_perm(GQA_KV_HEADS)]
    wt = jnp.concatenate([w_in[:, c_v:c_lr0], wq, wk, w_in[:, ck:]], axis=1).T.astype(BF16)
    wlr = jnp.zeros((128, 2 * GLA_QK), F32)
    wlr = wlr.at[:GLA_RANK, :GLA_QK].set(w_lr[0]).at[GLA_RANK:2 * GLA_RANK, GLA_QK:].set(w_lr[1]).astype(BF16)
    blr = b_lr.reshape(1, 2 * GLA_QK)
    perm = _pair_perm(1)
    gq = qk_g[0][perm].reshape(HEAD_DIM, 1)
    gk = qk_g[1][perm].reshape(HEAD_DIM, 1)
    tok = lambda n: pl.BlockSpec((1, TM, n), lambda bb, i: (bb, i, 0))
    tr = lambda n: pl.BlockSpec((1, 1, n, TM), lambda bb, i: (bb, i, 0, 0))
    return pl.pallas_call(
        _even_in_kernel,
        grid=(b, nt),
        in_specs=[tok(d),
                  pl.BlockSpec((1, N_MOD, d), lambda bb, i: (_mod_row(bb, i), 0, 0)),
                  _const_spec((1, d)),
                  _const_spec(wn.shape), _const_spec(wt.shape), _const_spec(wlr.shape), _const_spec(blr.shape),
                  _const_spec(gq.shape), _const_spec(gk.shape),
                  pl.BlockSpec((32, TM), lambda bb, i: (0, i)),
                  pl.BlockSpec((32, TM), lambda bb, i: (0, i))],
        out_specs=[tok(2 * GLA_QK), tok(2 * GLA_QK), tr(GLA_V), tr(GLA_V), tr(512), tok(128),
                   tr(GQA_KV_HEADS * GQA_VROWS)],
        out_shape=[jax.ShapeDtypeStruct((b, t, 2 * GLA_QK), F32),
                   jax.ShapeDtypeStruct((b, t, 2 * GLA_QK), F32),
                   jax.ShapeDtypeStruct((b, nt, GLA_V, TM), BF16),
                   jax.ShapeDtypeStruct((b, nt, GLA_V, TM), F32),
                   jax.ShapeDtypeStruct((b, nt, 512, TM), BF16),
                   jax.ShapeDtypeStruct((b, t, 128), BF16),
                   jax.ShapeDtypeStruct((b, nt, GQA_KV_HEADS * GQA_VROWS, TM), BF16)],
        compiler_params=_params("parallel", "parallel"),
        name="even_in",
    )(xs, mods_l, g0, wn, wt, wlr, blr, gq, gk, cos_t, sin_t)


def _odd_in(xs, mods_l, g0, w_in, cos_t, sin_t):
    b, t, d = xs.shape
    nt = t // TM
    wn = w_in[:, :HY_COLS].astype(BF16)
    perm = _pair_perm(2 * DIFF_HEADS)
    wq = w_in[:, HY_COLS:HY_COLS + DIFF_QK][:, perm]
    wk = w_in[:, HY_COLS + DIFF_QK:HY_COLS + 2 * DIFF_QK][:, perm]
    wt = jnp.concatenate([wq, wk, w_in[:, HY_COLS + 2 * DIFF_QK:]], axis=1).T.astype(BF16)
    tok = lambda n: pl.BlockSpec((1, TM, n), lambda bb, i: (bb, i, 0))
    tr = lambda n: pl.BlockSpec((1, 1, n, TM), lambda bb, i: (bb, i, 0, 0))
    return pl.pallas_call(
        _odd_in_kernel,
        grid=(b, nt),
        in_specs=[tok(d),
                  pl.BlockSpec((1, N_MOD, d), lambda bb, i: (_mod_row(bb, i), 0, 0)),
                  _const_spec((1, d)),
                  _const_spec(wn.shape), _const_spec(wt.shape),
                  pl.BlockSpec((32, TM), lambda bb, i: (0, i)),
                  pl.BlockSpec((32, TM), lambda bb, i: (0, i))],
        out_specs=[tok(HY_COLS), tr(DIFF_QK), tok(DIFF_QK), tr(DIFF_HEADS * DIFF_VROWS)],
        out_shape=[jax.ShapeDtypeStruct((b, t, HY_COLS), F32),
                   jax.ShapeDtypeStruct((b, nt, DIFF_QK, TM), BF16),
                   jax.ShapeDtypeStruct((b, t, DIFF_QK), BF16),
                   jax.ShapeDtypeStruct((b, nt, DIFF_HEADS * DIFF_VROWS, TM), BF16)],
        compiler_params=_params("parallel", "parallel"),
        name="odd_in",
    )(xs, mods_l, g0, wn, wt, cos_t, sin_t)


def _attn_core(q_ref, k_ref, v_ref, qpad_ref, m_ref, acc_ref, s_refs, c_refs, *, n_kv, k_col, q_slot, v_row, v_rows):
    i = pl.program_id(1)
    zeros = jnp.zeros((HEAD_DIM, TM), BF16)
    for hh in range(N_SCORE_HEADS):
        qh = q_ref[0, 0, hh * HEAD_DIM:(hh + 1) * HEAD_DIM, :]
        qpad_ref[hh] = jnp.concatenate([qh, zeros] if q_slot(hh) == 0 else [zeros, qh], axis=0)
    m_ref[...] = jnp.full(m_ref.shape, -jnp.inf, F32)
    acc_ref[...] = jnp.zeros(acc_ref.shape, F32)

    def scores(j, hh, slot):
        kc = k_ref[0, pl.ds(pl.multiple_of(j * TM, TM), TM), k_col(hh):k_col(hh) + 128]
        s = jnp.dot(kc, qpad_ref[hh], preferred_element_type=F32)
        s_refs[slot][hh] = s
        c_refs[slot][hh] = jnp.max(s, axis=0, keepdims=True)

    def consume(j, hh, slot):
        vc = v_ref[0, j, v_row(hh):v_row(hh) + v_rows, :]
        m_old = m_ref[hh]
        m_new = jnp.maximum(m_old, c_refs[slot][hh])
        p = jnp.exp2(s_refs[slot][hh] - m_new).astype(BF16)
        alpha = jnp.exp2(m_old - m_new)
        acc_ref[hh] = acc_ref[hh] * alpha + jnp.dot(vc, p, preferred_element_type=F32)
        m_ref[hh] = m_new

    for hh in range(N_SCORE_HEADS):
        scores(0, hh, 0)

    def body(jj, carry):
        j = 2 * jj
        for hh in range(N_SCORE_HEADS):
            consume(j, hh, 0)
            scores(j + 1, hh, 1)
        for hh in range(N_SCORE_HEADS):
            consume(j + 1, hh, 1)
            scores(j + 2, hh, 0)
        return carry

    n = jnp.where(i == 0, 1, n_kv)
    lax.fori_loop(0, (n - 1) // 2, body, 0)
    for hh in range(N_SCORE_HEADS):
        consume(n - 1, hh, 0)


def _attn_scratch(v_rows):
    return [pltpu.VMEM((N_SCORE_HEADS, 128, TM), BF16),
            pltpu.VMEM((N_SCORE_HEADS, 1, TM), F32),
            pltpu.VMEM((N_SCORE_HEADS, v_rows, TM), F32),
            pltpu.VMEM((N_SCORE_HEADS, TM, TM), F32), pltpu.VMEM((N_SCORE_HEADS, TM, TM), F32),
            pltpu.VMEM((N_SCORE_HEADS, 1, TM), F32), pltpu.VMEM((N_SCORE_HEADS, 1, TM), F32)]


def _gqa_kernel(q_ref, k_ref, v_ref, o_ref, qpad_ref, m_ref, acc_ref, s0_ref, s1_ref, c0_ref, c1_ref, *, n_kv):
    group = GQA_HEADS // GQA_KV_HEADS
    _attn_core(q_ref, k_ref, v_ref, qpad_ref, m_ref, acc_ref, (s0_ref, s1_ref), (c0_ref, c1_ref), n_kv=n_kv,
               k_col=lambda h: 0, q_slot=lambda h: h // group, v_row=lambda h: (h // group) * GQA_VROWS,
               v_rows=GQA_VROWS)
    for hh in range(GQA_HEADS):
        a = acc_ref[hh]
        o_ref[0, 0, hh * HEAD_DIM:(hh + 1) * HEAD_DIM, :] = (a[:HEAD_DIM] / a[HEAD_DIM:HEAD_DIM + 1]).astype(o_ref.dtype)


def _gqa_attention(aq, ak, av):
    b, nt, nq, _ = aq.shape
    t = nt * TM
    assert nt % 2 == 1
    return pl.pallas_call(
        functools.partial(_gqa_kernel, n_kv=nt),
        grid=(b, nt),
        in_specs=[pl.BlockSpec((1, 1, nq, TM), lambda bb, i: (bb, i, 0, 0)),
                  pl.BlockSpec((1, t, 128), lambda bb, i: (bb, 0, 0)),
                  pl.BlockSpec((1, nt, GQA_KV_HEADS * GQA_VROWS, TM), lambda bb, i: (bb, 0, 0, 0))],
        out_specs=pl.BlockSpec((1, 1, nq, TM), lambda bb, i: (bb, i, 0, 0)),
        out_shape=jax.ShapeDtypeStruct((b, nt, nq, TM), BF16),
        scratch_shapes=_attn_scratch(GQA_VROWS),
        compiler_params=_params("parallel", "arbitrary"),
        name="gqa_attn",
    )(aq, ak, av)


def _diff_kernel(q_ref, k_ref, v_ref, lam_ref, gain_ref, o_ref, qpad_ref, m_ref, acc_ref, s0_ref, s1_ref,
                 c0_ref, c1_ref, *, n_kv, lam_init):
    _attn_core(q_ref, k_ref, v_ref, qpad_ref, m_ref, acc_ref, (s0_ref, s1_ref), (c0_ref, c1_ref), n_kv=n_kv,
               k_col=lambda hc: (hc // 2) * 128, q_slot=lambda hc: hc % 2, v_row=lambda hc: (hc // 2) * DIFF_VROWS,
               v_rows=DIFF_VROWS)
    lp = lam_ref[...]
    lam = (jnp.exp(jnp.sum(lp[0:1] * lp[1:2], axis=1, keepdims=True))
           - jnp.exp(jnp.sum(lp[2:3] * lp[3:4], axis=1, keepdims=True)) + lam_init)
    gain = gain_ref[...]
    nv = 2 * HEAD_DIM
    for hh in range(DIFF_HEADS):
        a0, a1 = acc_ref[2 * hh], acc_ref[2 * hh + 1]
        o = a0[:nv] / a0[nv:nv + 1] - lam * (a1[:nv] / a1[nv:nv + 1])
        y = o * lax.rsqrt(jnp.mean(o * o, axis=0, keepdims=True) + NORM_EPS) * gain
        o_ref[0, 0, hh * nv:(hh + 1) * nv, :] = (y * (1.0 - lam_init)).astype(o_ref.dtype)


def _diff_attention(dq, dk, dv, lam_p, gain, layer_idx):
    b, nt, nq, _ = dq.shape
    t = nt * TM
    assert nt % 2 == 1
    lam_init = 0.8 - 0.6 * math.exp(-0.3 * layer_idx)
    return pl.pallas_call(
        functools.partial(_diff_kernel, n_kv=nt, lam_init=lam_init),
        grid=(b, nt),
        in_specs=[pl.BlockSpec((1, 1, nq, TM), lambda bb, i: (bb, i, 0, 0)),
                  pl.BlockSpec((1, t, DIFF_QK), lambda bb, i: (bb, 0, 0)),
                  pl.BlockSpec((1, nt, DIFF_HEADS * DIFF_VROWS, TM), lambda bb, i: (bb, 0, 0, 0)),
                  _const_spec(lam_p.shape),
                  _const_spec((2 * HEAD_DIM, 1))],
        out_specs=pl.BlockSpec((1, 1, DIFF_V, TM), lambda bb, i: (bb, i, 0, 0)),
        out_shape=jax.ShapeDtypeStruct((b, nt, DIFF_V, TM), BF16),
        scratch_shapes=_attn_scratch(DIFF_VROWS),
        compiler_params=_params("parallel", "arbitrary"),
        name="diff_attn",
    )(dq, dk, dv, lam_p, gain.reshape(2 * HEAD_DIM, 1))


def _split_bf16(x):
    hi = x.astype(BF16)
    return hi, (x - hi.astype(F32)).astype(BF16)


def _nt(a, b):
    return lax.dot_general(a, b, (((1,), (1,)), ((), ())), preferred_element_type=F32)


def _gla_kernel(qk_ref, la_ref, vt_ref, o_ref, st_ref, *, backward):
    @pl.when(pl.program_id(1) == 0)
    def _():
        st_ref[...] = jnp.zeros(st_ref.shape, F32)

    n_chunk = TM // GLA_CHUNK
    r = lax.broadcasted_iota(jnp.int32, (TM, TM), 0)
    c = lax.broadcasted_iota(jnp.int32, (TM, TM), 1)
    same = (r // GLA_CHUNK) == (c // GLA_CHUNK)
    causal = (c >= r) if backward else (c <= r)
    tri = jnp.where(same & causal, 1.0, 0.0).astype(BF16)
    blk = jnp.where(same, 1.0, 0.0).astype(BF16)
    la_hi, la_lo = _split_bf16(la_ref[0])
    cum = jnp.dot(tri, la_hi, preferred_element_type=F32) + jnp.dot(tri, la_lo, preferred_element_type=F32)
    tot = jnp.dot(blk, la_hi, preferred_element_type=F32) + jnp.dot(blk, la_lo, preferred_element_type=F32)
    q = qk_ref[0, :, :GLA_QK]
    k = qk_ref[0, :, GLA_QK:]
    q_in = (q * jnp.exp(cum)).astype(BF16)
    k_in = k * jnp.exp(-cum)
    k_end = k * jnp.exp(tot - cum)
    dec = jnp.exp(tot)
    lane = lax.broadcasted_iota(jnp.int32, (TM, 128), 1)
    row = lax.broadcasted_iota(jnp.int32, (TM, 128), 0)
    keep = same & ((r >= c) if backward else (r <= c))
    order = range(n_chunk - 1, -1, -1) if backward else range(n_chunk)
    for h in range(GLA_HEADS):
        pair = slice((h // 2) * 128, (h // 2 + 1) * 128)
        head_lanes = (lane // GLA_DK) == (h % 2)
        q_pair = q_in[:, pair]
        k_in_h = jnp.where(head_lanes, k_in[:, pair], 0.0).astype(BF16)
        k_end_h = jnp.where(head_lanes, k_end[:, pair], 0.0)
        vt = vt_ref[0, 0, h * GLA_DV:(h + 1) * GLA_DV, :]
        a_t = jnp.where(keep, _nt(k_in_h, q_pair), 0.0).astype(BF16)
        o_t = jnp.dot(vt, a_t, preferred_element_type=F32)
        s_t = st_ref[h]
        for ci in order:
            in_chunk = (row // GLA_CHUNK) == ci
            q_c = jnp.where(in_chunk, q_pair, jnp.zeros_like(q_pair))
            o_t = o_t + _nt(s_t.astype(BF16), q_c)
            k_c = jnp.where(in_chunk, k_end_h, 0.0).astype(BF16)
            s_t = s_t * dec[ci * GLA_CHUNK:ci * GLA_CHUNK + 1, pair] + jnp.dot(vt, k_c, preferred_element_type=F32)
        st_ref[h] = s_t
        o_ref[0, 0, h * GLA_DV:(h + 1) * GLA_DV, :] = o_t


def _gla_direction(qk, la, vt, backward):
    b, t, _ = qk.shape
    nt = t // TM
    d = 1 if backward else 0
    tile = (lambda i: jnp.where(i == 0, 0, nt - i)) if backward else (lambda i: i)
    return pl.pallas_call(
        functools.partial(_gla_kernel, backward=backward),
        grid=(b, nt),
        in_specs=[pl.BlockSpec((1, TM, 2 * GLA_QK), lambda bb, i: (bb, tile(i), 0)),
                  pl.BlockSpec((1, TM, GLA_QK), lambda bb, i: (bb, tile(i), d)),
                  pl.BlockSpec((1, 1, GLA_V, TM), lambda bb, i: (bb, tile(i), 0, 0))],
        out_specs=pl.BlockSpec((1, 1, GLA_V, TM), lambda bb, i: (bb, tile(i), 0, 0)),
        out_shape=jax.ShapeDtypeStruct((b, nt, GLA_V, TM), F32),
        scratch_shapes=[pltpu.VMEM((GLA_HEADS, GLA_DV, 128), F32)],
        compiler_params=_params("parallel", "arbitrary"),
        name="gla_bwd" if backward else "gla_fwd",
    )(qk, la, vt)


def _post_tail(x, mods, g, ya, yb_ref, woa_ref, wob_ref, w1_ref, w2_ref, o_ref, ff_chunk):
    yb = yb_ref[0, 0].astype(F32).T.astype(BF16)
    y = (jnp.dot(ya, woa_ref[...], preferred_element_type=F32)
         + jnp.dot(yb, wob_ref[...], preferred_element_type=F32))
    x1 = x + mods[2:3] * _rms(y, g[1:2])
    h = (_rms(x1, g[2:3]) * (1.0 + mods[4:5]) + mods[3:4]).astype(BF16)
    d_ff = w1_ref.shape[1]
    acc = jnp.zeros(x.shape, F32)
    for c in range(d_ff // ff_chunk):
        u = jnp.dot(h, w1_ref[:, c * ff_chunk:(c + 1) * ff_chunk], preferred_element_type=F32)
        u = jnp.square(jnp.maximum(u, 0.0)).astype(BF16)
        acc = acc + jnp.dot(u, w2_ref[c * ff_chunk:(c + 1) * ff_chunk, :], preferred_element_type=F32)
    o_ref[0] = x1 + mods[5:6] * _rms(acc, g[3:4])


def _post_gla_kernel(xs_ref, of_ref, ob_ref, gg_ref, gain_ref, yb_ref, mods_ref, g_ref, woa_ref, wob_ref, w1_ref,
                     w2_ref, o_ref, *, ff_chunk):
    gain = gain_ref[...]
    parts = []
    for hh in range(GLA_HEADS):
        rows = slice(hh * GLA_DV, (hh + 1) * GLA_DV)
        o = of_ref[0, 0, rows, :] + ob_ref[0, 0, rows, :]
        gt = gg_ref[0, 0, rows, :]
        y = o * lax.rsqrt(jnp.mean(o * o, axis=0, keepdims=True) + NORM_EPS) * gain
        parts.append(y * (gt / (1.0 + jnp.exp(-gt))))
    ya = jnp.concatenate(parts, axis=0).T.astype(BF16)
    _post_tail(xs_ref[0], mods_ref[0], g_ref[...], ya, yb_ref, woa_ref, wob_ref, w1_ref, w2_ref, o_ref, ff_chunk)


def _post_kernel(xs_ref, ya_ref, yb_ref, mods_ref, g_ref, woa_ref, wob_ref, w1_ref, w2_ref, o_ref, *, ff_chunk):
    _post_tail(xs_ref[0], mods_ref[0], g_ref[...], ya_ref[0].astype(BF16), yb_ref, woa_ref, wob_ref, w1_ref, w2_ref,
               o_ref, ff_chunk)


def _post(xs, ya, yb_t, mods_l, g, w_out, w1, w2, skip_ctx):
    b, t, d = xs.shape
    nt = t // TM
    half = w_out.shape[0] // 2
    woa = w_out[:half].astype(BF16)
    wob = w_out[half:].astype(BF16)
    w1 = w1.astype(BF16)
    w2 = w2.astype(BF16)
    off = 1 if skip_ctx else 0
    tok = lambda n: pl.BlockSpec((1, TM, n), lambda bb, i: (bb, i + off, 0))
    tr = lambda n: pl.BlockSpec((1, 1, n, TM), lambda bb, i: (bb, i + off, 0, 0))
    if isinstance(ya, tuple):
        o_f, o_b, gg, gain = ya
        body, name = _post_gla_kernel, "post_gla"
        a_specs = [tr(half), tr(half), tr(half), _const_spec((GLA_DV, 1))]
        a_args = [o_f, o_b, gg, gain.reshape(GLA_DV, 1)]
    else:
        body, name = _post_kernel, "post"
        a_specs, a_args = [tok(half)], [ya]
    return pl.pallas_call(
        functools.partial(body, ff_chunk=1024),
        grid=(b, nt - off),
        in_specs=[tok(d)] + a_specs + [
                  tr(half),
                  pl.BlockSpec((1, N_MOD, d), lambda bb, i: (_mod_row(bb, i + off), 0, 0)),
                  _const_spec(g.shape),
                  _const_spec(woa.shape), _const_spec(wob.shape), _const_spec(w1.shape), _const_spec(w2.shape)],
        out_specs=pl.BlockSpec((1, TM, d), lambda bb, i: (bb, i, 0)),
        out_shape=jax.ShapeDtypeStruct((b, t - off * TM, d), F32),
        compiler_params=_params("parallel", "parallel"),
        name=name,
    )(xs, *a_args, yb_t, mods_l, g, woa, wob, w1, w2)


def _short_conv_kernel(u_ref, prev_ref, next_ref, w_ref, b_ref, v_ref, x1_ref, x2_ref, *, n_tiles):
    i = pl.program_id(1)
    u = u_ref[0]
    rows = lax.broadcasted_iota(jnp.int32, u.shape, 0)
    prev_row = jnp.where(i == 0, 0.0, prev_ref[0, 7:8, :])
    next_row = jnp.where(i == n_tiles - 1, 0.0, next_ref[0, 0:1, :])
    up = jnp.where(rows == 0, prev_row, pltpu.roll(u, 1, axis=0))
    dn = jnp.where(rows == TM - 1, next_row, pltpu.roll(u, TM - 1, axis=0))
    w = w_ref[...]
    s = up * w[0:1] + u * w[1:2] + dn * w[2:3] + b_ref[...]
    v_ref[0] = s[:, :HY_CH]
    x1_ref[0] = s[:, HY_CH:2 * HY_CH]
    x2_ref[0] = s[:, 2 * HY_CH:]


def _short_conv(u, conv_w, conv_b, first_tile, n_tiles):
    b, t, c = u.shape
    r8 = TM // 8
    last_blk = t // 8 - 1
    out = jax.ShapeDtypeStruct((b, n_tiles * TM, HY_CH), F32)
    ospec = pl.BlockSpec((1, TM, HY_CH), lambda bb, i: (bb, i, 0))
    return pl.pallas_call(
        functools.partial(_short_conv_kernel, n_tiles=n_tiles),
        grid=(b, n_tiles),
        in_specs=[pl.BlockSpec((1, TM, c), lambda bb, i: (bb, i + first_tile, 0)),
                  pl.BlockSpec((1, 8, c), lambda bb, i: (bb, jnp.maximum((i + first_tile) * r8 - 1, 0), 0)),
                  pl.BlockSpec((1, 8, c), lambda bb, i: (bb, jnp.minimum((i + first_tile + 1) * r8, last_blk), 0)),
                  _const_spec(conv_w.shape), _const_spec((1, c))],
        out_specs=[ospec, ospec, ospec],
        out_shape=[out, out, out],
        compiler_params=_params("parallel", "parallel"),
        name="hy_short",
    )(u, u, u, conv_w, conv_b.reshape(1, c))


def _dot3(a, b):
    ah, al = _split_bf16(a)
    bh, bl = _split_bf16(b)
    d = lambda x, y: jnp.dot(x, y, preferred_element_type=F32)
    return d(ah, bh) + (d(al, bh) + d(ah, bl))


def _filter_kernel(w1t_ref, w1c_ref, w1s_ref, b1_ref, w2_ref, b2_ref, w3_ref, delta_ref, o_ref, *, seq, rb):
    n = pl.program_id(0) * rb + lax.broadcasted_iota(jnp.int32, (rb, 1), 0)
    d = jnp.where(n < seq, n, 2 * seq - n).astype(F32)
    t = d / max(seq - 1, 1)
    w = (2 * math.pi / seq) * d
    band = lax.broadcasted_iota(jnp.int32, (1, HY_BANDS), 1).astype(F32)
    bands = 1e-4 + band * ((HY_BANDS - 1 - 1e-4) / (HY_BANDS - 1))
    ang = w * bands
    pre = t * w1t_ref[...] + _dot3(jnp.cos(ang), w1c_ref[...]) - _dot3(jnp.sin(ang), w1s_ref[...]) + b1_ref[...]
    hid = jnp.sin(pre)
    hid = jnp.sin(_dot3(hid, w2_ref[...]) + b2_ref[...])
    decay = jnp.where(n == seq, 0.0, jnp.exp(-t * delta_ref[...]))
    for o in range(HY_ORDER):
        o_ref[o] = (_dot3(hid, w3_ref[0, o]) * decay).astype(o_ref.dtype)


def _hyena_kernels(seq, f_w1, f_b1, f_w2, f_b2, f_w3):
    rb = min(512, seq)
    nblk = 2 * seq // rb
    ffn = f_w2.shape[0]
    w3 = f_w3.reshape(ffn, HY_ORDER, 2, HY_CH).transpose(2, 1, 0, 3)
    deltas = jnp.abs(jnp.linspace(math.log(HY_TARGET) / HY_FAST_PCT, math.log(HY_TARGET) / HY_SLOW_PCT,
                                  HY_CH, dtype=F32)).reshape(1, HY_CH)
    return pl.pallas_call(
        functools.partial(_filter_kernel, seq=seq, rb=rb),
        grid=(nblk,),
        in_specs=[_const_spec((1, ffn)), _const_spec((HY_BANDS, ffn)), _const_spec((HY_BANDS, ffn)),
                  _const_spec((1, ffn)), _const_spec((ffn, ffn)), _const_spec((1, ffn)),
                  pl.BlockSpec((1, HY_ORDER, ffn, HY_CH), lambda j: ((2 * j) // nblk, 0, 0, 0)),
                  _const_spec((1, HY_CH))],
        out_specs=pl.BlockSpec((HY_ORDER, rb, HY_CH), lambda j: (0, j, 0)),
        out_shape=jax.ShapeDtypeStruct((HY_ORDER, 2 * seq, HY_CH), BF16),
        compiler_params=_params("parallel"),
        name="hy_filter",
    )(f_w1[0:1], f_w1[1:1 + HY_BANDS], f_w1[1 + HY_BANDS:], f_b1.reshape(1, ffn), f_w2, f_b2.reshape(1, ffn), w3,
      deltas)


def _dft_tables(seq):
    n = 2 * seq
    r = math.isqrt(n)
    assert r * r == n
    idx = jnp.arange(r, dtype=jnp.int32)
    phi = (2 * math.pi / r) * ((idx[:, None] * idx[None, :]) % r).astype(F32)
    f1 = jnp.concatenate([jnp.cos(phi), -jnp.sin(phi)], axis=0).astype(BF16)
    m = (idx[:, None, None] + r * idx[None, :, None]) * idx[None, None, :]
    th = (2 * math.pi / n) * (m % n).astype(F32)
    c, s = jnp.cos(th), jnp.sin(th)
    blockm = lambda re, im: jnp.concatenate([jnp.concatenate([re, -im], axis=-1),
                                             jnp.concatenate([im, re], axis=-1)], axis=-2)
    g = blockm(c, -s).astype(BF16)
    ginv = blockm(jnp.swapaxes(c, 1, 2), jnp.swapaxes(s, 1, 2)).astype(BF16)
    finv = (jnp.concatenate([jnp.cos(phi.T), -jnp.sin(phi.T)], axis=1)[:r // 2] / n).astype(BF16)
    return r, f1, g, ginv, finv


DFT_BB = 8


def _dft1_kernel(f_ref, x_ref, o_ref):
    f = f_ref[...]
    r = o_ref.shape[2]
    x = pltpu.einshape("abc->bac", x_ref[0].astype(F32))
    ys = [jnp.dot(f, x[j].astype(BF16), preferred_element_type=F32) for j in range(DFT_BB)]
    y = pltpu.einshape("jkc->kjc", jnp.stack(ys, axis=0))
    o_ref[0, 0] = y[:r].astype(o_ref.dtype)
    o_ref[0, 1] = y[r:].astype(o_ref.dtype)


def _dft_stage1(f, x4):
    b, ka, r, c = x4.shape
    return pl.pallas_call(
        _dft1_kernel,
        grid=(b, r // DFT_BB),
        in_specs=[_const_spec(f.shape), pl.BlockSpec((1, ka, DFT_BB, c), lambda bi, j: (bi, 0, j, 0))],
        out_specs=pl.BlockSpec((1, 2, r, DFT_BB, c), lambda bi, j: (bi, 0, 0, j, 0)),
        out_shape=jax.ShapeDtypeStruct((b, 2, r, r, c), BF16),
        compiler_params=_params("parallel", "parallel"),
        name="hy_dft1",
    )(f, x4)


def _fspec_kernel(a_ref, g_ref, o_ref, *, kb, r):
    for kk in range(kb):
        a = a_ref[0, :, kk].reshape(2 * r, a_ref.shape[-1])
        o_ref[0, kk] = jnp.dot(g_ref[kk], a, preferred_element_type=F32)


def _filter_spectrum(a, g, kb):
    o, _, r, _, c = a.shape
    return pl.pallas_call(
        functools.partial(_fspec_kernel, kb=kb, r=r),
        grid=(o, r // kb),
        in_specs=[pl.BlockSpec((1, 2, kb, r, c), lambda oo, j: (oo, 0, j, 0, 0)),
                  pl.BlockSpec((kb, 2 * r, 2 * r), lambda oo, j: (j, 0, 0))],
        out_specs=pl.BlockSpec((1, kb, 2 * r, c), lambda oo, j: (oo, j, 0, 0)),
        out_shape=jax.ShapeDtypeStruct((o, r, 2 * r, c), F32),
        compiler_params=_params("parallel", "parallel"),
        name="hy_fspec",
    )(a, g)


def _dft2_kernel(a_ref, g_ref, gi_ref, k_ref, o_ref, *, kb, r):
    for kk in range(kb):
        a = a_ref[0, :, kk].reshape(2 * r, a_ref.shape[-1])
        x = jnp.dot(g_ref[kk], a, preferred_element_type=F32)
        ks = k_ref[0, kk]
        xr, xi, kr, ki = x[:r], x[r:], ks[:r], ks[r:]
        y = jnp.concatenate([xr * kr - xi * ki, xr * ki + xi * kr], axis=0).astype(BF16)
        bq = jnp.dot(gi_ref[kk], y, preferred_element_type=F32)
        o_ref[0, 0, kk] = bq[:r].astype(o_ref.dtype)
        o_ref[0, 1, kk] = bq[r:].astype(o_ref.dtype)


def _dft_stage2(a, g, ginv, kspec, order, kb):
    b, _, r, _, c = a.shape
    blk = pl.BlockSpec((1, 2, kb, r, c), lambda j, bb: (bb, 0, j, 0, 0))
    tab = pl.BlockSpec((kb, 2 * r, 2 * r), lambda j, bb: (j, 0, 0))
    return pl.pallas_call(
        functools.partial(_dft2_kernel, kb=kb, r=r),
        grid=(r // kb, b),
        in_specs=[blk, tab, tab, pl.BlockSpec((1, kb, 2 * r, c), lambda j, bb: (order, j, 0, 0))],
        out_specs=blk,
        out_shape=jax.ShapeDtypeStruct(a.shape, BF16),
        compiler_params=_params("parallel", "arbitrary"),
        name="hy_dft2",
    )(a, g, ginv, kspec)


def _dft3_kernel(f_ref, b_ref, gate_ref, z_ref, bias_ref, o_ref):
    f = f_ref[...]
    bq = jnp.concatenate([b_ref[0, 0], b_ref[0, 1]], axis=0)
    bq = pltpu.einshape("kjc->jkc", bq.astype(F32))
    ys = [jnp.dot(f, bq[j].astype(BF16), preferred_element_type=F32) for j in range(DFT_BB)]
    y = pltpu.einshape("jac->ajc", jnp.stack(ys, axis=0))
    o_ref[0] = gate_ref[0] * (y + z_ref[0] * bias_ref[...])


def _dft_stage3(finv, bq5, gate4, z4, bias):
    b, _, r, _, c = bq5.shape
    ra = finv.shape[0]
    io = pl.BlockSpec((1, ra, DFT_BB, c), lambda bi, j: (bi, 0, j, 0))
    return pl.pallas_call(
        _dft3_kernel,
        grid=(b, r // DFT_BB),
        in_specs=[_const_spec(finv.shape), pl.BlockSpec((1, 2, r, DFT_BB, c), lambda bi, j: (bi, 0, 0, j, 0)), io, io,
                  _const_spec((1, 1, c))],
        out_specs=io,
        out_shape=jax.ShapeDtypeStruct((b, ra, r, c), F32),
        compiler_params=_params("parallel", "parallel"),
        name="hy_dft3",
    )(finv, bq5, gate4, z4, bias)


def _long_conv_gated(v, gates, kern, hy_bias):
    b, seq, c = v.shape
    r, f1, g, ginv, finv = _dft_tables(seq)
    kb = 8
    kspec = _filter_spectrum(_dft_stage1(f1, kern.reshape(HY_ORDER, r, r, c)), g, kb)
    z = v.reshape(b, r // 2, r, c)
    for o in range(HY_ORDER):
        a = _dft_stage1(f1[:, :r // 2], z)
        bq = _dft_stage2(a, g, ginv, kspec, o, kb)
        z = _dft_stage3(finv, bq, gates[o].reshape(b, r // 2, r, c), z, hy_bias[o].reshape(1, 1, c))
    return z.reshape(b, seq, c)


def _ctx_conv_kernel(v_ref, x1_ref, x2_ref, k_ref, ff_ref, fi_ref, bias_ref, o_ref):
    n = ff_ref.shape[1]
    z = v_ref[0]
    for o, gate_ref in enumerate((x1_ref, x2_ref)):
        ks = jnp.dot(ff_ref[...], k_ref[o], preferred_element_type=F32)
        x = jnp.dot(ff_ref[:, :TM], z.astype(BF16), preferred_element_type=F32)
        xr, xi, kr, ki = x[:n], x[n:], ks[:n], ks[n:]
        y = jnp.concatenate([xr * kr - xi * ki, xr * ki + xi * kr], axis=0).astype(BF16)
        conv = jnp.dot(fi_ref[...], y, preferred_element_type=F32)
        z = gate_ref[0] * (conv + z * bias_ref[o:o + 1])
    o_ref[0] = z


def _ctx_conv_gated(v, x1, x2, kern, hy_bias):
    b, seq, c = v.shape
    n = 2 * seq
    idx = jnp.arange(n, dtype=jnp.int32)
    th = (2 * math.pi / n) * ((idx[:, None] * idx[None, :]) % n).astype(F32)
    ff = jnp.concatenate([jnp.cos(th), -jnp.sin(th)], axis=0).astype(BF16)
    fi = (jnp.concatenate([jnp.cos(th), -jnp.sin(th)], axis=1)[:seq] / n).astype(BF16)
    io = pl.BlockSpec((1, seq, c), lambda bb: (bb, 0, 0))
    return pl.pallas_call(
        _ctx_conv_kernel,
        grid=(b,),
        in_specs=[io, io, io, _const_spec(kern.shape), _const_spec(ff.shape), _const_spec(fi.shape),
                  _const_spec(hy_bias.shape)],
        out_specs=io,
        out_shape=jax.ShapeDtypeStruct((b, seq, c), F32),
        compiler_params=_params("parallel"),
        name="hy_ctx",
    )(v, x1, x2, kern, ff, fi, hy_bias)


def _hyena(u, lc, conv_w, conv_b, f_w1, f_b1, f_w2, f_b2, f_w3, hy_bias):
    seq = u.shape[1] - lc
    vc, x1c, x2c = _short_conv(u, conv_w, conv_b, 0, lc // TM)
    vx, x1x, x2x = _short_conv(u, conv_w, conv_b, lc // TM, seq // TM)
    yc = _ctx_conv_gated(vc, x1c, x2c, _hyena_kernels(lc, f_w1, f_b1, f_w2, f_b2, f_w3), hy_bias)
    yx = _long_conv_gated(vx, (x1x, x2x), _hyena_kernels(seq, f_w1, f_b1, f_w2, f_b2, f_w3), hy_bias)
    return jnp.concatenate([yc, yx], axis=1)


def _rope_tables_t(seq, lc):
    rows = seq // GRID_W
    r = jnp.broadcast_to(jnp.arange(rows, dtype=F32)[:, None], (rows, GRID_W)).reshape(-1)
    col = jnp.broadcast_to(jnp.arange(GRID_W, dtype=F32)[None, :], (rows, GRID_W)).reshape(-1)
    n_pairs = HEAD_DIM // 4
    inv = ROPE_THETA ** (-jnp.arange(n_pairs, dtype=F32) / n_pairs)
    ang = jnp.concatenate([r[:, None] * inv, col[:, None] * inv], axis=-1)
    cos = jnp.concatenate([jnp.ones((lc, 2 * n_pairs), F32), jnp.cos(ang)], axis=0)
    sin = jnp.concatenate([jnp.zeros((lc, 2 * n_pairs), F32), jnp.sin(ang)], axis=0)
    return cos.T, sin.T


def kernel(x, c, ctx, c_ctx, w_ada, b_ada, norm_g, w_mlp_in, w_mlp_out, ev_w_in, ev_w_lr, ev_b_lr, ev_gla_g,
           ev_qk_g, ev_w_out, od_w_in, od_conv_w, od_conv_b, od_f_w1, od_f_b1, od_f_w2, od_f_b2, od_f_w3,
           od_hy_bias, od_lam, od_diff_g, od_w_out):
    bsz, seq, d = x.shape
    lc = ctx.shape[1]
    depth = w_ada.shape[0]
    assert lc == TM and seq % TM == 0 and bsz <= 4
    cos_t, sin_t = _rope_tables_t(seq, lc)
    cc = jnp.zeros((8, d), F32).at[:bsz].set(c).at[4].set(c_ctx)
    mods = _mods(cc, w_ada, b_ada).reshape(depth, 8, N_MOD, d)
    xs = jnp.concatenate([ctx, x], axis=1)
    for i in range(depth):
        last = i == depth - 1
        j = i // 2
        g = norm_g[i]
        if i % 2 == 0:
            qk, la, gv, gg, aq, ak, av = _even_in(xs, mods[i], g[0:1], ev_w_in[j], ev_w_lr[j], ev_b_lr[j],
                                                  ev_qk_g[j], cos_t, sin_t)
            ya = (_gla_direction(qk, la, gv, False), _gla_direction(qk, la, gv, True), gg, ev_gla_g[j])
            yb = _gqa_attention(aq, ak, av)
            w_out = ev_w_out[j]
        else:
            u, dq, dk, dv = _odd_in(xs, mods[i], g[0:1], od_w_in[j], cos_t, sin_t)
            ya = _hyena(u, lc, od_conv_w[j], od_conv_b[j], od_f_w1[j], od_f_b1[j], od_f_w2[j], od_f_b2[j],
                        od_f_w3[j], od_hy_bias[j])
            yb = _diff_attention(dq, dk, dv, od_lam[j], od_diff_g[j], i)
            w_out = od_w_out[j]
        xs = _post(xs, ya, yb, mods[i], g, w_out, w_mlp_in[i], w_mlp_out[i], skip_ctx=last)
    return xs
```

```python
import functools
import math

import jax
import jax.numpy as jnp
import numpy as np
from jax import lax
from jax.experimental import pallas as pl
from jax.experimental.pallas import tpu as pltpu

F32 = jnp.float32
BF16 = jnp.bfloat16

TM = 256
GRID_W = 64
HEAD_DIM = 64
ROPE_THETA = 10000.0
NORM_EPS = 1e-6
N_MOD = 6
VMEM_LIMIT = 56 * 1024 * 1024

GLA_HEADS = 4
GLA_DK = 64
GLA_DV = 128
GLA_RANK = 16
GLA_NORMALIZER = 16.0
GLA_CHUNK = 64
GLA_QK = GLA_HEADS * GLA_DK
GLA_V = GLA_HEADS * GLA_DV
GQA_HEADS = 8
GQA_KV_HEADS = 2
HY_CH = 512
HY_ORDER = 2
HY_BANDS = 16
HY_TARGET = 1e-2
HY_FAST_PCT = 0.3
HY_SLOW_PCT = 1.5
HY_COLS = (HY_ORDER + 1) * HY_CH
DIFF_HEADS = 4
DIFF_QK = DIFF_HEADS * 2 * HEAD_DIM
DIFF_V = DIFF_HEADS * 2 * HEAD_DIM
ONES_ROWS = 16
N_SCORE_HEADS = 8
LOG2E = 1.4426950408889634
Q_SCALE = HEAD_DIM ** -0.5 * LOG2E
GQA_VROWS = HEAD_DIM + ONES_ROWS
DIFF_VROWS = 2 * HEAD_DIM + ONES_ROWS


def _params(*sem):
    return pltpu.CompilerParams(dimension_semantics=sem, vmem_limit_bytes=VMEM_LIMIT)


def _const_spec(shape):
    nd = len(shape)
    return pl.BlockSpec(shape, lambda *_: (0,) * nd, pipeline_mode=pl.Buffered(1))


def _rms(x, g):
    return x * lax.rsqrt(jnp.mean(x * x, axis=-1, keepdims=True) + NORM_EPS) * g


MOD_CTX_ROW = 4
POST_PB = 2
IN_PB = 2


def _mod_block(bb, i, pb):
    return jnp.where(i == 0, MOD_CTX_ROW // pb, bb)


def _mods_kernel(cc_ref, w_ref, b_ref, o_ref):
    cc = cc_ref[...]
    s = cc / (1.0 + jnp.exp(-cc))
    o_ref[0] = jnp.dot(s.astype(BF16), w_ref[0].astype(BF16), preferred_element_type=F32) + b_ref[0]


def _mods(cc, w_ada, b_ada):
    depth, d, n = w_ada.shape
    nb = n // 4
    return pl.pallas_call(
        _mods_kernel,
        grid=(depth, n // nb),
        in_specs=[pl.BlockSpec((8, d), lambda l, j: (0, 0)),
                  pl.BlockSpec((1, d, nb), lambda l, j: (l, 0, j)),
                  pl.BlockSpec((1, 1, nb), lambda l, j: (l, 0, j))],
        out_specs=pl.BlockSpec((1, 8, nb), lambda l, j: (l, 0, j)),
        out_shape=jax.ShapeDtypeStruct((depth, 8, n), F32),
        compiler_params=_params("parallel", "parallel"),
        name="ada_mods",
    )(cc, w_ada, b_ada.reshape(depth, 1, n))


def _normed_input(x, mods, g):
    return (_rms(x, g) * (1.0 + mods[1:2]) + mods[0:1]).astype(BF16)


def _rope_rows(y, cos, sin):
    x1, x2 = y[:32], y[32:]
    return x1 * cos - x2 * sin, x1 * sin + x2 * cos


def _log_sigmoid(x):
    return jnp.minimum(x, 0.0) - jnp.log(1.0 + jnp.exp(-jnp.abs(x)))


def _nt(a, b):
    return lax.dot_general(a, b, (((1,), (1,)), ((), ())), preferred_element_type=F32)


def _even_in_kernel(xs_ref, mods_ref, g_ref, wn_ref, wt_ref, wlr_ref, blr_ref, gq_ref, gk_ref,
                    cos_ref, sin_ref, qk_ref, la_ref, gv_ref, gg_ref, aq_ref, ak_ref, av_ref):
    hs = [_normed_input(xs_ref[s], mods_ref[s], g_ref[0:1]) for s in range(IN_PB)]
    ps = [jnp.dot(h, wn_ref[...], preferred_element_type=F32) for h in hs]
    pts = [_nt(wt_ref[...], h) for h in hs]
    cos, sin = cos_ref[...], sin_ref[...]
    gq, gk_gain = gq_ref[...], gk_ref[...]
    ones = jnp.ones((ONES_ROWS, TM), BF16)
    for s in range(IN_PB):
        p, pt_all = ps[s], pts[s]
        qk_ref[s, :, :GLA_QK] = p[:, :GLA_QK] * GLA_DK ** -0.5
        qk_ref[s, :, GLA_QK:] = p[:, GLA_QK:2 * GLA_QK]
        lr = p[:, 2 * GLA_QK:].astype(BF16)
        gk = jnp.dot(lr, wlr_ref[...], preferred_element_type=F32) + blr_ref[...]
        la_ref[s] = _log_sigmoid(gk) * (1.0 / GLA_NORMALIZER)
        gv_ref[s, 0] = pt_all[:GLA_V].astype(BF16)
        gg_ref[s, 0] = pt_all[GLA_V:2 * GLA_V]
        pt = pt_all[2 * GLA_V:]
        for hh in range(GQA_HEADS):
            blk = pt[hh * 64:(hh + 1) * 64]
            y = blk * lax.rsqrt(jnp.mean(blk * blk, axis=0, keepdims=True) + NORM_EPS) * gq
            o1, o2 = _rope_rows(y, cos, sin)
            aq_ref[s, 0, hh * 64:hh * 64 + 32, :] = (o1 * Q_SCALE).astype(BF16)
            aq_ref[s, 0, hh * 64 + 32:hh * 64 + 64, :] = (o2 * Q_SCALE).astype(BF16)
        kparts = []
        for hh in range(GQA_KV_HEADS):
            blk = pt[512 + hh * 64:512 + (hh + 1) * 64]
            y = blk * lax.rsqrt(jnp.mean(blk * blk, axis=0, keepdims=True) + NORM_EPS) * gk_gain
            o1, o2 = _rope_rows(y, cos, sin)
            kparts += [o1, o2]
        ak_ref[s] = jnp.concatenate(kparts, axis=0).T.astype(BF16)
        for hh in range(GQA_KV_HEADS):
            av_ref[s, 0, hh * GQA_VROWS:hh * GQA_VROWS + 64, :] = pt[640 + hh * 64:640 + (hh + 1) * 64].astype(BF16)
            av_ref[s, 0, hh * GQA_VROWS + 64:(hh + 1) * GQA_VROWS, :] = ones


def _odd_in_kernel(xs_ref, mods_ref, g_ref, wn_ref, wt_ref, cos_ref, sin_ref,
                   u_ref, dq_ref, dk_ref, dv_ref):
    hs = [_normed_input(xs_ref[s], mods_ref[s], g_ref[0:1]) for s in range(IN_PB)]
    us = [jnp.dot(h, wn_ref[...], preferred_element_type=F32) for h in hs]
    pts = [_nt(wt_ref[...], h) for h in hs]
    cos, sin = cos_ref[...], sin_ref[...]
    ones = jnp.ones((ONES_ROWS, TM), BF16)
    nv = 2 * HEAD_DIM
    for s in range(IN_PB):
        pt = pts[s]
        u_ref[s] = us[s]
        kparts = []
        for hh in range(2 * DIFF_HEADS):
            o1, o2 = _rope_rows(pt[hh * 64:(hh + 1) * 64], cos, sin)
            dq_ref[s, 0, hh * 64:hh * 64 + 32, :] = (o1 * Q_SCALE).astype(BF16)
            dq_ref[s, 0, hh * 64 + 32:hh * 64 + 64, :] = (o2 * Q_SCALE).astype(BF16)
            k1, k2 = _rope_rows(pt[DIFF_QK + hh * 64:DIFF_QK + (hh + 1) * 64], cos, sin)
            kparts += [k1, k2]
        dk_ref[s] = jnp.concatenate(kparts, axis=0).T.astype(BF16)
        for hh in range(DIFF_HEADS):
            rows = pt[2 * DIFF_QK + hh * nv:2 * DIFF_QK + (hh + 1) * nv]
            dv_ref[s, 0, hh * DIFF_VROWS:hh * DIFF_VROWS + nv, :] = rows.astype(BF16)
            dv_ref[s, 0, hh * DIFF_VROWS + nv:(hh + 1) * DIFF_VROWS, :] = ones


def _pair_perm(n_heads):
    base = np.concatenate([np.arange(0, HEAD_DIM, 2), np.arange(1, HEAD_DIM, 2)])
    return np.concatenate([h * HEAD_DIM + base for h in range(n_heads)])


def _even_in(xs, mods_l, g0, w_in, w_lr, b_lr, qk_g, cos_t, sin_t):
    b, t, d = xs.shape
    nt = t // TM
    c_v = 2 * GLA_QK
    c_g = c_v + GLA_V
    c_lr0 = c_g + GLA_V
    c_lr = c_lr0 + 2 * GLA_RANK
    wn = jnp.concatenate([w_in[:, :c_v], w_in[:, c_lr0:c_lr], jnp.zeros((d, 128 - 2 * GLA_RANK), F32)],
                         axis=1).astype(BF16)
    cq = c_lr + GQA_HEADS * HEAD_DIM
    ck = cq + GQA_KV_HEADS * HEAD_DIM
    wq = w_in[:, c_lr:cq][:, _pair_perm(GQA_HEADS)]
    wk = w_in[:, cq:ck][:, _pair_perm(GQA_KV_HEADS)]
    wt = jnp.concatenate([w_in[:, c_v:c_lr0], wq, wk, w_in[:, ck:]], axis=1).T.astype(BF16)
    wlr = jnp.zeros((128, 2 * GLA_QK), F32)
    wlr = wlr.at[:GLA_RANK, :GLA_QK].set(w_lr[0]).at[GLA_RANK:2 * GLA_RANK, GLA_QK:].set(w_lr[1]).astype(BF16)
    blr = b_lr.reshape(1, 2 * GLA_QK)
    perm = _pair_perm(1)
    gq = qk_g[0][perm].reshape(HEAD_DIM, 1)
    gk = qk_g[1][perm].reshape(HEAD_DIM, 1)
    tok = lambda n: pl.BlockSpec((IN_PB, TM, n), lambda bb, i: (bb, i, 0))
    tr = lambda n: pl.BlockSpec((IN_PB, 1, n, TM), lambda bb, i: (bb, i, 0, 0))
    return pl.pallas_call(
        _even_in_kernel,
        grid=(b // IN_PB, nt),
        in_specs=[tok(d),
                  pl.BlockSpec((IN_PB, N_MOD, d), lambda bb, i: (_mod_block(bb, i, IN_PB), 0, 0)),
                  _const_spec((1, d)),
                  _const_spec(wn.shape), _const_spec(wt.shape), _const_spec(wlr.shape), _const_spec(blr.shape),
                  _const_spec(gq.shape), _const_spec(gk.shape),
                  pl.BlockSpec((32, TM), lambda bb, i: (0, i)),
                  pl.BlockSpec((32, TM), lambda bb, i: (0, i))],
        out_specs=[tok(2 * GLA_QK), tok(2 * GLA_QK), tr(GLA_V), tr(GLA_V), tr(512), tok(128),
                   tr(GQA_KV_HEADS * GQA_VROWS)],
        out_shape=[jax.ShapeDtypeStruct((b, t, 2 * GLA_QK), F32),
                   jax.ShapeDtypeStruct((b, t, 2 * GLA_QK), F32),
                   jax.ShapeDtypeStruct((b, nt, GLA_V, TM), BF16),
                   jax.ShapeDtypeStruct((b, nt, GLA_V, TM), F32),
                   jax.ShapeDtypeStruct((b, nt, 512, TM), BF16),
                   jax.ShapeDtypeStruct((b, t, 128), BF16),
                   jax.ShapeDtypeStruct((b, nt, GQA_KV_HEADS * GQA_VROWS, TM), BF16)],
        compiler_params=_params("parallel", "parallel"),
        name="even_in",
    )(xs, mods_l, g0, wn, wt, wlr, blr, gq, gk, cos_t, sin_t)


def _odd_in(xs, mods_l, g0, w_in, cos_t, sin_t):
    b, t, d = xs.shape
    nt = t // TM
    wn = w_in[:, :HY_COLS].astype(BF16)
    perm = _pair_perm(2 * DIFF_HEADS)
    wq = w_in[:, HY_COLS:HY_COLS + DIFF_QK][:, perm]
    wk = w_in[:, HY_COLS + DIFF_QK:HY_COLS + 2 * DIFF_QK][:, perm]
    wt = jnp.concatenate([wq, wk, w_in[:, HY_COLS + 2 * DIFF_QK:]], axis=1).T.astype(BF16)
    tok = lambda n: pl.BlockSpec((IN_PB, TM, n), lambda bb, i: (bb, i, 0))
    tr = lambda n: pl.BlockSpec((IN_PB, 1, n, TM), lambda bb, i: (bb, i, 0, 0))
    return pl.pallas_call(
        _odd_in_kernel,
        grid=(b // IN_PB, nt),
        in_specs=[tok(d),
                  pl.BlockSpec((IN_PB, N_MOD, d), lambda bb, i: (_mod_block(bb, i, IN_PB), 0, 0)),
                  _const_spec((1, d)),
                  _const_spec(wn.shape), _const_spec(wt.shape),
                  pl.BlockSpec((32, TM), lambda bb, i: (0, i)),
                  pl.BlockSpec((32, TM), lambda bb, i: (0, i))],
        out_specs=[tok(HY_COLS), tr(DIFF_QK), tok(DIFF_QK), tr(DIFF_HEADS * DIFF_VROWS)],
        out_shape=[jax.ShapeDtypeStruct((b, t, HY_COLS), F32),
                   jax.ShapeDtypeStruct((b, nt, DIFF_QK, TM), BF16),
                   jax.ShapeDtypeStruct((b, t, DIFF_QK), BF16),
                   jax.ShapeDtypeStruct((b, nt, DIFF_HEADS * DIFF_VROWS, TM), BF16)],
        compiler_params=_params("parallel", "parallel"),
        name="odd_in",
    )(xs, mods_l, g0, wn, wt, cos_t, sin_t)


def _attn_core(q_ref, k_ref, v_ref, qpad_ref, m_ref, acc_ref, s_refs, c_refs, *, n_kv, k_col, q_slot, v_row, v_rows):
    i = pl.program_id(1)
    zeros = jnp.zeros((HEAD_DIM, TM), BF16)
    for hh in range(N_SCORE_HEADS):
        qh = q_ref[0, 0, hh * HEAD_DIM:(hh + 1) * HEAD_DIM, :]
        qpad_ref[hh] = jnp.concatenate([qh, zeros] if q_slot(hh) == 0 else [zeros, qh], axis=0)
    m_ref[...] = jnp.full(m_ref.shape, -jnp.inf, F32)
    acc_ref[...] = jnp.zeros(acc_ref.shape, F32)

    def scores(j, hh, slot):
        kc = k_ref[0, pl.ds(pl.multiple_of(j * TM, TM), TM), k_col(hh):k_col(hh) + 128]
        s = jnp.dot(kc, qpad_ref[hh], preferred_element_type=F32)
        s_refs[slot][hh] = s
        c_refs[slot][hh] = jnp.max(s, axis=0, keepdims=True)

    def consume(j, hh, slot):
        vc = v_ref[0, j, v_row(hh):v_row(hh) + v_rows, :]
        m_old = m_ref[hh]
        m_new = jnp.maximum(m_old, c_refs[slot][hh])
        p = jnp.exp2(s_refs[slot][hh] - m_new).astype(BF16)
        alpha = jnp.exp2(m_old - m_new)
        acc_ref[hh] = acc_ref[hh] * alpha + jnp.dot(vc, p, preferred_element_type=F32)
        m_ref[hh] = m_new

    for hh in range(N_SCORE_HEADS):
        scores(0, hh, 0)

    def body(jj, carry):
        j = 2 * jj
        for hh in range(N_SCORE_HEADS):
            consume(j, hh, 0)
            scores(j + 1, hh, 1)
        for hh in range(N_SCORE_HEADS):
            consume(j + 1, hh, 1)
            scores(j + 2, hh, 0)
        return carry

    n = jnp.where(i == 0, 1, n_kv)
    lax.fori_loop(0, (n - 1) // 2, body, 0)
    for hh in range(N_SCORE_HEADS):
        consume(n - 1, hh, 0)


def _attn_scratch(v_rows):
    return [pltpu.VMEM((N_SCORE_HEADS, 128, TM), BF16),
            pltpu.VMEM((N_SCORE_HEADS, 1, TM), F32),
            pltpu.VMEM((N_SCORE_HEADS, v_rows, TM), F32),
            pltpu.VMEM((N_SCORE_HEADS, TM, TM), F32), pltpu.VMEM((N_SCORE_HEADS, TM, TM), F32),
            pltpu.VMEM((N_SCORE_HEADS, 1, TM), F32), pltpu.VMEM((N_SCORE_HEADS, 1, TM), F32)]


def _gqa_kernel(q_ref, k_ref, v_ref, o_ref, qpad_ref, m_ref, acc_ref, s0_ref, s1_ref, c0_ref, c1_ref, *, n_kv):
    group = GQA_HEADS // GQA_KV_HEADS
    _attn_core(q_ref, k_ref, v_ref, qpad_ref, m_ref, acc_ref, (s0_ref, s1_ref), (c0_ref, c1_ref), n_kv=n_kv,
               k_col=lambda h: 0, q_slot=lambda h: h // group, v_row=lambda h: (h // group) * GQA_VROWS,
               v_rows=GQA_VROWS)
    for hh in range(GQA_HEADS):
        a = acc_ref[hh]
        o_ref[0, 0, hh * HEAD_DIM:(hh + 1) * HEAD_DIM, :] = (a[:HEAD_DIM] / a[HEAD_DIM:HEAD_DIM + 1]).astype(o_ref.dtype)


def _gqa_attention(aq, ak, av):
    b, nt, nq, _ = aq.shape
    t = nt * TM
    assert nt % 2 == 1
    return pl.pallas_call(
        functools.partial(_gqa_kernel, n_kv=nt),
        grid=(b, nt),
        in_specs=[pl.BlockSpec((1, 1, nq, TM), lambda bb, i: (bb, i, 0, 0)),
                  pl.BlockSpec((1, t, 128), lambda bb, i: (bb, 0, 0)),
                  pl.BlockSpec((1, nt, GQA_KV_HEADS * GQA_VROWS, TM), lambda bb, i: (bb, 0, 0, 0))],
        out_specs=pl.BlockSpec((1, 1, nq, TM), lambda bb, i: (bb, i, 0, 0)),
        out_shape=jax.ShapeDtypeStruct((b, nt, nq, TM), BF16),
        scratch_shapes=_attn_scratch(GQA_VROWS),
        compiler_params=_params("parallel", "arbitrary"),
        name="gqa_attn",
    )(aq, ak, av)


def _diff_kernel(q_ref, k_ref, v_ref, lam_ref, gain_ref, o_ref, qpad_ref, m_ref, acc_ref, s0_ref, s1_ref,
                 c0_ref, c1_ref, *, n_kv, lam_init):
    _attn_core(q_ref, k_ref, v_ref, qpad_ref, m_ref, acc_ref, (s0_ref, s1_ref), (c0_ref, c1_ref), n_kv=n_kv,
               k_col=lambda hc: (hc // 2) * 128, q_slot=lambda hc: hc % 2, v_row=lambda hc: (hc // 2) * DIFF_VROWS,
               v_rows=DIFF_VROWS)
    lp = lam_ref[...]
    lam = (jnp.exp(jnp.sum(lp[0:1] * lp[1:2], axis=1, keepdims=True))
           - jnp.exp(jnp.sum(lp[2:3] * lp[3:4], axis=1, keepdims=True)) + lam_init)
    gain = gain_ref[...]
    nv = 2 * HEAD_DIM
    for hh in range(DIFF_HEADS):
        a0, a1 = acc_ref[2 * hh], acc_ref[2 * hh + 1]
        o = a0[:nv] / a0[nv:nv + 1] - lam * (a1[:nv] / a1[nv:nv + 1])
        y = o * lax.rsqrt(jnp.mean(o * o, axis=0, keepdims=True) + NORM_EPS) * gain
        o_ref[0, 0, hh * nv:(hh + 1) * nv, :] = (y * (1.0 - lam_init)).astype(o_ref.dtype)


def _diff_attention(dq, dk, dv, lam_p, gain, layer_idx):
    b, nt, nq, _ = dq.shape
    t = nt * TM
    assert nt % 2 == 1
    lam_init = 0.8 - 0.6 * math.exp(-0.3 * layer_idx)
    return pl.pallas_call(
        functools.partial(_diff_kernel, n_kv=nt, lam_init=lam_init),
        grid=(b, nt),
        in_specs=[pl.BlockSpec((1, 1, nq, TM), lambda bb, i: (bb, i, 0, 0)),
                  pl.BlockSpec((1, t, DIFF_QK), lambda bb, i: (bb, 0, 0)),
                  pl.BlockSpec((1, nt, DIFF_HEADS * DIFF_VROWS, TM), lambda bb, i: (bb, 0, 0, 0)),
                  _const_spec(lam_p.shape),
                  _const_spec((2 * HEAD_DIM, 1))],
        out_specs=pl.BlockSpec((1, 1, DIFF_V, TM), lambda bb, i: (bb, i, 0, 0)),
        out_shape=jax.ShapeDtypeStruct((b, nt, DIFF_V, TM), BF16),
        scratch_shapes=_attn_scratch(DIFF_VROWS),
        compiler_params=_params("parallel", "arbitrary"),
        name="diff_attn",
    )(dq, dk, dv, lam_p, gain.reshape(2 * HEAD_DIM, 1))


def _split_bf16(x):
    hi = x.astype(BF16)
    return hi, (x - hi.astype(F32)).astype(BF16)


def _gla_kernel(qk_ref, la_ref, vt_ref, o_ref, st_ref, *, backward):
    @pl.when(pl.program_id(1) == 0)
    def _():
        st_ref[...] = jnp.zeros(st_ref.shape, F32)

    n_chunk = TM // GLA_CHUNK
    r = lax.broadcasted_iota(jnp.int32, (TM, TM), 0)
    c = lax.broadcasted_iota(jnp.int32, (TM, TM), 1)
    same = (r // GLA_CHUNK) == (c // GLA_CHUNK)
    causal = (c >= r) if backward else (c <= r)
    tri = jnp.where(same & causal, 1.0, 0.0).astype(BF16)
    blk = jnp.where(same, 1.0, 0.0).astype(BF16)
    la_hi, la_lo = _split_bf16(la_ref[0])
    cum = jnp.dot(tri, la_hi, preferred_element_type=F32) + jnp.dot(tri, la_lo, preferred_element_type=F32)
    tot = jnp.dot(blk, la_hi, preferred_element_type=F32) + jnp.dot(blk, la_lo, preferred_element_type=F32)
    q = qk_ref[0, :, :GLA_QK]
    k = qk_ref[0, :, GLA_QK:]
    q_in = (q * jnp.exp(cum)).astype(BF16)
    k_in = k * jnp.exp(-cum)
    k_end = k * jnp.exp(tot - cum)
    dec = jnp.exp(tot)
    lane = lax.broadcasted_iota(jnp.int32, (TM, 128), 1)
    row = lax.broadcasted_iota(jnp.int32, (TM, 128), 0)
    keep = same & ((r >= c) if backward else (r <= c))
    order = range(n_chunk - 1, -1, -1) if backward else range(n_chunk)
    for h in range(GLA_HEADS):
        pair = slice((h // 2) * 128, (h // 2 + 1) * 128)
        head_lanes = (lane // GLA_DK) == (h % 2)
        q_pair = q_in[:, pair]
        k_in_h = jnp.where(head_lanes, k_in[:, pair], 0.0).astype(BF16)
        k_end_h = jnp.where(head_lanes, k_end[:, pair], 0.0)
        vt = vt_ref[0, 0, h * GLA_DV:(h + 1) * GLA_DV, :]
        a_t = jnp.where(keep, _nt(k_in_h, q_pair), 0.0).astype(BF16)
        o_t = jnp.dot(vt, a_t, preferred_element_type=F32)
        s_t = st_ref[h]
        for ci in order:
            in_chunk = (row // GLA_CHUNK) == ci
            q_c = jnp.where(in_chunk, q_pair, jnp.zeros_like(q_pair))
            o_t = o_t + _nt(s_t.astype(BF16), q_c)
            k_c = jnp.where(in_chunk, k_end_h, 0.0).astype(BF16)
            s_t = s_t * dec[ci * GLA_CHUNK:ci * GLA_CHUNK + 1, pair] + jnp.dot(vt, k_c, preferred_element_type=F32)
        st_ref[h] = s_t
        o_ref[0, 0, h * GLA_DV:(h + 1) * GLA_DV, :] = o_t


def _gla_direction(qk, la, vt, backward):
    b, t, _ = qk.shape
    nt = t // TM
    d = 1 if backward else 0
    tile = (lambda i: jnp.where(i == 0, 0, nt - i)) if backward else (lambda i: i)
    return pl.pallas_call(
        functools.partial(_gla_kernel, backward=backward),
        grid=(b, nt),
        in_specs=[pl.BlockSpec((1, TM, 2 * GLA_QK), lambda bb, i: (bb, tile(i), 0)),
                  pl.BlockSpec((1, TM, GLA_QK), lambda bb, i: (bb, tile(i), d)),
                  pl.BlockSpec((1, 1, GLA_V, TM), lambda bb, i: (bb, tile(i), 0, 0))],
        out_specs=pl.BlockSpec((1, 1, GLA_V, TM), lambda bb, i: (bb, tile(i), 0, 0)),
        out_shape=jax.ShapeDtypeStruct((b, nt, GLA_V, TM), F32),
        scratch_shapes=[pltpu.VMEM((GLA_HEADS, GLA_DV, 128), F32)],
        compiler_params=_params("parallel", "arbitrary"),
        name="gla_bwd" if backward else "gla_fwd",
    )(qk, la, vt)


def _post_phases(xs, mods, g, yas, ybs, woa_ref, wob_ref, w1_ref, w2_ref, ff_chunk):
    n = len(xs)
    ys = [jnp.dot(yas[s], woa_ref[...], preferred_element_type=F32)
          + jnp.dot(ybs[s].astype(F32).T.astype(BF16), wob_ref[...], preferred_element_type=F32) for s in range(n)]
    x1s = [xs[s] + mods[s][2:3] * _rms(ys[s], g[1:2]) for s in range(n)]
    hs = [(_rms(x1s[s], g[2:3]) * (1.0 + mods[s][4:5]) + mods[s][3:4]).astype(BF16) for s in range(n)]
    d_ff = w1_ref.shape[1]
    accs = [jnp.zeros(xs[0].shape, F32) for _ in range(n)]
    for c in range(d_ff // ff_chunk):
        cols = slice(c * ff_chunk, (c + 1) * ff_chunk)
        for s in range(n):
            u = jnp.dot(hs[s], w1_ref[:, cols], preferred_element_type=F32)
            u = jnp.square(jnp.maximum(u, 0.0)).astype(BF16)
            accs[s] = accs[s] + jnp.dot(u, w2_ref[cols, :], preferred_element_type=F32)
    return [x1s[s] + mods[s][5:6] * _rms(accs[s], g[3:4]) for s in range(n)]


def _post_gla_kernel(xs_ref, of_ref, ob_ref, gg_ref, gain_ref, yb_ref, mods_ref, g_ref, woa_ref, wob_ref, w1_ref,
                     w2_ref, o_ref, *, ff_chunk):
    gain = gain_ref[...]
    yas = []
    for s in range(POST_PB):
        parts = []
        for hh in range(GLA_HEADS):
            rows = slice(hh * GLA_DV, (hh + 1) * GLA_DV)
            o = of_ref[s, 0, rows, :] + ob_ref[s, 0, rows, :]
            gt = gg_ref[s, 0, rows, :]
            y = o * lax.rsqrt(jnp.mean(o * o, axis=0, keepdims=True) + NORM_EPS) * gain
            parts.append(y * (gt / (1.0 + jnp.exp(-gt))))
        yas.append(jnp.concatenate(parts, axis=0).T.astype(BF16))
    outs = _post_phases([xs_ref[s] for s in range(POST_PB)], [mods_ref[s] for s in range(POST_PB)], g_ref[...], yas,
                        [yb_ref[s, 0] for s in range(POST_PB)], woa_ref, wob_ref, w1_ref, w2_ref, ff_chunk)
    for s in range(POST_PB):
        o_ref[s] = outs[s]


def _post_kernel(xs_ref, ya_ref, yb_ref, mods_ref, g_ref, woa_ref, wob_ref, w1_ref, w2_ref, o_ref, *, ff_chunk):
    outs = _post_phases([xs_ref[s] for s in range(POST_PB)], [mods_ref[s] for s in range(POST_PB)], g_ref[...],
                        [ya_ref[s].astype(BF16) for s in range(POST_PB)], [yb_ref[s, 0] for s in range(POST_PB)],
                        woa_ref, wob_ref, w1_ref, w2_ref, ff_chunk)
    for s in range(POST_PB):
        o_ref[s] = outs[s]


def _post(xs, ya, yb_t, mods_l, g, w_out, w1, w2, skip_ctx):
    b, t, d = xs.shape
    nt = t // TM
    pb = POST_PB
    half = w_out.shape[0] // 2
    woa = w_out[:half].astype(BF16)
    wob = w_out[half:].astype(BF16)
    w1 = w1.astype(BF16)
    w2 = w2.astype(BF16)
    off = 1 if skip_ctx else 0
    tok = lambda n: pl.BlockSpec((pb, TM, n), lambda bb, i: (bb, i + off, 0))
    tr = lambda n: pl.BlockSpec((pb, 1, n, TM), lambda bb, i: (bb, i + off, 0, 0))
    if isinstance(ya, tuple):
        o_f, o_b, gg, gain = ya
        body, name = _post_gla_kernel, "post_gla"
        a_specs = [tr(half), tr(half), tr(half), _const_spec((GLA_DV, 1))]
        a_args = [o_f, o_b, gg, gain.reshape(GLA_DV, 1)]
    else:
        body, name = _post_kernel, "post"
        a_specs, a_args = [tok(half)], [ya]
    return pl.pallas_call(
        functools.partial(body, ff_chunk=1024),
        grid=(b // pb, nt - off),
        in_specs=[tok(d)] + a_specs + [
                  tr(half),
                  pl.BlockSpec((pb, N_MOD, d), lambda bb, i: (_mod_block(bb, i + off, pb), 0, 0)),
                  _const_spec(g.shape),
                  _const_spec(woa.shape), _const_spec(wob.shape), _const_spec(w1.shape), _const_spec(w2.shape)],
        out_specs=pl.BlockSpec((pb, TM, d), lambda bb, i: (bb, i, 0)),
        out_shape=jax.ShapeDtypeStruct((b, t - off * TM, d), F32),
        compiler_params=_params("parallel", "parallel"),
        name=name,
    )(xs, *a_args, yb_t, mods_l, g, woa, wob, w1, w2)


def _short_conv_kernel(u_ref, prev_ref, next_ref, w_ref, b_ref, v_ref, x1_ref, x2_ref, *, n_tiles):
    i = pl.program_id(1)
    u = u_ref[0]
    rows = lax.broadcasted_iota(jnp.int32, u.shape, 0)
    prev_row = jnp.where(i == 0, 0.0, prev_ref[0, 7:8, :])
    next_row = jnp.where(i == n_tiles - 1, 0.0, next_ref[0, 0:1, :])
    up = jnp.where(rows == 0, prev_row, pltpu.roll(u, 1, axis=0))
    dn = jnp.where(rows == TM - 1, next_row, pltpu.roll(u, TM - 1, axis=0))
    w = w_ref[...]
    s = up * w[0:1] + u * w[1:2] + dn * w[2:3] + b_ref[...]
    v_ref[0] = s[:, :HY_CH]
    x1_ref[0] = s[:, HY_CH:2 * HY_CH]
    x2_ref[0] = s[:, 2 * HY_CH:]


def _short_conv(u, conv_w, conv_b, first_tile, n_tiles):
    b, t, c = u.shape
    r8 = TM // 8
    last_blk = t // 8 - 1
    out = jax.ShapeDtypeStruct((b, n_tiles * TM, HY_CH), F32)
    ospec = pl.BlockSpec((1, TM, HY_CH), lambda bb, i: (bb, i, 0))
    return pl.pallas_call(
        functools.partial(_short_conv_kernel, n_tiles=n_tiles),
        grid=(b, n_tiles),
        in_specs=[pl.BlockSpec((1, TM, c), lambda bb, i: (bb, i + first_tile, 0)),
                  pl.BlockSpec((1, 8, c), lambda bb, i: (bb, jnp.maximum((i + first_tile) * r8 - 1, 0), 0)),
                  pl.BlockSpec((1, 8, c), lambda bb, i: (bb, jnp.minimum((i + first_tile + 1) * r8, last_blk), 0)),
                  _const_spec(conv_w.shape), _const_spec((1, c))],
        out_specs=[ospec, ospec, ospec],
        out_shape=[out, out, out],
        compiler_params=_params("parallel", "parallel"),
        name="hy_short",
    )(u, u, u, conv_w, conv_b.reshape(1, c))


def _dot3(a, b):
    ah, al = _split_bf16(a)
    bh, bl = _split_bf16(b)
    d = lambda x, y: jnp.dot(x, y, preferred_element_type=F32)
    return d(ah, bh) + (d(al, bh) + d(ah, bl))


def _filter_kernel(w1t_ref, w1c_ref, w1s_ref, b1_ref, w2_ref, b2_ref, w3_ref, delta_ref, o_ref, *, seq, rb):
    n = pl.program_id(0) * rb + lax.broadcasted_iota(jnp.int32, (rb, 1), 0)
    d = jnp.where(n < seq, n, 2 * seq - n).astype(F32)
    t = d / max(seq - 1, 1)
    w = (2 * math.pi / seq) * d
    band = lax.broadcasted_iota(jnp.int32, (1, HY_BANDS), 1).astype(F32)
    bands = 1e-4 + band * ((HY_BANDS - 1 - 1e-4) / (HY_BANDS - 1))
    ang = w * bands
    pre = t * w1t_ref[...] + _dot3(jnp.cos(ang), w1c_ref[...]) - _dot3(jnp.sin(ang), w1s_ref[...]) + b1_ref[...]
    hid = jnp.sin(pre)
    hid = jnp.sin(_dot3(hid, w2_ref[...]) + b2_ref[...])
    decay = jnp.where(n == seq, 0.0, jnp.exp(-t * delta_ref[...]))
    for o in range(HY_ORDER):
        o_ref[o] = (_dot3(hid, w3_ref[0, o]) * decay).astype(o_ref.dtype)


def _hyena_kernels(seq, f_w1, f_b1, f_w2, f_b2, f_w3):
    rb = min(512, seq)
    nblk = 2 * seq // rb
    ffn = f_w2.shape[0]
    w3 = f_w3.reshape(ffn, HY_ORDER, 2, HY_CH).transpose(2, 1, 0, 3)
    deltas = jnp.abs(jnp.linspace(math.log(HY_TARGET) / HY_FAST_PCT, math.log(HY_TARGET) / HY_SLOW_PCT,
                                  HY_CH, dtype=F32)).reshape(1, HY_CH)
    return pl.pallas_call(
        functools.partial(_filter_kernel, seq=seq, rb=rb),
        grid=(nblk,),
        in_specs=[_const_spec((1, ffn)), _const_spec((HY_BANDS, ffn)), _const_spec((HY_BANDS, ffn)),
                  _const_spec((1, ffn)), _const_spec((ffn, ffn)), _const_spec((1, ffn)),
                  pl.BlockSpec((1, HY_ORDER, ffn, HY_CH), lambda j: ((2 * j) // nblk, 0, 0, 0)),
                  _const_spec((1, HY_CH))],
        out_specs=pl.BlockSpec((HY_ORDER, rb, HY_CH), lambda j: (0, j, 0)),
        out_shape=jax.ShapeDtypeStruct((HY_ORDER, 2 * seq, HY_CH), BF16),
        compiler_params=_params("parallel"),
        name="hy_filter",
    )(f_w1[0:1], f_w1[1:1 + HY_BANDS], f_w1[1 + HY_BANDS:], f_b1.reshape(1, ffn), f_w2, f_b2.reshape(1, ffn), w3,
      deltas)


def _dft_tables(seq):
    n = 2 * seq
    r = math.isqrt(n)
    assert r * r == n
    idx = jnp.arange(r, dtype=jnp.int32)
    phi = (2 * math.pi / r) * ((idx[:, None] * idx[None, :]) % r).astype(F32)
    f1 = jnp.concatenate([jnp.cos(phi), -jnp.sin(phi)], axis=0).astype(BF16)
    m = (idx[:, None, None] + r * idx[None, :, None]) * idx[None, None, :]
    th = (2 * math.pi / n) * (m % n).astype(F32)
    c, s = jnp.cos(th), jnp.sin(th)
    blockm = lambda re, im: jnp.concatenate([jnp.concatenate([re, -im], axis=-1),
                                             jnp.concatenate([im, re], axis=-1)], axis=-2)
    g = blockm(c, -s).astype(BF16)
    ginv = blockm(jnp.swapaxes(c, 1, 2), jnp.swapaxes(s, 1, 2)).astype(BF16)
    finv = (jnp.concatenate([jnp.cos(phi.T), -jnp.sin(phi.T)], axis=1)[:r // 2] / n).astype(BF16)
    return r, f1, g, ginv, finv


DFT_BB = 8


def _dft1_kernel(f_ref, x_ref, o_ref):
    f = f_ref[...]
    r = o_ref.shape[2]
    x = pltpu.einshape("abc->bac", x_ref[0].astype(F32))
    ys = [jnp.dot(f, x[j].astype(BF16), preferred_element_type=F32) for j in range(DFT_BB)]
    y = pltpu.einshape("jkc->kjc", jnp.stack(ys, axis=0))
    o_ref[0, 0] = y[:r].astype(o_ref.dtype)
    o_ref[0, 1] = y[r:].astype(o_ref.dtype)


def _dft_stage1(f, x4):
    b, ka, r, c = x4.shape
    return pl.pallas_call(
        _dft1_kernel,
        grid=(b, r // DFT_BB),
        in_specs=[_const_spec(f.shape), pl.BlockSpec((1, ka, DFT_BB, c), lambda bi, j: (bi, 0, j, 0))],
        out_specs=pl.BlockSpec((1, 2, r, DFT_BB, c), lambda bi, j: (bi, 0, 0, j, 0)),
        out_shape=jax.ShapeDtypeStruct((b, 2, r, r, c), BF16),
        compiler_params=_params("parallel", "parallel"),
        name="hy_dft1",
    )(f, x4)


def _fspec_kernel(a_ref, g_ref, o_ref, *, kb, r):
    for kk in range(kb):
        a = a_ref[0, :, kk].reshape(2 * r, a_ref.shape[-1])
        o_ref[0, kk] = jnp.dot(g_ref[kk], a, preferred_element_type=F32)


def _filter_spectrum(a, g, kb):
    o, _, r, _, c = a.shape
    return pl.pallas_call(
        functools.partial(_fspec_kernel, kb=kb, r=r),
        grid=(o, r // kb),
        in_specs=[pl.BlockSpec((1, 2, kb, r, c), lambda oo, j: (oo, 0, j, 0, 0)),
                  pl.BlockSpec((kb, 2 * r, 2 * r), lambda oo, j: (j, 0, 0))],
        out_specs=pl.BlockSpec((1, kb, 2 * r, c), lambda oo, j: (oo, j, 0, 0)),
        out_shape=jax.ShapeDtypeStruct((o, r, 2 * r, c), F32),
        compiler_params=_params("parallel", "parallel"),
        name="hy_fspec",
    )(a, g)


def _dft2_kernel(a_ref, g_ref, gi_ref, k_ref, o_ref, *, kb, r):
    for kk in range(kb):
        a = a_ref[0, :, kk].reshape(2 * r, a_ref.shape[-1])
        x = jnp.dot(g_ref[kk], a, preferred_element_type=F32)
        ks = k_ref[0, kk]
        xr, xi, kr, ki = x[:r], x[r:], ks[:r], ks[r:]
        y = jnp.concatenate([xr * kr - xi * ki, xr * ki + xi * kr], axis=0).astype(BF16)
        bq = jnp.dot(gi_ref[kk], y, preferred_element_type=F32)
        o_ref[0, 0, kk] = bq[:r].astype(o_ref.dtype)
        o_ref[0, 1, kk] = bq[r:].astype(o_ref.dtype)


def _dft_stage2(a, g, ginv, kspec, order, kb):
    b, _, r, _, c = a.shape
    blk = pl.BlockSpec((1, 2, kb, r, c), lambda j, bb: (bb, 0, j, 0, 0))
    tab = pl.BlockSpec((kb, 2 * r, 2 * r), lambda j, bb: (j, 0, 0))
    return pl.pallas_call(
        functools.partial(_dft2_kernel, kb=kb, r=r),
        grid=(r // kb, b),
        in_specs=[blk, tab, tab, pl.BlockSpec((1, kb, 2 * r, c), lambda j, bb: (order, j, 0, 0))],
        out_specs=blk,
        out_shape=jax.ShapeDtypeStruct(a.shape, BF16),
        compiler_params=_params("parallel", "arbitrary"),
        name="hy_dft2",
    )(a, g, ginv, kspec)


def _dft3_kernel(f_ref, b_ref, gate_ref, z_ref, bias_ref, o_ref):
    f = f_ref[...]
    bq = jnp.concatenate([b_ref[0, 0], b_ref[0, 1]], axis=0)
    bq = pltpu.einshape("kjc->jkc", bq.astype(F32))
    ys = [jnp.dot(f, bq[j].astype(BF16), preferred_element_type=F32) for j in range(DFT_BB)]
    y = pltpu.einshape("jac->ajc", jnp.stack(ys, axis=0))
    o_ref[0] = gate_ref[0] * (y + z_ref[0] * bias_ref[...])


def _dft_stage3(finv, bq5, gate4, z4, bias):
    b, _, r, _, c = bq5.shape
    ra = finv.shape[0]
    io = pl.BlockSpec((1, ra, DFT_BB, c), lambda bi, j: (bi, 0, j, 0))
    return pl.pallas_call(
        _dft3_kernel,
        grid=(b, r // DFT_BB),
        in_specs=[_const_spec(finv.shape), pl.BlockSpec((1, 2, r, DFT_BB, c), lambda bi, j: (bi, 0, 0, j, 0)), io, io,
                  _const_spec((1, 1, c))],
        out_specs=io,
        out_shape=jax.ShapeDtypeStruct((b, ra, r, c), F32),
        compiler_params=_params("parallel", "parallel"),
        name="hy_dft3",
    )(finv, bq5, gate4, z4, bias)


def _long_conv_gated(v, gates, kern, hy_bias):
    b, seq, c = v.shape
    r, f1, g, ginv, finv = _dft_tables(seq)
    kb = 8
    kspec = _filter_spectrum(_dft_stage1(f1, kern.reshape(HY_ORDER, r, r, c)), g, kb)
    z = v.reshape(b, r // 2, r, c)
    for o in range(HY_ORDER):
        a = _dft_stage1(f1[:, :r // 2], z)
        bq = _dft_stage2(a, g, ginv, kspec, o, kb)
        z = _dft_stage3(finv, bq, gates[o].reshape(b, r // 2, r, c), z, hy_bias[o].reshape(1, 1, c))
    return z.reshape(b, seq, c)


def _ctx_conv_kernel(v_ref, x1_ref, x2_ref, k_ref, ff_ref, fi_ref, bias_ref, o_ref):
    n = ff_ref.shape[1]
    z = v_ref[0]
    for o, gate_ref in enumerate((x1_ref, x2_ref)):
        ks = jnp.dot(ff_ref[...], k_ref[o], preferred_element_type=F32)
        x = jnp.dot(ff_ref[:, :TM], z.astype(BF16), preferred_element_type=F32)
        xr, xi, kr, ki = x[:n], x[n:], ks[:n], ks[n:]
        y = jnp.concatenate([xr * kr - xi * ki, xr * ki + xi * kr], axis=0).astype(BF16)
        conv = jnp.dot(fi_ref[...], y, preferred_element_type=F32)
        z = gate_ref[0] * (conv + z * bias_ref[o:o + 1])
    o_ref[0] = z


def _ctx_conv_gated(v, x1, x2, kern, hy_bias):
    b, seq, c = v.shape
    n = 2 * seq
    idx = jnp.arange(n, dtype=jnp.int32)
    th = (2 * math.pi / n) * ((idx[:, None] * idx[None, :]) % n).astype(F32)
    ff = jnp.concatenate([jnp.cos(th), -jnp.sin(th)], axis=0).astype(BF16)
    fi = (jnp.concatenate([jnp.cos(th), -jnp.sin(th)], axis=1)[:seq] / n).astype(BF16)
    io = pl.BlockSpec((1, seq, c), lambda bb: (bb, 0, 0))
    return pl.pallas_call(
        _ctx_conv_kernel,
        grid=(b,),
        in_specs=[io, io, io, _const_spec(kern.shape), _const_spec(ff.shape), _const_spec(fi.shape),
                  _const_spec(hy_bias.shape)],
        out_specs=io,
        out_shape=jax.ShapeDtypeStruct((b, seq, c), F32),
        compiler_params=_params("parallel"),
        name="hy_ctx",
    )(v, x1, x2, kern, ff, fi, hy_bias)


def _hyena(u, lc, conv_w, conv_b, f_w1, f_b1, f_w2, f_b2, f_w3, hy_bias):
    seq = u.shape[1] - lc
    vc, x1c, x2c = _short_conv(u, conv_w, conv_b, 0, lc // TM)
    vx, x1x, x2x = _short_conv(u, conv_w, conv_b, lc // TM, seq // TM)
    yc = _ctx_conv_gated(vc, x1c, x2c, _hyena_kernels(lc, f_w1, f_b1, f_w2, f_b2, f_w3), hy_bias)
    yx = _long_conv_gated(vx, (x1x, x2x), _hyena_kernels(seq, f_w1, f_b1, f_w2, f_b2, f_w3), hy_bias)
    return jnp.concatenate([yc, yx], axis=1)


def _rope_tables_t(seq, lc):
    rows = seq // GRID_W
    r = jnp.broadcast_to(jnp.arange(rows, dtype=F32)[:, None], (rows, GRID_W)).reshape(-1)
    col = jnp.broadcast_to(jnp.arange(GRID_W, dtype=F32)[None, :], (rows, GRID_W)).reshape(-1)
    n_pairs = HEAD_DIM // 4
    inv = ROPE_THETA ** (-jnp.arange(n_pairs, dtype=F32) / n_pairs)
    ang = jnp.concatenate([r[:, None] * inv, col[:, None] * inv], axis=-1)
    cos = jnp.concatenate([jnp.ones((lc, 2 * n_pairs), F32), jnp.cos(ang)], axis=0)
    sin = jnp.concatenate([jnp.zeros((lc, 2 * n_pairs), F32), jnp.sin(ang)], axis=0)
    return cos.T, sin.T


def kernel(x, c, ctx, c_ctx, w_ada, b_ada, norm_g, w_mlp_in, w_mlp_out, ev_w_in, ev_w_lr, ev_b_lr, ev_gla_g,
           ev_qk_g, ev_w_out, od_w_in, od_conv_w, od_conv_b, od_f_w1, od_f_b1, od_f_w2, od_f_b2, od_f_w3,
           od_hy_bias, od_lam, od_diff_g, od_w_out):
    bsz, seq, d = x.shape
    lc = ctx.shape[1]
    depth = w_ada.shape[0]
    assert lc == TM and seq % TM == 0 and bsz <= MOD_CTX_ROW and bsz % POST_PB == 0 and bsz % IN_PB == 0
    cos_t, sin_t = _rope_tables_t(seq, lc)
    cc = jnp.zeros((8, d), F32).at[:bsz].set(c).at[MOD_CTX_ROW:MOD_CTX_ROW + max(POST_PB, IN_PB)].set(c_ctx)
    mods = _mods(cc, w_ada, b_ada).reshape(depth, 8, N_MOD, d)
    xs = jnp.concatenate([ctx, x], axis=1)
    for i in range(depth):
        last = i == depth - 1
        j = i // 2
        g = norm_g[i]
        if i % 2 == 0:
            qk, la, gv, gg, aq, ak, av = _even_in(xs, mods[i], g[0:1], ev_w_in[j], ev_w_lr[j], ev_b_lr[j],
                                                  ev_qk_g[j], cos_t, sin_t)
            ya = (_gla_direction(qk, la, gv, False), _gla_direction(qk, la, gv, True), gg, ev_gla_g[j])
            yb = _gqa_attention(aq, ak, av)
            w_out = ev_w_out[j]
        else:
            u, dq, dk, dv = _odd_in(xs, mods[i], g[0:1], od_w_in[j], cos_t, sin_t)
            ya = _hyena(u, lc, od_conv_w[j], od_conv_b[j], od_f_w1[j], od_f_b1[j], od_f_w2[j], od_f_b2[j],
                        od_f_w3[j], od_hy_bias[j])
            yb = _diff_attention(dq, dk, dv, od_lam[j], od_diff_g[j], i)
            w_out = od_w_out[j]
        xs = _post(xs, ya, yb, mods[i], g, w_out, w_mlp_in[i], w_mlp_out[i], skip_ctx=last)
    return xs
```

```python
import functools
import math

import jax
import jax.numpy as jnp
import numpy as np
from jax import lax
from jax.experimental import pallas as pl
from jax.experimental.pallas import tpu as pltpu

F32 = jnp.float32
BF16 = jnp.bfloat16

TM = 256
GRID_W = 64
HEAD_DIM = 64
ROPE_THETA = 10000.0
NORM_EPS = 1e-6
N_MOD = 6
VMEM_LIMIT = 56 * 1024 * 1024

GLA_HEADS = 4
GLA_DK = 64
GLA_DV = 128
GLA_RANK = 16
GLA_NORMALIZER = 16.0
GLA_CHUNK = 64
GLA_QK = GLA_HEADS * GLA_DK
GLA_V = GLA_HEADS * GLA_DV
GQA_HEADS = 8
GQA_KV_HEADS = 2
HY_CH = 512
HY_ORDER = 2
HY_BANDS = 16
HY_TARGET = 1e-2
HY_FAST_PCT = 0.3
HY_SLOW_PCT = 1.5
HY_COLS = (HY_ORDER + 1) * HY_CH
DIFF_HEADS = 4
DIFF_QK = DIFF_HEADS * 2 * HEAD_DIM
DIFF_V = DIFF_HEADS * 2 * HEAD_DIM
ONES_ROWS = 16
N_SCORE_HEADS = 8
LOG2E = 1.4426950408889634
Q_SCALE = HEAD_DIM ** -0.5 * LOG2E
GQA_VROWS = HEAD_DIM + ONES_ROWS
DIFF_VROWS = 2 * HEAD_DIM + ONES_ROWS


def _params(*sem):
    return pltpu.CompilerParams(dimension_semantics=sem, vmem_limit_bytes=VMEM_LIMIT)


def _const_spec(shape):
    nd = len(shape)
    return pl.BlockSpec(shape, lambda *_: (0,) * nd, pipeline_mode=pl.Buffered(1))


def _rms(x, g):
    return x * lax.rsqrt(jnp.mean(x * x, axis=-1, keepdims=True) + NORM_EPS) * g


MOD_CTX_ROW = 4
POST_PB = 2
IN_PB = 2


def _mod_block(bb, i, pb):
    return jnp.where(i == 0, MOD_CTX_ROW // pb, bb)


def _mods_kernel(cc_ref, w_ref, b_ref, o_ref):
    cc = cc_ref[...]
    s = cc / (1.0 + jnp.exp(-cc))
    o_ref[0] = jnp.dot(s.astype(BF16), w_ref[0].astype(BF16), preferred_element_type=F32) + b_ref[0]


def _mods(cc, w_ada, b_ada):
    depth, d, n = w_ada.shape
    nb = n // 4
    return pl.pallas_call(
        _mods_kernel,
        grid=(depth, n // nb),
        in_specs=[pl.BlockSpec((8, d), lambda l, j: (0, 0)),
                  pl.BlockSpec((1, d, nb), lambda l, j: (l, 0, j)),
                  pl.BlockSpec((1, 1, nb), lambda l, j: (l, 0, j))],
        out_specs=pl.BlockSpec((1, 8, nb), lambda l, j: (l, 0, j)),
        out_shape=jax.ShapeDtypeStruct((depth, 8, n), F32),
        compiler_params=_params("parallel", "parallel"),
        name="ada_mods",
    )(cc, w_ada, b_ada.reshape(depth, 1, n))


def _normed_input(x, mods, g):
    return (_rms(x, g) * (1.0 + mods[1:2]) + mods[0:1]).astype(BF16)


def _rope_rows(y, cos, sin):
    x1, x2 = y[:32], y[32:]
    return x1 * cos - x2 * sin, x1 * sin + x2 * cos


def _log_sigmoid(x):
    return jnp.minimum(x, 0.0) - jnp.log(1.0 + jnp.exp(-jnp.abs(x)))


def _nt(a, b):
    return lax.dot_general(a, b, (((1,), (1,)), ((), ())), preferred_element_type=F32)


def _even_in_kernel(xs_ref, mods_ref, g_ref, wn_ref, wt_ref, wlr_ref, blr_ref, gq_ref, gk_ref,
                    cos_ref, sin_ref, qk_ref, la_ref, gv_ref, gg_ref, aq_ref, ak_ref, av_ref):
    hs = [_normed_input(xs_ref[s], mods_ref[s], g_ref[0:1]) for s in range(IN_PB)]
    ps = [jnp.dot(h, wn_ref[...], preferred_element_type=F32) for h in hs]
    pts = [_nt(wt_ref[...], h) for h in hs]
    cos, sin = cos_ref[...], sin_ref[...]
    gq, gk_gain = gq_ref[...], gk_ref[...]
    ones = jnp.ones((ONES_ROWS, TM), BF16)
    for s in range(IN_PB):
        p, pt_all = ps[s], pts[s]
        qk_ref[s, :, :GLA_QK] = p[:, :GLA_QK] * GLA_DK ** -0.5
        qk_ref[s, :, GLA_QK:] = p[:, GLA_QK:2 * GLA_QK]
        lr = p[:, 2 * GLA_QK:].astype(BF16)
        gk = jnp.dot(lr, wlr_ref[...], preferred_element_type=F32) + blr_ref[...]
        la_ref[s] = _log_sigmoid(gk) * (1.0 / GLA_NORMALIZER)
        gv_ref[s, 0] = pt_all[:GLA_V].astype(BF16)
        gg_ref[s, 0] = pt_all[GLA_V:2 * GLA_V]
        pt = pt_all[2 * GLA_V:]
        for hh in range(GQA_HEADS):
            blk = pt[hh * 64:(hh + 1) * 64]
            y = blk * lax.rsqrt(jnp.mean(blk * blk, axis=0, keepdims=True) + NORM_EPS) * gq
            o1, o2 = _rope_rows(y, cos, sin)
            aq_ref[s, 0, hh * 64:hh * 64 + 32, :] = (o1 * Q_SCALE).astype(BF16)
            aq_ref[s, 0, hh * 64 + 32:hh * 64 + 64, :] = (o2 * Q_SCALE).astype(BF16)
        kparts = []
        for hh in range(GQA_KV_HEADS):
            blk = pt[512 + hh * 64:512 + (hh + 1) * 64]
            y = blk * lax.rsqrt(jnp.mean(blk * blk, axis=0, keepdims=True) + NORM_EPS) * gk_gain
            o1, o2 = _rope_rows(y, cos, sin)
            kparts += [o1, o2]
        ak_ref[s] = jnp.concatenate(kparts, axis=0).T.astype(BF16)
        for hh in range(GQA_KV_HEADS):
            av_ref[s, 0, hh * GQA_VROWS:hh * GQA_VROWS + 64, :] = pt[640 + hh * 64:640 + (hh + 1) * 64].astype(BF16)
            av_ref[s, 0, hh * GQA_VROWS + 64:(hh + 1) * GQA_VROWS, :] = ones


def _odd_in_kernel(xs_ref, mods_ref, g_ref, wn_ref, wt_ref, cos_ref, sin_ref,
                   u_ref, dq_ref, dk_ref, dv_ref):
    hs = [_normed_input(xs_ref[s], mods_ref[s], g_ref[0:1]) for s in range(IN_PB)]
    us = [jnp.dot(h, wn_ref[...], preferred_element_type=F32) for h in hs]
    pts = [_nt(wt_ref[...], h) for h in hs]
    cos, sin = cos_ref[...], sin_ref[...]
    ones = jnp.ones((ONES_ROWS, TM), BF16)
    nv = 2 * HEAD_DIM
    for s in range(IN_PB):
        pt = pts[s]
        u_ref[s] = us[s]
        kparts = []
        for hh in range(2 * DIFF_HEADS):
            o1, o2 = _rope_rows(pt[hh * 64:(hh + 1) * 64], cos, sin)
            dq_ref[s, 0, hh * 64:hh * 64 + 32, :] = (o1 * Q_SCALE).astype(BF16)
            dq_ref[s, 0, hh * 64 + 32:hh * 64 + 64, :] = (o2 * Q_SCALE).astype(BF16)
            k1, k2 = _rope_rows(pt[DIFF_QK + hh * 64:DIFF_QK + (hh + 1) * 64], cos, sin)
            kparts += [k1, k2]
        dk_ref[s] = jnp.concatenate(kparts, axis=0).T.astype(BF16)
        for hh in range(DIFF_HEADS):
            rows = pt[2 * DIFF_QK + hh * nv:2 * DIFF_QK + (hh + 1) * nv]
            dv_ref[s, 0, hh * DIFF_VROWS:hh * DIFF_VROWS + nv, :] = rows.astype(BF16)
            dv_ref[s, 0, hh * DIFF_VROWS + nv:(hh + 1) * DIFF_VROWS, :] = ones


def _pair_perm(n_heads):
    base = np.concatenate([np.arange(0, HEAD_DIM, 2), np.arange(1, HEAD_DIM, 2)])
    return np.concatenate([h * HEAD_DIM + base for h in range(n_heads)])


def _even_in(xs, mods_l, g0, w_in, w_lr, b_lr, qk_g, cos_t, sin_t):
    b, t, d = xs.shape
    nt = t // TM
    c_v = 2 * GLA_QK
    c_g = c_v + GLA_V
    c_lr0 = c_g + GLA_V
    c_lr = c_lr0 + 2 * GLA_RANK
    wn = jnp.concatenate([w_in[:, :c_v], w_in[:, c_lr0:c_lr], jnp.zeros((d, 128 - 2 * GLA_RANK), F32)],
                         axis=1).astype(BF16)
    cq = c_lr + GQA_HEADS * HEAD_DIM
    ck = cq + GQA_KV_HEADS * HEAD_DIM
    wq = w_in[:, c_lr:cq][:, _pair_perm(GQA_HEADS)]
    wk = w_in[:, cq:ck][:, _pair_perm(GQA_KV_HEADS)]
    wt = jnp.concatenate([w_in[:, c_v:c_lr0], wq, wk, w_in[:, ck:]], axis=1).T.astype(BF16)
    wlr = jnp.zeros((128, 2 * GLA_QK), F32)
    wlr = wlr.at[:GLA_RANK, :GLA_QK].set(w_lr[0]).at[GLA_RANK:2 * GLA_RANK, GLA_QK:].set(w_lr[1]).astype(BF16)
    blr = b_lr.reshape(1, 2 * GLA_QK)
    perm = _pair_perm(1)
    gq = qk_g[0][perm].reshape(HEAD_DIM, 1)
    gk = qk_g[1][perm].reshape(HEAD_DIM, 1)
    tok = lambda n: pl.BlockSpec((IN_PB, TM, n), lambda bb, i: (bb, i, 0))
    tr = lambda n: pl.BlockSpec((IN_PB, 1, n, TM), lambda bb, i: (bb, i, 0, 0))
    return pl.pallas_call(
        _even_in_kernel,
        grid=(b // IN_PB, nt),
        in_specs=[tok(d),
                  pl.BlockSpec((IN_PB, N_MOD, d), lambda bb, i: (_mod_block(bb, i, IN_PB), 0, 0)),
                  _const_spec((1, d)),
                  _const_spec(wn.shape), _const_spec(wt.shape), _const_spec(wlr.shape), _const_spec(blr.shape),
                  _const_spec(gq.shape), _const_spec(gk.shape),
                  pl.BlockSpec((32, TM), lambda bb, i: (0, i)),
                  pl.BlockSpec((32, TM), lambda bb, i: (0, i))],
        out_specs=[tok(2 * GLA_QK), tok(2 * GLA_QK), tr(GLA_V), tr(GLA_V), tr(512), tok(128),
                   tr(GQA_KV_HEADS * GQA_VROWS)],
        out_shape=[jax.ShapeDtypeStruct((b, t, 2 * GLA_QK), F32),
                   jax.ShapeDtypeStruct((b, t, 2 * GLA_QK), F32),
                   jax.ShapeDtypeStruct((b, nt, GLA_V, TM), BF16),
                   jax.ShapeDtypeStruct((b, nt, GLA_V, TM), F32),
                   jax.ShapeDtypeStruct((b, nt, 512, TM), BF16),
                   jax.ShapeDtypeStruct((b, t, 128), BF16),
                   jax.ShapeDtypeStruct((b, nt, GQA_KV_HEADS * GQA_VROWS, TM), BF16)],
        compiler_params=_params("parallel", "parallel"),
        name="even_in",
    )(xs, mods_l, g0, wn, wt, wlr, blr, gq, gk, cos_t, sin_t)


def _odd_in(xs, mods_l, g0, w_in, cos_t, sin_t):
    b, t, d = xs.shape
    nt = t // TM
    wn = w_in[:, :HY_COLS].astype(BF16)
    perm = _pair_perm(2 * DIFF_HEADS)
    wq = w_in[:, HY_COLS:HY_COLS + DIFF_QK][:, perm]
    wk = w_in[:, HY_COLS + DIFF_QK:HY_COLS + 2 * DIFF_QK][:, perm]
    wt = jnp.concatenate([wq, wk, w_in[:, HY_COLS + 2 * DIFF_QK:]], axis=1).T.astype(BF16)
    tok = lambda n: pl.BlockSpec((IN_PB, TM, n), lambda bb, i: (bb, i, 0))
    tr = lambda n: pl.BlockSpec((IN_PB, 1, n, TM), lambda bb, i: (bb, i, 0, 0))
    return pl.pallas_call(
        _odd_in_kernel,
        grid=(b // IN_PB, nt),
        in_specs=[tok(d),
                  pl.BlockSpec((IN_PB, N_MOD, d), lambda bb, i: (_mod_block(bb, i, IN_PB), 0, 0)),
                  _const_spec((1, d)),
                  _const_spec(wn.shape), _const_spec(wt.shape),
                  pl.BlockSpec((32, TM), lambda bb, i: (0, i)),
                  pl.BlockSpec((32, TM), lambda bb, i: (0, i))],
        out_specs=[tok(HY_COLS), tr(DIFF_QK), tok(DIFF_QK), tr(DIFF_HEADS * DIFF_VROWS)],
        out_shape=[jax.ShapeDtypeStruct((b, t, HY_COLS), F32),
                   jax.ShapeDtypeStruct((b, nt, DIFF_QK, TM), BF16),
                   jax.ShapeDtypeStruct((b, t, DIFF_QK), BF16),
                   jax.ShapeDtypeStruct((b, nt, DIFF_HEADS * DIFF_VROWS, TM), BF16)],
        compiler_params=_params("parallel", "parallel"),
        name="odd_in",
    )(xs, mods_l, g0, wn, wt, cos_t, sin_t)


def _attn_core(q_ref, k_ref, v_ref, qpad_ref, m_ref, acc_ref, s_refs, c_refs, *, n_kv, k_col, q_slot, v_row, v_rows,
               scores_between):
    i = pl.program_id(1)
    zeros = jnp.zeros((HEAD_DIM, TM), BF16)
    for hh in range(N_SCORE_HEADS):
        qh = q_ref[0, 0, hh * HEAD_DIM:(hh + 1) * HEAD_DIM, :]
        qpad_ref[hh] = jnp.concatenate([qh, zeros] if q_slot(hh) == 0 else [zeros, qh], axis=0)
    m_ref[...] = jnp.full(m_ref.shape, -jnp.inf, F32)
    acc_ref[...] = jnp.zeros(acc_ref.shape, F32)

    def scores(j, hh, slot):
        kc = k_ref[0, pl.ds(pl.multiple_of(j * TM, TM), TM), k_col(hh):k_col(hh) + 128]
        s = jnp.dot(kc, qpad_ref[hh], preferred_element_type=F32)
        s_refs[slot][hh] = s
        c_refs[slot][hh] = jnp.max(s, axis=0, keepdims=True)

    def probs(hh, slot):
        m_old = m_ref[hh]
        m_new = jnp.maximum(m_old, c_refs[slot][hh])
        return m_old, m_new, jnp.exp2(s_refs[slot][hh] - m_new).astype(BF16)

    def accumulate(j, hh, state):
        m_old, m_new, p = state
        vc = v_ref[0, j, v_row(hh):v_row(hh) + v_rows, :]
        alpha = jnp.exp2(m_old - m_new)
        acc_ref[hh] = acc_ref[hh] * alpha + jnp.dot(vc, p, preferred_element_type=F32)
        m_ref[hh] = m_new

    for hh in range(N_SCORE_HEADS):
        scores(0, hh, 0)

    def step(j, slot):
        for hh in range(N_SCORE_HEADS):
            state = probs(hh, slot)
            if scores_between:
                scores(j + 1, hh, 1 - slot)
            accumulate(j, hh, state)
            if not scores_between:
                scores(j + 1, hh, 1 - slot)

    def body(jj, carry):
        step(2 * jj, 0)
        step(2 * jj + 1, 1)
        return carry

    n = jnp.where(i == 0, 1, n_kv)
    lax.fori_loop(0, (n - 1) // 2, body, 0)
    for hh in range(N_SCORE_HEADS):
        accumulate(n - 1, hh, probs(hh, 0))


def _attn_scratch(v_rows):
    return [pltpu.VMEM((N_SCORE_HEADS, 128, TM), BF16),
            pltpu.VMEM((N_SCORE_HEADS, 1, TM), F32),
            pltpu.VMEM((N_SCORE_HEADS, v_rows, TM), F32),
            pltpu.VMEM((N_SCORE_HEADS, TM, TM), F32), pltpu.VMEM((N_SCORE_HEADS, TM, TM), F32),
            pltpu.VMEM((N_SCORE_HEADS, 1, TM), F32), pltpu.VMEM((N_SCORE_HEADS, 1, TM), F32)]


def _gqa_kernel(q_ref, k_ref, v_ref, o_ref, qpad_ref, m_ref, acc_ref, s0_ref, s1_ref, c0_ref, c1_ref, *, n_kv):
    group = GQA_HEADS // GQA_KV_HEADS
    _attn_core(q_ref, k_ref, v_ref, qpad_ref, m_ref, acc_ref, (s0_ref, s1_ref), (c0_ref, c1_ref), n_kv=n_kv,
               k_col=lambda h: 0, q_slot=lambda h: h // group, v_row=lambda h: (h // group) * GQA_VROWS,
               v_rows=GQA_VROWS, scores_between=True)
    for hh in range(GQA_HEADS):
        a = acc_ref[hh]
        o_ref[0, 0, hh * HEAD_DIM:(hh + 1) * HEAD_DIM, :] = (a[:HEAD_DIM] / a[HEAD_DIM:HEAD_DIM + 1]).astype(o_ref.dtype)


def _gqa_attention(aq, ak, av):
    b, nt, nq, _ = aq.shape
    t = nt * TM
    assert nt % 2 == 1
    return pl.pallas_call(
        functools.partial(_gqa_kernel, n_kv=nt),
        grid=(b, nt),
        in_specs=[pl.BlockSpec((1, 1, nq, TM), lambda bb, i: (bb, i, 0, 0)),
                  pl.BlockSpec((1, t, 128), lambda bb, i: (bb, 0, 0)),
                  pl.BlockSpec((1, nt, GQA_KV_HEADS * GQA_VROWS, TM), lambda bb, i: (bb, 0, 0, 0))],
        out_specs=pl.BlockSpec((1, 1, nq, TM), lambda bb, i: (bb, i, 0, 0)),
        out_shape=jax.ShapeDtypeStruct((b, nt, nq, TM), BF16),
        scratch_shapes=_attn_scratch(GQA_VROWS),
        compiler_params=_params("parallel", "arbitrary"),
        name="gqa_attn",
    )(aq, ak, av)


def _diff_kernel(q_ref, k_ref, v_ref, lam_ref, gain_ref, o_ref, qpad_ref, m_ref, acc_ref, s0_ref, s1_ref,
                 c0_ref, c1_ref, *, n_kv, lam_init):
    _attn_core(q_ref, k_ref, v_ref, qpad_ref, m_ref, acc_ref, (s0_ref, s1_ref), (c0_ref, c1_ref), n_kv=n_kv,
               k_col=lambda hc: (hc // 2) * 128, q_slot=lambda hc: hc % 2, v_row=lambda hc: (hc // 2) * DIFF_VROWS,
               v_rows=DIFF_VROWS, scores_between=False)
    lp = lam_ref[...]
    lam = (jnp.exp(jnp.sum(lp[0:1] * lp[1:2], axis=1, keepdims=True))
           - jnp.exp(jnp.sum(lp[2:3] * lp[3:4], axis=1, keepdims=True)) + lam_init)
    gain = gain_ref[...]
    nv = 2 * HEAD_DIM
    for hh in range(DIFF_HEADS):
        a0, a1 = acc_ref[2 * hh], acc_ref[2 * hh + 1]
        o = a0[:nv] / a0[nv:nv + 1] - lam * (a1[:nv] / a1[nv:nv + 1])
        y = o * lax.rsqrt(jnp.mean(o * o, axis=0, keepdims=True) + NORM_EPS) * gain
        o_ref[0, 0, hh * nv:(hh + 1) * nv, :] = (y * (1.0 - lam_init)).astype(o_ref.dtype)


def _diff_attention(dq, dk, dv, lam_p, gain, layer_idx):
    b, nt, nq, _ = dq.shape
    t = nt * TM
    assert nt % 2 == 1
    lam_init = 0.8 - 0.6 * math.exp(-0.3 * layer_idx)
    return pl.pallas_call(
        functools.partial(_diff_kernel, n_kv=nt, lam_init=lam_init),
        grid=(b, nt),
        in_specs=[pl.BlockSpec((1, 1, nq, TM), lambda bb, i: (bb, i, 0, 0)),
                  pl.BlockSpec((1, t, DIFF_QK), lambda bb, i: (bb, 0, 0)),
                  pl.BlockSpec((1, nt, DIFF_HEADS * DIFF_VROWS, TM), lambda bb, i: (bb, 0, 0, 0)),
                  _const_spec(lam_p.shape),
                  _const_spec((2 * HEAD_DIM, 1))],
        out_specs=pl.BlockSpec((1, 1, DIFF_V, TM), lambda bb, i: (bb, i, 0, 0)),
        out_shape=jax.ShapeDtypeStruct((b, nt, DIFF_V, TM), BF16),
        scratch_shapes=_attn_scratch(DIFF_VROWS),
        compiler_params=_params("parallel", "arbitrary"),
        name="diff_attn",
    )(dq, dk, dv, lam_p, gain.reshape(2 * HEAD_DIM, 1))


def _split_bf16(x):
    hi = x.astype(BF16)
    return hi, (x - hi.astype(F32)).astype(BF16)


def _gla_kernel(qk_ref, la_ref, vt_ref, o_ref, st_ref, *, backward):
    @pl.when(pl.program_id(1) == 0)
    def _():
        st_ref[...] = jnp.zeros(st_ref.shape, F32)

    n_chunk = TM // GLA_CHUNK
    r = lax.broadcasted_iota(jnp.int32, (TM, TM), 0)
    c = lax.broadcasted_iota(jnp.int32, (TM, TM), 1)
    same = (r // GLA_CHUNK) == (c // GLA_CHUNK)
    causal = (c >= r) if backward else (c <= r)
    tri = jnp.where(same & causal, 1.0, 0.0).astype(BF16)
    blk = jnp.where(same, 1.0, 0.0).astype(BF16)
    la_hi, la_lo = _split_bf16(la_ref[0])
    cum = jnp.dot(tri, la_hi, preferred_element_type=F32) + jnp.dot(tri, la_lo, preferred_element_type=F32)
    tot = jnp.dot(blk, la_hi, preferred_element_type=F32) + jnp.dot(blk, la_lo, preferred_element_type=F32)
    q = qk_ref[0, :, :GLA_QK]
    k = qk_ref[0, :, GLA_QK:]
    q_in = (q * jnp.exp(cum)).astype(BF16)
    k_in = k * jnp.exp(-cum)
    k_end = k * jnp.exp(tot - cum)
    dec = jnp.exp(tot)
    lane = lax.broadcasted_iota(jnp.int32, (TM, 128), 1)
    row = lax.broadcasted_iota(jnp.int32, (TM, 128), 0)
    keep = same & ((r >= c) if backward else (r <= c))
    order = range(n_chunk - 1, -1, -1) if backward else range(n_chunk)
    hs = range(GLA_HEADS)
    pairs = [slice((h // 2) * 128, (h // 2 + 1) * 128) for h in hs]
    head_lanes = [(lane // GLA_DK) == (h % 2) for h in hs]
    q_pair = [q_in[:, pairs[h]] for h in hs]
    k_in_h = [jnp.where(head_lanes[h], k_in[:, pairs[h]], 0.0).astype(BF16) for h in hs]
    k_end_h = [jnp.where(head_lanes[h], k_end[:, pairs[h]], 0.0) for h in hs]
    vts = [vt_ref[0, 0, h * GLA_DV:(h + 1) * GLA_DV, :] for h in hs]
    a_ts = [jnp.where(keep, _nt(k_in_h[h], q_pair[h]), 0.0).astype(BF16) for h in hs]
    o_ts = [jnp.dot(vts[h], a_ts[h], preferred_element_type=F32) for h in hs]
    s_ts = [st_ref[h] for h in hs]
    for ci in order:
        in_chunk = (row // GLA_CHUNK) == ci
        for h in hs:
            q_c = jnp.where(in_chunk, q_pair[h], jnp.zeros_like(q_pair[h]))
            o_ts[h] = o_ts[h] + _nt(s_ts[h].astype(BF16), q_c)
            k_c = jnp.where(in_chunk, k_end_h[h], 0.0).astype(BF16)
            s_ts[h] = (s_ts[h] * dec[ci * GLA_CHUNK:ci * GLA_CHUNK + 1, pairs[h]]
                       + jnp.dot(vts[h], k_c, preferred_element_type=F32))
    for h in hs:
        st_ref[h] = s_ts[h]
        o_ref[0, 0, h * GLA_DV:(h + 1) * GLA_DV, :] = o_ts[h]


def _gla_direction(qk, la, vt, backward):
    b, t, _ = qk.shape
    nt = t // TM
    d = 1 if backward else 0
    tile = (lambda i: jnp.where(i == 0, 0, nt - i)) if backward else (lambda i: i)
    return pl.pallas_call(
        functools.partial(_gla_kernel, backward=backward),
        grid=(b, nt),
        in_specs=[pl.BlockSpec((1, TM, 2 * GLA_QK), lambda bb, i: (bb, tile(i), 0)),
                  pl.BlockSpec((1, TM, GLA_QK), lambda bb, i: (bb, tile(i), d)),
                  pl.BlockSpec((1, 1, GLA_V, TM), lambda bb, i: (bb, tile(i), 0, 0))],
        out_specs=pl.BlockSpec((1, 1, GLA_V, TM), lambda bb, i: (bb, tile(i), 0, 0)),
        out_shape=jax.ShapeDtypeStruct((b, nt, GLA_V, TM), F32),
        scratch_shapes=[pltpu.VMEM((GLA_HEADS, GLA_DV, 128), F32)],
        compiler_params=_params("parallel", "arbitrary"),
        name="gla_bwd" if backward else "gla_fwd",
    )(qk, la, vt)


def _post_phases(xs, mods, g, yas, ybs, woa_ref, wob_ref, w1_ref, w2_ref, ff_chunk):
    n = len(xs)
    ys = [jnp.dot(yas[s], woa_ref[...], preferred_element_type=F32)
          + jnp.dot(ybs[s].astype(F32).T.astype(BF16), wob_ref[...], preferred_element_type=F32) for s in range(n)]
    x1s = [xs[s] + mods[s][2:3] * _rms(ys[s], g[1:2]) for s in range(n)]
    hs = [(_rms(x1s[s], g[2:3]) * (1.0 + mods[s][4:5]) + mods[s][3:4]).astype(BF16) for s in range(n)]
    d_ff = w1_ref.shape[1]
    accs = [jnp.zeros(xs[0].shape, F32) for _ in range(n)]
    for c in range(d_ff // ff_chunk):
        cols = slice(c * ff_chunk, (c + 1) * ff_chunk)
        for s in range(n):
            u = jnp.dot(hs[s], w1_ref[:, cols], preferred_element_type=F32)
            u = jnp.square(jnp.maximum(u, 0.0)).astype(BF16)
            accs[s] = accs[s] + jnp.dot(u, w2_ref[cols, :], preferred_element_type=F32)
    return [x1s[s] + mods[s][5:6] * _rms(accs[s], g[3:4]) for s in range(n)]


def _post_gla_kernel(xs_ref, of_ref, ob_ref, gg_ref, gain_ref, yb_ref, mods_ref, g_ref, woa_ref, wob_ref, w1_ref,
                     w2_ref, o_ref, *, ff_chunk):
    gain = gain_ref[...]
    yas = []
    for s in range(POST_PB):
        parts = []
        for hh in range(GLA_HEADS):
            rows = slice(hh * GLA_DV, (hh + 1) * GLA_DV)
            o = of_ref[s, 0, rows, :] + ob_ref[s, 0, rows, :]
            gt = gg_ref[s, 0, rows, :]
            y = o * lax.rsqrt(jnp.mean(o * o, axis=0, keepdims=True) + NORM_EPS) * gain
            parts.append(y * (gt / (1.0 + jnp.exp(-gt))))
        yas.append(jnp.concatenate(parts, axis=0).T.astype(BF16))
    outs = _post_phases([xs_ref[s] for s in range(POST_PB)], [mods_ref[s] for s in range(POST_PB)], g_ref[...], yas,
                        [yb_ref[s, 0] for s in range(POST_PB)], woa_ref, wob_ref, w1_ref, w2_ref, ff_chunk)
    for s in range(POST_PB):
        o_ref[s] = outs[s]


def _post_kernel(xs_ref, ya_ref, yb_ref, mods_ref, g_ref, woa_ref, wob_ref, w1_ref, w2_ref, o_ref, *, ff_chunk):
    outs = _post_phases([xs_ref[s] for s in range(POST_PB)], [mods_ref[s] for s in range(POST_PB)], g_ref[...],
                        [ya_ref[s].astype(BF16) for s in range(POST_PB)], [yb_ref[s, 0] for s in range(POST_PB)],
                        woa_ref, wob_ref, w1_ref, w2_ref, ff_chunk)
    for s in range(POST_PB):
        o_ref[s] = outs[s]


def _post(xs, ya, yb_t, mods_l, g, w_out, w1, w2, skip_ctx):
    b, t, d = xs.shape
    nt = t // TM
    pb = POST_PB
    half = w_out.shape[0] // 2
    woa = w_out[:half].astype(BF16)
    wob = w_out[half:].astype(BF16)
    w1 = w1.astype(BF16)
    w2 = w2.astype(BF16)
    off = 1 if skip_ctx else 0
    tok = lambda n: pl.BlockSpec((pb, TM, n), lambda bb, i: (bb, i + off, 0))
    tr = lambda n: pl.BlockSpec((pb, 1, n, TM), lambda bb, i: (bb, i + off, 0, 0))
    if isinstance(ya, tuple):
        o_f, o_b, gg, gain = ya
        body, name = _post_gla_kernel, "post_gla"
        a_specs = [tr(half), tr(half), tr(half), _const_spec((GLA_DV, 1))]
        a_args = [o_f, o_b, gg, gain.reshape(GLA_DV, 1)]
    else:
        body, name = _post_kernel, "post"
        a_specs, a_args = [tok(half)], [ya]
    return pl.pallas_call(
        functools.partial(body, ff_chunk=1024),
        grid=(b // pb, nt - off),
        in_specs=[tok(d)] + a_specs + [
                  tr(half),
                  pl.BlockSpec((pb, N_MOD, d), lambda bb, i: (_mod_block(bb, i + off, pb), 0, 0)),
                  _const_spec(g.shape),
                  _const_spec(woa.shape), _const_spec(wob.shape), _const_spec(w1.shape), _const_spec(w2.shape)],
        out_specs=pl.BlockSpec((pb, TM, d), lambda bb, i: (bb, i, 0)),
        out_shape=jax.ShapeDtypeStruct((b, t - off * TM, d), F32),
        compiler_params=_params("parallel", "parallel"),
        name=name,
    )(xs, *a_args, yb_t, mods_l, g, woa, wob, w1, w2)


def _short_conv_kernel(u_ref, prev_ref, next_ref, w_ref, b_ref, v_ref, x1_ref, x2_ref, *, n_tiles):
    i = pl.program_id(1)
    u = u_ref[0]
    rows = lax.broadcasted_iota(jnp.int32, u.shape, 0)
    prev_row = jnp.where(i == 0, 0.0, prev_ref[0, 7:8, :])
    next_row = jnp.where(i == n_tiles - 1, 0.0, next_ref[0, 0:1, :])
    up = jnp.where(rows == 0, prev_row, pltpu.roll(u, 1, axis=0))
    dn = jnp.where(rows == TM - 1, next_row, pltpu.roll(u, TM - 1, axis=0))
    w = w_ref[...]
    s = up * w[0:1] + u * w[1:2] + dn * w[2:3] + b_ref[...]
    v_ref[0] = s[:, :HY_CH]
    x1_ref[0] = s[:, HY_CH:2 * HY_CH]
    x2_ref[0] = s[:, 2 * HY_CH:]


def _short_conv(u, conv_w, conv_b, first_tile, n_tiles):
    b, t, c = u.shape
    r8 = TM // 8
    last_blk = t // 8 - 1
    out = jax.ShapeDtypeStruct((b, n_tiles * TM, HY_CH), F32)
    ospec = pl.BlockSpec((1, TM, HY_CH), lambda bb, i: (bb, i, 0))
    return pl.pallas_call(
        functools.partial(_short_conv_kernel, n_tiles=n_tiles),
        grid=(b, n_tiles),
        in_specs=[pl.BlockSpec((1, TM, c), lambda bb, i: (bb, i + first_tile, 0)),
                  pl.BlockSpec((1, 8, c), lambda bb, i: (bb, jnp.maximum((i + first_tile) * r8 - 1, 0), 0)),
                  pl.BlockSpec((1, 8, c), lambda bb, i: (bb, jnp.minimum((i + first_tile + 1) * r8, last_blk), 0)),
                  _const_spec(conv_w.shape), _const_spec((1, c))],
        out_specs=[ospec, ospec, ospec],
        out_shape=[out, out, out],
        compiler_params=_params("parallel", "parallel"),
        name="hy_short",
    )(u, u, u, conv_w, conv_b.reshape(1, c))


def _dot3(a, b):
    ah, al = _split_bf16(a)
    bh, bl = _split_bf16(b)
    d = lambda x, y: jnp.dot(x, y, preferred_element_type=F32)
    return d(ah, bh) + (d(al, bh) + d(ah, bl))


def _filter_kernel(w1t_ref, w1c_ref, w1s_ref, b1_ref, w2_ref, b2_ref, w3_ref, delta_ref, o_ref, *, seq, rb):
    n = pl.program_id(0) * rb + lax.broadcasted_iota(jnp.int32, (rb, 1), 0)
    d = jnp.where(n < seq, n, 2 * seq - n).astype(F32)
    t = d / max(seq - 1, 1)
    w = (2 * math.pi / seq) * d
    band = lax.broadcasted_iota(jnp.int32, (1, HY_BANDS), 1).astype(F32)
    bands = 1e-4 + band * ((HY_BANDS - 1 - 1e-4) / (HY_BANDS - 1))
    ang = w * bands
    pre = t * w1t_ref[...] + _dot3(jnp.cos(ang), w1c_ref[...]) - _dot3(jnp.sin(ang), w1s_ref[...]) + b1_ref[...]
    hid = jnp.sin(pre)
    hid = jnp.sin(_dot3(hid, w2_ref[...]) + b2_ref[...])
    decay = jnp.where(n == seq, 0.0, jnp.exp(-t * delta_ref[...]))
    for o in range(HY_ORDER):
        o_ref[o] = (_dot3(hid, w3_ref[0, o]) * decay).astype(o_ref.dtype)


def _hyena_kernels(seq, f_w1, f_b1, f_w2, f_b2, f_w3):
    rb = min(512, seq)
    nblk = 2 * seq // rb
    ffn = f_w2.shape[0]
    w3 = f_w3.reshape(ffn, HY_ORDER, 2, HY_CH).transpose(2, 1, 0, 3)
    deltas = jnp.abs(jnp.linspace(math.log(HY_TARGET) / HY_FAST_PCT, math.log(HY_TARGET) / HY_SLOW_PCT,
                                  HY_CH, dtype=F32)).reshape(1, HY_CH)
    return pl.pallas_call(
        functools.partial(_filter_kernel, seq=seq, rb=rb),
        grid=(nblk,),
        in_specs=[_const_spec((1, ffn)), _const_spec((HY_BANDS, ffn)), _const_spec((HY_BANDS, ffn)),
                  _const_spec((1, ffn)), _const_spec((ffn, ffn)), _const_spec((1, ffn)),
                  pl.BlockSpec((1, HY_ORDER, ffn, HY_CH), lambda j: ((2 * j) // nblk, 0, 0, 0)),
                  _const_spec((1, HY_CH))],
        out_specs=pl.BlockSpec((HY_ORDER, rb, HY_CH), lambda j: (0, j, 0)),
        out_shape=jax.ShapeDtypeStruct((HY_ORDER, 2 * seq, HY_CH), BF16),
        compiler_params=_params("parallel"),
        name="hy_filter",
    )(f_w1[0:1], f_w1[1:1 + HY_BANDS], f_w1[1 + HY_BANDS:], f_b1.reshape(1, ffn), f_w2, f_b2.reshape(1, ffn), w3,
      deltas)


def _dft_tables(seq):
    n = 2 * seq
    r = math.isqrt(n)
    assert r * r == n
    idx = jnp.arange(r, dtype=jnp.int32)
    phi = (2 * math.pi / r) * ((idx[:, None] * idx[None, :]) % r).astype(F32)
    f1 = jnp.concatenate([jnp.cos(phi), -jnp.sin(phi)], axis=0).astype(BF16)
    m = (idx[:, None, None] + r * idx[None, :, None]) * idx[None, None, :]
    th = (2 * math.pi / n) * (m % n).astype(F32)
    c, s = jnp.cos(th), jnp.sin(th)
    blockm = lambda re, im: jnp.concatenate([jnp.concatenate([re, -im], axis=-1),
                                             jnp.concatenate([im, re], axis=-1)], axis=-2)
    g = blockm(c, -s).astype(BF16)
    ginv = blockm(jnp.swapaxes(c, 1, 2), jnp.swapaxes(s, 1, 2)).astype(BF16)
    finv = (jnp.concatenate([jnp.cos(phi.T), -jnp.sin(phi.T)], axis=1)[:r // 2] / n).astype(BF16)
    return r, f1, g, ginv, finv


DFT_BB = 8


def _dft1_kernel(f_ref, x_ref, o_ref):
    f = f_ref[...]
    r = o_ref.shape[2]
    x = pltpu.einshape("abc->bac", x_ref[0].astype(F32))
    ys = [jnp.dot(f, x[j].astype(BF16), preferred_element_type=F32) for j in range(DFT_BB)]
    y = pltpu.einshape("jkc->kjc", jnp.stack(ys, axis=0))
    o_ref[0, 0] = y[:r].astype(o_ref.dtype)
    o_ref[0, 1] = y[r:].astype(o_ref.dtype)


def _dft_stage1(f, x4):
    b, ka, r, c = x4.shape
    return pl.pallas_call(
        _dft1_kernel,
        grid=(b, r // DFT_BB),
        in_specs=[_const_spec(f.shape), pl.BlockSpec((1, ka, DFT_BB, c), lambda bi, j: (bi, 0, j, 0))],
        out_specs=pl.BlockSpec((1, 2, r, DFT_BB, c), lambda bi, j: (bi, 0, 0, j, 0)),
        out_shape=jax.ShapeDtypeStruct((b, 2, r, r, c), BF16),
        compiler_params=_params("parallel", "parallel"),
        name="hy_dft1",
    )(f, x4)


def _fspec_kernel(a_ref, g_ref, o_ref, *, kb, r):
    for kk in range(kb):
        a = a_ref[0, :, kk].reshape(2 * r, a_ref.shape[-1])
        o_ref[0, kk] = jnp.dot(g_ref[kk], a, preferred_element_type=F32)


def _filter_spectrum(a, g, kb):
    o, _, r, _, c = a.shape
    return pl.pallas_call(
        functools.partial(_fspec_kernel, kb=kb, r=r),
        grid=(o, r // kb),
        in_specs=[pl.BlockSpec((1, 2, kb, r, c), lambda oo, j: (oo, 0, j, 0, 0)),
                  pl.BlockSpec((kb, 2 * r, 2 * r), lambda oo, j: (j, 0, 0))],
        out_specs=pl.BlockSpec((1, kb, 2 * r, c), lambda oo, j: (oo, j, 0, 0)),
        out_shape=jax.ShapeDtypeStruct((o, r, 2 * r, c), F32),
        compiler_params=_params("parallel", "parallel"),
        name="hy_fspec",
    )(a, g)


def _dft2_kernel(a_ref, g_ref, gi_ref, k_ref, o_ref, *, kb, r):
    xs = [jnp.dot(g_ref[kk], a_ref[0, :, kk].reshape(2 * r, a_ref.shape[-1]), preferred_element_type=F32)
          for kk in range(kb)]
    ys = []
    for kk in range(kb):
        x, ks = xs[kk], k_ref[0, kk]
        xr, xi, kr, ki = x[:r], x[r:], ks[:r], ks[r:]
        ys.append(jnp.concatenate([xr * kr - xi * ki, xr * ki + xi * kr], axis=0).astype(BF16))
    bqs = [jnp.dot(gi_ref[kk], ys[kk], preferred_element_type=F32) for kk in range(kb)]
    for kk in range(kb):
        o_ref[0, 0, kk] = bqs[kk][:r].astype(o_ref.dtype)
        o_ref[0, 1, kk] = bqs[kk][r:].astype(o_ref.dtype)


def _dft_stage2(a, g, ginv, kspec, order, kb):
    b, _, r, _, c = a.shape
    blk = pl.BlockSpec((1, 2, kb, r, c), lambda j, bb: (bb, 0, j, 0, 0))
    tab = pl.BlockSpec((kb, 2 * r, 2 * r), lambda j, bb: (j, 0, 0))
    return pl.pallas_call(
        functools.partial(_dft2_kernel, kb=kb, r=r),
        grid=(r // kb, b),
        in_specs=[blk, tab, tab, pl.BlockSpec((1, kb, 2 * r, c), lambda j, bb: (order, j, 0, 0))],
        out_specs=blk,
        out_shape=jax.ShapeDtypeStruct(a.shape, BF16),
        compiler_params=_params("parallel", "arbitrary"),
        name="hy_dft2",
    )(a, g, ginv, kspec)


def _dft3_kernel(f_ref, b_ref, gate_ref, z_ref, bias_ref, o_ref):
    f = f_ref[...]
    bq = jnp.concatenate([b_ref[0, 0], b_ref[0, 1]], axis=0)
    bq = pltpu.einshape("kjc->jkc", bq.astype(F32))
    ys = [jnp.dot(f, bq[j].astype(BF16), preferred_element_type=F32) for j in range(DFT_BB)]
    y = pltpu.einshape("jac->ajc", jnp.stack(ys, axis=0))
    o_ref[0] = gate_ref[0] * (y + z_ref[0] * bias_ref[...])


def _dft_stage3(finv, bq5, gate4, z4, bias):
    b, _, r, _, c = bq5.shape
    ra = finv.shape[0]
    io = pl.BlockSpec((1, ra, DFT_BB, c), lambda bi, j: (bi, 0, j, 0))
    return pl.pallas_call(
        _dft3_kernel,
        grid=(b, r // DFT_BB),
        in_specs=[_const_spec(finv.shape), pl.BlockSpec((1, 2, r, DFT_BB, c), lambda bi, j: (bi, 0, 0, j, 0)), io, io,
                  _const_spec((1, 1, c))],
        out_specs=io,
        out_shape=jax.ShapeDtypeStruct((b, ra, r, c), F32),
        compiler_params=_params("parallel", "parallel"),
        name="hy_dft3",
    )(finv, bq5, gate4, z4, bias)


def _long_conv_gated(v, gates, kern, hy_bias):
    b, seq, c = v.shape
    r, f1, g, ginv, finv = _dft_tables(seq)
    kb = 8
    kspec = _filter_spectrum(_dft_stage1(f1, kern.reshape(HY_ORDER, r, r, c)), g, kb)
    z = v.reshape(b, r // 2, r, c)
    for o in range(HY_ORDER):
        a = _dft_stage1(f1[:, :r // 2], z)
        bq = _dft_stage2(a, g, ginv, kspec, o, kb)
        z = _dft_stage3(finv, bq, gates[o].reshape(b, r // 2, r, c), z, hy_bias[o].reshape(1, 1, c))
    return z.reshape(b, seq, c)


def _ctx_conv_kernel(v_ref, x1_ref, x2_ref, k_ref, ff_ref, fi_ref, bias_ref, o_ref):
    n = ff_ref.shape[1]
    z = v_ref[0]
    for o, gate_ref in enumerate((x1_ref, x2_ref)):
        ks = jnp.dot(ff_ref[...], k_ref[o], preferred_element_type=F32)
        x = jnp.dot(ff_ref[:, :TM], z.astype(BF16), preferred_element_type=F32)
        xr, xi, kr, ki = x[:n], x[n:], ks[:n], ks[n:]
        y = jnp.concatenate([xr * kr - xi * ki, xr * ki + xi * kr], axis=0).astype(BF16)
        conv = jnp.dot(fi_ref[...], y, preferred_element_type=F32)
        z = gate_ref[0] * (conv + z * bias_ref[o:o + 1])
    o_ref[0] = z


def _ctx_conv_gated(v, x1, x2, kern, hy_bias):
    b, seq, c = v.shape
    n = 2 * seq
    idx = jnp.arange(n, dtype=jnp.int32)
    th = (2 * math.pi / n) * ((idx[:, None] * idx[None, :]) % n).astype(F32)
    ff = jnp.concatenate([jnp.cos(th), -jnp.sin(th)], axis=0).astype(BF16)
    fi = (jnp.concatenate([jnp.cos(th), -jnp.sin(th)], axis=1)[:seq] / n).astype(BF16)
    io = pl.BlockSpec((1, seq, c), lambda bb: (bb, 0, 0))
    return pl.pallas_call(
        _ctx_conv_kernel,
        grid=(b,),
        in_specs=[io, io, io, _const_spec(kern.shape), _const_spec(ff.shape), _const_spec(fi.shape),
                  _const_spec(hy_bias.shape)],
        out_specs=io,
        out_shape=jax.ShapeDtypeStruct((b, seq, c), F32),
        compiler_params=_params("parallel"),
        name="hy_ctx",
    )(v, x1, x2, kern, ff, fi, hy_bias)


def _hyena(u, lc, conv_w, conv_b, f_w1, f_b1, f_w2, f_b2, f_w3, hy_bias):
    seq = u.shape[1] - lc
    vc, x1c, x2c = _short_conv(u, conv_w, conv_b, 0, lc // TM)
    vx, x1x, x2x = _short_conv(u, conv_w, conv_b, lc // TM, seq // TM)
    yc = _ctx_conv_gated(vc, x1c, x2c, _hyena_kernels(lc, f_w1, f_b1, f_w2, f_b2, f_w3), hy_bias)
    yx = _long_conv_gated(vx, (x1x, x2x), _hyena_kernels(seq, f_w1, f_b1, f_w2, f_b2, f_w3), hy_bias)
    return jnp.concatenate([yc, yx], axis=1)


def _rope_tables_t(seq, lc):
    rows = seq // GRID_W
    r = jnp.broadcast_to(jnp.arange(rows, dtype=F32)[:, None], (rows, GRID_W)).reshape(-1)
    col = jnp.broadcast_to(jnp.arange(GRID_W, dtype=F32)[None, :], (rows, GRID_W)).reshape(-1)
    n_pairs = HEAD_DIM // 4
    inv = ROPE_THETA ** (-jnp.arange(n_pairs, dtype=F32) / n_pairs)
    ang = jnp.concatenate([r[:, None] * inv, col[:, None] * inv], axis=-1)
    cos = jnp.concatenate([jnp.ones((lc, 2 * n_pairs), F32), jnp.cos(ang)], axis=0)
    sin = jnp.concatenate([jnp.zeros((lc, 2 * n_pairs), F32), jnp.sin(ang)], axis=0)
    return cos.T, sin.T


def kernel(x, c, ctx, c_ctx, w_ada, b_ada, norm_g, w_mlp_in, w_mlp_out, ev_w_in, ev_w_lr, ev_b_lr, ev_gla_g,
           ev_qk_g, ev_w_out, od_w_in, od_conv_w, od_conv_b, od_f_w1, od_f_b1, od_f_w2, od_f_b2, od_f_w3,
           od_hy_bias, od_lam, od_diff_g, od_w_out):
    bsz, seq, d = x.shape
    lc = ctx.shape[1]
    depth = w_ada.shape[0]
    assert lc == TM and seq % TM == 0 and bsz <= MOD_CTX_ROW and bsz % POST_PB == 0 and bsz % IN_PB == 0
    cos_t, sin_t = _rope_tables_t(seq, lc)
    cc = jnp.zeros((8, d), F32).at[:bsz].set(c).at[MOD_CTX_ROW:MOD_CTX_ROW + max(POST_PB, IN_PB)].set(c_ctx)
    mods = _mods(cc, w_ada, b_ada).reshape(depth, 8, N_MOD, d)
    xs = jnp.concatenate([ctx, x], axis=1)
    for i in range(depth):
        last = i == depth - 1
        j = i // 2
        g = norm_g[i]
        if i % 2 == 0:
            qk, la, gv, gg, aq, ak, av = _even_in(xs, mods[i], g[0:1], ev_w_in[j], ev_w_lr[j], ev_b_lr[j],
                                                  ev_qk_g[j], cos_t, sin_t)
            ya = (_gla_direction(qk, la, gv, False), _gla_direction(qk, la, gv, True), gg, ev_gla_g[j])
            yb = _gqa_attention(aq, ak, av)
            w_out = ev_w_out[j]
        else:
            u, dq, dk, dv = _odd_in(xs, mods[i], g[0:1], od_w_in[j], cos_t, sin_t)
            ya = _hyena(u, lc, od_conv_w[j], od_conv_b[j], od_f_w1[j], od_f_b1[j], od_f_w2[j], od_f_b2[j],
                        od_f_w3[j], od_hy_bias[j])
            yb = _diff_attention(dq, dk, dv, od_lam[j], od_diff_g[j], i)
            w_out = od_w_out[j]
        xs = _post(xs, ya, yb, mods[i], g, w_out, w_mlp_in[i], w_mlp_out[i], skip_ctx=last)
    return xs
```

```python
import functools
import math

import jax
import jax.numpy as jnp
import numpy as np
from jax import lax
from jax.experimental import pallas as pl
from jax.experimental.pallas import tpu as pltpu

F32 = jnp.float32
BF16 = jnp.bfloat16

TM = 256
GRID_W = 64
HEAD_DIM = 64
ROPE_THETA = 10000.0
NORM_EPS = 1e-6
N_MOD = 6
VMEM_LIMIT = 56 * 1024 * 1024

GLA_HEADS = 4
GLA_DK = 64
GLA_DV = 128
GLA_RANK = 16
GLA_NORMALIZER = 16.0
GLA_CHUNK = 64
GLA_QK = GLA_HEADS * GLA_DK
GLA_V = GLA_HEADS * GLA_DV
GQA_HEADS = 8
GQA_KV_HEADS = 2
HY_CH = 512
HY_ORDER = 2
HY_BANDS = 16
HY_TARGET = 1e-2
HY_FAST_PCT = 0.3
HY_SLOW_PCT = 1.5
HY_COLS = (HY_ORDER + 1) * HY_CH
DIFF_HEADS = 4
DIFF_QK = DIFF_HEADS * 2 * HEAD_DIM
DIFF_V = DIFF_HEADS * 2 * HEAD_DIM
ONES_ROWS = 16
N_SCORE_HEADS = 8
LOG2E = 1.4426950408889634
Q_SCALE = HEAD_DIM ** -0.5 * LOG2E
GQA_VROWS = HEAD_DIM + ONES_ROWS
SCORE_BOUND_SLACK = 1.01
SCORE_BOUND_LIMIT = 60.0
DIFF_VROWS = 2 * HEAD_DIM + ONES_ROWS


def _params(*sem):
    return pltpu.CompilerParams(dimension_semantics=sem, vmem_limit_bytes=VMEM_LIMIT)


def _const_spec(shape):
    nd = len(shape)
    return pl.BlockSpec(shape, lambda *_: (0,) * nd, pipeline_mode=pl.Buffered(1))


def _rms(x, g):
    return x * lax.rsqrt(jnp.mean(x * x, axis=-1, keepdims=True) + NORM_EPS) * g


MOD_CTX_ROW = 4
POST_PB = 2
IN_PB = 2


def _mod_block(bb, i, pb):
    return jnp.where(i == 0, MOD_CTX_ROW // pb, bb)


def _mods_kernel(cc_ref, w_ref, b_ref, o_ref):
    cc = cc_ref[...]
    s = cc / (1.0 + jnp.exp(-cc))
    o_ref[0] = jnp.dot(s.astype(BF16), w_ref[0].astype(BF16), preferred_element_type=F32) + b_ref[0]


def _mods(cc, w_ada, b_ada):
    depth, d, n = w_ada.shape
    nb = n // 4
    return pl.pallas_call(
        _mods_kernel,
        grid=(depth, n // nb),
        in_specs=[pl.BlockSpec((8, d), lambda l, j: (0, 0)),
                  pl.BlockSpec((1, d, nb), lambda l, j: (l, 0, j)),
                  pl.BlockSpec((1, 1, nb), lambda l, j: (l, 0, j))],
        out_specs=pl.BlockSpec((1, 8, nb), lambda l, j: (l, 0, j)),
        out_shape=jax.ShapeDtypeStruct((depth, 8, n), F32),
        compiler_params=_params("parallel", "parallel"),
        name="ada_mods",
    )(cc, w_ada, b_ada.reshape(depth, 1, n))


def _normed_input(x, mods, g):
    return (_rms(x, g) * (1.0 + mods[1:2]) + mods[0:1]).astype(BF16)


def _rope_rows(y, cos, sin):
    x1, x2 = y[:32], y[32:]
    return x1 * cos - x2 * sin, x1 * sin + x2 * cos


def _log_sigmoid(x):
    return jnp.minimum(x, 0.0) - jnp.log(1.0 + jnp.exp(-jnp.abs(x)))


def _nt(a, b):
    return lax.dot_general(a, b, (((1,), (1,)), ((), ())), preferred_element_type=F32)


def _max_sq_norm_row(parts):
    n2 = sum(jnp.sum(jnp.square(p.astype(BF16).astype(F32)), axis=0, keepdims=True) for p in parts)
    return jnp.broadcast_to(jnp.max(n2, axis=1, keepdims=True), (1, 128))


def _even_in_kernel(xs_ref, mods_ref, g_ref, wn_ref, wt_ref, wlr_ref, blr_ref, gq_ref, gk_ref,
                    cos_ref, sin_ref, qk_ref, la_ref, gv_ref, gg_ref, aq_ref, ak_ref, av_ref, kn_ref):
    hs = [_normed_input(xs_ref[s], mods_ref[s], g_ref[0:1]) for s in range(IN_PB)]
    ps = [jnp.dot(h, wn_ref[...], preferred_element_type=F32) for h in hs]
    pts = [_nt(wt_ref[...], h) for h in hs]
    cos, sin = cos_ref[...], sin_ref[...]
    gq, gk_gain = gq_ref[...], gk_ref[...]
    ones = jnp.ones((ONES_ROWS, TM), BF16)
    for s in range(IN_PB):
        p, pt_all = ps[s], pts[s]
        qk_ref[s, :, :GLA_QK] = p[:, :GLA_QK] * GLA_DK ** -0.5
        qk_ref[s, :, GLA_QK:] = p[:, GLA_QK:2 * GLA_QK]
        lr = p[:, 2 * GLA_QK:].astype(BF16)
        gk = jnp.dot(lr, wlr_ref[...], preferred_element_type=F32) + blr_ref[...]
        la_ref[s] = _log_sigmoid(gk) * (1.0 / GLA_NORMALIZER)
        gv_ref[s, 0] = pt_all[:GLA_V].astype(BF16)
        gg_ref[s, 0] = pt_all[GLA_V:2 * GLA_V]
        pt = pt_all[2 * GLA_V:]
        for hh in range(GQA_HEADS):
            blk = pt[hh * 64:(hh + 1) * 64]
            y = blk * lax.rsqrt(jnp.mean(blk * blk, axis=0, keepdims=True) + NORM_EPS) * gq
            o1, o2 = _rope_rows(y, cos, sin)
            aq_ref[s, 0, hh * 64:hh * 64 + 32, :] = (o1 * Q_SCALE).astype(BF16)
            aq_ref[s, 0, hh * 64 + 32:hh * 64 + 64, :] = (o2 * Q_SCALE).astype(BF16)
        kparts, knorms = [], []
        for hh in range(GQA_KV_HEADS):
            blk = pt[512 + hh * 64:512 + (hh + 1) * 64]
            y = blk * lax.rsqrt(jnp.mean(blk * blk, axis=0, keepdims=True) + NORM_EPS) * gk_gain
            o1, o2 = _rope_rows(y, cos, sin)
            kparts += [o1, o2]
            knorms.append(_max_sq_norm_row([o1, o2]))
        ak_ref[s] = jnp.concatenate(kparts, axis=0).T.astype(BF16)
        kn_ref[s, 0] = jnp.concatenate(knorms + [jnp.zeros((8 - GQA_KV_HEADS, 128), F32)], axis=0)
        for hh in range(GQA_KV_HEADS):
            av_ref[s, 0, hh * GQA_VROWS:hh * GQA_VROWS + 64, :] = pt[640 + hh * 64:640 + (hh + 1) * 64].astype(BF16)
            av_ref[s, 0, hh * GQA_VROWS + 64:(hh + 1) * GQA_VROWS, :] = ones


def _odd_in_kernel(xs_ref, mods_ref, g_ref, wn_ref, wt_ref, cos_ref, sin_ref,
                   u_ref, dq_ref, dk_ref, dv_ref, kn_ref):
    hs = [_normed_input(xs_ref[s], mods_ref[s], g_ref[0:1]) for s in range(IN_PB)]
    us = [jnp.dot(h, wn_ref[...], preferred_element_type=F32) for h in hs]
    pts = [_nt(wt_ref[...], h) for h in hs]
    cos, sin = cos_ref[...], sin_ref[...]
    ones = jnp.ones((ONES_ROWS, TM), BF16)
    nv = 2 * HEAD_DIM
    for s in range(IN_PB):
        pt = pts[s]
        u_ref[s] = us[s]
        kparts, knorms = [], []
        for hh in range(2 * DIFF_HEADS):
            o1, o2 = _rope_rows(pt[hh * 64:(hh + 1) * 64], cos, sin)
            dq_ref[s, 0, hh * 64:hh * 64 + 32, :] = (o1 * Q_SCALE).astype(BF16)
            dq_ref[s, 0, hh * 64 + 32:hh * 64 + 64, :] = (o2 * Q_SCALE).astype(BF16)
            k1, k2 = _rope_rows(pt[DIFF_QK + hh * 64:DIFF_QK + (hh + 1) * 64], cos, sin)
            kparts += [k1, k2]
            knorms.append(_max_sq_norm_row([k1, k2]))
        dk_ref[s] = jnp.concatenate(kparts, axis=0).T.astype(BF16)
        kn_ref[s, 0] = jnp.concatenate(knorms, axis=0)
        for hh in range(DIFF_HEADS):
            rows = pt[2 * DIFF_QK + hh * nv:2 * DIFF_QK + (hh + 1) * nv]
            dv_ref[s, 0, hh * DIFF_VROWS:hh * DIFF_VROWS + nv, :] = rows.astype(BF16)
            dv_ref[s, 0, hh * DIFF_VROWS + nv:(hh + 1) * DIFF_VROWS, :] = ones


def _pair_perm(n_heads):
    base = np.concatenate([np.arange(0, HEAD_DIM, 2), np.arange(1, HEAD_DIM, 2)])
    return np.concatenate([h * HEAD_DIM + base for h in range(n_heads)])


def _even_in(xs, mods_l, g0, w_in, w_lr, b_lr, qk_g, cos_t, sin_t):
    b, t, d = xs.shape
    nt = t // TM
    c_v = 2 * GLA_QK
    c_g = c_v + GLA_V
    c_lr0 = c_g + GLA_V
    c_lr = c_lr0 + 2 * GLA_RANK
    wn = jnp.concatenate([w_in[:, :c_v], w_in[:, c_lr0:c_lr], jnp.zeros((d, 128 - 2 * GLA_RANK), F32)],
                         axis=1).astype(BF16)
    cq = c_lr + GQA_HEADS * HEAD_DIM
    ck = cq + GQA_KV_HEADS * HEAD_DIM
    wq = w_in[:, c_lr:cq][:, _pair_perm(GQA_HEADS)]
    wk = w_in[:, cq:ck][:, _pair_perm(GQA_KV_HEADS)]
    wt = jnp.concatenate([w_in[:, c_v:c_lr0], wq, wk, w_in[:, ck:]], axis=1).T.astype(BF16)
    wlr = jnp.zeros((128, 2 * GLA_QK), F32)
    wlr = wlr.at[:GLA_RANK, :GLA_QK].set(w_lr[0]).at[GLA_RANK:2 * GLA_RANK, GLA_QK:].set(w_lr[1]).astype(BF16)
    blr = b_lr.reshape(1, 2 * GLA_QK)
    perm = _pair_perm(1)
    gq = qk_g[0][perm].reshape(HEAD_DIM, 1)
    gk = qk_g[1][perm].reshape(HEAD_DIM, 1)
    tok = lambda n: pl.BlockSpec((IN_PB, TM, n), lambda bb, i: (bb, i, 0))
    tr = lambda n: pl.BlockSpec((IN_PB, 1, n, TM), lambda bb, i: (bb, i, 0, 0))
    return pl.pallas_call(
        _even_in_kernel,
        grid=(b // IN_PB, nt),
        in_specs=[tok(d),
                  pl.BlockSpec((IN_PB, N_MOD, d), lambda bb, i: (_mod_block(bb, i, IN_PB), 0, 0)),
                  _const_spec((1, d)),
                  _const_spec(wn.shape), _const_spec(wt.shape), _const_spec(wlr.shape), _const_spec(blr.shape),
                  _const_spec(gq.shape), _const_spec(gk.shape),
                  pl.BlockSpec((32, TM), lambda bb, i: (0, i)),
                  pl.BlockSpec((32, TM), lambda bb, i: (0, i))],
        out_specs=[tok(2 * GLA_QK), tok(2 * GLA_QK), tr(GLA_V), tr(GLA_V), tr(512), tok(128),
                   tr(GQA_KV_HEADS * GQA_VROWS), pl.BlockSpec((IN_PB, 1, 8, 128), lambda bb, i: (bb, i, 0, 0))],
        out_shape=[jax.ShapeDtypeStruct((b, t, 2 * GLA_QK), F32),
                   jax.ShapeDtypeStruct((b, t, 2 * GLA_QK), F32),
                   jax.ShapeDtypeStruct((b, nt, GLA_V, TM), BF16),
                   jax.ShapeDtypeStruct((b, nt, GLA_V, TM), F32),
                   jax.ShapeDtypeStruct((b, nt, 512, TM), BF16),
                   jax.ShapeDtypeStruct((b, t, 128), BF16),
                   jax.ShapeDtypeStruct((b, nt, GQA_KV_HEADS * GQA_VROWS, TM), BF16),
                   jax.ShapeDtypeStruct((b, nt, 8, 128), F32)],
        compiler_params=_params("parallel", "parallel"),
        name="even_in",
    )(xs, mods_l, g0, wn, wt, wlr, blr, gq, gk, cos_t, sin_t)


def _odd_in(xs, mods_l, g0, w_in, cos_t, sin_t):
    b, t, d = xs.shape
    nt = t // TM
    wn = w_in[:, :HY_COLS].astype(BF16)
    perm = _pair_perm(2 * DIFF_HEADS)
    wq = w_in[:, HY_COLS:HY_COLS + DIFF_QK][:, perm]
    wk = w_in[:, HY_COLS + DIFF_QK:HY_COLS + 2 * DIFF_QK][:, perm]
    wt = jnp.concatenate([wq, wk, w_in[:, HY_COLS + 2 * DIFF_QK:]], axis=1).T.astype(BF16)
    tok = lambda n: pl.BlockSpec((IN_PB, TM, n), lambda bb, i: (bb, i, 0))
    tr = lambda n: pl.BlockSpec((IN_PB, 1, n, TM), lambda bb, i: (bb, i, 0, 0))
    return pl.pallas_call(
        _odd_in_kernel,
        grid=(b // IN_PB, nt),
        in_specs=[tok(d),
                  pl.BlockSpec((IN_PB, N_MOD, d), lambda bb, i: (_mod_block(bb, i, IN_PB), 0, 0)),
                  _const_spec((1, d)),
                  _const_spec(wn.shape), _const_spec(wt.shape),
                  pl.BlockSpec((32, TM), lambda bb, i: (0, i)),
                  pl.BlockSpec((32, TM), lambda bb, i: (0, i))],
        out_specs=[tok(HY_COLS), tr(DIFF_QK), tok(DIFF_QK), tr(DIFF_HEADS * DIFF_VROWS),
                   pl.BlockSpec((IN_PB, 1, 8, 128), lambda bb, i: (bb, i, 0, 0))],
        out_shape=[jax.ShapeDtypeStruct((b, t, HY_COLS), F32),
                   jax.ShapeDtypeStruct((b, nt, DIFF_QK, TM), BF16),
                   jax.ShapeDtypeStruct((b, t, DIFF_QK), BF16),
                   jax.ShapeDtypeStruct((b, nt, DIFF_HEADS * DIFF_VROWS, TM), BF16),
                   jax.ShapeDtypeStruct((b, nt, 8, 128), F32)],
        compiler_params=_params("parallel", "parallel"),
        name="odd_in",
    )(xs, mods_l, g0, wn, wt, cos_t, sin_t)


def _attn_core(q_ref, k_ref, v_ref, kmax_ref, scratch, *, n_kv, k_col, k_row, q_slot, v_row, v_rows, scores_between):
    qpad_ref, bound_ref, m_ref, acc_ref, s0_ref, s1_ref, c0_ref, c1_ref, p0_ref, p1_ref = scratch
    s_refs, c_refs, p_refs = (s0_ref, s1_ref), (c0_ref, c1_ref), (p0_ref, p1_ref)
    i = pl.program_id(1)
    zeros = jnp.zeros((HEAD_DIM, TM), BF16)
    bound_max = jnp.zeros((1, TM), F32)
    for hh in range(N_SCORE_HEADS):
        qh = q_ref[0, 0, hh * HEAD_DIM:(hh + 1) * HEAD_DIM, :]
        qpad_ref[hh] = jnp.concatenate([qh, zeros] if q_slot(hh) == 0 else [zeros, qh], axis=0)
        qf = qh.astype(F32)
        q_norm2 = jnp.sum(qf * qf, axis=0, keepdims=True)
        k_norm2 = kmax_ref[0, k_row(hh):k_row(hh) + 1, 0:1]
        bound = jnp.sqrt(q_norm2 * k_norm2) * SCORE_BOUND_SLACK
        bound_ref[hh] = bound
        bound_max = jnp.maximum(bound_max, bound)
    acc_ref[...] = jnp.zeros(acc_ref.shape, F32)
    n = jnp.where(i == 0, 1, n_kv)
    fixed_reference = jnp.max(bound_max) <= SCORE_BOUND_LIMIT

    def produce(j, hh, slot):
        kc = k_ref[0, pl.ds(pl.multiple_of(j * TM, TM), TM), k_col(hh):k_col(hh) + 128]
        s = jnp.dot(kc, qpad_ref[hh], preferred_element_type=F32)
        p_refs[slot][hh] = jnp.exp2(s - bound_ref[hh]).astype(BF16)

    def reduce(j, hh, slot):
        vc = v_ref[0, j, v_row(hh):v_row(hh) + v_rows, :]
        acc_ref[hh] = acc_ref[hh] + jnp.dot(vc, p_refs[slot][hh], preferred_element_type=F32)

    @pl.when(fixed_reference)
    def _():
        for hh in range(N_SCORE_HEADS):
            produce(0, hh, 0)

        def fast_step(j, slot):
            for hh in range(N_SCORE_HEADS):
                produce(j + 1, hh, 1 - slot)
                reduce(j, hh, slot)

        def fast_body(jj, carry):
            fast_step(2 * jj, 0)
            fast_step(2 * jj + 1, 1)
            return carry

        lax.fori_loop(0, (n - 1) // 2, fast_body, 0)
        for hh in range(N_SCORE_HEADS):
            reduce(n - 1, hh, 0)

    @pl.when(jnp.logical_not(fixed_reference))
    def _():
        _attn_running_max(k_ref, v_ref, qpad_ref, m_ref, acc_ref, s_refs, c_refs, n=n, k_col=k_col, v_row=v_row,
                          v_rows=v_rows, scores_between=scores_between)


def _attn_running_max(k_ref, v_ref, qpad_ref, m_ref, acc_ref, s_refs, c_refs, *, n, k_col, v_row, v_rows,
                      scores_between):
    m_ref[...] = jnp.full(m_ref.shape, -jnp.inf, F32)

    def scores(j, hh, slot):
        kc = k_ref[0, pl.ds(pl.multiple_of(j * TM, TM), TM), k_col(hh):k_col(hh) + 128]
        s = jnp.dot(kc, qpad_ref[hh], preferred_element_type=F32)
        s_refs[slot][hh] = s
        c_refs[slot][hh] = jnp.max(s, axis=0, keepdims=True)

    def probs(hh, slot):
        m_old = m_ref[hh]
        m_new = jnp.maximum(m_old, c_refs[slot][hh])
        return m_old, m_new, jnp.exp2(s_refs[slot][hh] - m_new).astype(BF16)

    def accumulate(j, hh, state):
        m_old, m_new, p = state
        vc = v_ref[0, j, v_row(hh):v_row(hh) + v_rows, :]
        alpha = jnp.exp2(m_old - m_new)
        acc_ref[hh] = acc_ref[hh] * alpha + jnp.dot(vc, p, preferred_element_type=F32)
        m_ref[hh] = m_new

    for hh in range(N_SCORE_HEADS):
        scores(0, hh, 0)

    def step(j, slot):
        for hh in range(N_SCORE_HEADS):
            state = probs(hh, slot)
            if scores_between:
                scores(j + 1, hh, 1 - slot)
            accumulate(j, hh, state)
            if not scores_between:
                scores(j + 1, hh, 1 - slot)

    def body(jj, carry):
        step(2 * jj, 0)
        step(2 * jj + 1, 1)
        return carry

    lax.fori_loop(0, (n - 1) // 2, body, 0)
    for hh in range(N_SCORE_HEADS):
        accumulate(n - 1, hh, probs(hh, 0))


def _attn_scratch(v_rows):
    stat = pltpu.VMEM((N_SCORE_HEADS, 1, TM), F32)
    return [pltpu.VMEM((N_SCORE_HEADS, 128, TM), BF16),
            stat, stat,
            pltpu.VMEM((N_SCORE_HEADS, v_rows, TM), F32),
            pltpu.VMEM((N_SCORE_HEADS, TM, TM), F32), pltpu.VMEM((N_SCORE_HEADS, TM, TM), F32),
            stat, stat,
            pltpu.VMEM((N_SCORE_HEADS, TM, TM), BF16), pltpu.VMEM((N_SCORE_HEADS, TM, TM), BF16)]


def _gqa_kernel(q_ref, k_ref, v_ref, kmax_ref, o_ref, *scratch, n_kv):
    group = GQA_HEADS // GQA_KV_HEADS
    _attn_core(q_ref, k_ref, v_ref, kmax_ref, scratch, n_kv=n_kv,
               k_col=lambda h: 0, k_row=lambda h: h // group, q_slot=lambda h: h // group,
               v_row=lambda h: (h // group) * GQA_VROWS, v_rows=GQA_VROWS, scores_between=True)
    acc_ref = scratch[3]
    for hh in range(GQA_HEADS):
        a = acc_ref[hh]
        o_ref[0, 0, hh * HEAD_DIM:(hh + 1) * HEAD_DIM, :] = (a[:HEAD_DIM] / a[HEAD_DIM:HEAD_DIM + 1]).astype(o_ref.dtype)


def _gqa_attention(aq, ak, av, kmax):
    b, nt, nq, _ = aq.shape
    t = nt * TM
    assert nt % 2 == 1
    return pl.pallas_call(
        functools.partial(_gqa_kernel, n_kv=nt),
        grid=(b, nt),
        in_specs=[pl.BlockSpec((1, 1, nq, TM), lambda bb, i: (bb, i, 0, 0)),
                  pl.BlockSpec((1, t, 128), lambda bb, i: (bb, 0, 0)),
                  pl.BlockSpec((1, nt, GQA_KV_HEADS * GQA_VROWS, TM), lambda bb, i: (bb, 0, 0, 0)),
                  pl.BlockSpec((1, 8, 128), lambda bb, i: (bb, 0, 0))],
        out_specs=pl.BlockSpec((1, 1, nq, TM), lambda bb, i: (bb, i, 0, 0)),
        out_shape=jax.ShapeDtypeStruct((b, nt, nq, TM), BF16),
        scratch_shapes=_attn_scratch(GQA_VROWS),
        compiler_params=_params("parallel", "arbitrary"),
        name="gqa_attn",
    )(aq, ak, av, kmax)


def _diff_kernel(q_ref, k_ref, v_ref, kmax_ref, lam_ref, gain_ref, o_ref, *scratch, n_kv, lam_init):
    _attn_core(q_ref, k_ref, v_ref, kmax_ref, scratch, n_kv=n_kv,
               k_col=lambda hc: (hc // 2) * 128, k_row=lambda hc: hc, q_slot=lambda hc: hc % 2,
               v_row=lambda hc: (hc // 2) * DIFF_VROWS, v_rows=DIFF_VROWS, scores_between=False)
    acc_ref = scratch[3]
    lp = lam_ref[...]
    lam = (jnp.exp(jnp.sum(lp[0:1] * lp[1:2], axis=1, keepdims=True))
           - jnp.exp(jnp.sum(lp[2:3] * lp[3:4], axis=1, keepdims=True)) + lam_init)
    gain = gain_ref[...]
    nv = 2 * HEAD_DIM
    for hh in range(DIFF_HEADS):
        a0, a1 = acc_ref[2 * hh], acc_ref[2 * hh + 1]
        o = a0[:nv] / a0[nv:nv + 1] - lam * (a1[:nv] / a1[nv:nv + 1])
        y = o * lax.rsqrt(jnp.mean(o * o, axis=0, keepdims=True) + NORM_EPS) * gain
        o_ref[0, 0, hh * nv:(hh + 1) * nv, :] = (y * (1.0 - lam_init)).astype(o_ref.dtype)


def _diff_attention(dq, dk, dv, kmax, lam_p, gain, layer_idx):
    b, nt, nq, _ = dq.shape
    t = nt * TM
    assert nt % 2 == 1
    lam_init = 0.8 - 0.6 * math.exp(-0.3 * layer_idx)
    return pl.pallas_call(
        functools.partial(_diff_kernel, n_kv=nt, lam_init=lam_init),
        grid=(b, nt),
        in_specs=[pl.BlockSpec((1, 1, nq, TM), lambda bb, i: (bb, i, 0, 0)),
                  pl.BlockSpec((1, t, DIFF_QK), lambda bb, i: (bb, 0, 0)),
                  pl.BlockSpec((1, nt, DIFF_HEADS * DIFF_VROWS, TM), lambda bb, i: (bb, 0, 0, 0)),
                  pl.BlockSpec((1, 8, 128), lambda bb, i: (bb, 0, 0)),
                  _const_spec(lam_p.shape),
                  _const_spec((2 * HEAD_DIM, 1))],
        out_specs=pl.BlockSpec((1, 1, DIFF_V, TM), lambda bb, i: (bb, i, 0, 0)),
        out_shape=jax.ShapeDtypeStruct((b, nt, DIFF_V, TM), BF16),
        scratch_shapes=_attn_scratch(DIFF_VROWS),
        compiler_params=_params("parallel", "arbitrary"),
        name="diff_attn",
    )(dq, dk, dv, kmax, lam_p, gain.reshape(2 * HEAD_DIM, 1))


def _split_bf16(x):
    hi = x.astype(BF16)
    return hi, (x - hi.astype(F32)).astype(BF16)


def _gla_kernel(qk_ref, la_ref, vt_ref, o_ref, st_ref, *, backward):
    @pl.when(pl.program_id(1) == 0)
    def _():
        st_ref[...] = jnp.zeros(st_ref.shape, F32)

    n_chunk = TM // GLA_CHUNK
    r = lax.broadcasted_iota(jnp.int32, (TM, TM), 0)
    c = lax.broadcasted_iota(jnp.int32, (TM, TM), 1)
    same = (r // GLA_CHUNK) == (c // GLA_CHUNK)
    causal = (c >= r) if backward else (c <= r)
    tri = jnp.where(same & causal, 1.0, 0.0).astype(BF16)
    blk = jnp.where(same, 1.0, 0.0).astype(BF16)
    la_hi, la_lo = _split_bf16(la_ref[0])
    cum = jnp.dot(tri, la_hi, preferred_element_type=F32) + jnp.dot(tri, la_lo, preferred_element_type=F32)
    tot = jnp.dot(blk, la_hi, preferred_element_type=F32) + jnp.dot(blk, la_lo, preferred_element_type=F32)
    q = qk_ref[0, :, :GLA_QK]
    k = qk_ref[0, :, GLA_QK:]
    q_in = (q * jnp.exp(cum)).astype(BF16)
    k_in = k * jnp.exp(-cum)
    k_end = k * jnp.exp(tot - cum)
    dec = jnp.exp(tot)
    lane = lax.broadcasted_iota(jnp.int32, (TM, 128), 1)
    row = lax.broadcasted_iota(jnp.int32, (TM, 128), 0)
    keep = same & ((r >= c) if backward else (r <= c))
    order = range(n_chunk - 1, -1, -1) if backward else range(n_chunk)
    hs = range(GLA_HEADS)
    pairs = [slice((h // 2) * 128, (h // 2 + 1) * 128) for h in hs]
    head_lanes = [(lane // GLA_DK) == (h % 2) for h in hs]
    q_pair = [q_in[:, pairs[h]] for h in hs]
    k_in_h = [jnp.where(head_lanes[h], k_in[:, pairs[h]], 0.0).astype(BF16) for h in hs]
    k_end_h = [jnp.where(head_lanes[h], k_end[:, pairs[h]], 0.0) for h in hs]
    vts = [vt_ref[0, 0, h * GLA_DV:(h + 1) * GLA_DV, :] for h in hs]
    a_ts = [jnp.where(keep, _nt(k_in_h[h], q_pair[h]), 0.0).astype(BF16) for h in hs]
    o_ts = [jnp.dot(vts[h], a_ts[h], preferred_element_type=F32) for h in hs]
    s_ts = [st_ref[h] for h in hs]
    for ci in order:
        in_chunk = (row // GLA_CHUNK) == ci
        for h in hs:
            q_c = jnp.where(in_chunk, q_pair[h], jnp.zeros_like(q_pair[h]))
            o_ts[h] = o_ts[h] + _nt(s_ts[h].astype(BF16), q_c)
            k_c = jnp.where(in_chunk, k_end_h[h], 0.0).astype(BF16)
            s_ts[h] = (s_ts[h] * dec[ci * GLA_CHUNK:ci * GLA_CHUNK + 1, pairs[h]]
                       + jnp.dot(vts[h], k_c, preferred_element_type=F32))
    for h in hs:
        st_ref[h] = s_ts[h]
        o_ref[0, 0, h * GLA_DV:(h + 1) * GLA_DV, :] = o_ts[h]


def _gla_direction(qk, la, vt, backward):
    b, t, _ = qk.shape
    nt = t // TM
    d = 1 if backward else 0
    tile = (lambda i: jnp.where(i == 0, 0, nt - i)) if backward else (lambda i: i)
    return pl.pallas_call(
        functools.partial(_gla_kernel, backward=backward),
        grid=(b, nt),
        in_specs=[pl.BlockSpec((1, TM, 2 * GLA_QK), lambda bb, i: (bb, tile(i), 0)),
                  pl.BlockSpec((1, TM, GLA_QK), lambda bb, i: (bb, tile(i), d)),
                  pl.BlockSpec((1, 1, GLA_V, TM), lambda bb, i: (bb, tile(i), 0, 0))],
        out_specs=pl.BlockSpec((1, 1, GLA_V, TM), lambda bb, i: (bb, tile(i), 0, 0)),
        out_shape=jax.ShapeDtypeStruct((b, nt, GLA_V, TM), F32),
        scratch_shapes=[pltpu.VMEM((GLA_HEADS, GLA_DV, 128), F32)],
        compiler_params=_params("parallel", "arbitrary"),
        name="gla_bwd" if backward else "gla_fwd",
    )(qk, la, vt)


def _post_phases(xs, mods, g, yas, ybs, woa_ref, wob_ref, w1_ref, w2_ref, ff_chunk):
    n = len(xs)
    ys = [jnp.dot(yas[s], woa_ref[...], preferred_element_type=F32)
          + jnp.dot(ybs[s].astype(F32).T.astype(BF16), wob_ref[...], preferred_element_type=F32) for s in range(n)]
    x1s = [xs[s] + mods[s][2:3] * _rms(ys[s], g[1:2]) for s in range(n)]
    hs = [(_rms(x1s[s], g[2:3]) * (1.0 + mods[s][4:5]) + mods[s][3:4]).astype(BF16) for s in range(n)]
    d_ff = w1_ref.shape[1]
    accs = [jnp.zeros(xs[0].shape, F32) for _ in range(n)]
    for c in range(d_ff // ff_chunk):
        cols = slice(c * ff_chunk, (c + 1) * ff_chunk)
        for s in range(n):
            u = jnp.dot(hs[s], w1_ref[:, cols], preferred_element_type=F32)
            u = jnp.square(jnp.maximum(u, 0.0)).astype(BF16)
            accs[s] = accs[s] + jnp.dot(u, w2_ref[cols, :], preferred_element_type=F32)
    return [x1s[s] + mods[s][5:6] * _rms(accs[s], g[3:4]) for s in range(n)]


def _post_gla_kernel(xs_ref, of_ref, ob_ref, gg_ref, gain_ref, yb_ref, mods_ref, g_ref, woa_ref, wob_ref, w1_ref,
                     w2_ref, o_ref, *, ff_chunk):
    gain = gain_ref[...]
    yas = []
    for s in range(POST_PB):
        parts = []
        for hh in range(GLA_HEADS):
            rows = slice(hh * GLA_DV, (hh + 1) * GLA_DV)
            o = of_ref[s, 0, rows, :] + ob_ref[s, 0, rows, :]
            gt = gg_ref[s, 0, rows, :]
            y = o * lax.rsqrt(jnp.mean(o * o, axis=0, keepdims=True) + NORM_EPS) * gain
            parts.append(y * (gt / (1.0 + jnp.exp(-gt))))
        yas.append(jnp.concatenate(parts, axis=0).T.astype(BF16))
    outs = _post_phases([xs_ref[s] for s in range(POST_PB)], [mods_ref[s] for s in range(POST_PB)], g_ref[...], yas,
                        [yb_ref[s, 0] for s in range(POST_PB)], woa_ref, wob_ref, w1_ref, w2_ref, ff_chunk)
    for s in range(POST_PB):
        o_ref[s] = outs[s]


def _post_kernel(xs_ref, ya_ref, yb_ref, mods_ref, g_ref, woa_ref, wob_ref, w1_ref, w2_ref, o_ref, *, ff_chunk):
    outs = _post_phases([xs_ref[s] for s in range(POST_PB)], [mods_ref[s] for s in range(POST_PB)], g_ref[...],
                        [ya_ref[s].astype(BF16) for s in range(POST_PB)], [yb_ref[s, 0] for s in range(POST_PB)],
                        woa_ref, wob_ref, w1_ref, w2_ref, ff_chunk)
    for s in range(POST_PB):
        o_ref[s] = outs[s]


def _post(xs, ya, yb_t, mods_l, g, w_out, w1, w2, skip_ctx):
    b, t, d = xs.shape
    nt = t // TM
    pb = POST_PB
    half = w_out.shape[0] // 2
    woa = w_out[:half].astype(BF16)
    wob = w_out[half:].astype(BF16)
    w1 = w1.astype(BF16)
    w2 = w2.astype(BF16)
    off = 1 if skip_ctx else 0
    tok = lambda n: pl.BlockSpec((pb, TM, n), lambda bb, i: (bb, i + off, 0))
    tr = lambda n: pl.BlockSpec((pb, 1, n, TM), lambda bb, i: (bb, i + off, 0, 0))
    if isinstance(ya, tuple):
        o_f, o_b, gg, gain = ya
        body, name = _post_gla_kernel, "post_gla"
        a_specs = [tr(half), tr(half), tr(half), _const_spec((GLA_DV, 1))]
        a_args = [o_f, o_b, gg, gain.reshape(GLA_DV, 1)]
    else:
        body, name = _post_kernel, "post"
        a_specs, a_args = [tok(half)], [ya]
    return pl.pallas_call(
        functools.partial(body, ff_chunk=1024),
        grid=(b // pb, nt - off),
        in_specs=[tok(d)] + a_specs + [
                  tr(half),
                  pl.BlockSpec((pb, N_MOD, d), lambda bb, i: (_mod_block(bb, i + off, pb), 0, 0)),
                  _const_spec(g.shape),
                  _const_spec(woa.shape), _const_spec(wob.shape), _const_spec(w1.shape), _const_spec(w2.shape)],
        out_specs=pl.BlockSpec((pb, TM, d), lambda bb, i: (bb, i, 0)),
        out_shape=jax.ShapeDtypeStruct((b, t - off * TM, d), F32),
        compiler_params=_params("parallel", "parallel"),
        name=name,
    )(xs, *a_args, yb_t, mods_l, g, woa, wob, w1, w2)


def _short_conv_kernel(u_ref, prev_ref, next_ref, w_ref, b_ref, v_ref, x1_ref, x2_ref, *, n_tiles):
    i = pl.program_id(1)
    u = u_ref[0]
    rows = lax.broadcasted_iota(jnp.int32, u.shape, 0)
    prev_row = jnp.where(i == 0, 0.0, prev_ref[0, 7:8, :])
    next_row = jnp.where(i == n_tiles - 1, 0.0, next_ref[0, 0:1, :])
    up = jnp.where(rows == 0, prev_row, pltpu.roll(u, 1, axis=0))
    dn = jnp.where(rows == TM - 1, next_row, pltpu.roll(u, TM - 1, axis=0))
    w = w_ref[...]
    s = up * w[0:1] + u * w[1:2] + dn * w[2:3] + b_ref[...]
    v_ref[0] = s[:, :HY_CH]
    x1_ref[0] = s[:, HY_CH:2 * HY_CH]
    x2_ref[0] = s[:, 2 * HY_CH:]


def _short_conv(u, conv_w, conv_b, first_tile, n_tiles):
    b, t, c = u.shape
    r8 = TM // 8
    last_blk = t // 8 - 1
    out = jax.ShapeDtypeStruct((b, n_tiles * TM, HY_CH), F32)
    ospec = pl.BlockSpec((1, TM, HY_CH), lambda bb, i: (bb, i, 0))
    return pl.pallas_call(
        functools.partial(_short_conv_kernel, n_tiles=n_tiles),
        grid=(b, n_tiles),
        in_specs=[pl.BlockSpec((1, TM, c), lambda bb, i: (bb, i + first_tile, 0)),
                  pl.BlockSpec((1, 8, c), lambda bb, i: (bb, jnp.maximum((i + first_tile) * r8 - 1, 0), 0)),
                  pl.BlockSpec((1, 8, c), lambda bb, i: (bb, jnp.minimum((i + first_tile + 1) * r8, last_blk), 0)),
                  _const_spec(conv_w.shape), _const_spec((1, c))],
        out_specs=[ospec, ospec, ospec],
        out_shape=[out, out, out],
        compiler_params=_params("parallel", "parallel"),
        name="hy_short",
    )(u, u, u, conv_w, conv_b.reshape(1, c))


def _dot3(a, b):
    ah, al = _split_bf16(a)
    bh, bl = _split_bf16(b)
    d = lambda x, y: jnp.dot(x, y, preferred_element_type=F32)
    return d(ah, bh) + (d(al, bh) + d(ah, bl))


def _filter_kernel(w1t_ref, w1c_ref, w1s_ref, b1_ref, w2_ref, b2_ref, w3_ref, delta_ref, o_ref, *, seq, rb):
    n = pl.program_id(0) * rb + lax.broadcasted_iota(jnp.int32, (rb, 1), 0)
    d = jnp.where(n < seq, n, 2 * seq - n).astype(F32)
    t = d / max(seq - 1, 1)
    w = (2 * math.pi / seq) * d
    band = lax.broadcasted_iota(jnp.int32, (1, HY_BANDS), 1).astype(F32)
    bands = 1e-4 + band * ((HY_BANDS - 1 - 1e-4) / (HY_BANDS - 1))
    ang = w * bands
    pre = t * w1t_ref[...] + _dot3(jnp.cos(ang), w1c_ref[...]) - _dot3(jnp.sin(ang), w1s_ref[...]) + b1_ref[...]
    hid = jnp.sin(pre)
    hid = jnp.sin(_dot3(hid, w2_ref[...]) + b2_ref[...])
    decay = jnp.where(n == seq, 0.0, jnp.exp(-t * delta_ref[...]))
    for o in range(HY_ORDER):
        o_ref[o] = (_dot3(hid, w3_ref[0, o]) * decay).astype(o_ref.dtype)


def _hyena_kernels(seq, f_w1, f_b1, f_w2, f_b2, f_w3):
    rb = min(512, seq)
    nblk = 2 * seq // rb
    ffn = f_w2.shape[0]
    w3 = f_w3.reshape(ffn, HY_ORDER, 2, HY_CH).transpose(2, 1, 0, 3)
    deltas = jnp.abs(jnp.linspace(math.log(HY_TARGET) / HY_FAST_PCT, math.log(HY_TARGET) / HY_SLOW_PCT,
                                  HY_CH, dtype=F32)).reshape(1, HY_CH)
    return pl.pallas_call(
        functools.partial(_filter_kernel, seq=seq, rb=rb),
        grid=(nblk,),
        in_specs=[_const_spec((1, ffn)), _const_spec((HY_BANDS, ffn)), _const_spec((HY_BANDS, ffn)),
                  _const_spec((1, ffn)), _const_spec((ffn, ffn)), _const_spec((1, ffn)),
                  pl.BlockSpec((1, HY_ORDER, ffn, HY_CH), lambda j: ((2 * j) // nblk, 0, 0, 0)),
                  _const_spec((1, HY_CH))],
        out_specs=pl.BlockSpec((HY_ORDER, rb, HY_CH), lambda j: (0, j, 0)),
        out_shape=jax.ShapeDtypeStruct((HY_ORDER, 2 * seq, HY_CH), BF16),
        compiler_params=_params("parallel"),
        name="hy_filter",
    )(f_w1[0:1], f_w1[1:1 + HY_BANDS], f_w1[1 + HY_BANDS:], f_b1.reshape(1, ffn), f_w2, f_b2.reshape(1, ffn), w3,
      deltas)


def _dft_tables(seq):
    n = 2 * seq
    r = math.isqrt(n)
    assert r * r == n
    idx = jnp.arange(r, dtype=jnp.int32)
    phi = (2 * math.pi / r) * ((idx[:, None] * idx[None, :]) % r).astype(F32)
    f1 = jnp.concatenate([jnp.cos(phi), -jnp.sin(phi)], axis=0).astype(BF16)
    m = (idx[:, None, None] + r * idx[None, :, None]) * idx[None, None, :]
    th = (2 * math.pi / n) * (m % n).astype(F32)
    c, s = jnp.cos(th), jnp.sin(th)
    blockm = lambda re, im: jnp.concatenate([jnp.concatenate([re, -im], axis=-1),
                                             jnp.concatenate([im, re], axis=-1)], axis=-2)
    g = blockm(c, -s).astype(BF16)
    ginv = blockm(jnp.swapaxes(c, 1, 2), jnp.swapaxes(s, 1, 2)).astype(BF16)
    finv = (jnp.concatenate([jnp.cos(phi.T), -jnp.sin(phi.T)], axis=1)[:r // 2] / n).astype(BF16)
    return r, f1, g, ginv, finv


DFT_BB = 8


def _dft1_kernel(f_ref, x_ref, o_ref):
    f = f_ref[...]
    r = o_ref.shape[2]
    x = pltpu.einshape("abc->bac", x_ref[0].astype(F32))
    ys = [jnp.dot(f, x[j].astype(BF16), preferred_element_type=F32) for j in range(DFT_BB)]
    y = pltpu.einshape("jkc->kjc", jnp.stack(ys, axis=0))
    o_ref[0, 0] = y[:r].astype(o_ref.dtype)
    o_ref[0, 1] = y[r:].astype(o_ref.dtype)


def _dft_stage1(f, x4):
    b, ka, r, c = x4.shape
    return pl.pallas_call(
        _dft1_kernel,
        grid=(b, r // DFT_BB),
        in_specs=[_const_spec(f.shape), pl.BlockSpec((1, ka, DFT_BB, c), lambda bi, j: (bi, 0, j, 0))],
        out_specs=pl.BlockSpec((1, 2, r, DFT_BB, c), lambda bi, j: (bi, 0, 0, j, 0)),
        out_shape=jax.ShapeDtypeStruct((b, 2, r, r, c), BF16),
        compiler_params=_params("parallel", "parallel"),
        name="hy_dft1",
    )(f, x4)


def _fspec_kernel(a_ref, g_ref, o_ref, *, kb, r):
    for kk in range(kb):
        a = a_ref[0, :, kk].reshape(2 * r, a_ref.shape[-1])
        o_ref[0, kk] = jnp.dot(g_ref[kk], a, preferred_element_type=F32)


def _filter_spectrum(a, g, kb):
    o, _, r, _, c = a.shape
    return pl.pallas_call(
        functools.partial(_fspec_kernel, kb=kb, r=r),
        grid=(o, r // kb),
        in_specs=[pl.BlockSpec((1, 2, kb, r, c), lambda oo, j: (oo, 0, j, 0, 0)),
                  pl.BlockSpec((kb, 2 * r, 2 * r), lambda oo, j: (j, 0, 0))],
        out_specs=pl.BlockSpec((1, kb, 2 * r, c), lambda oo, j: (oo, j, 0, 0)),
        out_shape=jax.ShapeDtypeStruct((o, r, 2 * r, c), F32),
        compiler_params=_params("parallel", "parallel"),
        name="hy_fspec",
    )(a, g)


def _dft2_kernel(a_ref, g_ref, gi_ref, k_ref, o_ref, *, kb, r):
    xs = [jnp.dot(g_ref[kk], a_ref[0, :, kk].reshape(2 * r, a_ref.shape[-1]), preferred_element_type=F32)
          for kk in range(kb)]
    ys = []
    for kk in range(kb):
        x, ks = xs[kk], k_ref[0, kk]
        xr, xi, kr, ki = x[:r], x[r:], ks[:r], ks[r:]
        ys.append(jnp.concatenate([xr * kr - xi * ki, xr * ki + xi * kr], axis=0).astype(BF16))
    bqs = [jnp.dot(gi_ref[kk], ys[kk], preferred_element_type=F32) for kk in range(kb)]
    for kk in range(kb):
        o_ref[0, 0, kk] = bqs[kk][:r].astype(o_ref.dtype)
        o_ref[0, 1, kk] = bqs[kk][r:].astype(o_ref.dtype)


def _dft_stage2(a, g, ginv, kspec, order, kb):
    b, _, r, _, c = a.shape
    blk = pl.BlockSpec((1, 2, kb, r, c), lambda j, bb: (bb, 0, j, 0, 0))
    tab = pl.BlockSpec((kb, 2 * r, 2 * r), lambda j, bb: (j, 0, 0))
    return pl.pallas_call(
        functools.partial(_dft2_kernel, kb=kb, r=r),
        grid=(r // kb, b),
        in_specs=[blk, tab, tab, pl.BlockSpec((1, kb, 2 * r, c), lambda j, bb: (order, j, 0, 0))],
        out_specs=blk,
        out_shape=jax.ShapeDtypeStruct(a.shape, BF16),
        compiler_params=_params("parallel", "arbitrary"),
        name="hy_dft2",
    )(a, g, ginv, kspec)


def _dft3_kernel(f_ref, b_ref, gate_ref, z_ref, bias_ref, o_ref):
    f = f_ref[...]
    bq = jnp.concatenate([b_ref[0, 0], b_ref[0, 1]], axis=0)
    bq = pltpu.einshape("kjc->jkc", bq.astype(F32))
    ys = [jnp.dot(f, bq[j].astype(BF16), preferred_element_type=F32) for j in range(DFT_BB)]
    y = pltpu.einshape("jac->ajc", jnp.stack(ys, axis=0))
    o_ref[0] = gate_ref[0] * (y + z_ref[0] * bias_ref[...])


def _dft_stage3(finv, bq5, gate4, z4, bias):
    b, _, r, _, c = bq5.shape
    ra = finv.shape[0]
    io = pl.BlockSpec((1, ra, DFT_BB, c), lambda bi, j: (bi, 0, j, 0))
    return pl.pallas_call(
        _dft3_kernel,
        grid=(b, r // DFT_BB),
        in_specs=[_const_spec(finv.shape), pl.BlockSpec((1, 2, r, DFT_BB, c), lambda bi, j: (bi, 0, 0, j, 0)), io, io,
                  _const_spec((1, 1, c))],
        out_specs=io,
        out_shape=jax.ShapeDtypeStruct((b, ra, r, c), F32),
        compiler_params=_params("parallel", "parallel"),
        name="hy_dft3",
    )(finv, bq5, gate4, z4, bias)


def _long_conv_gated(v, gates, kern, hy_bias):
    b, seq, c = v.shape
    r, f1, g, ginv, finv = _dft_tables(seq)
    kb = 8
    kspec = _filter_spectrum(_dft_stage1(f1, kern.reshape(HY_ORDER, r, r, c)), g, kb)
    z = v.reshape(b, r // 2, r, c)
    for o in range(HY_ORDER):
        a = _dft_stage1(f1[:, :r // 2], z)
        bq = _dft_stage2(a, g, ginv, kspec, o, kb)
        z = _dft_stage3(finv, bq, gates[o].reshape(b, r // 2, r, c), z, hy_bias[o].reshape(1, 1, c))
    return z.reshape(b, seq, c)


def _ctx_conv_kernel(v_ref, x1_ref, x2_ref, k_ref, ff_ref, fi_ref, bias_ref, o_ref):
    n = ff_ref.shape[1]
    z = v_ref[0]
    for o, gate_ref in enumerate((x1_ref, x2_ref)):
        ks = jnp.dot(ff_ref[...], k_ref[o], preferred_element_type=F32)
        x = jnp.dot(ff_ref[:, :TM], z.astype(BF16), preferred_element_type=F32)
        xr, xi, kr, ki = x[:n], x[n:], ks[:n], ks[n:]
        y = jnp.concatenate([xr * kr - xi * ki, xr * ki + xi * kr], axis=0).astype(BF16)
        conv = jnp.dot(fi_ref[...], y, preferred_element_type=F32)
        z = gate_ref[0] * (conv + z * bias_ref[o:o + 1])
    o_ref[0] = z


def _ctx_conv_gated(v, x1, x2, kern, hy_bias):
    b, seq, c = v.shape
    n = 2 * seq
    idx = jnp.arange(n, dtype=jnp.int32)
    th = (2 * math.pi / n) * ((idx[:, None] * idx[None, :]) % n).astype(F32)
    ff = jnp.concatenate([jnp.cos(th), -jnp.sin(th)], axis=0).astype(BF16)
    fi = (jnp.concatenate([jnp.cos(th), -jnp.sin(th)], axis=1)[:seq] / n).astype(BF16)
    io = pl.BlockSpec((1, seq, c), lambda bb: (bb, 0, 0))
    return pl.pallas_call(
        _ctx_conv_kernel,
        grid=(b,),
        in_specs=[io, io, io, _const_spec(kern.shape), _const_spec(ff.shape), _const_spec(fi.shape),
                  _const_spec(hy_bias.shape)],
        out_specs=io,
        out_shape=jax.ShapeDtypeStruct((b, seq, c), F32),
        compiler_params=_params("parallel"),
        name="hy_ctx",
    )(v, x1, x2, kern, ff, fi, hy_bias)


def _hyena(u, lc, conv_w, conv_b, f_w1, f_b1, f_w2, f_b2, f_w3, hy_bias):
    seq = u.shape[1] - lc
    vc, x1c, x2c = _short_conv(u, conv_w, conv_b, 0, lc // TM)
    vx, x1x, x2x = _short_conv(u, conv_w, conv_b, lc // TM, seq // TM)
    yc = _ctx_conv_gated(vc, x1c, x2c, _hyena_kernels(lc, f_w1, f_b1, f_w2, f_b2, f_w3), hy_bias)
    yx = _long_conv_gated(vx, (x1x, x2x), _hyena_kernels(seq, f_w1, f_b1, f_w2, f_b2, f_w3), hy_bias)
    return jnp.concatenate([yc, yx], axis=1)


def _rope_tables_t(seq, lc):
    rows = seq // GRID_W
    r = jnp.broadcast_to(jnp.arange(rows, dtype=F32)[:, None], (rows, GRID_W)).reshape(-1)
    col = jnp.broadcast_to(jnp.arange(GRID_W, dtype=F32)[None, :], (rows, GRID_W)).reshape(-1)
    n_pairs = HEAD_DIM // 4
    inv = ROPE_THETA ** (-jnp.arange(n_pairs, dtype=F32) / n_pairs)
    ang = jnp.concatenate([r[:, None] * inv, col[:, None] * inv], axis=-1)
    cos = jnp.concatenate([jnp.ones((lc, 2 * n_pairs), F32), jnp.cos(ang)], axis=0)
    sin = jnp.concatenate([jnp.zeros((lc, 2 * n_pairs), F32), jnp.sin(ang)], axis=0)
    return cos.T, sin.T


def kernel(x, c, ctx, c_ctx, w_ada, b_ada, norm_g, w_mlp_in, w_mlp_out, ev_w_in, ev_w_lr, ev_b_lr, ev_gla_g,
           ev_qk_g, ev_w_out, od_w_in, od_conv_w, od_conv_b, od_f_w1, od_f_b1, od_f_w2, od_f_b2, od_f_w3,
           od_hy_bias, od_lam, od_diff_g, od_w_out):
    bsz, seq, d = x.shape
    lc = ctx.shape[1]
    depth = w_ada.shape[0]
    assert lc == TM and seq % TM == 0 and bsz <= MOD_CTX_ROW and bsz % POST_PB == 0 and bsz % IN_PB == 0
    cos_t, sin_t = _rope_tables_t(seq, lc)
    cc = jnp.zeros((8, d), F32).at[:bsz].set(c).at[MOD_CTX_ROW:MOD_CTX_ROW + max(POST_PB, IN_PB)].set(c_ctx)
    mods = _mods(cc, w_ada, b_ada).reshape(depth, 8, N_MOD, d)
    xs = jnp.concatenate([ctx, x], axis=1)
    for i in range(depth):
        last = i == depth - 1
        j = i // 2
        g = norm_g[i]
        if i % 2 == 0:
            qk, la, gv, gg, aq, ak, av, kn = _even_in(xs, mods[i], g[0:1], ev_w_in[j], ev_w_lr[j], ev_b_lr[j],
                                                  ev_qk_g[j], cos_t, sin_t)
            ya = (_gla_direction(qk, la, gv, False), _gla_direction(qk, la, gv, True), gg, ev_gla_g[j])
            yb = _gqa_attention(aq, ak, av, jnp.max(kn, axis=1))
            w_out = ev_w_out[j]
        else:
            u, dq, dk, dv, kn = _odd_in(xs, mods[i], g[0:1], od_w_in[j], cos_t, sin_t)
            ya = _hyena(u, lc, od_conv_w[j], od_conv_b[j], od_f_w1[j], od_f_b1[j], od_f_w2[j], od_f_b2[j],
                        od_f_w3[j], od_hy_bias[j])
            yb = _diff_attention(dq, dk, dv, jnp.max(kn, axis=1), od_lam[j], od_diff_g[j], i)
            w_out = od_w_out[j]
        xs = _post(xs, ya, yb, mods[i], g, w_out, w_mlp_in[i], w_mlp_out[i], skip_ctx=last)
    return xs
```

```python
import functools
import math

import jax
import jax.numpy as jnp
import numpy as np
from jax import lax
from jax.experimental import pallas as pl
from jax.experimental.pallas import tpu as pltpu

F32 = jnp.float32
BF16 = jnp.bfloat16

TM = 256
GRID_W = 64
HEAD_DIM = 64
ROPE_THETA = 10000.0
NORM_EPS = 1e-6
N_MOD = 6
VMEM_LIMIT = 56 * 1024 * 1024

GLA_HEADS = 4
GLA_DK = 64
GLA_DV = 128
GLA_RANK = 16
GLA_NORMALIZER = 16.0
GLA_CHUNK = 64
GLA_QK = GLA_HEADS * GLA_DK
GLA_V = GLA_HEADS * GLA_DV
GQA_HEADS = 8
GQA_KV_HEADS = 2
HY_CH = 512
HY_ORDER = 2
HY_BANDS = 16
HY_TARGET = 1e-2
HY_FAST_PCT = 0.3
HY_SLOW_PCT = 1.5
HY_COLS = (HY_ORDER + 1) * HY_CH
DIFF_HEADS = 4
DIFF_QK = DIFF_HEADS * 2 * HEAD_DIM
DIFF_V = DIFF_HEADS * 2 * HEAD_DIM
ONES_ROWS = 16
N_SCORE_HEADS = 8
LOG2E = 1.4426950408889634
Q_SCALE = HEAD_DIM ** -0.5 * LOG2E
GQA_VROWS = HEAD_DIM + ONES_ROWS
SCORE_BOUND_SLACK = 1.01
SCORE_BOUND_LIMIT = 60.0
DIFF_VROWS = 2 * HEAD_DIM + ONES_ROWS


def _params(*sem):
    return pltpu.CompilerParams(dimension_semantics=sem, vmem_limit_bytes=VMEM_LIMIT)


def _const_spec(shape):
    nd = len(shape)
    return pl.BlockSpec(shape, lambda *_: (0,) * nd, pipeline_mode=pl.Buffered(1))


def _rms(x, g):
    return x * lax.rsqrt(jnp.mean(x * x, axis=-1, keepdims=True) + NORM_EPS) * g


MOD_CTX_ROW = 4
POST_PB = 2
IN_PB = 2


def _mod_block(bb, i, pb):
    return jnp.where(i == 0, MOD_CTX_ROW // pb, bb)


def _mods_kernel(cc_ref, w_ref, b_ref, o_ref):
    cc = cc_ref[...]
    s = cc / (1.0 + jnp.exp(-cc))
    o_ref[0] = jnp.dot(s.astype(BF16), w_ref[0].astype(BF16), preferred_element_type=F32) + b_ref[0]


def _mods(cc, w_ada, b_ada):
    depth, d, n = w_ada.shape
    nb = n // 4
    return pl.pallas_call(
        _mods_kernel,
        grid=(depth, n // nb),
        in_specs=[pl.BlockSpec((8, d), lambda l, j: (0, 0)),
                  pl.BlockSpec((1, d, nb), lambda l, j: (l, 0, j)),
                  pl.BlockSpec((1, 1, nb), lambda l, j: (l, 0, j))],
        out_specs=pl.BlockSpec((1, 8, nb), lambda l, j: (l, 0, j)),
        out_shape=jax.ShapeDtypeStruct((depth, 8, n), F32),
        compiler_params=_params("parallel", "parallel"),
        name="ada_mods",
    )(cc, w_ada, b_ada.reshape(depth, 1, n))


def _normed_input(x, mods, g):
    return (_rms(x, g) * (1.0 + mods[1:2]) + mods[0:1]).astype(BF16)


def _rope_rows(y, cos, sin):
    x1, x2 = y[:32], y[32:]
    return x1 * cos - x2 * sin, x1 * sin + x2 * cos


def _log_sigmoid(x):
    return jnp.minimum(x, 0.0) - jnp.log(1.0 + jnp.exp(-jnp.abs(x)))


def _nt(a, b):
    return lax.dot_general(a, b, (((1,), (1,)), ((), ())), preferred_element_type=F32)


def _max_sq_norm_row(parts):
    n2 = sum(jnp.sum(jnp.square(p.astype(BF16).astype(F32)), axis=0, keepdims=True) for p in parts)
    return jnp.broadcast_to(jnp.max(n2, axis=1, keepdims=True), (1, 128))


def _even_in_kernel(xs_ref, mods_ref, g_ref, wn_ref, wt_ref, wlr_ref, blr_ref, gq_ref, gk_ref,
                    cos_ref, sin_ref, qk_ref, la_ref, gv_ref, gg_ref, aq_ref, ak_ref, av_ref, kn_ref):
    hs = [_normed_input(xs_ref[s], mods_ref[s], g_ref[0:1]) for s in range(IN_PB)]
    ps = [jnp.dot(h, wn_ref[...], preferred_element_type=F32) for h in hs]
    pts = [_nt(wt_ref[...], h) for h in hs]
    cos, sin = cos_ref[...], sin_ref[...]
    gq, gk_gain = gq_ref[...], gk_ref[...]
    ones = jnp.ones((ONES_ROWS, TM), BF16)
    for s in range(IN_PB):
        p, pt_all = ps[s], pts[s]
        qk_ref[s, :, :GLA_QK] = p[:, :GLA_QK] * GLA_DK ** -0.5
        qk_ref[s, :, GLA_QK:] = p[:, GLA_QK:2 * GLA_QK]
        lr = p[:, 2 * GLA_QK:].astype(BF16)
        gk = jnp.dot(lr, wlr_ref[...], preferred_element_type=F32) + blr_ref[...]
        la_ref[s] = _log_sigmoid(gk) * (1.0 / GLA_NORMALIZER)
        gv_ref[s, 0] = pt_all[:GLA_V].astype(BF16)
        gg_ref[s, 0] = pt_all[GLA_V:2 * GLA_V]
        pt = pt_all[2 * GLA_V:]
        for hh in range(GQA_HEADS):
            blk = pt[hh * 64:(hh + 1) * 64]
            y = blk * lax.rsqrt(jnp.mean(blk * blk, axis=0, keepdims=True) + NORM_EPS) * gq
            o1, o2 = _rope_rows(y, cos, sin)
            aq_ref[s, 0, hh * 64:hh * 64 + 32, :] = (o1 * Q_SCALE).astype(BF16)
            aq_ref[s, 0, hh * 64 + 32:hh * 64 + 64, :] = (o2 * Q_SCALE).astype(BF16)
        kparts, knorms = [], []
        for hh in range(GQA_KV_HEADS):
            blk = pt[512 + hh * 64:512 + (hh + 1) * 64]
            y = blk * lax.rsqrt(jnp.mean(blk * blk, axis=0, keepdims=True) + NORM_EPS) * gk_gain
            o1, o2 = _rope_rows(y, cos, sin)
            kparts += [o1, o2]
            knorms.append(_max_sq_norm_row([o1, o2]))
        ak_ref[s] = jnp.concatenate(kparts, axis=0).T.astype(BF16)
        kn_ref[s, 0] = jnp.concatenate(knorms + [jnp.zeros((8 - GQA_KV_HEADS, 128), F32)], axis=0)
        for hh in range(GQA_KV_HEADS):
            av_ref[s, 0, hh * GQA_VROWS:hh * GQA_VROWS + 64, :] = pt[640 + hh * 64:640 + (hh + 1) * 64].astype(BF16)
            av_ref[s, 0, hh * GQA_VROWS + 64:(hh + 1) * GQA_VROWS, :] = ones


def _odd_in_kernel(xs_ref, prev_ref, next_ref, mods_ref, g_ref, wn_ref, wt_ref, cw_ref, cb_ref, cos_ref, sin_ref,
                   v_ref, x1_ref, x2_ref, dq_ref, dk_ref, dv_ref, kn_ref, *, n_tiles):
    i = pl.program_id(1)
    hs = [_normed_input(xs_ref[s], mods_ref[s], g_ref[0:1]) for s in range(IN_PB)]
    h_ext = [jnp.concatenate([hs[s], _normed_input(prev_ref[s], mods_ref[s], g_ref[0:1]),
                              _normed_input(next_ref[s], mods_ref[s], g_ref[0:1])], axis=0) for s in range(IN_PB)]
    us = [jnp.dot(h, wn_ref[...], preferred_element_type=F32) for h in h_ext]
    pts = [_nt(wt_ref[...], h) for h in hs]
    cos, sin = cos_ref[...], sin_ref[...]
    ones = jnp.ones((ONES_ROWS, TM), BF16)
    nv = 2 * HEAD_DIM
    cw = cw_ref[...]
    row_id = lax.broadcasted_iota(jnp.int32, (TM, HY_COLS), 0)
    no_prev = (i == 0) | (i == 1)
    no_next = (i == 0) | (i == n_tiles - 1)
    for s in range(IN_PB):
        pt = pts[s]
        u = us[s][:TM]
        prev_row = jnp.where(no_prev, 0.0, us[s][TM + 7:TM + 8])
        next_row = jnp.where(no_next, 0.0, us[s][TM + 8:TM + 9])
        up = jnp.where(row_id == 0, prev_row, pltpu.roll(u, 1, axis=0))
        dn = jnp.where(row_id == TM - 1, next_row, pltpu.roll(u, TM - 1, axis=0))
        conv = up * cw[0:1] + u * cw[1:2] + dn * cw[2:3] + cb_ref[...]
        v_ref[s] = conv[:, :HY_CH]
        x1_ref[s] = conv[:, HY_CH:2 * HY_CH]
        x2_ref[s] = conv[:, 2 * HY_CH:]
        kparts, knorms = [], []
        for hh in range(2 * DIFF_HEADS):
            o1, o2 = _rope_rows(pt[hh * 64:(hh + 1) * 64], cos, sin)
            dq_ref[s, 0, hh * 64:hh * 64 + 32, :] = (o1 * Q_SCALE).astype(BF16)
            dq_ref[s, 0, hh * 64 + 32:hh * 64 + 64, :] = (o2 * Q_SCALE).astype(BF16)
            k1, k2 = _rope_rows(pt[DIFF_QK + hh * 64:DIFF_QK + (hh + 1) * 64], cos, sin)
            kparts += [k1, k2]
            knorms.append(_max_sq_norm_row([k1, k2]))
        dk_ref[s] = jnp.concatenate(kparts, axis=0).T.astype(BF16)
        kn_ref[s, 0] = jnp.concatenate(knorms, axis=0)
        for hh in range(DIFF_HEADS):
            rows = pt[2 * DIFF_QK + hh * nv:2 * DIFF_QK + (hh + 1) * nv]
            dv_ref[s, 0, hh * DIFF_VROWS:hh * DIFF_VROWS + nv, :] = rows.astype(BF16)
            dv_ref[s, 0, hh * DIFF_VROWS + nv:(hh + 1) * DIFF_VROWS, :] = ones


def _pair_perm(n_heads):
    base = np.concatenate([np.arange(0, HEAD_DIM, 2), np.arange(1, HEAD_DIM, 2)])
    return np.concatenate([h * HEAD_DIM + base for h in range(n_heads)])


def _even_in(xs, mods_l, g0, w_in, w_lr, b_lr, qk_g, cos_t, sin_t):
    b, t, d = xs.shape
    nt = t // TM
    c_v = 2 * GLA_QK
    c_g = c_v + GLA_V
    c_lr0 = c_g + GLA_V
    c_lr = c_lr0 + 2 * GLA_RANK
    wn = jnp.concatenate([w_in[:, :c_v], w_in[:, c_lr0:c_lr], jnp.zeros((d, 128 - 2 * GLA_RANK), F32)],
                         axis=1).astype(BF16)
    cq = c_lr + GQA_HEADS * HEAD_DIM
    ck = cq + GQA_KV_HEADS * HEAD_DIM
    wq = w_in[:, c_lr:cq][:, _pair_perm(GQA_HEADS)]
    wk = w_in[:, cq:ck][:, _pair_perm(GQA_KV_HEADS)]
    wt = jnp.concatenate([w_in[:, c_v:c_lr0], wq, wk, w_in[:, ck:]], axis=1).T.astype(BF16)
    wlr = jnp.zeros((128, 2 * GLA_QK), F32)
    wlr = wlr.at[:GLA_RANK, :GLA_QK].set(w_lr[0]).at[GLA_RANK:2 * GLA_RANK, GLA_QK:].set(w_lr[1]).astype(BF16)
    blr = b_lr.reshape(1, 2 * GLA_QK)
    perm = _pair_perm(1)
    gq = qk_g[0][perm].reshape(HEAD_DIM, 1)
    gk = qk_g[1][perm].reshape(HEAD_DIM, 1)
    tok = lambda n: pl.BlockSpec((IN_PB, TM, n), lambda bb, i: (bb, i, 0))
    tr = lambda n: pl.BlockSpec((IN_PB, 1, n, TM), lambda bb, i: (bb, i, 0, 0))
    return pl.pallas_call(
        _even_in_kernel,
        grid=(b // IN_PB, nt),
        in_specs=[tok(d),
                  pl.BlockSpec((IN_PB, N_MOD, d), lambda bb, i: (_mod_block(bb, i, IN_PB), 0, 0)),
                  _const_spec((1, d)),
                  _const_spec(wn.shape), _const_spec(wt.shape), _const_spec(wlr.shape), _const_spec(blr.shape),
                  _const_spec(gq.shape), _const_spec(gk.shape),
                  pl.BlockSpec((32, TM), lambda bb, i: (0, i)),
                  pl.BlockSpec((32, TM), lambda bb, i: (0, i))],
        out_specs=[tok(2 * GLA_QK), tok(2 * GLA_QK), tr(GLA_V), tr(GLA_V), tr(512), tok(128),
                   tr(GQA_KV_HEADS * GQA_VROWS), pl.BlockSpec((IN_PB, 1, 8, 128), lambda bb, i: (bb, i, 0, 0))],
        out_shape=[jax.ShapeDtypeStruct((b, t, 2 * GLA_QK), F32),
                   jax.ShapeDtypeStruct((b, t, 2 * GLA_QK), F32),
                   jax.ShapeDtypeStruct((b, nt, GLA_V, TM), BF16),
                   jax.ShapeDtypeStruct((b, nt, GLA_V, TM), F32),
                   jax.ShapeDtypeStruct((b, nt, 512, TM), BF16),
                   jax.ShapeDtypeStruct((b, t, 128), BF16),
                   jax.ShapeDtypeStruct((b, nt, GQA_KV_HEADS * GQA_VROWS, TM), BF16),
                   jax.ShapeDtypeStruct((b, nt, 8, 128), F32)],
        compiler_params=_params("parallel", "parallel"),
        name="even_in",
    )(xs, mods_l, g0, wn, wt, wlr, blr, gq, gk, cos_t, sin_t)


def _odd_in(xs, mods_l, g0, w_in, conv_w, conv_b, cos_t, sin_t):
    b, t, d = xs.shape
    nt = t // TM
    r8 = TM // 8
    last_blk = t // 8 - 1
    wn = w_in[:, :HY_COLS].astype(BF16)
    perm = _pair_perm(2 * DIFF_HEADS)
    wq = w_in[:, HY_COLS:HY_COLS + DIFF_QK][:, perm]
    wk = w_in[:, HY_COLS + DIFF_QK:HY_COLS + 2 * DIFF_QK][:, perm]
    wt = jnp.concatenate([wq, wk, w_in[:, HY_COLS + 2 * DIFF_QK:]], axis=1).T.astype(BF16)
    tok = lambda n: pl.BlockSpec((IN_PB, TM, n), lambda bb, i: (bb, i, 0))
    tr = lambda n: pl.BlockSpec((IN_PB, 1, n, TM), lambda bb, i: (bb, i, 0, 0))
    hy = jax.ShapeDtypeStruct((b, t, HY_CH), F32)
    return pl.pallas_call(
        functools.partial(_odd_in_kernel, n_tiles=nt),
        grid=(b // IN_PB, nt),
        in_specs=[tok(d),
                  pl.BlockSpec((IN_PB, 8, d), lambda bb, i: (bb, jnp.maximum(i * r8 - 1, 0), 0)),
                  pl.BlockSpec((IN_PB, 8, d), lambda bb, i: (bb, jnp.minimum((i + 1) * r8, last_blk), 0)),
                  pl.BlockSpec((IN_PB, N_MOD, d), lambda bb, i: (_mod_block(bb, i, IN_PB), 0, 0)),
                  _const_spec((1, d)),
                  _const_spec(wn.shape), _const_spec(wt.shape),
                  _const_spec(conv_w.shape), _const_spec((1, HY_COLS)),
                  pl.BlockSpec((32, TM), lambda bb, i: (0, i)),
                  pl.BlockSpec((32, TM), lambda bb, i: (0, i))],
        out_specs=[tok(HY_CH), tok(HY_CH), tok(HY_CH), tr(DIFF_QK), tok(DIFF_QK), tr(DIFF_HEADS * DIFF_VROWS),
                   pl.BlockSpec((IN_PB, 1, 8, 128), lambda bb, i: (bb, i, 0, 0))],
        out_shape=[hy, hy, hy,
                   jax.ShapeDtypeStruct((b, nt, DIFF_QK, TM), BF16),
                   jax.ShapeDtypeStruct((b, t, DIFF_QK), BF16),
                   jax.ShapeDtypeStruct((b, nt, DIFF_HEADS * DIFF_VROWS, TM), BF16),
                   jax.ShapeDtypeStruct((b, nt, 8, 128), F32)],
        compiler_params=_params("parallel", "parallel"),
        name="odd_in",
    )(xs, xs, xs, mods_l, g0, wn, wt, conv_w, conv_b.reshape(1, HY_COLS), cos_t, sin_t)


def _attn_core(q_ref, k_ref, v_ref, kmax_ref, scratch, *, n_kv, k_col, k_row, q_slot, v_row, v_rows, scores_between):
    qpad_ref, bound_ref, m_ref, acc_ref, s0_ref, s1_ref, c0_ref, c1_ref, p0_ref, p1_ref = scratch
    s_refs, c_refs, p_refs = (s0_ref, s1_ref), (c0_ref, c1_ref), (p0_ref, p1_ref)
    i = pl.program_id(1)
    zeros = jnp.zeros((HEAD_DIM, TM), BF16)
    bound_max = jnp.zeros((1, TM), F32)
    for hh in range(N_SCORE_HEADS):
        qh = q_ref[0, 0, hh * HEAD_DIM:(hh + 1) * HEAD_DIM, :]
        qpad_ref[hh] = jnp.concatenate([qh, zeros] if q_slot(hh) == 0 else [zeros, qh], axis=0)
        qf = qh.astype(F32)
        q_norm2 = jnp.sum(qf * qf, axis=0, keepdims=True)
        k_norm2 = kmax_ref[0, k_row(hh):k_row(hh) + 1, 0:1]
        bound = jnp.sqrt(q_norm2 * k_norm2) * SCORE_BOUND_SLACK
        bound_ref[hh] = bound
        bound_max = jnp.maximum(bound_max, bound)
    acc_ref[...] = jnp.zeros(acc_ref.shape, F32)
    n = jnp.where(i == 0, 1, n_kv)
    fixed_reference = jnp.max(bound_max) <= SCORE_BOUND_LIMIT

    def produce(j, hh, slot):
        kc = k_ref[0, pl.ds(pl.multiple_of(j * TM, TM), TM), k_col(hh):k_col(hh) + 128]
        s = jnp.dot(kc, qpad_ref[hh], preferred_element_type=F32)
        p_refs[slot][hh] = jnp.exp2(s - bound_ref[hh]).astype(BF16)

    def reduce(j, hh, slot):
        vc = v_ref[0, j, v_row(hh):v_row(hh) + v_rows, :]
        acc_ref[hh] = acc_ref[hh] + jnp.dot(vc, p_refs[slot][hh], preferred_element_type=F32)

    @pl.when(fixed_reference)
    def _():
        for hh in range(N_SCORE_HEADS):
            produce(0, hh, 0)

        def fast_step(j, slot):
            for hh in range(N_SCORE_HEADS):
                produce(j + 1, hh, 1 - slot)
                reduce(j, hh, slot)

        def fast_body(jj, carry):
            fast_step(2 * jj, 0)
            fast_step(2 * jj + 1, 1)
            return carry

        lax.fori_loop(0, (n - 1) // 2, fast_body, 0)
        for hh in range(N_SCORE_HEADS):
            reduce(n - 1, hh, 0)

    @pl.when(jnp.logical_not(fixed_reference))
    def _():
        _attn_running_max(k_ref, v_ref, qpad_ref, m_ref, acc_ref, s_refs, c_refs, n=n, k_col=k_col, v_row=v_row,
                          v_rows=v_rows, scores_between=scores_between)


def _attn_running_max(k_ref, v_ref, qpad_ref, m_ref, acc_ref, s_refs, c_refs, *, n, k_col, v_row, v_rows,
                      scores_between):
    m_ref[...] = jnp.full(m_ref.shape, -jnp.inf, F32)

    def scores(j, hh, slot):
        kc = k_ref[0, pl.ds(pl.multiple_of(j * TM, TM), TM), k_col(hh):k_col(hh) + 128]
        s = jnp.dot(kc, qpad_ref[hh], preferred_element_type=F32)
        s_refs[slot][hh] = s
        c_refs[slot][hh] = jnp.max(s, axis=0, keepdims=True)

    def probs(hh, slot):
        m_old = m_ref[hh]
        m_new = jnp.maximum(m_old, c_refs[slot][hh])
        return m_old, m_new, jnp.exp2(s_refs[slot][hh] - m_new).astype(BF16)

    def accumulate(j, hh, state):
        m_old, m_new, p = state
        vc = v_ref[0, j, v_row(hh):v_row(hh) + v_rows, :]
        alpha = jnp.exp2(m_old - m_new)
        acc_ref[hh] = acc_ref[hh] * alpha + jnp.dot(vc, p, preferred_element_type=F32)
        m_ref[hh] = m_new

    for hh in range(N_SCORE_HEADS):
        scores(0, hh, 0)

    def step(j, slot):
        for hh in range(N_SCORE_HEADS):
            state = probs(hh, slot)
            if scores_between:
                scores(j + 1, hh, 1 - slot)
            accumulate(j, hh, state)
            if not scores_between:
                scores(j + 1, hh, 1 - slot)

    def body(jj, carry):
        step(2 * jj, 0)
        step(2 * jj + 1, 1)
        return carry

    lax.fori_loop(0, (n - 1) // 2, body, 0)
    for hh in range(N_SCORE_HEADS):
        accumulate(n - 1, hh, probs(hh, 0))


def _attn_scratch(v_rows):
    stat = pltpu.VMEM((N_SCORE_HEADS, 1, TM), F32)
    return [pltpu.VMEM((N_SCORE_HEADS, 128, TM), BF16),
            stat, stat,
            pltpu.VMEM((N_SCORE_HEADS, v_rows, TM), F32),
            pltpu.VMEM((N_SCORE_HEADS, TM, TM), F32), pltpu.VMEM((N_SCORE_HEADS, TM, TM), F32),
            stat, stat,
            pltpu.VMEM((N_SCORE_HEADS, TM, TM), BF16), pltpu.VMEM((N_SCORE_HEADS, TM, TM), BF16)]


def _gqa_kernel(q_ref, k_ref, v_ref, kmax_ref, o_ref, *scratch, n_kv):
    group = GQA_HEADS // GQA_KV_HEADS
    _attn_core(q_ref, k_ref, v_ref, kmax_ref, scratch, n_kv=n_kv,
               k_col=lambda h: 0, k_row=lambda h: h // group, q_slot=lambda h: h // group,
               v_row=lambda h: (h // group) * GQA_VROWS, v_rows=GQA_VROWS, scores_between=True)
    acc_ref = scratch[3]
    for hh in range(GQA_HEADS):
        a = acc_ref[hh]
        o_ref[0, 0, hh * HEAD_DIM:(hh + 1) * HEAD_DIM, :] = (a[:HEAD_DIM] / a[HEAD_DIM:HEAD_DIM + 1]).astype(o_ref.dtype)


def _gqa_attention(aq, ak, av, kmax):
    b, nt, nq, _ = aq.shape
    t = nt * TM
    assert nt % 2 == 1
    return pl.pallas_call(
        functools.partial(_gqa_kernel, n_kv=nt),
        grid=(b, nt),
        in_specs=[pl.BlockSpec((1, 1, nq, TM), lambda bb, i: (bb, i, 0, 0)),
                  pl.BlockSpec((1, t, 128), lambda bb, i: (bb, 0, 0)),
                  pl.BlockSpec((1, nt, GQA_KV_HEADS * GQA_VROWS, TM), lambda bb, i: (bb, 0, 0, 0)),
                  pl.BlockSpec((1, 8, 128), lambda bb, i: (bb, 0, 0))],
        out_specs=pl.BlockSpec((1, 1, nq, TM), lambda bb, i: (bb, i, 0, 0)),
        out_shape=jax.ShapeDtypeStruct((b, nt, nq, TM), BF16),
        scratch_shapes=_attn_scratch(GQA_VROWS),
        compiler_params=_params("parallel", "arbitrary"),
        name="gqa_attn",
    )(aq, ak, av, kmax)


def _diff_kernel(q_ref, k_ref, v_ref, kmax_ref, lam_ref, gain_ref, o_ref, *scratch, n_kv, lam_init):
    _attn_core(q_ref, k_ref, v_ref, kmax_ref, scratch, n_kv=n_kv,
               k_col=lambda hc: (hc // 2) * 128, k_row=lambda hc: hc, q_slot=lambda hc: hc % 2,
               v_row=lambda hc: (hc // 2) * DIFF_VROWS, v_rows=DIFF_VROWS, scores_between=False)
    acc_ref = scratch[3]
    lp = lam_ref[...]
    lam = (jnp.exp(jnp.sum(lp[0:1] * lp[1:2], axis=1, keepdims=True))
           - jnp.exp(jnp.sum(lp[2:3] * lp[3:4], axis=1, keepdims=True)) + lam_init)
    gain = gain_ref[...]
    nv = 2 * HEAD_DIM
    for hh in range(DIFF_HEADS):
        a0, a1 = acc_ref[2 * hh], acc_ref[2 * hh + 1]
        o = a0[:nv] / a0[nv:nv + 1] - lam * (a1[:nv] / a1[nv:nv + 1])
        y = o * lax.rsqrt(jnp.mean(o * o, axis=0, keepdims=True) + NORM_EPS) * gain
        o_ref[0, 0, hh * nv:(hh + 1) * nv, :] = (y * (1.0 - lam_init)).astype(o_ref.dtype)


def _diff_attention(dq, dk, dv, kmax, lam_p, gain, layer_idx):
    b, nt, nq, _ = dq.shape
    t = nt * TM
    assert nt % 2 == 1
    lam_init = 0.8 - 0.6 * math.exp(-0.3 * layer_idx)
    return pl.pallas_call(
        functools.partial(_diff_kernel, n_kv=nt, lam_init=lam_init),
        grid=(b, nt),
        in_specs=[pl.BlockSpec((1, 1, nq, TM), lambda bb, i: (bb, i, 0, 0)),
                  pl.BlockSpec((1, t, DIFF_QK), lambda bb, i: (bb, 0, 0)),
                  pl.BlockSpec((1, nt, DIFF_HEADS * DIFF_VROWS, TM), lambda bb, i: (bb, 0, 0, 0)),
                  pl.BlockSpec((1, 8, 128), lambda bb, i: (bb, 0, 0)),
                  _const_spec(lam_p.shape),
                  _const_spec((2 * HEAD_DIM, 1))],
        out_specs=pl.BlockSpec((1, 1, DIFF_V, TM), lambda bb, i: (bb, i, 0, 0)),
        out_shape=jax.ShapeDtypeStruct((b, nt, DIFF_V, TM), BF16),
        scratch_shapes=_attn_scratch(DIFF_VROWS),
        compiler_params=_params("parallel", "arbitrary"),
        name="diff_attn",
    )(dq, dk, dv, kmax, lam_p, gain.reshape(2 * HEAD_DIM, 1))


def _split_bf16(x):
    hi = x.astype(BF16)
    return hi, (x - hi.astype(F32)).astype(BF16)


def _gla_kernel(qk_ref, la_ref, vt_ref, o_ref, st_ref, *, backward):
    @pl.when(pl.program_id(1) == 0)
    def _():
        st_ref[...] = jnp.zeros(st_ref.shape, F32)

    n_chunk = TM // GLA_CHUNK
    r = lax.broadcasted_iota(jnp.int32, (TM, TM), 0)
    c = lax.broadcasted_iota(jnp.int32, (TM, TM), 1)
    same = (r // GLA_CHUNK) == (c // GLA_CHUNK)
    causal = (c >= r) if backward else (c <= r)
    tri = jnp.where(same & causal, 1.0, 0.0).astype(BF16)
    blk = jnp.where(same, 1.0, 0.0).astype(BF16)
    la_hi, la_lo = _split_bf16(la_ref[0])
    cum = jnp.dot(tri, la_hi, preferred_element_type=F32) + jnp.dot(tri, la_lo, preferred_element_type=F32)
    tot = jnp.dot(blk, la_hi, preferred_element_type=F32) + jnp.dot(blk, la_lo, preferred_element_type=F32)
    q = qk_ref[0, :, :GLA_QK]
    k = qk_ref[0, :, GLA_QK:]
    q_in = (q * jnp.exp(cum)).astype(BF16)
    k_in = k * jnp.exp(-cum)
    k_end = k * jnp.exp(tot - cum)
    dec = jnp.exp(tot)
    lane = lax.broadcasted_iota(jnp.int32, (TM, 128), 1)
    row = lax.broadcasted_iota(jnp.int32, (TM, 128), 0)
    keep = same & ((r >= c) if backward else (r <= c))
    order = range(n_chunk - 1, -1, -1) if backward else range(n_chunk)
    hs = range(GLA_HEADS)
    pairs = [slice((h // 2) * 128, (h // 2 + 1) * 128) for h in hs]
    head_lanes = [(lane // GLA_DK) == (h % 2) for h in hs]
    q_pair = [q_in[:, pairs[h]] for h in hs]
    k_in_h = [jnp.where(head_lanes[h], k_in[:, pairs[h]], 0.0).astype(BF16) for h in hs]
    k_end_h = [jnp.where(head_lanes[h], k_end[:, pairs[h]], 0.0) for h in hs]
    vts = [vt_ref[0, 0, h * GLA_DV:(h + 1) * GLA_DV, :] for h in hs]
    a_ts = [jnp.where(keep, _nt(k_in_h[h], q_pair[h]), 0.0).astype(BF16) for h in hs]
    o_ts = [jnp.dot(vts[h], a_ts[h], preferred_element_type=F32) for h in hs]
    s_ts = [st_ref[h] for h in hs]
    for ci in order:
        in_chunk = (row // GLA_CHUNK) == ci
        for h in hs:
            q_c = jnp.where(in_chunk, q_pair[h], jnp.zeros_like(q_pair[h]))
            o_ts[h] = o_ts[h] + _nt(s_ts[h].astype(BF16), q_c)
            k_c = jnp.where(in_chunk, k_end_h[h], 0.0).astype(BF16)
            s_ts[h] = (s_ts[h] * dec[ci * GLA_CHUNK:ci * GLA_CHUNK + 1, pairs[h]]
                       + jnp.dot(vts[h], k_c, preferred_element_type=F32))
    for h in hs:
        st_ref[h] = s_ts[h]
        o_ref[0, 0, h * GLA_DV:(h + 1) * GLA_DV, :] = o_ts[h]


def _gla_direction(qk, la, vt, backward):
    b, t, _ = qk.shape
    nt = t // TM
    d = 1 if backward else 0
    tile = (lambda i: jnp.where(i == 0, 0, nt - i)) if backward else (lambda i: i)
    return pl.pallas_call(
        functools.partial(_gla_kernel, backward=backward),
        grid=(b, nt),
        in_specs=[pl.BlockSpec((1, TM, 2 * GLA_QK), lambda bb, i: (bb, tile(i), 0)),
                  pl.BlockSpec((1, TM, GLA_QK), lambda bb, i: (bb, tile(i), d)),
                  pl.BlockSpec((1, 1, GLA_V, TM), lambda bb, i: (bb, tile(i), 0, 0))],
        out_specs=pl.BlockSpec((1, 1, GLA_V, TM), lambda bb, i: (bb, tile(i), 0, 0)),
        out_shape=jax.ShapeDtypeStruct((b, nt, GLA_V, TM), F32),
        scratch_shapes=[pltpu.VMEM((GLA_HEADS, GLA_DV, 128), F32)],
        compiler_params=_params("parallel", "arbitrary"),
        name="gla_bwd" if backward else "gla_fwd",
    )(qk, la, vt)


def _post_phases(xs, mods, g, yas, ybs, woa_ref, wob_ref, w1_ref, w2_ref, ff_chunk):
    n = len(xs)
    ys = [jnp.dot(yas[s], woa_ref[...], preferred_element_type=F32)
          + jnp.dot(ybs[s].astype(F32).T.astype(BF16), wob_ref[...], preferred_element_type=F32) for s in range(n)]
    x1s = [xs[s] + mods[s][2:3] * _rms(ys[s], g[1:2]) for s in range(n)]
    hs = [(_rms(x1s[s], g[2:3]) * (1.0 + mods[s][4:5]) + mods[s][3:4]).astype(BF16) for s in range(n)]
    d_ff = w1_ref.shape[1]
    accs = [jnp.zeros(xs[0].shape, F32) for _ in range(n)]
    for c in range(d_ff // ff_chunk):
        cols = slice(c * ff_chunk, (c + 1) * ff_chunk)
        for s in range(n):
            u = jnp.dot(hs[s], w1_ref[:, cols], preferred_element_type=F32)
            u = jnp.square(jnp.maximum(u, 0.0)).astype(BF16)
            accs[s] = accs[s] + jnp.dot(u, w2_ref[cols, :], preferred_element_type=F32)
    return [x1s[s] + mods[s][5:6] * _rms(accs[s], g[3:4]) for s in range(n)]


def _post_gla_kernel(xs_ref, of_ref, ob_ref, gg_ref, gain_ref, yb_ref, mods_ref, g_ref, woa_ref, wob_ref, w1_ref,
                     w2_ref, o_ref, *, ff_chunk):
    gain = gain_ref[...]
    yas = []
    for s in range(POST_PB):
        parts = []
        for hh in range(GLA_HEADS):
            rows = slice(hh * GLA_DV, (hh + 1) * GLA_DV)
            o = of_ref[s, 0, rows, :] + ob_ref[s, 0, rows, :]
            gt = gg_ref[s, 0, rows, :]
            y = o * lax.rsqrt(jnp.mean(o * o, axis=0, keepdims=True) + NORM_EPS) * gain
            parts.append(y * (gt / (1.0 + jnp.exp(-gt))))
        yas.append(jnp.concatenate(parts, axis=0).T.astype(BF16))
    outs = _post_phases([xs_ref[s] for s in range(POST_PB)], [mods_ref[s] for s in range(POST_PB)], g_ref[...], yas,
                        [yb_ref[s, 0] for s in range(POST_PB)], woa_ref, wob_ref, w1_ref, w2_ref, ff_chunk)
    for s in range(POST_PB):
        o_ref[s] = outs[s]


def _post_kernel(xs_ref, ya_ref, yb_ref, mods_ref, g_ref, woa_ref, wob_ref, w1_ref, w2_ref, o_ref, *, ff_chunk):
    outs = _post_phases([xs_ref[s] for s in range(POST_PB)], [mods_ref[s] for s in range(POST_PB)], g_ref[...],
                        [ya_ref[s].astype(BF16) for s in range(POST_PB)], [yb_ref[s, 0] for s in range(POST_PB)],
                        woa_ref, wob_ref, w1_ref, w2_ref, ff_chunk)
    for s in range(POST_PB):
        o_ref[s] = outs[s]


def _post(xs, ya, yb_t, mods_l, g, w_out, w1, w2, skip_ctx):
    b, t, d = xs.shape
    nt = t // TM
    pb = POST_PB
    half = w_out.shape[0] // 2
    woa = w_out[:half].astype(BF16)
    wob = w_out[half:].astype(BF16)
    w1 = w1.astype(BF16)
    w2 = w2.astype(BF16)
    off = 1 if skip_ctx else 0
    tok = lambda n: pl.BlockSpec((pb, TM, n), lambda bb, i: (bb, i + off, 0))
    tr = lambda n: pl.BlockSpec((pb, 1, n, TM), lambda bb, i: (bb, i + off, 0, 0))
    if isinstance(ya, tuple):
        o_f, o_b, gg, gain = ya
        body, name = _post_gla_kernel, "post_gla"
        a_specs = [tr(half), tr(half), tr(half), _const_spec((GLA_DV, 1))]
        a_args = [o_f, o_b, gg, gain.reshape(GLA_DV, 1)]
    else:
        body, name = _post_kernel, "post"
        a_specs, a_args = [tok(half)], [ya]
    return pl.pallas_call(
        functools.partial(body, ff_chunk=1024),
        grid=(b // pb, nt - off),
        in_specs=[tok(d)] + a_specs + [
                  tr(half),
                  pl.BlockSpec((pb, N_MOD, d), lambda bb, i: (_mod_block(bb, i + off, pb), 0, 0)),
                  _const_spec(g.shape),
                  _const_spec(woa.shape), _const_spec(wob.shape), _const_spec(w1.shape), _const_spec(w2.shape)],
        out_specs=pl.BlockSpec((pb, TM, d), lambda bb, i: (bb, i, 0)),
        out_shape=jax.ShapeDtypeStruct((b, t - off * TM, d), F32),
        compiler_params=_params("parallel", "parallel"),
        name=name,
    )(xs, *a_args, yb_t, mods_l, g, woa, wob, w1, w2)


def _dot3(a, b):
    ah, al = _split_bf16(a)
    bh, bl = _split_bf16(b)
    d = lambda x, y: jnp.dot(x, y, preferred_element_type=F32)
    return d(ah, bh) + (d(al, bh) + d(ah, bl))


def _filter_kernel(w1t_ref, w1c_ref, w1s_ref, b1_ref, w2_ref, b2_ref, w3_ref, delta_ref, o_ref, *, seq, rb):
    n = pl.program_id(0) * rb + lax.broadcasted_iota(jnp.int32, (rb, 1), 0)
    d = jnp.where(n < seq, n, 2 * seq - n).astype(F32)
    t = d / max(seq - 1, 1)
    w = (2 * math.pi / seq) * d
    band = lax.broadcasted_iota(jnp.int32, (1, HY_BANDS), 1).astype(F32)
    bands = 1e-4 + band * ((HY_BANDS - 1 - 1e-4) / (HY_BANDS - 1))
    ang = w * bands
    pre = t * w1t_ref[...] + _dot3(jnp.cos(ang), w1c_ref[...]) - _dot3(jnp.sin(ang), w1s_ref[...]) + b1_ref[...]
    hid = jnp.sin(pre)
    hid = jnp.sin(_dot3(hid, w2_ref[...]) + b2_ref[...])
    decay = jnp.where(n == seq, 0.0, jnp.exp(-t * delta_ref[...]))
    for o in range(HY_ORDER):
        o_ref[o] = (_dot3(hid, w3_ref[0, o]) * decay).astype(o_ref.dtype)


def _hyena_kernels(seq, f_w1, f_b1, f_w2, f_b2, f_w3):
    rb = min(512, seq)
    nblk = 2 * seq // rb
    ffn = f_w2.shape[0]
    w3 = f_w3.reshape(ffn, HY_ORDER, 2, HY_CH).transpose(2, 1, 0, 3)
    deltas = jnp.abs(jnp.linspace(math.log(HY_TARGET) / HY_FAST_PCT, math.log(HY_TARGET) / HY_SLOW_PCT,
                                  HY_CH, dtype=F32)).reshape(1, HY_CH)
    return pl.pallas_call(
        functools.partial(_filter_kernel, seq=seq, rb=rb),
        grid=(nblk,),
        in_specs=[_const_spec((1, ffn)), _const_spec((HY_BANDS, ffn)), _const_spec((HY_BANDS, ffn)),
                  _const_spec((1, ffn)), _const_spec((ffn, ffn)), _const_spec((1, ffn)),
                  pl.BlockSpec((1, HY_ORDER, ffn, HY_CH), lambda j: ((2 * j) // nblk, 0, 0, 0)),
                  _const_spec((1, HY_CH))],
        out_specs=pl.BlockSpec((HY_ORDER, rb, HY_CH), lambda j: (0, j, 0)),
        out_shape=jax.ShapeDtypeStruct((HY_ORDER, 2 * seq, HY_CH), BF16),
        compiler_params=_params("parallel"),
        name="hy_filter",
    )(f_w1[0:1], f_w1[1:1 + HY_BANDS], f_w1[1 + HY_BANDS:], f_b1.reshape(1, ffn), f_w2, f_b2.reshape(1, ffn), w3,
      deltas)


def _dft_tables(seq):
    n = 2 * seq
    r = math.isqrt(n)
    assert r * r == n
    idx = jnp.arange(r, dtype=jnp.int32)
    phi = (2 * math.pi / r) * ((idx[:, None] * idx[None, :]) % r).astype(F32)
    f1 = jnp.concatenate([jnp.cos(phi), -jnp.sin(phi)], axis=0).astype(BF16)
    m = (idx[:, None, None] + r * idx[None, :, None]) * idx[None, None, :]
    th = (2 * math.pi / n) * (m % n).astype(F32)
    c, s = jnp.cos(th), jnp.sin(th)
    blockm = lambda re, im: jnp.concatenate([jnp.concatenate([re, -im], axis=-1),
                                             jnp.concatenate([im, re], axis=-1)], axis=-2)
    g = blockm(c, -s).astype(BF16)
    ginv = blockm(jnp.swapaxes(c, 1, 2), jnp.swapaxes(s, 1, 2)).astype(BF16)
    finv = (jnp.concatenate([jnp.cos(phi.T), -jnp.sin(phi.T)], axis=1)[:r // 2] / n).astype(BF16)
    return r, f1, g, ginv, finv


DFT_BB = 8


def _dft1_kernel(f_ref, x_ref, o_ref, *, a0):
    f = f_ref[...]
    r = o_ref.shape[2]
    x = pltpu.einshape("abc->bac", x_ref[0, a0:].astype(F32))
    ys = [jnp.dot(f, x[j].astype(BF16), preferred_element_type=F32) for j in range(DFT_BB)]
    y = pltpu.einshape("jkc->kjc", jnp.stack(ys, axis=0))
    o_ref[0, 0] = y[:r].astype(o_ref.dtype)
    o_ref[0, 1] = y[r:].astype(o_ref.dtype)


def _dft_stage1(f, x4, a0=0):
    b, rows, r, c = x4.shape
    return pl.pallas_call(
        functools.partial(_dft1_kernel, a0=a0),
        grid=(b, r // DFT_BB),
        in_specs=[_const_spec(f.shape), pl.BlockSpec((1, rows, DFT_BB, c), lambda bi, j: (bi, 0, j, 0))],
        out_specs=pl.BlockSpec((1, 2, r, DFT_BB, c), lambda bi, j: (bi, 0, 0, j, 0)),
        out_shape=jax.ShapeDtypeStruct((b, 2, r, r, c), BF16),
        compiler_params=_params("parallel", "parallel"),
        name="hy_dft1",
    )(f, x4)


def _fspec_kernel(a_ref, g_ref, o_ref, *, kb, r):
    for kk in range(kb):
        a = a_ref[0, :, kk].reshape(2 * r, a_ref.shape[-1])
        o_ref[0, kk] = jnp.dot(g_ref[kk], a, preferred_element_type=F32)


def _filter_spectrum(a, g, kb):
    o, _, r, _, c = a.shape
    return pl.pallas_call(
        functools.partial(_fspec_kernel, kb=kb, r=r),
        grid=(o, r // kb),
        in_specs=[pl.BlockSpec((1, 2, kb, r, c), lambda oo, j: (oo, 0, j, 0, 0)),
                  pl.BlockSpec((kb, 2 * r, 2 * r), lambda oo, j: (j, 0, 0))],
        out_specs=pl.BlockSpec((1, kb, 2 * r, c), lambda oo, j: (oo, j, 0, 0)),
        out_shape=jax.ShapeDtypeStruct((o, r, 2 * r, c), F32),
        compiler_params=_params("parallel", "parallel"),
        name="hy_fspec",
    )(a, g)


def _dft2_kernel(a_ref, g_ref, gi_ref, k_ref, o_ref, *, kb, r):
    xs = [jnp.dot(g_ref[kk], a_ref[0, :, kk].reshape(2 * r, a_ref.shape[-1]), preferred_element_type=F32)
          for kk in range(kb)]
    ys = []
    for kk in range(kb):
        x, ks = xs[kk], k_ref[0, kk]
        xr, xi, kr, ki = x[:r], x[r:], ks[:r], ks[r:]
        ys.append(jnp.concatenate([xr * kr - xi * ki, xr * ki + xi * kr], axis=0).astype(BF16))
    bqs = [jnp.dot(gi_ref[kk], ys[kk], preferred_element_type=F32) for kk in range(kb)]
    for kk in range(kb):
        o_ref[0, 0, kk] = bqs[kk][:r].astype(o_ref.dtype)
        o_ref[0, 1, kk] = bqs[kk][r:].astype(o_ref.dtype)


def _dft_stage2(a, g, ginv, kspec, order, kb):
    b, _, r, _, c = a.shape
    blk = pl.BlockSpec((1, 2, kb, r, c), lambda j, bb: (bb, 0, j, 0, 0))
    tab = pl.BlockSpec((kb, 2 * r, 2 * r), lambda j, bb: (j, 0, 0))
    return pl.pallas_call(
        functools.partial(_dft2_kernel, kb=kb, r=r),
        grid=(r // kb, b),
        in_specs=[blk, tab, tab, pl.BlockSpec((1, kb, 2 * r, c), lambda j, bb: (order, j, 0, 0))],
        out_specs=blk,
        out_shape=jax.ShapeDtypeStruct(a.shape, BF16),
        compiler_params=_params("parallel", "arbitrary"),
        name="hy_dft2",
    )(a, g, ginv, kspec)


def _dft3_kernel(f_ref, b_ref, gate_ref, z_ref, bias_ref, head_ref, o_ref, *, a0):
    f = f_ref[...]
    bq = jnp.concatenate([b_ref[0, 0], b_ref[0, 1]], axis=0)
    bq = pltpu.einshape("kjc->jkc", bq.astype(F32))
    ys = [jnp.dot(f, bq[j].astype(BF16), preferred_element_type=F32) for j in range(DFT_BB)]
    y = pltpu.einshape("jac->ajc", jnp.stack(ys, axis=0))
    o_ref[0, :a0] = head_ref[0]
    o_ref[0, a0:] = gate_ref[0, a0:] * (y + z_ref[0, a0:] * bias_ref[...])


def _dft_stage3(finv, bq5, gate4, z4, bias, head4):
    b, _, r, _, c = bq5.shape
    rows = z4.shape[1]
    a0 = head4.shape[1]
    io = pl.BlockSpec((1, rows, DFT_BB, c), lambda bi, j: (bi, 0, j, 0))
    return pl.pallas_call(
        functools.partial(_dft3_kernel, a0=a0),
        grid=(b, r // DFT_BB),
        in_specs=[_const_spec(finv.shape), pl.BlockSpec((1, 2, r, DFT_BB, c), lambda bi, j: (bi, 0, 0, j, 0)), io, io,
                  _const_spec((1, 1, c)), pl.BlockSpec((1, a0, DFT_BB, c), lambda bi, j: (bi, 0, j, 0))],
        out_specs=io,
        out_shape=jax.ShapeDtypeStruct((b, rows, r, c), F32),
        compiler_params=_params("parallel", "parallel"),
        name="hy_dft3",
    )(finv, bq5, gate4, z4, bias, head4)


def _long_conv_gated(v, gates, kern, hy_bias, ctx_out, lc):
    b, t, c = v.shape
    r, f1, g, ginv, finv = _dft_tables(t - lc)
    assert lc % r == 0
    a0 = lc // r
    kb = 8
    kspec = _filter_spectrum(_dft_stage1(f1, kern.reshape(HY_ORDER, r, r, c)), g, kb)
    as_blocks = lambda x: x.reshape(b, t // r, r, c)
    z = as_blocks(v)
    heads = [jnp.zeros((b, a0, r, c), F32), ctx_out.reshape(b, a0, r, c)]
    for o in range(HY_ORDER):
        a = _dft_stage1(f1[:, :r // 2], z, a0)
        bq = _dft_stage2(a, g, ginv, kspec, o, kb)
        z = _dft_stage3(finv, bq, as_blocks(gates[o]), z, hy_bias[o].reshape(1, 1, c), heads[o])
    return z.reshape(b, t, c)


def _ctx_conv_kernel(v_ref, x1_ref, x2_ref, k_ref, ff_ref, fi_ref, bias_ref, o_ref):
    n = ff_ref.shape[1]
    z = v_ref[0]
    for o, gate_ref in enumerate((x1_ref, x2_ref)):
        ks = jnp.dot(ff_ref[...], k_ref[o], preferred_element_type=F32)
        x = jnp.dot(ff_ref[:, :TM], z.astype(BF16), preferred_element_type=F32)
        xr, xi, kr, ki = x[:n], x[n:], ks[:n], ks[n:]
        y = jnp.concatenate([xr * kr - xi * ki, xr * ki + xi * kr], axis=0).astype(BF16)
        conv = jnp.dot(fi_ref[...], y, preferred_element_type=F32)
        z = gate_ref[0] * (conv + z * bias_ref[o:o + 1])
    o_ref[0] = z


def _ctx_conv_gated(v, x1, x2, kern, hy_bias):
    b, _, c = v.shape
    seq = TM
    n = 2 * seq
    idx = jnp.arange(n, dtype=jnp.int32)
    th = (2 * math.pi / n) * ((idx[:, None] * idx[None, :]) % n).astype(F32)
    ff = jnp.concatenate([jnp.cos(th), -jnp.sin(th)], axis=0).astype(BF16)
    fi = (jnp.concatenate([jnp.cos(th), -jnp.sin(th)], axis=1)[:seq] / n).astype(BF16)
    io = pl.BlockSpec((1, seq, c), lambda bb: (bb, 0, 0))
    return pl.pallas_call(
        _ctx_conv_kernel,
        grid=(b,),
        in_specs=[io, io, io, _const_spec(kern.shape), _const_spec(ff.shape), _const_spec(fi.shape),
                  _const_spec(hy_bias.shape)],
        out_specs=io,
        out_shape=jax.ShapeDtypeStruct((b, seq, c), F32),
        compiler_params=_params("parallel"),
        name="hy_ctx",
    )(v, x1, x2, kern, ff, fi, hy_bias)


def _hyena(v, x1, x2, lc, f_w1, f_b1, f_w2, f_b2, f_w3, hy_bias):
    seq = v.shape[1] - lc
    yc = _ctx_conv_gated(v, x1, x2, _hyena_kernels(lc, f_w1, f_b1, f_w2, f_b2, f_w3), hy_bias)
    return _long_conv_gated(v, (x1, x2), _hyena_kernels(seq, f_w1, f_b1, f_w2, f_b2, f_w3), hy_bias, yc, lc)


def _rope_tables_t(seq, lc):
    rows = seq // GRID_W
    r = jnp.broadcast_to(jnp.arange(rows, dtype=F32)[:, None], (rows, GRID_W)).reshape(-1)
    col = jnp.broadcast_to(jnp.arange(GRID_W, dtype=F32)[None, :], (rows, GRID_W)).reshape(-1)
    n_pairs = HEAD_DIM // 4
    inv = ROPE_THETA ** (-jnp.arange(n_pairs, dtype=F32) / n_pairs)
    ang = jnp.concatenate([r[:, None] * inv, col[:, None] * inv], axis=-1)
    cos = jnp.concatenate([jnp.ones((lc, 2 * n_pairs), F32), jnp.cos(ang)], axis=0)
    sin = jnp.concatenate([jnp.zeros((lc, 2 * n_pairs), F32), jnp.sin(ang)], axis=0)
    return cos.T, sin.T


def kernel(x, c, ctx, c_ctx, w_ada, b_ada, norm_g, w_mlp_in, w_mlp_out, ev_w_in, ev_w_lr, ev_b_lr, ev_gla_g,
           ev_qk_g, ev_w_out, od_w_in, od_conv_w, od_conv_b, od_f_w1, od_f_b1, od_f_w2, od_f_b2, od_f_w3,
           od_hy_bias, od_lam, od_diff_g, od_w_out):
    bsz, seq, d = x.shape
    lc = ctx.shape[1]
    depth = w_ada.shape[0]
    assert lc == TM and seq % TM == 0 and bsz <= MOD_CTX_ROW and bsz % POST_PB == 0 and bsz % IN_PB == 0
    cos_t, sin_t = _rope_tables_t(seq, lc)
    cc = jnp.zeros((8, d), F32).at[:bsz].set(c).at[MOD_CTX_ROW:MOD_CTX_ROW + max(POST_PB, IN_PB)].set(c_ctx)
    mods = _mods(cc, w_ada, b_ada).reshape(depth, 8, N_MOD, d)
    xs = jnp.concatenate([ctx, x], axis=1)
    for i in range(depth):
        last = i == depth - 1
        j = i // 2
        g = norm_g[i]
        if i % 2 == 0:
            qk, la, gv, gg, aq, ak, av, kn = _even_in(xs, mods[i], g[0:1], ev_w_in[j], ev_w_lr[j], ev_b_lr[j],
                                                  ev_qk_g[j], cos_t, sin_t)
            ya = (_gla_direction(qk, la, gv, False), _gla_direction(qk, la, gv, True), gg, ev_gla_g[j])
            yb = _gqa_attention(aq, ak, av, jnp.max(kn, axis=1))
            w_out = ev_w_out[j]
        else:
            hv, hx1, hx2, dq, dk, dv, kn = _odd_in(xs, mods[i], g[0:1], od_w_in[j], od_conv_w[j], od_conv_b[j],
                                                   cos_t, sin_t)
            ya = _hyena(hv, hx1, hx2, lc, od_f_w1[j], od_f_b1[j], od_f_w2[j], od_f_b2[j], od_f_w3[j], od_hy_bias[j])
            yb = _diff_attention(dq, dk, dv, jnp.max(kn, axis=1), od_lam[j], od_diff_g[j], i)
            w_out = od_w_out[j]
        xs = _post(xs, ya, yb, mods[i], g, w_out, w_mlp_in[i], w_mlp_out[i], skip_ctx=last)
    return xs
```

```python
import functools
import math

import jax
import jax.numpy as jnp
import numpy as np
from jax import lax
from jax.experimental import pallas as pl
from jax.experimental.pallas import tpu as pltpu

F32 = jnp.float32
BF16 = jnp.bfloat16

TM = 256
GRID_W = 64
HEAD_DIM = 64
ROPE_THETA = 10000.0
NORM_EPS = 1e-6
N_MOD = 6
VMEM_LIMIT = 56 * 1024 * 1024

GLA_HEADS = 4
GLA_DK = 64
GLA_DV = 128
GLA_RANK = 16
GLA_NORMALIZER = 16.0
GLA_CHUNK = 64
GLA_QK = GLA_HEADS * GLA_DK
GLA_V = GLA_HEADS * GLA_DV
GQA_HEADS = 8
GQA_KV_HEADS = 2
HY_CH = 512
HY_ORDER = 2
HY_BANDS = 16
HY_TARGET = 1e-2
HY_FAST_PCT = 0.3
HY_SLOW_PCT = 1.5
HY_COLS = (HY_ORDER + 1) * HY_CH
DIFF_HEADS = 4
DIFF_QK = DIFF_HEADS * 2 * HEAD_DIM
DIFF_V = DIFF_HEADS * 2 * HEAD_DIM
ONES_ROWS = 16
N_SCORE_HEADS = 8
LOG2E = 1.4426950408889634
Q_SCALE = HEAD_DIM ** -0.5 * LOG2E
GQA_VROWS = HEAD_DIM + ONES_ROWS
SCORE_BOUND_SLACK = 1.01
SCORE_BOUND_LIMIT = 60.0
DIFF_VROWS = 2 * HEAD_DIM + ONES_ROWS


def _params(*sem):
    return pltpu.CompilerParams(dimension_semantics=sem, vmem_limit_bytes=VMEM_LIMIT)


def _const_spec(shape):
    nd = len(shape)
    return pl.BlockSpec(shape, lambda *_: (0,) * nd, pipeline_mode=pl.Buffered(1))


def _rms(x, g):
    return x * lax.rsqrt(jnp.mean(x * x, axis=-1, keepdims=True) + NORM_EPS) * g


MOD_CTX_ROW = 4
POST_PB = 2
IN_PB = 2


def _mod_block(bb, i, pb):
    return jnp.where(i == 0, MOD_CTX_ROW // pb, bb)


def _mods_kernel(cc_ref, w_ref, b_ref, o_ref):
    cc = cc_ref[...]
    s = cc / (1.0 + jnp.exp(-cc))
    o_ref[0] = jnp.dot(s.astype(BF16), w_ref[0].astype(BF16), preferred_element_type=F32) + b_ref[0]


def _mods(cc, w_ada, b_ada):
    depth, d, n = w_ada.shape
    nb = n // 4
    return pl.pallas_call(
        _mods_kernel,
        grid=(depth, n // nb),
        in_specs=[pl.BlockSpec((8, d), lambda l, j: (0, 0)),
                  pl.BlockSpec((1, d, nb), lambda l, j: (l, 0, j)),
                  pl.BlockSpec((1, 1, nb), lambda l, j: (l, 0, j))],
        out_specs=pl.BlockSpec((1, 8, nb), lambda l, j: (l, 0, j)),
        out_shape=jax.ShapeDtypeStruct((depth, 8, n), F32),
        compiler_params=_params("parallel", "parallel"),
        name="ada_mods",
    )(cc, w_ada, b_ada.reshape(depth, 1, n))


def _normed_input(x, mods, g):
    return (_rms(x, g) * (1.0 + mods[1:2]) + mods[0:1]).astype(BF16)


def _rope_rows(y, cos, sin):
    x1, x2 = y[:32], y[32:]
    return x1 * cos - x2 * sin, x1 * sin + x2 * cos


def _log_sigmoid(x):
    return jnp.minimum(x, 0.0) - jnp.log(1.0 + jnp.exp(-jnp.abs(x)))


def _nt(a, b):
    return lax.dot_general(a, b, (((1,), (1,)), ((), ())), preferred_element_type=F32)


def _max_sq_norm_row(parts):
    n2 = sum(jnp.sum(jnp.square(p.astype(BF16).astype(F32)), axis=0, keepdims=True) for p in parts)
    return jnp.broadcast_to(jnp.max(n2, axis=1, keepdims=True), (1, 128))


def _even_in_kernel(xs_ref, mods_ref, g_ref, wn_ref, wt_ref, wlr_ref, blr_ref, gq_ref, gk_ref,
                    cos_ref, sin_ref, qk_ref, la_ref, gv_ref, gg_ref, aq_ref, ak_ref, av_ref, kn_ref):
    hs = [_normed_input(xs_ref[s], mods_ref[s], g_ref[0:1]) for s in range(IN_PB)]
    ps = [jnp.dot(h, wn_ref[...], preferred_element_type=F32) for h in hs]
    pts = [_nt(wt_ref[...], h) for h in hs]
    cos, sin = cos_ref[...], sin_ref[...]
    gq, gk_gain = gq_ref[...], gk_ref[...]
    ones = jnp.ones((ONES_ROWS, TM), BF16)
    for s in range(IN_PB):
        p, pt_all = ps[s], pts[s]
        qk_ref[s, :, :GLA_QK] = p[:, :GLA_QK] * GLA_DK ** -0.5
        qk_ref[s, :, GLA_QK:] = p[:, GLA_QK:2 * GLA_QK]
        lr = p[:, 2 * GLA_QK:].astype(BF16)
        gk = jnp.dot(lr, wlr_ref[...], preferred_element_type=F32) + blr_ref[...]
        la_ref[s] = _log_sigmoid(gk) * (1.0 / GLA_NORMALIZER)
        gv_ref[s, 0] = pt_all[:GLA_V].astype(BF16)
        gg_ref[s, 0] = pt_all[GLA_V:2 * GLA_V]
        pt = pt_all[2 * GLA_V:]
        for hh in range(GQA_HEADS):
            blk = pt[hh * 64:(hh + 1) * 64]
            y = blk * lax.rsqrt(jnp.mean(blk * blk, axis=0, keepdims=True) + NORM_EPS) * gq
            o1, o2 = _rope_rows(y, cos, sin)
            aq_ref[s, 0, hh * 64:hh * 64 + 32, :] = (o1 * Q_SCALE).astype(BF16)
            aq_ref[s, 0, hh * 64 + 32:hh * 64 + 64, :] = (o2 * Q_SCALE).astype(BF16)
        kparts, knorms = [], []
        for hh in range(GQA_KV_HEADS):
            blk = pt[512 + hh * 64:512 + (hh + 1) * 64]
            y = blk * lax.rsqrt(jnp.mean(blk * blk, axis=0, keepdims=True) + NORM_EPS) * gk_gain
            o1, o2 = _rope_rows(y, cos, sin)
            kparts += [o1, o2]
            knorms.append(_max_sq_norm_row([o1, o2]))
        ak_ref[s] = jnp.concatenate(kparts, axis=0).T.astype(BF16)
        kn_ref[s, 0] = jnp.concatenate(knorms + [jnp.zeros((8 - GQA_KV_HEADS, 128), F32)], axis=0)
        for hh in range(GQA_KV_HEADS):
            av_ref[s, 0, hh * GQA_VROWS:hh * GQA_VROWS + 64, :] = pt[640 + hh * 64:640 + (hh + 1) * 64].astype(BF16)
            av_ref[s, 0, hh * GQA_VROWS + 64:(hh + 1) * GQA_VROWS, :] = ones


def _odd_in_kernel(xs_ref, prev_ref, next_ref, mods_ref, g_ref, wn_ref, wt_ref, cw_ref, cb_ref, cos_ref, sin_ref,
                   v_ref, x1_ref, x2_ref, dq_ref, dk_ref, dv_ref, kn_ref, *, n_tiles):
    i = pl.program_id(1)
    hs = [_normed_input(xs_ref[s], mods_ref[s], g_ref[0:1]) for s in range(IN_PB)]
    h_ext = [jnp.concatenate([hs[s], _normed_input(prev_ref[s], mods_ref[s], g_ref[0:1]),
                              _normed_input(next_ref[s], mods_ref[s], g_ref[0:1])], axis=0) for s in range(IN_PB)]
    us = [jnp.dot(h, wn_ref[...], preferred_element_type=F32) for h in h_ext]
    pts = [_nt(wt_ref[...], h) for h in hs]
    cos, sin = cos_ref[...], sin_ref[...]
    ones = jnp.ones((ONES_ROWS, TM), BF16)
    nv = 2 * HEAD_DIM
    cw = cw_ref[...]
    row_id = lax.broadcasted_iota(jnp.int32, (TM, HY_COLS), 0)
    no_prev = (i == 0) | (i == 1)
    no_next = (i == 0) | (i == n_tiles - 1)
    for s in range(IN_PB):
        pt = pts[s]
        u = us[s][:TM]
        prev_row = jnp.where(no_prev, 0.0, us[s][TM + 7:TM + 8])
        next_row = jnp.where(no_next, 0.0, us[s][TM + 8:TM + 9])
        up = jnp.where(row_id == 0, prev_row, pltpu.roll(u, 1, axis=0))
        dn = jnp.where(row_id == TM - 1, next_row, pltpu.roll(u, TM - 1, axis=0))
        conv = up * cw[0:1] + u * cw[1:2] + dn * cw[2:3] + cb_ref[...]
        v_ref[s] = conv[:, :HY_CH]
        x1_ref[s] = conv[:, HY_CH:2 * HY_CH]
        x2_ref[s] = conv[:, 2 * HY_CH:]
        kparts, knorms = [], []
        for hh in range(2 * DIFF_HEADS):
            o1, o2 = _rope_rows(pt[hh * 64:(hh + 1) * 64], cos, sin)
            dq_ref[s, 0, hh * 64:hh * 64 + 32, :] = (o1 * Q_SCALE).astype(BF16)
            dq_ref[s, 0, hh * 64 + 32:hh * 64 + 64, :] = (o2 * Q_SCALE).astype(BF16)
            k1, k2 = _rope_rows(pt[DIFF_QK + hh * 64:DIFF_QK + (hh + 1) * 64], cos, sin)
            kparts += [k1, k2]
            knorms.append(_max_sq_norm_row([k1, k2]))
        dk_ref[s] = jnp.concatenate(kparts, axis=0).T.astype(BF16)
        kn_ref[s, 0] = jnp.concatenate(knorms, axis=0)
        for hh in range(DIFF_HEADS):
            rows = pt[2 * DIFF_QK + hh * nv:2 * DIFF_QK + (hh + 1) * nv]
            dv_ref[s, 0, hh * DIFF_VROWS:hh * DIFF_VROWS + nv, :] = rows.astype(BF16)
            dv_ref[s, 0, hh * DIFF_VROWS + nv:(hh + 1) * DIFF_VROWS, :] = ones


def _pair_perm(n_heads):
    base = np.concatenate([np.arange(0, HEAD_DIM, 2), np.arange(1, HEAD_DIM, 2)])
    return np.concatenate([h * HEAD_DIM + base for h in range(n_heads)])


def _even_in(xs, mods_l, g0, w_in, w_lr, b_lr, qk_g, cos_t, sin_t):
    b, t, d = xs.shape
    nt = t // TM
    c_v = 2 * GLA_QK
    c_g = c_v + GLA_V
    c_lr0 = c_g + GLA_V
    c_lr = c_lr0 + 2 * GLA_RANK
    wn = jnp.concatenate([w_in[:, :c_v], w_in[:, c_lr0:c_lr], jnp.zeros((d, 128 - 2 * GLA_RANK), F32)],
                         axis=1).astype(BF16)
    cq = c_lr + GQA_HEADS * HEAD_DIM
    ck = cq + GQA_KV_HEADS * HEAD_DIM
    wq = w_in[:, c_lr:cq][:, _pair_perm(GQA_HEADS)]
    wk = w_in[:, cq:ck][:, _pair_perm(GQA_KV_HEADS)]
    wt = jnp.concatenate([w_in[:, c_v:c_lr0], wq, wk, w_in[:, ck:]], axis=1).T.astype(BF16)
    wlr = jnp.zeros((128, 2 * GLA_QK), F32)
    wlr = wlr.at[:GLA_RANK, :GLA_QK].set(w_lr[0]).at[GLA_RANK:2 * GLA_RANK, GLA_QK:].set(w_lr[1]).astype(BF16)
    blr = b_lr.reshape(1, 2 * GLA_QK)
    perm = _pair_perm(1)
    gq = qk_g[0][perm].reshape(HEAD_DIM, 1)
    gk = qk_g[1][perm].reshape(HEAD_DIM, 1)
    tok = lambda n: pl.BlockSpec((IN_PB, TM, n), lambda bb, i: (bb, i, 0))
    tr = lambda n: pl.BlockSpec((IN_PB, 1, n, TM), lambda bb, i: (bb, i, 0, 0))
    return pl.pallas_call(
        _even_in_kernel,
        grid=(b // IN_PB, nt),
        in_specs=[tok(d),
                  pl.BlockSpec((IN_PB, N_MOD, d), lambda bb, i: (_mod_block(bb, i, IN_PB), 0, 0)),
                  _const_spec((1, d)),
                  _const_spec(wn.shape), _const_spec(wt.shape), _const_spec(wlr.shape), _const_spec(blr.shape),
                  _const_spec(gq.shape), _const_spec(gk.shape),
                  pl.BlockSpec((32, TM), lambda bb, i: (0, i)),
                  pl.BlockSpec((32, TM), lambda bb, i: (0, i))],
        out_specs=[tok(2 * GLA_QK), tok(2 * GLA_QK), tr(GLA_V), tr(GLA_V), tr(512), tok(128),
                   tr(GQA_KV_HEADS * GQA_VROWS), pl.BlockSpec((IN_PB, 1, 8, 128), lambda bb, i: (bb, i, 0, 0))],
        out_shape=[jax.ShapeDtypeStruct((b, t, 2 * GLA_QK), F32),
                   jax.ShapeDtypeStruct((b, t, 2 * GLA_QK), F32),
                   jax.ShapeDtypeStruct((b, nt, GLA_V, TM), BF16),
                   jax.ShapeDtypeStruct((b, nt, GLA_V, TM), F32),
                   jax.ShapeDtypeStruct((b, nt, 512, TM), BF16),
                   jax.ShapeDtypeStruct((b, t, 128), BF16),
                   jax.ShapeDtypeStruct((b, nt, GQA_KV_HEADS * GQA_VROWS, TM), BF16),
                   jax.ShapeDtypeStruct((b, nt, 8, 128), F32)],
        compiler_params=_params("parallel", "parallel"),
        name="even_in",
    )(xs, mods_l, g0, wn, wt, wlr, blr, gq, gk, cos_t, sin_t)


def _odd_in(xs, mods_l, g0, w_in, conv_w, conv_b, cos_t, sin_t):
    b, t, d = xs.shape
    nt = t // TM
    r8 = TM // 8
    last_blk = t // 8 - 1
    wn = w_in[:, :HY_COLS].astype(BF16)
    perm = _pair_perm(2 * DIFF_HEADS)
    wq = w_in[:, HY_COLS:HY_COLS + DIFF_QK][:, perm]
    wk = w_in[:, HY_COLS + DIFF_QK:HY_COLS + 2 * DIFF_QK][:, perm]
    wt = jnp.concatenate([wq, wk, w_in[:, HY_COLS + 2 * DIFF_QK:]], axis=1).T.astype(BF16)
    tok = lambda n: pl.BlockSpec((IN_PB, TM, n), lambda bb, i: (bb, i, 0))
    tr = lambda n: pl.BlockSpec((IN_PB, 1, n, TM), lambda bb, i: (bb, i, 0, 0))
    hy = jax.ShapeDtypeStruct((b, t, HY_CH), F32)
    return pl.pallas_call(
        functools.partial(_odd_in_kernel, n_tiles=nt),
        grid=(b // IN_PB, nt),
        in_specs=[tok(d),
                  pl.BlockSpec((IN_PB, 8, d), lambda bb, i: (bb, jnp.maximum(i * r8 - 1, 0), 0)),
                  pl.BlockSpec((IN_PB, 8, d), lambda bb, i: (bb, jnp.minimum((i + 1) * r8, last_blk), 0)),
                  pl.BlockSpec((IN_PB, N_MOD, d), lambda bb, i: (_mod_block(bb, i, IN_PB), 0, 0)),
                  _const_spec((1, d)),
                  _const_spec(wn.shape), _const_spec(wt.shape),
                  _const_spec(conv_w.shape), _const_spec((1, HY_COLS)),
                  pl.BlockSpec((32, TM), lambda bb, i: (0, i)),
                  pl.BlockSpec((32, TM), lambda bb, i: (0, i))],
        out_specs=[tok(HY_CH), tok(HY_CH), tok(HY_CH), tr(DIFF_QK), tok(DIFF_QK), tr(DIFF_HEADS * DIFF_VROWS),
                   pl.BlockSpec((IN_PB, 1, 8, 128), lambda bb, i: (bb, i, 0, 0))],
        out_shape=[hy, hy, hy,
                   jax.ShapeDtypeStruct((b, nt, DIFF_QK, TM), BF16),
                   jax.ShapeDtypeStruct((b, t, DIFF_QK), BF16),
                   jax.ShapeDtypeStruct((b, nt, DIFF_HEADS * DIFF_VROWS, TM), BF16),
                   jax.ShapeDtypeStruct((b, nt, 8, 128), F32)],
        compiler_params=_params("parallel", "parallel"),
        name="odd_in",
    )(xs, xs, xs, mods_l, g0, wn, wt, conv_w, conv_b.reshape(1, HY_COLS), cos_t, sin_t)


def _attn_core(q_ref, k_ref, v_ref, kmax_ref, scratch, *, n_kv, k_col, k_row, q_slot, v_row, v_rows, scores_between):
    qpad_ref, bound_ref, m_ref, acc_ref, s0_ref, s1_ref, c0_ref, c1_ref, p0_ref, p1_ref = scratch
    s_refs, c_refs, p_refs = (s0_ref, s1_ref), (c0_ref, c1_ref), (p0_ref, p1_ref)
    i = pl.program_id(1)
    zeros = jnp.zeros((HEAD_DIM, TM), BF16)
    bound_max = jnp.zeros((1, TM), F32)
    for hh in range(N_SCORE_HEADS):
        qh = q_ref[0, 0, hh * HEAD_DIM:(hh + 1) * HEAD_DIM, :]
        qpad_ref[hh] = jnp.concatenate([qh, zeros] if q_slot(hh) == 0 else [zeros, qh], axis=0)
        qf = qh.astype(F32)
        q_norm2 = jnp.sum(qf * qf, axis=0, keepdims=True)
        k_norm2 = kmax_ref[0, k_row(hh):k_row(hh) + 1, 0:1]
        bound = jnp.sqrt(q_norm2 * k_norm2) * SCORE_BOUND_SLACK
        bound_ref[hh] = bound
        bound_max = jnp.maximum(bound_max, bound)
    acc_ref[...] = jnp.zeros(acc_ref.shape, F32)
    n = jnp.where(i == 0, 1, n_kv)
    fixed_reference = jnp.max(bound_max) <= SCORE_BOUND_LIMIT

    def produce(j, hh, slot):
        kc = k_ref[0, pl.ds(pl.multiple_of(j * TM, TM), TM), k_col(hh):k_col(hh) + 128]
        s = jnp.dot(kc, qpad_ref[hh], preferred_element_type=F32)
        p = jnp.exp2(s - bound_ref[hh])
        m_ref[hh] = m_ref[hh] + jnp.sum(p, axis=0, keepdims=True)
        p_refs[slot][hh] = p.astype(BF16)

    n_val = v_rows - ONES_ROWS

    def reduce(j, hh, slot):
        vc = v_ref[0, j, v_row(hh):v_row(hh) + n_val, :]
        acc_ref[hh, :n_val] = acc_ref[hh, :n_val] + jnp.dot(vc, p_refs[slot][hh], preferred_element_type=F32)

    @pl.when(fixed_reference)
    def _():
        m_ref[...] = jnp.zeros(m_ref.shape, F32)
        for hh in range(N_SCORE_HEADS):
            produce(0, hh, 0)

        def fast_step(j, slot):
            for hh in range(N_SCORE_HEADS):
                produce(j + 1, hh, 1 - slot)
                reduce(j, hh, slot)

        def fast_body(jj, carry):
            fast_step(2 * jj, 0)
            fast_step(2 * jj + 1, 1)
            return carry

        lax.fori_loop(0, (n - 1) // 2, fast_body, 0)
        for hh in range(N_SCORE_HEADS):
            reduce(n - 1, hh, 0)
            acc_ref[hh, n_val:, :] = jnp.broadcast_to(m_ref[hh], (ONES_ROWS, TM))

    @pl.when(jnp.logical_not(fixed_reference))
    def _():
        _attn_running_max(k_ref, v_ref, qpad_ref, m_ref, acc_ref, s_refs, c_refs, n=n, k_col=k_col, v_row=v_row,
                          v_rows=v_rows, scores_between=scores_between)


def _attn_running_max(k_ref, v_ref, qpad_ref, m_ref, acc_ref, s_refs, c_refs, *, n, k_col, v_row, v_rows,
                      scores_between):
    m_ref[...] = jnp.full(m_ref.shape, -jnp.inf, F32)

    def scores(j, hh, slot):
        kc = k_ref[0, pl.ds(pl.multiple_of(j * TM, TM), TM), k_col(hh):k_col(hh) + 128]
        s = jnp.dot(kc, qpad_ref[hh], preferred_element_type=F32)
        s_refs[slot][hh] = s
        c_refs[slot][hh] = jnp.max(s, axis=0, keepdims=True)

    def probs(hh, slot):
        m_old = m_ref[hh]
        m_new = jnp.maximum(m_old, c_refs[slot][hh])
        return m_old, m_new, jnp.exp2(s_refs[slot][hh] - m_new).astype(BF16)

    def accumulate(j, hh, state):
        m_old, m_new, p = state
        vc = v_ref[0, j, v_row(hh):v_row(hh) + v_rows, :]
        alpha = jnp.exp2(m_old - m_new)
        acc_ref[hh] = acc_ref[hh] * alpha + jnp.dot(vc, p, preferred_element_type=F32)
        m_ref[hh] = m_new

    for hh in range(N_SCORE_HEADS):
        scores(0, hh, 0)

    def step(j, slot):
        for hh in range(N_SCORE_HEADS):
            state = probs(hh, slot)
            if scores_between:
                scores(j + 1, hh, 1 - slot)
            accumulate(j, hh, state)
            if not scores_between:
                scores(j + 1, hh, 1 - slot)

    def body(jj, carry):
        step(2 * jj, 0)
        step(2 * jj + 1, 1)
        return carry

    lax.fori_loop(0, (n - 1) // 2, body, 0)
    for hh in range(N_SCORE_HEADS):
        accumulate(n - 1, hh, probs(hh, 0))


def _attn_scratch(v_rows):
    stat = pltpu.VMEM((N_SCORE_HEADS, 1, TM), F32)
    return [pltpu.VMEM((N_SCORE_HEADS, 128, TM), BF16),
            stat, stat,
            pltpu.VMEM((N_SCORE_HEADS, v_rows, TM), F32),
            pltpu.VMEM((N_SCORE_HEADS, TM, TM), F32), pltpu.VMEM((N_SCORE_HEADS, TM, TM), F32),
            stat, stat,
            pltpu.VMEM((N_SCORE_HEADS, TM, TM), BF16), pltpu.VMEM((N_SCORE_HEADS, TM, TM), BF16)]


def _gqa_kernel(q_ref, k_ref, v_ref, kmax_ref, o_ref, *scratch, n_kv):
    group = GQA_HEADS // GQA_KV_HEADS
    _attn_core(q_ref, k_ref, v_ref, kmax_ref, scratch, n_kv=n_kv,
               k_col=lambda h: 0, k_row=lambda h: h // group, q_slot=lambda h: h // group,
               v_row=lambda h: (h // group) * GQA_VROWS, v_rows=GQA_VROWS, scores_between=True)
    acc_ref = scratch[3]
    for hh in range(GQA_HEADS):
        a = acc_ref[hh]
        o_ref[0, 0, hh * HEAD_DIM:(hh + 1) * HEAD_DIM, :] = (a[:HEAD_DIM] / a[HEAD_DIM:HEAD_DIM + 1]).astype(o_ref.dtype)


def _gqa_attention(aq, ak, av, kmax):
    b, nt, nq, _ = aq.shape
    t = nt * TM
    assert nt % 2 == 1
    return pl.pallas_call(
        functools.partial(_gqa_kernel, n_kv=nt),
        grid=(b, nt),
        in_specs=[pl.BlockSpec((1, 1, nq, TM), lambda bb, i: (bb, i, 0, 0)),
                  pl.BlockSpec((1, t, 128), lambda bb, i: (bb, 0, 0)),
                  pl.BlockSpec((1, nt, GQA_KV_HEADS * GQA_VROWS, TM), lambda bb, i: (bb, 0, 0, 0)),
                  pl.BlockSpec((1, 8, 128), lambda bb, i: (bb, 0, 0))],
        out_specs=pl.BlockSpec((1, 1, nq, TM), lambda bb, i: (bb, i, 0, 0)),
        out_shape=jax.ShapeDtypeStruct((b, nt, nq, TM), BF16),
        scratch_shapes=_attn_scratch(GQA_VROWS),
        compiler_params=_params("parallel", "arbitrary"),
        name="gqa_attn",
    )(aq, ak, av, kmax)


def _diff_kernel(q_ref, k_ref, v_ref, kmax_ref, lam_ref, gain_ref, o_ref, *scratch, n_kv, lam_init):
    _attn_core(q_ref, k_ref, v_ref, kmax_ref, scratch, n_kv=n_kv,
               k_col=lambda hc: (hc // 2) * 128, k_row=lambda hc: hc, q_slot=lambda hc: hc % 2,
               v_row=lambda hc: (hc // 2) * DIFF_VROWS, v_rows=DIFF_VROWS, scores_between=False)
    acc_ref = scratch[3]
    lp = lam_ref[...]
    lam = (jnp.exp(jnp.sum(lp[0:1] * lp[1:2], axis=1, keepdims=True))
           - jnp.exp(jnp.sum(lp[2:3] * lp[3:4], axis=1, keepdims=True)) + lam_init)
    gain = gain_ref[...]
    nv = 2 * HEAD_DIM
    for hh in range(DIFF_HEADS):
        a0, a1 = acc_ref[2 * hh], acc_ref[2 * hh + 1]
        o = a0[:nv] / a0[nv:nv + 1] - lam * (a1[:nv] / a1[nv:nv + 1])
        y = o * lax.rsqrt(jnp.mean(o * o, axis=0, keepdims=True) + NORM_EPS) * gain
        o_ref[0, 0, hh * nv:(hh + 1) * nv, :] = (y * (1.0 - lam_init)).astype(o_ref.dtype)


def _diff_attention(dq, dk, dv, kmax, lam_p, gain, layer_idx):
    b, nt, nq, _ = dq.shape
    t = nt * TM
    assert nt % 2 == 1
    lam_init = 0.8 - 0.6 * math.exp(-0.3 * layer_idx)
    return pl.pallas_call(
        functools.partial(_diff_kernel, n_kv=nt, lam_init=lam_init),
        grid=(b, nt),
        in_specs=[pl.BlockSpec((1, 1, nq, TM), lambda bb, i: (bb, i, 0, 0)),
                  pl.BlockSpec((1, t, DIFF_QK), lambda bb, i: (bb, 0, 0)),
                  pl.BlockSpec((1, nt, DIFF_HEADS * DIFF_VROWS, TM), lambda bb, i: (bb, 0, 0, 0)),
                  pl.BlockSpec((1, 8, 128), lambda bb, i: (bb, 0, 0)),
                  _const_spec(lam_p.shape),
                  _const_spec((2 * HEAD_DIM, 1))],
        out_specs=pl.BlockSpec((1, 1, DIFF_V, TM), lambda bb, i: (bb, i, 0, 0)),
        out_shape=jax.ShapeDtypeStruct((b, nt, DIFF_V, TM), BF16),
        scratch_shapes=_attn_scratch(DIFF_VROWS),
        compiler_params=_params("parallel", "arbitrary"),
        name="diff_attn",
    )(dq, dk, dv, kmax, lam_p, gain.reshape(2 * HEAD_DIM, 1))


def _split_bf16(x):
    hi = x.astype(BF16)
    return hi, (x - hi.astype(F32)).astype(BF16)


def _gla_kernel(qk_ref, la_ref, vt_ref, o_ref, st_ref, *, backward):
    @pl.when(pl.program_id(1) == 0)
    def _():
        st_ref[...] = jnp.zeros(st_ref.shape, F32)

    n_chunk = TM // GLA_CHUNK
    r = lax.broadcasted_iota(jnp.int32, (TM, TM), 0)
    c = lax.broadcasted_iota(jnp.int32, (TM, TM), 1)
    same = (r // GLA_CHUNK) == (c // GLA_CHUNK)
    causal = (c >= r) if backward else (c <= r)
    tri = jnp.where(same & causal, 1.0, 0.0).astype(BF16)
    blk = jnp.where(same, 1.0, 0.0).astype(BF16)
    la_hi, la_lo = _split_bf16(la_ref[0])
    cum = jnp.dot(tri, la_hi, preferred_element_type=F32) + jnp.dot(tri, la_lo, preferred_element_type=F32)
    tot = jnp.dot(blk, la_hi, preferred_element_type=F32) + jnp.dot(blk, la_lo, preferred_element_type=F32)
    q = qk_ref[0, :, :GLA_QK]
    k = qk_ref[0, :, GLA_QK:]
    q_in = (q * jnp.exp(cum)).astype(BF16)
    k_in = k * jnp.exp(-cum)
    k_end = k * jnp.exp(tot - cum)
    dec = jnp.exp(tot)
    lane = lax.broadcasted_iota(jnp.int32, (TM, 128), 1)
    row = lax.broadcasted_iota(jnp.int32, (TM, 128), 0)
    keep = same & ((r >= c) if backward else (r <= c))
    order = range(n_chunk - 1, -1, -1) if backward else range(n_chunk)
    hs = range(GLA_HEADS)
    pairs = [slice((h // 2) * 128, (h // 2 + 1) * 128) for h in hs]
    head_lanes = [(lane // GLA_DK) == (h % 2) for h in hs]
    q_pair = [q_in[:, pairs[h]] for h in hs]
    k_in_h = [jnp.where(head_lanes[h], k_in[:, pairs[h]], 0.0).astype(BF16) for h in hs]
    k_end_h = [jnp.where(head_lanes[h], k_end[:, pairs[h]], 0.0) for h in hs]
    vts = [vt_ref[0, 0, h * GLA_DV:(h + 1) * GLA_DV, :] for h in hs]
    a_ts = [jnp.where(keep, _nt(k_in_h[h], q_pair[h]), 0.0).astype(BF16) for h in hs]
    o_ts = [jnp.dot(vts[h], a_ts[h], preferred_element_type=F32) for h in hs]
    s_ts = [st_ref[h] for h in hs]
    for ci in order:
        in_chunk = (row // GLA_CHUNK) == ci
        for h in hs:
            q_c = jnp.where(in_chunk, q_pair[h], jnp.zeros_like(q_pair[h]))
            o_ts[h] = o_ts[h] + _nt(s_ts[h].astype(BF16), q_c)
            k_c = jnp.where(in_chunk, k_end_h[h], 0.0).astype(BF16)
            s_ts[h] = (s_ts[h] * dec[ci * GLA_CHUNK:ci * GLA_CHUNK + 1, pairs[h]]
                       + jnp.dot(vts[h], k_c, preferred_element_type=F32))
    for h in hs:
        st_ref[h] = s_ts[h]
        o_ref[0, 0, h * GLA_DV:(h + 1) * GLA_DV, :] = o_ts[h]


def _gla_direction(qk, la, vt, backward):
    b, t, _ = qk.shape
    nt = t // TM
    d = 1 if backward else 0
    tile = (lambda i: jnp.where(i == 0, 0, nt - i)) if backward else (lambda i: i)
    return pl.pallas_call(
        functools.partial(_gla_kernel, backward=backward),
        grid=(b, nt),
        in_specs=[pl.BlockSpec((1, TM, 2 * GLA_QK), lambda bb, i: (bb, tile(i), 0)),
                  pl.BlockSpec((1, TM, GLA_QK), lambda bb, i: (bb, tile(i), d)),
                  pl.BlockSpec((1, 1, GLA_V, TM), lambda bb, i: (bb, tile(i), 0, 0))],
        out_specs=pl.BlockSpec((1, 1, GLA_V, TM), lambda bb, i: (bb, tile(i), 0, 0)),
        out_shape=jax.ShapeDtypeStruct((b, nt, GLA_V, TM), F32),
        scratch_shapes=[pltpu.VMEM((GLA_HEADS, GLA_DV, 128), F32)],
        compiler_params=_params("parallel", "arbitrary"),
        name="gla_bwd" if backward else "gla_fwd",
    )(qk, la, vt)


def _post_phases(xs, mods, g, yas, ybs, woa_ref, wob_ref, w1_ref, w2_ref, ff_chunk):
    n = len(xs)
    ys = [jnp.dot(yas[s], woa_ref[...], preferred_element_type=F32)
          + jnp.dot(ybs[s].astype(F32).T.astype(BF16), wob_ref[...], preferred_element_type=F32) for s in range(n)]
    x1s = [xs[s] + mods[s][2:3] * _rms(ys[s], g[1:2]) for s in range(n)]
    hs = [(_rms(x1s[s], g[2:3]) * (1.0 + mods[s][4:5]) + mods[s][3:4]).astype(BF16) for s in range(n)]
    d_ff = w1_ref.shape[1]
    accs = [jnp.zeros(xs[0].shape, F32) for _ in range(n)]
    for c in range(d_ff // ff_chunk):
        cols = slice(c * ff_chunk, (c + 1) * ff_chunk)
        for s in range(n):
            u = jnp.dot(hs[s], w1_ref[:, cols], preferred_element_type=F32)
            u = jnp.square(jnp.maximum(u, 0.0)).astype(BF16)
            accs[s] = accs[s] + jnp.dot(u, w2_ref[cols, :], preferred_element_type=F32)
    return [x1s[s] + mods[s][5:6] * _rms(accs[s], g[3:4]) for s in range(n)]


def _post_gla_kernel(xs_ref, of_ref, ob_ref, gg_ref, gain_ref, yb_ref, mods_ref, g_ref, woa_ref, wob_ref, w1_ref,
                     w2_ref, o_ref, *, ff_chunk):
    gain = gain_ref[...]
    yas = []
    for s in range(POST_PB):
        parts = []
        for hh in range(GLA_HEADS):
            rows = slice(hh * GLA_DV, (hh + 1) * GLA_DV)
            o = of_ref[s, 0, rows, :] + ob_ref[s, 0, rows, :]
            gt = gg_ref[s, 0, rows, :]
            y = o * lax.rsqrt(jnp.mean(o * o, axis=0, keepdims=True) + NORM_EPS) * gain
            parts.append(y * (gt / (1.0 + jnp.exp(-gt))))
        yas.append(jnp.concatenate(parts, axis=0).T.astype(BF16))
    outs = _post_phases([xs_ref[s] for s in range(POST_PB)], [mods_ref[s] for s in range(POST_PB)], g_ref[...], yas,
                        [yb_ref[s, 0] for s in range(POST_PB)], woa_ref, wob_ref, w1_ref, w2_ref, ff_chunk)
    for s in range(POST_PB):
        o_ref[s] = outs[s]


def _post_kernel(xs_ref, ya_ref, yb_ref, mods_ref, g_ref, woa_ref, wob_ref, w1_ref, w2_ref, o_ref, *, ff_chunk):
    outs = _post_phases([xs_ref[s] for s in range(POST_PB)], [mods_ref[s] for s in range(POST_PB)], g_ref[...],
                        [ya_ref[s].astype(BF16) for s in range(POST_PB)], [yb_ref[s, 0] for s in range(POST_PB)],
                        woa_ref, wob_ref, w1_ref, w2_ref, ff_chunk)
    for s in range(POST_PB):
        o_ref[s] = outs[s]


def _post(xs, ya, yb_t, mods_l, g, w_out, w1, w2, skip_ctx):
    b, t, d = xs.shape
    nt = t // TM
    pb = POST_PB
    half = w_out.shape[0] // 2
    woa = w_out[:half].astype(BF16)
    wob = w_out[half:].astype(BF16)
    w1 = w1.astype(BF16)
    w2 = w2.astype(BF16)
    off = 1 if skip_ctx else 0
    tok = lambda n: pl.BlockSpec((pb, TM, n), lambda bb, i: (bb, i + off, 0))
    tr = lambda n: pl.BlockSpec((pb, 1, n, TM), lambda bb, i: (bb, i + off, 0, 0))
    if isinstance(ya, tuple):
        o_f, o_b, gg, gain = ya
        body, name = _post_gla_kernel, "post_gla"
        a_specs = [tr(half), tr(half), tr(half), _const_spec((GLA_DV, 1))]
        a_args = [o_f, o_b, gg, gain.reshape(GLA_DV, 1)]
    else:
        body, name = _post_kernel, "post"
        a_specs, a_args = [tok(half)], [ya]
    return pl.pallas_call(
        functools.partial(body, ff_chunk=1024),
        grid=(b // pb, nt - off),
        in_specs=[tok(d)] + a_specs + [
                  tr(half),
                  pl.BlockSpec((pb, N_MOD, d), lambda bb, i: (_mod_block(bb, i + off, pb), 0, 0)),
                  _const_spec(g.shape),
                  _const_spec(woa.shape), _const_spec(wob.shape), _const_spec(w1.shape), _const_spec(w2.shape)],
        out_specs=pl.BlockSpec((pb, TM, d), lambda bb, i: (bb, i, 0)),
        out_shape=jax.ShapeDtypeStruct((b, t - off * TM, d), F32),
        compiler_params=_params("parallel", "parallel"),
        name=name,
    )(xs, *a_args, yb_t, mods_l, g, woa, wob, w1, w2)


def _dot3(a, b):
    ah, al = _split_bf16(a)
    bh, bl = _split_bf16(b)
    d = lambda x, y: jnp.dot(x, y, preferred_element_type=F32)
    return d(ah, bh) + (d(al, bh) + d(ah, bl))


def _filter_kernel(w1t_ref, w1c_ref, w1s_ref, b1_ref, w2_ref, b2_ref, w3_ref, delta_ref, o_ref, *, seq, rb):
    n = pl.program_id(0) * rb + lax.broadcasted_iota(jnp.int32, (rb, 1), 0)
    d = jnp.where(n < seq, n, 2 * seq - n).astype(F32)
    t = d / max(seq - 1, 1)
    w = (2 * math.pi / seq) * d
    band = lax.broadcasted_iota(jnp.int32, (1, HY_BANDS), 1).astype(F32)
    bands = 1e-4 + band * ((HY_BANDS - 1 - 1e-4) / (HY_BANDS - 1))
    ang = w * bands
    pre = t * w1t_ref[...] + _dot3(jnp.cos(ang), w1c_ref[...]) - _dot3(jnp.sin(ang), w1s_ref[...]) + b1_ref[...]
    hid = jnp.sin(pre)
    hid = jnp.sin(_dot3(hid, w2_ref[...]) + b2_ref[...])
    decay = jnp.where(n == seq, 0.0, jnp.exp(-t * delta_ref[...]))
    for o in range(HY_ORDER):
        o_ref[o] = (_dot3(hid, w3_ref[0, o]) * decay).astype(o_ref.dtype)


def _hyena_kernels(seq, f_w1, f_b1, f_w2, f_b2, f_w3):
    rb = min(512, seq)
    nblk = 2 * seq // rb
    ffn = f_w2.shape[0]
    w3 = f_w3.reshape(ffn, HY_ORDER, 2, HY_CH).transpose(2, 1, 0, 3)
    deltas = jnp.abs(jnp.linspace(math.log(HY_TARGET) / HY_FAST_PCT, math.log(HY_TARGET) / HY_SLOW_PCT,
                                  HY_CH, dtype=F32)).reshape(1, HY_CH)
    return pl.pallas_call(
        functools.partial(_filter_kernel, seq=seq, rb=rb),
        grid=(nblk,),
        in_specs=[_const_spec((1, ffn)), _const_spec((HY_BANDS, ffn)), _const_spec((HY_BANDS, ffn)),
                  _const_spec((1, ffn)), _const_spec((ffn, ffn)), _const_spec((1, ffn)),
                  pl.BlockSpec((1, HY_ORDER, ffn, HY_CH), lambda j: ((2 * j) // nblk, 0, 0, 0)),
                  _const_spec((1, HY_CH))],
        out_specs=pl.BlockSpec((HY_ORDER, rb, HY_CH), lambda j: (0, j, 0)),
        out_shape=jax.ShapeDtypeStruct((HY_ORDER, 2 * seq, HY_CH), BF16),
        compiler_params=_params("parallel"),
        name="hy_filter",
    )(f_w1[0:1], f_w1[1:1 + HY_BANDS], f_w1[1 + HY_BANDS:], f_b1.reshape(1, ffn), f_w2, f_b2.reshape(1, ffn), w3,
      deltas)


def _dft_tables(seq):
    n = 2 * seq
    r = math.isqrt(n)
    assert r * r == n
    r1 = (r // 2 + 1 + 7) // 8 * 8
    idx = jnp.arange(r, dtype=jnp.int32)
    phi = (2 * math.pi / r) * ((idx[:r1, None] * idx[None, :]) % r).astype(F32)
    f1 = jnp.concatenate([jnp.cos(phi), -jnp.sin(phi)], axis=0).astype(BF16)
    m = (idx[:r1, None, None] + r * idx[None, :, None]) * idx[None, None, :]
    th = (2 * math.pi / n) * (m % n).astype(F32)
    c, s = jnp.cos(th), jnp.sin(th)
    blockm = lambda re, im: jnp.concatenate([jnp.concatenate([re, -im], axis=-1),
                                             jnp.concatenate([im, re], axis=-1)], axis=-2)
    g = blockm(c, -s).astype(BF16)
    ginv = blockm(jnp.swapaxes(c, 1, 2), jnp.swapaxes(s, 1, 2)).astype(BF16)
    k1 = idx[:r1]
    weight = jnp.where((k1 == 0) | (k1 == r // 2), 1.0, jnp.where(k1 < r // 2, 2.0, 0.0)) / n
    phi_t = phi.T[:r // 2]
    finv = jnp.concatenate([jnp.cos(phi_t) * weight, -jnp.sin(phi_t) * weight], axis=1).astype(BF16)
    return r, f1, g, ginv, finv


DFT_BB = 8


def _dft1_kernel(f_ref, x_ref, o_ref, *, a0):
    f = f_ref[...]
    r = o_ref.shape[2]
    x = pltpu.einshape("abc->bac", x_ref[0, a0:].astype(F32))
    ys = [jnp.dot(f, x[j].astype(BF16), preferred_element_type=F32) for j in range(DFT_BB)]
    y = pltpu.einshape("jkc->kjc", jnp.stack(ys, axis=0))
    o_ref[0, 0] = y[:r].astype(o_ref.dtype)
    o_ref[0, 1] = y[r:].astype(o_ref.dtype)


def _dft_stage1(f, x4, a0=0):
    b, rows, r, c = x4.shape
    r1 = f.shape[0] // 2
    return pl.pallas_call(
        functools.partial(_dft1_kernel, a0=a0),
        grid=(b, r // DFT_BB),
        in_specs=[_const_spec(f.shape), pl.BlockSpec((1, rows, DFT_BB, c), lambda bi, j: (bi, 0, j, 0))],
        out_specs=pl.BlockSpec((1, 2, r1, DFT_BB, c), lambda bi, j: (bi, 0, 0, j, 0)),
        out_shape=jax.ShapeDtypeStruct((b, 2, r1, r, c), BF16),
        compiler_params=_params("parallel", "parallel"),
        name="hy_dft1",
    )(f, x4)


def _fspec_kernel(a_ref, g_ref, o_ref, *, kb, r):
    for kk in range(kb):
        a = a_ref[0, :, kk].reshape(2 * r, a_ref.shape[-1])
        o_ref[0, kk] = jnp.dot(g_ref[kk], a, preferred_element_type=F32)


def _filter_spectrum(a, g, kb):
    o, _, r1, r, c = a.shape
    return pl.pallas_call(
        functools.partial(_fspec_kernel, kb=kb, r=r),
        grid=(o, r1 // kb),
        in_specs=[pl.BlockSpec((1, 2, kb, r, c), lambda oo, j: (oo, 0, j, 0, 0)),
                  pl.BlockSpec((kb, 2 * r, 2 * r), lambda oo, j: (j, 0, 0))],
        out_specs=pl.BlockSpec((1, kb, 2 * r, c), lambda oo, j: (oo, j, 0, 0)),
        out_shape=jax.ShapeDtypeStruct((o, r1, 2 * r, c), F32),
        compiler_params=_params("parallel", "parallel"),
        name="hy_fspec",
    )(a, g)


def _dft2_kernel(a_ref, g_ref, gi_ref, k_ref, o_ref, *, kb, r):
    xs = [jnp.dot(g_ref[kk], a_ref[0, :, kk].reshape(2 * r, a_ref.shape[-1]), preferred_element_type=F32)
          for kk in range(kb)]
    ys = []
    for kk in range(kb):
        x, ks = xs[kk], k_ref[0, kk]
        xr, xi, kr, ki = x[:r], x[r:], ks[:r], ks[r:]
        ys.append(jnp.concatenate([xr * kr - xi * ki, xr * ki + xi * kr], axis=0).astype(BF16))
    bqs = [jnp.dot(gi_ref[kk], ys[kk], preferred_element_type=F32) for kk in range(kb)]
    for kk in range(kb):
        o_ref[0, 0, kk] = bqs[kk][:r].astype(o_ref.dtype)
        o_ref[0, 1, kk] = bqs[kk][r:].astype(o_ref.dtype)


def _dft_stage2(a, g, ginv, kspec, order, kb):
    b, _, r1, r, c = a.shape
    blk = pl.BlockSpec((1, 2, kb, r, c), lambda j, bb: (bb, 0, j, 0, 0))
    tab = pl.BlockSpec((kb, 2 * r, 2 * r), lambda j, bb: (j, 0, 0))
    return pl.pallas_call(
        functools.partial(_dft2_kernel, kb=kb, r=r),
        grid=(r1 // kb, b),
        in_specs=[blk, tab, tab, pl.BlockSpec((1, kb, 2 * r, c), lambda j, bb: (order, j, 0, 0))],
        out_specs=blk,
        out_shape=jax.ShapeDtypeStruct(a.shape, BF16),
        compiler_params=_params("parallel", "arbitrary"),
        name="hy_dft2",
    )(a, g, ginv, kspec)


def _dft3_kernel(f_ref, b_ref, gate_ref, z_ref, bias_ref, head_ref, o_ref, *, a0):
    f = f_ref[...]
    bq = jnp.concatenate([b_ref[0, 0], b_ref[0, 1]], axis=0)
    bq = pltpu.einshape("kjc->jkc", bq.astype(F32))
    ys = [jnp.dot(f, bq[j].astype(BF16), preferred_element_type=F32) for j in range(DFT_BB)]
    y = pltpu.einshape("jac->ajc", jnp.stack(ys, axis=0))
    o_ref[0, :a0] = head_ref[0]
    o_ref[0, a0:] = gate_ref[0, a0:] * (y + z_ref[0, a0:] * bias_ref[...])


def _dft_stage3(finv, bq5, gate4, z4, bias, head4):
    b, _, r1, r, c = bq5.shape
    rows = z4.shape[1]
    a0 = head4.shape[1]
    io = pl.BlockSpec((1, rows, DFT_BB, c), lambda bi, j: (bi, 0, j, 0))
    return pl.pallas_call(
        functools.partial(_dft3_kernel, a0=a0),
        grid=(b, r // DFT_BB),
        in_specs=[_const_spec(finv.shape), pl.BlockSpec((1, 2, r1, DFT_BB, c), lambda bi, j: (bi, 0, 0, j, 0)), io, io,
                  _const_spec((1, 1, c)), pl.BlockSpec((1, a0, DFT_BB, c), lambda bi, j: (bi, 0, j, 0))],
        out_specs=io,
        out_shape=jax.ShapeDtypeStruct((b, rows, r, c), F32),
        compiler_params=_params("parallel", "parallel"),
        name="hy_dft3",
    )(finv, bq5, gate4, z4, bias, head4)


def _long_conv_gated(v, gates, kern, hy_bias, ctx_out, lc):
    b, t, c = v.shape
    r, f1, g, ginv, finv = _dft_tables(t - lc)
    assert lc % r == 0
    a0 = lc // r
    kb = 8
    kspec = _filter_spectrum(_dft_stage1(f1, kern.reshape(HY_ORDER, r, r, c)), g, kb)
    as_blocks = lambda x: x.reshape(b, t // r, r, c)
    z = as_blocks(v)
    heads = [jnp.zeros((b, a0, r, c), F32), ctx_out.reshape(b, a0, r, c)]
    for o in range(HY_ORDER):
        a = _dft_stage1(f1[:, :r // 2], z, a0)
        bq = _dft_stage2(a, g, ginv, kspec, o, kb)
        z = _dft_stage3(finv, bq, as_blocks(gates[o]), z, hy_bias[o].reshape(1, 1, c), heads[o])
    return z.reshape(b, t, c)


def _ctx_conv_kernel(v_ref, x1_ref, x2_ref, k_ref, ff_ref, fi_ref, bias_ref, o_ref):
    n = ff_ref.shape[1]
    z = v_ref[0]
    for o, gate_ref in enumerate((x1_ref, x2_ref)):
        ks = jnp.dot(ff_ref[...], k_ref[o], preferred_element_type=F32)
        x = jnp.dot(ff_ref[:, :TM], z.astype(BF16), preferred_element_type=F32)
        xr, xi, kr, ki = x[:n], x[n:], ks[:n], ks[n:]
        y = jnp.concatenate([xr * kr - xi * ki, xr * ki + xi * kr], axis=0).astype(BF16)
        conv = jnp.dot(fi_ref[...], y, preferred_element_type=F32)
        z = gate_ref[0] * (conv + z * bias_ref[o:o + 1])
    o_ref[0] = z


def _ctx_conv_gated(v, x1, x2, kern, hy_bias):
    b, _, c = v.shape
    seq = TM
    n = 2 * seq
    idx = jnp.arange(n, dtype=jnp.int32)
    th = (2 * math.pi / n) * ((idx[:, None] * idx[None, :]) % n).astype(F32)
    ff = jnp.concatenate([jnp.cos(th), -jnp.sin(th)], axis=0).astype(BF16)
    fi = (jnp.concatenate([jnp.cos(th), -jnp.sin(th)], axis=1)[:seq] / n).astype(BF16)
    io = pl.BlockSpec((1, seq, c), lambda bb: (bb, 0, 0))
    return pl.pallas_call(
        _ctx_conv_kernel,
        grid=(b,),
        in_specs=[io, io, io, _const_spec(kern.shape), _const_spec(ff.shape), _const_spec(fi.shape),
                  _const_spec(hy_bias.shape)],
        out_specs=io,
        out_shape=jax.ShapeDtypeStruct((b, seq, c), F32),
        compiler_params=_params("parallel"),
        name="hy_ctx",
    )(v, x1, x2, kern, ff, fi, hy_bias)


def _hyena(v, x1, x2, lc, f_w1, f_b1, f_w2, f_b2, f_w3, hy_bias):
    seq = v.shape[1] - lc
    yc = _ctx_conv_gated(v, x1, x2, _hyena_kernels(lc, f_w1, f_b1, f_w2, f_b2, f_w3), hy_bias)
    return _long_conv_gated(v, (x1, x2), _hyena_kernels(seq, f_w1, f_b1, f_w2, f_b2, f_w3), hy_bias, yc, lc)


def _rope_tables_t(seq, lc):
    rows = seq // GRID_W
    r = jnp.broadcast_to(jnp.arange(rows, dtype=F32)[:, None], (rows, GRID_W)).reshape(-1)
    col = jnp.broadcast_to(jnp.arange(GRID_W, dtype=F32)[None, :], (rows, GRID_W)).reshape(-1)
    n_pairs = HEAD_DIM // 4
    inv = ROPE_THETA ** (-jnp.arange(n_pairs, dtype=F32) / n_pairs)
    ang = jnp.concatenate([r[:, None] * inv, col[:, None] * inv], axis=-1)
    cos = jnp.concatenate([jnp.ones((lc, 2 * n_pairs), F32), jnp.cos(ang)], axis=0)
    sin = jnp.concatenate([jnp.zeros((lc, 2 * n_pairs), F32), jnp.sin(ang)], axis=0)
    return cos.T, sin.T


def kernel(x, c, ctx, c_ctx, w_ada, b_ada, norm_g, w_mlp_in, w_mlp_out, ev_w_in, ev_w_lr, ev_b_lr, ev_gla_g,
           ev_qk_g, ev_w_out, od_w_in, od_conv_w, od_conv_b, od_f_w1, od_f_b1, od_f_w2, od_f_b2, od_f_w3,
           od_hy_bias, od_lam, od_diff_g, od_w_out):
    bsz, seq, d = x.shape
    lc = ctx.shape[1]
    depth = w_ada.shape[0]
    assert lc == TM and seq % TM == 0 and bsz <= MOD_CTX_ROW and bsz % POST_PB == 0 and bsz % IN_PB == 0
    cos_t, sin_t = _rope_tables_t(seq, lc)
    cc = jnp.zeros((8, d), F32).at[:bsz].set(c).at[MOD_CTX_ROW:MOD_CTX_ROW + max(POST_PB, IN_PB)].set(c_ctx)
    mods = _mods(cc, w_ada, b_ada).reshape(depth, 8, N_MOD, d)
    xs = jnp.concatenate([ctx, x], axis=1)
    for i in range(depth):
        last = i == depth - 1
        j = i // 2
        g = norm_g[i]
        if i % 2 == 0:
            qk, la, gv, gg, aq, ak, av, kn = _even_in(xs, mods[i], g[0:1], ev_w_in[j], ev_w_lr[j], ev_b_lr[j],
                                                  ev_qk_g[j], cos_t, sin_t)
            ya = (_gla_direction(qk, la, gv, False), _gla_direction(qk, la, gv, True), gg, ev_gla_g[j])
            yb = _gqa_attention(aq, ak, av, jnp.max(kn, axis=1))
            w_out = ev_w_out[j]
        else:
            hv, hx1, hx2, dq, dk, dv, kn = _odd_in(xs, mods[i], g[0:1], od_w_in[j], od_conv_w[j], od_conv_b[j],
                                                   cos_t, sin_t)
            ya = _hyena(hv, hx1, hx2, lc, od_f_w1[j], od_f_b1[j], od_f_w2[j], od_f_b2[j], od_f_w3[j], od_hy_bias[j])
            yb = _diff_attention(dq, dk, dv, jnp.max(kn, axis=1), od_lam[j], od_diff_g[j], i)
            w_out = od_w_out[j]
        xs = _post(xs, ya, yb, mods[i], g, w_out, w_mlp_in[i], w_mlp_out[i], skip_ctx=last)
    return xs
```

```python
import functools
import math

import jax
import jax.numpy as jnp
import numpy as np
from jax import lax
from jax.experimental import pallas as pl
from jax.experimental.pallas import tpu as pltpu

F32 = jnp.float32
BF16 = jnp.bfloat16

TM = 256
GRID_W = 64
HEAD_DIM = 64
ROPE_THETA = 10000.0
NORM_EPS = 1e-6
N_MOD = 6
VMEM_LIMIT = 56 * 1024 * 1024

GLA_HEADS = 4
GLA_DK = 64
GLA_DV = 128
GLA_RANK = 16
GLA_NORMALIZER = 16.0
GLA_CHUNK = 64
GLA_QK = GLA_HEADS * GLA_DK
GLA_V = GLA_HEADS * GLA_DV
GQA_HEADS = 8
GQA_KV_HEADS = 2
HY_CH = 512
HY_ORDER = 2
HY_BANDS = 16
HY_TARGET = 1e-2
HY_FAST_PCT = 0.3
HY_SLOW_PCT = 1.5
HY_COLS = (HY_ORDER + 1) * HY_CH
DIFF_HEADS = 4
DIFF_QK = DIFF_HEADS * 2 * HEAD_DIM
DIFF_V = DIFF_HEADS * 2 * HEAD_DIM
ONES_ROWS = 16
N_SCORE_HEADS = 8
LOG2E = 1.4426950408889634
Q_SCALE = HEAD_DIM ** -0.5 * LOG2E
GQA_VROWS = HEAD_DIM + ONES_ROWS
SCORE_BOUND_SLACK = 1.01
FAST_UNROLL = 8
SCORE_BOUND_LIMIT = 60.0
DIFF_VROWS = 2 * HEAD_DIM + ONES_ROWS


def _params(*sem):
    return pltpu.CompilerParams(dimension_semantics=sem, vmem_limit_bytes=VMEM_LIMIT)


def _const_spec(shape):
    nd = len(shape)
    return pl.BlockSpec(shape, lambda *_: (0,) * nd, pipeline_mode=pl.Buffered(1))


def _rms(x, g):
    return x * lax.rsqrt(jnp.mean(x * x, axis=-1, keepdims=True) + NORM_EPS) * g


MOD_CTX_ROW = 4
POST_PB = 2
IN_PB = 2


def _mod_block(bb, i, pb):
    return jnp.where(i == 0, MOD_CTX_ROW // pb, bb)


def _mods_kernel(cc_ref, w_ref, b_ref, o_ref):
    cc = cc_ref[...]
    s = cc / (1.0 + jnp.exp(-cc))
    o_ref[0] = jnp.dot(s.astype(BF16), w_ref[0].astype(BF16), preferred_element_type=F32) + b_ref[0]


def _mods(cc, w_ada, b_ada):
    depth, d, n = w_ada.shape
    nb = n // 4
    return pl.pallas_call(
        _mods_kernel,
        grid=(depth, n // nb),
        in_specs=[pl.BlockSpec((8, d), lambda l, j: (0, 0)),
                  pl.BlockSpec((1, d, nb), lambda l, j: (l, 0, j)),
                  pl.BlockSpec((1, 1, nb), lambda l, j: (l, 0, j))],
        out_specs=pl.BlockSpec((1, 8, nb), lambda l, j: (l, 0, j)),
        out_shape=jax.ShapeDtypeStruct((depth, 8, n), F32),
        compiler_params=_params("parallel", "parallel"),
        name="ada_mods",
    )(cc, w_ada, b_ada.reshape(depth, 1, n))


def _normed_input(x, mods, g):
    return (_rms(x, g) * (1.0 + mods[1:2]) + mods[0:1]).astype(BF16)


def _rope_rows(y, cos, sin):
    x1, x2 = y[:32], y[32:]
    return x1 * cos - x2 * sin, x1 * sin + x2 * cos


def _log_sigmoid(x):
    return jnp.minimum(x, 0.0) - jnp.log(1.0 + jnp.exp(-jnp.abs(x)))


def _nt(a, b):
    return lax.dot_general(a, b, (((1,), (1,)), ((), ())), preferred_element_type=F32)


def _max_sq_norm_row(parts):
    n2 = sum(jnp.sum(jnp.square(p.astype(BF16).astype(F32)), axis=0, keepdims=True) for p in parts)
    return jnp.broadcast_to(jnp.max(n2, axis=1, keepdims=True), (1, 128))


def _even_in_kernel(xs_ref, mods_ref, g_ref, wn_ref, wt_ref, wlr_ref, blr_ref, gq_ref, gk_ref,
                    cos_ref, sin_ref, qk_ref, la_ref, gv_ref, gg_ref, aq_ref, ak_ref, av_ref, kn_ref):
    hs = [_normed_input(xs_ref[s], mods_ref[s], g_ref[0:1]) for s in range(IN_PB)]
    ps = [jnp.dot(h, wn_ref[...], preferred_element_type=F32) for h in hs]
    pts = [_nt(wt_ref[...], h) for h in hs]
    cos, sin = cos_ref[...], sin_ref[...]
    gq, gk_gain = gq_ref[...], gk_ref[...]
    ones = jnp.ones((ONES_ROWS, TM), BF16)
    for s in range(IN_PB):
        p, pt_all = ps[s], pts[s]
        qk_ref[s, :, :GLA_QK] = p[:, :GLA_QK] * GLA_DK ** -0.5
        qk_ref[s, :, GLA_QK:] = p[:, GLA_QK:2 * GLA_QK]
        lr = p[:, 2 * GLA_QK:].astype(BF16)
        gk = jnp.dot(lr, wlr_ref[...], preferred_element_type=F32) + blr_ref[...]
        la_ref[s] = _log_sigmoid(gk) * (1.0 / GLA_NORMALIZER)
        gv_ref[s, 0] = pt_all[:GLA_V].astype(BF16)
        gg_ref[s, 0] = pt_all[GLA_V:2 * GLA_V]
        pt = pt_all[2 * GLA_V:]
        for hh in range(GQA_HEADS):
            blk = pt[hh * 64:(hh + 1) * 64]
            y = blk * lax.rsqrt(jnp.mean(blk * blk, axis=0, keepdims=True) + NORM_EPS) * gq
            o1, o2 = _rope_rows(y, cos, sin)
            aq_ref[s, 0, hh * 64:hh * 64 + 32, :] = (o1 * Q_SCALE).astype(BF16)
            aq_ref[s, 0, hh * 64 + 32:hh * 64 + 64, :] = (o2 * Q_SCALE).astype(BF16)
        kparts, knorms = [], []
        for hh in range(GQA_KV_HEADS):
            blk = pt[512 + hh * 64:512 + (hh + 1) * 64]
            y = blk * lax.rsqrt(jnp.mean(blk * blk, axis=0, keepdims=True) + NORM_EPS) * gk_gain
            o1, o2 = _rope_rows(y, cos, sin)
            kparts += [o1, o2]
            knorms.append(_max_sq_norm_row([o1, o2]))
        ak_ref[s] = jnp.concatenate(kparts, axis=0).T.astype(BF16)
        kn_ref[s, 0] = jnp.concatenate(knorms + [jnp.zeros((8 - GQA_KV_HEADS, 128), F32)], axis=0)
        for hh in range(GQA_KV_HEADS):
            av_ref[s, 0, hh * GQA_VROWS:hh * GQA_VROWS + 64, :] = pt[640 + hh * 64:640 + (hh + 1) * 64].astype(BF16)
            av_ref[s, 0, hh * GQA_VROWS + 64:(hh + 1) * GQA_VROWS, :] = ones


def _odd_in_kernel(xs_ref, prev_ref, next_ref, mods_ref, g_ref, wn_ref, wt_ref, cw_ref, cb_ref, cos_ref, sin_ref,
                   v_ref, x1_ref, x2_ref, dq_ref, dk_ref, dv_ref, kn_ref, *, n_tiles):
    i = pl.program_id(1)
    hs = [_normed_input(xs_ref[s], mods_ref[s], g_ref[0:1]) for s in range(IN_PB)]
    h_ext = [jnp.concatenate([hs[s], _normed_input(prev_ref[s], mods_ref[s], g_ref[0:1]),
                              _normed_input(next_ref[s], mods_ref[s], g_ref[0:1])], axis=0) for s in range(IN_PB)]
    us = [jnp.dot(h, wn_ref[...], preferred_element_type=F32) for h in h_ext]
    pts = [_nt(wt_ref[...], h) for h in hs]
    cos, sin = cos_ref[...], sin_ref[...]
    ones = jnp.ones((ONES_ROWS, TM), BF16)
    nv = 2 * HEAD_DIM
    cw = cw_ref[...]
    row_id = lax.broadcasted_iota(jnp.int32, (TM, HY_COLS), 0)
    no_prev = (i == 0) | (i == 1)
    no_next = (i == 0) | (i == n_tiles - 1)
    for s in range(IN_PB):
        pt = pts[s]
        u = us[s][:TM]
        prev_row = jnp.where(no_prev, 0.0, us[s][TM + 7:TM + 8])
        next_row = jnp.where(no_next, 0.0, us[s][TM + 8:TM + 9])
        up = jnp.where(row_id == 0, prev_row, pltpu.roll(u, 1, axis=0))
        dn = jnp.where(row_id == TM - 1, next_row, pltpu.roll(u, TM - 1, axis=0))
        conv = up * cw[0:1] + u * cw[1:2] + dn * cw[2:3] + cb_ref[...]
        v_ref[s] = conv[:, :HY_CH]
        x1_ref[s] = conv[:, HY_CH:2 * HY_CH]
        x2_ref[s] = conv[:, 2 * HY_CH:]
        kparts, knorms = [], []
        for hh in range(2 * DIFF_HEADS):
            o1, o2 = _rope_rows(pt[hh * 64:(hh + 1) * 64], cos, sin)
            dq_ref[s, 0, hh * 64:hh * 64 + 32, :] = (o1 * Q_SCALE).astype(BF16)
            dq_ref[s, 0, hh * 64 + 32:hh * 64 + 64, :] = (o2 * Q_SCALE).astype(BF16)
            k1, k2 = _rope_rows(pt[DIFF_QK + hh * 64:DIFF_QK + (hh + 1) * 64], cos, sin)
            kparts += [k1, k2]
            knorms.append(_max_sq_norm_row([k1, k2]))
        dk_ref[s] = jnp.concatenate(kparts, axis=0).T.astype(BF16)
        kn_ref[s, 0] = jnp.concatenate(knorms, axis=0)
        for hh in range(DIFF_HEADS):
            rows = pt[2 * DIFF_QK + hh * nv:2 * DIFF_QK + (hh + 1) * nv]
            dv_ref[s, 0, hh * DIFF_VROWS:hh * DIFF_VROWS + nv, :] = rows.astype(BF16)
            dv_ref[s, 0, hh * DIFF_VROWS + nv:(hh + 1) * DIFF_VROWS, :] = ones


def _pair_perm(n_heads):
    base = np.concatenate([np.arange(0, HEAD_DIM, 2), np.arange(1, HEAD_DIM, 2)])
    return np.concatenate([h * HEAD_DIM + base for h in range(n_heads)])


def _even_in(xs, mods_l, g0, w_in, w_lr, b_lr, qk_g, cos_t, sin_t):
    b, t, d = xs.shape
    nt = t // TM
    c_v = 2 * GLA_QK
    c_g = c_v + GLA_V
    c_lr0 = c_g + GLA_V
    c_lr = c_lr0 + 2 * GLA_RANK
    wn = jnp.concatenate([w_in[:, :c_v], w_in[:, c_lr0:c_lr], jnp.zeros((d, 128 - 2 * GLA_RANK), F32)],
                         axis=1).astype(BF16)
    cq = c_lr + GQA_HEADS * HEAD_DIM
    ck = cq + GQA_KV_HEADS * HEAD_DIM
    wq = w_in[:, c_lr:cq][:, _pair_perm(GQA_HEADS)]
    wk = w_in[:, cq:ck][:, _pair_perm(GQA_KV_HEADS)]
    wt = jnp.concatenate([w_in[:, c_v:c_lr0], wq, wk, w_in[:, ck:]], axis=1).T.astype(BF16)
    wlr = jnp.zeros((128, 2 * GLA_QK), F32)
    wlr = wlr.at[:GLA_RANK, :GLA_QK].set(w_lr[0]).at[GLA_RANK:2 * GLA_RANK, GLA_QK:].set(w_lr[1]).astype(BF16)
    blr = b_lr.reshape(1, 2 * GLA_QK)
    perm = _pair_perm(1)
    gq = qk_g[0][perm].reshape(HEAD_DIM, 1)
    gk = qk_g[1][perm].reshape(HEAD_DIM, 1)
    tok = lambda n: pl.BlockSpec((IN_PB, TM, n), lambda bb, i: (bb, i, 0))
    tr = lambda n: pl.BlockSpec((IN_PB, 1, n, TM), lambda bb, i: (bb, i, 0, 0))
    return pl.pallas_call(
        _even_in_kernel,
        grid=(b // IN_PB, nt),
        in_specs=[tok(d),
                  pl.BlockSpec((IN_PB, N_MOD, d), lambda bb, i: (_mod_block(bb, i, IN_PB), 0, 0)),
                  _const_spec((1, d)),
                  _const_spec(wn.shape), _const_spec(wt.shape), _const_spec(wlr.shape), _const_spec(blr.shape),
                  _const_spec(gq.shape), _const_spec(gk.shape),
                  pl.BlockSpec((32, TM), lambda bb, i: (0, i)),
                  pl.BlockSpec((32, TM), lambda bb, i: (0, i))],
        out_specs=[tok(2 * GLA_QK), tok(2 * GLA_QK), tr(GLA_V), tr(GLA_V), tr(512), tok(128),
                   tr(GQA_KV_HEADS * GQA_VROWS), pl.BlockSpec((IN_PB, 1, 8, 128), lambda bb, i: (bb, i, 0, 0))],
        out_shape=[jax.ShapeDtypeStruct((b, t, 2 * GLA_QK), F32),
                   jax.ShapeDtypeStruct((b, t, 2 * GLA_QK), F32),
                   jax.ShapeDtypeStruct((b, nt, GLA_V, TM), BF16),
                   jax.ShapeDtypeStruct((b, nt, GLA_V, TM), F32),
                   jax.ShapeDtypeStruct((b, nt, 512, TM), BF16),
                   jax.ShapeDtypeStruct((b, t, 128), BF16),
                   jax.ShapeDtypeStruct((b, nt, GQA_KV_HEADS * GQA_VROWS, TM), BF16),
                   jax.ShapeDtypeStruct((b, nt, 8, 128), F32)],
        compiler_params=_params("parallel", "parallel"),
        name="even_in",
    )(xs, mods_l, g0, wn, wt, wlr, blr, gq, gk, cos_t, sin_t)


def _odd_in(xs, mods_l, g0, w_in, conv_w, conv_b, cos_t, sin_t):
    b, t, d = xs.shape
    nt = t // TM
    r8 = TM // 8
    last_blk = t // 8 - 1
    wn = w_in[:, :HY_COLS].astype(BF16)
    perm = _pair_perm(2 * DIFF_HEADS)
    wq = w_in[:, HY_COLS:HY_COLS + DIFF_QK][:, perm]
    wk = w_in[:, HY_COLS + DIFF_QK:HY_COLS + 2 * DIFF_QK][:, perm]
    wt = jnp.concatenate([wq, wk, w_in[:, HY_COLS + 2 * DIFF_QK:]], axis=1).T.astype(BF16)
    tok = lambda n: pl.BlockSpec((IN_PB, TM, n), lambda bb, i: (bb, i, 0))
    tr = lambda n: pl.BlockSpec((IN_PB, 1, n, TM), lambda bb, i: (bb, i, 0, 0))
    hy = jax.ShapeDtypeStruct((b, t, HY_CH), F32)
    return pl.pallas_call(
        functools.partial(_odd_in_kernel, n_tiles=nt),
        grid=(b // IN_PB, nt),
        in_specs=[tok(d),
                  pl.BlockSpec((IN_PB, 8, d), lambda bb, i: (bb, jnp.maximum(i * r8 - 1, 0), 0)),
                  pl.BlockSpec((IN_PB, 8, d), lambda bb, i: (bb, jnp.minimum((i + 1) * r8, last_blk), 0)),
                  pl.BlockSpec((IN_PB, N_MOD, d), lambda bb, i: (_mod_block(bb, i, IN_PB), 0, 0)),
                  _const_spec((1, d)),
                  _const_spec(wn.shape), _const_spec(wt.shape),
                  _const_spec(conv_w.shape), _const_spec((1, HY_COLS)),
                  pl.BlockSpec((32, TM), lambda bb, i: (0, i)),
                  pl.BlockSpec((32, TM), lambda bb, i: (0, i))],
        out_specs=[tok(HY_CH), tok(HY_CH), tok(HY_CH), tr(DIFF_QK), tok(DIFF_QK), tr(DIFF_HEADS * DIFF_VROWS),
                   pl.BlockSpec((IN_PB, 1, 8, 128), lambda bb, i: (bb, i, 0, 0))],
        out_shape=[hy, hy, hy,
                   jax.ShapeDtypeStruct((b, nt, DIFF_QK, TM), BF16),
                   jax.ShapeDtypeStruct((b, t, DIFF_QK), BF16),
                   jax.ShapeDtypeStruct((b, nt, DIFF_HEADS * DIFF_VROWS, TM), BF16),
                   jax.ShapeDtypeStruct((b, nt, 8, 128), F32)],
        compiler_params=_params("parallel", "parallel"),
        name="odd_in",
    )(xs, xs, xs, mods_l, g0, wn, wt, conv_w, conv_b.reshape(1, HY_COLS), cos_t, sin_t)


def _attn_core(q_ref, k_ref, v_ref, kmax_ref, scratch, *, n_kv, k_col, k_row, q_slot, v_row, v_rows, scores_between,
               vpu_denominator):
    qpad_ref, bound_ref, m_ref, acc_ref, s0_ref, s1_ref, c0_ref, c1_ref, p0_ref, p1_ref = scratch
    s_refs, c_refs, p_refs = (s0_ref, s1_ref), (c0_ref, c1_ref), (p0_ref, p1_ref)
    i = pl.program_id(1)
    zeros = jnp.zeros((HEAD_DIM, TM), BF16)
    bound_max = jnp.zeros((1, TM), F32)
    for hh in range(N_SCORE_HEADS):
        qh = q_ref[0, 0, hh * HEAD_DIM:(hh + 1) * HEAD_DIM, :]
        qpad_ref[hh] = jnp.concatenate([qh, zeros] if q_slot(hh) == 0 else [zeros, qh], axis=0)
        qf = qh.astype(F32)
        q_norm2 = jnp.sum(qf * qf, axis=0, keepdims=True)
        k_norm2 = kmax_ref[0, k_row(hh):k_row(hh) + 1, 0:1]
        bound = jnp.sqrt(q_norm2 * k_norm2) * SCORE_BOUND_SLACK
        bound_ref[hh] = bound
        bound_max = jnp.maximum(bound_max, bound)
    acc_ref[...] = jnp.zeros(acc_ref.shape, F32)
    n = jnp.where(i == 0, 1, n_kv)
    fixed_reference = jnp.max(bound_max) <= SCORE_BOUND_LIMIT

    def produce(j, hh, slot):
        kc = k_ref[0, pl.ds(pl.multiple_of(j * TM, TM), TM), k_col(hh):k_col(hh) + 128]
        s = jnp.dot(kc, qpad_ref[hh], preferred_element_type=F32)
        p = jnp.exp2(s - bound_ref[hh])
        if vpu_denominator:
            m_ref[hh] = m_ref[hh] + jnp.sum(p, axis=0, keepdims=True)
        p_refs[slot][hh] = p.astype(BF16)

    n_val = v_rows - ONES_ROWS if vpu_denominator else v_rows

    def reduce(j, hh, slot):
        vc = v_ref[0, j, v_row(hh):v_row(hh) + n_val, :]
        acc_ref[hh, :n_val] = acc_ref[hh, :n_val] + jnp.dot(vc, p_refs[slot][hh], preferred_element_type=F32)

    @pl.when(fixed_reference)
    def _():
        m_ref[...] = jnp.zeros(m_ref.shape, F32)
        for hh in range(N_SCORE_HEADS):
            produce(0, hh, 0)

        def fast_step(j, slot):
            for hh in range(N_SCORE_HEADS):
                produce(j + 1, hh, 1 - slot)
                reduce(j, hh, slot)

        def fast_body(jj, carry):
            for u in range(FAST_UNROLL):
                fast_step(FAST_UNROLL * jj + u, u % 2)
            return carry

        lax.fori_loop(0, (n - 1) // FAST_UNROLL, fast_body, 0)
        for hh in range(N_SCORE_HEADS):
            reduce(n - 1, hh, 0)
            if vpu_denominator:
                acc_ref[hh, n_val:, :] = jnp.broadcast_to(m_ref[hh], (ONES_ROWS, TM))

    @pl.when(jnp.logical_not(fixed_reference))
    def _():
        _attn_running_max(k_ref, v_ref, qpad_ref, m_ref, acc_ref, s_refs, c_refs, n=n, k_col=k_col, v_row=v_row,
                          v_rows=v_rows, scores_between=scores_between)


def _attn_running_max(k_ref, v_ref, qpad_ref, m_ref, acc_ref, s_refs, c_refs, *, n, k_col, v_row, v_rows,
                      scores_between):
    m_ref[...] = jnp.full(m_ref.shape, -jnp.inf, F32)

    def scores(j, hh, slot):
        kc = k_ref[0, pl.ds(pl.multiple_of(j * TM, TM), TM), k_col(hh):k_col(hh) + 128]
        s = jnp.dot(kc, qpad_ref[hh], preferred_element_type=F32)
        s_refs[slot][hh] = s
        c_refs[slot][hh] = jnp.max(s, axis=0, keepdims=True)

    def probs(hh, slot):
        m_old = m_ref[hh]
        m_new = jnp.maximum(m_old, c_refs[slot][hh])
        return m_old, m_new, jnp.exp2(s_refs[slot][hh] - m_new).astype(BF16)

    def accumulate(j, hh, state):
        m_old, m_new, p = state
        vc = v_ref[0, j, v_row(hh):v_row(hh) + v_rows, :]
        alpha = jnp.exp2(m_old - m_new)
        acc_ref[hh] = acc_ref[hh] * alpha + jnp.dot(vc, p, preferred_element_type=F32)
        m_ref[hh] = m_new

    for hh in range(N_SCORE_HEADS):
        scores(0, hh, 0)

    def step(j, slot):
        for hh in range(N_SCORE_HEADS):
            state = probs(hh, slot)
            if scores_between:
                scores(j + 1, hh, 1 - slot)
            accumulate(j, hh, state)
            if not scores_between:
                scores(j + 1, hh, 1 - slot)

    def body(jj, carry):
        step(2 * jj, 0)
        step(2 * jj + 1, 1)
        return carry

    lax.fori_loop(0, (n - 1) // 2, body, 0)
    for hh in range(N_SCORE_HEADS):
        accumulate(n - 1, hh, probs(hh, 0))


def _attn_scratch(v_rows):
    stat = pltpu.VMEM((N_SCORE_HEADS, 1, TM), F32)
    return [pltpu.VMEM((N_SCORE_HEADS, 128, TM), BF16),
            stat, stat,
            pltpu.VMEM((N_SCORE_HEADS, v_rows, TM), F32),
            pltpu.VMEM((N_SCORE_HEADS, TM, TM), F32), pltpu.VMEM((N_SCORE_HEADS, TM, TM), F32),
            stat, stat,
            pltpu.VMEM((N_SCORE_HEADS, TM, TM), BF16), pltpu.VMEM((N_SCORE_HEADS, TM, TM), BF16)]


def _gqa_kernel(q_ref, k_ref, v_ref, kmax_ref, o_ref, *scratch, n_kv):
    group = GQA_HEADS // GQA_KV_HEADS
    _attn_core(q_ref, k_ref, v_ref, kmax_ref, scratch, n_kv=n_kv,
               k_col=lambda h: 0, k_row=lambda h: h // group, q_slot=lambda h: h // group,
               v_row=lambda h: (h // group) * GQA_VROWS, v_rows=GQA_VROWS, scores_between=True,
               vpu_denominator=False)
    acc_ref = scratch[3]
    for hh in range(GQA_HEADS):
        a = acc_ref[hh]
        o_ref[0, 0, hh * HEAD_DIM:(hh + 1) * HEAD_DIM, :] = (a[:HEAD_DIM] / a[HEAD_DIM:HEAD_DIM + 1]).astype(o_ref.dtype)


def _gqa_attention(aq, ak, av, kmax):
    b, nt, nq, _ = aq.shape
    t = nt * TM
    assert nt % 2 == 1 and (nt - 1) % FAST_UNROLL == 0 and FAST_UNROLL % 2 == 0
    return pl.pallas_call(
        functools.partial(_gqa_kernel, n_kv=nt),
        grid=(b, nt),
        in_specs=[pl.BlockSpec((1, 1, nq, TM), lambda bb, i: (bb, i, 0, 0)),
                  pl.BlockSpec((1, t, 128), lambda bb, i: (bb, 0, 0)),
                  pl.BlockSpec((1, nt, GQA_KV_HEADS * GQA_VROWS, TM), lambda bb, i: (bb, 0, 0, 0)),
                  pl.BlockSpec((1, 8, 128), lambda bb, i: (bb, 0, 0))],
        out_specs=pl.BlockSpec((1, 1, nq, TM), lambda bb, i: (bb, i, 0, 0)),
        out_shape=jax.ShapeDtypeStruct((b, nt, nq, TM), BF16),
        scratch_shapes=_attn_scratch(GQA_VROWS),
        compiler_params=_params("parallel", "arbitrary"),
        name="gqa_attn",
    )(aq, ak, av, kmax)


def _diff_kernel(q_ref, k_ref, v_ref, kmax_ref, lam_ref, gain_ref, o_ref, *scratch, n_kv, lam_init):
    _attn_core(q_ref, k_ref, v_ref, kmax_ref, scratch, n_kv=n_kv,
               k_col=lambda hc: (hc // 2) * 128, k_row=lambda hc: hc, q_slot=lambda hc: hc % 2,
               v_row=lambda hc: (hc // 2) * DIFF_VROWS, v_rows=DIFF_VROWS, scores_between=False,
               vpu_denominator=True)
    acc_ref = scratch[3]
    lp = lam_ref[...]
    lam = (jnp.exp(jnp.sum(lp[0:1] * lp[1:2], axis=1, keepdims=True))
           - jnp.exp(jnp.sum(lp[2:3] * lp[3:4], axis=1, keepdims=True)) + lam_init)
    gain = gain_ref[...]
    nv = 2 * HEAD_DIM
    for hh in range(DIFF_HEADS):
        a0, a1 = acc_ref[2 * hh], acc_ref[2 * hh + 1]
        o = a0[:nv] / a0[nv:nv + 1] - lam * (a1[:nv] / a1[nv:nv + 1])
        y = o * lax.rsqrt(jnp.mean(o * o, axis=0, keepdims=True) + NORM_EPS) * gain
        o_ref[0, 0, hh * nv:(hh + 1) * nv, :] = (y * (1.0 - lam_init)).astype(o_ref.dtype)


def _diff_attention(dq, dk, dv, kmax, lam_p, gain, layer_idx):
    b, nt, nq, _ = dq.shape
    t = nt * TM
    assert nt % 2 == 1 and (nt - 1) % FAST_UNROLL == 0 and FAST_UNROLL % 2 == 0
    lam_init = 0.8 - 0.6 * math.exp(-0.3 * layer_idx)
    return pl.pallas_call(
        functools.partial(_diff_kernel, n_kv=nt, lam_init=lam_init),
        grid=(b, nt),
        in_specs=[pl.BlockSpec((1, 1, nq, TM), lambda bb, i: (bb, i, 0, 0)),
                  pl.BlockSpec((1, t, DIFF_QK), lambda bb, i: (bb, 0, 0)),
                  pl.BlockSpec((1, nt, DIFF_HEADS * DIFF_VROWS, TM), lambda bb, i: (bb, 0, 0, 0)),
                  pl.BlockSpec((1, 8, 128), lambda bb, i: (bb, 0, 0)),
                  _const_spec(lam_p.shape),
                  _const_spec((2 * HEAD_DIM, 1))],
        out_specs=pl.BlockSpec((1, 1, DIFF_V, TM), lambda bb, i: (bb, i, 0, 0)),
        out_shape=jax.ShapeDtypeStruct((b, nt, DIFF_V, TM), BF16),
        scratch_shapes=_attn_scratch(DIFF_VROWS),
        compiler_params=_params("parallel", "arbitrary"),
        name="diff_attn",
    )(dq, dk, dv, kmax, lam_p, gain.reshape(2 * HEAD_DIM, 1))


def _split_bf16(x):
    hi = x.astype(BF16)
    return hi, (x - hi.astype(F32)).astype(BF16)


def _gla_kernel(qk_ref, la_ref, vt_ref, o_ref, st_ref, *, backward):
    @pl.when(pl.program_id(1) == 0)
    def _():
        st_ref[...] = jnp.zeros(st_ref.shape, F32)

    n_chunk = TM // GLA_CHUNK
    r = lax.broadcasted_iota(jnp.int32, (TM, TM), 0)
    c = lax.broadcasted_iota(jnp.int32, (TM, TM), 1)
    same = (r // GLA_CHUNK) == (c // GLA_CHUNK)
    causal = (c >= r) if backward else (c <= r)
    tri = jnp.where(same & causal, 1.0, 0.0).astype(BF16)
    blk = jnp.where(same, 1.0, 0.0).astype(BF16)
    la_hi, la_lo = _split_bf16(la_ref[0])
    cum = jnp.dot(tri, la_hi, preferred_element_type=F32) + jnp.dot(tri, la_lo, preferred_element_type=F32)
    tot = jnp.dot(blk, la_hi, preferred_element_type=F32) + jnp.dot(blk, la_lo, preferred_element_type=F32)
    q = qk_ref[0, :, :GLA_QK]
    k = qk_ref[0, :, GLA_QK:]
    q_in = (q * jnp.exp(cum)).astype(BF16)
    k_in = k * jnp.exp(-cum)
    k_end = k * jnp.exp(tot - cum)
    dec = jnp.exp(tot)
    lane = lax.broadcasted_iota(jnp.int32, (TM, 128), 1)
    row = lax.broadcasted_iota(jnp.int32, (TM, 128), 0)
    keep = same & ((r >= c) if backward else (r <= c))
    order = range(n_chunk - 1, -1, -1) if backward else range(n_chunk)
    hs = range(GLA_HEADS)
    pairs = [slice((h // 2) * 128, (h // 2 + 1) * 128) for h in hs]
    head_lanes = [(lane // GLA_DK) == (h % 2) for h in hs]
    q_pair = [q_in[:, pairs[h]] for h in hs]
    k_in_h = [jnp.where(head_lanes[h], k_in[:, pairs[h]], 0.0).astype(BF16) for h in hs]
    k_end_h = [jnp.where(head_lanes[h], k_end[:, pairs[h]], 0.0) for h in hs]
    vts = [vt_ref[0, 0, h * GLA_DV:(h + 1) * GLA_DV, :] for h in hs]
    a_ts = [jnp.where(keep, _nt(k_in_h[h], q_pair[h]), 0.0).astype(BF16) for h in hs]
    o_ts = [jnp.dot(vts[h], a_ts[h], preferred_element_type=F32) for h in hs]
    s_ts = [st_ref[h] for h in hs]
    for ci in order:
        in_chunk = (row // GLA_CHUNK) == ci
        for h in hs:
            q_c = jnp.where(in_chunk, q_pair[h], jnp.zeros_like(q_pair[h]))
            o_ts[h] = o_ts[h] + _nt(s_ts[h].astype(BF16), q_c)
            k_c = jnp.where(in_chunk, k_end_h[h], 0.0).astype(BF16)
            s_ts[h] = (s_ts[h] * dec[ci * GLA_CHUNK:ci * GLA_CHUNK + 1, pairs[h]]
                       + jnp.dot(vts[h], k_c, preferred_element_type=F32))
    for h in hs:
        st_ref[h] = s_ts[h]
        o_ref[0, 0, h * GLA_DV:(h + 1) * GLA_DV, :] = o_ts[h]


def _gla_direction(qk, la, vt, backward):
    b, t, _ = qk.shape
    nt = t // TM
    d = 1 if backward else 0
    tile = (lambda i: jnp.where(i == 0, 0, nt - i)) if backward else (lambda i: i)
    return pl.pallas_call(
        functools.partial(_gla_kernel, backward=backward),
        grid=(b, nt),
        in_specs=[pl.BlockSpec((1, TM, 2 * GLA_QK), lambda bb, i: (bb, tile(i), 0)),
                  pl.BlockSpec((1, TM, GLA_QK), lambda bb, i: (bb, tile(i), d)),
                  pl.BlockSpec((1, 1, GLA_V, TM), lambda bb, i: (bb, tile(i), 0, 0))],
        out_specs=pl.BlockSpec((1, 1, GLA_V, TM), lambda bb, i: (bb, tile(i), 0, 0)),
        out_shape=jax.ShapeDtypeStruct((b, nt, GLA_V, TM), F32),
        scratch_shapes=[pltpu.VMEM((GLA_HEADS, GLA_DV, 128), F32)],
        compiler_params=_params("parallel", "arbitrary"),
        name="gla_bwd" if backward else "gla_fwd",
    )(qk, la, vt)


def _post_phases(xs, mods, g, yas, ybs, woa_ref, wob_ref, w1_ref, w2_ref, ff_chunk):
    n = len(xs)
    ys = [jnp.dot(yas[s], woa_ref[...], preferred_element_type=F32)
          + jnp.dot(ybs[s].astype(F32).T.astype(BF16), wob_ref[...], preferred_element_type=F32) for s in range(n)]
    x1s = [xs[s] + mods[s][2:3] * _rms(ys[s], g[1:2]) for s in range(n)]
    hs = [(_rms(x1s[s], g[2:3]) * (1.0 + mods[s][4:5]) + mods[s][3:4]).astype(BF16) for s in range(n)]
    d_ff = w1_ref.shape[1]
    accs = [jnp.zeros(xs[0].shape, F32) for _ in range(n)]
    for c in range(d_ff // ff_chunk):
        cols = slice(c * ff_chunk, (c + 1) * ff_chunk)
        for s in range(n):
            u = jnp.dot(hs[s], w1_ref[:, cols], preferred_element_type=F32)
            u = jnp.square(jnp.maximum(u, 0.0)).astype(BF16)
            accs[s] = accs[s] + jnp.dot(u, w2_ref[cols, :], preferred_element_type=F32)
    return [x1s[s] + mods[s][5:6] * _rms(accs[s], g[3:4]) for s in range(n)]


def _post_gla_kernel(xs_ref, of_ref, ob_ref, gg_ref, gain_ref, yb_ref, mods_ref, g_ref, woa_ref, wob_ref, w1_ref,
                     w2_ref, o_ref, *, ff_chunk):
    gain = gain_ref[...]
    yas = []
    for s in range(POST_PB):
        parts = []
        for hh in range(GLA_HEADS):
            rows = slice(hh * GLA_DV, (hh + 1) * GLA_DV)
            o = of_ref[s, 0, rows, :] + ob_ref[s, 0, rows, :]
            gt = gg_ref[s, 0, rows, :]
            y = o * lax.rsqrt(jnp.mean(o * o, axis=0, keepdims=True) + NORM_EPS) * gain
            parts.append(y * (gt / (1.0 + jnp.exp(-gt))))
        yas.append(jnp.concatenate(parts, axis=0).T.astype(BF16))
    outs = _post_phases([xs_ref[s] for s in range(POST_PB)], [mods_ref[s] for s in range(POST_PB)], g_ref[...], yas,
                        [yb_ref[s, 0] for s in range(POST_PB)], woa_ref, wob_ref, w1_ref, w2_ref, ff_chunk)
    for s in range(POST_PB):
        o_ref[s] = outs[s]


def _post_kernel(xs_ref, ya_ref, yb_ref, mods_ref, g_ref, woa_ref, wob_ref, w1_ref, w2_ref, o_ref, *, ff_chunk):
    outs = _post_phases([xs_ref[s] for s in range(POST_PB)], [mods_ref[s] for s in range(POST_PB)], g_ref[...],
                        [ya_ref[s].astype(BF16) for s in range(POST_PB)], [yb_ref[s, 0] for s in range(POST_PB)],
                        woa_ref, wob_ref, w1_ref, w2_ref, ff_chunk)
    for s in range(POST_PB):
        o_ref[s] = outs[s]


def _post(xs, ya, yb_t, mods_l, g, w_out, w1, w2, skip_ctx):
    b, t, d = xs.shape
    nt = t // TM
    pb = POST_PB
    half = w_out.shape[0] // 2
    woa = w_out[:half].astype(BF16)
    wob = w_out[half:].astype(BF16)
    w1 = w1.astype(BF16)
    w2 = w2.astype(BF16)
    off = 1 if skip_ctx else 0
    tok = lambda n: pl.BlockSpec((pb, TM, n), lambda bb, i: (bb, i + off, 0))
    tr = lambda n: pl.BlockSpec((pb, 1, n, TM), lambda bb, i: (bb, i + off, 0, 0))
    if isinstance(ya, tuple):
        o_f, o_b, gg, gain = ya
        body, name = _post_gla_kernel, "post_gla"
        a_specs = [tr(half), tr(half), tr(half), _const_spec((GLA_DV, 1))]
        a_args = [o_f, o_b, gg, gain.reshape(GLA_DV, 1)]
    else:
        body, name = _post_kernel, "post"
        a_specs, a_args = [tok(half)], [ya]
    return pl.pallas_call(
        functools.partial(body, ff_chunk=1024),
        grid=(b // pb, nt - off),
        in_specs=[tok(d)] + a_specs + [
                  tr(half),
                  pl.BlockSpec((pb, N_MOD, d), lambda bb, i: (_mod_block(bb, i + off, pb), 0, 0)),
                  _const_spec(g.shape),
                  _const_spec(woa.shape), _const_spec(wob.shape), _const_spec(w1.shape), _const_spec(w2.shape)],
        out_specs=pl.BlockSpec((pb, TM, d), lambda bb, i: (bb, i, 0)),
        out_shape=jax.ShapeDtypeStruct((b, t - off * TM, d), F32),
        compiler_params=_params("parallel", "parallel"),
        name=name,
    )(xs, *a_args, yb_t, mods_l, g, woa, wob, w1, w2)


def _dot3(a, b):
    ah, al = _split_bf16(a)
    bh, bl = _split_bf16(b)
    d = lambda x, y: jnp.dot(x, y, preferred_element_type=F32)
    return d(ah, bh) + (d(al, bh) + d(ah, bl))


def _filter_kernel(w1t_ref, w1c_ref, w1s_ref, b1_ref, w2_ref, b2_ref, w3_ref, delta_ref, o_ref, *, seq, rb):
    n = pl.program_id(0) * rb + lax.broadcasted_iota(jnp.int32, (rb, 1), 0)
    d = jnp.where(n < seq, n, 2 * seq - n).astype(F32)
    t = d / max(seq - 1, 1)
    w = (2 * math.pi / seq) * d
    band = lax.broadcasted_iota(jnp.int32, (1, HY_BANDS), 1).astype(F32)
    bands = 1e-4 + band * ((HY_BANDS - 1 - 1e-4) / (HY_BANDS - 1))
    ang = w * bands
    pre = t * w1t_ref[...] + _dot3(jnp.cos(ang), w1c_ref[...]) - _dot3(jnp.sin(ang), w1s_ref[...]) + b1_ref[...]
    hid = jnp.sin(pre)
    hid = jnp.sin(_dot3(hid, w2_ref[...]) + b2_ref[...])
    decay = jnp.where(n == seq, 0.0, jnp.exp(-t * delta_ref[...]))
    for o in range(HY_ORDER):
        o_ref[o] = (_dot3(hid, w3_ref[0, o]) * decay).astype(o_ref.dtype)


def _hyena_kernels(seq, f_w1, f_b1, f_w2, f_b2, f_w3):
    rb = min(512, seq)
    nblk = 2 * seq // rb
    ffn = f_w2.shape[0]
    w3 = f_w3.reshape(ffn, HY_ORDER, 2, HY_CH).transpose(2, 1, 0, 3)
    deltas = jnp.abs(jnp.linspace(math.log(HY_TARGET) / HY_FAST_PCT, math.log(HY_TARGET) / HY_SLOW_PCT,
                                  HY_CH, dtype=F32)).reshape(1, HY_CH)
    return pl.pallas_call(
        functools.partial(_filter_kernel, seq=seq, rb=rb),
        grid=(nblk,),
        in_specs=[_const_spec((1, ffn)), _const_spec((HY_BANDS, ffn)), _const_spec((HY_BANDS, ffn)),
                  _const_spec((1, ffn)), _const_spec((ffn, ffn)), _const_spec((1, ffn)),
                  pl.BlockSpec((1, HY_ORDER, ffn, HY_CH), lambda j: ((2 * j) // nblk, 0, 0, 0)),
                  _const_spec((1, HY_CH))],
        out_specs=pl.BlockSpec((HY_ORDER, rb, HY_CH), lambda j: (0, j, 0)),
        out_shape=jax.ShapeDtypeStruct((HY_ORDER, 2 * seq, HY_CH), BF16),
        compiler_params=_params("parallel"),
        name="hy_filter",
    )(f_w1[0:1], f_w1[1:1 + HY_BANDS], f_w1[1 + HY_BANDS:], f_b1.reshape(1, ffn), f_w2, f_b2.reshape(1, ffn), w3,
      deltas)


def _dft_tables(seq):
    n = 2 * seq
    r = math.isqrt(n)
    assert r * r == n
    r1 = (r // 2 + 1 + 7) // 8 * 8
    idx = jnp.arange(r, dtype=jnp.int32)
    phi = (2 * math.pi / r) * ((idx[:r1, None] * idx[None, :]) % r).astype(F32)
    f1 = jnp.concatenate([jnp.cos(phi), -jnp.sin(phi)], axis=0).astype(BF16)
    m = (idx[:r1, None, None] + r * idx[None, :, None]) * idx[None, None, :]
    th = (2 * math.pi / n) * (m % n).astype(F32)
    c, s = jnp.cos(th), jnp.sin(th)
    blockm = lambda re, im: jnp.concatenate([jnp.concatenate([re, -im], axis=-1),
                                             jnp.concatenate([im, re], axis=-1)], axis=-2)
    g = blockm(c, -s).astype(BF16)
    ginv = blockm(jnp.swapaxes(c, 1, 2), jnp.swapaxes(s, 1, 2)).astype(BF16)
    k1 = idx[:r1]
    weight = jnp.where((k1 == 0) | (k1 == r // 2), 1.0, jnp.where(k1 < r // 2, 2.0, 0.0)) / n
    phi_t = phi.T[:r // 2]
    finv = jnp.concatenate([jnp.cos(phi_t) * weight, -jnp.sin(phi_t) * weight], axis=1).astype(BF16)
    return r, f1, g, ginv, finv


DFT_BB = 8


def _dft1_kernel(f_ref, x_ref, o_ref, *, a0):
    f = f_ref[...]
    r = o_ref.shape[2]
    x = pltpu.einshape("abc->bac", x_ref[0, a0:].astype(F32))
    ys = [jnp.dot(f, x[j].astype(BF16), preferred_element_type=F32) for j in range(DFT_BB)]
    y = pltpu.einshape("jkc->kjc", jnp.stack(ys, axis=0))
    o_ref[0, 0] = y[:r].astype(o_ref.dtype)
    o_ref[0, 1] = y[r:].astype(o_ref.dtype)


def _dft_stage1(f, x4, a0=0):
    b, rows, r, c = x4.shape
    r1 = f.shape[0] // 2
    return pl.pallas_call(
        functools.partial(_dft1_kernel, a0=a0),
        grid=(b, r // DFT_BB),
        in_specs=[_const_spec(f.shape), pl.BlockSpec((1, rows, DFT_BB, c), lambda bi, j: (bi, 0, j, 0))],
        out_specs=pl.BlockSpec((1, 2, r1, DFT_BB, c), lambda bi, j: (bi, 0, 0, j, 0)),
        out_shape=jax.ShapeDtypeStruct((b, 2, r1, r, c), BF16),
        compiler_params=_params("parallel", "parallel"),
        name="hy_dft1",
    )(f, x4)


def _fspec_kernel(a_ref, g_ref, o_ref, *, kb, r):
    for kk in range(kb):
        a = a_ref[0, :, kk].reshape(2 * r, a_ref.shape[-1])
        o_ref[0, kk] = jnp.dot(g_ref[kk], a, preferred_element_type=F32)


def _filter_spectrum(a, g, kb):
    o, _, r1, r, c = a.shape
    return pl.pallas_call(
        functools.partial(_fspec_kernel, kb=kb, r=r),
        grid=(o, r1 // kb),
        in_specs=[pl.BlockSpec((1, 2, kb, r, c), lambda oo, j: (oo, 0, j, 0, 0)),
                  pl.BlockSpec((kb, 2 * r, 2 * r), lambda oo, j: (j, 0, 0))],
        out_specs=pl.BlockSpec((1, kb, 2 * r, c), lambda oo, j: (oo, j, 0, 0)),
        out_shape=jax.ShapeDtypeStruct((o, r1, 2 * r, c), F32),
        compiler_params=_params("parallel", "parallel"),
        name="hy_fspec",
    )(a, g)


def _dft2_kernel(a_ref, g_ref, gi_ref, k_ref, o_ref, *, kb, r):
    xs = [jnp.dot(g_ref[kk], a_ref[0, :, kk].reshape(2 * r, a_ref.shape[-1]), preferred_element_type=F32)
          for kk in range(kb)]
    ys = []
    for kk in range(kb):
        x, ks = xs[kk], k_ref[0, kk]
        xr, xi, kr, ki = x[:r], x[r:], ks[:r], ks[r:]
        ys.append(jnp.concatenate([xr * kr - xi * ki, xr * ki + xi * kr], axis=0).astype(BF16))
    bqs = [jnp.dot(gi_ref[kk], ys[kk], preferred_element_type=F32) for kk in range(kb)]
    for kk in range(kb):
        o_ref[0, 0, kk] = bqs[kk][:r].astype(o_ref.dtype)
        o_ref[0, 1, kk] = bqs[kk][r:].astype(o_ref.dtype)


def _dft_stage2(a, g, ginv, kspec, order, kb):
    b, _, r1, r, c = a.shape
    blk = pl.BlockSpec((1, 2, kb, r, c), lambda j, bb: (bb, 0, j, 0, 0))
    tab = pl.BlockSpec((kb, 2 * r, 2 * r), lambda j, bb: (j, 0, 0))
    return pl.pallas_call(
        functools.partial(_dft2_kernel, kb=kb, r=r),
        grid=(r1 // kb, b),
        in_specs=[blk, tab, tab, pl.BlockSpec((1, kb, 2 * r, c), lambda j, bb: (order, j, 0, 0))],
        out_specs=blk,
        out_shape=jax.ShapeDtypeStruct(a.shape, BF16),
        compiler_params=_params("parallel", "arbitrary"),
        name="hy_dft2",
    )(a, g, ginv, kspec)


def _dft3_kernel(f_ref, b_ref, gate_ref, z_ref, bias_ref, head_ref, o_ref, *, a0):
    f = f_ref[...]
    bq = jnp.concatenate([b_ref[0, 0], b_ref[0, 1]], axis=0)
    bq = pltpu.einshape("kjc->jkc", bq.astype(F32))
    ys = [jnp.dot(f, bq[j].astype(BF16), preferred_element_type=F32) for j in range(DFT_BB)]
    y = pltpu.einshape("jac->ajc", jnp.stack(ys, axis=0))
    o_ref[0, :a0] = head_ref[0]
    o_ref[0, a0:] = gate_ref[0, a0:] * (y + z_ref[0, a0:] * bias_ref[...])


def _dft_stage3(finv, bq5, gate4, z4, bias, head4):
    b, _, r1, r, c = bq5.shape
    rows = z4.shape[1]
    a0 = head4.shape[1]
    io = pl.BlockSpec((1, rows, DFT_BB, c), lambda bi, j: (bi, 0, j, 0))
    return pl.pallas_call(
        functools.partial(_dft3_kernel, a0=a0),
        grid=(b, r // DFT_BB),
        in_specs=[_const_spec(finv.shape), pl.BlockSpec((1, 2, r1, DFT_BB, c), lambda bi, j: (bi, 0, 0, j, 0)), io, io,
                  _const_spec((1, 1, c)), pl.BlockSpec((1, a0, DFT_BB, c), lambda bi, j: (bi, 0, j, 0))],
        out_specs=io,
        out_shape=jax.ShapeDtypeStruct((b, rows, r, c), F32),
        compiler_params=_params("parallel", "parallel"),
        name="hy_dft3",
    )(finv, bq5, gate4, z4, bias, head4)


def _long_conv_gated(v, gates, kern, hy_bias, ctx_out, lc):
    b, t, c = v.shape
    r, f1, g, ginv, finv = _dft_tables(t - lc)
    assert lc % r == 0
    a0 = lc // r
    kb = 8
    kspec = _filter_spectrum(_dft_stage1(f1, kern.reshape(HY_ORDER, r, r, c)), g, kb)
    as_blocks = lambda x: x.reshape(b, t // r, r, c)
    z = as_blocks(v)
    heads = [jnp.zeros((b, a0, r, c), F32), ctx_out.reshape(b, a0, r, c)]
    for o in range(HY_ORDER):
        a = _dft_stage1(f1[:, :r // 2], z, a0)
        bq = _dft_stage2(a, g, ginv, kspec, o, kb)
        z = _dft_stage3(finv, bq, as_blocks(gates[o]), z, hy_bias[o].reshape(1, 1, c), heads[o])
    return z.reshape(b, t, c)


def _ctx_conv_kernel(v_ref, x1_ref, x2_ref, k_ref, ff_ref, fi_ref, bias_ref, o_ref):
    n = ff_ref.shape[1]
    z = v_ref[0]
    for o, gate_ref in enumerate((x1_ref, x2_ref)):
        ks = jnp.dot(ff_ref[...], k_ref[o], preferred_element_type=F32)
        x = jnp.dot(ff_ref[:, :TM], z.astype(BF16), preferred_element_type=F32)
        xr, xi, kr, ki = x[:n], x[n:], ks[:n], ks[n:]
        y = jnp.concatenate([xr * kr - xi * ki, xr * ki + xi * kr], axis=0).astype(BF16)
        conv = jnp.dot(fi_ref[...], y, preferred_element_type=F32)
        z = gate_ref[0] * (conv + z * bias_ref[o:o + 1])
    o_ref[0] = z


def _ctx_conv_gated(v, x1, x2, kern, hy_bias):
    b, _, c = v.shape
    seq = TM
    n = 2 * seq
    idx = jnp.arange(n, dtype=jnp.int32)
    th = (2 * math.pi / n) * ((idx[:, None] * idx[None, :]) % n).astype(F32)
    ff = jnp.concatenate([jnp.cos(th), -jnp.sin(th)], axis=0).astype(BF16)
    fi = (jnp.concatenate([jnp.cos(th), -jnp.sin(th)], axis=1)[:seq] / n).astype(BF16)
    io = pl.BlockSpec((1, seq, c), lambda bb: (bb, 0, 0))
    return pl.pallas_call(
        _ctx_conv_kernel,
        grid=(b,),
        in_specs=[io, io, io, _const_spec(kern.shape), _const_spec(ff.shape), _const_spec(fi.shape),
                  _const_spec(hy_bias.shape)],
        out_specs=io,
        out_shape=jax.ShapeDtypeStruct((b, seq, c), F32),
        compiler_params=_params("parallel"),
        name="hy_ctx",
    )(v, x1, x2, kern, ff, fi, hy_bias)


def _hyena(v, x1, x2, lc, f_w1, f_b1, f_w2, f_b2, f_w3, hy_bias):
    seq = v.shape[1] - lc
    yc = _ctx_conv_gated(v, x1, x2, _hyena_kernels(lc, f_w1, f_b1, f_w2, f_b2, f_w3), hy_bias)
    return _long_conv_gated(v, (x1, x2), _hyena_kernels(seq, f_w1, f_b1, f_w2, f_b2, f_w3), hy_bias, yc, lc)


def _rope_tables_t(seq, lc):
    rows = seq // GRID_W
    r = jnp.broadcast_to(jnp.arange(rows, dtype=F32)[:, None], (rows, GRID_W)).reshape(-1)
    col = jnp.broadcast_to(jnp.arange(GRID_W, dtype=F32)[None, :], (rows, GRID_W)).reshape(-1)
    n_pairs = HEAD_DIM // 4
    inv = ROPE_THETA ** (-jnp.arange(n_pairs, dtype=F32) / n_pairs)
    ang = jnp.concatenate([r[:, None] * inv, col[:, None] * inv], axis=-1)
    cos = jnp.concatenate([jnp.ones((lc, 2 * n_pairs), F32), jnp.cos(ang)], axis=0)
    sin = jnp.concatenate([jnp.zeros((lc, 2 * n_pairs), F32), jnp.sin(ang)], axis=0)
    return cos.T, sin.T


def kernel(x, c, ctx, c_ctx, w_ada, b_ada, norm_g, w_mlp_in, w_mlp_out, ev_w_in, ev_w_lr, ev_b_lr, ev_gla_g,
           ev_qk_g, ev_w_out, od_w_in, od_conv_w, od_conv_b, od_f_w1, od_f_b1, od_f_w2, od_f_b2, od_f_w3,
           od_hy_bias, od_lam, od_diff_g, od_w_out):
    bsz, seq, d = x.shape
    lc = ctx.shape[1]
    depth = w_ada.shape[0]
    assert lc == TM and seq % TM == 0 and bsz <= MOD_CTX_ROW and bsz % POST_PB == 0 and bsz % IN_PB == 0
    cos_t, sin_t = _rope_tables_t(seq, lc)
    cc = jnp.zeros((8, d), F32).at[:bsz].set(c).at[MOD_CTX_ROW:MOD_CTX_ROW + max(POST_PB, IN_PB)].set(c_ctx)
    mods = _mods(cc, w_ada, b_ada).reshape(depth, 8, N_MOD, d)
    xs = jnp.concatenate([ctx, x], axis=1)
    for i in range(depth):
        last = i == depth - 1
        j = i // 2
        g = norm_g[i]
        if i % 2 == 0:
            qk, la, gv, gg, aq, ak, av, kn = _even_in(xs, mods[i], g[0:1], ev_w_in[j], ev_w_lr[j], ev_b_lr[j],
                                                  ev_qk_g[j], cos_t, sin_t)
            ya = (_gla_direction(qk, la, gv, False), _gla_direction(qk, la, gv, True), gg, ev_gla_g[j])
            yb = _gqa_attention(aq, ak, av, jnp.max(kn, axis=1))
            w_out = ev_w_out[j]
        else:
            hv, hx1, hx2, dq, dk, dv, kn = _odd_in(xs, mods[i], g[0:1], od_w_in[j], od_conv_w[j], od_conv_b[j],
                                                   cos_t, sin_t)
            ya = _hyena(hv, hx1, hx2, lc, od_f_w1[j], od_f_b1[j], od_f_w2[j], od_f_b2[j], od_f_w3[j], od_hy_bias[j])
            yb = _diff_attention(dq, dk, dv, jnp.max(kn, axis=1), od_lam[j], od_diff_g[j], i)
            w_out = od_w_out[j]
        xs = _post(xs, ya, yb, mods[i], g, w_out, w_mlp_in[i], w_mlp_out[i], skip_ctx=last)
    return xs
```

```python
import functools
import math

import jax
import jax.numpy as jnp
import numpy as np
from jax import lax
from jax.experimental import pallas as pl
from jax.experimental.pallas import tpu as pltpu

F32 = jnp.float32
BF16 = jnp.bfloat16

TM = 256
GRID_W = 64
HEAD_DIM = 64
ROPE_THETA = 10000.0
NORM_EPS = 1e-6
N_MOD = 6
VMEM_LIMIT = 56 * 1024 * 1024

GLA_HEADS = 4
GLA_DK = 64
GLA_DV = 128
GLA_RANK = 16
GLA_NORMALIZER = 16.0
GLA_CHUNK = 64
GLA_QK = GLA_HEADS * GLA_DK
GLA_V = GLA_HEADS * GLA_DV
GQA_HEADS = 8
GQA_KV_HEADS = 2
HY_CH = 512
HY_ORDER = 2
HY_BANDS = 16
HY_TARGET = 1e-2
HY_FAST_PCT = 0.3
HY_SLOW_PCT = 1.5
HY_COLS = (HY_ORDER + 1) * HY_CH
DIFF_HEADS = 4
DIFF_QK = DIFF_HEADS * 2 * HEAD_DIM
DIFF_V = DIFF_HEADS * 2 * HEAD_DIM
ONES_ROWS = 16
N_SCORE_HEADS = 8
LOG2E = 1.4426950408889634
Q_SCALE = HEAD_DIM ** -0.5 * LOG2E
GQA_VROWS = HEAD_DIM + ONES_ROWS
SCORE_BOUND_SLACK = 1.01
FAST_UNROLL_MAX = 16
SCORE_BOUND_LIMIT = 60.0
DIFF_VROWS = 2 * HEAD_DIM + ONES_ROWS


def _params(*sem):
    return pltpu.CompilerParams(dimension_semantics=sem, vmem_limit_bytes=VMEM_LIMIT)


def _const_spec(shape):
    nd = len(shape)
    return pl.BlockSpec(shape, lambda *_: (0,) * nd, pipeline_mode=pl.Buffered(1))


def _rms(x, g):
    return x * lax.rsqrt(jnp.mean(x * x, axis=-1, keepdims=True) + NORM_EPS) * g


MOD_CTX_ROW = 4
POST_PB = 2
IN_PB = 2


def _mod_block(bb, i, pb):
    return jnp.where(i == 0, MOD_CTX_ROW // pb, bb)


def _mods_kernel(cc_ref, w_ref, b_ref, o_ref):
    cc = cc_ref[...]
    s = cc / (1.0 + jnp.exp(-cc))
    o_ref[0] = jnp.dot(s.astype(BF16), w_ref[0].astype(BF16), preferred_element_type=F32) + b_ref[0]


def _mods(cc, w_ada, b_ada):
    depth, d, n = w_ada.shape
    nb = n // 4
    return pl.pallas_call(
        _mods_kernel,
        grid=(depth, n // nb),
        in_specs=[pl.BlockSpec((8, d), lambda l, j: (0, 0)),
                  pl.BlockSpec((1, d, nb), lambda l, j: (l, 0, j)),
                  pl.BlockSpec((1, 1, nb), lambda l, j: (l, 0, j))],
        out_specs=pl.BlockSpec((1, 8, nb), lambda l, j: (l, 0, j)),
        out_shape=jax.ShapeDtypeStruct((depth, 8, n), F32),
        compiler_params=_params("parallel", "parallel"),
        name="ada_mods",
    )(cc, w_ada, b_ada.reshape(depth, 1, n))


def _normed_input(x, mods, g):
    return (_rms(x, g) * (1.0 + mods[1:2]) + mods[0:1]).astype(BF16)


def _rope_rows(y, cos, sin):
    x1, x2 = y[:32], y[32:]
    return x1 * cos - x2 * sin, x1 * sin + x2 * cos


def _log_sigmoid(x):
    return jnp.minimum(x, 0.0) - jnp.log(1.0 + jnp.exp(-jnp.abs(x)))


def _nt(a, b):
    return lax.dot_general(a, b, (((1,), (1,)), ((), ())), preferred_element_type=F32)


def _max_sq_norm_row(parts):
    n2 = sum(jnp.sum(jnp.square(p.astype(BF16).astype(F32)), axis=0, keepdims=True) for p in parts)
    return jnp.broadcast_to(jnp.max(n2, axis=1, keepdims=True), (1, 128))


def _even_in_kernel(xs_ref, mods_ref, g_ref, wn_ref, wt_ref, wlr_ref, blr_ref, gq_ref, gk_ref,
                    cos_ref, sin_ref, qk_ref, la_ref, gv_ref, gg_ref, aq_ref, ak_ref, av_ref, kn_ref):
    hs = [_normed_input(xs_ref[s], mods_ref[s], g_ref[0:1]) for s in range(IN_PB)]
    ps = [jnp.dot(h, wn_ref[...], preferred_element_type=F32) for h in hs]
    pts = [_nt(wt_ref[...], h) for h in hs]
    cos, sin = cos_ref[...], sin_ref[...]
    gq, gk_gain = gq_ref[...], gk_ref[...]
    ones = jnp.ones((ONES_ROWS, TM), BF16)
    for s in range(IN_PB):
        p, pt_all = ps[s], pts[s]
        qk_ref[s, :, :GLA_QK] = p[:, :GLA_QK] * GLA_DK ** -0.5
        qk_ref[s, :, GLA_QK:] = p[:, GLA_QK:2 * GLA_QK]
        lr = p[:, 2 * GLA_QK:].astype(BF16)
        gk = jnp.dot(lr, wlr_ref[...], preferred_element_type=F32) + blr_ref[...]
        la_ref[s] = _log_sigmoid(gk) * (1.0 / GLA_NORMALIZER)
        gv_ref[s, 0] = pt_all[:GLA_V].astype(BF16)
        gg_ref[s, 0] = pt_all[GLA_V:2 * GLA_V]
        pt = pt_all[2 * GLA_V:]
        for hh in range(GQA_HEADS):
            blk = pt[hh * 64:(hh + 1) * 64]
            y = blk * lax.rsqrt(jnp.mean(blk * blk, axis=0, keepdims=True) + NORM_EPS) * gq
            o1, o2 = _rope_rows(y, cos, sin)
            aq_ref[s, 0, hh * 64:hh * 64 + 32, :] = (o1 * Q_SCALE).astype(BF16)
            aq_ref[s, 0, hh * 64 + 32:hh * 64 + 64, :] = (o2 * Q_SCALE).astype(BF16)
        kparts, knorms = [], []
        for hh in range(GQA_KV_HEADS):
            blk = pt[512 + hh * 64:512 + (hh + 1) * 64]
            y = blk * lax.rsqrt(jnp.mean(blk * blk, axis=0, keepdims=True) + NORM_EPS) * gk_gain
            o1, o2 = _rope_rows(y, cos, sin)
            kparts += [o1, o2]
            knorms.append(_max_sq_norm_row([o1, o2]))
        ak_ref[s] = jnp.concatenate(kparts, axis=0).T.astype(BF16)
        kn_ref[s, 0] = jnp.concatenate(knorms + [jnp.zeros((8 - GQA_KV_HEADS, 128), F32)], axis=0)
        for hh in range(GQA_KV_HEADS):
            av_ref[s, 0, hh * GQA_VROWS:hh * GQA_VROWS + 64, :] = pt[640 + hh * 64:640 + (hh + 1) * 64].astype(BF16)
            av_ref[s, 0, hh * GQA_VROWS + 64:(hh + 1) * GQA_VROWS, :] = ones


def _odd_in_kernel(xs_ref, prev_ref, next_ref, mods_ref, g_ref, wn_ref, wt_ref, cw_ref, cb_ref, cos_ref, sin_ref,
                   v_ref, x1_ref, x2_ref, dq_ref, dk_ref, dv_ref, kn_ref, *, n_tiles):
    i = pl.program_id(1)
    hs = [_normed_input(xs_ref[s], mods_ref[s], g_ref[0:1]) for s in range(IN_PB)]
    h_ext = [jnp.concatenate([hs[s], _normed_input(prev_ref[s], mods_ref[s], g_ref[0:1]),
                              _normed_input(next_ref[s], mods_ref[s], g_ref[0:1])], axis=0) for s in range(IN_PB)]
    us = [jnp.dot(h, wn_ref[...], preferred_element_type=F32) for h in h_ext]
    pts = [_nt(wt_ref[...], h) for h in hs]
    cos, sin = cos_ref[...], sin_ref[...]
    ones = jnp.ones((ONES_ROWS, TM), BF16)
    nv = 2 * HEAD_DIM
    cw = cw_ref[...]
    row_id = lax.broadcasted_iota(jnp.int32, (TM, HY_COLS), 0)
    no_prev = (i == 0) | (i == 1)
    no_next = (i == 0) | (i == n_tiles - 1)
    for s in range(IN_PB):
        pt = pts[s]
        u = us[s][:TM]
        prev_row = jnp.where(no_prev, 0.0, us[s][TM + 7:TM + 8])
        next_row = jnp.where(no_next, 0.0, us[s][TM + 8:TM + 9])
        up = jnp.where(row_id == 0, prev_row, pltpu.roll(u, 1, axis=0))
        dn = jnp.where(row_id == TM - 1, next_row, pltpu.roll(u, TM - 1, axis=0))
        conv = up * cw[0:1] + u * cw[1:2] + dn * cw[2:3] + cb_ref[...]
        v_ref[s] = conv[:, :HY_CH]
        x1_ref[s] = conv[:, HY_CH:2 * HY_CH]
        x2_ref[s] = conv[:, 2 * HY_CH:]
        kparts, knorms = [], []
        for hh in range(2 * DIFF_HEADS):
            o1, o2 = _rope_rows(pt[hh * 64:(hh + 1) * 64], cos, sin)
            dq_ref[s, 0, hh * 64:hh * 64 + 32, :] = (o1 * Q_SCALE).astype(BF16)
            dq_ref[s, 0, hh * 64 + 32:hh * 64 + 64, :] = (o2 * Q_SCALE).astype(BF16)
            k1, k2 = _rope_rows(pt[DIFF_QK + hh * 64:DIFF_QK + (hh + 1) * 64], cos, sin)
            kparts += [k1, k2]
            knorms.append(_max_sq_norm_row([k1, k2]))
        dk_ref[s] = jnp.concatenate(kparts, axis=0).T.astype(BF16)
        kn_ref[s, 0] = jnp.concatenate(knorms, axis=0)
        for hh in range(DIFF_HEADS):
            rows = pt[2 * DIFF_QK + hh * nv:2 * DIFF_QK + (hh + 1) * nv]
            dv_ref[s, 0, hh * DIFF_VROWS:hh * DIFF_VROWS + nv, :] = rows.astype(BF16)
            dv_ref[s, 0, hh * DIFF_VROWS + nv:(hh + 1) * DIFF_VROWS, :] = ones


def _pair_perm(n_heads):
    base = np.concatenate([np.arange(0, HEAD_DIM, 2), np.arange(1, HEAD_DIM, 2)])
    return np.concatenate([h * HEAD_DIM + base for h in range(n_heads)])


def _even_in(xs, mods_l, g0, w_in, w_lr, b_lr, qk_g, cos_t, sin_t):
    b, t, d = xs.shape
    nt = t // TM
    c_v = 2 * GLA_QK
    c_g = c_v + GLA_V
    c_lr0 = c_g + GLA_V
    c_lr = c_lr0 + 2 * GLA_RANK
    wn = jnp.concatenate([w_in[:, :c_v], w_in[:, c_lr0:c_lr], jnp.zeros((d, 128 - 2 * GLA_RANK), F32)],
                         axis=1).astype(BF16)
    cq = c_lr + GQA_HEADS * HEAD_DIM
    ck = cq + GQA_KV_HEADS * HEAD_DIM
    wq = w_in[:, c_lr:cq][:, _pair_perm(GQA_HEADS)]
    wk = w_in[:, cq:ck][:, _pair_perm(GQA_KV_HEADS)]
    wt = jnp.concatenate([w_in[:, c_v:c_lr0], wq, wk, w_in[:, ck:]], axis=1).T.astype(BF16)
    wlr = jnp.zeros((128, 2 * GLA_QK), F32)
    wlr = wlr.at[:GLA_RANK, :GLA_QK].set(w_lr[0]).at[GLA_RANK:2 * GLA_RANK, GLA_QK:].set(w_lr[1]).astype(BF16)
    blr = b_lr.reshape(1, 2 * GLA_QK)
    perm = _pair_perm(1)
    gq = qk_g[0][perm].reshape(HEAD_DIM, 1)
    gk = qk_g[1][perm].reshape(HEAD_DIM, 1)
    tok = lambda n: pl.BlockSpec((IN_PB, TM, n), lambda bb, i: (bb, i, 0))
    tr = lambda n: pl.BlockSpec((IN_PB, 1, n, TM), lambda bb, i: (bb, i, 0, 0))
    return pl.pallas_call(
        _even_in_kernel,
        grid=(b // IN_PB, nt),
        in_specs=[tok(d),
                  pl.BlockSpec((IN_PB, N_MOD, d), lambda bb, i: (_mod_block(bb, i, IN_PB), 0, 0)),
                  _const_spec((1, d)),
                  _const_spec(wn.shape), _const_spec(wt.shape), _const_spec(wlr.shape), _const_spec(blr.shape),
                  _const_spec(gq.shape), _const_spec(gk.shape),
                  pl.BlockSpec((32, TM), lambda bb, i: (0, i)),
                  pl.BlockSpec((32, TM), lambda bb, i: (0, i))],
        out_specs=[tok(2 * GLA_QK), tok(2 * GLA_QK), tr(GLA_V), tr(GLA_V), tr(512), tok(128),
                   tr(GQA_KV_HEADS * GQA_VROWS), pl.BlockSpec((IN_PB, 1, 8, 128), lambda bb, i: (bb, i, 0, 0))],
        out_shape=[jax.ShapeDtypeStruct((b, t, 2 * GLA_QK), F32),
                   jax.ShapeDtypeStruct((b, t, 2 * GLA_QK), F32),
                   jax.ShapeDtypeStruct((b, nt, GLA_V, TM), BF16),
                   jax.ShapeDtypeStruct((b, nt, GLA_V, TM), F32),
                   jax.ShapeDtypeStruct((b, nt, 512, TM), BF16),
                   jax.ShapeDtypeStruct((b, t, 128), BF16),
                   jax.ShapeDtypeStruct((b, nt, GQA_KV_HEADS * GQA_VROWS, TM), BF16),
                   jax.ShapeDtypeStruct((b, nt, 8, 128), F32)],
        compiler_params=_params("parallel", "parallel"),
        name="even_in",
    )(xs, mods_l, g0, wn, wt, wlr, blr, gq, gk, cos_t, sin_t)


def _odd_in(xs, mods_l, g0, w_in, conv_w, conv_b, cos_t, sin_t):
    b, t, d = xs.shape
    nt = t // TM
    r8 = TM // 8
    last_blk = t // 8 - 1
    wn = w_in[:, :HY_COLS].astype(BF16)
    perm = _pair_perm(2 * DIFF_HEADS)
    wq = w_in[:, HY_COLS:HY_COLS + DIFF_QK][:, perm]
    wk = w_in[:, HY_COLS + DIFF_QK:HY_COLS + 2 * DIFF_QK][:, perm]
    wt = jnp.concatenate([wq, wk, w_in[:, HY_COLS + 2 * DIFF_QK:]], axis=1).T.astype(BF16)
    tok = lambda n: pl.BlockSpec((IN_PB, TM, n), lambda bb, i: (bb, i, 0))
    tr = lambda n: pl.BlockSpec((IN_PB, 1, n, TM), lambda bb, i: (bb, i, 0, 0))
    hy = jax.ShapeDtypeStruct((b, t, HY_CH), F32)
    return pl.pallas_call(
        functools.partial(_odd_in_kernel, n_tiles=nt),
        grid=(b // IN_PB, nt),
        in_specs=[tok(d),
                  pl.BlockSpec((IN_PB, 8, d), lambda bb, i: (bb, jnp.maximum(i * r8 - 1, 0), 0)),
                  pl.BlockSpec((IN_PB, 8, d), lambda bb, i: (bb, jnp.minimum((i + 1) * r8, last_blk), 0)),
                  pl.BlockSpec((IN_PB, N_MOD, d), lambda bb, i: (_mod_block(bb, i, IN_PB), 0, 0)),
                  _const_spec((1, d)),
                  _const_spec(wn.shape), _const_spec(wt.shape),
                  _const_spec(conv_w.shape), _const_spec((1, HY_COLS)),
                  pl.BlockSpec((32, TM), lambda bb, i: (0, i)),
                  pl.BlockSpec((32, TM), lambda bb, i: (0, i))],
        out_specs=[tok(HY_CH), tok(HY_CH), tok(HY_CH), tr(DIFF_QK), tok(DIFF_QK), tr(DIFF_HEADS * DIFF_VROWS),
                   pl.BlockSpec((IN_PB, 1, 8, 128), lambda bb, i: (bb, i, 0, 0))],
        out_shape=[hy, hy, hy,
                   jax.ShapeDtypeStruct((b, nt, DIFF_QK, TM), BF16),
                   jax.ShapeDtypeStruct((b, t, DIFF_QK), BF16),
                   jax.ShapeDtypeStruct((b, nt, DIFF_HEADS * DIFF_VROWS, TM), BF16),
                   jax.ShapeDtypeStruct((b, nt, 8, 128), F32)],
        compiler_params=_params("parallel", "parallel"),
        name="odd_in",
    )(xs, xs, xs, mods_l, g0, wn, wt, conv_w, conv_b.reshape(1, HY_COLS), cos_t, sin_t)


def _attn_core(q_ref, k_ref, v_ref, kmax_ref, scratch, *, n_kv, k_col, k_row, q_slot, v_row, v_rows, scores_between,
               vpu_denominator):
    qpad_ref, bound_ref, m_ref, acc_ref, s0_ref, s1_ref, c0_ref, c1_ref, p0_ref, p1_ref = scratch
    s_refs, c_refs, p_refs = (s0_ref, s1_ref), (c0_ref, c1_ref), (p0_ref, p1_ref)
    i = pl.program_id(1)
    zeros = jnp.zeros((HEAD_DIM, TM), BF16)
    bound_max = jnp.zeros((1, TM), F32)
    for hh in range(N_SCORE_HEADS):
        qh = q_ref[0, 0, hh * HEAD_DIM:(hh + 1) * HEAD_DIM, :]
        qpad_ref[hh] = jnp.concatenate([qh, zeros] if q_slot(hh) == 0 else [zeros, qh], axis=0)
        qf = qh.astype(F32)
        q_norm2 = jnp.sum(qf * qf, axis=0, keepdims=True)
        k_norm2 = kmax_ref[0, k_row(hh):k_row(hh) + 1, 0:1]
        bound = jnp.sqrt(q_norm2 * k_norm2) * SCORE_BOUND_SLACK
        bound_ref[hh] = bound
        bound_max = jnp.maximum(bound_max, bound)
    acc_ref[...] = jnp.zeros(acc_ref.shape, F32)
    n = jnp.where(i == 0, 1, n_kv)
    fixed_reference = jnp.max(bound_max) <= SCORE_BOUND_LIMIT

    def produce(j, hh, slot):
        kc = k_ref[0, pl.ds(pl.multiple_of(j * TM, TM), TM), k_col(hh):k_col(hh) + 128]
        s = jnp.dot(kc, qpad_ref[hh], preferred_element_type=F32)
        p = jnp.exp2(s - bound_ref[hh])
        if vpu_denominator:
            m_ref[hh] = m_ref[hh] + jnp.sum(p, axis=0, keepdims=True)
        p_refs[slot][hh] = p.astype(BF16)

    n_val = v_rows - ONES_ROWS if vpu_denominator else v_rows

    def reduce(j, hh, slot):
        vc = v_ref[0, j, v_row(hh):v_row(hh) + n_val, :]
        acc_ref[hh, :n_val] = acc_ref[hh, :n_val] + jnp.dot(vc, p_refs[slot][hh], preferred_element_type=F32)

    unroll = math.gcd(n_kv - 1, FAST_UNROLL_MAX)

    @pl.when(fixed_reference)
    def _():
        m_ref[...] = jnp.zeros(m_ref.shape, F32)
        for hh in range(N_SCORE_HEADS):
            produce(0, hh, 0)

        def fast_step(j, slot):
            for hh in range(N_SCORE_HEADS):
                produce(j + 1, hh, 1 - slot)
                reduce(j, hh, slot)

        def fast_body(jj, carry):
            for u in range(unroll):
                fast_step(unroll * jj + u, u % 2)
            return carry

        lax.fori_loop(0, (n - 1) // unroll, fast_body, 0)
        for hh in range(N_SCORE_HEADS):
            reduce(n - 1, hh, 0)
            if vpu_denominator:
                acc_ref[hh, n_val:, :] = jnp.broadcast_to(m_ref[hh], (ONES_ROWS, TM))

    @pl.when(jnp.logical_not(fixed_reference))
    def _():
        _attn_running_max(k_ref, v_ref, qpad_ref, m_ref, acc_ref, s_refs, c_refs, n=n, k_col=k_col, v_row=v_row,
                          v_rows=v_rows, scores_between=scores_between)


def _attn_running_max(k_ref, v_ref, qpad_ref, m_ref, acc_ref, s_refs, c_refs, *, n, k_col, v_row, v_rows,
                      scores_between):
    m_ref[...] = jnp.full(m_ref.shape, -jnp.inf, F32)

    def scores(j, hh, slot):
        kc = k_ref[0, pl.ds(pl.multiple_of(j * TM, TM), TM), k_col(hh):k_col(hh) + 128]
        s = jnp.dot(kc, qpad_ref[hh], preferred_element_type=F32)
        s_refs[slot][hh] = s
        c_refs[slot][hh] = jnp.max(s, axis=0, keepdims=True)

    def probs(hh, slot):
        m_old = m_ref[hh]
        m_new = jnp.maximum(m_old, c_refs[slot][hh])
        return m_old, m_new, jnp.exp2(s_refs[slot][hh] - m_new).astype(BF16)

    def accumulate(j, hh, state):
        m_old, m_new, p = state
        vc = v_ref[0, j, v_row(hh):v_row(hh) + v_rows, :]
        alpha = jnp.exp2(m_old - m_new)
        acc_ref[hh] = acc_ref[hh] * alpha + jnp.dot(vc, p, preferred_element_type=F32)
        m_ref[hh] = m_new

    for hh in range(N_SCORE_HEADS):
        scores(0, hh, 0)

    def step(j, slot):
        for hh in range(N_SCORE_HEADS):
            state = probs(hh, slot)
            if scores_between:
                scores(j + 1, hh, 1 - slot)
            accumulate(j, hh, state)
            if not scores_between:
                scores(j + 1, hh, 1 - slot)

    def body(jj, carry):
        step(2 * jj, 0)
        step(2 * jj + 1, 1)
        return carry

    lax.fori_loop(0, (n - 1) // 2, body, 0)
    for hh in range(N_SCORE_HEADS):
        accumulate(n - 1, hh, probs(hh, 0))


def _attn_scratch(v_rows):
    stat = pltpu.VMEM((N_SCORE_HEADS, 1, TM), F32)
    return [pltpu.VMEM((N_SCORE_HEADS, 128, TM), BF16),
            stat, stat,
            pltpu.VMEM((N_SCORE_HEADS, v_rows, TM), F32),
            pltpu.VMEM((N_SCORE_HEADS, TM, TM), F32), pltpu.VMEM((N_SCORE_HEADS, TM, TM), F32),
            stat, stat,
            pltpu.VMEM((N_SCORE_HEADS, TM, TM), BF16), pltpu.VMEM((N_SCORE_HEADS, TM, TM), BF16)]


def _gqa_kernel(q_ref, k_ref, v_ref, kmax_ref, o_ref, *scratch, n_kv):
    group = GQA_HEADS // GQA_KV_HEADS
    _attn_core(q_ref, k_ref, v_ref, kmax_ref, scratch, n_kv=n_kv,
               k_col=lambda h: 0, k_row=lambda h: h // group, q_slot=lambda h: h // group,
               v_row=lambda h: (h // group) * GQA_VROWS, v_rows=GQA_VROWS, scores_between=True,
               vpu_denominator=False)
    acc_ref = scratch[3]
    for hh in range(GQA_HEADS):
        a = acc_ref[hh]
        o_ref[0, 0, hh * HEAD_DIM:(hh + 1) * HEAD_DIM, :] = (a[:HEAD_DIM] / a[HEAD_DIM:HEAD_DIM + 1]).astype(o_ref.dtype)


def _gqa_attention(aq, ak, av, kmax):
    b, nt, nq, _ = aq.shape
    t = nt * TM
    assert nt % 2 == 1 and nt > 1
    return pl.pallas_call(
        functools.partial(_gqa_kernel, n_kv=nt),
        grid=(b, nt),
        in_specs=[pl.BlockSpec((1, 1, nq, TM), lambda bb, i: (bb, i, 0, 0)),
                  pl.BlockSpec((1, t, 128), lambda bb, i: (bb, 0, 0)),
                  pl.BlockSpec((1, nt, GQA_KV_HEADS * GQA_VROWS, TM), lambda bb, i: (bb, 0, 0, 0)),
                  pl.BlockSpec((1, 8, 128), lambda bb, i: (bb, 0, 0))],
        out_specs=pl.BlockSpec((1, 1, nq, TM), lambda bb, i: (bb, i, 0, 0)),
        out_shape=jax.ShapeDtypeStruct((b, nt, nq, TM), BF16),
        scratch_shapes=_attn_scratch(GQA_VROWS),
        compiler_params=_params("parallel", "arbitrary"),
        name="gqa_attn",
    )(aq, ak, av, kmax)


def _diff_kernel(q_ref, k_ref, v_ref, kmax_ref, lam_ref, gain_ref, o_ref, *scratch, n_kv, lam_init):
    _attn_core(q_ref, k_ref, v_ref, kmax_ref, scratch, n_kv=n_kv,
               k_col=lambda hc: (hc // 2) * 128, k_row=lambda hc: hc, q_slot=lambda hc: hc % 2,
               v_row=lambda hc: (hc // 2) * DIFF_VROWS, v_rows=DIFF_VROWS, scores_between=False,
               vpu_denominator=True)
    acc_ref = scratch[3]
    lp = lam_ref[...]
    lam = (jnp.exp(jnp.sum(lp[0:1] * lp[1:2], axis=1, keepdims=True))
           - jnp.exp(jnp.sum(lp[2:3] * lp[3:4], axis=1, keepdims=True)) + lam_init)
    gain = gain_ref[...]
    nv = 2 * HEAD_DIM
    for hh in range(DIFF_HEADS):
        a0, a1 = acc_ref[2 * hh], acc_ref[2 * hh + 1]
        o = a0[:nv] / a0[nv:nv + 1] - lam * (a1[:nv] / a1[nv:nv + 1])
        y = o * lax.rsqrt(jnp.mean(o * o, axis=0, keepdims=True) + NORM_EPS) * gain
        o_ref[0, 0, hh * nv:(hh + 1) * nv, :] = (y * (1.0 - lam_init)).astype(o_ref.dtype)


def _diff_attention(dq, dk, dv, kmax, lam_p, gain, layer_idx):
    b, nt, nq, _ = dq.shape
    t = nt * TM
    assert nt % 2 == 1 and nt > 1
    lam_init = 0.8 - 0.6 * math.exp(-0.3 * layer_idx)
    return pl.pallas_call(
        functools.partial(_diff_kernel, n_kv=nt, lam_init=lam_init),
        grid=(b, nt),
        in_specs=[pl.BlockSpec((1, 1, nq, TM), lambda bb, i: (bb, i, 0, 0)),
                  pl.BlockSpec((1, t, DIFF_QK), lambda bb, i: (bb, 0, 0)),
                  pl.BlockSpec((1, nt, DIFF_HEADS * DIFF_VROWS, TM), lambda bb, i: (bb, 0, 0, 0)),
                  pl.BlockSpec((1, 8, 128), lambda bb, i: (bb, 0, 0)),
                  _const_spec(lam_p.shape),
                  _const_spec((2 * HEAD_DIM, 1))],
        out_specs=pl.BlockSpec((1, 1, DIFF_V, TM), lambda bb, i: (bb, i, 0, 0)),
        out_shape=jax.ShapeDtypeStruct((b, nt, DIFF_V, TM), BF16),
        scratch_shapes=_attn_scratch(DIFF_VROWS),
        compiler_params=_params("parallel", "arbitrary"),
        name="diff_attn",
    )(dq, dk, dv, kmax, lam_p, gain.reshape(2 * HEAD_DIM, 1))


def _split_bf16(x):
    hi = x.astype(BF16)
    return hi, (x - hi.astype(F32)).astype(BF16)


def _gla_kernel(qk_ref, la_ref, vt_ref, o_ref, st_ref, *, backward):
    @pl.when(pl.program_id(1) == 0)
    def _():
        st_ref[...] = jnp.zeros(st_ref.shape, F32)

    n_chunk = TM // GLA_CHUNK
    r = lax.broadcasted_iota(jnp.int32, (TM, TM), 0)
    c = lax.broadcasted_iota(jnp.int32, (TM, TM), 1)
    same = (r // GLA_CHUNK) == (c // GLA_CHUNK)
    causal = (c >= r) if backward else (c <= r)
    tri = jnp.where(same & causal, 1.0, 0.0).astype(BF16)
    blk = jnp.where(same, 1.0, 0.0).astype(BF16)
    la_hi, la_lo = _split_bf16(la_ref[0])
    cum = jnp.dot(tri, la_hi, preferred_element_type=F32) + jnp.dot(tri, la_lo, preferred_element_type=F32)
    tot = jnp.dot(blk, la_hi, preferred_element_type=F32) + jnp.dot(blk, la_lo, preferred_element_type=F32)
    q = qk_ref[0, :, :GLA_QK]
    k = qk_ref[0, :, GLA_QK:]
    q_in = (q * jnp.exp(cum)).astype(BF16)
    k_in = k * jnp.exp(-cum)
    k_end = k * jnp.exp(tot - cum)
    dec = jnp.exp(tot)
    lane = lax.broadcasted_iota(jnp.int32, (TM, 128), 1)
    row = lax.broadcasted_iota(jnp.int32, (TM, 128), 0)
    keep = same & ((r >= c) if backward else (r <= c))
    order = range(n_chunk - 1, -1, -1) if backward else range(n_chunk)
    hs = range(GLA_HEADS)
    pairs = [slice((h // 2) * 128, (h // 2 + 1) * 128) for h in hs]
    head_lanes = [(lane // GLA_DK) == (h % 2) for h in hs]
    q_pair = [q_in[:, pairs[h]] for h in hs]
    k_in_h = [jnp.where(head_lanes[h], k_in[:, pairs[h]], 0.0).astype(BF16) for h in hs]
    k_end_h = [jnp.where(head_lanes[h], k_end[:, pairs[h]], 0.0) for h in hs]
    vts = [vt_ref[0, 0, h * GLA_DV:(h + 1) * GLA_DV, :] for h in hs]
    a_ts = [jnp.where(keep, _nt(k_in_h[h], q_pair[h]), 0.0).astype(BF16) for h in hs]
    o_ts = [jnp.dot(vts[h], a_ts[h], preferred_element_type=F32) for h in hs]
    s_ts = [st_ref[h] for h in hs]
    for ci in order:
        in_chunk = (row // GLA_CHUNK) == ci
        for h in hs:
            q_c = jnp.where(in_chunk, q_pair[h], jnp.zeros_like(q_pair[h]))
            o_ts[h] = o_ts[h] + _nt(s_ts[h].astype(BF16), q_c)
            k_c = jnp.where(in_chunk, k_end_h[h], 0.0).astype(BF16)
            s_ts[h] = (s_ts[h] * dec[ci * GLA_CHUNK:ci * GLA_CHUNK + 1, pairs[h]]
                       + jnp.dot(vts[h], k_c, preferred_element_type=F32))
    for h in hs:
        st_ref[h] = s_ts[h]
        o_ref[0, 0, h * GLA_DV:(h + 1) * GLA_DV, :] = o_ts[h]


def _gla_direction(qk, la, vt, backward):
    b, t, _ = qk.shape
    nt = t // TM
    d = 1 if backward else 0
    tile = (lambda i: jnp.where(i == 0, 0, nt - i)) if backward else (lambda i: i)
    return pl.pallas_call(
        functools.partial(_gla_kernel, backward=backward),
        grid=(b, nt),
        in_specs=[pl.BlockSpec((1, TM, 2 * GLA_QK), lambda bb, i: (bb, tile(i), 0)),
                  pl.BlockSpec((1, TM, GLA_QK), lambda bb, i: (bb, tile(i), d)),
                  pl.BlockSpec((1, 1, GLA_V, TM), lambda bb, i: (bb, tile(i), 0, 0))],
        out_specs=pl.BlockSpec((1, 1, GLA_V, TM), lambda bb, i: (bb, tile(i), 0, 0)),
        out_shape=jax.ShapeDtypeStruct((b, nt, GLA_V, TM), F32),
        scratch_shapes=[pltpu.VMEM((GLA_HEADS, GLA_DV, 128), F32)],
        compiler_params=_params("parallel", "arbitrary"),
        name="gla_bwd" if backward else "gla_fwd",
    )(qk, la, vt)


def _post_phases(xs, mods, g, yas, ybs, woa_ref, wob_ref, w1_ref, w2_ref, ff_chunk):
    n = len(xs)
    ys = [jnp.dot(yas[s], woa_ref[...], preferred_element_type=F32)
          + jnp.dot(ybs[s].astype(F32).T.astype(BF16), wob_ref[...], preferred_element_type=F32) for s in range(n)]
    x1s = [xs[s] + mods[s][2:3] * _rms(ys[s], g[1:2]) for s in range(n)]
    hs = [(_rms(x1s[s], g[2:3]) * (1.0 + mods[s][4:5]) + mods[s][3:4]).astype(BF16) for s in range(n)]
    d_ff = w1_ref.shape[1]
    accs = [jnp.zeros(xs[0].shape, F32) for _ in range(n)]
    for c in range(d_ff // ff_chunk):
        cols = slice(c * ff_chunk, (c + 1) * ff_chunk)
        for s in range(n):
            u = jnp.dot(hs[s], w1_ref[:, cols], preferred_element_type=F32)
            u = jnp.square(jnp.maximum(u, 0.0)).astype(BF16)
            accs[s] = accs[s] + jnp.dot(u, w2_ref[cols, :], preferred_element_type=F32)
    return [x1s[s] + mods[s][5:6] * _rms(accs[s], g[3:4]) for s in range(n)]


def _post_gla_kernel(xs_ref, of_ref, ob_ref, gg_ref, gain_ref, yb_ref, mods_ref, g_ref, woa_ref, wob_ref, w1_ref,
                     w2_ref, o_ref, *, ff_chunk):
    gain = gain_ref[...]
    yas = []
    for s in range(POST_PB):
        parts = []
        for hh in range(GLA_HEADS):
            rows = slice(hh * GLA_DV, (hh + 1) * GLA_DV)
            o = of_ref[s, 0, rows, :] + ob_ref[s, 0, rows, :]
            gt = gg_ref[s, 0, rows, :]
            y = o * lax.rsqrt(jnp.mean(o * o, axis=0, keepdims=True) + NORM_EPS) * gain
            parts.append(y * (gt / (1.0 + jnp.exp(-gt))))
        yas.append(jnp.concatenate(parts, axis=0).T.astype(BF16))
    outs = _post_phases([xs_ref[s] for s in range(POST_PB)], [mods_ref[s] for s in range(POST_PB)], g_ref[...], yas,
                        [yb_ref[s, 0] for s in range(POST_PB)], woa_ref, wob_ref, w1_ref, w2_ref, ff_chunk)
    for s in range(POST_PB):
        o_ref[s] = outs[s]


def _post_kernel(xs_ref, ya_ref, yb_ref, mods_ref, g_ref, woa_ref, wob_ref, w1_ref, w2_ref, o_ref, *, ff_chunk):
    outs = _post_phases([xs_ref[s] for s in range(POST_PB)], [mods_ref[s] for s in range(POST_PB)], g_ref[...],
                        [ya_ref[s].astype(BF16) for s in range(POST_PB)], [yb_ref[s, 0] for s in range(POST_PB)],
                        woa_ref, wob_ref, w1_ref, w2_ref, ff_chunk)
    for s in range(POST_PB):
        o_ref[s] = outs[s]


def _post(xs, ya, yb_t, mods_l, g, w_out, w1, w2, skip_ctx):
    b, t, d = xs.shape
    nt = t // TM
    pb = POST_PB
    half = w_out.shape[0] // 2
    woa = w_out[:half].astype(BF16)
    wob = w_out[half:].astype(BF16)
    w1 = w1.astype(BF16)
    w2 = w2.astype(BF16)
    off = 1 if skip_ctx else 0
    tok = lambda n: pl.BlockSpec((pb, TM, n), lambda bb, i: (bb, i + off, 0))
    tr = lambda n: pl.BlockSpec((pb, 1, n, TM), lambda bb, i: (bb, i + off, 0, 0))
    if isinstance(ya, tuple):
        o_f, o_b, gg, gain = ya
        body, name = _post_gla_kernel, "post_gla"
        a_specs = [tr(half), tr(half), tr(half), _const_spec((GLA_DV, 1))]
        a_args = [o_f, o_b, gg, gain.reshape(GLA_DV, 1)]
    else:
        body, name = _post_kernel, "post"
        a_specs, a_args = [tok(half)], [ya]
    return pl.pallas_call(
        functools.partial(body, ff_chunk=1024),
        grid=(b // pb, nt - off),
        in_specs=[tok(d)] + a_specs + [
                  tr(half),
                  pl.BlockSpec((pb, N_MOD, d), lambda bb, i: (_mod_block(bb, i + off, pb), 0, 0)),
                  _const_spec(g.shape),
                  _const_spec(woa.shape), _const_spec(wob.shape), _const_spec(w1.shape), _const_spec(w2.shape)],
        out_specs=pl.BlockSpec((pb, TM, d), lambda bb, i: (bb, i, 0)),
        out_shape=jax.ShapeDtypeStruct((b, t - off * TM, d), F32),
        compiler_params=_params("parallel", "parallel"),
        name=name,
    )(xs, *a_args, yb_t, mods_l, g, woa, wob, w1, w2)


def _dot3(a, b):
    ah, al = _split_bf16(a)
    bh, bl = _split_bf16(b)
    d = lambda x, y: jnp.dot(x, y, preferred_element_type=F32)
    return d(ah, bh) + (d(al, bh) + d(ah, bl))


def _filter_kernel(w1t_ref, w1c_ref, w1s_ref, b1_ref, w2_ref, b2_ref, w3_ref, delta_ref, o_ref, *, seq, rb):
    n = pl.program_id(0) * rb + lax.broadcasted_iota(jnp.int32, (rb, 1), 0)
    d = jnp.where(n < seq, n, 2 * seq - n).astype(F32)
    t = d / max(seq - 1, 1)
    w = (2 * math.pi / seq) * d
    band = lax.broadcasted_iota(jnp.int32, (1, HY_BANDS), 1).astype(F32)
    bands = 1e-4 + band * ((HY_BANDS - 1 - 1e-4) / (HY_BANDS - 1))
    ang = w * bands
    pre = t * w1t_ref[...] + _dot3(jnp.cos(ang), w1c_ref[...]) - _dot3(jnp.sin(ang), w1s_ref[...]) + b1_ref[...]
    hid = jnp.sin(pre)
    hid = jnp.sin(_dot3(hid, w2_ref[...]) + b2_ref[...])
    decay = jnp.where(n == seq, 0.0, jnp.exp(-t * delta_ref[...]))
    for o in range(HY_ORDER):
        o_ref[o] = (_dot3(hid, w3_ref[0, o]) * decay).astype(o_ref.dtype)


def _hyena_kernels(seq, f_w1, f_b1, f_w2, f_b2, f_w3):
    rb = min(512, seq)
    nblk = 2 * seq // rb
    ffn = f_w2.shape[0]
    w3 = f_w3.reshape(ffn, HY_ORDER, 2, HY_CH).transpose(2, 1, 0, 3)
    deltas = jnp.abs(jnp.linspace(math.log(HY_TARGET) / HY_FAST_PCT, math.log(HY_TARGET) / HY_SLOW_PCT,
                                  HY_CH, dtype=F32)).reshape(1, HY_CH)
    return pl.pallas_call(
        functools.partial(_filter_kernel, seq=seq, rb=rb),
        grid=(nblk,),
        in_specs=[_const_spec((1, ffn)), _const_spec((HY_BANDS, ffn)), _const_spec((HY_BANDS, ffn)),
                  _const_spec((1, ffn)), _const_spec((ffn, ffn)), _const_spec((1, ffn)),
                  pl.BlockSpec((1, HY_ORDER, ffn, HY_CH), lambda j: ((2 * j) // nblk, 0, 0, 0)),
                  _const_spec((1, HY_CH))],
        out_specs=pl.BlockSpec((HY_ORDER, rb, HY_CH), lambda j: (0, j, 0)),
        out_shape=jax.ShapeDtypeStruct((HY_ORDER, 2 * seq, HY_CH), BF16),
        compiler_params=_params("parallel"),
        name="hy_filter",
    )(f_w1[0:1], f_w1[1:1 + HY_BANDS], f_w1[1 + HY_BANDS:], f_b1.reshape(1, ffn), f_w2, f_b2.reshape(1, ffn), w3,
      deltas)


def _dft_tables(seq):
    n = 2 * seq
    r = math.isqrt(n)
    assert r * r == n
    r1 = (r // 2 + 1 + 7) // 8 * 8
    idx = jnp.arange(r, dtype=jnp.int32)
    phi = (2 * math.pi / r) * ((idx[:r1, None] * idx[None, :]) % r).astype(F32)
    f1 = jnp.concatenate([jnp.cos(phi), -jnp.sin(phi)], axis=0).astype(BF16)
    m = (idx[:r1, None, None] + r * idx[None, :, None]) * idx[None, None, :]
    th = (2 * math.pi / n) * (m % n).astype(F32)
    c, s = jnp.cos(th), jnp.sin(th)
    blockm = lambda re, im: jnp.concatenate([jnp.concatenate([re, -im], axis=-1),
                                             jnp.concatenate([im, re], axis=-1)], axis=-2)
    g = blockm(c, -s).astype(BF16)
    ginv = blockm(jnp.swapaxes(c, 1, 2), jnp.swapaxes(s, 1, 2)).astype(BF16)
    k1 = idx[:r1]
    weight = jnp.where((k1 == 0) | (k1 == r // 2), 1.0, jnp.where(k1 < r // 2, 2.0, 0.0)) / n
    phi_t = phi.T[:r // 2]
    finv = jnp.concatenate([jnp.cos(phi_t) * weight, -jnp.sin(phi_t) * weight], axis=1).astype(BF16)
    return r, f1, g, ginv, finv


DFT_BB = 32


def _dft1_kernel(f_ref, x_ref, o_ref, *, a0):
    f = f_ref[...]
    r = o_ref.shape[2]
    x = pltpu.einshape("abc->bac", x_ref[0, a0:].astype(F32))
    ys = [jnp.dot(f, x[j].astype(BF16), preferred_element_type=F32) for j in range(DFT_BB)]
    y = pltpu.einshape("jkc->kjc", jnp.stack(ys, axis=0))
    o_ref[0, 0] = y[:r].astype(o_ref.dtype)
    o_ref[0, 1] = y[r:].astype(o_ref.dtype)


def _dft_stage1(f, x4, a0=0):
    b, rows, r, c = x4.shape
    r1 = f.shape[0] // 2
    return pl.pallas_call(
        functools.partial(_dft1_kernel, a0=a0),
        grid=(b, r // DFT_BB),
        in_specs=[_const_spec(f.shape), pl.BlockSpec((1, rows, DFT_BB, c), lambda bi, j: (bi, 0, j, 0))],
        out_specs=pl.BlockSpec((1, 2, r1, DFT_BB, c), lambda bi, j: (bi, 0, 0, j, 0)),
        out_shape=jax.ShapeDtypeStruct((b, 2, r1, r, c), BF16),
        compiler_params=_params("parallel", "parallel"),
        name="hy_dft1",
    )(f, x4)


def _fspec_kernel(a_ref, g_ref, o_ref, *, kb, r):
    for kk in range(kb):
        a = a_ref[0, :, kk].reshape(2 * r, a_ref.shape[-1])
        o_ref[0, kk] = jnp.dot(g_ref[kk], a, preferred_element_type=F32)


def _filter_spectrum(a, g, kb):
    o, _, r1, r, c = a.shape
    return pl.pallas_call(
        functools.partial(_fspec_kernel, kb=kb, r=r),
        grid=(o, r1 // kb),
        in_specs=[pl.BlockSpec((1, 2, kb, r, c), lambda oo, j: (oo, 0, j, 0, 0)),
                  pl.BlockSpec((kb, 2 * r, 2 * r), lambda oo, j: (j, 0, 0))],
        out_specs=pl.BlockSpec((1, kb, 2 * r, c), lambda oo, j: (oo, j, 0, 0)),
        out_shape=jax.ShapeDtypeStruct((o, r1, 2 * r, c), F32),
        compiler_params=_params("parallel", "parallel"),
        name="hy_fspec",
    )(a, g)


def _dft2_kernel(a_ref, g_ref, gi_ref, k_ref, o_ref, *, kb, r):
    xs = [jnp.dot(g_ref[kk], a_ref[0, :, kk].reshape(2 * r, a_ref.shape[-1]), preferred_element_type=F32)
          for kk in range(kb)]
    ys = []
    for kk in range(kb):
        x, ks = xs[kk], k_ref[0, kk]
        xr, xi, kr, ki = x[:r], x[r:], ks[:r], ks[r:]
        ys.append(jnp.concatenate([xr * kr - xi * ki, xr * ki + xi * kr], axis=0).astype(BF16))
    bqs = [jnp.dot(gi_ref[kk], ys[kk], preferred_element_type=F32) for kk in range(kb)]
    for kk in range(kb):
        o_ref[0, 0, kk] = bqs[kk][:r].astype(o_ref.dtype)
        o_ref[0, 1, kk] = bqs[kk][r:].astype(o_ref.dtype)


def _dft_stage2(a, g, ginv, kspec, order, kb):
    b, _, r1, r, c = a.shape
    blk = pl.BlockSpec((1, 2, kb, r, c), lambda j, bb: (bb, 0, j, 0, 0))
    tab = pl.BlockSpec((kb, 2 * r, 2 * r), lambda j, bb: (j, 0, 0))
    return pl.pallas_call(
        functools.partial(_dft2_kernel, kb=kb, r=r),
        grid=(r1 // kb, b),
        in_specs=[blk, tab, tab, pl.BlockSpec((1, kb, 2 * r, c), lambda j, bb: (order, j, 0, 0))],
        out_specs=blk,
        out_shape=jax.ShapeDtypeStruct(a.shape, BF16),
        compiler_params=_params("parallel", "arbitrary"),
        name="hy_dft2",
    )(a, g, ginv, kspec)


def _dft3_kernel(f_ref, b_ref, gate_ref, z_ref, bias_ref, head_ref, o_ref, *, a0):
    f = f_ref[...]
    bq = jnp.concatenate([b_ref[0, 0], b_ref[0, 1]], axis=0)
    bq = pltpu.einshape("kjc->jkc", bq.astype(F32))
    ys = [jnp.dot(f, bq[j].astype(BF16), preferred_element_type=F32) for j in range(DFT_BB)]
    y = pltpu.einshape("jac->ajc", jnp.stack(ys, axis=0))
    o_ref[0, :a0] = head_ref[0]
    o_ref[0, a0:] = gate_ref[0, a0:] * (y + z_ref[0, a0:] * bias_ref[...])


def _dft_stage3(finv, bq5, gate4, z4, bias, head4):
    b, _, r1, r, c = bq5.shape
    rows = z4.shape[1]
    a0 = head4.shape[1]
    io = pl.BlockSpec((1, rows, DFT_BB, c), lambda bi, j: (bi, 0, j, 0))
    return pl.pallas_call(
        functools.partial(_dft3_kernel, a0=a0),
        grid=(b, r // DFT_BB),
        in_specs=[_const_spec(finv.shape), pl.BlockSpec((1, 2, r1, DFT_BB, c), lambda bi, j: (bi, 0, 0, j, 0)), io, io,
                  _const_spec((1, 1, c)), pl.BlockSpec((1, a0, DFT_BB, c), lambda bi, j: (bi, 0, j, 0))],
        out_specs=io,
        out_shape=jax.ShapeDtypeStruct((b, rows, r, c), F32),
        compiler_params=_params("parallel", "parallel"),
        name="hy_dft3",
    )(finv, bq5, gate4, z4, bias, head4)


def _long_conv_gated(v, gates, kern, hy_bias, ctx_out, lc):
    b, t, c = v.shape
    r, f1, g, ginv, finv = _dft_tables(t - lc)
    assert lc % r == 0
    a0 = lc // r
    kb = 8
    kspec = _filter_spectrum(_dft_stage1(f1, kern.reshape(HY_ORDER, r, r, c)), g, kb)
    as_blocks = lambda x: x.reshape(b, t // r, r, c)
    z = as_blocks(v)
    heads = [jnp.zeros((b, a0, r, c), F32), ctx_out.reshape(b, a0, r, c)]
    for o in range(HY_ORDER):
        a = _dft_stage1(f1[:, :r // 2], z, a0)
        bq = _dft_stage2(a, g, ginv, kspec, o, kb)
        z = _dft_stage3(finv, bq, as_blocks(gates[o]), z, hy_bias[o].reshape(1, 1, c), heads[o])
    return z.reshape(b, t, c)


def _ctx_conv_kernel(v_ref, x1_ref, x2_ref, k_ref, ff_ref, fi_ref, bias_ref, o_ref):
    n = ff_ref.shape[1]
    z = v_ref[0]
    for o, gate_ref in enumerate((x1_ref, x2_ref)):
        ks = jnp.dot(ff_ref[...], k_ref[o], preferred_element_type=F32)
        x = jnp.dot(ff_ref[:, :TM], z.astype(BF16), preferred_element_type=F32)
        xr, xi, kr, ki = x[:n], x[n:], ks[:n], ks[n:]
        y = jnp.concatenate([xr * kr - xi * ki, xr * ki + xi * kr], axis=0).astype(BF16)
        conv = jnp.dot(fi_ref[...], y, preferred_element_type=F32)
        z = gate_ref[0] * (conv + z * bias_ref[o:o + 1])
    o_ref[0] = z


def _ctx_conv_gated(v, x1, x2, kern, hy_bias):
    b, _, c = v.shape
    seq = TM
    n = 2 * seq
    idx = jnp.arange(n, dtype=jnp.int32)
    th = (2 * math.pi / n) * ((idx[:, None] * idx[None, :]) % n).astype(F32)
    ff = jnp.concatenate([jnp.cos(th), -jnp.sin(th)], axis=0).astype(BF16)
    fi = (jnp.concatenate([jnp.cos(th), -jnp.sin(th)], axis=1)[:seq] / n).astype(BF16)
    io = pl.BlockSpec((1, seq, c), lambda bb: (bb, 0, 0))
    return pl.pallas_call(
        _ctx_conv_kernel,
        grid=(b,),
        in_specs=[io, io, io, _const_spec(kern.shape), _const_spec(ff.shape), _const_spec(fi.shape),
                  _const_spec(hy_bias.shape)],
        out_specs=io,
        out_shape=jax.ShapeDtypeStruct((b, seq, c), F32),
        compiler_params=_params("parallel"),
        name="hy_ctx",
    )(v, x1, x2, kern, ff, fi, hy_bias)


def _hyena(v, x1, x2, lc, f_w1, f_b1, f_w2, f_b2, f_w3, hy_bias):
    seq = v.shape[1] - lc
    yc = _ctx_conv_gated(v, x1, x2, _hyena_kernels(lc, f_w1, f_b1, f_w2, f_b2, f_w3), hy_bias)
    return _long_conv_gated(v, (x1, x2), _hyena_kernels(seq, f_w1, f_b1, f_w2, f_b2, f_w3), hy_bias, yc, lc)


def _rope_tables_t(seq, lc):
    rows = seq // GRID_W
    r = jnp.broadcast_to(jnp.arange(rows, dtype=F32)[:, None], (rows, GRID_W)).reshape(-1)
    col = jnp.broadcast_to(jnp.arange(GRID_W, dtype=F32)[None, :], (rows, GRID_W)).reshape(-1)
    n_pairs = HEAD_DIM // 4
    inv = ROPE_THETA ** (-jnp.arange(n_pairs, dtype=F32) / n_pairs)
    ang = jnp.concatenate([r[:, None] * inv, col[:, None] * inv], axis=-1)
    cos = jnp.concatenate([jnp.ones((lc, 2 * n_pairs), F32), jnp.cos(ang)], axis=0)
    sin = jnp.concatenate([jnp.zeros((lc, 2 * n_pairs), F32), jnp.sin(ang)], axis=0)
    return cos.T, sin.T


def kernel(x, c, ctx, c_ctx, w_ada, b_ada, norm_g, w_mlp_in, w_mlp_out, ev_w_in, ev_w_lr, ev_b_lr, ev_gla_g,
           ev_qk_g, ev_w_out, od_w_in, od_conv_w, od_conv_b, od_f_w1, od_f_b1, od_f_w2, od_f_b2, od_f_w3,
           od_hy_bias, od_lam, od_diff_g, od_w_out):
    bsz, seq, d = x.shape
    lc = ctx.shape[1]
    depth = w_ada.shape[0]
    assert lc == TM and seq % TM == 0 and bsz <= MOD_CTX_ROW and bsz % POST_PB == 0 and bsz % IN_PB == 0
    cos_t, sin_t = _rope_tables_t(seq, lc)
    cc = jnp.zeros((8, d), F32).at[:bsz].set(c).at[MOD_CTX_ROW:MOD_CTX_ROW + max(POST_PB, IN_PB)].set(c_ctx)
    mods = _mods(cc, w_ada, b_ada).reshape(depth, 8, N_MOD, d)
    xs = jnp.concatenate([ctx, x], axis=1)
    for i in range(depth):
        last = i == depth - 1
        j = i // 2
        g = norm_g[i]
        if i % 2 == 0:
            qk, la, gv, gg, aq, ak, av, kn = _even_in(xs, mods[i], g[0:1], ev_w_in[j], ev_w_lr[j], ev_b_lr[j],
                                                  ev_qk_g[j], cos_t, sin_t)
            ya = (_gla_direction(qk, la, gv, False), _gla_direction(qk, la, gv, True), gg, ev_gla_g[j])
            yb = _gqa_attention(aq, ak, av, jnp.max(kn, axis=1))
            w_out = ev_w_out[j]
        else:
            hv, hx1, hx2, dq, dk, dv, kn = _odd_in(xs, mods[i], g[0:1], od_w_in[j], od_conv_w[j], od_conv_b[j],
                                                   cos_t, sin_t)
            ya = _hyena(hv, hx1, hx2, lc, od_f_w1[j], od_f_b1[j], od_f_w2[j], od_f_b2[j], od_f_w3[j], od_hy_bias[j])
            yb = _diff_attention(dq, dk, dv, jnp.max(kn, axis=1), od_lam[j], od_diff_g[j], i)
            w_out = od_w_out[j]
        xs = _post(xs, ya, yb, mods[i], g, w_out, w_mlp_in[i], w_mlp_out[i], skip_ctx=last)
    return xs
```

```python
import functools
import math

import jax
import jax.numpy as jnp
import numpy as np
from jax import lax
from jax.experimental import pallas as pl
from jax.experimental.pallas import tpu as pltpu

F32 = jnp.float32
BF16 = jnp.bfloat16

TM = 256
GRID_W = 64
HEAD_DIM = 64
ROPE_THETA = 10000.0
NORM_EPS = 1e-6
N_MOD = 6
VMEM_LIMIT = 56 * 1024 * 1024

GLA_HEADS = 4
GLA_DK = 64
GLA_DV = 128
GLA_RANK = 16
GLA_NORMALIZER = 16.0
GLA_CHUNK = 64
GLA_QK = GLA_HEADS * GLA_DK
GLA_V = GLA_HEADS * GLA_DV
GQA_HEADS = 8
GQA_KV_HEADS = 2
HY_CH = 512
HY_ORDER = 2
HY_BANDS = 16
HY_TARGET = 1e-2
HY_FAST_PCT = 0.3
HY_SLOW_PCT = 1.5
HY_COLS = (HY_ORDER + 1) * HY_CH
DIFF_HEADS = 4
DIFF_QK = DIFF_HEADS * 2 * HEAD_DIM
DIFF_V = DIFF_HEADS * 2 * HEAD_DIM
ONES_ROWS = 16
N_SCORE_HEADS = 8
LOG2E = 1.4426950408889634
Q_SCALE = HEAD_DIM ** -0.5 * LOG2E
GQA_VROWS = HEAD_DIM + ONES_ROWS
SCORE_BOUND_SLACK = 1.01
FAST_UNROLL_MAX = 16
SCORE_BOUND_LIMIT = 60.0
DIFF_VROWS = 2 * HEAD_DIM + ONES_ROWS


def _params(*sem):
    return pltpu.CompilerParams(dimension_semantics=sem, vmem_limit_bytes=VMEM_LIMIT)


def _const_spec(shape):
    nd = len(shape)
    return pl.BlockSpec(shape, lambda *_: (0,) * nd, pipeline_mode=pl.Buffered(1))


def _rms(x, g):
    return x * lax.rsqrt(jnp.mean(x * x, axis=-1, keepdims=True) + NORM_EPS) * g


MOD_CTX_ROW = 4
POST_PB = 2
IN_PB = 2
GLA_PB = 2


def _mod_block(bb, i, pb):
    return jnp.where(i == 0, MOD_CTX_ROW // pb, bb)


def _mods_kernel(cc_ref, w_ref, b_ref, o_ref):
    cc = cc_ref[...]
    s = cc / (1.0 + jnp.exp(-cc))
    o_ref[0] = jnp.dot(s.astype(BF16), w_ref[0].astype(BF16), preferred_element_type=F32) + b_ref[0]


def _mods(cc, w_ada, b_ada):
    depth, d, n = w_ada.shape
    nb = n // 4
    return pl.pallas_call(
        _mods_kernel,
        grid=(depth, n // nb),
        in_specs=[pl.BlockSpec((8, d), lambda l, j: (0, 0)),
                  pl.BlockSpec((1, d, nb), lambda l, j: (l, 0, j)),
                  pl.BlockSpec((1, 1, nb), lambda l, j: (l, 0, j))],
        out_specs=pl.BlockSpec((1, 8, nb), lambda l, j: (l, 0, j)),
        out_shape=jax.ShapeDtypeStruct((depth, 8, n), F32),
        compiler_params=_params("parallel", "parallel"),
        name="ada_mods",
    )(cc, w_ada, b_ada.reshape(depth, 1, n))


def _normed_input(x, mods, g):
    return (_rms(x, g) * (1.0 + mods[1:2]) + mods[0:1]).astype(BF16)


def _rope_rows(y, cos, sin):
    x1, x2 = y[:32], y[32:]
    return x1 * cos - x2 * sin, x1 * sin + x2 * cos


def _log_sigmoid(x):
    return jnp.minimum(x, 0.0) - jnp.log(1.0 + jnp.exp(-jnp.abs(x)))


def _nt(a, b):
    return lax.dot_general(a, b, (((1,), (1,)), ((), ())), preferred_element_type=F32)


def _max_sq_norm_row(parts):
    n2 = sum(jnp.sum(jnp.square(p.astype(BF16).astype(F32)), axis=0, keepdims=True) for p in parts)
    return jnp.broadcast_to(jnp.max(n2, axis=1, keepdims=True), (1, 128))


def _even_in_kernel(xs_ref, mods_ref, g_ref, wn_ref, wt_ref, wlr_ref, blr_ref, gq_ref, gk_ref,
                    cos_ref, sin_ref, qk_ref, la_ref, gv_ref, gg_ref, aq_ref, ak_ref, av_ref, kn_ref):
    hs = [_normed_input(xs_ref[s], mods_ref[s], g_ref[0:1]) for s in range(IN_PB)]
    ps = [jnp.dot(h, wn_ref[...], preferred_element_type=F32) for h in hs]
    pts = [_nt(wt_ref[...], h) for h in hs]
    cos, sin = cos_ref[...], sin_ref[...]
    gq, gk_gain = gq_ref[...], gk_ref[...]
    ones = jnp.ones((ONES_ROWS, TM), BF16)
    for s in range(IN_PB):
        p, pt_all = ps[s], pts[s]
        qk_ref[s, :, :GLA_QK] = p[:, :GLA_QK] * GLA_DK ** -0.5
        qk_ref[s, :, GLA_QK:] = p[:, GLA_QK:2 * GLA_QK]
        lr = p[:, 2 * GLA_QK:].astype(BF16)
        gk = jnp.dot(lr, wlr_ref[...], preferred_element_type=F32) + blr_ref[...]
        la_ref[s] = _log_sigmoid(gk) * (1.0 / GLA_NORMALIZER)
        gv_ref[s, 0] = pt_all[:GLA_V].astype(BF16)
        gg_ref[s, 0] = pt_all[GLA_V:2 * GLA_V]
        pt = pt_all[2 * GLA_V:]
        for hh in range(GQA_HEADS):
            blk = pt[hh * 64:(hh + 1) * 64]
            y = blk * lax.rsqrt(jnp.mean(blk * blk, axis=0, keepdims=True) + NORM_EPS) * gq
            o1, o2 = _rope_rows(y, cos, sin)
            aq_ref[s, 0, hh * 64:hh * 64 + 32, :] = (o1 * Q_SCALE).astype(BF16)
            aq_ref[s, 0, hh * 64 + 32:hh * 64 + 64, :] = (o2 * Q_SCALE).astype(BF16)
        kparts, knorms = [], []
        for hh in range(GQA_KV_HEADS):
            blk = pt[512 + hh * 64:512 + (hh + 1) * 64]
            y = blk * lax.rsqrt(jnp.mean(blk * blk, axis=0, keepdims=True) + NORM_EPS) * gk_gain
            o1, o2 = _rope_rows(y, cos, sin)
            kparts += [o1, o2]
            knorms.append(_max_sq_norm_row([o1, o2]))
        ak_ref[s] = jnp.concatenate(kparts, axis=0).T.astype(BF16)
        kn_ref[s, 0] = jnp.concatenate(knorms + [jnp.zeros((8 - GQA_KV_HEADS, 128), F32)], axis=0)
        for hh in range(GQA_KV_HEADS):
            av_ref[s, 0, hh * GQA_VROWS:hh * GQA_VROWS + 64, :] = pt[640 + hh * 64:640 + (hh + 1) * 64].astype(BF16)
            av_ref[s, 0, hh * GQA_VROWS + 64:(hh + 1) * GQA_VROWS, :] = ones


def _odd_in_kernel(xs_ref, prev_ref, next_ref, mods_ref, g_ref, wn_ref, wt_ref, cw_ref, cb_ref, cos_ref, sin_ref,
                   v_ref, x1_ref, x2_ref, dq_ref, dk_ref, dv_ref, kn_ref, *, n_tiles):
    i = pl.program_id(1)
    hs = [_normed_input(xs_ref[s], mods_ref[s], g_ref[0:1]) for s in range(IN_PB)]
    h_ext = [jnp.concatenate([hs[s], _normed_input(prev_ref[s], mods_ref[s], g_ref[0:1]),
                              _normed_input(next_ref[s], mods_ref[s], g_ref[0:1])], axis=0) for s in range(IN_PB)]
    us = [jnp.dot(h, wn_ref[...], preferred_element_type=F32) for h in h_ext]
    pts = [_nt(wt_ref[...], h) for h in hs]
    cos, sin = cos_ref[...], sin_ref[...]
    ones = jnp.ones((ONES_ROWS, TM), BF16)
    nv = 2 * HEAD_DIM
    cw = cw_ref[...]
    row_id = lax.broadcasted_iota(jnp.int32, (TM, HY_COLS), 0)
    no_prev = (i == 0) | (i == 1)
    no_next = (i == 0) | (i == n_tiles - 1)
    for s in range(IN_PB):
        pt = pts[s]
        u = us[s][:TM]
        prev_row = jnp.where(no_prev, 0.0, us[s][TM + 7:TM + 8])
        next_row = jnp.where(no_next, 0.0, us[s][TM + 8:TM + 9])
        up = jnp.where(row_id == 0, prev_row, pltpu.roll(u, 1, axis=0))
        dn = jnp.where(row_id == TM - 1, next_row, pltpu.roll(u, TM - 1, axis=0))
        conv = up * cw[0:1] + u * cw[1:2] + dn * cw[2:3] + cb_ref[...]
        v_ref[s] = conv[:, :HY_CH]
        x1_ref[s] = conv[:, HY_CH:2 * HY_CH]
        x2_ref[s] = conv[:, 2 * HY_CH:]
        kparts, knorms = [], []
        for hh in range(2 * DIFF_HEADS):
            o1, o2 = _rope_rows(pt[hh * 64:(hh + 1) * 64], cos, sin)
            dq_ref[s, 0, hh * 64:hh * 64 + 32, :] = (o1 * Q_SCALE).astype(BF16)
            dq_ref[s, 0, hh * 64 + 32:hh * 64 + 64, :] = (o2 * Q_SCALE).astype(BF16)
            k1, k2 = _rope_rows(pt[DIFF_QK + hh * 64:DIFF_QK + (hh + 1) * 64], cos, sin)
            kparts += [k1, k2]
            knorms.append(_max_sq_norm_row([k1, k2]))
        dk_ref[s] = jnp.concatenate(kparts, axis=0).T.astype(BF16)
        kn_ref[s, 0] = jnp.concatenate(knorms, axis=0)
        for hh in range(DIFF_HEADS):
            rows = pt[2 * DIFF_QK + hh * nv:2 * DIFF_QK + (hh + 1) * nv]
            dv_ref[s, 0, hh * DIFF_VROWS:hh * DIFF_VROWS + nv, :] = rows.astype(BF16)
            dv_ref[s, 0, hh * DIFF_VROWS + nv:(hh + 1) * DIFF_VROWS, :] = ones


def _pair_perm(n_heads):
    base = np.concatenate([np.arange(0, HEAD_DIM, 2), np.arange(1, HEAD_DIM, 2)])
    return np.concatenate([h * HEAD_DIM + base for h in range(n_heads)])


def _even_in(xs, mods_l, g0, w_in, w_lr, b_lr, qk_g, cos_t, sin_t):
    b, t, d = xs.shape
    nt = t // TM
    c_v = 2 * GLA_QK
    c_g = c_v + GLA_V
    c_lr0 = c_g + GLA_V
    c_lr = c_lr0 + 2 * GLA_RANK
    wn = jnp.concatenate([w_in[:, :c_v], w_in[:, c_lr0:c_lr], jnp.zeros((d, 128 - 2 * GLA_RANK), F32)],
                         axis=1).astype(BF16)
    cq = c_lr + GQA_HEADS * HEAD_DIM
    ck = cq + GQA_KV_HEADS * HEAD_DIM
    wq = w_in[:, c_lr:cq][:, _pair_perm(GQA_HEADS)]
    wk = w_in[:, cq:ck][:, _pair_perm(GQA_KV_HEADS)]
    wt = jnp.concatenate([w_in[:, c_v:c_lr0], wq, wk, w_in[:, ck:]], axis=1).T.astype(BF16)
    wlr = jnp.zeros((128, 2 * GLA_QK), F32)
    wlr = wlr.at[:GLA_RANK, :GLA_QK].set(w_lr[0]).at[GLA_RANK:2 * GLA_RANK, GLA_QK:].set(w_lr[1]).astype(BF16)
    blr = b_lr.reshape(1, 2 * GLA_QK)
    perm = _pair_perm(1)
    gq = qk_g[0][perm].reshape(HEAD_DIM, 1)
    gk = qk_g[1][perm].reshape(HEAD_DIM, 1)
    tok = lambda n: pl.BlockSpec((IN_PB, TM, n), lambda bb, i: (bb, i, 0))
    tr = lambda n: pl.BlockSpec((IN_PB, 1, n, TM), lambda bb, i: (bb, i, 0, 0))
    return pl.pallas_call(
        _even_in_kernel,
        grid=(b // IN_PB, nt),
        in_specs=[tok(d),
                  pl.BlockSpec((IN_PB, N_MOD, d), lambda bb, i: (_mod_block(bb, i, IN_PB), 0, 0)),
                  _const_spec((1, d)),
                  _const_spec(wn.shape), _const_spec(wt.shape), _const_spec(wlr.shape), _const_spec(blr.shape),
                  _const_spec(gq.shape), _const_spec(gk.shape),
                  pl.BlockSpec((32, TM), lambda bb, i: (0, i)),
                  pl.BlockSpec((32, TM), lambda bb, i: (0, i))],
        out_specs=[tok(2 * GLA_QK), tok(2 * GLA_QK), tr(GLA_V), tr(GLA_V), tr(512), tok(128),
                   tr(GQA_KV_HEADS * GQA_VROWS), pl.BlockSpec((IN_PB, 1, 8, 128), lambda bb, i: (bb, i, 0, 0))],
        out_shape=[jax.ShapeDtypeStruct((b, t, 2 * GLA_QK), F32),
                   jax.ShapeDtypeStruct((b, t, 2 * GLA_QK), F32),
                   jax.ShapeDtypeStruct((b, nt, GLA_V, TM), BF16),
                   jax.ShapeDtypeStruct((b, nt, GLA_V, TM), F32),
                   jax.ShapeDtypeStruct((b, nt, 512, TM), BF16),
                   jax.ShapeDtypeStruct((b, t, 128), BF16),
                   jax.ShapeDtypeStruct((b, nt, GQA_KV_HEADS * GQA_VROWS, TM), BF16),
                   jax.ShapeDtypeStruct((b, nt, 8, 128), F32)],
        compiler_params=_params("parallel", "parallel"),
        name="even_in",
    )(xs, mods_l, g0, wn, wt, wlr, blr, gq, gk, cos_t, sin_t)


def _odd_in(xs, mods_l, g0, w_in, conv_w, conv_b, cos_t, sin_t):
    b, t, d = xs.shape
    nt = t // TM
    r8 = TM // 8
    last_blk = t // 8 - 1
    wn = w_in[:, :HY_COLS].astype(BF16)
    perm = _pair_perm(2 * DIFF_HEADS)
    wq = w_in[:, HY_COLS:HY_COLS + DIFF_QK][:, perm]
    wk = w_in[:, HY_COLS + DIFF_QK:HY_COLS + 2 * DIFF_QK][:, perm]
    wt = jnp.concatenate([wq, wk, w_in[:, HY_COLS + 2 * DIFF_QK:]], axis=1).T.astype(BF16)
    tok = lambda n: pl.BlockSpec((IN_PB, TM, n), lambda bb, i: (bb, i, 0))
    tr = lambda n: pl.BlockSpec((IN_PB, 1, n, TM), lambda bb, i: (bb, i, 0, 0))
    hy = jax.ShapeDtypeStruct((b, t, HY_CH), F32)
    return pl.pallas_call(
        functools.partial(_odd_in_kernel, n_tiles=nt),
        grid=(b // IN_PB, nt),
        in_specs=[tok(d),
                  pl.BlockSpec((IN_PB, 8, d), lambda bb, i: (bb, jnp.maximum(i * r8 - 1, 0), 0)),
                  pl.BlockSpec((IN_PB, 8, d), lambda bb, i: (bb, jnp.minimum((i + 1) * r8, last_blk), 0)),
                  pl.BlockSpec((IN_PB, N_MOD, d), lambda bb, i: (_mod_block(bb, i, IN_PB), 0, 0)),
                  _const_spec((1, d)),
                  _const_spec(wn.shape), _const_spec(wt.shape),
                  _const_spec(conv_w.shape), _const_spec((1, HY_COLS)),
                  pl.BlockSpec((32, TM), lambda bb, i: (0, i)),
                  pl.BlockSpec((32, TM), lambda bb, i: (0, i))],
        out_specs=[tok(HY_CH), tok(HY_CH), tok(HY_CH), tr(DIFF_QK), tok(DIFF_QK), tr(DIFF_HEADS * DIFF_VROWS),
                   pl.BlockSpec((IN_PB, 1, 8, 128), lambda bb, i: (bb, i, 0, 0))],
        out_shape=[hy, hy, hy,
                   jax.ShapeDtypeStruct((b, nt, DIFF_QK, TM), BF16),
                   jax.ShapeDtypeStruct((b, t, DIFF_QK), BF16),
                   jax.ShapeDtypeStruct((b, nt, DIFF_HEADS * DIFF_VROWS, TM), BF16),
                   jax.ShapeDtypeStruct((b, nt, 8, 128), F32)],
        compiler_params=_params("parallel", "parallel"),
        name="odd_in",
    )(xs, xs, xs, mods_l, g0, wn, wt, conv_w, conv_b.reshape(1, HY_COLS), cos_t, sin_t)


def _attn_core(q_ref, k_ref, v_ref, kmax_ref, scratch, *, n_kv, k_col, k_row, q_slot, v_row, v_rows, scores_between,
               vpu_denominator):
    qpad_ref, bound_ref, m_ref, acc_ref, s0_ref, s1_ref, c0_ref, c1_ref, p0_ref, p1_ref = scratch
    s_refs, c_refs, p_refs = (s0_ref, s1_ref), (c0_ref, c1_ref), (p0_ref, p1_ref)
    i = pl.program_id(1)
    zeros = jnp.zeros((HEAD_DIM, TM), BF16)
    bound_max = jnp.zeros((1, TM), F32)
    for hh in range(N_SCORE_HEADS):
        qh = q_ref[0, 0, hh * HEAD_DIM:(hh + 1) * HEAD_DIM, :]
        qpad_ref[hh] = jnp.concatenate([qh, zeros] if q_slot(hh) == 0 else [zeros, qh], axis=0)
        qf = qh.astype(F32)
        q_norm2 = jnp.sum(qf * qf, axis=0, keepdims=True)
        k_norm2 = kmax_ref[0, k_row(hh):k_row(hh) + 1, 0:1]
        bound = jnp.sqrt(q_norm2 * k_norm2) * SCORE_BOUND_SLACK
        bound_ref[hh] = bound
        bound_max = jnp.maximum(bound_max, bound)
    acc_ref[...] = jnp.zeros(acc_ref.shape, F32)
    n = jnp.where(i == 0, 1, n_kv)
    fixed_reference = jnp.max(bound_max) <= SCORE_BOUND_LIMIT

    def produce(j, hh, slot):
        kc = k_ref[0, pl.ds(pl.multiple_of(j * TM, TM), TM), k_col(hh):k_col(hh) + 128]
        s = jnp.dot(kc, qpad_ref[hh], preferred_element_type=F32)
        p = jnp.exp2(s - bound_ref[hh])
        if vpu_denominator:
            m_ref[hh] = m_ref[hh] + jnp.sum(p, axis=0, keepdims=True)
        p_refs[slot][hh] = p.astype(BF16)

    n_val = v_rows - ONES_ROWS if vpu_denominator else v_rows

    def reduce(j, hh, slot):
        vc = v_ref[0, j, v_row(hh):v_row(hh) + n_val, :]
        acc_ref[hh, :n_val] = acc_ref[hh, :n_val] + jnp.dot(vc, p_refs[slot][hh], preferred_element_type=F32)

    unroll = math.gcd(n_kv - 1, FAST_UNROLL_MAX)

    @pl.when(fixed_reference)
    def _():
        m_ref[...] = jnp.zeros(m_ref.shape, F32)
        for hh in range(N_SCORE_HEADS):
            produce(0, hh, 0)

        def fast_step(j, slot):
            for hh in range(N_SCORE_HEADS):
                produce(j + 1, hh, 1 - slot)
                reduce(j, hh, slot)

        def fast_body(jj, carry):
            for u in range(unroll):
                fast_step(unroll * jj + u, u % 2)
            return carry

        lax.fori_loop(0, (n - 1) // unroll, fast_body, 0)
        for hh in range(N_SCORE_HEADS):
            reduce(n - 1, hh, 0)
            if vpu_denominator:
                acc_ref[hh, n_val:, :] = jnp.broadcast_to(m_ref[hh], (ONES_ROWS, TM))

    @pl.when(jnp.logical_not(fixed_reference))
    def _():
        _attn_running_max(k_ref, v_ref, qpad_ref, m_ref, acc_ref, s_refs, c_refs, n=n, k_col=k_col, v_row=v_row,
                          v_rows=v_rows, scores_between=scores_between)


def _attn_running_max(k_ref, v_ref, qpad_ref, m_ref, acc_ref, s_refs, c_refs, *, n, k_col, v_row, v_rows,
                      scores_between):
    m_ref[...] = jnp.full(m_ref.shape, -jnp.inf, F32)

    def scores(j, hh, slot):
        kc = k_ref[0, pl.ds(pl.multiple_of(j * TM, TM), TM), k_col(hh):k_col(hh) + 128]
        s = jnp.dot(kc, qpad_ref[hh], preferred_element_type=F32)
        s_refs[slot][hh] = s
        c_refs[slot][hh] = jnp.max(s, axis=0, keepdims=True)

    def probs(hh, slot):
        m_old = m_ref[hh]
        m_new = jnp.maximum(m_old, c_refs[slot][hh])
        return m_old, m_new, jnp.exp2(s_refs[slot][hh] - m_new).astype(BF16)

    def accumulate(j, hh, state):
        m_old, m_new, p = state
        vc = v_ref[0, j, v_row(hh):v_row(hh) + v_rows, :]
        alpha = jnp.exp2(m_old - m_new)
        acc_ref[hh] = acc_ref[hh] * alpha + jnp.dot(vc, p, preferred_element_type=F32)
        m_ref[hh] = m_new

    for hh in range(N_SCORE_HEADS):
        scores(0, hh, 0)

    def step(j, slot):
        for hh in range(N_SCORE_HEADS):
            state = probs(hh, slot)
            if scores_between:
                scores(j + 1, hh, 1 - slot)
            accumulate(j, hh, state)
            if not scores_between:
                scores(j + 1, hh, 1 - slot)

    def body(jj, carry):
        step(2 * jj, 0)
        step(2 * jj + 1, 1)
        return carry

    lax.fori_loop(0, (n - 1) // 2, body, 0)
    for hh in range(N_SCORE_HEADS):
        accumulate(n - 1, hh, probs(hh, 0))


def _attn_scratch(v_rows):
    stat = pltpu.VMEM((N_SCORE_HEADS, 1, TM), F32)
    return [pltpu.VMEM((N_SCORE_HEADS, 128, TM), BF16),
            stat, stat,
            pltpu.VMEM((N_SCORE_HEADS, v_rows, TM), F32),
            pltpu.VMEM((N_SCORE_HEADS, TM, TM), F32), pltpu.VMEM((N_SCORE_HEADS, TM, TM), F32),
            stat, stat,
            pltpu.VMEM((N_SCORE_HEADS, TM, TM), BF16), pltpu.VMEM((N_SCORE_HEADS, TM, TM), BF16)]


def _gqa_kernel(q_ref, k_ref, v_ref, kmax_ref, o_ref, *scratch, n_kv):
    group = GQA_HEADS // GQA_KV_HEADS
    _attn_core(q_ref, k_ref, v_ref, kmax_ref, scratch, n_kv=n_kv,
               k_col=lambda h: 0, k_row=lambda h: h // group, q_slot=lambda h: h // group,
               v_row=lambda h: (h // group) * GQA_VROWS, v_rows=GQA_VROWS, scores_between=True,
               vpu_denominator=False)
    acc_ref = scratch[3]
    for hh in range(GQA_HEADS):
        a = acc_ref[hh]
        o_ref[0, 0, hh * HEAD_DIM:(hh + 1) * HEAD_DIM, :] = (a[:HEAD_DIM] / a[HEAD_DIM:HEAD_DIM + 1]).astype(o_ref.dtype)


def _gqa_attention(aq, ak, av, kmax):
    b, nt, nq, _ = aq.shape
    t = nt * TM
    assert nt % 2 == 1 and nt > 1
    return pl.pallas_call(
        functools.partial(_gqa_kernel, n_kv=nt),
        grid=(b, nt),
        in_specs=[pl.BlockSpec((1, 1, nq, TM), lambda bb, i: (bb, i, 0, 0)),
                  pl.BlockSpec((1, t, 128), lambda bb, i: (bb, 0, 0)),
                  pl.BlockSpec((1, nt, GQA_KV_HEADS * GQA_VROWS, TM), lambda bb, i: (bb, 0, 0, 0)),
                  pl.BlockSpec((1, 8, 128), lambda bb, i: (bb, 0, 0))],
        out_specs=pl.BlockSpec((1, 1, nq, TM), lambda bb, i: (bb, i, 0, 0)),
        out_shape=jax.ShapeDtypeStruct((b, nt, nq, TM), BF16),
        scratch_shapes=_attn_scratch(GQA_VROWS),
        compiler_params=_params("parallel", "arbitrary"),
        name="gqa_attn",
    )(aq, ak, av, kmax)


def _diff_kernel(q_ref, k_ref, v_ref, kmax_ref, lam_ref, gain_ref, o_ref, *scratch, n_kv, lam_init):
    _attn_core(q_ref, k_ref, v_ref, kmax_ref, scratch, n_kv=n_kv,
               k_col=lambda hc: (hc // 2) * 128, k_row=lambda hc: hc, q_slot=lambda hc: hc % 2,
               v_row=lambda hc: (hc // 2) * DIFF_VROWS, v_rows=DIFF_VROWS, scores_between=False,
               vpu_denominator=True)
    acc_ref = scratch[3]
    lp = lam_ref[...]
    lam = (jnp.exp(jnp.sum(lp[0:1] * lp[1:2], axis=1, keepdims=True))
           - jnp.exp(jnp.sum(lp[2:3] * lp[3:4], axis=1, keepdims=True)) + lam_init)
    gain = gain_ref[...]
    nv = 2 * HEAD_DIM
    for hh in range(DIFF_HEADS):
        a0, a1 = acc_ref[2 * hh], acc_ref[2 * hh + 1]
        o = a0[:nv] / a0[nv:nv + 1] - lam * (a1[:nv] / a1[nv:nv + 1])
        y = o * lax.rsqrt(jnp.mean(o * o, axis=0, keepdims=True) + NORM_EPS) * gain
        o_ref[0, 0, hh * nv:(hh + 1) * nv, :] = (y * (1.0 - lam_init)).astype(o_ref.dtype)


def _diff_attention(dq, dk, dv, kmax, lam_p, gain, layer_idx):
    b, nt, nq, _ = dq.shape
    t = nt * TM
    assert nt % 2 == 1 and nt > 1
    lam_init = 0.8 - 0.6 * math.exp(-0.3 * layer_idx)
    return pl.pallas_call(
        functools.partial(_diff_kernel, n_kv=nt, lam_init=lam_init),
        grid=(b, nt),
        in_specs=[pl.BlockSpec((1, 1, nq, TM), lambda bb, i: (bb, i, 0, 0)),
                  pl.BlockSpec((1, t, DIFF_QK), lambda bb, i: (bb, 0, 0)),
                  pl.BlockSpec((1, nt, DIFF_HEADS * DIFF_VROWS, TM), lambda bb, i: (bb, 0, 0, 0)),
                  pl.BlockSpec((1, 8, 128), lambda bb, i: (bb, 0, 0)),
                  _const_spec(lam_p.shape),
                  _const_spec((2 * HEAD_DIM, 1))],
        out_specs=pl.BlockSpec((1, 1, DIFF_V, TM), lambda bb, i: (bb, i, 0, 0)),
        out_shape=jax.ShapeDtypeStruct((b, nt, DIFF_V, TM), BF16),
        scratch_shapes=_attn_scratch(DIFF_VROWS),
        compiler_params=_params("parallel", "arbitrary"),
        name="diff_attn",
    )(dq, dk, dv, kmax, lam_p, gain.reshape(2 * HEAD_DIM, 1))


def _split_bf16(x):
    hi = x.astype(BF16)
    return hi, (x - hi.astype(F32)).astype(BF16)


def _gla_kernel(qk_ref, la_ref, vt_ref, o_ref, st_ref, *, backward):
    @pl.when(pl.program_id(1) == 0)
    def _():
        st_ref[...] = jnp.zeros(st_ref.shape, F32)

    n_chunk = TM // GLA_CHUNK
    r = lax.broadcasted_iota(jnp.int32, (TM, TM), 0)
    c = lax.broadcasted_iota(jnp.int32, (TM, TM), 1)
    same = (r // GLA_CHUNK) == (c // GLA_CHUNK)
    causal = (c >= r) if backward else (c <= r)
    tri = jnp.where(same & causal, 1.0, 0.0).astype(BF16)
    blk = jnp.where(same, 1.0, 0.0).astype(BF16)
    lane = lax.broadcasted_iota(jnp.int32, (TM, 128), 1)
    row = lax.broadcasted_iota(jnp.int32, (TM, 128), 0)
    keep = same & ((r >= c) if backward else (r <= c))
    order = range(n_chunk - 1, -1, -1) if backward else range(n_chunk)
    units = [(s, h) for s in range(GLA_PB) for h in range(GLA_HEADS)]
    pair = lambda h: slice((h // 2) * 128, (h // 2 + 1) * 128)
    q_in, k_in, k_end, dec = [], [], [], []
    for s in range(GLA_PB):
        la_hi, la_lo = _split_bf16(la_ref[s])
        cum = jnp.dot(tri, la_hi, preferred_element_type=F32) + jnp.dot(tri, la_lo, preferred_element_type=F32)
        tot = jnp.dot(blk, la_hi, preferred_element_type=F32) + jnp.dot(blk, la_lo, preferred_element_type=F32)
        q = qk_ref[s, :, :GLA_QK]
        k = qk_ref[s, :, GLA_QK:]
        q_in.append((q * jnp.exp(cum)).astype(BF16))
        k_in.append(k * jnp.exp(-cum))
        k_end.append(k * jnp.exp(tot - cum))
        dec.append(jnp.exp(tot))
    head_lanes = {h: (lane // GLA_DK) == (h % 2) for h in range(GLA_HEADS)}
    q_pair = {(s, h): q_in[s][:, pair(h)] for s, h in units}
    k_in_h = {(s, h): jnp.where(head_lanes[h], k_in[s][:, pair(h)], 0.0).astype(BF16) for s, h in units}
    k_end_h = {(s, h): jnp.where(head_lanes[h], k_end[s][:, pair(h)], 0.0) for s, h in units}
    vts = {(s, h): vt_ref[s, 0, h * GLA_DV:(h + 1) * GLA_DV, :] for s, h in units}
    a_ts = {u: jnp.where(keep, _nt(k_in_h[u], q_pair[u]), 0.0).astype(BF16) for u in units}
    o_ts = {u: jnp.dot(vts[u], a_ts[u], preferred_element_type=F32) for u in units}
    s_ts = {(s, h): st_ref[s, h] for s, h in units}
    for ci in order:
        in_chunk = (row // GLA_CHUNK) == ci
        for u in units:
            s, h = u
            q_c = jnp.where(in_chunk, q_pair[u], jnp.zeros_like(q_pair[u]))
            o_ts[u] = o_ts[u] + _nt(s_ts[u].astype(BF16), q_c)
            k_c = jnp.where(in_chunk, k_end_h[u], 0.0).astype(BF16)
            s_ts[u] = (s_ts[u] * dec[s][ci * GLA_CHUNK:ci * GLA_CHUNK + 1, pair(h)]
                       + jnp.dot(vts[u], k_c, preferred_element_type=F32))
    for s, h in units:
        st_ref[s, h] = s_ts[(s, h)]
        o_ref[s, 0, h * GLA_DV:(h + 1) * GLA_DV, :] = o_ts[(s, h)]


def _gla_direction(qk, la, vt, backward):
    b, t, _ = qk.shape
    nt = t // TM
    d = 1 if backward else 0
    tile = (lambda i: jnp.where(i == 0, 0, nt - i)) if backward else (lambda i: i)
    return pl.pallas_call(
        functools.partial(_gla_kernel, backward=backward),
        grid=(b // GLA_PB, nt),
        in_specs=[pl.BlockSpec((GLA_PB, TM, 2 * GLA_QK), lambda bb, i: (bb, tile(i), 0)),
                  pl.BlockSpec((GLA_PB, TM, GLA_QK), lambda bb, i: (bb, tile(i), d)),
                  pl.BlockSpec((GLA_PB, 1, GLA_V, TM), lambda bb, i: (bb, tile(i), 0, 0))],
        out_specs=pl.BlockSpec((GLA_PB, 1, GLA_V, TM), lambda bb, i: (bb, tile(i), 0, 0)),
        out_shape=jax.ShapeDtypeStruct((b, nt, GLA_V, TM), F32),
        scratch_shapes=[pltpu.VMEM((GLA_PB, GLA_HEADS, GLA_DV, 128), F32)],
        compiler_params=_params("parallel", "arbitrary"),
        name="gla_bwd" if backward else "gla_fwd",
    )(qk, la, vt)


def _post_phases(xs, mods, g, yas, ybs, woa_ref, wob_ref, w1_ref, w2_ref, ff_chunk):
    n = len(xs)
    ys = [jnp.dot(yas[s], woa_ref[...], preferred_element_type=F32)
          + jnp.dot(ybs[s].astype(F32).T.astype(BF16), wob_ref[...], preferred_element_type=F32) for s in range(n)]
    x1s = [xs[s] + mods[s][2:3] * _rms(ys[s], g[1:2]) for s in range(n)]
    hs = [(_rms(x1s[s], g[2:3]) * (1.0 + mods[s][4:5]) + mods[s][3:4]).astype(BF16) for s in range(n)]
    d_ff = w1_ref.shape[1]
    accs = [jnp.zeros(xs[0].shape, F32) for _ in range(n)]
    for c in range(d_ff // ff_chunk):
        cols = slice(c * ff_chunk, (c + 1) * ff_chunk)
        for s in range(n):
            u = jnp.dot(hs[s], w1_ref[:, cols], preferred_element_type=F32)
            u = jnp.square(jnp.maximum(u, 0.0)).astype(BF16)
            accs[s] = accs[s] + jnp.dot(u, w2_ref[cols, :], preferred_element_type=F32)
    return [x1s[s] + mods[s][5:6] * _rms(accs[s], g[3:4]) for s in range(n)]


def _post_gla_kernel(xs_ref, of_ref, ob_ref, gg_ref, gain_ref, yb_ref, mods_ref, g_ref, woa_ref, wob_ref, w1_ref,
                     w2_ref, o_ref, *, ff_chunk):
    gain = gain_ref[...]
    yas = []
    for s in range(POST_PB):
        parts = []
        for hh in range(GLA_HEADS):
            rows = slice(hh * GLA_DV, (hh + 1) * GLA_DV)
            o = of_ref[s, 0, rows, :] + ob_ref[s, 0, rows, :]
            gt = gg_ref[s, 0, rows, :]
            y = o * lax.rsqrt(jnp.mean(o * o, axis=0, keepdims=True) + NORM_EPS) * gain
            parts.append(y * (gt / (1.0 + jnp.exp(-gt))))
        yas.append(jnp.concatenate(parts, axis=0).T.astype(BF16))
    outs = _post_phases([xs_ref[s] for s in range(POST_PB)], [mods_ref[s] for s in range(POST_PB)], g_ref[...], yas,
                        [yb_ref[s, 0] for s in range(POST_PB)], woa_ref, wob_ref, w1_ref, w2_ref, ff_chunk)
    for s in range(POST_PB):
        o_ref[s] = outs[s]


def _post_kernel(xs_ref, ya_ref, yb_ref, mods_ref, g_ref, woa_ref, wob_ref, w1_ref, w2_ref, o_ref, *, ff_chunk):
    outs = _post_phases([xs_ref[s] for s in range(POST_PB)], [mods_ref[s] for s in range(POST_PB)], g_ref[...],
                        [ya_ref[s].astype(BF16) for s in range(POST_PB)], [yb_ref[s, 0] for s in range(POST_PB)],
                        woa_ref, wob_ref, w1_ref, w2_ref, ff_chunk)
    for s in range(POST_PB):
        o_ref[s] = outs[s]


def _post(xs, ya, yb_t, mods_l, g, w_out, w1, w2, skip_ctx):
    b, t, d = xs.shape
    nt = t // TM
    pb = POST_PB
    half = w_out.shape[0] // 2
    woa = w_out[:half].astype(BF16)
    wob = w_out[half:].astype(BF16)
    w1 = w1.astype(BF16)
    w2 = w2.astype(BF16)
    off = 1 if skip_ctx else 0
    tok = lambda n: pl.BlockSpec((pb, TM, n), lambda bb, i: (bb, i + off, 0))
    tr = lambda n: pl.BlockSpec((pb, 1, n, TM), lambda bb, i: (bb, i + off, 0, 0))
    if isinstance(ya, tuple):
        o_f, o_b, gg, gain = ya
        body, name = _post_gla_kernel, "post_gla"
        a_specs = [tr(half), tr(half), tr(half), _const_spec((GLA_DV, 1))]
        a_args = [o_f, o_b, gg, gain.reshape(GLA_DV, 1)]
    else:
        body, name = _post_kernel, "post"
        a_specs, a_args = [tok(half)], [ya]
    return pl.pallas_call(
        functools.partial(body, ff_chunk=1024),
        grid=(b // pb, nt - off),
        in_specs=[tok(d)] + a_specs + [
                  tr(half),
                  pl.BlockSpec((pb, N_MOD, d), lambda bb, i: (_mod_block(bb, i + off, pb), 0, 0)),
                  _const_spec(g.shape),
                  _const_spec(woa.shape), _const_spec(wob.shape), _const_spec(w1.shape), _const_spec(w2.shape)],
        out_specs=pl.BlockSpec((pb, TM, d), lambda bb, i: (bb, i, 0)),
        out_shape=jax.ShapeDtypeStruct((b, t - off * TM, d), F32),
        compiler_params=_params("parallel", "parallel"),
        name=name,
    )(xs, *a_args, yb_t, mods_l, g, woa, wob, w1, w2)


def _dot3(a, b):
    ah, al = _split_bf16(a)
    bh, bl = _split_bf16(b)
    d = lambda x, y: jnp.dot(x, y, preferred_element_type=F32)
    return d(ah, bh) + (d(al, bh) + d(ah, bl))


def _filter_kernel(w1t_ref, w1c_ref, w1s_ref, b1_ref, w2_ref, b2_ref, w3_ref, delta_ref, o_ref, *, seq, rb):
    tap = pl.program_id(0) * rb + lax.broadcasted_iota(jnp.int32, (rb, 1), 0)
    d = tap.astype(F32)
    t = d / max(seq - 1, 1)
    w = (2 * math.pi / seq) * d
    band = lax.broadcasted_iota(jnp.int32, (1, HY_BANDS), 1).astype(F32)
    bands = 1e-4 + band * ((HY_BANDS - 1 - 1e-4) / (HY_BANDS - 1))
    ang = w * bands
    pre = t * w1t_ref[...] + _dot3(jnp.cos(ang), w1c_ref[...]) - _dot3(jnp.sin(ang), w1s_ref[...]) + b1_ref[...]
    hid = jnp.sin(pre)
    hid = jnp.sin(_dot3(hid, w2_ref[...]) + b2_ref[...])
    decay = jnp.exp(-t * delta_ref[...])
    decay_bwd = jnp.where(tap == 0, 0.0, decay)
    for o in range(HY_ORDER):
        o_ref[2 * o] = (_dot3(hid, w3_ref[2 * o]) * decay).astype(o_ref.dtype)
        o_ref[2 * o + 1] = (_dot3(hid, w3_ref[2 * o + 1]) * decay_bwd).astype(o_ref.dtype)


def _hyena_kernels(seq, f_w1, f_b1, f_w2, f_b2, f_w3):
    rb = min(512, seq)
    ffn = f_w2.shape[0]
    w3 = f_w3.reshape(ffn, 2 * HY_ORDER, HY_CH).transpose(1, 0, 2)
    deltas = jnp.abs(jnp.linspace(math.log(HY_TARGET) / HY_FAST_PCT, math.log(HY_TARGET) / HY_SLOW_PCT,
                                  HY_CH, dtype=F32)).reshape(1, HY_CH)
    return pl.pallas_call(
        functools.partial(_filter_kernel, seq=seq, rb=rb),
        grid=(seq // rb,),
        in_specs=[_const_spec((1, ffn)), _const_spec((HY_BANDS, ffn)), _const_spec((HY_BANDS, ffn)),
                  _const_spec((1, ffn)), _const_spec((ffn, ffn)), _const_spec((1, ffn)),
                  _const_spec(w3.shape), _const_spec((1, HY_CH))],
        out_specs=pl.BlockSpec((2 * HY_ORDER, rb, HY_CH), lambda j: (0, j, 0)),
        out_shape=jax.ShapeDtypeStruct((2 * HY_ORDER, seq, HY_CH), BF16),
        compiler_params=_params("parallel"),
        name="hy_filter",
    )(f_w1[0:1], f_w1[1:1 + HY_BANDS], f_w1[1 + HY_BANDS:], f_b1.reshape(1, ffn), f_w2, f_b2.reshape(1, ffn), w3,
      deltas)


def _dft_tables(seq):
    n = 2 * seq
    r = math.isqrt(n)
    assert r * r == n
    r1 = (r // 2 + 1 + 7) // 8 * 8
    idx = jnp.arange(r, dtype=jnp.int32)
    phi = (2 * math.pi / r) * ((idx[:r1, None] * idx[None, :]) % r).astype(F32)
    f1 = jnp.concatenate([jnp.cos(phi), -jnp.sin(phi)], axis=0).astype(BF16)
    m = (idx[:r1, None, None] + r * idx[None, :, None]) * idx[None, None, :]
    th = (2 * math.pi / n) * (m % n).astype(F32)
    c, s = jnp.cos(th), jnp.sin(th)
    blockm = lambda re, im: jnp.concatenate([jnp.concatenate([re, -im], axis=-1),
                                             jnp.concatenate([im, re], axis=-1)], axis=-2)
    g = blockm(c, -s).astype(BF16)
    ginv = blockm(jnp.swapaxes(c, 1, 2), jnp.swapaxes(s, 1, 2)).astype(BF16)
    k1 = idx[:r1]
    weight = jnp.where((k1 == 0) | (k1 == r // 2), 1.0, jnp.where(k1 < r // 2, 2.0, 0.0)) / n
    phi_t = phi.T[:r // 2]
    finv = jnp.concatenate([jnp.cos(phi_t) * weight, -jnp.sin(phi_t) * weight], axis=1).astype(BF16)
    return r, f1, g, ginv, finv


DFT_BB = 32


def _dft1_kernel(f_ref, x_ref, o_ref, *, a0):
    f = f_ref[...]
    r = o_ref.shape[2]
    x = pltpu.einshape("abc->bac", x_ref[0, a0:].astype(F32))
    ys = [jnp.dot(f, x[j].astype(BF16), preferred_element_type=F32) for j in range(DFT_BB)]
    y = pltpu.einshape("jkc->kjc", jnp.stack(ys, axis=0))
    o_ref[0, 0] = y[:r].astype(o_ref.dtype)
    o_ref[0, 1] = y[r:].astype(o_ref.dtype)


def _dft_stage1(f, x4, a0=0):
    b, rows, r, c = x4.shape
    r1 = f.shape[0] // 2
    return pl.pallas_call(
        functools.partial(_dft1_kernel, a0=a0),
        grid=(b, r // DFT_BB),
        in_specs=[_const_spec(f.shape), pl.BlockSpec((1, rows, DFT_BB, c), lambda bi, j: (bi, 0, j, 0))],
        out_specs=pl.BlockSpec((1, 2, r1, DFT_BB, c), lambda bi, j: (bi, 0, 0, j, 0)),
        out_shape=jax.ShapeDtypeStruct((b, 2, r1, r, c), BF16),
        compiler_params=_params("parallel", "parallel"),
        name="hy_dft1",
    )(f, x4)


def _fspec_kernel(af_ref, ab_ref, g_ref, o_ref, *, kb, r):
    for kk in range(kb):
        xf = jnp.dot(g_ref[kk], af_ref[0, :, kk].reshape(2 * r, af_ref.shape[-1]), preferred_element_type=F32)
        xb = jnp.dot(g_ref[kk], ab_ref[0, :, kk].reshape(2 * r, ab_ref.shape[-1]), preferred_element_type=F32)
        o_ref[0, kk, :r] = xf[:r] + xb[:r]
        o_ref[0, kk, r:] = xf[r:] - xb[r:]


def _filter_spectrum(a, g, kb):
    o2, _, r1, r, c = a.shape
    blk = lambda d: pl.BlockSpec((1, 2, kb, r, c), lambda oo, j: (2 * oo + d, 0, j, 0, 0))
    return pl.pallas_call(
        functools.partial(_fspec_kernel, kb=kb, r=r),
        grid=(o2 // 2, r1 // kb),
        in_specs=[blk(0), blk(1), pl.BlockSpec((kb, 2 * r, 2 * r), lambda oo, j: (j, 0, 0))],
        out_specs=pl.BlockSpec((1, kb, 2 * r, c), lambda oo, j: (oo, j, 0, 0)),
        out_shape=jax.ShapeDtypeStruct((o2 // 2, r1, 2 * r, c), F32),
        compiler_params=_params("parallel", "parallel"),
        name="hy_fspec",
    )(a, a, g)


def _dft2_kernel(a_ref, g_ref, gi_ref, k_ref, o_ref, *, kb, r):
    xs = [jnp.dot(g_ref[kk], a_ref[0, :, kk].reshape(2 * r, a_ref.shape[-1]), preferred_element_type=F32)
          for kk in range(kb)]
    ys = []
    for kk in range(kb):
        x, ks = xs[kk], k_ref[0, kk]
        xr, xi, kr, ki = x[:r], x[r:], ks[:r], ks[r:]
        ys.append(jnp.concatenate([xr * kr - xi * ki, xr * ki + xi * kr], axis=0).astype(BF16))
    bqs = [jnp.dot(gi_ref[kk], ys[kk], preferred_element_type=F32) for kk in range(kb)]
    for kk in range(kb):
        o_ref[0, 0, kk] = bqs[kk][:r].astype(o_ref.dtype)
        o_ref[0, 1, kk] = bqs[kk][r:].astype(o_ref.dtype)


def _dft_stage2(a, g, ginv, kspec, order, kb):
    b, _, r1, r, c = a.shape
    blk = pl.BlockSpec((1, 2, kb, r, c), lambda j, bb: (bb, 0, j, 0, 0))
    tab = pl.BlockSpec((kb, 2 * r, 2 * r), lambda j, bb: (j, 0, 0))
    return pl.pallas_call(
        functools.partial(_dft2_kernel, kb=kb, r=r),
        grid=(r1 // kb, b),
        in_specs=[blk, tab, tab, pl.BlockSpec((1, kb, 2 * r, c), lambda j, bb: (order, j, 0, 0))],
        out_specs=blk,
        out_shape=jax.ShapeDtypeStruct(a.shape, BF16),
        compiler_params=_params("parallel", "arbitrary"),
        name="hy_dft2",
    )(a, g, ginv, kspec)


def _dft3_kernel(f_ref, b_ref, gate_ref, z_ref, bias_ref, head_ref, o_ref, *, a0):
    f = f_ref[...]
    bq = jnp.concatenate([b_ref[0, 0], b_ref[0, 1]], axis=0)
    bq = pltpu.einshape("kjc->jkc", bq.astype(F32))
    ys = [jnp.dot(f, bq[j].astype(BF16), preferred_element_type=F32) for j in range(DFT_BB)]
    y = pltpu.einshape("jac->ajc", jnp.stack(ys, axis=0))
    o_ref[0, :a0] = head_ref[0]
    o_ref[0, a0:] = gate_ref[0, a0:] * (y + z_ref[0, a0:] * bias_ref[...])


def _dft_stage3(finv, bq5, gate4, z4, bias, head4):
    b, _, r1, r, c = bq5.shape
    rows = z4.shape[1]
    a0 = head4.shape[1]
    io = pl.BlockSpec((1, rows, DFT_BB, c), lambda bi, j: (bi, 0, j, 0))
    return pl.pallas_call(
        functools.partial(_dft3_kernel, a0=a0),
        grid=(b, r // DFT_BB),
        in_specs=[_const_spec(finv.shape), pl.BlockSpec((1, 2, r1, DFT_BB, c), lambda bi, j: (bi, 0, 0, j, 0)), io, io,
                  _const_spec((1, 1, c)), pl.BlockSpec((1, a0, DFT_BB, c), lambda bi, j: (bi, 0, j, 0))],
        out_specs=io,
        out_shape=jax.ShapeDtypeStruct((b, rows, r, c), F32),
        compiler_params=_params("parallel", "parallel"),
        name="hy_dft3",
    )(finv, bq5, gate4, z4, bias, head4)


def _long_conv_gated(v, gates, kern, hy_bias, ctx_out, lc):
    b, t, c = v.shape
    r, f1, g, ginv, finv = _dft_tables(t - lc)
    assert lc % r == 0
    a0 = lc // r
    kb = 8
    kspec = _filter_spectrum(_dft_stage1(f1[:, :r // 2], kern.reshape(2 * HY_ORDER, r // 2, r, c)), g, kb)
    as_blocks = lambda x: x.reshape(b, t // r, r, c)
    z = as_blocks(v)
    heads = [jnp.zeros((b, a0, r, c), F32), ctx_out.reshape(b, a0, r, c)]
    for o in range(HY_ORDER):
        a = _dft_stage1(f1[:, :r // 2], z, a0)
        bq = _dft_stage2(a, g, ginv, kspec, o, kb)
        z = _dft_stage3(finv, bq, as_blocks(gates[o]), z, hy_bias[o].reshape(1, 1, c), heads[o])
    return z.reshape(b, t, c)


def _ctx_conv_kernel(v_ref, x1_ref, x2_ref, k_ref, ff_ref, fi_ref, bias_ref, o_ref):
    n = ff_ref.shape[1]
    z = v_ref[0]
    for o, gate_ref in enumerate((x1_ref, x2_ref)):
        kf = jnp.dot(ff_ref[:, :TM], k_ref[2 * o], preferred_element_type=F32)
        kb = jnp.dot(ff_ref[:, :TM], k_ref[2 * o + 1], preferred_element_type=F32)
        x = jnp.dot(ff_ref[:, :TM], z.astype(BF16), preferred_element_type=F32)
        xr, xi, kr, ki = x[:n], x[n:], kf[:n] + kb[:n], kf[n:] - kb[n:]
        y = jnp.concatenate([xr * kr - xi * ki, xr * ki + xi * kr], axis=0).astype(BF16)
        conv = jnp.dot(fi_ref[...], y, preferred_element_type=F32)
        z = gate_ref[0] * (conv + z * bias_ref[o:o + 1])
    o_ref[0] = z


def _ctx_conv_gated(v, x1, x2, kern, hy_bias):
    b, _, c = v.shape
    seq = TM
    n = 2 * seq
    idx = jnp.arange(n, dtype=jnp.int32)
    th = (2 * math.pi / n) * ((idx[:, None] * idx[None, :]) % n).astype(F32)
    ff = jnp.concatenate([jnp.cos(th), -jnp.sin(th)], axis=0).astype(BF16)
    fi = (jnp.concatenate([jnp.cos(th), -jnp.sin(th)], axis=1)[:seq] / n).astype(BF16)
    io = pl.BlockSpec((1, seq, c), lambda bb: (bb, 0, 0))
    return pl.pallas_call(
        _ctx_conv_kernel,
        grid=(b,),
        in_specs=[io, io, io, _const_spec(kern.shape), _const_spec(ff.shape), _const_spec(fi.shape),
                  _const_spec(hy_bias.shape)],
        out_specs=io,
        out_shape=jax.ShapeDtypeStruct((b, seq, c), F32),
        compiler_params=_params("parallel"),
        name="hy_ctx",
    )(v, x1, x2, kern, ff, fi, hy_bias)


def _hyena(v, x1, x2, lc, f_w1, f_b1, f_w2, f_b2, f_w3, hy_bias):
    seq = v.shape[1] - lc
    yc = _ctx_conv_gated(v, x1, x2, _hyena_kernels(lc, f_w1, f_b1, f_w2, f_b2, f_w3), hy_bias)
    return _long_conv_gated(v, (x1, x2), _hyena_kernels(seq, f_w1, f_b1, f_w2, f_b2, f_w3), hy_bias, yc, lc)


def _rope_tables_t(seq, lc):
    rows = seq // GRID_W
    r = jnp.broadcast_to(jnp.arange(rows, dtype=F32)[:, None], (rows, GRID_W)).reshape(-1)
    col = jnp.broadcast_to(jnp.arange(GRID_W, dtype=F32)[None, :], (rows, GRID_W)).reshape(-1)
    n_pairs = HEAD_DIM // 4
    inv = ROPE_THETA ** (-jnp.arange(n_pairs, dtype=F32) / n_pairs)
    ang = jnp.concatenate([r[:, None] * inv, col[:, None] * inv], axis=-1)
    cos = jnp.concatenate([jnp.ones((lc, 2 * n_pairs), F32), jnp.cos(ang)], axis=0)
    sin = jnp.concatenate([jnp.zeros((lc, 2 * n_pairs), F32), jnp.sin(ang)], axis=0)
    return cos.T, sin.T


def kernel(x, c, ctx, c_ctx, w_ada, b_ada, norm_g, w_mlp_in, w_mlp_out, ev_w_in, ev_w_lr, ev_b_lr, ev_gla_g,
           ev_qk_g, ev_w_out, od_w_in, od_conv_w, od_conv_b, od_f_w1, od_f_b1, od_f_w2, od_f_b2, od_f_w3,
           od_hy_bias, od_lam, od_diff_g, od_w_out):
    bsz, seq, d = x.shape
    lc = ctx.shape[1]
    depth = w_ada.shape[0]
    assert lc == TM and seq % TM == 0 and bsz <= MOD_CTX_ROW and bsz % POST_PB == 0 and bsz % IN_PB == 0 and bsz % GLA_PB == 0
    cos_t, sin_t = _rope_tables_t(seq, lc)
    cc = jnp.zeros((8, d), F32).at[:bsz].set(c).at[MOD_CTX_ROW:MOD_CTX_ROW + max(POST_PB, IN_PB)].set(c_ctx)
    mods = _mods(cc, w_ada, b_ada).reshape(depth, 8, N_MOD, d)
    xs = jnp.concatenate([ctx, x], axis=1)
    for i in range(depth):
        last = i == depth - 1
        j = i // 2
        g = norm_g[i]
        if i % 2 == 0:
            qk, la, gv, gg, aq, ak, av, kn = _even_in(xs, mods[i], g[0:1], ev_w_in[j], ev_w_lr[j], ev_b_lr[j],
                                                  ev_qk_g[j], cos_t, sin_t)
            ya = (_gla_direction(qk, la, gv, False), _gla_direction(qk, la, gv, True), gg, ev_gla_g[j])
            yb = _gqa_attention(aq, ak, av, jnp.max(kn, axis=1))
            w_out = ev_w_out[j]
        else:
            hv, hx1, hx2, dq, dk, dv, kn = _odd_in(xs, mods[i], g[0:1], od_w_in[j], od_conv_w[j], od_conv_b[j],
                                                   cos_t, sin_t)
            ya = _hyena(hv, hx1, hx2, lc, od_f_w1[j], od_f_b1[j], od_f_w2[j], od_f_b2[j], od_f_w3[j], od_hy_bias[j])
            yb = _diff_attention(dq, dk, dv, jnp.max(kn, axis=1), od_lam[j], od_diff_g[j], i)
            w_out = od_w_out[j]
        xs = _post(xs, ya, yb, mods[i], g, w_out, w_mlp_in[i], w_mlp_out[i], skip_ctx=last)
    return xs
```

```python
import functools
import math

import jax
import jax.numpy as jnp
import numpy as np
from jax import lax
from jax.experimental import pallas as pl
from jax.experimental.pallas import tpu as pltpu

F32 = jnp.float32
BF16 = jnp.bfloat16

TM = 256
GRID_W = 64
HEAD_DIM = 64
ROPE_THETA = 10000.0
NORM_EPS = 1e-6
N_MOD = 6
VMEM_LIMIT = 56 * 1024 * 1024

GLA_HEADS = 4
GLA_DK = 64
GLA_DV = 128
GLA_RANK = 16
GLA_NORMALIZER = 16.0
GLA_CHUNK = 64
GLA_QK = GLA_HEADS * GLA_DK
GLA_V = GLA_HEADS * GLA_DV
GQA_HEADS = 8
GQA_KV_HEADS = 2
HY_CH = 512
HY_ORDER = 2
HY_BANDS = 16
HY_TARGET = 1e-2
HY_FAST_PCT = 0.3
HY_SLOW_PCT = 1.5
HY_COLS = (HY_ORDER + 1) * HY_CH
DIFF_HEADS = 4
DIFF_QK = DIFF_HEADS * 2 * HEAD_DIM
DIFF_V = DIFF_HEADS * 2 * HEAD_DIM
ONES_ROWS = 16
N_SCORE_HEADS = 8
LOG2E = 1.4426950408889634
Q_SCALE = HEAD_DIM ** -0.5 * LOG2E
GQA_VROWS = HEAD_DIM + ONES_ROWS
SCORE_BOUND_SLACK = 1.01
FAST_UNROLL_MAX = 16
SCORE_BOUND_LIMIT = 60.0
DIFF_VROWS = 2 * HEAD_DIM + ONES_ROWS


def _params(*sem):
    return pltpu.CompilerParams(dimension_semantics=sem, vmem_limit_bytes=VMEM_LIMIT)


def _const_spec(shape):
    nd = len(shape)
    return pl.BlockSpec(shape, lambda *_: (0,) * nd, pipeline_mode=pl.Buffered(1))


def _rms(x, g):
    return x * lax.rsqrt(jnp.mean(x * x, axis=-1, keepdims=True) + NORM_EPS) * g


MOD_CTX_ROW = 4
POST_PB = 2
IN_PB = 4
GLA_PB = 4


def _mod_block(bb, i, pb):
    return jnp.where(i == 0, MOD_CTX_ROW // pb, bb)


def _mods_kernel(cc_ref, w_ref, b_ref, o_ref):
    cc = cc_ref[...]
    s = cc / (1.0 + jnp.exp(-cc))
    o_ref[0] = jnp.dot(s.astype(BF16), w_ref[0].astype(BF16), preferred_element_type=F32) + b_ref[0]


def _mods(cc, w_ada, b_ada):
    depth, d, n = w_ada.shape
    nb = n // 4
    return pl.pallas_call(
        _mods_kernel,
        grid=(depth, n // nb),
        in_specs=[pl.BlockSpec((8, d), lambda l, j: (0, 0)),
                  pl.BlockSpec((1, d, nb), lambda l, j: (l, 0, j)),
                  pl.BlockSpec((1, 1, nb), lambda l, j: (l, 0, j))],
        out_specs=pl.BlockSpec((1, 8, nb), lambda l, j: (l, 0, j)),
        out_shape=jax.ShapeDtypeStruct((depth, 8, n), F32),
        compiler_params=_params("parallel", "parallel"),
        name="ada_mods",
    )(cc, w_ada, b_ada.reshape(depth, 1, n))


def _normed_input(x, mods, g):
    return (_rms(x, g) * (1.0 + mods[1:2]) + mods[0:1]).astype(BF16)


def _rope_rows(y, cos, sin):
    x1, x2 = y[:32], y[32:]
    return x1 * cos - x2 * sin, x1 * sin + x2 * cos


def _log_sigmoid(x):
    return jnp.minimum(x, 0.0) - jnp.log(1.0 + jnp.exp(-jnp.abs(x)))


def _nt(a, b):
    return lax.dot_general(a, b, (((1,), (1,)), ((), ())), preferred_element_type=F32)


def _max_sq_norm_row(parts):
    n2 = sum(jnp.sum(jnp.square(p.astype(BF16).astype(F32)), axis=0, keepdims=True) for p in parts)
    return jnp.broadcast_to(jnp.max(n2, axis=1, keepdims=True), (1, 128))


def _even_in_kernel(xs_ref, mods_ref, g_ref, wn_ref, wt_ref, wlr_ref, blr_ref, gq_ref, gk_ref,
                    cos_ref, sin_ref, qk_ref, la_ref, gv_ref, gg_ref, aq_ref, ak_ref, av_ref, kn_ref):
    hs = [_normed_input(xs_ref[s], mods_ref[s], g_ref[0:1]) for s in range(IN_PB)]
    ps = [jnp.dot(h, wn_ref[...], preferred_element_type=F32) for h in hs]
    pts = [_nt(wt_ref[...], h) for h in hs]
    cos, sin = cos_ref[...], sin_ref[...]
    gq, gk_gain = gq_ref[...], gk_ref[...]
    ones = jnp.ones((ONES_ROWS, TM), BF16)
    for s in range(IN_PB):
        p, pt_all = ps[s], pts[s]
        qk_ref[s, :, :GLA_QK] = p[:, :GLA_QK] * GLA_DK ** -0.5
        qk_ref[s, :, GLA_QK:] = p[:, GLA_QK:2 * GLA_QK]
        lr = p[:, 2 * GLA_QK:].astype(BF16)
        gk = jnp.dot(lr, wlr_ref[...], preferred_element_type=F32) + blr_ref[...]
        la_ref[s] = _log_sigmoid(gk) * (1.0 / GLA_NORMALIZER)
        gv_ref[s, 0] = pt_all[:GLA_V].astype(BF16)
        gg_ref[s, 0] = pt_all[GLA_V:2 * GLA_V]
        pt = pt_all[2 * GLA_V:]
        for hh in range(GQA_HEADS):
            blk = pt[hh * 64:(hh + 1) * 64]
            y = blk * lax.rsqrt(jnp.mean(blk * blk, axis=0, keepdims=True) + NORM_EPS) * gq
            o1, o2 = _rope_rows(y, cos, sin)
            aq_ref[s, 0, hh * 64:hh * 64 + 32, :] = (o1 * Q_SCALE).astype(BF16)
            aq_ref[s, 0, hh * 64 + 32:hh * 64 + 64, :] = (o2 * Q_SCALE).astype(BF16)
        kparts, knorms = [], []
        for hh in range(GQA_KV_HEADS):
            blk = pt[512 + hh * 64:512 + (hh + 1) * 64]
            y = blk * lax.rsqrt(jnp.mean(blk * blk, axis=0, keepdims=True) + NORM_EPS) * gk_gain
            o1, o2 = _rope_rows(y, cos, sin)
            kparts += [o1, o2]
            knorms.append(_max_sq_norm_row([o1, o2]))
        ak_ref[s] = jnp.concatenate(kparts, axis=0).T.astype(BF16)
        kn_ref[s, 0] = jnp.concatenate(knorms + [jnp.zeros((8 - GQA_KV_HEADS, 128), F32)], axis=0)
        for hh in range(GQA_KV_HEADS):
            av_ref[s, 0, hh * GQA_VROWS:hh * GQA_VROWS + 64, :] = pt[640 + hh * 64:640 + (hh + 1) * 64].astype(BF16)
            av_ref[s, 0, hh * GQA_VROWS + 64:(hh + 1) * GQA_VROWS, :] = ones


def _odd_in_kernel(xs_ref, prev_ref, next_ref, mods_ref, g_ref, wn_ref, wt_ref, cw_ref, cb_ref, cos_ref, sin_ref,
                   v_ref, x1_ref, x2_ref, dq_ref, dk_ref, dv_ref, kn_ref, *, n_tiles):
    i = pl.program_id(1)
    hs = [_normed_input(xs_ref[s], mods_ref[s], g_ref[0:1]) for s in range(IN_PB)]
    h_ext = [jnp.concatenate([hs[s], _normed_input(prev_ref[s], mods_ref[s], g_ref[0:1]),
                              _normed_input(next_ref[s], mods_ref[s], g_ref[0:1])], axis=0) for s in range(IN_PB)]
    us = [jnp.dot(h, wn_ref[...], preferred_element_type=F32) for h in h_ext]
    pts = [_nt(wt_ref[...], h) for h in hs]
    cos, sin = cos_ref[...], sin_ref[...]
    ones = jnp.ones((ONES_ROWS, TM), BF16)
    nv = 2 * HEAD_DIM
    cw = cw_ref[...]
    row_id = lax.broadcasted_iota(jnp.int32, (TM, HY_COLS), 0)
    no_prev = (i == 0) | (i == 1)
    no_next = (i == 0) | (i == n_tiles - 1)
    for s in range(IN_PB):
        pt = pts[s]
        u = us[s][:TM]
        prev_row = jnp.where(no_prev, 0.0, us[s][TM + 7:TM + 8])
        next_row = jnp.where(no_next, 0.0, us[s][TM + 8:TM + 9])
        up = jnp.where(row_id == 0, prev_row, pltpu.roll(u, 1, axis=0))
        dn = jnp.where(row_id == TM - 1, next_row, pltpu.roll(u, TM - 1, axis=0))
        conv = up * cw[0:1] + u * cw[1:2] + dn * cw[2:3] + cb_ref[...]
        v_ref[s] = conv[:, :HY_CH]
        x1_ref[s] = conv[:, HY_CH:2 * HY_CH]
        x2_ref[s] = conv[:, 2 * HY_CH:]
        kparts, knorms = [], []
        for hh in range(2 * DIFF_HEADS):
            o1, o2 = _rope_rows(pt[hh * 64:(hh + 1) * 64], cos, sin)
            dq_ref[s, 0, hh * 64:hh * 64 + 32, :] = (o1 * Q_SCALE).astype(BF16)
            dq_ref[s, 0, hh * 64 + 32:hh * 64 + 64, :] = (o2 * Q_SCALE).astype(BF16)
            k1, k2 = _rope_rows(pt[DIFF_QK + hh * 64:DIFF_QK + (hh + 1) * 64], cos, sin)
            kparts += [k1, k2]
            knorms.append(_max_sq_norm_row([k1, k2]))
        dk_ref[s] = jnp.concatenate(kparts, axis=0).T.astype(BF16)
        kn_ref[s, 0] = jnp.concatenate(knorms, axis=0)
        for hh in range(DIFF_HEADS):
            rows = pt[2 * DIFF_QK + hh * nv:2 * DIFF_QK + (hh + 1) * nv]
            dv_ref[s, 0, hh * DIFF_VROWS:hh * DIFF_VROWS + nv, :] = rows.astype(BF16)
            dv_ref[s, 0, hh * DIFF_VROWS + nv:(hh + 1) * DIFF_VROWS, :] = ones


def _pair_perm(n_heads):
    base = np.concatenate([np.arange(0, HEAD_DIM, 2), np.arange(1, HEAD_DIM, 2)])
    return np.concatenate([h * HEAD_DIM + base for h in range(n_heads)])


def _even_in(xs, mods_l, g0, w_in, w_lr, b_lr, qk_g, cos_t, sin_t):
    b, t, d = xs.shape
    nt = t // TM
    c_v = 2 * GLA_QK
    c_g = c_v + GLA_V
    c_lr0 = c_g + GLA_V
    c_lr = c_lr0 + 2 * GLA_RANK
    wn = jnp.concatenate([w_in[:, :c_v], w_in[:, c_lr0:c_lr], jnp.zeros((d, 128 - 2 * GLA_RANK), F32)],
                         axis=1).astype(BF16)
    cq = c_lr + GQA_HEADS * HEAD_DIM
    ck = cq + GQA_KV_HEADS * HEAD_DIM
    wq = w_in[:, c_lr:cq][:, _pair_perm(GQA_HEADS)]
    wk = w_in[:, cq:ck][:, _pair_perm(GQA_KV_HEADS)]
    wt = jnp.concatenate([w_in[:, c_v:c_lr0], wq, wk, w_in[:, ck:]], axis=1).T.astype(BF16)
    wlr = jnp.zeros((128, 2 * GLA_QK), F32)
    wlr = wlr.at[:GLA_RANK, :GLA_QK].set(w_lr[0]).at[GLA_RANK:2 * GLA_RANK, GLA_QK:].set(w_lr[1]).astype(BF16)
    blr = b_lr.reshape(1, 2 * GLA_QK)
    perm = _pair_perm(1)
    gq = qk_g[0][perm].reshape(HEAD_DIM, 1)
    gk = qk_g[1][perm].reshape(HEAD_DIM, 1)
    tok = lambda n: pl.BlockSpec((IN_PB, TM, n), lambda bb, i: (bb, i, 0))
    tr = lambda n: pl.BlockSpec((IN_PB, 1, n, TM), lambda bb, i: (bb, i, 0, 0))
    return pl.pallas_call(
        _even_in_kernel,
        grid=(b // IN_PB, nt),
        in_specs=[tok(d),
                  pl.BlockSpec((IN_PB, N_MOD, d), lambda bb, i: (_mod_block(bb, i, IN_PB), 0, 0)),
                  _const_spec((1, d)),
                  _const_spec(wn.shape), _const_spec(wt.shape), _const_spec(wlr.shape), _const_spec(blr.shape),
                  _const_spec(gq.shape), _const_spec(gk.shape),
                  pl.BlockSpec((32, TM), lambda bb, i: (0, i)),
                  pl.BlockSpec((32, TM), lambda bb, i: (0, i))],
        out_specs=[tok(2 * GLA_QK), tok(2 * GLA_QK), tr(GLA_V), tr(GLA_V), tr(512), tok(128),
                   tr(GQA_KV_HEADS * GQA_VROWS), pl.BlockSpec((IN_PB, 1, 8, 128), lambda bb, i: (bb, i, 0, 0))],
        out_shape=[jax.ShapeDtypeStruct((b, t, 2 * GLA_QK), F32),
                   jax.ShapeDtypeStruct((b, t, 2 * GLA_QK), F32),
                   jax.ShapeDtypeStruct((b, nt, GLA_V, TM), BF16),
                   jax.ShapeDtypeStruct((b, nt, GLA_V, TM), F32),
                   jax.ShapeDtypeStruct((b, nt, 512, TM), BF16),
                   jax.ShapeDtypeStruct((b, t, 128), BF16),
                   jax.ShapeDtypeStruct((b, nt, GQA_KV_HEADS * GQA_VROWS, TM), BF16),
                   jax.ShapeDtypeStruct((b, nt, 8, 128), F32)],
        compiler_params=_params("parallel", "parallel"),
        name="even_in",
    )(xs, mods_l, g0, wn, wt, wlr, blr, gq, gk, cos_t, sin_t)


def _odd_in(xs, mods_l, g0, w_in, conv_w, conv_b, cos_t, sin_t):
    b, t, d = xs.shape
    nt = t // TM
    r8 = TM // 8
    last_blk = t // 8 - 1
    wn = w_in[:, :HY_COLS].astype(BF16)
    perm = _pair_perm(2 * DIFF_HEADS)
    wq = w_in[:, HY_COLS:HY_COLS + DIFF_QK][:, perm]
    wk = w_in[:, HY_COLS + DIFF_QK:HY_COLS + 2 * DIFF_QK][:, perm]
    wt = jnp.concatenate([wq, wk, w_in[:, HY_COLS + 2 * DIFF_QK:]], axis=1).T.astype(BF16)
    tok = lambda n: pl.BlockSpec((IN_PB, TM, n), lambda bb, i: (bb, i, 0))
    tr = lambda n: pl.BlockSpec((IN_PB, 1, n, TM), lambda bb, i: (bb, i, 0, 0))
    hy = jax.ShapeDtypeStruct((b, t, HY_CH), F32)
    return pl.pallas_call(
        functools.partial(_odd_in_kernel, n_tiles=nt),
        grid=(b // IN_PB, nt),
        in_specs=[tok(d),
                  pl.BlockSpec((IN_PB, 8, d), lambda bb, i: (bb, jnp.maximum(i * r8 - 1, 0), 0)),
                  pl.BlockSpec((IN_PB, 8, d), lambda bb, i: (bb, jnp.minimum((i + 1) * r8, last_blk), 0)),
                  pl.BlockSpec((IN_PB, N_MOD, d), lambda bb, i: (_mod_block(bb, i, IN_PB), 0, 0)),
                  _const_spec((1, d)),
                  _const_spec(wn.shape), _const_spec(wt.shape),
                  _const_spec(conv_w.shape), _const_spec((1, HY_COLS)),
                  pl.BlockSpec((32, TM), lambda bb, i: (0, i)),
                  pl.BlockSpec((32, TM), lambda bb, i: (0, i))],
        out_specs=[tok(HY_CH), tok(HY_CH), tok(HY_CH), tr(DIFF_QK), tok(DIFF_QK), tr(DIFF_HEADS * DIFF_VROWS),
                   pl.BlockSpec((IN_PB, 1, 8, 128), lambda bb, i: (bb, i, 0, 0))],
        out_shape=[hy, hy, hy,
                   jax.ShapeDtypeStruct((b, nt, DIFF_QK, TM), BF16),
                   jax.ShapeDtypeStruct((b, t, DIFF_QK), BF16),
                   jax.ShapeDtypeStruct((b, nt, DIFF_HEADS * DIFF_VROWS, TM), BF16),
                   jax.ShapeDtypeStruct((b, nt, 8, 128), F32)],
        compiler_params=_params("parallel", "parallel"),
        name="odd_in",
    )(xs, xs, xs, mods_l, g0, wn, wt, conv_w, conv_b.reshape(1, HY_COLS), cos_t, sin_t)


def _attn_core(q_ref, k_ref, v_ref, kmax_ref, scratch, *, n_kv, k_col, k_row, q_slot, v_row, v_rows, scores_between,
               vpu_denominator):
    qpad_ref, bound_ref, m_ref, acc_ref, s0_ref, s1_ref, c0_ref, c1_ref, p0_ref, p1_ref = scratch
    s_refs, c_refs, p_refs = (s0_ref, s1_ref), (c0_ref, c1_ref), (p0_ref, p1_ref)
    i = pl.program_id(1)
    zeros = jnp.zeros((HEAD_DIM, TM), BF16)
    bound_max = jnp.zeros((1, TM), F32)
    for hh in range(N_SCORE_HEADS):
        qh = q_ref[0, 0, hh * HEAD_DIM:(hh + 1) * HEAD_DIM, :]
        qpad_ref[hh] = jnp.concatenate([qh, zeros] if q_slot(hh) == 0 else [zeros, qh], axis=0)
        qf = qh.astype(F32)
        q_norm2 = jnp.sum(qf * qf, axis=0, keepdims=True)
        k_norm2 = kmax_ref[0, k_row(hh):k_row(hh) + 1, 0:1]
        bound = jnp.sqrt(q_norm2 * k_norm2) * SCORE_BOUND_SLACK
        bound_ref[hh] = bound
        bound_max = jnp.maximum(bound_max, bound)
    acc_ref[...] = jnp.zeros(acc_ref.shape, F32)
    n = jnp.where(i == 0, 1, n_kv)
    fixed_reference = jnp.max(bound_max) <= SCORE_BOUND_LIMIT

    def produce(j, hh, slot):
        kc = k_ref[0, pl.ds(pl.multiple_of(j * TM, TM), TM), k_col(hh):k_col(hh) + 128]
        s = jnp.dot(kc, qpad_ref[hh], preferred_element_type=F32)
        p = jnp.exp2(s - bound_ref[hh])
        if vpu_denominator:
            m_ref[hh] = m_ref[hh] + jnp.sum(p, axis=0, keepdims=True)
        p_refs[slot][hh] = p.astype(BF16)

    n_val = v_rows - ONES_ROWS if vpu_denominator else v_rows

    def reduce(j, hh, slot):
        vc = v_ref[0, j, v_row(hh):v_row(hh) + n_val, :]
        acc_ref[hh, :n_val] = acc_ref[hh, :n_val] + jnp.dot(vc, p_refs[slot][hh], preferred_element_type=F32)

    unroll = math.gcd(n_kv - 1, FAST_UNROLL_MAX)

    @pl.when(fixed_reference)
    def _():
        m_ref[...] = jnp.zeros(m_ref.shape, F32)
        for hh in range(N_SCORE_HEADS):
            produce(0, hh, 0)

        def fast_step(j, slot):
            for hh in range(N_SCORE_HEADS):
                produce(j + 1, hh, 1 - slot)
                reduce(j, hh, slot)

        def fast_body(jj, carry):
            for u in range(unroll):
                fast_step(unroll * jj + u, u % 2)
            return carry

        lax.fori_loop(0, (n - 1) // unroll, fast_body, 0)
        for hh in range(N_SCORE_HEADS):
            reduce(n - 1, hh, 0)
            if vpu_denominator:
                acc_ref[hh, n_val:, :] = jnp.broadcast_to(m_ref[hh], (ONES_ROWS, TM))

    @pl.when(jnp.logical_not(fixed_reference))
    def _():
        _attn_running_max(k_ref, v_ref, qpad_ref, m_ref, acc_ref, s_refs, c_refs, n=n, k_col=k_col, v_row=v_row,
                          v_rows=v_rows, scores_between=scores_between)


def _attn_running_max(k_ref, v_ref, qpad_ref, m_ref, acc_ref, s_refs, c_refs, *, n, k_col, v_row, v_rows,
                      scores_between):
    m_ref[...] = jnp.full(m_ref.shape, -jnp.inf, F32)

    def scores(j, hh, slot):
        kc = k_ref[0, pl.ds(pl.multiple_of(j * TM, TM), TM), k_col(hh):k_col(hh) + 128]
        s = jnp.dot(kc, qpad_ref[hh], preferred_element_type=F32)
        s_refs[slot][hh] = s
        c_refs[slot][hh] = jnp.max(s, axis=0, keepdims=True)

    def probs(hh, slot):
        m_old = m_ref[hh]
        m_new = jnp.maximum(m_old, c_refs[slot][hh])
        return m_old, m_new, jnp.exp2(s_refs[slot][hh] - m_new).astype(BF16)

    def accumulate(j, hh, state):
        m_old, m_new, p = state
        vc = v_ref[0, j, v_row(hh):v_row(hh) + v_rows, :]
        alpha = jnp.exp2(m_old - m_new)
        acc_ref[hh] = acc_ref[hh] * alpha + jnp.dot(vc, p, preferred_element_type=F32)
        m_ref[hh] = m_new

    for hh in range(N_SCORE_HEADS):
        scores(0, hh, 0)

    def step(j, slot):
        for hh in range(N_SCORE_HEADS):
            state = probs(hh, slot)
            if scores_between:
                scores(j + 1, hh, 1 - slot)
            accumulate(j, hh, state)
            if not scores_between:
                scores(j + 1, hh, 1 - slot)

    def body(jj, carry):
        step(2 * jj, 0)
        step(2 * jj + 1, 1)
        return carry

    lax.fori_loop(0, (n - 1) // 2, body, 0)
    for hh in range(N_SCORE_HEADS):
        accumulate(n - 1, hh, probs(hh, 0))


def _attn_scratch(v_rows):
    stat = pltpu.VMEM((N_SCORE_HEADS, 1, TM), F32)
    return [pltpu.VMEM((N_SCORE_HEADS, 128, TM), BF16),
            stat, stat,
            pltpu.VMEM((N_SCORE_HEADS, v_rows, TM), F32),
            pltpu.VMEM((N_SCORE_HEADS, TM, TM), F32), pltpu.VMEM((N_SCORE_HEADS, TM, TM), F32),
            stat, stat,
            pltpu.VMEM((N_SCORE_HEADS, TM, TM), BF16), pltpu.VMEM((N_SCORE_HEADS, TM, TM), BF16)]


def _gqa_kernel(q_ref, k_ref, v_ref, kmax_ref, o_ref, *scratch, n_kv):
    group = GQA_HEADS // GQA_KV_HEADS
    _attn_core(q_ref, k_ref, v_ref, kmax_ref, scratch, n_kv=n_kv,
               k_col=lambda h: 0, k_row=lambda h: h // group, q_slot=lambda h: h // group,
               v_row=lambda h: (h // group) * GQA_VROWS, v_rows=GQA_VROWS, scores_between=True,
               vpu_denominator=False)
    acc_ref = scratch[3]
    for hh in range(GQA_HEADS):
        a = acc_ref[hh]
        o_ref[0, 0, hh * HEAD_DIM:(hh + 1) * HEAD_DIM, :] = (a[:HEAD_DIM] / a[HEAD_DIM:HEAD_DIM + 1]).astype(o_ref.dtype)


def _gqa_attention(aq, ak, av, kmax):
    b, nt, nq, _ = aq.shape
    t = nt * TM
    assert nt % 2 == 1 and nt > 1
    return pl.pallas_call(
        functools.partial(_gqa_kernel, n_kv=nt),
        grid=(b, nt),
        in_specs=[pl.BlockSpec((1, 1, nq, TM), lambda bb, i: (bb, i, 0, 0)),
                  pl.BlockSpec((1, t, 128), lambda bb, i: (bb, 0, 0)),
                  pl.BlockSpec((1, nt, GQA_KV_HEADS * GQA_VROWS, TM), lambda bb, i: (bb, 0, 0, 0)),
                  pl.BlockSpec((1, 8, 128), lambda bb, i: (bb, 0, 0))],
        out_specs=pl.BlockSpec((1, 1, nq, TM), lambda bb, i: (bb, i, 0, 0)),
        out_shape=jax.ShapeDtypeStruct((b, nt, nq, TM), BF16),
        scratch_shapes=_attn_scratch(GQA_VROWS),
        compiler_params=_params("parallel", "arbitrary"),
        name="gqa_attn",
    )(aq, ak, av, kmax)


def _diff_kernel(q_ref, k_ref, v_ref, kmax_ref, lam_ref, gain_ref, o_ref, *scratch, n_kv, lam_init):
    _attn_core(q_ref, k_ref, v_ref, kmax_ref, scratch, n_kv=n_kv,
               k_col=lambda hc: (hc // 2) * 128, k_row=lambda hc: hc, q_slot=lambda hc: hc % 2,
               v_row=lambda hc: (hc // 2) * DIFF_VROWS, v_rows=DIFF_VROWS, scores_between=False,
               vpu_denominator=True)
    acc_ref = scratch[3]
    lp = lam_ref[...]
    lam = (jnp.exp(jnp.sum(lp[0:1] * lp[1:2], axis=1, keepdims=True))
           - jnp.exp(jnp.sum(lp[2:3] * lp[3:4], axis=1, keepdims=True)) + lam_init)
    gain = gain_ref[...]
    nv = 2 * HEAD_DIM
    for hh in range(DIFF_HEADS):
        a0, a1 = acc_ref[2 * hh], acc_ref[2 * hh + 1]
        o = a0[:nv] / a0[nv:nv + 1] - lam * (a1[:nv] / a1[nv:nv + 1])
        y = o * lax.rsqrt(jnp.mean(o * o, axis=0, keepdims=True) + NORM_EPS) * gain
        o_ref[0, 0, hh * nv:(hh + 1) * nv, :] = (y * (1.0 - lam_init)).astype(o_ref.dtype)


def _diff_attention(dq, dk, dv, kmax, lam_p, gain, layer_idx):
    b, nt, nq, _ = dq.shape
    t = nt * TM
    assert nt % 2 == 1 and nt > 1
    lam_init = 0.8 - 0.6 * math.exp(-0.3 * layer_idx)
    return pl.pallas_call(
        functools.partial(_diff_kernel, n_kv=nt, lam_init=lam_init),
        grid=(b, nt),
        in_specs=[pl.BlockSpec((1, 1, nq, TM), lambda bb, i: (bb, i, 0, 0)),
                  pl.BlockSpec((1, t, DIFF_QK), lambda bb, i: (bb, 0, 0)),
                  pl.BlockSpec((1, nt, DIFF_HEADS * DIFF_VROWS, TM), lambda bb, i: (bb, 0, 0, 0)),
                  pl.BlockSpec((1, 8, 128), lambda bb, i: (bb, 0, 0)),
                  _const_spec(lam_p.shape),
                  _const_spec((2 * HEAD_DIM, 1))],
        out_specs=pl.BlockSpec((1, 1, DIFF_V, TM), lambda bb, i: (bb, i, 0, 0)),
        out_shape=jax.ShapeDtypeStruct((b, nt, DIFF_V, TM), BF16),
        scratch_shapes=_attn_scratch(DIFF_VROWS),
        compiler_params=_params("parallel", "arbitrary"),
        name="diff_attn",
    )(dq, dk, dv, kmax, lam_p, gain.reshape(2 * HEAD_DIM, 1))


def _split_bf16(x):
    hi = x.astype(BF16)
    return hi, (x - hi.astype(F32)).astype(BF16)


def _gla_kernel(qk_ref, la_ref, vt_ref, o_ref, st_ref, *, backward):
    @pl.when(pl.program_id(1) == 0)
    def _():
        st_ref[...] = jnp.zeros(st_ref.shape, F32)

    n_chunk = TM // GLA_CHUNK
    r = lax.broadcasted_iota(jnp.int32, (TM, TM), 0)
    c = lax.broadcasted_iota(jnp.int32, (TM, TM), 1)
    same = (r // GLA_CHUNK) == (c // GLA_CHUNK)
    causal = (c >= r) if backward else (c <= r)
    tri = jnp.where(same & causal, 1.0, 0.0).astype(BF16)
    blk = jnp.where(same, 1.0, 0.0).astype(BF16)
    lane = lax.broadcasted_iota(jnp.int32, (TM, 128), 1)
    row = lax.broadcasted_iota(jnp.int32, (TM, 128), 0)
    keep = same & ((r >= c) if backward else (r <= c))
    order = range(n_chunk - 1, -1, -1) if backward else range(n_chunk)
    units = [(s, h) for s in range(GLA_PB) for h in range(GLA_HEADS)]
    pair = lambda h: slice((h // 2) * 128, (h // 2 + 1) * 128)
    q_in, k_in, k_end, dec = [], [], [], []
    for s in range(GLA_PB):
        la_hi, la_lo = _split_bf16(la_ref[s])
        cum = jnp.dot(tri, la_hi, preferred_element_type=F32) + jnp.dot(tri, la_lo, preferred_element_type=F32)
        tot = jnp.dot(blk, la_hi, preferred_element_type=F32) + jnp.dot(blk, la_lo, preferred_element_type=F32)
        q = qk_ref[s, :, :GLA_QK]
        k = qk_ref[s, :, GLA_QK:]
        q_in.append((q * jnp.exp(cum)).astype(BF16))
        k_in.append(k * jnp.exp(-cum))
        k_end.append(k * jnp.exp(tot - cum))
        dec.append(jnp.exp(tot))
    head_lanes = {h: (lane // GLA_DK) == (h % 2) for h in range(GLA_HEADS)}
    q_pair = {(s, h): q_in[s][:, pair(h)] for s, h in units}
    k_in_h = {(s, h): jnp.where(head_lanes[h], k_in[s][:, pair(h)], 0.0).astype(BF16) for s, h in units}
    k_end_h = {(s, h): jnp.where(head_lanes[h], k_end[s][:, pair(h)], 0.0) for s, h in units}
    vts = {(s, h): vt_ref[s, 0, h * GLA_DV:(h + 1) * GLA_DV, :] for s, h in units}
    a_ts = {u: jnp.where(keep, _nt(k_in_h[u], q_pair[u]), 0.0).astype(BF16) for u in units}
    o_ts = {u: jnp.dot(vts[u], a_ts[u], preferred_element_type=F32) for u in units}
    s_ts = {(s, h): st_ref[s, h] for s, h in units}
    for ci in order:
        in_chunk = (row // GLA_CHUNK) == ci
        for u in units:
            s, h = u
            q_c = jnp.where(in_chunk, q_pair[u], jnp.zeros_like(q_pair[u]))
            o_ts[u] = o_ts[u] + _nt(s_ts[u].astype(BF16), q_c)
            k_c = jnp.where(in_chunk, k_end_h[u], 0.0).astype(BF16)
            s_ts[u] = (s_ts[u] * dec[s][ci * GLA_CHUNK:ci * GLA_CHUNK + 1, pair(h)]
                       + jnp.dot(vts[u], k_c, preferred_element_type=F32))
    for s, h in units:
        st_ref[s, h] = s_ts[(s, h)]
        o_ref[s, 0, h * GLA_DV:(h + 1) * GLA_DV, :] = o_ts[(s, h)]


def _gla_direction(qk, la, vt, backward):
    b, t, _ = qk.shape
    nt = t // TM
    d = 1 if backward else 0
    tile = (lambda i: jnp.where(i == 0, 0, nt - i)) if backward else (lambda i: i)
    return pl.pallas_call(
        functools.partial(_gla_kernel, backward=backward),
        grid=(b // GLA_PB, nt),
        in_specs=[pl.BlockSpec((GLA_PB, TM, 2 * GLA_QK), lambda bb, i: (bb, tile(i), 0)),
                  pl.BlockSpec((GLA_PB, TM, GLA_QK), lambda bb, i: (bb, tile(i), d)),
                  pl.BlockSpec((GLA_PB, 1, GLA_V, TM), lambda bb, i: (bb, tile(i), 0, 0))],
        out_specs=pl.BlockSpec((GLA_PB, 1, GLA_V, TM), lambda bb, i: (bb, tile(i), 0, 0)),
        out_shape=jax.ShapeDtypeStruct((b, nt, GLA_V, TM), F32),
        scratch_shapes=[pltpu.VMEM((GLA_PB, GLA_HEADS, GLA_DV, 128), F32)],
        compiler_params=_params("parallel", "arbitrary"),
        name="gla_bwd" if backward else "gla_fwd",
    )(qk, la, vt)


def _post_phases(xs, mods, g, yas, ybs, woa_ref, wob_ref, w1_ref, w2_ref, ff_chunk):
    n = len(xs)
    ys = [jnp.dot(yas[s], woa_ref[...], preferred_element_type=F32)
          + jnp.dot(ybs[s].astype(F32).T.astype(BF16), wob_ref[...], preferred_element_type=F32) for s in range(n)]
    x1s = [xs[s] + mods[s][2:3] * _rms(ys[s], g[1:2]) for s in range(n)]
    hs = [(_rms(x1s[s], g[2:3]) * (1.0 + mods[s][4:5]) + mods[s][3:4]).astype(BF16) for s in range(n)]
    d_ff = w1_ref.shape[1]
    accs = [jnp.zeros(xs[0].shape, F32) for _ in range(n)]
    for c in range(d_ff // ff_chunk):
        cols = slice(c * ff_chunk, (c + 1) * ff_chunk)
        for s in range(n):
            u = jnp.dot(hs[s], w1_ref[:, cols], preferred_element_type=F32)
            u = jnp.square(jnp.maximum(u, 0.0)).astype(BF16)
            accs[s] = accs[s] + jnp.dot(u, w2_ref[cols, :], preferred_element_type=F32)
    return [x1s[s] + mods[s][5:6] * _rms(accs[s], g[3:4]) for s in range(n)]


def _post_gla_kernel(xs_ref, of_ref, ob_ref, gg_ref, gain_ref, yb_ref, mods_ref, g_ref, woa_ref, wob_ref, w1_ref,
                     w2_ref, o_ref, *, ff_chunk):
    gain = gain_ref[...]
    yas = []
    for s in range(POST_PB):
        parts = []
        for hh in range(GLA_HEADS):
            rows = slice(hh * GLA_DV, (hh + 1) * GLA_DV)
            o = of_ref[s, 0, rows, :] + ob_ref[s, 0, rows, :]
            gt = gg_ref[s, 0, rows, :]
            y = o * lax.rsqrt(jnp.mean(o * o, axis=0, keepdims=True) + NORM_EPS) * gain
            parts.append(y * (gt / (1.0 + jnp.exp(-gt))))
        yas.append(jnp.concatenate(parts, axis=0).T.astype(BF16))
    outs = _post_phases([xs_ref[s] for s in range(POST_PB)], [mods_ref[s] for s in range(POST_PB)], g_ref[...], yas,
                        [yb_ref[s, 0] for s in range(POST_PB)], woa_ref, wob_ref, w1_ref, w2_ref, ff_chunk)
    for s in range(POST_PB):
        o_ref[s] = outs[s]


def _post_kernel(xs_ref, ya_ref, yb_ref, mods_ref, g_ref, woa_ref, wob_ref, w1_ref, w2_ref, o_ref, *, ff_chunk):
    outs = _post_phases([xs_ref[s] for s in range(POST_PB)], [mods_ref[s] for s in range(POST_PB)], g_ref[...],
                        [ya_ref[s].astype(BF16) for s in range(POST_PB)], [yb_ref[s, 0] for s in range(POST_PB)],
                        woa_ref, wob_ref, w1_ref, w2_ref, ff_chunk)
    for s in range(POST_PB):
        o_ref[s] = outs[s]


def _post(xs, ya, yb_t, mods_l, g, w_out, w1, w2, skip_ctx):
    b, t, d = xs.shape
    nt = t // TM
    pb = POST_PB
    half = w_out.shape[0] // 2
    woa = w_out[:half].astype(BF16)
    wob = w_out[half:].astype(BF16)
    w1 = w1.astype(BF16)
    w2 = w2.astype(BF16)
    off = 1 if skip_ctx else 0
    tok = lambda n: pl.BlockSpec((pb, TM, n), lambda bb, i: (bb, i + off, 0))
    tr = lambda n: pl.BlockSpec((pb, 1, n, TM), lambda bb, i: (bb, i + off, 0, 0))
    if isinstance(ya, tuple):
        o_f, o_b, gg, gain = ya
        body, name = _post_gla_kernel, "post_gla"
        a_specs = [tr(half), tr(half), tr(half), _const_spec((GLA_DV, 1))]
        a_args = [o_f, o_b, gg, gain.reshape(GLA_DV, 1)]
    else:
        body, name = _post_kernel, "post"
        a_specs, a_args = [tok(half)], [ya]
    return pl.pallas_call(
        functools.partial(body, ff_chunk=1024),
        grid=(b // pb, nt - off),
        in_specs=[tok(d)] + a_specs + [
                  tr(half),
                  pl.BlockSpec((pb, N_MOD, d), lambda bb, i: (_mod_block(bb, i + off, pb), 0, 0)),
                  _const_spec(g.shape),
                  _const_spec(woa.shape), _const_spec(wob.shape), _const_spec(w1.shape), _const_spec(w2.shape)],
        out_specs=pl.BlockSpec((pb, TM, d), lambda bb, i: (bb, i, 0)),
        out_shape=jax.ShapeDtypeStruct((b, t - off * TM, d), F32),
        compiler_params=_params("parallel", "parallel"),
        name=name,
    )(xs, *a_args, yb_t, mods_l, g, woa, wob, w1, w2)


def _dot3(a, b):
    ah, al = _split_bf16(a)
    bh, bl = _split_bf16(b)
    d = lambda x, y: jnp.dot(x, y, preferred_element_type=F32)
    return d(ah, bh) + (d(al, bh) + d(ah, bl))


def _filter_kernel(w1t_ref, w1c_ref, w1s_ref, b1_ref, w2_ref, b2_ref, w3_ref, delta_ref, o_ref, *, seq, rb):
    tap = pl.program_id(0) * rb + lax.broadcasted_iota(jnp.int32, (rb, 1), 0)
    d = tap.astype(F32)
    t = d / max(seq - 1, 1)
    w = (2 * math.pi / seq) * d
    band = lax.broadcasted_iota(jnp.int32, (1, HY_BANDS), 1).astype(F32)
    bands = 1e-4 + band * ((HY_BANDS - 1 - 1e-4) / (HY_BANDS - 1))
    ang = w * bands
    pre = t * w1t_ref[...] + _dot3(jnp.cos(ang), w1c_ref[...]) - _dot3(jnp.sin(ang), w1s_ref[...]) + b1_ref[...]
    hid = jnp.sin(pre)
    hid = jnp.sin(_dot3(hid, w2_ref[...]) + b2_ref[...])
    decay = jnp.exp(-t * delta_ref[...])
    decay_bwd = jnp.where(tap == 0, 0.0, decay)
    for o in range(HY_ORDER):
        o_ref[2 * o] = (_dot3(hid, w3_ref[2 * o]) * decay).astype(o_ref.dtype)
        o_ref[2 * o + 1] = (_dot3(hid, w3_ref[2 * o + 1]) * decay_bwd).astype(o_ref.dtype)


def _hyena_kernels(seq, f_w1, f_b1, f_w2, f_b2, f_w3):
    rb = min(512, seq)
    ffn = f_w2.shape[0]
    w3 = f_w3.reshape(ffn, 2 * HY_ORDER, HY_CH).transpose(1, 0, 2)
    deltas = jnp.abs(jnp.linspace(math.log(HY_TARGET) / HY_FAST_PCT, math.log(HY_TARGET) / HY_SLOW_PCT,
                                  HY_CH, dtype=F32)).reshape(1, HY_CH)
    return pl.pallas_call(
        functools.partial(_filter_kernel, seq=seq, rb=rb),
        grid=(seq // rb,),
        in_specs=[_const_spec((1, ffn)), _const_spec((HY_BANDS, ffn)), _const_spec((HY_BANDS, ffn)),
                  _const_spec((1, ffn)), _const_spec((ffn, ffn)), _const_spec((1, ffn)),
                  _const_spec(w3.shape), _const_spec((1, HY_CH))],
        out_specs=pl.BlockSpec((2 * HY_ORDER, rb, HY_CH), lambda j: (0, j, 0)),
        out_shape=jax.ShapeDtypeStruct((2 * HY_ORDER, seq, HY_CH), BF16),
        compiler_params=_params("parallel"),
        name="hy_filter",
    )(f_w1[0:1], f_w1[1:1 + HY_BANDS], f_w1[1 + HY_BANDS:], f_b1.reshape(1, ffn), f_w2, f_b2.reshape(1, ffn), w3,
      deltas)


def _dft_tables(seq):
    n = 2 * seq
    r = math.isqrt(n)
    assert r * r == n
    r1 = (r // 2 + 1 + 7) // 8 * 8
    idx = jnp.arange(r, dtype=jnp.int32)
    phi = (2 * math.pi / r) * ((idx[:r1, None] * idx[None, :]) % r).astype(F32)
    f1 = jnp.concatenate([jnp.cos(phi), -jnp.sin(phi)], axis=0).astype(BF16)
    m = (idx[:r1, None, None] + r * idx[None, :, None]) * idx[None, None, :]
    th = (2 * math.pi / n) * (m % n).astype(F32)
    c, s = jnp.cos(th), jnp.sin(th)
    blockm = lambda re, im: jnp.concatenate([jnp.concatenate([re, -im], axis=-1),
                                             jnp.concatenate([im, re], axis=-1)], axis=-2)
    g = blockm(c, -s).astype(BF16)
    ginv = blockm(jnp.swapaxes(c, 1, 2), jnp.swapaxes(s, 1, 2)).astype(BF16)
    k1 = idx[:r1]
    weight = jnp.where((k1 == 0) | (k1 == r // 2), 1.0, jnp.where(k1 < r // 2, 2.0, 0.0)) / n
    phi_t = phi.T[:r // 2]
    finv = jnp.concatenate([jnp.cos(phi_t) * weight, -jnp.sin(phi_t) * weight], axis=1).astype(BF16)
    return r, f1, g, ginv, finv


DFT_BB = 32


def _dft1_kernel(f_ref, x_ref, o_ref, *, a0):
    f = f_ref[...]
    r = o_ref.shape[2]
    x = pltpu.einshape("abc->bac", x_ref[0, a0:].astype(F32))
    ys = [jnp.dot(f, x[j].astype(BF16), preferred_element_type=F32) for j in range(DFT_BB)]
    y = pltpu.einshape("jkc->kjc", jnp.stack(ys, axis=0))
    o_ref[0, 0] = y[:r].astype(o_ref.dtype)
    o_ref[0, 1] = y[r:].astype(o_ref.dtype)


def _dft_stage1(f, x4, a0=0):
    b, rows, r, c = x4.shape
    r1 = f.shape[0] // 2
    return pl.pallas_call(
        functools.partial(_dft1_kernel, a0=a0),
        grid=(b, r // DFT_BB),
        in_specs=[_const_spec(f.shape), pl.BlockSpec((1, rows, DFT_BB, c), lambda bi, j: (bi, 0, j, 0))],
        out_specs=pl.BlockSpec((1, 2, r1, DFT_BB, c), lambda bi, j: (bi, 0, 0, j, 0)),
        out_shape=jax.ShapeDtypeStruct((b, 2, r1, r, c), BF16),
        compiler_params=_params("parallel", "parallel"),
        name="hy_dft1",
    )(f, x4)


def _fspec_kernel(af_ref, ab_ref, g_ref, o_ref, *, kb, r):
    for kk in range(kb):
        xf = jnp.dot(g_ref[kk], af_ref[0, :, kk].reshape(2 * r, af_ref.shape[-1]), preferred_element_type=F32)
        xb = jnp.dot(g_ref[kk], ab_ref[0, :, kk].reshape(2 * r, ab_ref.shape[-1]), preferred_element_type=F32)
        o_ref[0, kk, :r] = xf[:r] + xb[:r]
        o_ref[0, kk, r:] = xf[r:] - xb[r:]


def _filter_spectrum(a, g, kb):
    o2, _, r1, r, c = a.shape
    blk = lambda d: pl.BlockSpec((1, 2, kb, r, c), lambda oo, j: (2 * oo + d, 0, j, 0, 0))
    return pl.pallas_call(
        functools.partial(_fspec_kernel, kb=kb, r=r),
        grid=(o2 // 2, r1 // kb),
        in_specs=[blk(0), blk(1), pl.BlockSpec((kb, 2 * r, 2 * r), lambda oo, j: (j, 0, 0))],
        out_specs=pl.BlockSpec((1, kb, 2 * r, c), lambda oo, j: (oo, j, 0, 0)),
        out_shape=jax.ShapeDtypeStruct((o2 // 2, r1, 2 * r, c), F32),
        compiler_params=_params("parallel", "parallel"),
        name="hy_fspec",
    )(a, a, g)


def _dft2_kernel(a_ref, g_ref, gi_ref, k_ref, o_ref, *, kb, r):
    xs = [jnp.dot(g_ref[kk], a_ref[0, :, kk].reshape(2 * r, a_ref.shape[-1]), preferred_element_type=F32)
          for kk in range(kb)]
    ys = []
    for kk in range(kb):
        x, ks = xs[kk], k_ref[0, kk]
        xr, xi, kr, ki = x[:r], x[r:], ks[:r], ks[r:]
        ys.append(jnp.concatenate([xr * kr - xi * ki, xr * ki + xi * kr], axis=0).astype(BF16))
    bqs = [jnp.dot(gi_ref[kk], ys[kk], preferred_element_type=F32) for kk in range(kb)]
    for kk in range(kb):
        o_ref[0, 0, kk] = bqs[kk][:r].astype(o_ref.dtype)
        o_ref[0, 1, kk] = bqs[kk][r:].astype(o_ref.dtype)


def _dft_stage2(a, g, ginv, kspec, order, kb):
    b, _, r1, r, c = a.shape
    blk = pl.BlockSpec((1, 2, kb, r, c), lambda j, bb: (bb, 0, j, 0, 0))
    tab = pl.BlockSpec((kb, 2 * r, 2 * r), lambda j, bb: (j, 0, 0))
    return pl.pallas_call(
        functools.partial(_dft2_kernel, kb=kb, r=r),
        grid=(r1 // kb, b),
        in_specs=[blk, tab, tab, pl.BlockSpec((1, kb, 2 * r, c), lambda j, bb: (order, j, 0, 0))],
        out_specs=blk,
        out_shape=jax.ShapeDtypeStruct(a.shape, BF16),
        compiler_params=_params("parallel", "arbitrary"),
        name="hy_dft2",
    )(a, g, ginv, kspec)


def _dft3_kernel(f_ref, b_ref, gate_ref, z_ref, bias_ref, head_ref, o_ref, *, a0):
    f = f_ref[...]
    bq = jnp.concatenate([b_ref[0, 0], b_ref[0, 1]], axis=0)
    bq = pltpu.einshape("kjc->jkc", bq.astype(F32))
    ys = [jnp.dot(f, bq[j].astype(BF16), preferred_element_type=F32) for j in range(DFT_BB)]
    y = pltpu.einshape("jac->ajc", jnp.stack(ys, axis=0))
    o_ref[0, :a0] = head_ref[0]
    o_ref[0, a0:] = gate_ref[0, a0:] * (y + z_ref[0, a0:] * bias_ref[...])


def _dft_stage3(finv, bq5, gate4, z4, bias, head4):
    b, _, r1, r, c = bq5.shape
    rows = z4.shape[1]
    a0 = head4.shape[1]
    io = pl.BlockSpec((1, rows, DFT_BB, c), lambda bi, j: (bi, 0, j, 0))
    return pl.pallas_call(
        functools.partial(_dft3_kernel, a0=a0),
        grid=(b, r // DFT_BB),
        in_specs=[_const_spec(finv.shape), pl.BlockSpec((1, 2, r1, DFT_BB, c), lambda bi, j: (bi, 0, 0, j, 0)), io, io,
                  _const_spec((1, 1, c)), pl.BlockSpec((1, a0, DFT_BB, c), lambda bi, j: (bi, 0, j, 0))],
        out_specs=io,
        out_shape=jax.ShapeDtypeStruct((b, rows, r, c), F32),
        compiler_params=_params("parallel", "parallel"),
        name="hy_dft3",
    )(finv, bq5, gate4, z4, bias, head4)


def _long_conv_gated(v, gates, kern, hy_bias, ctx_out, lc):
    b, t, c = v.shape
    r, f1, g, ginv, finv = _dft_tables(t - lc)
    assert lc % r == 0
    a0 = lc // r
    kb = 8
    kspec = _filter_spectrum(_dft_stage1(f1[:, :r // 2], kern.reshape(2 * HY_ORDER, r // 2, r, c)), g, kb)
    as_blocks = lambda x: x.reshape(b, t // r, r, c)
    z = as_blocks(v)
    heads = [jnp.zeros((b, a0, r, c), F32), ctx_out.reshape(b, a0, r, c)]
    for o in range(HY_ORDER):
        a = _dft_stage1(f1[:, :r // 2], z, a0)
        bq = _dft_stage2(a, g, ginv, kspec, o, kb)
        z = _dft_stage3(finv, bq, as_blocks(gates[o]), z, hy_bias[o].reshape(1, 1, c), heads[o])
    return z.reshape(b, t, c)


def _ctx_conv_kernel(v_ref, x1_ref, x2_ref, k_ref, ff_ref, fi_ref, bias_ref, o_ref):
    n = ff_ref.shape[1]
    z = v_ref[0]
    for o, gate_ref in enumerate((x1_ref, x2_ref)):
        kf = jnp.dot(ff_ref[:, :TM], k_ref[2 * o], preferred_element_type=F32)
        kb = jnp.dot(ff_ref[:, :TM], k_ref[2 * o + 1], preferred_element_type=F32)
        x = jnp.dot(ff_ref[:, :TM], z.astype(BF16), preferred_element_type=F32)
        xr, xi, kr, ki = x[:n], x[n:], kf[:n] + kb[:n], kf[n:] - kb[n:]
        y = jnp.concatenate([xr * kr - xi * ki, xr * ki + xi * kr], axis=0).astype(BF16)
        conv = jnp.dot(fi_ref[...], y, preferred_element_type=F32)
        z = gate_ref[0] * (conv + z * bias_ref[o:o + 1])
    o_ref[0] = z


def _ctx_conv_gated(v, x1, x2, kern, hy_bias):
    b, _, c = v.shape
    seq = TM
    n = 2 * seq
    idx = jnp.arange(n, dtype=jnp.int32)
    th = (2 * math.pi / n) * ((idx[:, None] * idx[None, :]) % n).astype(F32)
    ff = jnp.concatenate([jnp.cos(th), -jnp.sin(th)], axis=0).astype(BF16)
    fi = (jnp.concatenate([jnp.cos(th), -jnp.sin(th)], axis=1)[:seq] / n).astype(BF16)
    io = pl.BlockSpec((1, seq, c), lambda bb: (bb, 0, 0))
    return pl.pallas_call(
        _ctx_conv_kernel,
        grid=(b,),
        in_specs=[io, io, io, _const_spec(kern.shape), _const_spec(ff.shape), _const_spec(fi.shape),
                  _const_spec(hy_bias.shape)],
        out_specs=io,
        out_shape=jax.ShapeDtypeStruct((b, seq, c), F32),
        compiler_params=_params("parallel"),
        name="hy_ctx",
    )(v, x1, x2, kern, ff, fi, hy_bias)


def _hyena(v, x1, x2, lc, f_w1, f_b1, f_w2, f_b2, f_w3, hy_bias):
    seq = v.shape[1] - lc
    yc = _ctx_conv_gated(v, x1, x2, _hyena_kernels(lc, f_w1, f_b1, f_w2, f_b2, f_w3), hy_bias)
    return _long_conv_gated(v, (x1, x2), _hyena_kernels(seq, f_w1, f_b1, f_w2, f_b2, f_w3), hy_bias, yc, lc)


def _rope_tables_t(seq, lc):
    rows = seq // GRID_W
    r = jnp.broadcast_to(jnp.arange(rows, dtype=F32)[:, None], (rows, GRID_W)).reshape(-1)
    col = jnp.broadcast_to(jnp.arange(GRID_W, dtype=F32)[None, :], (rows, GRID_W)).reshape(-1)
    n_pairs = HEAD_DIM // 4
    inv = ROPE_THETA ** (-jnp.arange(n_pairs, dtype=F32) / n_pairs)
    ang = jnp.concatenate([r[:, None] * inv, col[:, None] * inv], axis=-1)
    cos = jnp.concatenate([jnp.ones((lc, 2 * n_pairs), F32), jnp.cos(ang)], axis=0)
    sin = jnp.concatenate([jnp.zeros((lc, 2 * n_pairs), F32), jnp.sin(ang)], axis=0)
    return cos.T, sin.T


def kernel(x, c, ctx, c_ctx, w_ada, b_ada, norm_g, w_mlp_in, w_mlp_out, ev_w_in, ev_w_lr, ev_b_lr, ev_gla_g,
           ev_qk_g, ev_w_out, od_w_in, od_conv_w, od_conv_b, od_f_w1, od_f_b1, od_f_w2, od_f_b2, od_f_w3,
           od_hy_bias, od_lam, od_diff_g, od_w_out):
    bsz, seq, d = x.shape
    lc = ctx.shape[1]
    depth = w_ada.shape[0]
    assert lc == TM and seq % TM == 0 and bsz <= MOD_CTX_ROW and bsz % POST_PB == 0 and bsz % IN_PB == 0 and bsz % GLA_PB == 0
    cos_t, sin_t = _rope_tables_t(seq, lc)
    cc = jnp.zeros((8, d), F32).at[:bsz].set(c).at[MOD_CTX_ROW:MOD_CTX_ROW + max(POST_PB, IN_PB)].set(c_ctx)
    mods = _mods(cc, w_ada, b_ada).reshape(depth, 8, N_MOD, d)
    xs = jnp.concatenate([ctx, x], axis=1)
    for i in range(depth):
        last = i == depth - 1
        j = i // 2
        g = norm_g[i]
        if i % 2 == 0:
            qk, la, gv, gg, aq, ak, av, kn = _even_in(xs, mods[i], g[0:1], ev_w_in[j], ev_w_lr[j], ev_b_lr[j],
                                                  ev_qk_g[j], cos_t, sin_t)
            ya = (_gla_direction(qk, la, gv, False), _gla_direction(qk, la, gv, True), gg, ev_gla_g[j])
            yb = _gqa_attention(aq, ak, av, jnp.max(kn, axis=1))
            w_out = ev_w_out[j]
        else:
            hv, hx1, hx2, dq, dk, dv, kn = _odd_in(xs, mods[i], g[0:1], od_w_in[j], od_conv_w[j], od_conv_b[j],
                                                   cos_t, sin_t)
            ya = _hyena(hv, hx1, hx2, lc, od_f_w1[j], od_f_b1[j], od_f_w2[j], od_f_b2[j], od_f_w3[j], od_hy_bias[j])
            yb = _diff_attention(dq, dk, dv, jnp.max(kn, axis=1), od_lam[j], od_diff_g[j], i)
            w_out = od_w_out[j]
        xs = _post(xs, ya, yb, mods[i], g, w_out, w_mlp_in[i], w_mlp_out[i], skip_ctx=last)
    return xs
```

```python
import functools
import math

import jax
import jax.numpy as jnp
import numpy as np
from jax import lax
from jax.experimental import pallas as pl
from jax.experimental.pallas import tpu as pltpu

F32 = jnp.float32
BF16 = jnp.bfloat16

TM = 256
GRID_W = 64
HEAD_DIM = 64
ROPE_THETA = 10000.0
NORM_EPS = 1e-6
N_MOD = 6
VMEM_LIMIT = 56 * 1024 * 1024

GLA_HEADS = 4
GLA_DK = 64
GLA_DV = 128
GLA_RANK = 16
GLA_NORMALIZER = 16.0
GLA_CHUNK = 64
GLA_QK = GLA_HEADS * GLA_DK
GLA_V = GLA_HEADS * GLA_DV
GQA_HEADS = 8
GQA_KV_HEADS = 2
HY_CH = 512
HY_ORDER = 2
HY_BANDS = 16
HY_TARGET = 1e-2
HY_FAST_PCT = 0.3
HY_SLOW_PCT = 1.5
HY_COLS = (HY_ORDER + 1) * HY_CH
DIFF_HEADS = 4
DIFF_QK = DIFF_HEADS * 2 * HEAD_DIM
DIFF_V = DIFF_HEADS * 2 * HEAD_DIM
ONES_ROWS = 16
N_SCORE_HEADS = 8
LOG2E = 1.4426950408889634
Q_SCALE = HEAD_DIM ** -0.5 * LOG2E
GQA_VROWS = HEAD_DIM + ONES_ROWS
SCORE_BOUND_SLACK = 1.01
FAST_UNROLL_MAX = 32
SCORE_BOUND_LIMIT = 60.0
DIFF_VROWS = 2 * HEAD_DIM + ONES_ROWS


def _params(*sem):
    return pltpu.CompilerParams(dimension_semantics=sem, vmem_limit_bytes=VMEM_LIMIT)


def _const_spec(shape):
    nd = len(shape)
    return pl.BlockSpec(shape, lambda *_: (0,) * nd, pipeline_mode=pl.Buffered(1))


def _rms(x, g):
    return x * lax.rsqrt(jnp.mean(x * x, axis=-1, keepdims=True) + NORM_EPS) * g


MOD_CTX_ROW = 4
POST_PB = 2
IN_PB = 4
GLA_PB = 4


def _mod_block(bb, i, pb):
    return jnp.where(i == 0, MOD_CTX_ROW // pb, bb)


def _stream_specs(xs, pb, off=0):
    if isinstance(xs, tuple):
        x, ctx = xs
        d = x.shape[-1]
        return [pl.BlockSpec((pb, TM, d), lambda bb, i: (bb, jnp.maximum(i + off - 1, 0), 0)),
                pl.BlockSpec((pb, TM, d), lambda bb, i: (bb, 0, 0))], [x, ctx]
    return [pl.BlockSpec((pb, TM, xs.shape[-1]), lambda bb, i: (bb, i + off, 0))], [xs]


def _stream_shape(xs):
    if isinstance(xs, tuple):
        return xs[0].shape[0], xs[0].shape[1] + xs[1].shape[1], xs[0].shape[2]
    return xs.shape


def _stream_tile(stream_refs, s, tile):
    if len(stream_refs) == 2:
        return jnp.where(tile == 0, stream_refs[1][s], stream_refs[0][s])
    return stream_refs[0][s]


def _mods_kernel(cc_ref, w_ref, b_ref, o_ref):
    cc = cc_ref[...]
    s = cc / (1.0 + jnp.exp(-cc))
    o_ref[0] = jnp.dot(s.astype(BF16), w_ref[0].astype(BF16), preferred_element_type=F32) + b_ref[0]


def _mods(cc, w_ada, b_ada):
    depth, d, n = w_ada.shape
    nb = n // 4
    return pl.pallas_call(
        _mods_kernel,
        grid=(depth, n // nb),
        in_specs=[pl.BlockSpec((8, d), lambda l, j: (0, 0)),
                  pl.BlockSpec((1, d, nb), lambda l, j: (l, 0, j)),
                  pl.BlockSpec((1, 1, nb), lambda l, j: (l, 0, j))],
        out_specs=pl.BlockSpec((1, 8, nb), lambda l, j: (l, 0, j)),
        out_shape=jax.ShapeDtypeStruct((depth, 8, n), F32),
        compiler_params=_params("parallel", "parallel"),
        name="ada_mods",
    )(cc, w_ada, b_ada.reshape(depth, 1, n))


def _normed_input(x, mods, g):
    return (_rms(x, g) * (1.0 + mods[1:2]) + mods[0:1]).astype(BF16)


def _rope_rows(y, cos, sin):
    x1, x2 = y[:32], y[32:]
    return x1 * cos - x2 * sin, x1 * sin + x2 * cos


def _log_sigmoid(x):
    return jnp.minimum(x, 0.0) - jnp.log(1.0 + jnp.exp(-jnp.abs(x)))


def _nt(a, b):
    return lax.dot_general(a, b, (((1,), (1,)), ((), ())), preferred_element_type=F32)


def _max_sq_norm_row(parts):
    n2 = sum(jnp.sum(jnp.square(p.astype(BF16).astype(F32)), axis=0, keepdims=True) for p in parts)
    return jnp.broadcast_to(jnp.max(n2, axis=1, keepdims=True), (1, 128))


def _even_in_kernel(*refs, n_stream):
    (mods_ref, g_ref, wn_ref, wt_ref, wlr_ref, blr_ref, gq_ref, gk_ref, cos_ref, sin_ref,
     qk_ref, la_ref, gv_ref, gg_ref, aq_ref, ak_ref, av_ref, kn_ref) = refs[n_stream:]
    tile = pl.program_id(1)
    hs = [_normed_input(_stream_tile(refs[:n_stream], s, tile), mods_ref[s], g_ref[0:1]) for s in range(IN_PB)]
    ps = [jnp.dot(h, wn_ref[...], preferred_element_type=F32) for h in hs]
    pts = [_nt(wt_ref[...], h) for h in hs]
    cos, sin = cos_ref[...], sin_ref[...]
    gq, gk_gain = gq_ref[...], gk_ref[...]
    ones = jnp.ones((ONES_ROWS, TM), BF16)
    for s in range(IN_PB):
        p, pt_all = ps[s], pts[s]
        qk_ref[s, :, :GLA_QK] = p[:, :GLA_QK] * GLA_DK ** -0.5
        qk_ref[s, :, GLA_QK:] = p[:, GLA_QK:2 * GLA_QK]
        lr = p[:, 2 * GLA_QK:].astype(BF16)
        gk = jnp.dot(lr, wlr_ref[...], preferred_element_type=F32) + blr_ref[...]
        la_ref[s] = _log_sigmoid(gk) * (1.0 / GLA_NORMALIZER)
        gv_ref[s, 0] = pt_all[:GLA_V].astype(BF16)
        gg_ref[s, 0] = pt_all[GLA_V:2 * GLA_V]
        pt = pt_all[2 * GLA_V:]
        for hh in range(GQA_HEADS):
            blk = pt[hh * 64:(hh + 1) * 64]
            y = blk * lax.rsqrt(jnp.mean(blk * blk, axis=0, keepdims=True) + NORM_EPS) * gq
            o1, o2 = _rope_rows(y, cos, sin)
            aq_ref[s, 0, hh * 64:hh * 64 + 32, :] = (o1 * Q_SCALE).astype(BF16)
            aq_ref[s, 0, hh * 64 + 32:hh * 64 + 64, :] = (o2 * Q_SCALE).astype(BF16)
        kparts, knorms = [], []
        for hh in range(GQA_KV_HEADS):
            blk = pt[512 + hh * 64:512 + (hh + 1) * 64]
            y = blk * lax.rsqrt(jnp.mean(blk * blk, axis=0, keepdims=True) + NORM_EPS) * gk_gain
            o1, o2 = _rope_rows(y, cos, sin)
            kparts += [o1, o2]
            knorms.append(_max_sq_norm_row([o1, o2]))
        ak_ref[s] = jnp.concatenate(kparts, axis=0).T.astype(BF16)
        kn_ref[s, 0] = jnp.concatenate(knorms + [jnp.zeros((8 - GQA_KV_HEADS, 128), F32)], axis=0)
        for hh in range(GQA_KV_HEADS):
            av_ref[s, 0, hh * GQA_VROWS:hh * GQA_VROWS + 64, :] = pt[640 + hh * 64:640 + (hh + 1) * 64].astype(BF16)
            av_ref[s, 0, hh * GQA_VROWS + 64:(hh + 1) * GQA_VROWS, :] = ones


def _odd_in_kernel(xs_ref, prev_ref, next_ref, mods_ref, g_ref, wn_ref, wt_ref, cw_ref, cb_ref, cos_ref, sin_ref,
                   v_ref, x1_ref, x2_ref, dq_ref, dk_ref, dv_ref, kn_ref, *, n_tiles):
    i = pl.program_id(1)
    hs = [_normed_input(xs_ref[s], mods_ref[s], g_ref[0:1]) for s in range(IN_PB)]
    h_ext = [jnp.concatenate([hs[s], _normed_input(prev_ref[s], mods_ref[s], g_ref[0:1]),
                              _normed_input(next_ref[s], mods_ref[s], g_ref[0:1])], axis=0) for s in range(IN_PB)]
    us = [jnp.dot(h, wn_ref[...], preferred_element_type=F32) for h in h_ext]
    pts = [_nt(wt_ref[...], h) for h in hs]
    cos, sin = cos_ref[...], sin_ref[...]
    ones = jnp.ones((ONES_ROWS, TM), BF16)
    nv = 2 * HEAD_DIM
    cw = cw_ref[...]
    row_id = lax.broadcasted_iota(jnp.int32, (TM, HY_COLS), 0)
    no_prev = (i == 0) | (i == 1)
    no_next = (i == 0) | (i == n_tiles - 1)
    for s in range(IN_PB):
        pt = pts[s]
        u = us[s][:TM]
        prev_row = jnp.where(no_prev, 0.0, us[s][TM + 7:TM + 8])
        next_row = jnp.where(no_next, 0.0, us[s][TM + 8:TM + 9])
        up = jnp.where(row_id == 0, prev_row, pltpu.roll(u, 1, axis=0))
        dn = jnp.where(row_id == TM - 1, next_row, pltpu.roll(u, TM - 1, axis=0))
        conv = up * cw[0:1] + u * cw[1:2] + dn * cw[2:3] + cb_ref[...]
        v_ref[s] = conv[:, :HY_CH]
        x1_ref[s] = conv[:, HY_CH:2 * HY_CH]
        x2_ref[s] = conv[:, 2 * HY_CH:]
        kparts, knorms = [], []
        for hh in range(2 * DIFF_HEADS):
            o1, o2 = _rope_rows(pt[hh * 64:(hh + 1) * 64], cos, sin)
            dq_ref[s, 0, hh * 64:hh * 64 + 32, :] = (o1 * Q_SCALE).astype(BF16)
            dq_ref[s, 0, hh * 64 + 32:hh * 64 + 64, :] = (o2 * Q_SCALE).astype(BF16)
            k1, k2 = _rope_rows(pt[DIFF_QK + hh * 64:DIFF_QK + (hh + 1) * 64], cos, sin)
            kparts += [k1, k2]
            knorms.append(_max_sq_norm_row([k1, k2]))
        dk_ref[s] = jnp.concatenate(kparts, axis=0).T.astype(BF16)
        kn_ref[s, 0] = jnp.concatenate(knorms, axis=0)
        for hh in range(DIFF_HEADS):
            rows = pt[2 * DIFF_QK + hh * nv:2 * DIFF_QK + (hh + 1) * nv]
            dv_ref[s, 0, hh * DIFF_VROWS:hh * DIFF_VROWS + nv, :] = rows.astype(BF16)
            dv_ref[s, 0, hh * DIFF_VROWS + nv:(hh + 1) * DIFF_VROWS, :] = ones


def _pair_perm(n_heads):
    base = np.concatenate([np.arange(0, HEAD_DIM, 2), np.arange(1, HEAD_DIM, 2)])
    return np.concatenate([h * HEAD_DIM + base for h in range(n_heads)])


def _even_in(xs, mods_l, g0, w_in, w_lr, b_lr, qk_g, cos_t, sin_t):
    b, t, d = _stream_shape(xs)
    nt = t // TM
    x_specs, x_args = _stream_specs(xs, IN_PB)
    c_v = 2 * GLA_QK
    c_g = c_v + GLA_V
    c_lr0 = c_g + GLA_V
    c_lr = c_lr0 + 2 * GLA_RANK
    wn = jnp.concatenate([w_in[:, :c_v], w_in[:, c_lr0:c_lr], jnp.zeros((d, 128 - 2 * GLA_RANK), F32)],
                         axis=1).astype(BF16)
    cq = c_lr + GQA_HEADS * HEAD_DIM
    ck = cq + GQA_KV_HEADS * HEAD_DIM
    wq = w_in[:, c_lr:cq][:, _pair_perm(GQA_HEADS)]
    wk = w_in[:, cq:ck][:, _pair_perm(GQA_KV_HEADS)]
    wt = jnp.concatenate([w_in[:, c_v:c_lr0], wq, wk, w_in[:, ck:]], axis=1).T.astype(BF16)
    wlr = jnp.zeros((128, 2 * GLA_QK), F32)
    wlr = wlr.at[:GLA_RANK, :GLA_QK].set(w_lr[0]).at[GLA_RANK:2 * GLA_RANK, GLA_QK:].set(w_lr[1]).astype(BF16)
    blr = b_lr.reshape(1, 2 * GLA_QK)
    perm = _pair_perm(1)
    gq = qk_g[0][perm].reshape(HEAD_DIM, 1)
    gk = qk_g[1][perm].reshape(HEAD_DIM, 1)
    tok = lambda n: pl.BlockSpec((IN_PB, TM, n), lambda bb, i: (bb, i, 0))
    tr = lambda n: pl.BlockSpec((IN_PB, 1, n, TM), lambda bb, i: (bb, i, 0, 0))
    return pl.pallas_call(
        functools.partial(_even_in_kernel, n_stream=len(x_args)),
        grid=(b // IN_PB, nt),
        in_specs=x_specs + [
                  pl.BlockSpec((IN_PB, N_MOD, d), lambda bb, i: (_mod_block(bb, i, IN_PB), 0, 0)),
                  _const_spec((1, d)),
                  _const_spec(wn.shape), _const_spec(wt.shape), _const_spec(wlr.shape), _const_spec(blr.shape),
                  _const_spec(gq.shape), _const_spec(gk.shape),
                  pl.BlockSpec((32, TM), lambda bb, i: (0, i)),
                  pl.BlockSpec((32, TM), lambda bb, i: (0, i))],
        out_specs=[tok(2 * GLA_QK), tok(2 * GLA_QK), tr(GLA_V), tr(GLA_V), tr(512), tok(128),
                   tr(GQA_KV_HEADS * GQA_VROWS), pl.BlockSpec((IN_PB, 1, 8, 128), lambda bb, i: (bb, i, 0, 0))],
        out_shape=[jax.ShapeDtypeStruct((b, t, 2 * GLA_QK), F32),
                   jax.ShapeDtypeStruct((b, t, 2 * GLA_QK), F32),
                   jax.ShapeDtypeStruct((b, nt, GLA_V, TM), BF16),
                   jax.ShapeDtypeStruct((b, nt, GLA_V, TM), F32),
                   jax.ShapeDtypeStruct((b, nt, 512, TM), BF16),
                   jax.ShapeDtypeStruct((b, t, 128), BF16),
                   jax.ShapeDtypeStruct((b, nt, GQA_KV_HEADS * GQA_VROWS, TM), BF16),
                   jax.ShapeDtypeStruct((b, nt, 8, 128), F32)],
        compiler_params=_params("parallel", "parallel"),
        name="even_in",
    )(*x_args, mods_l, g0, wn, wt, wlr, blr, gq, gk, cos_t, sin_t)


def _odd_in(xs, mods_l, g0, w_in, conv_w, conv_b, cos_t, sin_t):
    b, t, d = xs.shape
    nt = t // TM
    r8 = TM // 8
    last_blk = t // 8 - 1
    wn = w_in[:, :HY_COLS].astype(BF16)
    perm = _pair_perm(2 * DIFF_HEADS)
    wq = w_in[:, HY_COLS:HY_COLS + DIFF_QK][:, perm]
    wk = w_in[:, HY_COLS + DIFF_QK:HY_COLS + 2 * DIFF_QK][:, perm]
    wt = jnp.concatenate([wq, wk, w_in[:, HY_COLS + 2 * DIFF_QK:]], axis=1).T.astype(BF16)
    tok = lambda n: pl.BlockSpec((IN_PB, TM, n), lambda bb, i: (bb, i, 0))
    tr = lambda n: pl.BlockSpec((IN_PB, 1, n, TM), lambda bb, i: (bb, i, 0, 0))
    hy = jax.ShapeDtypeStruct((b, t, HY_CH), F32)
    return pl.pallas_call(
        functools.partial(_odd_in_kernel, n_tiles=nt),
        grid=(b // IN_PB, nt),
        in_specs=[tok(d),
                  pl.BlockSpec((IN_PB, 8, d), lambda bb, i: (bb, jnp.maximum(i * r8 - 1, 0), 0)),
                  pl.BlockSpec((IN_PB, 8, d), lambda bb, i: (bb, jnp.minimum((i + 1) * r8, last_blk), 0)),
                  pl.BlockSpec((IN_PB, N_MOD, d), lambda bb, i: (_mod_block(bb, i, IN_PB), 0, 0)),
                  _const_spec((1, d)),
                  _const_spec(wn.shape), _const_spec(wt.shape),
                  _const_spec(conv_w.shape), _const_spec((1, HY_COLS)),
                  pl.BlockSpec((32, TM), lambda bb, i: (0, i)),
                  pl.BlockSpec((32, TM), lambda bb, i: (0, i))],
        out_specs=[tok(HY_CH), tok(HY_CH), tok(HY_CH), tr(DIFF_QK), tok(DIFF_QK), tr(DIFF_HEADS * DIFF_VROWS),
                   pl.BlockSpec((IN_PB, 1, 8, 128), lambda bb, i: (bb, i, 0, 0))],
        out_shape=[hy, hy, hy,
                   jax.ShapeDtypeStruct((b, nt, DIFF_QK, TM), BF16),
                   jax.ShapeDtypeStruct((b, t, DIFF_QK), BF16),
                   jax.ShapeDtypeStruct((b, nt, DIFF_HEADS * DIFF_VROWS, TM), BF16),
                   jax.ShapeDtypeStruct((b, nt, 8, 128), F32)],
        compiler_params=_params("parallel", "parallel"),
        name="odd_in",
    )(xs, xs, xs, mods_l, g0, wn, wt, conv_w, conv_b.reshape(1, HY_COLS), cos_t, sin_t)


def _attn_core(q_ref, k_ref, v_ref, kmax_ref, scratch, *, n_kv, k_col, k_row, q_slot, v_row, v_rows, scores_between,
               vpu_denominator):
    qpad_ref, bound_ref, m_ref, acc_ref, s0_ref, s1_ref, c0_ref, c1_ref, p0_ref, p1_ref = scratch
    s_refs, c_refs, p_refs = (s0_ref, s1_ref), (c0_ref, c1_ref), (p0_ref, p1_ref)
    i = pl.program_id(1)
    zeros = jnp.zeros((HEAD_DIM, TM), BF16)
    bound_max = jnp.zeros((1, TM), F32)
    for hh in range(N_SCORE_HEADS):
        qh = q_ref[0, 0, hh * HEAD_DIM:(hh + 1) * HEAD_DIM, :]
        qpad_ref[hh] = jnp.concatenate([qh, zeros] if q_slot(hh) == 0 else [zeros, qh], axis=0)
        qf = qh.astype(F32)
        q_norm2 = jnp.sum(qf * qf, axis=0, keepdims=True)
        k_norm2 = kmax_ref[0, k_row(hh):k_row(hh) + 1, 0:1]
        bound = jnp.sqrt(q_norm2 * k_norm2) * SCORE_BOUND_SLACK
        bound_ref[hh] = bound
        bound_max = jnp.maximum(bound_max, bound)
    acc_ref[...] = jnp.zeros(acc_ref.shape, F32)
    n = jnp.where(i == 0, 1, n_kv)
    fixed_reference = jnp.max(bound_max) <= SCORE_BOUND_LIMIT

    def produce(j, hh, slot):
        kc = k_ref[0, pl.ds(pl.multiple_of(j * TM, TM), TM), k_col(hh):k_col(hh) + 128]
        s = jnp.dot(kc, qpad_ref[hh], preferred_element_type=F32)
        p = jnp.exp2(s - bound_ref[hh])
        if vpu_denominator:
            m_ref[hh] = m_ref[hh] + jnp.sum(p, axis=0, keepdims=True)
        p_refs[slot][hh] = p.astype(BF16)

    n_val = v_rows - ONES_ROWS if vpu_denominator else v_rows

    def reduce(j, hh, slot):
        vc = v_ref[0, j, v_row(hh):v_row(hh) + n_val, :]
        acc_ref[hh, :n_val] = acc_ref[hh, :n_val] + jnp.dot(vc, p_refs[slot][hh], preferred_element_type=F32)

    unroll = math.gcd(n_kv - 1, FAST_UNROLL_MAX)

    @pl.when(fixed_reference)
    def _():
        m_ref[...] = jnp.zeros(m_ref.shape, F32)
        for hh in range(N_SCORE_HEADS):
            produce(0, hh, 0)

        def fast_step(j, slot):
            for hh in range(N_SCORE_HEADS):
                produce(j + 1, hh, 1 - slot)
                reduce(j, hh, slot)

        def fast_body(jj, carry):
            for u in range(unroll):
                fast_step(unroll * jj + u, u % 2)
            return carry

        lax.fori_loop(0, (n - 1) // unroll, fast_body, 0)
        for hh in range(N_SCORE_HEADS):
            reduce(n - 1, hh, 0)
            if vpu_denominator:
                acc_ref[hh, n_val:, :] = jnp.broadcast_to(m_ref[hh], (ONES_ROWS, TM))

    @pl.when(jnp.logical_not(fixed_reference))
    def _():
        _attn_running_max(k_ref, v_ref, qpad_ref, m_ref, acc_ref, s_refs, c_refs, n=n, k_col=k_col, v_row=v_row,
                          v_rows=v_rows, scores_between=scores_between)


def _attn_running_max(k_ref, v_ref, qpad_ref, m_ref, acc_ref, s_refs, c_refs, *, n, k_col, v_row, v_rows,
                      scores_between):
    m_ref[...] = jnp.full(m_ref.shape, -jnp.inf, F32)

    def scores(j, hh, slot):
        kc = k_ref[0, pl.ds(pl.multiple_of(j * TM, TM), TM), k_col(hh):k_col(hh) + 128]
        s = jnp.dot(kc, qpad_ref[hh], preferred_element_type=F32)
        s_refs[slot][hh] = s
        c_refs[slot][hh] = jnp.max(s, axis=0, keepdims=True)

    def probs(hh, slot):
        m_old = m_ref[hh]
        m_new = jnp.maximum(m_old, c_refs[slot][hh])
        return m_old, m_new, jnp.exp2(s_refs[slot][hh] - m_new).astype(BF16)

    def accumulate(j, hh, state):
        m_old, m_new, p = state
        vc = v_ref[0, j, v_row(hh):v_row(hh) + v_rows, :]
        alpha = jnp.exp2(m_old - m_new)
        acc_ref[hh] = acc_ref[hh] * alpha + jnp.dot(vc, p, preferred_element_type=F32)
        m_ref[hh] = m_new

    for hh in range(N_SCORE_HEADS):
        scores(0, hh, 0)

    def step(j, slot):
        for hh in range(N_SCORE_HEADS):
            state = probs(hh, slot)
            if scores_between:
                scores(j + 1, hh, 1 - slot)
            accumulate(j, hh, state)
            if not scores_between:
                scores(j + 1, hh, 1 - slot)

    def body(jj, carry):
        step(2 * jj, 0)
        step(2 * jj + 1, 1)
        return carry

    lax.fori_loop(0, (n - 1) // 2, body, 0)
    for hh in range(N_SCORE_HEADS):
        accumulate(n - 1, hh, probs(hh, 0))


def _attn_scratch(v_rows):
    stat = pltpu.VMEM((N_SCORE_HEADS, 1, TM), F32)
    return [pltpu.VMEM((N_SCORE_HEADS, 128, TM), BF16),
            stat, stat,
            pltpu.VMEM((N_SCORE_HEADS, v_rows, TM), F32),
            pltpu.VMEM((N_SCORE_HEADS, TM, TM), F32), pltpu.VMEM((N_SCORE_HEADS, TM, TM), F32),
            stat, stat,
            pltpu.VMEM((N_SCORE_HEADS, TM, TM), BF16), pltpu.VMEM((N_SCORE_HEADS, TM, TM), BF16)]


def _gqa_kernel(q_ref, k_ref, v_ref, kmax_ref, o_ref, *scratch, n_kv):
    group = GQA_HEADS // GQA_KV_HEADS
    _attn_core(q_ref, k_ref, v_ref, kmax_ref, scratch, n_kv=n_kv,
               k_col=lambda h: 0, k_row=lambda h: h // group, q_slot=lambda h: h // group,
               v_row=lambda h: (h // group) * GQA_VROWS, v_rows=GQA_VROWS, scores_between=True,
               vpu_denominator=False)
    acc_ref = scratch[3]
    for hh in range(GQA_HEADS):
        a = acc_ref[hh]
        o_ref[0, 0, hh * HEAD_DIM:(hh + 1) * HEAD_DIM, :] = (a[:HEAD_DIM] / a[HEAD_DIM:HEAD_DIM + 1]).astype(o_ref.dtype)


def _gqa_attention(aq, ak, av, kmax):
    b, nt, nq, _ = aq.shape
    t = nt * TM
    assert nt % 2 == 1 and nt > 1
    return pl.pallas_call(
        functools.partial(_gqa_kernel, n_kv=nt),
        grid=(b, nt),
        in_specs=[pl.BlockSpec((1, 1, nq, TM), lambda bb, i: (bb, i, 0, 0)),
                  pl.BlockSpec((1, t, 128), lambda bb, i: (bb, 0, 0)),
                  pl.BlockSpec((1, nt, GQA_KV_HEADS * GQA_VROWS, TM), lambda bb, i: (bb, 0, 0, 0)),
                  pl.BlockSpec((1, 8, 128), lambda bb, i: (bb, 0, 0))],
        out_specs=pl.BlockSpec((1, 1, nq, TM), lambda bb, i: (bb, i, 0, 0)),
        out_shape=jax.ShapeDtypeStruct((b, nt, nq, TM), BF16),
        scratch_shapes=_attn_scratch(GQA_VROWS),
        compiler_params=_params("parallel", "arbitrary"),
        name="gqa_attn",
    )(aq, ak, av, kmax)


def _diff_kernel(q_ref, k_ref, v_ref, kmax_ref, lam_ref, gain_ref, o_ref, *scratch, n_kv, lam_init):
    _attn_core(q_ref, k_ref, v_ref, kmax_ref, scratch, n_kv=n_kv,
               k_col=lambda hc: (hc // 2) * 128, k_row=lambda hc: hc, q_slot=lambda hc: hc % 2,
               v_row=lambda hc: (hc // 2) * DIFF_VROWS, v_rows=DIFF_VROWS, scores_between=False,
               vpu_denominator=True)
    acc_ref = scratch[3]
    lp = lam_ref[...]
    lam = (jnp.exp(jnp.sum(lp[0:1] * lp[1:2], axis=1, keepdims=True))
           - jnp.exp(jnp.sum(lp[2:3] * lp[3:4], axis=1, keepdims=True)) + lam_init)
    gain = gain_ref[...]
    nv = 2 * HEAD_DIM
    for hh in range(DIFF_HEADS):
        a0, a1 = acc_ref[2 * hh], acc_ref[2 * hh + 1]
        o = a0[:nv] / a0[nv:nv + 1] - lam * (a1[:nv] / a1[nv:nv + 1])
        y = o * lax.rsqrt(jnp.mean(o * o, axis=0, keepdims=True) + NORM_EPS) * gain
        o_ref[0, 0, hh * nv:(hh + 1) * nv, :] = (y * (1.0 - lam_init)).astype(o_ref.dtype)


def _diff_attention(dq, dk, dv, kmax, lam_p, gain, layer_idx):
    b, nt, nq, _ = dq.shape
    t = nt * TM
    assert nt % 2 == 1 and nt > 1
    lam_init = 0.8 - 0.6 * math.exp(-0.3 * layer_idx)
    return pl.pallas_call(
        functools.partial(_diff_kernel, n_kv=nt, lam_init=lam_init),
        grid=(b, nt),
        in_specs=[pl.BlockSpec((1, 1, nq, TM), lambda bb, i: (bb, i, 0, 0)),
                  pl.BlockSpec((1, t, DIFF_QK), lambda bb, i: (bb, 0, 0)),
                  pl.BlockSpec((1, nt, DIFF_HEADS * DIFF_VROWS, TM), lambda bb, i: (bb, 0, 0, 0)),
                  pl.BlockSpec((1, 8, 128), lambda bb, i: (bb, 0, 0)),
                  _const_spec(lam_p.shape),
                  _const_spec((2 * HEAD_DIM, 1))],
        out_specs=pl.BlockSpec((1, 1, DIFF_V, TM), lambda bb, i: (bb, i, 0, 0)),
        out_shape=jax.ShapeDtypeStruct((b, nt, DIFF_V, TM), BF16),
        scratch_shapes=_attn_scratch(DIFF_VROWS),
        compiler_params=_params("parallel", "arbitrary"),
        name="diff_attn",
    )(dq, dk, dv, kmax, lam_p, gain.reshape(2 * HEAD_DIM, 1))


def _split_bf16(x):
    hi = x.astype(BF16)
    return hi, (x - hi.astype(F32)).astype(BF16)


def _gla_kernel(qk_ref, la_ref, vt_ref, o_ref, st_ref, *, backward):
    @pl.when(pl.program_id(1) == 0)
    def _():
        st_ref[...] = jnp.zeros(st_ref.shape, F32)

    n_chunk = TM // GLA_CHUNK
    r = lax.broadcasted_iota(jnp.int32, (TM, TM), 0)
    c = lax.broadcasted_iota(jnp.int32, (TM, TM), 1)
    same = (r // GLA_CHUNK) == (c // GLA_CHUNK)
    causal = (c >= r) if backward else (c <= r)
    tri = jnp.where(same & causal, 1.0, 0.0).astype(BF16)
    blk = jnp.where(same, 1.0, 0.0).astype(BF16)
    lane = lax.broadcasted_iota(jnp.int32, (TM, 128), 1)
    row = lax.broadcasted_iota(jnp.int32, (TM, 128), 0)
    keep = same & ((r >= c) if backward else (r <= c))
    order = range(n_chunk - 1, -1, -1) if backward else range(n_chunk)
    units = [(s, h) for s in range(GLA_PB) for h in range(GLA_HEADS)]
    pair = lambda h: slice((h // 2) * 128, (h // 2 + 1) * 128)
    q_in, k_in, k_end, dec = [], [], [], []
    for s in range(GLA_PB):
        la_hi, la_lo = _split_bf16(la_ref[s])
        cum = jnp.dot(tri, la_hi, preferred_element_type=F32) + jnp.dot(tri, la_lo, preferred_element_type=F32)
        tot = jnp.dot(blk, la_hi, preferred_element_type=F32) + jnp.dot(blk, la_lo, preferred_element_type=F32)
        q = qk_ref[s, :, :GLA_QK]
        k = qk_ref[s, :, GLA_QK:]
        q_in.append((q * jnp.exp(cum)).astype(BF16))
        k_in.append(k * jnp.exp(-cum))
        k_end.append(k * jnp.exp(tot - cum))
        dec.append(jnp.exp(tot))
    head_lanes = {h: (lane // GLA_DK) == (h % 2) for h in range(GLA_HEADS)}
    q_pair = {(s, h): q_in[s][:, pair(h)] for s, h in units}
    k_in_h = {(s, h): jnp.where(head_lanes[h], k_in[s][:, pair(h)], 0.0).astype(BF16) for s, h in units}
    k_end_h = {(s, h): jnp.where(head_lanes[h], k_end[s][:, pair(h)], 0.0) for s, h in units}
    vts = {(s, h): vt_ref[s, 0, h * GLA_DV:(h + 1) * GLA_DV, :] for s, h in units}
    a_ts = {u: jnp.where(keep, _nt(k_in_h[u], q_pair[u]), 0.0).astype(BF16) for u in units}
    o_ts = {u: jnp.dot(vts[u], a_ts[u], preferred_element_type=F32) for u in units}
    s_ts = {(s, h): st_ref[s, h] for s, h in units}
    for ci in order:
        in_chunk = (row // GLA_CHUNK) == ci
        for u in units:
            s, h = u
            q_c = jnp.where(in_chunk, q_pair[u], jnp.zeros_like(q_pair[u]))
            o_ts[u] = o_ts[u] + _nt(s_ts[u].astype(BF16), q_c)
            k_c = jnp.where(in_chunk, k_end_h[u], 0.0).astype(BF16)
            s_ts[u] = (s_ts[u] * dec[s][ci * GLA_CHUNK:ci * GLA_CHUNK + 1, pair(h)]
                       + jnp.dot(vts[u], k_c, preferred_element_type=F32))
    for s, h in units:
        st_ref[s, h] = s_ts[(s, h)]
        o_ref[s, 0, h * GLA_DV:(h + 1) * GLA_DV, :] = o_ts[(s, h)]


def _gla_direction(qk, la, vt, backward):
    b, t, _ = qk.shape
    nt = t // TM
    d = 1 if backward else 0
    tile = (lambda i: jnp.where(i == 0, 0, nt - i)) if backward else (lambda i: i)
    return pl.pallas_call(
        functools.partial(_gla_kernel, backward=backward),
        grid=(b // GLA_PB, nt),
        in_specs=[pl.BlockSpec((GLA_PB, TM, 2 * GLA_QK), lambda bb, i: (bb, tile(i), 0)),
                  pl.BlockSpec((GLA_PB, TM, GLA_QK), lambda bb, i: (bb, tile(i), d)),
                  pl.BlockSpec((GLA_PB, 1, GLA_V, TM), lambda bb, i: (bb, tile(i), 0, 0))],
        out_specs=pl.BlockSpec((GLA_PB, 1, GLA_V, TM), lambda bb, i: (bb, tile(i), 0, 0)),
        out_shape=jax.ShapeDtypeStruct((b, nt, GLA_V, TM), F32),
        scratch_shapes=[pltpu.VMEM((GLA_PB, GLA_HEADS, GLA_DV, 128), F32)],
        compiler_params=_params("parallel", "arbitrary"),
        name="gla_bwd" if backward else "gla_fwd",
    )(qk, la, vt)


def _post_phases(xs, mods, g, yas, ybs, woa_ref, wob_ref, w1_ref, w2_ref, ff_chunk):
    n = len(xs)
    ys = [jnp.dot(yas[s], woa_ref[...], preferred_element_type=F32)
          + jnp.dot(ybs[s].astype(F32).T.astype(BF16), wob_ref[...], preferred_element_type=F32) for s in range(n)]
    x1s = [xs[s] + mods[s][2:3] * _rms(ys[s], g[1:2]) for s in range(n)]
    hs = [(_rms(x1s[s], g[2:3]) * (1.0 + mods[s][4:5]) + mods[s][3:4]).astype(BF16) for s in range(n)]
    d_ff = w1_ref.shape[1]
    accs = [jnp.zeros(xs[0].shape, F32) for _ in range(n)]
    for c in range(d_ff // ff_chunk):
        cols = slice(c * ff_chunk, (c + 1) * ff_chunk)
        for s in range(n):
            u = jnp.dot(hs[s], w1_ref[:, cols], preferred_element_type=F32)
            u = jnp.square(jnp.maximum(u, 0.0)).astype(BF16)
            accs[s] = accs[s] + jnp.dot(u, w2_ref[cols, :], preferred_element_type=F32)
    return [x1s[s] + mods[s][5:6] * _rms(accs[s], g[3:4]) for s in range(n)]


def _post_gla_kernel(*refs, ff_chunk, n_stream, off):
    of_ref, ob_ref, gg_ref, gain_ref, yb_ref, mods_ref, g_ref, woa_ref, wob_ref, w1_ref, w2_ref, o_ref = refs[n_stream:]
    tile = pl.program_id(1) + off
    gain = gain_ref[...]
    yas = []
    for s in range(POST_PB):
        parts = []
        for hh in range(GLA_HEADS):
            rows = slice(hh * GLA_DV, (hh + 1) * GLA_DV)
            o = of_ref[s, 0, rows, :] + ob_ref[s, 0, rows, :]
            gt = gg_ref[s, 0, rows, :]
            y = o * lax.rsqrt(jnp.mean(o * o, axis=0, keepdims=True) + NORM_EPS) * gain
            parts.append(y * (gt / (1.0 + jnp.exp(-gt))))
        yas.append(jnp.concatenate(parts, axis=0).T.astype(BF16))
    outs = _post_phases([_stream_tile(refs[:n_stream], s, tile) for s in range(POST_PB)],
                        [mods_ref[s] for s in range(POST_PB)], g_ref[...], yas,
                        [yb_ref[s, 0] for s in range(POST_PB)], woa_ref, wob_ref, w1_ref, w2_ref, ff_chunk)
    for s in range(POST_PB):
        o_ref[s] = outs[s]


def _post_kernel(*refs, ff_chunk, n_stream, off):
    ya_ref, yb_ref, mods_ref, g_ref, woa_ref, wob_ref, w1_ref, w2_ref, o_ref = refs[n_stream:]
    tile = pl.program_id(1) + off
    outs = _post_phases([_stream_tile(refs[:n_stream], s, tile) for s in range(POST_PB)],
                        [mods_ref[s] for s in range(POST_PB)], g_ref[...],
                        [ya_ref[s].astype(BF16) for s in range(POST_PB)], [yb_ref[s, 0] for s in range(POST_PB)],
                        woa_ref, wob_ref, w1_ref, w2_ref, ff_chunk)
    for s in range(POST_PB):
        o_ref[s] = outs[s]


def _post(xs, ya, yb_t, mods_l, g, w_out, w1, w2, skip_ctx):
    b, t, d = _stream_shape(xs)
    nt = t // TM
    pb = POST_PB
    half = w_out.shape[0] // 2
    woa = w_out[:half].astype(BF16)
    wob = w_out[half:].astype(BF16)
    w1 = w1.astype(BF16)
    w2 = w2.astype(BF16)
    off = 1 if skip_ctx else 0
    x_specs, x_args = _stream_specs(xs, pb, off)
    tok = lambda n: pl.BlockSpec((pb, TM, n), lambda bb, i: (bb, i + off, 0))
    tr = lambda n: pl.BlockSpec((pb, 1, n, TM), lambda bb, i: (bb, i + off, 0, 0))
    if isinstance(ya, tuple):
        o_f, o_b, gg, gain = ya
        body, name = _post_gla_kernel, "post_gla"
        a_specs = [tr(half), tr(half), tr(half), _const_spec((GLA_DV, 1))]
        a_args = [o_f, o_b, gg, gain.reshape(GLA_DV, 1)]
    else:
        body, name = _post_kernel, "post"
        a_specs, a_args = [tok(half)], [ya]
    return pl.pallas_call(
        functools.partial(body, ff_chunk=1024, n_stream=len(x_args), off=off),
        grid=(b // pb, nt - off),
        in_specs=x_specs + a_specs + [
                  tr(half),
                  pl.BlockSpec((pb, N_MOD, d), lambda bb, i: (_mod_block(bb, i + off, pb), 0, 0)),
                  _const_spec(g.shape),
                  _const_spec(woa.shape), _const_spec(wob.shape), _const_spec(w1.shape), _const_spec(w2.shape)],
        out_specs=pl.BlockSpec((pb, TM, d), lambda bb, i: (bb, i, 0)),
        out_shape=jax.ShapeDtypeStruct((b, t - off * TM, d), F32),
        compiler_params=_params("parallel", "parallel"),
        name=name,
    )(*x_args, *a_args, yb_t, mods_l, g, woa, wob, w1, w2)


def _dot3(a, b):
    ah, al = _split_bf16(a)
    bh, bl = _split_bf16(b)
    d = lambda x, y: jnp.dot(x, y, preferred_element_type=F32)
    return d(ah, bh) + (d(al, bh) + d(ah, bl))


def _filter_kernel(w1t_ref, w1c_ref, w1s_ref, b1_ref, w2_ref, b2_ref, w3_ref, delta_ref, o_ref, *, seq, rb):
    tap = pl.program_id(0) * rb + lax.broadcasted_iota(jnp.int32, (rb, 1), 0)
    d = tap.astype(F32)
    t = d / max(seq - 1, 1)
    w = (2 * math.pi / seq) * d
    band = lax.broadcasted_iota(jnp.int32, (1, HY_BANDS), 1).astype(F32)
    bands = 1e-4 + band * ((HY_BANDS - 1 - 1e-4) / (HY_BANDS - 1))
    ang = w * bands
    pre = t * w1t_ref[...] + _dot3(jnp.cos(ang), w1c_ref[...]) - _dot3(jnp.sin(ang), w1s_ref[...]) + b1_ref[...]
    hid = jnp.sin(pre)
    hid = jnp.sin(_dot3(hid, w2_ref[...]) + b2_ref[...])
    decay = jnp.exp(-t * delta_ref[...])
    decay_bwd = jnp.where(tap == 0, 0.0, decay)
    for o in range(HY_ORDER):
        o_ref[2 * o] = (_dot3(hid, w3_ref[2 * o]) * decay).astype(o_ref.dtype)
        o_ref[2 * o + 1] = (_dot3(hid, w3_ref[2 * o + 1]) * decay_bwd).astype(o_ref.dtype)


def _hyena_kernels(seq, f_w1, f_b1, f_w2, f_b2, f_w3):
    rb = min(512, seq)
    ffn = f_w2.shape[0]
    w3 = f_w3.reshape(ffn, 2 * HY_ORDER, HY_CH).transpose(1, 0, 2)
    deltas = jnp.abs(jnp.linspace(math.log(HY_TARGET) / HY_FAST_PCT, math.log(HY_TARGET) / HY_SLOW_PCT,
                                  HY_CH, dtype=F32)).reshape(1, HY_CH)
    return pl.pallas_call(
        functools.partial(_filter_kernel, seq=seq, rb=rb),
        grid=(seq // rb,),
        in_specs=[_const_spec((1, ffn)), _const_spec((HY_BANDS, ffn)), _const_spec((HY_BANDS, ffn)),
                  _const_spec((1, ffn)), _const_spec((ffn, ffn)), _const_spec((1, ffn)),
                  _const_spec(w3.shape), _const_spec((1, HY_CH))],
        out_specs=pl.BlockSpec((2 * HY_ORDER, rb, HY_CH), lambda j: (0, j, 0)),
        out_shape=jax.ShapeDtypeStruct((2 * HY_ORDER, seq, HY_CH), BF16),
        compiler_params=_params("parallel"),
        name="hy_filter",
    )(f_w1[0:1], f_w1[1:1 + HY_BANDS], f_w1[1 + HY_BANDS:], f_b1.reshape(1, ffn), f_w2, f_b2.reshape(1, ffn), w3,
      deltas)


def _dft_tables(seq):
    n = 2 * seq
    r = math.isqrt(n)
    assert r * r == n
    r1 = (r // 2 + 1 + 7) // 8 * 8
    idx = jnp.arange(r, dtype=jnp.int32)
    phi = (2 * math.pi / r) * ((idx[:r1, None] * idx[None, :]) % r).astype(F32)
    f1 = jnp.concatenate([jnp.cos(phi), -jnp.sin(phi)], axis=0).astype(BF16)
    m = (idx[:r1, None, None] + r * idx[None, :, None]) * idx[None, None, :]
    th = (2 * math.pi / n) * (m % n).astype(F32)
    c, s = jnp.cos(th), jnp.sin(th)
    blockm = lambda re, im: jnp.concatenate([jnp.concatenate([re, -im], axis=-1),
                                             jnp.concatenate([im, re], axis=-1)], axis=-2)
    g = blockm(c, -s).astype(BF16)
    ginv = blockm(jnp.swapaxes(c, 1, 2), jnp.swapaxes(s, 1, 2)).astype(BF16)
    k1 = idx[:r1]
    weight = jnp.where((k1 == 0) | (k1 == r // 2), 1.0, jnp.where(k1 < r // 2, 2.0, 0.0)) / n
    phi_t = phi.T[:r // 2]
    finv = jnp.concatenate([jnp.cos(phi_t) * weight, -jnp.sin(phi_t) * weight], axis=1).astype(BF16)
    return r, f1, g, ginv, finv


DFT_BB = 32


def _dft1_kernel(f_ref, x_ref, o_ref, *, a0):
    f = f_ref[...]
    r = o_ref.shape[2]
    x = pltpu.einshape("abc->bac", x_ref[0, a0:].astype(F32))
    ys = [jnp.dot(f, x[j].astype(BF16), preferred_element_type=F32) for j in range(DFT_BB)]
    y = pltpu.einshape("jkc->kjc", jnp.stack(ys, axis=0))
    o_ref[0, 0] = y[:r].astype(o_ref.dtype)
    o_ref[0, 1] = y[r:].astype(o_ref.dtype)


def _dft_stage1(f, x4, a0=0):
    b, rows, r, c = x4.shape
    r1 = f.shape[0] // 2
    return pl.pallas_call(
        functools.partial(_dft1_kernel, a0=a0),
        grid=(b, r // DFT_BB),
        in_specs=[_const_spec(f.shape), pl.BlockSpec((1, rows, DFT_BB, c), lambda bi, j: (bi, 0, j, 0))],
        out_specs=pl.BlockSpec((1, 2, r1, DFT_BB, c), lambda bi, j: (bi, 0, 0, j, 0)),
        out_shape=jax.ShapeDtypeStruct((b, 2, r1, r, c), BF16),
        compiler_params=_params("parallel", "parallel"),
        name="hy_dft1",
    )(f, x4)


def _fspec_kernel(af_ref, ab_ref, g_ref, o_ref, *, kb, r):
    for kk in range(kb):
        xf = jnp.dot(g_ref[kk], af_ref[0, :, kk].reshape(2 * r, af_ref.shape[-1]), preferred_element_type=F32)
        xb = jnp.dot(g_ref[kk], ab_ref[0, :, kk].reshape(2 * r, ab_ref.shape[-1]), preferred_element_type=F32)
        o_ref[0, kk, :r] = xf[:r] + xb[:r]
        o_ref[0, kk, r:] = xf[r:] - xb[r:]


def _filter_spectrum(a, g, kb):
    o2, _, r1, r, c = a.shape
    blk = lambda d: pl.BlockSpec((1, 2, kb, r, c), lambda oo, j: (2 * oo + d, 0, j, 0, 0))
    return pl.pallas_call(
        functools.partial(_fspec_kernel, kb=kb, r=r),
        grid=(o2 // 2, r1 // kb),
        in_specs=[blk(0), blk(1), pl.BlockSpec((kb, 2 * r, 2 * r), lambda oo, j: (j, 0, 0))],
        out_specs=pl.BlockSpec((1, kb, 2 * r, c), lambda oo, j: (oo, j, 0, 0)),
        out_shape=jax.ShapeDtypeStruct((o2 // 2, r1, 2 * r, c), F32),
        compiler_params=_params("parallel", "parallel"),
        name="hy_fspec",
    )(a, a, g)


def _dft2_kernel(a_ref, g_ref, gi_ref, k_ref, o_ref, *, kb, r):
    xs = [jnp.dot(g_ref[kk], a_ref[0, :, kk].reshape(2 * r, a_ref.shape[-1]), preferred_element_type=F32)
          for kk in range(kb)]
    ys = []
    for kk in range(kb):
        x, ks = xs[kk], k_ref[0, kk]
        xr, xi, kr, ki = x[:r], x[r:], ks[:r], ks[r:]
        ys.append(jnp.concatenate([xr * kr - xi * ki, xr * ki + xi * kr], axis=0).astype(BF16))
    bqs = [jnp.dot(gi_ref[kk], ys[kk], preferred_element_type=F32) for kk in range(kb)]
    for kk in range(kb):
        o_ref[0, 0, kk] = bqs[kk][:r].astype(o_ref.dtype)
        o_ref[0, 1, kk] = bqs[kk][r:].astype(o_ref.dtype)


def _dft_stage2(a, g, ginv, kspec, order, kb):
    b, _, r1, r, c = a.shape
    blk = pl.BlockSpec((1, 2, kb, r, c), lambda j, bb: (bb, 0, j, 0, 0))
    tab = pl.BlockSpec((kb, 2 * r, 2 * r), lambda j, bb: (j, 0, 0))
    return pl.pallas_call(
        functools.partial(_dft2_kernel, kb=kb, r=r),
        grid=(r1 // kb, b),
        in_specs=[blk, tab, tab, pl.BlockSpec((1, kb, 2 * r, c), lambda j, bb: (order, j, 0, 0))],
        out_specs=blk,
        out_shape=jax.ShapeDtypeStruct(a.shape, BF16),
        compiler_params=_params("parallel", "arbitrary"),
        name="hy_dft2",
    )(a, g, ginv, kspec)


def _dft3_kernel(f_ref, b_ref, gate_ref, z_ref, bias_ref, head_ref, o_ref, *, a0):
    f = f_ref[...]
    bq = jnp.concatenate([b_ref[0, 0], b_ref[0, 1]], axis=0)
    bq = pltpu.einshape("kjc->jkc", bq.astype(F32))
    ys = [jnp.dot(f, bq[j].astype(BF16), preferred_element_type=F32) for j in range(DFT_BB)]
    y = pltpu.einshape("jac->ajc", jnp.stack(ys, axis=0))
    o_ref[0, :a0] = head_ref[0]
    o_ref[0, a0:] = gate_ref[0, a0:] * (y + z_ref[0, a0:] * bias_ref[...])


def _dft_stage3(finv, bq5, gate4, z4, bias, head4):
    b, _, r1, r, c = bq5.shape
    rows = z4.shape[1]
    a0 = head4.shape[1]
    io = pl.BlockSpec((1, rows, DFT_BB, c), lambda bi, j: (bi, 0, j, 0))
    return pl.pallas_call(
        functools.partial(_dft3_kernel, a0=a0),
        grid=(b, r // DFT_BB),
        in_specs=[_const_spec(finv.shape), pl.BlockSpec((1, 2, r1, DFT_BB, c), lambda bi, j: (bi, 0, 0, j, 0)), io, io,
                  _const_spec((1, 1, c)), pl.BlockSpec((1, a0, DFT_BB, c), lambda bi, j: (bi, 0, j, 0))],
        out_specs=io,
        out_shape=jax.ShapeDtypeStruct((b, rows, r, c), F32),
        compiler_params=_params("parallel", "parallel"),
        name="hy_dft3",
    )(finv, bq5, gate4, z4, bias, head4)


def _long_conv_gated(v, gates, kern, hy_bias, ctx_out, lc):
    b, t, c = v.shape
    r, f1, g, ginv, finv = _dft_tables(t - lc)
    assert lc % r == 0
    a0 = lc // r
    kb = 8
    kspec = _filter_spectrum(_dft_stage1(f1[:, :r // 2], kern.reshape(2 * HY_ORDER, r // 2, r, c)), g, kb)
    as_blocks = lambda x: x.reshape(b, t // r, r, c)
    z = as_blocks(v)
    heads = [jnp.zeros((b, a0, r, c), F32), ctx_out.reshape(b, a0, r, c)]
    for o in range(HY_ORDER):
        a = _dft_stage1(f1[:, :r // 2], z, a0)
        bq = _dft_stage2(a, g, ginv, kspec, o, kb)
        z = _dft_stage3(finv, bq, as_blocks(gates[o]), z, hy_bias[o].reshape(1, 1, c), heads[o])
    return z.reshape(b, t, c)


def _ctx_conv_kernel(v_ref, x1_ref, x2_ref, k_ref, ff_ref, fi_ref, bias_ref, o_ref):
    n = ff_ref.shape[1]
    z = v_ref[0]
    for o, gate_ref in enumerate((x1_ref, x2_ref)):
        kf = jnp.dot(ff_ref[:, :TM], k_ref[2 * o], preferred_element_type=F32)
        kb = jnp.dot(ff_ref[:, :TM], k_ref[2 * o + 1], preferred_element_type=F32)
        x = jnp.dot(ff_ref[:, :TM], z.astype(BF16), preferred_element_type=F32)
        xr, xi, kr, ki = x[:n], x[n:], kf[:n] + kb[:n], kf[n:] - kb[n:]
        y = jnp.concatenate([xr * kr - xi * ki, xr * ki + xi * kr], axis=0).astype(BF16)
        conv = jnp.dot(fi_ref[...], y, preferred_element_type=F32)
        z = gate_ref[0] * (conv + z * bias_ref[o:o + 1])
    o_ref[0] = z


def _ctx_conv_gated(v, x1, x2, kern, hy_bias):
    b, _, c = v.shape
    seq = TM
    n = 2 * seq
    idx = jnp.arange(n, dtype=jnp.int32)
    th = (2 * math.pi / n) * ((idx[:, None] * idx[None, :]) % n).astype(F32)
    ff = jnp.concatenate([jnp.cos(th), -jnp.sin(th)], axis=0).astype(BF16)
    fi = (jnp.concatenate([jnp.cos(th), -jnp.sin(th)], axis=1)[:seq] / n).astype(BF16)
    io = pl.BlockSpec((1, seq, c), lambda bb: (bb, 0, 0))
    return pl.pallas_call(
        _ctx_conv_kernel,
        grid=(b,),
        in_specs=[io, io, io, _const_spec(kern.shape), _const_spec(ff.shape), _const_spec(fi.shape),
                  _const_spec(hy_bias.shape)],
        out_specs=io,
        out_shape=jax.ShapeDtypeStruct((b, seq, c), F32),
        compiler_params=_params("parallel"),
        name="hy_ctx",
    )(v, x1, x2, kern, ff, fi, hy_bias)


def _hyena(v, x1, x2, lc, f_w1, f_b1, f_w2, f_b2, f_w3, hy_bias):
    seq = v.shape[1] - lc
    yc = _ctx_conv_gated(v, x1, x2, _hyena_kernels(lc, f_w1, f_b1, f_w2, f_b2, f_w3), hy_bias)
    return _long_conv_gated(v, (x1, x2), _hyena_kernels(seq, f_w1, f_b1, f_w2, f_b2, f_w3), hy_bias, yc, lc)


def _rope_tables_t(seq, lc):
    rows = seq // GRID_W
    r = jnp.broadcast_to(jnp.arange(rows, dtype=F32)[:, None], (rows, GRID_W)).reshape(-1)
    col = jnp.broadcast_to(jnp.arange(GRID_W, dtype=F32)[None, :], (rows, GRID_W)).reshape(-1)
    n_pairs = HEAD_DIM // 4
    inv = ROPE_THETA ** (-jnp.arange(n_pairs, dtype=F32) / n_pairs)
    ang = jnp.concatenate([r[:, None] * inv, col[:, None] * inv], axis=-1)
    cos = jnp.concatenate([jnp.ones((lc, 2 * n_pairs), F32), jnp.cos(ang)], axis=0)
    sin = jnp.concatenate([jnp.zeros((lc, 2 * n_pairs), F32), jnp.sin(ang)], axis=0)
    return cos.T, sin.T


def kernel(x, c, ctx, c_ctx, w_ada, b_ada, norm_g, w_mlp_in, w_mlp_out, ev_w_in, ev_w_lr, ev_b_lr, ev_gla_g,
           ev_qk_g, ev_w_out, od_w_in, od_conv_w, od_conv_b, od_f_w1, od_f_b1, od_f_w2, od_f_b2, od_f_w3,
           od_hy_bias, od_lam, od_diff_g, od_w_out):
    bsz, seq, d = x.shape
    lc = ctx.shape[1]
    depth = w_ada.shape[0]
    assert lc == TM and seq % TM == 0 and bsz <= MOD_CTX_ROW and bsz % POST_PB == 0 and bsz % IN_PB == 0 and bsz % GLA_PB == 0
    cos_t, sin_t = _rope_tables_t(seq, lc)
    cc = jnp.zeros((8, d), F32).at[:bsz].set(c).at[MOD_CTX_ROW:MOD_CTX_ROW + max(POST_PB, IN_PB)].set(c_ctx)
    mods = _mods(cc, w_ada, b_ada).reshape(depth, 8, N_MOD, d)
    xs = (x, ctx)
    for i in range(depth):
        last = i == depth - 1
        j = i // 2
        g = norm_g[i]
        if i % 2 == 0:
            qk, la, gv, gg, aq, ak, av, kn = _even_in(xs, mods[i], g[0:1], ev_w_in[j], ev_w_lr[j], ev_b_lr[j],
                                                  ev_qk_g[j], cos_t, sin_t)
            ya = (_gla_direction(qk, la, gv, False), _gla_direction(qk, la, gv, True), gg, ev_gla_g[j])
            yb = _gqa_attention(aq, ak, av, jnp.max(kn, axis=1))
            w_out = ev_w_out[j]
        else:
            hv, hx1, hx2, dq, dk, dv, kn = _odd_in(xs, mods[i], g[0:1], od_w_in[j], od_conv_w[j], od_conv_b[j],
                                                   cos_t, sin_t)
            ya = _hyena(hv, hx1, hx2, lc, od_f_w1[j], od_f_b1[j], od_f_w2[j], od_f_b2[j], od_f_w3[j], od_hy_bias[j])
            yb = _diff_attention(dq, dk, dv, jnp.max(kn, axis=1), od_lam[j], od_diff_g[j], i)
            w_out = od_w_out[j]
        xs = _post(xs, ya, yb, mods[i], g, w_out, w_mlp_in[i], w_mlp_out[i], skip_ctx=last)
    return xs
```

```python
import functools
import math

import jax
import jax.numpy as jnp
import numpy as np
from jax import lax
from jax.experimental import pallas as pl
from jax.experimental.pallas import tpu as pltpu

F32 = jnp.float32
BF16 = jnp.bfloat16

TM = 256
GRID_W = 64
HEAD_DIM = 64
ROPE_THETA = 10000.0
NORM_EPS = 1e-6
N_MOD = 6
VMEM_LIMIT = 56 * 1024 * 1024

GLA_HEADS = 4
GLA_DK = 64
GLA_DV = 128
GLA_RANK = 16
GLA_NORMALIZER = 16.0
GLA_CHUNK = 64
GLA_QK = GLA_HEADS * GLA_DK
GLA_V = GLA_HEADS * GLA_DV
GQA_HEADS = 8
GQA_KV_HEADS = 2
HY_CH = 512
HY_ORDER = 2
HY_BANDS = 16
HY_TARGET = 1e-2
HY_FAST_PCT = 0.3
HY_SLOW_PCT = 1.5
HY_COLS = (HY_ORDER + 1) * HY_CH
DIFF_HEADS = 4
DIFF_QK = DIFF_HEADS * 2 * HEAD_DIM
DIFF_V = DIFF_HEADS * 2 * HEAD_DIM
ONES_ROWS = 16
N_SCORE_HEADS = 8
LOG2E = 1.4426950408889634
Q_SCALE = HEAD_DIM ** -0.5 * LOG2E
GQA_VROWS = HEAD_DIM + ONES_ROWS
SCORE_BOUND_SLACK = 1.01
FAST_UNROLL_MAX = 32
SCORE_BOUND_LIMIT = 60.0
DIFF_VROWS = 2 * HEAD_DIM + ONES_ROWS


def _params(*sem):
    return pltpu.CompilerParams(dimension_semantics=sem, vmem_limit_bytes=VMEM_LIMIT)


def _const_spec(shape):
    nd = len(shape)
    return pl.BlockSpec(shape, lambda *_: (0,) * nd, pipeline_mode=pl.Buffered(1))


def _rms(x, g):
    return x * lax.rsqrt(jnp.mean(x * x, axis=-1, keepdims=True) + NORM_EPS) * g


MOD_CTX_ROW = 4
POST_PB = 2
IN_PB = 4
GLA_PB = 4


def _mod_block(bb, i, pb):
    return jnp.where(i == 0, MOD_CTX_ROW // pb, bb)


def _stream_specs(xs, pb, off=0):
    if isinstance(xs, tuple):
        x, ctx = xs
        d = x.shape[-1]
        return [pl.BlockSpec((pb, TM, d), lambda bb, i: (bb, jnp.maximum(i + off - 1, 0), 0)),
                pl.BlockSpec((pb, TM, d), lambda bb, i: (bb, 0, 0))], [x, ctx]
    return [pl.BlockSpec((pb, TM, xs.shape[-1]), lambda bb, i: (bb, i + off, 0))], [xs]


def _stream_shape(xs):
    if isinstance(xs, tuple):
        return xs[0].shape[0], xs[0].shape[1] + xs[1].shape[1], xs[0].shape[2]
    return xs.shape


def _stream_tile(stream_refs, s, tile):
    if len(stream_refs) == 2:
        return jnp.where(tile == 0, stream_refs[1][s], stream_refs[0][s])
    return stream_refs[0][s]


def _mods_kernel(cc_ref, w_ref, b_ref, o_ref):
    cc = cc_ref[...]
    s = cc / (1.0 + jnp.exp(-cc))
    o_ref[0] = jnp.dot(s.astype(BF16), w_ref[0].astype(BF16), preferred_element_type=F32) + b_ref[0]


def _mods(cc, w_ada, b_ada):
    depth, d, n = w_ada.shape
    nb = n // 4
    return pl.pallas_call(
        _mods_kernel,
        grid=(depth, n // nb),
        in_specs=[pl.BlockSpec((8, d), lambda l, j: (0, 0)),
                  pl.BlockSpec((1, d, nb), lambda l, j: (l, 0, j)),
                  pl.BlockSpec((1, 1, nb), lambda l, j: (l, 0, j))],
        out_specs=pl.BlockSpec((1, 8, nb), lambda l, j: (l, 0, j)),
        out_shape=jax.ShapeDtypeStruct((depth, 8, n), F32),
        compiler_params=_params("parallel", "parallel"),
        name="ada_mods",
    )(cc, w_ada, b_ada.reshape(depth, 1, n))


def _normed_input(x, mods, g):
    return (_rms(x, g) * (1.0 + mods[1:2]) + mods[0:1]).astype(BF16)


def _rope_rows(y, cos, sin):
    x1, x2 = y[:32], y[32:]
    return x1 * cos - x2 * sin, x1 * sin + x2 * cos


def _log_sigmoid(x):
    return jnp.minimum(x, 0.0) - jnp.log(1.0 + jnp.exp(-jnp.abs(x)))


def _nt(a, b):
    return lax.dot_general(a, b, (((1,), (1,)), ((), ())), preferred_element_type=F32)


def _max_sq_norm_row(parts):
    n2 = sum(jnp.sum(jnp.square(p.astype(BF16).astype(F32)), axis=0, keepdims=True) for p in parts)
    return jnp.broadcast_to(jnp.max(n2, axis=1, keepdims=True), (1, 128))


def _even_in_kernel(*refs, n_stream):
    (mods_ref, g_ref, wn_ref, wt_ref, wlr_ref, blr_ref, gq_ref, gk_ref, cos_ref, sin_ref,
     qk_ref, la_ref, gv_ref, gg_ref, aq_ref, ak_ref, av_ref, kn_ref) = refs[n_stream:]
    tile = pl.program_id(1)
    hs = [_normed_input(_stream_tile(refs[:n_stream], s, tile), mods_ref[s], g_ref[0:1]) for s in range(IN_PB)]
    ps = [jnp.dot(h, wn_ref[...], preferred_element_type=F32) for h in hs]
    pts = [_nt(wt_ref[...], h) for h in hs]
    cos, sin = cos_ref[...], sin_ref[...]
    gq, gk_gain = gq_ref[...], gk_ref[...]
    ones = jnp.ones((ONES_ROWS, TM), BF16)
    for s in range(IN_PB):
        p, pt_all = ps[s], pts[s]
        qk_ref[s, :, :GLA_QK] = p[:, :GLA_QK] * GLA_DK ** -0.5
        qk_ref[s, :, GLA_QK:] = p[:, GLA_QK:2 * GLA_QK]
        lr = p[:, 2 * GLA_QK:].astype(BF16)
        gk = jnp.dot(lr, wlr_ref[...], preferred_element_type=F32) + blr_ref[...]
        la_ref[s] = _log_sigmoid(gk) * (1.0 / GLA_NORMALIZER)
        gv_ref[s, 0] = pt_all[:GLA_V].astype(BF16)
        gg_ref[s, 0] = pt_all[GLA_V:2 * GLA_V]
        pt = pt_all[2 * GLA_V:]
        for hh in range(GQA_HEADS):
            blk = pt[hh * 64:(hh + 1) * 64]
            y = blk * lax.rsqrt(jnp.mean(blk * blk, axis=0, keepdims=True) + NORM_EPS) * gq
            o1, o2 = _rope_rows(y, cos, sin)
            aq_ref[s, 0, hh * 64:hh * 64 + 32, :] = (o1 * Q_SCALE).astype(BF16)
            aq_ref[s, 0, hh * 64 + 32:hh * 64 + 64, :] = (o2 * Q_SCALE).astype(BF16)
        kparts, knorms = [], []
        for hh in range(GQA_KV_HEADS):
            blk = pt[512 + hh * 64:512 + (hh + 1) * 64]
            y = blk * lax.rsqrt(jnp.mean(blk * blk, axis=0, keepdims=True) + NORM_EPS) * gk_gain
            o1, o2 = _rope_rows(y, cos, sin)
            kparts += [o1, o2]
            knorms.append(_max_sq_norm_row([o1, o2]))
        ak_ref[s] = jnp.concatenate(kparts, axis=0).astype(BF16).T
        kn_ref[s, 0] = jnp.concatenate(knorms + [jnp.zeros((8 - GQA_KV_HEADS, 128), F32)], axis=0)
        for hh in range(GQA_KV_HEADS):
            av_ref[s, 0, hh * GQA_VROWS:hh * GQA_VROWS + 64, :] = pt[640 + hh * 64:640 + (hh + 1) * 64].astype(BF16)
            av_ref[s, 0, hh * GQA_VROWS + 64:(hh + 1) * GQA_VROWS, :] = ones


def _odd_in_kernel(xs_ref, prev_ref, next_ref, mods_ref, g_ref, wn_ref, wt_ref, cw_ref, cb_ref, cos_ref, sin_ref,
                   v_ref, x1_ref, x2_ref, dq_ref, dk_ref, dv_ref, kn_ref, *, n_tiles):
    i = pl.program_id(1)
    hs = [_normed_input(xs_ref[s], mods_ref[s], g_ref[0:1]) for s in range(IN_PB)]
    h_ext = [jnp.concatenate([hs[s], _normed_input(prev_ref[s], mods_ref[s], g_ref[0:1]),
                              _normed_input(next_ref[s], mods_ref[s], g_ref[0:1])], axis=0) for s in range(IN_PB)]
    us = [jnp.dot(h, wn_ref[...], preferred_element_type=F32) for h in h_ext]
    pts = [_nt(wt_ref[...], h) for h in hs]
    cos, sin = cos_ref[...], sin_ref[...]
    ones = jnp.ones((ONES_ROWS, TM), BF16)
    nv = 2 * HEAD_DIM
    cw = cw_ref[...]
    row_id = lax.broadcasted_iota(jnp.int32, (TM, HY_COLS), 0)
    no_prev = (i == 0) | (i == 1)
    no_next = (i == 0) | (i == n_tiles - 1)
    for s in range(IN_PB):
        pt = pts[s]
        u = us[s][:TM]
        prev_row = jnp.where(no_prev, 0.0, us[s][TM + 7:TM + 8])
        next_row = jnp.where(no_next, 0.0, us[s][TM + 8:TM + 9])
        up = jnp.where(row_id == 0, prev_row, pltpu.roll(u, 1, axis=0))
        dn = jnp.where(row_id == TM - 1, next_row, pltpu.roll(u, TM - 1, axis=0))
        conv = up * cw[0:1] + u * cw[1:2] + dn * cw[2:3] + cb_ref[...]
        v_ref[s] = conv[:, :HY_CH]
        x1_ref[s] = conv[:, HY_CH:2 * HY_CH]
        x2_ref[s] = conv[:, 2 * HY_CH:]
        kparts, knorms = [], []
        for hh in range(2 * DIFF_HEADS):
            o1, o2 = _rope_rows(pt[hh * 64:(hh + 1) * 64], cos, sin)
            dq_ref[s, 0, hh * 64:hh * 64 + 32, :] = (o1 * Q_SCALE).astype(BF16)
            dq_ref[s, 0, hh * 64 + 32:hh * 64 + 64, :] = (o2 * Q_SCALE).astype(BF16)
            k1, k2 = _rope_rows(pt[DIFF_QK + hh * 64:DIFF_QK + (hh + 1) * 64], cos, sin)
            kparts += [k1, k2]
            knorms.append(_max_sq_norm_row([k1, k2]))
        dk_ref[s] = jnp.concatenate(kparts, axis=0).astype(BF16).T
        kn_ref[s, 0] = jnp.concatenate(knorms, axis=0)
        for hh in range(DIFF_HEADS):
            rows = pt[2 * DIFF_QK + hh * nv:2 * DIFF_QK + (hh + 1) * nv]
            dv_ref[s, 0, hh * DIFF_VROWS:hh * DIFF_VROWS + nv, :] = rows.astype(BF16)
            dv_ref[s, 0, hh * DIFF_VROWS + nv:(hh + 1) * DIFF_VROWS, :] = ones


def _pair_perm(n_heads):
    base = np.concatenate([np.arange(0, HEAD_DIM, 2), np.arange(1, HEAD_DIM, 2)])
    return np.concatenate([h * HEAD_DIM + base for h in range(n_heads)])


def _even_in(xs, mods_l, g0, w_in, w_lr, b_lr, qk_g, cos_t, sin_t):
    b, t, d = _stream_shape(xs)
    nt = t // TM
    x_specs, x_args = _stream_specs(xs, IN_PB)
    c_v = 2 * GLA_QK
    c_g = c_v + GLA_V
    c_lr0 = c_g + GLA_V
    c_lr = c_lr0 + 2 * GLA_RANK
    wn = jnp.concatenate([w_in[:, :c_v], w_in[:, c_lr0:c_lr], jnp.zeros((d, 128 - 2 * GLA_RANK), F32)],
                         axis=1).astype(BF16)
    cq = c_lr + GQA_HEADS * HEAD_DIM
    ck = cq + GQA_KV_HEADS * HEAD_DIM
    wq = w_in[:, c_lr:cq][:, _pair_perm(GQA_HEADS)]
    wk = w_in[:, cq:ck][:, _pair_perm(GQA_KV_HEADS)]
    wt = jnp.concatenate([w_in[:, c_v:c_lr0], wq, wk, w_in[:, ck:]], axis=1).T.astype(BF16)
    wlr = jnp.zeros((128, 2 * GLA_QK), F32)
    wlr = wlr.at[:GLA_RANK, :GLA_QK].set(w_lr[0]).at[GLA_RANK:2 * GLA_RANK, GLA_QK:].set(w_lr[1]).astype(BF16)
    blr = b_lr.reshape(1, 2 * GLA_QK)
    perm = _pair_perm(1)
    gq = qk_g[0][perm].reshape(HEAD_DIM, 1)
    gk = qk_g[1][perm].reshape(HEAD_DIM, 1)
    tok = lambda n: pl.BlockSpec((IN_PB, TM, n), lambda bb, i: (bb, i, 0))
    tr = lambda n: pl.BlockSpec((IN_PB, 1, n, TM), lambda bb, i: (bb, i, 0, 0))
    return pl.pallas_call(
        functools.partial(_even_in_kernel, n_stream=len(x_args)),
        grid=(b // IN_PB, nt),
        in_specs=x_specs + [
                  pl.BlockSpec((IN_PB, N_MOD, d), lambda bb, i: (_mod_block(bb, i, IN_PB), 0, 0)),
                  _const_spec((1, d)),
                  _const_spec(wn.shape), _const_spec(wt.shape), _const_spec(wlr.shape), _const_spec(blr.shape),
                  _const_spec(gq.shape), _const_spec(gk.shape),
                  pl.BlockSpec((32, TM), lambda bb, i: (0, i)),
                  pl.BlockSpec((32, TM), lambda bb, i: (0, i))],
        out_specs=[tok(2 * GLA_QK), tok(2 * GLA_QK), tr(GLA_V), tr(GLA_V), tr(512), tok(128),
                   tr(GQA_KV_HEADS * GQA_VROWS), pl.BlockSpec((IN_PB, 1, 8, 128), lambda bb, i: (bb, i, 0, 0))],
        out_shape=[jax.ShapeDtypeStruct((b, t, 2 * GLA_QK), F32),
                   jax.ShapeDtypeStruct((b, t, 2 * GLA_QK), F32),
                   jax.ShapeDtypeStruct((b, nt, GLA_V, TM), BF16),
                   jax.ShapeDtypeStruct((b, nt, GLA_V, TM), F32),
                   jax.ShapeDtypeStruct((b, nt, 512, TM), BF16),
                   jax.ShapeDtypeStruct((b, t, 128), BF16),
                   jax.ShapeDtypeStruct((b, nt, GQA_KV_HEADS * GQA_VROWS, TM), BF16),
                   jax.ShapeDtypeStruct((b, nt, 8, 128), F32)],
        compiler_params=_params("parallel", "parallel"),
        name="even_in",
    )(*x_args, mods_l, g0, wn, wt, wlr, blr, gq, gk, cos_t, sin_t)


def _odd_in(xs, mods_l, g0, w_in, conv_w, conv_b, cos_t, sin_t):
    b, t, d = xs.shape
    nt = t // TM
    r8 = TM // 8
    last_blk = t // 8 - 1
    wn = w_in[:, :HY_COLS].astype(BF16)
    perm = _pair_perm(2 * DIFF_HEADS)
    wq = w_in[:, HY_COLS:HY_COLS + DIFF_QK][:, perm]
    wk = w_in[:, HY_COLS + DIFF_QK:HY_COLS + 2 * DIFF_QK][:, perm]
    wt = jnp.concatenate([wq, wk, w_in[:, HY_COLS + 2 * DIFF_QK:]], axis=1).T.astype(BF16)
    tok = lambda n: pl.BlockSpec((IN_PB, TM, n), lambda bb, i: (bb, i, 0))
    tr = lambda n: pl.BlockSpec((IN_PB, 1, n, TM), lambda bb, i: (bb, i, 0, 0))
    hy = jax.ShapeDtypeStruct((b, t, HY_CH), F32)
    return pl.pallas_call(
        functools.partial(_odd_in_kernel, n_tiles=nt),
        grid=(b // IN_PB, nt),
        in_specs=[tok(d),
                  pl.BlockSpec((IN_PB, 8, d), lambda bb, i: (bb, jnp.maximum(i * r8 - 1, 0), 0)),
                  pl.BlockSpec((IN_PB, 8, d), lambda bb, i: (bb, jnp.minimum((i + 1) * r8, last_blk), 0)),
                  pl.BlockSpec((IN_PB, N_MOD, d), lambda bb, i: (_mod_block(bb, i, IN_PB), 0, 0)),
                  _const_spec((1, d)),
                  _const_spec(wn.shape), _const_spec(wt.shape),
                  _const_spec(conv_w.shape), _const_spec((1, HY_COLS)),
                  pl.BlockSpec((32, TM), lambda bb, i: (0, i)),
                  pl.BlockSpec((32, TM), lambda bb, i: (0, i))],
        out_specs=[tok(HY_CH), tok(HY_CH), tok(HY_CH), tr(DIFF_QK), tok(DIFF_QK), tr(DIFF_HEADS * DIFF_VROWS),
                   pl.BlockSpec((IN_PB, 1, 8, 128), lambda bb, i: (bb, i, 0, 0))],
        out_shape=[hy, hy, hy,
                   jax.ShapeDtypeStruct((b, nt, DIFF_QK, TM), BF16),
                   jax.ShapeDtypeStruct((b, t, DIFF_QK), BF16),
                   jax.ShapeDtypeStruct((b, nt, DIFF_HEADS * DIFF_VROWS, TM), BF16),
                   jax.ShapeDtypeStruct((b, nt, 8, 128), F32)],
        compiler_params=_params("parallel", "parallel"),
        name="odd_in",
    )(xs, xs, xs, mods_l, g0, wn, wt, conv_w, conv_b.reshape(1, HY_COLS), cos_t, sin_t)


def _attn_core(q_ref, k_ref, v_ref, kmax_ref, scratch, *, n_kv, k_col, k_row, q_slot, v_row, v_rows, scores_between,
               vpu_denominator):
    qpad_ref, bound_ref, m_ref, acc_ref, s0_ref, s1_ref, c0_ref, c1_ref, p0_ref, p1_ref = scratch
    s_refs, c_refs, p_refs = (s0_ref, s1_ref), (c0_ref, c1_ref), (p0_ref, p1_ref)
    i = pl.program_id(1)
    zeros = jnp.zeros((HEAD_DIM, TM), BF16)
    bound_max = jnp.zeros((1, TM), F32)
    for hh in range(N_SCORE_HEADS):
        qh = q_ref[0, 0, hh * HEAD_DIM:(hh + 1) * HEAD_DIM, :]
        qpad_ref[hh] = jnp.concatenate([qh, zeros] if q_slot(hh) == 0 else [zeros, qh], axis=0)
        qf = qh.astype(F32)
        q_norm2 = jnp.sum(qf * qf, axis=0, keepdims=True)
        k_norm2 = kmax_ref[0, k_row(hh):k_row(hh) + 1, 0:1]
        bound = jnp.sqrt(q_norm2 * k_norm2) * SCORE_BOUND_SLACK
        bound_ref[hh] = bound
        bound_max = jnp.maximum(bound_max, bound)
    acc_ref[...] = jnp.zeros(acc_ref.shape, F32)
    n = jnp.where(i == 0, 1, n_kv)
    fixed_reference = jnp.max(bound_max) <= SCORE_BOUND_LIMIT

    def produce(j, hh, slot):
        kc = k_ref[0, pl.ds(pl.multiple_of(j * TM, TM), TM), k_col(hh):k_col(hh) + 128]
        s = jnp.dot(kc, qpad_ref[hh], preferred_element_type=F32)
        p = jnp.exp2(s - bound_ref[hh])
        if vpu_denominator:
            m_ref[hh] = m_ref[hh] + jnp.sum(p, axis=0, keepdims=True)
        p_refs[slot][hh] = p.astype(BF16)

    n_val = v_rows - ONES_ROWS if vpu_denominator else v_rows

    def reduce(j, hh, slot):
        vc = v_ref[0, j, v_row(hh):v_row(hh) + n_val, :]
        acc_ref[hh, :n_val] = acc_ref[hh, :n_val] + jnp.dot(vc, p_refs[slot][hh], preferred_element_type=F32)

    unroll = math.gcd(n_kv - 1, FAST_UNROLL_MAX)

    @pl.when(fixed_reference)
    def _():
        m_ref[...] = jnp.zeros(m_ref.shape, F32)
        for hh in range(N_SCORE_HEADS):
            produce(0, hh, 0)

        def fast_step(j, slot):
            for hh in range(N_SCORE_HEADS):
                produce(j + 1, hh, 1 - slot)
                reduce(j, hh, slot)

        def fast_body(jj, carry):
            for u in range(unroll):
                fast_step(unroll * jj + u, u % 2)
            return carry

        lax.fori_loop(0, (n - 1) // unroll, fast_body, 0)
        for hh in range(N_SCORE_HEADS):
            reduce(n - 1, hh, 0)
            if vpu_denominator:
                acc_ref[hh, n_val:, :] = jnp.broadcast_to(m_ref[hh], (ONES_ROWS, TM))

    @pl.when(jnp.logical_not(fixed_reference))
    def _():
        _attn_running_max(k_ref, v_ref, qpad_ref, m_ref, acc_ref, s_refs, c_refs, n=n, k_col=k_col, v_row=v_row,
                          v_rows=v_rows, scores_between=scores_between)


def _attn_running_max(k_ref, v_ref, qpad_ref, m_ref, acc_ref, s_refs, c_refs, *, n, k_col, v_row, v_rows,
                      scores_between):
    m_ref[...] = jnp.full(m_ref.shape, -jnp.inf, F32)

    def scores(j, hh, slot):
        kc = k_ref[0, pl.ds(pl.multiple_of(j * TM, TM), TM), k_col(hh):k_col(hh) + 128]
        s = jnp.dot(kc, qpad_ref[hh], preferred_element_type=F32)
        s_refs[slot][hh] = s
        c_refs[slot][hh] = jnp.max(s, axis=0, keepdims=True)

    def probs(hh, slot):
        m_old = m_ref[hh]
        m_new = jnp.maximum(m_old, c_refs[slot][hh])
        return m_old, m_new, jnp.exp2(s_refs[slot][hh] - m_new).astype(BF16)

    def accumulate(j, hh, state):
        m_old, m_new, p = state
        vc = v_ref[0, j, v_row(hh):v_row(hh) + v_rows, :]
        alpha = jnp.exp2(m_old - m_new)
        acc_ref[hh] = acc_ref[hh] * alpha + jnp.dot(vc, p, preferred_element_type=F32)
        m_ref[hh] = m_new

    for hh in range(N_SCORE_HEADS):
        scores(0, hh, 0)

    def step(j, slot):
        for hh in range(N_SCORE_HEADS):
            state = probs(hh, slot)
            if scores_between:
                scores(j + 1, hh, 1 - slot)
            accumulate(j, hh, state)
            if not scores_between:
                scores(j + 1, hh, 1 - slot)

    def body(jj, carry):
        step(2 * jj, 0)
        step(2 * jj + 1, 1)
        return carry

    lax.fori_loop(0, (n - 1) // 2, body, 0)
    for hh in range(N_SCORE_HEADS):
        accumulate(n - 1, hh, probs(hh, 0))


def _attn_scratch(v_rows):
    stat = pltpu.VMEM((N_SCORE_HEADS, 1, TM), F32)
    return [pltpu.VMEM((N_SCORE_HEADS, 128, TM), BF16),
            stat, stat,
            pltpu.VMEM((N_SCORE_HEADS, v_rows, TM), F32),
            pltpu.VMEM((N_SCORE_HEADS, TM, TM), F32), pltpu.VMEM((N_SCORE_HEADS, TM, TM), F32),
            stat, stat,
            pltpu.VMEM((N_SCORE_HEADS, TM, TM), BF16), pltpu.VMEM((N_SCORE_HEADS, TM, TM), BF16)]


def _gqa_kernel(q_ref, k_ref, v_ref, kmax_ref, o_ref, *scratch, n_kv):
    group = GQA_HEADS // GQA_KV_HEADS
    _attn_core(q_ref, k_ref, v_ref, kmax_ref, scratch, n_kv=n_kv,
               k_col=lambda h: 0, k_row=lambda h: h // group, q_slot=lambda h: h // group,
               v_row=lambda h: (h // group) * GQA_VROWS, v_rows=GQA_VROWS, scores_between=True,
               vpu_denominator=False)
    acc_ref = scratch[3]
    for hh in range(GQA_HEADS):
        a = acc_ref[hh]
        o_ref[0, 0, hh * HEAD_DIM:(hh + 1) * HEAD_DIM, :] = (a[:HEAD_DIM] / a[HEAD_DIM:HEAD_DIM + 1]).astype(o_ref.dtype)


def _gqa_attention(aq, ak, av, kmax):
    b, nt, nq, _ = aq.shape
    t = nt * TM
    assert nt % 2 == 1 and nt > 1
    return pl.pallas_call(
        functools.partial(_gqa_kernel, n_kv=nt),
        grid=(b, nt),
        in_specs=[pl.BlockSpec((1, 1, nq, TM), lambda bb, i: (bb, i, 0, 0)),
                  pl.BlockSpec((1, t, 128), lambda bb, i: (bb, 0, 0)),
                  pl.BlockSpec((1, nt, GQA_KV_HEADS * GQA_VROWS, TM), lambda bb, i: (bb, 0, 0, 0)),
                  pl.BlockSpec((1, 8, 128), lambda bb, i: (bb, 0, 0))],
        out_specs=pl.BlockSpec((1, 1, nq, TM), lambda bb, i: (bb, i, 0, 0)),
        out_shape=jax.ShapeDtypeStruct((b, nt, nq, TM), BF16),
        scratch_shapes=_attn_scratch(GQA_VROWS),
        compiler_params=_params("parallel", "arbitrary"),
        name="gqa_attn",
    )(aq, ak, av, kmax)


def _diff_kernel(q_ref, k_ref, v_ref, kmax_ref, lam_ref, gain_ref, o_ref, *scratch, n_kv, lam_init):
    _attn_core(q_ref, k_ref, v_ref, kmax_ref, scratch, n_kv=n_kv,
               k_col=lambda hc: (hc // 2) * 128, k_row=lambda hc: hc, q_slot=lambda hc: hc % 2,
               v_row=lambda hc: (hc // 2) * DIFF_VROWS, v_rows=DIFF_VROWS, scores_between=False,
               vpu_denominator=True)
    acc_ref = scratch[3]
    lp = lam_ref[...]
    lam = (jnp.exp(jnp.sum(lp[0:1] * lp[1:2], axis=1, keepdims=True))
           - jnp.exp(jnp.sum(lp[2:3] * lp[3:4], axis=1, keepdims=True)) + lam_init)
    gain = gain_ref[...]
    nv = 2 * HEAD_DIM
    for hh in range(DIFF_HEADS):
        a0, a1 = acc_ref[2 * hh], acc_ref[2 * hh + 1]
        o = a0[:nv] / a0[nv:nv + 1] - lam * (a1[:nv] / a1[nv:nv + 1])
        y = o * lax.rsqrt(jnp.mean(o * o, axis=0, keepdims=True) + NORM_EPS) * gain
        o_ref[0, 0, hh * nv:(hh + 1) * nv, :] = (y * (1.0 - lam_init)).astype(o_ref.dtype)


def _diff_attention(dq, dk, dv, kmax, lam_p, gain, layer_idx):
    b, nt, nq, _ = dq.shape
    t = nt * TM
    assert nt % 2 == 1 and nt > 1
    lam_init = 0.8 - 0.6 * math.exp(-0.3 * layer_idx)
    return pl.pallas_call(
        functools.partial(_diff_kernel, n_kv=nt, lam_init=lam_init),
        grid=(b, nt),
        in_specs=[pl.BlockSpec((1, 1, nq, TM), lambda bb, i: (bb, i, 0, 0)),
                  pl.BlockSpec((1, t, DIFF_QK), lambda bb, i: (bb, 0, 0)),
                  pl.BlockSpec((1, nt, DIFF_HEADS * DIFF_VROWS, TM), lambda bb, i: (bb, 0, 0, 0)),
                  pl.BlockSpec((1, 8, 128), lambda bb, i: (bb, 0, 0)),
                  _const_spec(lam_p.shape),
                  _const_spec((2 * HEAD_DIM, 1))],
        out_specs=pl.BlockSpec((1, 1, DIFF_V, TM), lambda bb, i: (bb, i, 0, 0)),
        out_shape=jax.ShapeDtypeStruct((b, nt, DIFF_V, TM), BF16),
        scratch_shapes=_attn_scratch(DIFF_VROWS),
        compiler_params=_params("parallel", "arbitrary"),
        name="diff_attn",
    )(dq, dk, dv, kmax, lam_p, gain.reshape(2 * HEAD_DIM, 1))


def _split_bf16(x):
    hi = x.astype(BF16)
    return hi, (x - hi.astype(F32)).astype(BF16)


def _gla_kernel(qk_ref, la_ref, vt_ref, o_ref, st_ref, *, backward):
    @pl.when(pl.program_id(1) == 0)
    def _():
        st_ref[...] = jnp.zeros(st_ref.shape, F32)

    n_chunk = TM // GLA_CHUNK
    r = lax.broadcasted_iota(jnp.int32, (TM, TM), 0)
    c = lax.broadcasted_iota(jnp.int32, (TM, TM), 1)
    same = (r // GLA_CHUNK) == (c // GLA_CHUNK)
    causal = (c >= r) if backward else (c <= r)
    tri = jnp.where(same & causal, 1.0, 0.0).astype(BF16)
    blk = jnp.where(same, 1.0, 0.0).astype(BF16)
    lane = lax.broadcasted_iota(jnp.int32, (TM, 128), 1)
    row = lax.broadcasted_iota(jnp.int32, (TM, 128), 0)
    keep = same & ((r >= c) if backward else (r <= c))
    order = range(n_chunk - 1, -1, -1) if backward else range(n_chunk)
    units = [(s, h) for s in range(GLA_PB) for h in range(GLA_HEADS)]
    pair = lambda h: slice((h // 2) * 128, (h // 2 + 1) * 128)
    q_in, k_in, k_end, dec = [], [], [], []
    for s in range(GLA_PB):
        la_hi, la_lo = _split_bf16(la_ref[s])
        cum = jnp.dot(tri, la_hi, preferred_element_type=F32) + jnp.dot(tri, la_lo, preferred_element_type=F32)
        tot = jnp.dot(blk, la_hi, preferred_element_type=F32) + jnp.dot(blk, la_lo, preferred_element_type=F32)
        q = qk_ref[s, :, :GLA_QK]
        k = qk_ref[s, :, GLA_QK:]
        q_in.append((q * jnp.exp(cum)).astype(BF16))
        k_in.append(k * jnp.exp(-cum))
        k_end.append(k * jnp.exp(tot - cum))
        dec.append(jnp.exp(tot))
    head_lanes = {h: (lane // GLA_DK) == (h % 2) for h in range(GLA_HEADS)}
    q_pair = {(s, h): q_in[s][:, pair(h)] for s, h in units}
    k_in_h = {(s, h): jnp.where(head_lanes[h], k_in[s][:, pair(h)], 0.0).astype(BF16) for s, h in units}
    k_end_h = {(s, h): jnp.where(head_lanes[h], k_end[s][:, pair(h)], 0.0) for s, h in units}
    vts = {(s, h): vt_ref[s, 0, h * GLA_DV:(h + 1) * GLA_DV, :] for s, h in units}
    a_ts = {u: jnp.where(keep, _nt(k_in_h[u], q_pair[u]), 0.0).astype(BF16) for u in units}
    o_ts = {u: jnp.dot(vts[u], a_ts[u], preferred_element_type=F32) for u in units}
    s_ts = {(s, h): st_ref[s, h] for s, h in units}
    for ci in order:
        in_chunk = (row // GLA_CHUNK) == ci
        for u in units:
            s, h = u
            q_c = jnp.where(in_chunk, q_pair[u], jnp.zeros_like(q_pair[u]))
            o_ts[u] = o_ts[u] + _nt(s_ts[u].astype(BF16), q_c)
            k_c = jnp.where(in_chunk, k_end_h[u], 0.0).astype(BF16)
            s_ts[u] = (s_ts[u] * dec[s][ci * GLA_CHUNK:ci * GLA_CHUNK + 1, pair(h)]
                       + jnp.dot(vts[u], k_c, preferred_element_type=F32))
    for s, h in units:
        st_ref[s, h] = s_ts[(s, h)]
        o_ref[s, 0, h * GLA_DV:(h + 1) * GLA_DV, :] = o_ts[(s, h)]


def _gla_direction(qk, la, vt, backward):
    b, t, _ = qk.shape
    nt = t // TM
    d = 1 if backward else 0
    tile = (lambda i: jnp.where(i == 0, 0, nt - i)) if backward else (lambda i: i)
    return pl.pallas_call(
        functools.partial(_gla_kernel, backward=backward),
        grid=(b // GLA_PB, nt),
        in_specs=[pl.BlockSpec((GLA_PB, TM, 2 * GLA_QK), lambda bb, i: (bb, tile(i), 0)),
                  pl.BlockSpec((GLA_PB, TM, GLA_QK), lambda bb, i: (bb, tile(i), d)),
                  pl.BlockSpec((GLA_PB, 1, GLA_V, TM), lambda bb, i: (bb, tile(i), 0, 0))],
        out_specs=pl.BlockSpec((GLA_PB, 1, GLA_V, TM), lambda bb, i: (bb, tile(i), 0, 0)),
        out_shape=jax.ShapeDtypeStruct((b, nt, GLA_V, TM), F32),
        scratch_shapes=[pltpu.VMEM((GLA_PB, GLA_HEADS, GLA_DV, 128), F32)],
        compiler_params=_params("parallel", "arbitrary"),
        name="gla_bwd" if backward else "gla_fwd",
    )(qk, la, vt)


def _post_phases(xs, mods, g, yas, ybs, woa_ref, wob_ref, w1_ref, w2_ref, ff_chunk):
    n = len(xs)
    ys = [jnp.dot(yas[s], woa_ref[...], preferred_element_type=F32)
          + jnp.dot(ybs[s].T, wob_ref[...], preferred_element_type=F32) for s in range(n)]
    x1s = [xs[s] + mods[s][2:3] * _rms(ys[s], g[1:2]) for s in range(n)]
    hs = [(_rms(x1s[s], g[2:3]) * (1.0 + mods[s][4:5]) + mods[s][3:4]).astype(BF16) for s in range(n)]
    d_ff = w1_ref.shape[1]
    accs = [jnp.zeros(xs[0].shape, F32) for _ in range(n)]
    for c in range(d_ff // ff_chunk):
        cols = slice(c * ff_chunk, (c + 1) * ff_chunk)
        for s in range(n):
            u = jnp.dot(hs[s], w1_ref[:, cols], preferred_element_type=F32)
            u = jnp.square(jnp.maximum(u, 0.0)).astype(BF16)
            accs[s] = accs[s] + jnp.dot(u, w2_ref[cols, :], preferred_element_type=F32)
    return [x1s[s] + mods[s][5:6] * _rms(accs[s], g[3:4]) for s in range(n)]


def _post_gla_kernel(*refs, ff_chunk, n_stream, off):
    of_ref, ob_ref, gg_ref, gain_ref, yb_ref, mods_ref, g_ref, woa_ref, wob_ref, w1_ref, w2_ref, o_ref = refs[n_stream:]
    tile = pl.program_id(1) + off
    gain = gain_ref[...]
    yas = []
    for s in range(POST_PB):
        parts = []
        for hh in range(GLA_HEADS):
            rows = slice(hh * GLA_DV, (hh + 1) * GLA_DV)
            o = of_ref[s, 0, rows, :] + ob_ref[s, 0, rows, :]
            gt = gg_ref[s, 0, rows, :]
            y = o * lax.rsqrt(jnp.mean(o * o, axis=0, keepdims=True) + NORM_EPS) * gain
            parts.append(y * (gt / (1.0 + jnp.exp(-gt))))
        yas.append(jnp.concatenate(parts, axis=0).astype(BF16).T)
    outs = _post_phases([_stream_tile(refs[:n_stream], s, tile) for s in range(POST_PB)],
                        [mods_ref[s] for s in range(POST_PB)], g_ref[...], yas,
                        [yb_ref[s, 0] for s in range(POST_PB)], woa_ref, wob_ref, w1_ref, w2_ref, ff_chunk)
    for s in range(POST_PB):
        o_ref[s] = outs[s]


def _post_kernel(*refs, ff_chunk, n_stream, off):
    ya_ref, yb_ref, mods_ref, g_ref, woa_ref, wob_ref, w1_ref, w2_ref, o_ref = refs[n_stream:]
    tile = pl.program_id(1) + off
    outs = _post_phases([_stream_tile(refs[:n_stream], s, tile) for s in range(POST_PB)],
                        [mods_ref[s] for s in range(POST_PB)], g_ref[...],
                        [ya_ref[s].astype(BF16) for s in range(POST_PB)], [yb_ref[s, 0] for s in range(POST_PB)],
                        woa_ref, wob_ref, w1_ref, w2_ref, ff_chunk)
    for s in range(POST_PB):
        o_ref[s] = outs[s]


def _post(xs, ya, yb_t, mods_l, g, w_out, w1, w2, skip_ctx):
    b, t, d = _stream_shape(xs)
    nt = t // TM
    pb = POST_PB
    half = w_out.shape[0] // 2
    woa = w_out[:half].astype(BF16)
    wob = w_out[half:].astype(BF16)
    w1 = w1.astype(BF16)
    w2 = w2.astype(BF16)
    off = 1 if skip_ctx else 0
    x_specs, x_args = _stream_specs(xs, pb, off)
    tok = lambda n: pl.BlockSpec((pb, TM, n), lambda bb, i: (bb, i + off, 0))
    tr = lambda n: pl.BlockSpec((pb, 1, n, TM), lambda bb, i: (bb, i + off, 0, 0))
    if isinstance(ya, tuple):
        o_f, o_b, gg, gain = ya
        body, name = _post_gla_kernel, "post_gla"
        a_specs = [tr(half), tr(half), tr(half), _const_spec((GLA_DV, 1))]
        a_args = [o_f, o_b, gg, gain.reshape(GLA_DV, 1)]
    else:
        body, name = _post_kernel, "post"
        a_specs, a_args = [tok(half)], [ya]
    return pl.pallas_call(
        functools.partial(body, ff_chunk=1024, n_stream=len(x_args), off=off),
        grid=(b // pb, nt - off),
        in_specs=x_specs + a_specs + [
                  tr(half),
                  pl.BlockSpec((pb, N_MOD, d), lambda bb, i: (_mod_block(bb, i + off, pb), 0, 0)),
                  _const_spec(g.shape),
                  _const_spec(woa.shape), _const_spec(wob.shape), _const_spec(w1.shape), _const_spec(w2.shape)],
        out_specs=pl.BlockSpec((pb, TM, d), lambda bb, i: (bb, i, 0)),
        out_shape=jax.ShapeDtypeStruct((b, t - off * TM, d), F32),
        compiler_params=_params("parallel", "parallel"),
        name=name,
    )(*x_args, *a_args, yb_t, mods_l, g, woa, wob, w1, w2)


def _dot3(a, b):
    ah, al = _split_bf16(a)
    bh, bl = _split_bf16(b)
    d = lambda x, y: jnp.dot(x, y, preferred_element_type=F32)
    return d(ah, bh) + (d(al, bh) + d(ah, bl))


def _filter_kernel(w1t_ref, w1c_ref, w1s_ref, b1_ref, w2_ref, b2_ref, w3_ref, delta_ref, o_ref, *, seq, rb):
    tap = pl.program_id(0) * rb + lax.broadcasted_iota(jnp.int32, (rb, 1), 0)
    d = tap.astype(F32)
    t = d / max(seq - 1, 1)
    w = (2 * math.pi / seq) * d
    band = lax.broadcasted_iota(jnp.int32, (1, HY_BANDS), 1).astype(F32)
    bands = 1e-4 + band * ((HY_BANDS - 1 - 1e-4) / (HY_BANDS - 1))
    ang = w * bands
    pre = t * w1t_ref[...] + _dot3(jnp.cos(ang), w1c_ref[...]) - _dot3(jnp.sin(ang), w1s_ref[...]) + b1_ref[...]
    hid = jnp.sin(pre)
    hid = jnp.sin(_dot3(hid, w2_ref[...]) + b2_ref[...])
    decay = jnp.exp(-t * delta_ref[...])
    decay_bwd = jnp.where(tap == 0, 0.0, decay)
    for o in range(HY_ORDER):
        o_ref[2 * o] = (_dot3(hid, w3_ref[2 * o]) * decay).astype(o_ref.dtype)
        o_ref[2 * o + 1] = (_dot3(hid, w3_ref[2 * o + 1]) * decay_bwd).astype(o_ref.dtype)


def _hyena_kernels(seq, f_w1, f_b1, f_w2, f_b2, f_w3):
    rb = min(512, seq)
    ffn = f_w2.shape[0]
    w3 = f_w3.reshape(ffn, 2 * HY_ORDER, HY_CH).transpose(1, 0, 2)
    deltas = jnp.abs(jnp.linspace(math.log(HY_TARGET) / HY_FAST_PCT, math.log(HY_TARGET) / HY_SLOW_PCT,
                                  HY_CH, dtype=F32)).reshape(1, HY_CH)
    return pl.pallas_call(
        functools.partial(_filter_kernel, seq=seq, rb=rb),
        grid=(seq // rb,),
        in_specs=[_const_spec((1, ffn)), _const_spec((HY_BANDS, ffn)), _const_spec((HY_BANDS, ffn)),
                  _const_spec((1, ffn)), _const_spec((ffn, ffn)), _const_spec((1, ffn)),
                  _const_spec(w3.shape), _const_spec((1, HY_CH))],
        out_specs=pl.BlockSpec((2 * HY_ORDER, rb, HY_CH), lambda j: (0, j, 0)),
        out_shape=jax.ShapeDtypeStruct((2 * HY_ORDER, seq, HY_CH), BF16),
        compiler_params=_params("parallel"),
        name="hy_filter",
    )(f_w1[0:1], f_w1[1:1 + HY_BANDS], f_w1[1 + HY_BANDS:], f_b1.reshape(1, ffn), f_w2, f_b2.reshape(1, ffn), w3,
      deltas)


def _dft_tables(seq):
    n = 2 * seq
    r = math.isqrt(n)
    assert r * r == n
    r1 = (r // 2 + 1 + 7) // 8 * 8
    idx = jnp.arange(r, dtype=jnp.int32)
    phi = (2 * math.pi / r) * ((idx[:r1, None] * idx[None, :]) % r).astype(F32)
    f1 = jnp.concatenate([jnp.cos(phi), -jnp.sin(phi)], axis=0).astype(BF16)
    m = (idx[:r1, None, None] + r * idx[None, :, None]) * idx[None, None, :]
    th = (2 * math.pi / n) * (m % n).astype(F32)
    c, s = jnp.cos(th), jnp.sin(th)
    blockm = lambda re, im: jnp.concatenate([jnp.concatenate([re, -im], axis=-1),
                                             jnp.concatenate([im, re], axis=-1)], axis=-2)
    g = blockm(c, -s).astype(BF16)
    ginv = blockm(jnp.swapaxes(c, 1, 2), jnp.swapaxes(s, 1, 2)).astype(BF16)
    k1 = idx[:r1]
    weight = jnp.where((k1 == 0) | (k1 == r // 2), 1.0, jnp.where(k1 < r // 2, 2.0, 0.0)) / n
    phi_t = phi.T[:r // 2]
    finv = jnp.concatenate([jnp.cos(phi_t) * weight, -jnp.sin(phi_t) * weight], axis=1).astype(BF16)
    return r, f1, g, ginv, finv


DFT_BB = 32


def _dft1_kernel(f_ref, x_ref, o_ref, *, a0):
    f = f_ref[...]
    r = o_ref.shape[2]
    x = pltpu.einshape("abc->bac", x_ref[0, a0:].astype(BF16))
    ys = [jnp.dot(f, x[j], preferred_element_type=F32) for j in range(DFT_BB)]
    y = pltpu.einshape("jkc->kjc", jnp.stack([v.astype(o_ref.dtype) for v in ys], axis=0))
    o_ref[0, 0] = y[:r]
    o_ref[0, 1] = y[r:]


def _dft_stage1(f, x4, a0=0):
    b, rows, r, c = x4.shape
    r1 = f.shape[0] // 2
    return pl.pallas_call(
        functools.partial(_dft1_kernel, a0=a0),
        grid=(b, r // DFT_BB),
        in_specs=[_const_spec(f.shape), pl.BlockSpec((1, rows, DFT_BB, c), lambda bi, j: (bi, 0, j, 0))],
        out_specs=pl.BlockSpec((1, 2, r1, DFT_BB, c), lambda bi, j: (bi, 0, 0, j, 0)),
        out_shape=jax.ShapeDtypeStruct((b, 2, r1, r, c), BF16),
        compiler_params=_params("parallel", "parallel"),
        name="hy_dft1",
    )(f, x4)


def _fspec_kernel(af_ref, ab_ref, g_ref, o_ref, *, kb, r):
    for kk in range(kb):
        xf = jnp.dot(g_ref[kk], af_ref[0, :, kk].reshape(2 * r, af_ref.shape[-1]), preferred_element_type=F32)
        xb = jnp.dot(g_ref[kk], ab_ref[0, :, kk].reshape(2 * r, ab_ref.shape[-1]), preferred_element_type=F32)
        o_ref[0, kk, :r] = xf[:r] + xb[:r]
        o_ref[0, kk, r:] = xf[r:] - xb[r:]


def _filter_spectrum(a, g, kb):
    o2, _, r1, r, c = a.shape
    blk = lambda d: pl.BlockSpec((1, 2, kb, r, c), lambda oo, j: (2 * oo + d, 0, j, 0, 0))
    return pl.pallas_call(
        functools.partial(_fspec_kernel, kb=kb, r=r),
        grid=(o2 // 2, r1 // kb),
        in_specs=[blk(0), blk(1), pl.BlockSpec((kb, 2 * r, 2 * r), lambda oo, j: (j, 0, 0))],
        out_specs=pl.BlockSpec((1, kb, 2 * r, c), lambda oo, j: (oo, j, 0, 0)),
        out_shape=jax.ShapeDtypeStruct((o2 // 2, r1, 2 * r, c), F32),
        compiler_params=_params("parallel", "parallel"),
        name="hy_fspec",
    )(a, a, g)


def _dft2_kernel(a_ref, g_ref, gi_ref, k_ref, o_ref, *, kb, r):
    xs = [jnp.dot(g_ref[kk], a_ref[0, :, kk].reshape(2 * r, a_ref.shape[-1]), preferred_element_type=F32)
          for kk in range(kb)]
    ys = []
    for kk in range(kb):
        x, ks = xs[kk], k_ref[0, kk]
        xr, xi, kr, ki = x[:r], x[r:], ks[:r], ks[r:]
        ys.append(jnp.concatenate([xr * kr - xi * ki, xr * ki + xi * kr], axis=0).astype(BF16))
    bqs = [jnp.dot(gi_ref[kk], ys[kk], preferred_element_type=F32) for kk in range(kb)]
    for kk in range(kb):
        o_ref[0, 0, kk] = bqs[kk][:r].astype(o_ref.dtype)
        o_ref[0, 1, kk] = bqs[kk][r:].astype(o_ref.dtype)


def _dft_stage2(a, g, ginv, kspec, order, kb):
    b, _, r1, r, c = a.shape
    blk = pl.BlockSpec((1, 2, kb, r, c), lambda j, bb: (bb, 0, j, 0, 0))
    tab = pl.BlockSpec((kb, 2 * r, 2 * r), lambda j, bb: (j, 0, 0))
    return pl.pallas_call(
        functools.partial(_dft2_kernel, kb=kb, r=r),
        grid=(r1 // kb, b),
        in_specs=[blk, tab, tab, pl.BlockSpec((1, kb, 2 * r, c), lambda j, bb: (order, j, 0, 0))],
        out_specs=blk,
        out_shape=jax.ShapeDtypeStruct(a.shape, BF16),
        compiler_params=_params("parallel", "arbitrary"),
        name="hy_dft2",
    )(a, g, ginv, kspec)


def _dft3_kernel(f_ref, b_ref, gate_ref, z_ref, bias_ref, head_ref, o_ref, *, a0):
    f = f_ref[...]
    bq = jnp.concatenate([b_ref[0, 0], b_ref[0, 1]], axis=0)
    bq = pltpu.einshape("kjc->jkc", bq)
    ys = [jnp.dot(f, bq[j], preferred_element_type=F32) for j in range(DFT_BB)]
    y = pltpu.einshape("jac->ajc", jnp.stack(ys, axis=0))
    o_ref[0, :a0] = head_ref[0]
    o_ref[0, a0:] = gate_ref[0, a0:] * (y + z_ref[0, a0:] * bias_ref[...])


def _dft_stage3(finv, bq5, gate4, z4, bias, head4):
    b, _, r1, r, c = bq5.shape
    rows = z4.shape[1]
    a0 = head4.shape[1]
    io = pl.BlockSpec((1, rows, DFT_BB, c), lambda bi, j: (bi, 0, j, 0))
    return pl.pallas_call(
        functools.partial(_dft3_kernel, a0=a0),
        grid=(b, r // DFT_BB),
        in_specs=[_const_spec(finv.shape), pl.BlockSpec((1, 2, r1, DFT_BB, c), lambda bi, j: (bi, 0, 0, j, 0)), io, io,
                  _const_spec((1, 1, c)), pl.BlockSpec((1, a0, DFT_BB, c), lambda bi, j: (bi, 0, j, 0))],
        out_specs=io,
        out_shape=jax.ShapeDtypeStruct((b, rows, r, c), F32),
        compiler_params=_params("parallel", "parallel"),
        name="hy_dft3",
    )(finv, bq5, gate4, z4, bias, head4)


def _long_conv_gated(v, gates, kern, hy_bias, ctx_out, lc):
    b, t, c = v.shape
    r, f1, g, ginv, finv = _dft_tables(t - lc)
    assert lc % r == 0
    a0 = lc // r
    kb = 8
    kspec = _filter_spectrum(_dft_stage1(f1[:, :r // 2], kern.reshape(2 * HY_ORDER, r // 2, r, c)), g, kb)
    as_blocks = lambda x: x.reshape(b, t // r, r, c)
    z = as_blocks(v)
    heads = [jnp.zeros((b, a0, r, c), F32), ctx_out.reshape(b, a0, r, c)]
    for o in range(HY_ORDER):
        a = _dft_stage1(f1[:, :r // 2], z, a0)
        bq = _dft_stage2(a, g, ginv, kspec, o, kb)
        z = _dft_stage3(finv, bq, as_blocks(gates[o]), z, hy_bias[o].reshape(1, 1, c), heads[o])
    return z.reshape(b, t, c)


def _ctx_conv_kernel(v_ref, x1_ref, x2_ref, k_ref, ff_ref, fi_ref, bias_ref, o_ref):
    n = ff_ref.shape[1]
    z = v_ref[0]
    for o, gate_ref in enumerate((x1_ref, x2_ref)):
        kf = jnp.dot(ff_ref[:, :TM], k_ref[2 * o], preferred_element_type=F32)
        kb = jnp.dot(ff_ref[:, :TM], k_ref[2 * o + 1], preferred_element_type=F32)
        x = jnp.dot(ff_ref[:, :TM], z.astype(BF16), preferred_element_type=F32)
        xr, xi, kr, ki = x[:n], x[n:], kf[:n] + kb[:n], kf[n:] - kb[n:]
        y = jnp.concatenate([xr * kr - xi * ki, xr * ki + xi * kr], axis=0).astype(BF16)
        conv = jnp.dot(fi_ref[...], y, preferred_element_type=F32)
        z = gate_ref[0] * (conv + z * bias_ref[o:o + 1])
    o_ref[0] = z


def _ctx_conv_gated(v, x1, x2, kern, hy_bias):
    b, _, c = v.shape
    seq = TM
    n = 2 * seq
    idx = jnp.arange(n, dtype=jnp.int32)
    th = (2 * math.pi / n) * ((idx[:, None] * idx[None, :]) % n).astype(F32)
    ff = jnp.concatenate([jnp.cos(th), -jnp.sin(th)], axis=0).astype(BF16)
    fi = (jnp.concatenate([jnp.cos(th), -jnp.sin(th)], axis=1)[:seq] / n).astype(BF16)
    io = pl.BlockSpec((1, seq, c), lambda bb: (bb, 0, 0))
    return pl.pallas_call(
        _ctx_conv_kernel,
        grid=(b,),
        in_specs=[io, io, io, _const_spec(kern.shape), _const_spec(ff.shape), _const_spec(fi.shape),
                  _const_spec(hy_bias.shape)],
        out_specs=io,
        out_shape=jax.ShapeDtypeStruct((b, seq, c), F32),
        compiler_params=_params("parallel"),
        name="hy_ctx",
    )(v, x1, x2, kern, ff, fi, hy_bias)


def _hyena(v, x1, x2, lc, f_w1, f_b1, f_w2, f_b2, f_w3, hy_bias):
    seq = v.shape[1] - lc
    yc = _ctx_conv_gated(v, x1, x2, _hyena_kernels(lc, f_w1, f_b1, f_w2, f_b2, f_w3), hy_bias)
    return _long_conv_gated(v, (x1, x2), _hyena_kernels(seq, f_w1, f_b1, f_w2, f_b2, f_w3), hy_bias, yc, lc)


def _rope_tables_t(seq, lc):
    rows = seq // GRID_W
    r = jnp.broadcast_to(jnp.arange(rows, dtype=F32)[:, None], (rows, GRID_W)).reshape(-1)
    col = jnp.broadcast_to(jnp.arange(GRID_W, dtype=F32)[None, :], (rows, GRID_W)).reshape(-1)
    n_pairs = HEAD_DIM // 4
    inv = ROPE_THETA ** (-jnp.arange(n_pairs, dtype=F32) / n_pairs)
    ang = jnp.concatenate([r[:, None] * inv, col[:, None] * inv], axis=-1)
    cos = jnp.concatenate([jnp.ones((lc, 2 * n_pairs), F32), jnp.cos(ang)], axis=0)
    sin = jnp.concatenate([jnp.zeros((lc, 2 * n_pairs), F32), jnp.sin(ang)], axis=0)
    return cos.T, sin.T


def kernel(x, c, ctx, c_ctx, w_ada, b_ada, norm_g, w_mlp_in, w_mlp_out, ev_w_in, ev_w_lr, ev_b_lr, ev_gla_g,
           ev_qk_g, ev_w_out, od_w_in, od_conv_w, od_conv_b, od_f_w1, od_f_b1, od_f_w2, od_f_b2, od_f_w3,
           od_hy_bias, od_lam, od_diff_g, od_w_out):
    bsz, seq, d = x.shape
    lc = ctx.shape[1]
    depth = w_ada.shape[0]
    assert lc == TM and seq % TM == 0 and bsz <= MOD_CTX_ROW and bsz % POST_PB == 0 and bsz % IN_PB == 0 and bsz % GLA_PB == 0
    cos_t, sin_t = _rope_tables_t(seq, lc)
    cc = jnp.zeros((8, d), F32).at[:bsz].set(c).at[MOD_CTX_ROW:MOD_CTX_ROW + max(POST_PB, IN_PB)].set(c_ctx)
    mods = _mods(cc, w_ada, b_ada).reshape(depth, 8, N_MOD, d)
    xs = (x, ctx)
    for i in range(depth):
        last = i == depth - 1
        j = i // 2
        g = norm_g[i]
        if i % 2 == 0:
            qk, la, gv, gg, aq, ak, av, kn = _even_in(xs, mods[i], g[0:1], ev_w_in[j], ev_w_lr[j], ev_b_lr[j],
                                                  ev_qk_g[j], cos_t, sin_t)
            ya = (_gla_direction(qk, la, gv, False), _gla_direction(qk, la, gv, True), gg, ev_gla_g[j])
            yb = _gqa_attention(aq, ak, av, jnp.max(kn, axis=1))
            w_out = ev_w_out[j]
        else:
            hv, hx1, hx2, dq, dk, dv, kn = _odd_in(xs, mods[i], g[0:1], od_w_in[j], od_conv_w[j], od_conv_b[j],
                                                   cos_t, sin_t)
            ya = _hyena(hv, hx1, hx2, lc, od_f_w1[j], od_f_b1[j], od_f_w2[j], od_f_b2[j], od_f_w3[j], od_hy_bias[j])
            yb = _diff_attention(dq, dk, dv, jnp.max(kn, axis=1), od_lam[j], od_diff_g[j], i)
            w_out = od_w_out[j]
        xs = _post(xs, ya, yb, mods[i], g, w_out, w_mlp_in[i], w_mlp_out[i], skip_ctx=last)
    return xs
```

```python
import functools
import math

import jax
import jax.numpy as jnp
import numpy as np
from jax import lax
from jax.experimental import pallas as pl
from jax.experimental.pallas import tpu as pltpu

F32 = jnp.float32
BF16 = jnp.bfloat16

TM = 256
GRID_W = 64
HEAD_DIM = 64
ROPE_THETA = 10000.0
NORM_EPS = 1e-6
N_MOD = 6
VMEM_LIMIT = 56 * 1024 * 1024

GLA_HEADS = 4
GLA_DK = 64
GLA_DV = 128
GLA_RANK = 16
GLA_NORMALIZER = 16.0
GLA_CHUNK = 64
GLA_QK = GLA_HEADS * GLA_DK
GLA_V = GLA_HEADS * GLA_DV
GQA_HEADS = 8
GQA_KV_HEADS = 2
HY_CH = 512
HY_ORDER = 2
HY_BANDS = 16
HY_TARGET = 1e-2
HY_FAST_PCT = 0.3
HY_SLOW_PCT = 1.5
HY_COLS = (HY_ORDER + 1) * HY_CH
DIFF_HEADS = 4
DIFF_QK = DIFF_HEADS * 2 * HEAD_DIM
DIFF_V = DIFF_HEADS * 2 * HEAD_DIM
ONES_ROWS = 16
N_SCORE_HEADS = 8
LOG2E = 1.4426950408889634
Q_SCALE = HEAD_DIM ** -0.5 * LOG2E
GQA_VROWS = HEAD_DIM + ONES_ROWS
SCORE_BOUND_SLACK = 1.01
FAST_UNROLL_MAX = 32
SCORE_BOUND_LIMIT = 60.0
DIFF_VROWS = 2 * HEAD_DIM + ONES_ROWS


def _params(*sem):
    return pltpu.CompilerParams(dimension_semantics=sem, vmem_limit_bytes=VMEM_LIMIT)


def _const_spec(shape):
    nd = len(shape)
    return pl.BlockSpec(shape, lambda *_: (0,) * nd, pipeline_mode=pl.Buffered(1))


def _rms(x, g):
    return x * lax.rsqrt(jnp.mean(x * x, axis=-1, keepdims=True) + NORM_EPS) * g


MOD_CTX_ROW = 4
POST_PB = 2
IN_PB = 4
GLA_PB = 4


def _mod_block(bb, i, pb):
    return jnp.where(i == 0, MOD_CTX_ROW // pb, bb)


def _stream_specs(xs, pb, off=0):
    if isinstance(xs, tuple):
        x, ctx = xs
        d = x.shape[-1]
        return [pl.BlockSpec((pb, TM, d), lambda bb, i: (bb, jnp.maximum(i + off - 1, 0), 0)),
                pl.BlockSpec((pb, TM, d), lambda bb, i: (bb, 0, 0))], [x, ctx]
    return [pl.BlockSpec((pb, TM, xs.shape[-1]), lambda bb, i: (bb, i + off, 0))], [xs]


def _stream_shape(xs):
    if isinstance(xs, tuple):
        return xs[0].shape[0], xs[0].shape[1] + xs[1].shape[1], xs[0].shape[2]
    return xs.shape


def _stream_tile(stream_refs, s, tile):
    if len(stream_refs) == 2:
        return jnp.where(tile == 0, stream_refs[1][s], stream_refs[0][s])
    return stream_refs[0][s]


def _mods_kernel(cc_ref, w_ref, b_ref, o_ref):
    cc = cc_ref[...]
    s = cc / (1.0 + jnp.exp(-cc))
    o_ref[0] = jnp.dot(s.astype(BF16), w_ref[0].astype(BF16), preferred_element_type=F32) + b_ref[0]


def _mods(cc, w_ada, b_ada):
    depth, d, n = w_ada.shape
    nb = n // 4
    return pl.pallas_call(
        _mods_kernel,
        grid=(depth, n // nb),
        in_specs=[pl.BlockSpec((8, d), lambda l, j: (0, 0)),
                  pl.BlockSpec((1, d, nb), lambda l, j: (l, 0, j)),
                  pl.BlockSpec((1, 1, nb), lambda l, j: (l, 0, j))],
        out_specs=pl.BlockSpec((1, 8, nb), lambda l, j: (l, 0, j)),
        out_shape=jax.ShapeDtypeStruct((depth, 8, n), F32),
        compiler_params=_params("parallel", "parallel"),
        name="ada_mods",
    )(cc, w_ada, b_ada.reshape(depth, 1, n))


def _normed_input(x, mods, g):
    return (_rms(x, g) * (1.0 + mods[1:2]) + mods[0:1]).astype(BF16)


def _rope_rows(y, cos, sin):
    x1, x2 = y[:32], y[32:]
    return x1 * cos - x2 * sin, x1 * sin + x2 * cos


def _log_sigmoid(x):
    return jnp.minimum(x, 0.0) - jnp.log(1.0 + jnp.exp(-jnp.abs(x)))


def _nt(a, b):
    return lax.dot_general(a, b, (((1,), (1,)), ((), ())), preferred_element_type=F32)


def _max_sq_norm_row(parts):
    n2 = sum(jnp.sum(jnp.square(p.astype(BF16).astype(F32)), axis=0, keepdims=True) for p in parts)
    return jnp.broadcast_to(jnp.max(n2, axis=1, keepdims=True), (1, 128))


def _even_in_kernel(*refs, n_stream):
    (mods_ref, g_ref, wn_ref, wt_ref, wlr_ref, blr_ref, gq_ref, gk_ref, cos_ref, sin_ref,
     qk_ref, la_ref, gv_ref, gg_ref, aq_ref, ak_ref, av_ref, kn_ref) = refs[n_stream:]
    tile = pl.program_id(1)
    hs = [_normed_input(_stream_tile(refs[:n_stream], s, tile), mods_ref[s], g_ref[0:1]) for s in range(IN_PB)]
    ps = [jnp.dot(h, wn_ref[...], preferred_element_type=F32) for h in hs]
    pts = [_nt(wt_ref[...], h) for h in hs]
    cos, sin = cos_ref[...], sin_ref[...]
    gq, gk_gain = gq_ref[...], gk_ref[...]
    ones = jnp.ones((ONES_ROWS, TM), BF16)
    for s in range(IN_PB):
        p, pt_all = ps[s], pts[s]
        qk_ref[s, :, :GLA_QK] = p[:, :GLA_QK] * GLA_DK ** -0.5
        qk_ref[s, :, GLA_QK:] = p[:, GLA_QK:2 * GLA_QK]
        lr = p[:, 2 * GLA_QK:].astype(BF16)
        gk = jnp.dot(lr, wlr_ref[...], preferred_element_type=F32) + blr_ref[...]
        la_ref[s] = _log_sigmoid(gk) * (1.0 / GLA_NORMALIZER)
        gv_ref[s, 0] = pt_all[:GLA_V].astype(BF16)
        gg_ref[s, 0] = pt_all[GLA_V:2 * GLA_V]
        pt = pt_all[2 * GLA_V:]
        for hh in range(GQA_HEADS):
            blk = pt[hh * 64:(hh + 1) * 64]
            y = blk * lax.rsqrt(jnp.mean(blk * blk, axis=0, keepdims=True) + NORM_EPS) * gq
            o1, o2 = _rope_rows(y, cos, sin)
            aq_ref[s, 0, hh * 64:hh * 64 + 32, :] = (o1 * Q_SCALE).astype(BF16)
            aq_ref[s, 0, hh * 64 + 32:hh * 64 + 64, :] = (o2 * Q_SCALE).astype(BF16)
        kparts, knorms = [], []
        for hh in range(GQA_KV_HEADS):
            blk = pt[512 + hh * 64:512 + (hh + 1) * 64]
            y = blk * lax.rsqrt(jnp.mean(blk * blk, axis=0, keepdims=True) + NORM_EPS) * gk_gain
            o1, o2 = _rope_rows(y, cos, sin)
            kparts += [o1, o2]
            knorms.append(_max_sq_norm_row([o1, o2]))
        ak_ref[s] = jnp.concatenate(kparts, axis=0).astype(BF16).T
        kn_ref[s, 0] = jnp.concatenate(knorms + [jnp.zeros((8 - GQA_KV_HEADS, 128), F32)], axis=0)
        for hh in range(GQA_KV_HEADS):
            av_ref[s, 0, hh * GQA_VROWS:hh * GQA_VROWS + 64, :] = pt[640 + hh * 64:640 + (hh + 1) * 64].astype(BF16)
            av_ref[s, 0, hh * GQA_VROWS + 64:(hh + 1) * GQA_VROWS, :] = ones


def _odd_in_kernel(xs_ref, prev_ref, next_ref, mods_ref, g_ref, wn_ref, wt_ref, cw_ref, cb_ref, cos_ref, sin_ref,
                   v_ref, x1_ref, x2_ref, dq_ref, dk_ref, dv_ref, kn_ref, *, n_tiles):
    i = pl.program_id(1)
    hs = [_normed_input(xs_ref[s], mods_ref[s], g_ref[0:1]) for s in range(IN_PB)]
    h_ext = [jnp.concatenate([hs[s], _normed_input(prev_ref[s], mods_ref[s], g_ref[0:1]),
                              _normed_input(next_ref[s], mods_ref[s], g_ref[0:1])], axis=0) for s in range(IN_PB)]
    us = [jnp.dot(h, wn_ref[...], preferred_element_type=F32) for h in h_ext]
    pts = [_nt(wt_ref[...], h) for h in hs]
    cos, sin = cos_ref[...], sin_ref[...]
    ones = jnp.ones((ONES_ROWS, TM), BF16)
    nv = 2 * HEAD_DIM
    cw = cw_ref[...]
    row_id = lax.broadcasted_iota(jnp.int32, (TM, HY_COLS), 0)
    no_prev = (i == 0) | (i == 1)
    no_next = (i == 0) | (i == n_tiles - 1)
    for s in range(IN_PB):
        pt = pts[s]
        u = us[s][:TM]
        prev_row = jnp.where(no_prev, 0.0, us[s][TM + 7:TM + 8])
        next_row = jnp.where(no_next, 0.0, us[s][TM + 8:TM + 9])
        up = jnp.where(row_id == 0, prev_row, pltpu.roll(u, 1, axis=0))
        dn = jnp.where(row_id == TM - 1, next_row, pltpu.roll(u, TM - 1, axis=0))
        conv = up * cw[0:1] + u * cw[1:2] + dn * cw[2:3] + cb_ref[...]
        v_ref[s] = conv[:, :HY_CH]
        x1_ref[s] = conv[:, HY_CH:2 * HY_CH]
        x2_ref[s] = conv[:, 2 * HY_CH:]
        kparts, knorms = [], []
        for hh in range(2 * DIFF_HEADS):
            o1, o2 = _rope_rows(pt[hh * 64:(hh + 1) * 64], cos, sin)
            dq_ref[s, 0, hh * 64:hh * 64 + 32, :] = (o1 * Q_SCALE).astype(BF16)
            dq_ref[s, 0, hh * 64 + 32:hh * 64 + 64, :] = (o2 * Q_SCALE).astype(BF16)
            k1, k2 = _rope_rows(pt[DIFF_QK + hh * 64:DIFF_QK + (hh + 1) * 64], cos, sin)
            kparts += [k1, k2]
            knorms.append(_max_sq_norm_row([k1, k2]))
        dk_ref[s] = jnp.concatenate(kparts, axis=0).astype(BF16).T
        kn_ref[s, 0] = jnp.concatenate(knorms, axis=0)
        for hh in range(DIFF_HEADS):
            rows = pt[2 * DIFF_QK + hh * nv:2 * DIFF_QK + (hh + 1) * nv]
            dv_ref[s, 0, hh * DIFF_VROWS:hh * DIFF_VROWS + nv, :] = rows.astype(BF16)
            dv_ref[s, 0, hh * DIFF_VROWS + nv:(hh + 1) * DIFF_VROWS, :] = ones


def _pair_perm(n_heads):
    base = np.concatenate([np.arange(0, HEAD_DIM, 2), np.arange(1, HEAD_DIM, 2)])
    return np.concatenate([h * HEAD_DIM + base for h in range(n_heads)])


def _even_in(xs, mods_l, g0, w_in, w_lr, b_lr, qk_g, cos_t, sin_t):
    b, t, d = _stream_shape(xs)
    nt = t // TM
    x_specs, x_args = _stream_specs(xs, IN_PB)
    c_v = 2 * GLA_QK
    c_g = c_v + GLA_V
    c_lr0 = c_g + GLA_V
    c_lr = c_lr0 + 2 * GLA_RANK
    wn = jnp.concatenate([w_in[:, :c_v], w_in[:, c_lr0:c_lr], jnp.zeros((d, 128 - 2 * GLA_RANK), F32)],
                         axis=1).astype(BF16)
    cq = c_lr + GQA_HEADS * HEAD_DIM
    ck = cq + GQA_KV_HEADS * HEAD_DIM
    wq = w_in[:, c_lr:cq][:, _pair_perm(GQA_HEADS)]
    wk = w_in[:, cq:ck][:, _pair_perm(GQA_KV_HEADS)]
    wt = jnp.concatenate([w_in[:, c_v:c_lr0], wq, wk, w_in[:, ck:]], axis=1).T.astype(BF16)
    wlr = jnp.zeros((128, 2 * GLA_QK), F32)
    wlr = wlr.at[:GLA_RANK, :GLA_QK].set(w_lr[0]).at[GLA_RANK:2 * GLA_RANK, GLA_QK:].set(w_lr[1]).astype(BF16)
    blr = b_lr.reshape(1, 2 * GLA_QK)
    perm = _pair_perm(1)
    gq = qk_g[0][perm].reshape(HEAD_DIM, 1)
    gk = qk_g[1][perm].reshape(HEAD_DIM, 1)
    tok = lambda n: pl.BlockSpec((IN_PB, TM, n), lambda bb, i: (bb, i, 0))
    tr = lambda n: pl.BlockSpec((IN_PB, 1, n, TM), lambda bb, i: (bb, i, 0, 0))
    return pl.pallas_call(
        functools.partial(_even_in_kernel, n_stream=len(x_args)),
        grid=(b // IN_PB, nt),
        in_specs=x_specs + [
                  pl.BlockSpec((IN_PB, N_MOD, d), lambda bb, i: (_mod_block(bb, i, IN_PB), 0, 0)),
                  _const_spec((1, d)),
                  _const_spec(wn.shape), _const_spec(wt.shape), _const_spec(wlr.shape), _const_spec(blr.shape),
                  _const_spec(gq.shape), _const_spec(gk.shape),
                  pl.BlockSpec((32, TM), lambda bb, i: (0, i)),
                  pl.BlockSpec((32, TM), lambda bb, i: (0, i))],
        out_specs=[tok(2 * GLA_QK), tok(2 * GLA_QK), tr(GLA_V), tr(GLA_V), tr(512), tok(128),
                   tr(GQA_KV_HEADS * GQA_VROWS), pl.BlockSpec((IN_PB, 1, 8, 128), lambda bb, i: (bb, i, 0, 0))],
        out_shape=[jax.ShapeDtypeStruct((b, t, 2 * GLA_QK), F32),
                   jax.ShapeDtypeStruct((b, t, 2 * GLA_QK), F32),
                   jax.ShapeDtypeStruct((b, nt, GLA_V, TM), BF16),
                   jax.ShapeDtypeStruct((b, nt, GLA_V, TM), F32),
                   jax.ShapeDtypeStruct((b, nt, 512, TM), BF16),
                   jax.ShapeDtypeStruct((b, t, 128), BF16),
                   jax.ShapeDtypeStruct((b, nt, GQA_KV_HEADS * GQA_VROWS, TM), BF16),
                   jax.ShapeDtypeStruct((b, nt, 8, 128), F32)],
        compiler_params=_params("parallel", "parallel"),
        name="even_in",
    )(*x_args, mods_l, g0, wn, wt, wlr, blr, gq, gk, cos_t, sin_t)


def _odd_in(xs, mods_l, g0, w_in, conv_w, conv_b, cos_t, sin_t):
    b, t, d = xs.shape
    nt = t // TM
    r8 = TM // 8
    last_blk = t // 8 - 1
    wn = w_in[:, :HY_COLS].astype(BF16)
    perm = _pair_perm(2 * DIFF_HEADS)
    wq = w_in[:, HY_COLS:HY_COLS + DIFF_QK][:, perm]
    wk = w_in[:, HY_COLS + DIFF_QK:HY_COLS + 2 * DIFF_QK][:, perm]
    wt = jnp.concatenate([wq, wk, w_in[:, HY_COLS + 2 * DIFF_QK:]], axis=1).T.astype(BF16)
    tok = lambda n: pl.BlockSpec((IN_PB, TM, n), lambda bb, i: (bb, i, 0))
    tr = lambda n: pl.BlockSpec((IN_PB, 1, n, TM), lambda bb, i: (bb, i, 0, 0))
    hy = jax.ShapeDtypeStruct((b, t, HY_CH), F32)
    return pl.pallas_call(
        functools.partial(_odd_in_kernel, n_tiles=nt),
        grid=(b // IN_PB, nt),
        in_specs=[tok(d),
                  pl.BlockSpec((IN_PB, 8, d), lambda bb, i: (bb, jnp.maximum(i * r8 - 1, 0), 0)),
                  pl.BlockSpec((IN_PB, 8, d), lambda bb, i: (bb, jnp.minimum((i + 1) * r8, last_blk), 0)),
                  pl.BlockSpec((IN_PB, N_MOD, d), lambda bb, i: (_mod_block(bb, i, IN_PB), 0, 0)),
                  _const_spec((1, d)),
                  _const_spec(wn.shape), _const_spec(wt.shape),
                  _const_spec(conv_w.shape), _const_spec((1, HY_COLS)),
                  pl.BlockSpec((32, TM), lambda bb, i: (0, i)),
                  pl.BlockSpec((32, TM), lambda bb, i: (0, i))],
        out_specs=[tok(HY_CH), tok(HY_CH), tok(HY_CH), tr(DIFF_QK), tok(DIFF_QK), tr(DIFF_HEADS * DIFF_VROWS),
                   pl.BlockSpec((IN_PB, 1, 8, 128), lambda bb, i: (bb, i, 0, 0))],
        out_shape=[hy, hy, hy,
                   jax.ShapeDtypeStruct((b, nt, DIFF_QK, TM), BF16),
                   jax.ShapeDtypeStruct((b, t, DIFF_QK), BF16),
                   jax.ShapeDtypeStruct((b, nt, DIFF_HEADS * DIFF_VROWS, TM), BF16),
                   jax.ShapeDtypeStruct((b, nt, 8, 128), F32)],
        compiler_params=_params("parallel", "parallel"),
        name="odd_in",
    )(xs, xs, xs, mods_l, g0, wn, wt, conv_w, conv_b.reshape(1, HY_COLS), cos_t, sin_t)


def _attn_core(q_ref, k_ref, v_ref, kmax_ref, scratch, *, n_kv, k_col, k_row, q_slot, v_row, v_rows, scores_between,
               vpu_denominator):
    qpad_ref, bound_ref, m_ref, acc_ref, s0_ref, s1_ref, c0_ref, c1_ref, p0_ref, p1_ref = scratch
    s_refs, c_refs, p_refs = (s0_ref, s1_ref), (c0_ref, c1_ref), (p0_ref, p1_ref)
    i = pl.program_id(1)
    zeros = jnp.zeros((HEAD_DIM, TM), BF16)
    bound_max = jnp.zeros((1, TM), F32)
    for hh in range(N_SCORE_HEADS):
        qh = q_ref[0, 0, hh * HEAD_DIM:(hh + 1) * HEAD_DIM, :]
        qpad_ref[hh] = jnp.concatenate([qh, zeros] if q_slot(hh) == 0 else [zeros, qh], axis=0)
        qf = qh.astype(F32)
        q_norm2 = jnp.sum(qf * qf, axis=0, keepdims=True)
        k_norm2 = kmax_ref[0, k_row(hh):k_row(hh) + 1, 0:1]
        bound = jnp.sqrt(q_norm2 * k_norm2) * SCORE_BOUND_SLACK
        bound_ref[hh] = bound
        bound_max = jnp.maximum(bound_max, bound)
    acc_ref[...] = jnp.zeros(acc_ref.shape, F32)
    n = jnp.where(i == 0, 1, n_kv)
    fixed_reference = jnp.max(bound_max) <= SCORE_BOUND_LIMIT

    def produce(j, hh, slot):
        kc = k_ref[0, pl.ds(pl.multiple_of(j * TM, TM), TM), k_col(hh):k_col(hh) + 128]
        s = jnp.dot(kc, qpad_ref[hh], preferred_element_type=F32)
        p = jnp.exp2(s - bound_ref[hh])
        if vpu_denominator:
            m_ref[hh] = m_ref[hh] + jnp.sum(p, axis=0, keepdims=True)
        p_refs[slot][hh] = p.astype(BF16)

    n_val = v_rows - ONES_ROWS if vpu_denominator else v_rows

    def reduce(j, hh, slot):
        vc = v_ref[0, j, v_row(hh):v_row(hh) + n_val, :]
        acc_ref[hh, :n_val] = acc_ref[hh, :n_val] + jnp.dot(vc, p_refs[slot][hh], preferred_element_type=F32)

    unroll = math.gcd(n_kv - 1, FAST_UNROLL_MAX)

    @pl.when(fixed_reference)
    def _():
        m_ref[...] = jnp.zeros(m_ref.shape, F32)
        for hh in range(N_SCORE_HEADS):
            produce(0, hh, 0)

        def fast_step(j, slot):
            for hh in range(N_SCORE_HEADS):
                produce(j + 1, hh, 1 - slot)
                reduce(j, hh, slot)

        def fast_body(jj, carry):
            for u in range(unroll):
                fast_step(unroll * jj + u, u % 2)
            return carry

        lax.fori_loop(0, (n - 1) // unroll, fast_body, 0)
        for hh in range(N_SCORE_HEADS):
            reduce(n - 1, hh, 0)
            if vpu_denominator:
                acc_ref[hh, n_val:, :] = jnp.broadcast_to(m_ref[hh], (ONES_ROWS, TM))

    @pl.when(jnp.logical_not(fixed_reference))
    def _():
        _attn_running_max(k_ref, v_ref, qpad_ref, m_ref, acc_ref, s_refs, c_refs, n=n, k_col=k_col, v_row=v_row,
                          v_rows=v_rows, scores_between=scores_between)


def _attn_running_max(k_ref, v_ref, qpad_ref, m_ref, acc_ref, s_refs, c_refs, *, n, k_col, v_row, v_rows,
                      scores_between):
    m_ref[...] = jnp.full(m_ref.shape, -jnp.inf, F32)

    def scores(j, hh, slot):
        kc = k_ref[0, pl.ds(pl.multiple_of(j * TM, TM), TM), k_col(hh):k_col(hh) + 128]
        s = jnp.dot(kc, qpad_ref[hh], preferred_element_type=F32)
        s_refs[slot][hh] = s
        c_refs[slot][hh] = jnp.max(s, axis=0, keepdims=True)

    def probs(hh, slot):
        m_old = m_ref[hh]
        m_new = jnp.maximum(m_old, c_refs[slot][hh])
        return m_old, m_new, jnp.exp2(s_refs[slot][hh] - m_new).astype(BF16)

    def accumulate(j, hh, state):
        m_old, m_new, p = state
        vc = v_ref[0, j, v_row(hh):v_row(hh) + v_rows, :]
        alpha = jnp.exp2(m_old - m_new)
        acc_ref[hh] = acc_ref[hh] * alpha + jnp.dot(vc, p, preferred_element_type=F32)
        m_ref[hh] = m_new

    for hh in range(N_SCORE_HEADS):
        scores(0, hh, 0)

    def step(j, slot):
        for hh in range(N_SCORE_HEADS):
            state = probs(hh, slot)
            if scores_between:
                scores(j + 1, hh, 1 - slot)
            accumulate(j, hh, state)
            if not scores_between:
                scores(j + 1, hh, 1 - slot)

    def body(jj, carry):
        step(2 * jj, 0)
        step(2 * jj + 1, 1)
        return carry

    lax.fori_loop(0, (n - 1) // 2, body, 0)
    for hh in range(N_SCORE_HEADS):
        accumulate(n - 1, hh, probs(hh, 0))


def _attn_scratch(v_rows):
    stat = pltpu.VMEM((N_SCORE_HEADS, 1, TM), F32)
    return [pltpu.VMEM((N_SCORE_HEADS, 128, TM), BF16),
            stat, stat,
            pltpu.VMEM((N_SCORE_HEADS, v_rows, TM), F32),
            pltpu.VMEM((N_SCORE_HEADS, TM, TM), F32), pltpu.VMEM((N_SCORE_HEADS, TM, TM), F32),
            stat, stat,
            pltpu.VMEM((N_SCORE_HEADS, TM, TM), BF16), pltpu.VMEM((N_SCORE_HEADS, TM, TM), BF16)]


def _gqa_kernel(q_ref, k_ref, v_ref, kmax_ref, o_ref, *scratch, n_kv):
    group = GQA_HEADS // GQA_KV_HEADS
    _attn_core(q_ref, k_ref, v_ref, kmax_ref, scratch, n_kv=n_kv,
               k_col=lambda h: 0, k_row=lambda h: h // group, q_slot=lambda h: h // group,
               v_row=lambda h: (h // group) * GQA_VROWS, v_rows=GQA_VROWS, scores_between=True,
               vpu_denominator=True)
    acc_ref = scratch[3]
    for hh in range(GQA_HEADS):
        a = acc_ref[hh]
        o_ref[0, 0, hh * HEAD_DIM:(hh + 1) * HEAD_DIM, :] = (a[:HEAD_DIM] / a[HEAD_DIM:HEAD_DIM + 1]).astype(o_ref.dtype)


def _gqa_attention(aq, ak, av, kmax):
    b, nt, nq, _ = aq.shape
    t = nt * TM
    assert nt % 2 == 1 and nt > 1
    return pl.pallas_call(
        functools.partial(_gqa_kernel, n_kv=nt),
        grid=(b, nt),
        in_specs=[pl.BlockSpec((1, 1, nq, TM), lambda bb, i: (bb, i, 0, 0)),
                  pl.BlockSpec((1, t, 128), lambda bb, i: (bb, 0, 0)),
                  pl.BlockSpec((1, nt, GQA_KV_HEADS * GQA_VROWS, TM), lambda bb, i: (bb, 0, 0, 0)),
                  pl.BlockSpec((1, 8, 128), lambda bb, i: (bb, 0, 0))],
        out_specs=pl.BlockSpec((1, 1, nq, TM), lambda bb, i: (bb, i, 0, 0)),
        out_shape=jax.ShapeDtypeStruct((b, nt, nq, TM), BF16),
        scratch_shapes=_attn_scratch(GQA_VROWS),
        compiler_params=_params("parallel", "arbitrary"),
        name="gqa_attn",
    )(aq, ak, av, kmax)


def _diff_kernel(q_ref, k_ref, v_ref, kmax_ref, lam_ref, gain_ref, o_ref, *scratch, n_kv, lam_init):
    _attn_core(q_ref, k_ref, v_ref, kmax_ref, scratch, n_kv=n_kv,
               k_col=lambda hc: (hc // 2) * 128, k_row=lambda hc: hc, q_slot=lambda hc: hc % 2,
               v_row=lambda hc: (hc // 2) * DIFF_VROWS, v_rows=DIFF_VROWS, scores_between=False,
               vpu_denominator=True)
    acc_ref = scratch[3]
    lp = lam_ref[...]
    lam = (jnp.exp(jnp.sum(lp[0:1] * lp[1:2], axis=1, keepdims=True))
           - jnp.exp(jnp.sum(lp[2:3] * lp[3:4], axis=1, keepdims=True)) + lam_init)
    gain = gain_ref[...]
    nv = 2 * HEAD_DIM
    for hh in range(DIFF_HEADS):
        a0, a1 = acc_ref[2 * hh], acc_ref[2 * hh + 1]
        o = a0[:nv] / a0[nv:nv + 1] - lam * (a1[:nv] / a1[nv:nv + 1])
        y = o * lax.rsqrt(jnp.mean(o * o, axis=0, keepdims=True) + NORM_EPS) * gain
        o_ref[0, 0, hh * nv:(hh + 1) * nv, :] = (y * (1.0 - lam_init)).astype(o_ref.dtype)


def _diff_attention(dq, dk, dv, kmax, lam_p, gain, layer_idx):
    b, nt, nq, _ = dq.shape
    t = nt * TM
    assert nt % 2 == 1 and nt > 1
    lam_init = 0.8 - 0.6 * math.exp(-0.3 * layer_idx)
    return pl.pallas_call(
        functools.partial(_diff_kernel, n_kv=nt, lam_init=lam_init),
        grid=(b, nt),
        in_specs=[pl.BlockSpec((1, 1, nq, TM), lambda bb, i: (bb, i, 0, 0)),
                  pl.BlockSpec((1, t, DIFF_QK), lambda bb, i: (bb, 0, 0)),
                  pl.BlockSpec((1, nt, DIFF_HEADS * DIFF_VROWS, TM), lambda bb, i: (bb, 0, 0, 0)),
                  pl.BlockSpec((1, 8, 128), lambda bb, i: (bb, 0, 0)),
                  _const_spec(lam_p.shape),
                  _const_spec((2 * HEAD_DIM, 1))],
        out_specs=pl.BlockSpec((1, 1, DIFF_V, TM), lambda bb, i: (bb, i, 0, 0)),
        out_shape=jax.ShapeDtypeStruct((b, nt, DIFF_V, TM), BF16),
        scratch_shapes=_attn_scratch(DIFF_VROWS),
        compiler_params=_params("parallel", "arbitrary"),
        name="diff_attn",
    )(dq, dk, dv, kmax, lam_p, gain.reshape(2 * HEAD_DIM, 1))


def _split_bf16(x):
    hi = x.astype(BF16)
    return hi, (x - hi.astype(F32)).astype(BF16)


def _gla_kernel(qk_ref, la_ref, vt_ref, o_ref, st_ref, *, backward):
    @pl.when(pl.program_id(1) == 0)
    def _():
        st_ref[...] = jnp.zeros(st_ref.shape, F32)

    n_chunk = TM // GLA_CHUNK
    r = lax.broadcasted_iota(jnp.int32, (TM, TM), 0)
    c = lax.broadcasted_iota(jnp.int32, (TM, TM), 1)
    same = (r // GLA_CHUNK) == (c // GLA_CHUNK)
    causal = (c >= r) if backward else (c <= r)
    tri = jnp.where(same & causal, 1.0, 0.0).astype(BF16)
    blk = jnp.where(same, 1.0, 0.0).astype(BF16)
    lane = lax.broadcasted_iota(jnp.int32, (TM, 128), 1)
    row = lax.broadcasted_iota(jnp.int32, (TM, 128), 0)
    keep = same & ((r >= c) if backward else (r <= c))
    order = range(n_chunk - 1, -1, -1) if backward else range(n_chunk)
    units = [(s, h) for s in range(GLA_PB) for h in range(GLA_HEADS)]
    pair = lambda h: slice((h // 2) * 128, (h // 2 + 1) * 128)
    q_in, k_in, k_end, dec = [], [], [], []
    for s in range(GLA_PB):
        la_hi, la_lo = _split_bf16(la_ref[s])
        cum = jnp.dot(tri, la_hi, preferred_element_type=F32) + jnp.dot(tri, la_lo, preferred_element_type=F32)
        tot = jnp.dot(blk, la_hi, preferred_element_type=F32) + jnp.dot(blk, la_lo, preferred_element_type=F32)
        q = qk_ref[s, :, :GLA_QK]
        k = qk_ref[s, :, GLA_QK:]
        q_in.append((q * jnp.exp(cum)).astype(BF16))
        k_in.append(k * jnp.exp(-cum))
        k_end.append(k * jnp.exp(tot - cum))
        dec.append(jnp.exp(tot))
    head_lanes = {h: (lane // GLA_DK) == (h % 2) for h in range(GLA_HEADS)}
    q_pair = {(s, h): q_in[s][:, pair(h)] for s, h in units}
    k_in_h = {(s, h): jnp.where(head_lanes[h], k_in[s][:, pair(h)], 0.0).astype(BF16) for s, h in units}
    k_end_h = {(s, h): jnp.where(head_lanes[h], k_end[s][:, pair(h)], 0.0) for s, h in units}
    vts = {(s, h): vt_ref[s, 0, h * GLA_DV:(h + 1) * GLA_DV, :] for s, h in units}
    a_ts = {u: jnp.where(keep, _nt(k_in_h[u], q_pair[u]), 0.0).astype(BF16) for u in units}
    o_ts = {u: jnp.dot(vts[u], a_ts[u], preferred_element_type=F32) for u in units}
    s_ts = {(s, h): st_ref[s, h] for s, h in units}
    for ci in order:
        in_chunk = (row // GLA_CHUNK) == ci
        for u in units:
            s, h = u
            q_c = jnp.where(in_chunk, q_pair[u], jnp.zeros_like(q_pair[u]))
            o_ts[u] = o_ts[u] + _nt(s_ts[u].astype(BF16), q_c)
            k_c = jnp.where(in_chunk, k_end_h[u], 0.0).astype(BF16)
            s_ts[u] = (s_ts[u] * dec[s][ci * GLA_CHUNK:ci * GLA_CHUNK + 1, pair(h)]
                       + jnp.dot(vts[u], k_c, preferred_element_type=F32))
    for s, h in units:
        st_ref[s, h] = s_ts[(s, h)]
        o_ref[s, 0, h * GLA_DV:(h + 1) * GLA_DV, :] = o_ts[(s, h)]


def _gla_direction(qk, la, vt, backward):
    b, t, _ = qk.shape
    nt = t // TM
    d = 1 if backward else 0
    tile = (lambda i: jnp.where(i == 0, 0, nt - i)) if backward else (lambda i: i)
    return pl.pallas_call(
        functools.partial(_gla_kernel, backward=backward),
        grid=(b // GLA_PB, nt),
        in_specs=[pl.BlockSpec((GLA_PB, TM, 2 * GLA_QK), lambda bb, i: (bb, tile(i), 0)),
                  pl.BlockSpec((GLA_PB, TM, GLA_QK), lambda bb, i: (bb, tile(i), d)),
                  pl.BlockSpec((GLA_PB, 1, GLA_V, TM), lambda bb, i: (bb, tile(i), 0, 0))],
        out_specs=pl.BlockSpec((GLA_PB, 1, GLA_V, TM), lambda bb, i: (bb, tile(i), 0, 0)),
        out_shape=jax.ShapeDtypeStruct((b, nt, GLA_V, TM), F32),
        scratch_shapes=[pltpu.VMEM((GLA_PB, GLA_HEADS, GLA_DV, 128), F32)],
        compiler_params=_params("parallel", "arbitrary"),
        name="gla_bwd" if backward else "gla_fwd",
    )(qk, la, vt)


def _post_phases(xs, mods, g, yas, ybs, woa_ref, wob_ref, w1_ref, w2_ref, ff_chunk):
    n = len(xs)
    ys = [jnp.dot(yas[s], woa_ref[...], preferred_element_type=F32)
          + jnp.dot(ybs[s].T, wob_ref[...], preferred_element_type=F32) for s in range(n)]
    x1s = [xs[s] + mods[s][2:3] * _rms(ys[s], g[1:2]) for s in range(n)]
    hs = [(_rms(x1s[s], g[2:3]) * (1.0 + mods[s][4:5]) + mods[s][3:4]).astype(BF16) for s in range(n)]
    d_ff = w1_ref.shape[1]
    accs = [jnp.zeros(xs[0].shape, F32) for _ in range(n)]
    for c in range(d_ff // ff_chunk):
        cols = slice(c * ff_chunk, (c + 1) * ff_chunk)
        for s in range(n):
            u = jnp.dot(hs[s], w1_ref[:, cols], preferred_element_type=F32)
            u = jnp.square(jnp.maximum(u, 0.0)).astype(BF16)
            accs[s] = accs[s] + jnp.dot(u, w2_ref[cols, :], preferred_element_type=F32)
    return [x1s[s] + mods[s][5:6] * _rms(accs[s], g[3:4]) for s in range(n)]


def _post_gla_kernel(*refs, ff_chunk, n_stream, off):
    of_ref, ob_ref, gg_ref, gain_ref, yb_ref, mods_ref, g_ref, woa_ref, wob_ref, w1_ref, w2_ref, o_ref = refs[n_stream:]
    tile = pl.program_id(1) + off
    gain = gain_ref[...]
    yas = []
    for s in range(POST_PB):
        parts = []
        for hh in range(GLA_HEADS):
            rows = slice(hh * GLA_DV, (hh + 1) * GLA_DV)
            o = of_ref[s, 0, rows, :] + ob_ref[s, 0, rows, :]
            gt = gg_ref[s, 0, rows, :]
            y = o * lax.rsqrt(jnp.mean(o * o, axis=0, keepdims=True) + NORM_EPS) * gain
            parts.append(y * (gt / (1.0 + jnp.exp(-gt))))
        yas.append(jnp.concatenate(parts, axis=0).astype(BF16).T)
    outs = _post_phases([_stream_tile(refs[:n_stream], s, tile) for s in range(POST_PB)],
                        [mods_ref[s] for s in range(POST_PB)], g_ref[...], yas,
                        [yb_ref[s, 0] for s in range(POST_PB)], woa_ref, wob_ref, w1_ref, w2_ref, ff_chunk)
    for s in range(POST_PB):
        o_ref[s] = outs[s]


def _post_kernel(*refs, ff_chunk, n_stream, off):
    ya_ref, yb_ref, mods_ref, g_ref, woa_ref, wob_ref, w1_ref, w2_ref, o_ref = refs[n_stream:]
    tile = pl.program_id(1) + off
    outs = _post_phases([_stream_tile(refs[:n_stream], s, tile) for s in range(POST_PB)],
                        [mods_ref[s] for s in range(POST_PB)], g_ref[...],
                        [ya_ref[s].astype(BF16) for s in range(POST_PB)], [yb_ref[s, 0] for s in range(POST_PB)],
                        woa_ref, wob_ref, w1_ref, w2_ref, ff_chunk)
    for s in range(POST_PB):
        o_ref[s] = outs[s]


def _post(xs, ya, yb_t, mods_l, g, w_out, w1, w2, skip_ctx):
    b, t, d = _stream_shape(xs)
    nt = t // TM
    pb = POST_PB
    half = w_out.shape[0] // 2
    woa = w_out[:half].astype(BF16)
    wob = w_out[half:].astype(BF16)
    w1 = w1.astype(BF16)
    w2 = w2.astype(BF16)
    off = 1 if skip_ctx else 0
    x_specs, x_args = _stream_specs(xs, pb, off)
    tok = lambda n: pl.BlockSpec((pb, TM, n), lambda bb, i: (bb, i + off, 0))
    tr = lambda n: pl.BlockSpec((pb, 1, n, TM), lambda bb, i: (bb, i + off, 0, 0))
    if isinstance(ya, tuple):
        o_f, o_b, gg, gain = ya
        body, name = _post_gla_kernel, "post_gla"
        a_specs = [tr(half), tr(half), tr(half), _const_spec((GLA_DV, 1))]
        a_args = [o_f, o_b, gg, gain.reshape(GLA_DV, 1)]
    else:
        body, name = _post_kernel, "post"
        a_specs, a_args = [tok(half)], [ya]
    return pl.pallas_call(
        functools.partial(body, ff_chunk=1024, n_stream=len(x_args), off=off),
        grid=(b // pb, nt - off),
        in_specs=x_specs + a_specs + [
                  tr(half),
                  pl.BlockSpec((pb, N_MOD, d), lambda bb, i: (_mod_block(bb, i + off, pb), 0, 0)),
                  _const_spec(g.shape),
                  _const_spec(woa.shape), _const_spec(wob.shape), _const_spec(w1.shape), _const_spec(w2.shape)],
        out_specs=pl.BlockSpec((pb, TM, d), lambda bb, i: (bb, i, 0)),
        out_shape=jax.ShapeDtypeStruct((b, t - off * TM, d), F32),
        compiler_params=_params("parallel", "parallel"),
        name=name,
    )(*x_args, *a_args, yb_t, mods_l, g, woa, wob, w1, w2)


def _dot3(a, b):
    ah, al = _split_bf16(a)
    bh, bl = _split_bf16(b)
    d = lambda x, y: jnp.dot(x, y, preferred_element_type=F32)
    return d(ah, bh) + (d(al, bh) + d(ah, bl))


def _filter_kernel(w1t_ref, w1c_ref, w1s_ref, b1_ref, w2_ref, b2_ref, w3_ref, delta_ref, o_ref, *, seq, rb):
    tap = pl.program_id(0) * rb + lax.broadcasted_iota(jnp.int32, (rb, 1), 0)
    d = tap.astype(F32)
    t = d / max(seq - 1, 1)
    w = (2 * math.pi / seq) * d
    band = lax.broadcasted_iota(jnp.int32, (1, HY_BANDS), 1).astype(F32)
    bands = 1e-4 + band * ((HY_BANDS - 1 - 1e-4) / (HY_BANDS - 1))
    ang = w * bands
    pre = t * w1t_ref[...] + _dot3(jnp.cos(ang), w1c_ref[...]) - _dot3(jnp.sin(ang), w1s_ref[...]) + b1_ref[...]
    hid = jnp.sin(pre)
    hid = jnp.sin(_dot3(hid, w2_ref[...]) + b2_ref[...])
    decay = jnp.exp(-t * delta_ref[...])
    decay_bwd = jnp.where(tap == 0, 0.0, decay)
    for o in range(HY_ORDER):
        o_ref[2 * o] = (_dot3(hid, w3_ref[2 * o]) * decay).astype(o_ref.dtype)
        o_ref[2 * o + 1] = (_dot3(hid, w3_ref[2 * o + 1]) * decay_bwd).astype(o_ref.dtype)


def _hyena_kernels(seq, f_w1, f_b1, f_w2, f_b2, f_w3):
    rb = min(512, seq)
    ffn = f_w2.shape[0]
    w3 = f_w3.reshape(ffn, 2 * HY_ORDER, HY_CH).transpose(1, 0, 2)
    deltas = jnp.abs(jnp.linspace(math.log(HY_TARGET) / HY_FAST_PCT, math.log(HY_TARGET) / HY_SLOW_PCT,
                                  HY_CH, dtype=F32)).reshape(1, HY_CH)
    return pl.pallas_call(
        functools.partial(_filter_kernel, seq=seq, rb=rb),
        grid=(seq // rb,),
        in_specs=[_const_spec((1, ffn)), _const_spec((HY_BANDS, ffn)), _const_spec((HY_BANDS, ffn)),
                  _const_spec((1, ffn)), _const_spec((ffn, ffn)), _const_spec((1, ffn)),
                  _const_spec(w3.shape), _const_spec((1, HY_CH))],
        out_specs=pl.BlockSpec((2 * HY_ORDER, rb, HY_CH), lambda j: (0, j, 0)),
        out_shape=jax.ShapeDtypeStruct((2 * HY_ORDER, seq, HY_CH), BF16),
        compiler_params=_params("parallel"),
        name="hy_filter",
    )(f_w1[0:1], f_w1[1:1 + HY_BANDS], f_w1[1 + HY_BANDS:], f_b1.reshape(1, ffn), f_w2, f_b2.reshape(1, ffn), w3,
      deltas)


def _dft_tables(seq):
    n = 2 * seq
    r = math.isqrt(n)
    assert r * r == n
    r1 = (r // 2 + 1 + 7) // 8 * 8
    idx = jnp.arange(r, dtype=jnp.int32)
    phi = (2 * math.pi / r) * ((idx[:r1, None] * idx[None, :]) % r).astype(F32)
    f1 = jnp.concatenate([jnp.cos(phi), -jnp.sin(phi)], axis=0).astype(BF16)
    m = (idx[:r1, None, None] + r * idx[None, :, None]) * idx[None, None, :]
    th = (2 * math.pi / n) * (m % n).astype(F32)
    c, s = jnp.cos(th), jnp.sin(th)
    blockm = lambda re, im: jnp.concatenate([jnp.concatenate([re, -im], axis=-1),
                                             jnp.concatenate([im, re], axis=-1)], axis=-2)
    g = blockm(c, -s).astype(BF16)
    ginv = blockm(jnp.swapaxes(c, 1, 2), jnp.swapaxes(s, 1, 2)).astype(BF16)
    k1 = idx[:r1]
    weight = jnp.where((k1 == 0) | (k1 == r // 2), 1.0, jnp.where(k1 < r // 2, 2.0, 0.0)) / n
    phi_t = phi.T[:r // 2]
    finv = jnp.concatenate([jnp.cos(phi_t) * weight, -jnp.sin(phi_t) * weight], axis=1).astype(BF16)
    return r, f1, g, ginv, finv


DFT_BB = 32


def _dft1_kernel(f_ref, x_ref, o_ref, *, a0):
    f = f_ref[...]
    r = o_ref.shape[2]
    x = pltpu.einshape("abc->bac", x_ref[0, a0:].astype(BF16))
    ys = [jnp.dot(f, x[j], preferred_element_type=F32) for j in range(DFT_BB)]
    y = pltpu.einshape("jkc->kjc", jnp.stack([v.astype(o_ref.dtype) for v in ys], axis=0))
    o_ref[0, 0] = y[:r]
    o_ref[0, 1] = y[r:]


def _dft_stage1(f, x4, a0=0):
    b, rows, r, c = x4.shape
    r1 = f.shape[0] // 2
    return pl.pallas_call(
        functools.partial(_dft1_kernel, a0=a0),
        grid=(b, r // DFT_BB),
        in_specs=[_const_spec(f.shape), pl.BlockSpec((1, rows, DFT_BB, c), lambda bi, j: (bi, 0, j, 0))],
        out_specs=pl.BlockSpec((1, 2, r1, DFT_BB, c), lambda bi, j: (bi, 0, 0, j, 0)),
        out_shape=jax.ShapeDtypeStruct((b, 2, r1, r, c), BF16),
        compiler_params=_params("parallel", "parallel"),
        name="hy_dft1",
    )(f, x4)


def _fspec_kernel(af_ref, ab_ref, g_ref, o_ref, *, kb, r):
    for kk in range(kb):
        xf = jnp.dot(g_ref[kk], af_ref[0, :, kk].reshape(2 * r, af_ref.shape[-1]), preferred_element_type=F32)
        xb = jnp.dot(g_ref[kk], ab_ref[0, :, kk].reshape(2 * r, ab_ref.shape[-1]), preferred_element_type=F32)
        o_ref[0, kk, :r] = xf[:r] + xb[:r]
        o_ref[0, kk, r:] = xf[r:] - xb[r:]


def _filter_spectrum(a, g, kb):
    o2, _, r1, r, c = a.shape
    blk = lambda d: pl.BlockSpec((1, 2, kb, r, c), lambda oo, j: (2 * oo + d, 0, j, 0, 0))
    return pl.pallas_call(
        functools.partial(_fspec_kernel, kb=kb, r=r),
        grid=(o2 // 2, r1 // kb),
        in_specs=[blk(0), blk(1), pl.BlockSpec((kb, 2 * r, 2 * r), lambda oo, j: (j, 0, 0))],
        out_specs=pl.BlockSpec((1, kb, 2 * r, c), lambda oo, j: (oo, j, 0, 0)),
        out_shape=jax.ShapeDtypeStruct((o2 // 2, r1, 2 * r, c), F32),
        compiler_params=_params("parallel", "parallel"),
        name="hy_fspec",
    )(a, a, g)


def _dft2_kernel(a_ref, g_ref, gi_ref, k_ref, o_ref, *, kb, r):
    xs = [jnp.dot(g_ref[kk], a_ref[0, :, kk].reshape(2 * r, a_ref.shape[-1]), preferred_element_type=F32)
          for kk in range(kb)]
    ys = []
    for kk in range(kb):
        x, ks = xs[kk], k_ref[0, kk]
        xr, xi, kr, ki = x[:r], x[r:], ks[:r], ks[r:]
        ys.append(jnp.concatenate([xr * kr - xi * ki, xr * ki + xi * kr], axis=0).astype(BF16))
    bqs = [jnp.dot(gi_ref[kk], ys[kk], preferred_element_type=F32) for kk in range(kb)]
    for kk in range(kb):
        o_ref[0, 0, kk] = bqs[kk][:r].astype(o_ref.dtype)
        o_ref[0, 1, kk] = bqs[kk][r:].astype(o_ref.dtype)


def _dft_stage2(a, g, ginv, kspec, order, kb):
    b, _, r1, r, c = a.shape
    blk = pl.BlockSpec((1, 2, kb, r, c), lambda j, bb: (bb, 0, j, 0, 0))
    tab = pl.BlockSpec((kb, 2 * r, 2 * r), lambda j, bb: (j, 0, 0))
    return pl.pallas_call(
        functools.partial(_dft2_kernel, kb=kb, r=r),
        grid=(r1 // kb, b),
        in_specs=[blk, tab, tab, pl.BlockSpec((1, kb, 2 * r, c), lambda j, bb: (order, j, 0, 0))],
        out_specs=blk,
        out_shape=jax.ShapeDtypeStruct(a.shape, BF16),
        compiler_params=_params("parallel", "arbitrary"),
        name="hy_dft2",
    )(a, g, ginv, kspec)


def _dft3_kernel(f_ref, b_ref, gate_ref, z_ref, bias_ref, head_ref, o_ref, *, a0):
    f = f_ref[...]
    bq = jnp.concatenate([b_ref[0, 0], b_ref[0, 1]], axis=0)
    bq = pltpu.einshape("kjc->jkc", bq)
    ys = [jnp.dot(f, bq[j], preferred_element_type=F32) for j in range(DFT_BB)]
    y = pltpu.einshape("jac->ajc", jnp.stack(ys, axis=0))
    o_ref[0, :a0] = head_ref[0]
    o_ref[0, a0:] = gate_ref[0, a0:] * (y + z_ref[0, a0:] * bias_ref[...])


def _dft_stage3(finv, bq5, gate4, z4, bias, head4):
    b, _, r1, r, c = bq5.shape
    rows = z4.shape[1]
    a0 = head4.shape[1]
    io = pl.BlockSpec((1, rows, DFT_BB, c), lambda bi, j: (bi, 0, j, 0))
    return pl.pallas_call(
        functools.partial(_dft3_kernel, a0=a0),
        grid=(b, r // DFT_BB),
        in_specs=[_const_spec(finv.shape), pl.BlockSpec((1, 2, r1, DFT_BB, c), lambda bi, j: (bi, 0, 0, j, 0)), io, io,
                  _const_spec((1, 1, c)), pl.BlockSpec((1, a0, DFT_BB, c), lambda bi, j: (bi, 0, j, 0))],
        out_specs=io,
        out_shape=jax.ShapeDtypeStruct((b, rows, r, c), F32),
        compiler_params=_params("parallel", "parallel"),
        name="hy_dft3",
    )(finv, bq5, gate4, z4, bias, head4)


def _long_conv_gated(v, gates, kern, hy_bias, ctx_out, lc):
    b, t, c = v.shape
    r, f1, g, ginv, finv = _dft_tables(t - lc)
    assert lc % r == 0
    a0 = lc // r
    kb = 8
    kspec = _filter_spectrum(_dft_stage1(f1[:, :r // 2], kern.reshape(2 * HY_ORDER, r // 2, r, c)), g, kb)
    as_blocks = lambda x: x.reshape(b, t // r, r, c)
    z = as_blocks(v)
    heads = [jnp.zeros((b, a0, r, c), F32), ctx_out.reshape(b, a0, r, c)]
    for o in range(HY_ORDER):
        a = _dft_stage1(f1[:, :r // 2], z, a0)
        bq = _dft_stage2(a, g, ginv, kspec, o, kb)
        z = _dft_stage3(finv, bq, as_blocks(gates[o]), z, hy_bias[o].reshape(1, 1, c), heads[o])
    return z.reshape(b, t, c)


def _ctx_conv_kernel(v_ref, x1_ref, x2_ref, k_ref, ff_ref, fi_ref, bias_ref, o_ref):
    n = ff_ref.shape[1]
    z = v_ref[0]
    for o, gate_ref in enumerate((x1_ref, x2_ref)):
        kf = jnp.dot(ff_ref[:, :TM], k_ref[2 * o], preferred_element_type=F32)
        kb = jnp.dot(ff_ref[:, :TM], k_ref[2 * o + 1], preferred_element_type=F32)
        x = jnp.dot(ff_ref[:, :TM], z.astype(BF16), preferred_element_type=F32)
        xr, xi, kr, ki = x[:n], x[n:], kf[:n] + kb[:n], kf[n:] - kb[n:]
        y = jnp.concatenate([xr * kr - xi * ki, xr * ki + xi * kr], axis=0).astype(BF16)
        conv = jnp.dot(fi_ref[...], y, preferred_element_type=F32)
        z = gate_ref[0] * (conv + z * bias_ref[o:o + 1])
    o_ref[0] = z


def _ctx_conv_gated(v, x1, x2, kern, hy_bias):
    b, _, c = v.shape
    seq = TM
    n = 2 * seq
    idx = jnp.arange(n, dtype=jnp.int32)
    th = (2 * math.pi / n) * ((idx[:, None] * idx[None, :]) % n).astype(F32)
    ff = jnp.concatenate([jnp.cos(th), -jnp.sin(th)], axis=0).astype(BF16)
    fi = (jnp.concatenate([jnp.cos(th), -jnp.sin(th)], axis=1)[:seq] / n).astype(BF16)
    io = pl.BlockSpec((1, seq, c), lambda bb: (bb, 0, 0))
    return pl.pallas_call(
        _ctx_conv_kernel,
        grid=(b,),
        in_specs=[io, io, io, _const_spec(kern.shape), _const_spec(ff.shape), _const_spec(fi.shape),
                  _const_spec(hy_bias.shape)],
        out_specs=io,
        out_shape=jax.ShapeDtypeStruct((b, seq, c), F32),
        compiler_params=_params("parallel"),
        name="hy_ctx",
    )(v, x1, x2, kern, ff, fi, hy_bias)


def _hyena(v, x1, x2, lc, f_w1, f_b1, f_w2, f_b2, f_w3, hy_bias):
    seq = v.shape[1] - lc
    yc = _ctx_conv_gated(v, x1, x2, _hyena_kernels(lc, f_w1, f_b1, f_w2, f_b2, f_w3), hy_bias)
    return _long_conv_gated(v, (x1, x2), _hyena_kernels(seq, f_w1, f_b1, f_w2, f_b2, f_w3), hy_bias, yc, lc)


def _rope_tables_t(seq, lc):
    rows = seq // GRID_W
    r = jnp.broadcast_to(jnp.arange(rows, dtype=F32)[:, None], (rows, GRID_W)).reshape(-1)
    col = jnp.broadcast_to(jnp.arange(GRID_W, dtype=F32)[None, :], (rows, GRID_W)).reshape(-1)
    n_pairs = HEAD_DIM // 4
    inv = ROPE_THETA ** (-jnp.arange(n_pairs, dtype=F32) / n_pairs)
    ang = jnp.concatenate([r[:, None] * inv, col[:, None] * inv], axis=-1)
    cos = jnp.concatenate([jnp.ones((lc, 2 * n_pairs), F32), jnp.cos(ang)], axis=0)
    sin = jnp.concatenate([jnp.zeros((lc, 2 * n_pairs), F32), jnp.sin(ang)], axis=0)
    return cos.T, sin.T


def kernel(x, c, ctx, c_ctx, w_ada, b_ada, norm_g, w_mlp_in, w_mlp_out, ev_w_in, ev_w_lr, ev_b_lr, ev_gla_g,
           ev_qk_g, ev_w_out, od_w_in, od_conv_w, od_conv_b, od_f_w1, od_f_b1, od_f_w2, od_f_b2, od_f_w3,
           od_hy_bias, od_lam, od_diff_g, od_w_out):
    bsz, seq, d = x.shape
    lc = ctx.shape[1]
    depth = w_ada.shape[0]
    assert lc == TM and seq % TM == 0 and bsz <= MOD_CTX_ROW and bsz % POST_PB == 0 and bsz % IN_PB == 0 and bsz % GLA_PB == 0
    cos_t, sin_t = _rope_tables_t(seq, lc)
    cc = jnp.zeros((8, d), F32).at[:bsz].set(c).at[MOD_CTX_ROW:MOD_CTX_ROW + max(POST_PB, IN_PB)].set(c_ctx)
    mods = _mods(cc, w_ada, b_ada).reshape(depth, 8, N_MOD, d)
    xs = (x, ctx)
    for i in range(depth):
        last = i == depth - 1
        j = i // 2
        g = norm_g[i]
        if i % 2 == 0:
            qk, la, gv, gg, aq, ak, av, kn = _even_in(xs, mods[i], g[0:1], ev_w_in[j], ev_w_lr[j], ev_b_lr[j],
                                                  ev_qk_g[j], cos_t, sin_t)
            ya = (_gla_direction(qk, la, gv, False), _gla_direction(qk, la, gv, True), gg, ev_gla_g[j])
            yb = _gqa_attention(aq, ak, av, jnp.max(kn, axis=1))
            w_out = ev_w_out[j]
        else:
            hv, hx1, hx2, dq, dk, dv, kn = _odd_in(xs, mods[i], g[0:1], od_w_in[j], od_conv_w[j], od_conv_b[j],
                                                   cos_t, sin_t)
            ya = _hyena(hv, hx1, hx2, lc, od_f_w1[j], od_f_b1[j], od_f_w2[j], od_f_b2[j], od_f_w3[j], od_hy_bias[j])
            yb = _diff_attention(dq, dk, dv, jnp.max(kn, axis=1), od_lam[j], od_diff_g[j], i)
            w_out = od_w_out[j]
        xs = _post(xs, ya, yb, mods[i], g, w_out, w_mlp_in[i], w_mlp_out[i], skip_ctx=last)
    return xs
```

```python
import functools
import math

import jax
import jax.numpy as jnp
import numpy as np
from jax import lax
from jax.experimental import pallas as pl
from jax.experimental.pallas import tpu as pltpu

F32 = jnp.float32
BF16 = jnp.bfloat16

TM = 256
GRID_W = 64
HEAD_DIM = 64
ROPE_THETA = 10000.0
NORM_EPS = 1e-6
N_MOD = 6
VMEM_LIMIT = 56 * 1024 * 1024

GLA_HEADS = 4
GLA_DK = 64
GLA_DV = 128
GLA_RANK = 16
GLA_NORMALIZER = 16.0
GLA_CHUNK = 64
GLA_QK = GLA_HEADS * GLA_DK
GLA_V = GLA_HEADS * GLA_DV
GQA_HEADS = 8
GQA_KV_HEADS = 2
HY_CH = 512
HY_ORDER = 2
HY_BANDS = 16
HY_TARGET = 1e-2
HY_FAST_PCT = 0.3
HY_SLOW_PCT = 1.5
HY_COLS = (HY_ORDER + 1) * HY_CH
DIFF_HEADS = 4
DIFF_QK = DIFF_HEADS * 2 * HEAD_DIM
DIFF_V = DIFF_HEADS * 2 * HEAD_DIM
ONES_ROWS = 16
N_SCORE_HEADS = 8
LOG2E = 1.4426950408889634
Q_SCALE = HEAD_DIM ** -0.5 * LOG2E
GQA_VROWS = HEAD_DIM + ONES_ROWS
SCORE_BOUND_SLACK = 1.01
FAST_UNROLL_MAX = 32
SCORE_BOUND_LIMIT = 60.0
DIFF_VROWS = 2 * HEAD_DIM + ONES_ROWS


def _params(*sem):
    return pltpu.CompilerParams(dimension_semantics=sem, vmem_limit_bytes=VMEM_LIMIT)


def _const_spec(shape):
    nd = len(shape)
    return pl.BlockSpec(shape, lambda *_: (0,) * nd, pipeline_mode=pl.Buffered(1))


def _rms(x, g):
    return x * lax.rsqrt(jnp.mean(x * x, axis=-1, keepdims=True) + NORM_EPS) * g


MOD_CTX_ROW = 4
POST_PB = 2
IN_PB = 4
GLA_PB = 4


def _mod_block(bb, i, pb):
    return jnp.where(i == 0, MOD_CTX_ROW // pb, bb)


def _stream_specs(xs, pb, off=0):
    if isinstance(xs, tuple):
        x, ctx = xs
        d = x.shape[-1]
        return [pl.BlockSpec((pb, TM, d), lambda bb, i: (bb, jnp.maximum(i + off - 1, 0), 0)),
                pl.BlockSpec((pb, TM, d), lambda bb, i: (bb, 0, 0))], [x, ctx]
    return [pl.BlockSpec((pb, TM, xs.shape[-1]), lambda bb, i: (bb, i + off, 0))], [xs]


def _stream_shape(xs):
    if isinstance(xs, tuple):
        return xs[0].shape[0], xs[0].shape[1] + xs[1].shape[1], xs[0].shape[2]
    return xs.shape


def _stream_tile(stream_refs, s, tile):
    if len(stream_refs) == 2:
        return jnp.where(tile == 0, stream_refs[1][s], stream_refs[0][s])
    return stream_refs[0][s]


def _mods_kernel(cc_ref, w_ref, b_ref, o_ref):
    cc = cc_ref[...]
    s = cc / (1.0 + jnp.exp(-cc))
    o_ref[0] = jnp.dot(s.astype(BF16), w_ref[0].astype(BF16), preferred_element_type=F32) + b_ref[0]


def _mods(cc, w_ada, b_ada):
    depth, d, n = w_ada.shape
    nb = n // 4
    return pl.pallas_call(
        _mods_kernel,
        grid=(depth, n // nb),
        in_specs=[pl.BlockSpec((8, d), lambda l, j: (0, 0)),
                  pl.BlockSpec((1, d, nb), lambda l, j: (l, 0, j)),
                  pl.BlockSpec((1, 1, nb), lambda l, j: (l, 0, j))],
        out_specs=pl.BlockSpec((1, 8, nb), lambda l, j: (l, 0, j)),
        out_shape=jax.ShapeDtypeStruct((depth, 8, n), F32),
        compiler_params=_params("parallel", "parallel"),
        name="ada_mods",
    )(cc, w_ada, b_ada.reshape(depth, 1, n))


def _normed_input(x, mods, g):
    return (_rms(x, g) * (1.0 + mods[1:2]) + mods[0:1]).astype(BF16)


def _rope_rows(y, cos, sin):
    x1, x2 = y[:32], y[32:]
    return x1 * cos - x2 * sin, x1 * sin + x2 * cos


def _log_sigmoid(x):
    return jnp.minimum(x, 0.0) - jnp.log(1.0 + jnp.exp(-jnp.abs(x)))


def _nt(a, b):
    return lax.dot_general(a, b, (((1,), (1,)), ((), ())), preferred_element_type=F32)


def _max_sq_norm_row(parts):
    n2 = sum(jnp.sum(jnp.square(p.astype(BF16).astype(F32)), axis=0, keepdims=True) for p in parts)
    return jnp.broadcast_to(jnp.max(n2, axis=1, keepdims=True), (1, 128))


def _even_in_kernel(*refs, n_stream):
    (mods_ref, g_ref, wn_ref, wt_ref, wlr_ref, blr_ref, gq_ref, gk_ref, cos_ref, sin_ref,
     qk_ref, la_ref, gv_ref, gg_ref, aq_ref, ak_ref, av_ref, kn_ref) = refs[n_stream:]
    tile = pl.program_id(1)
    hs = [_normed_input(_stream_tile(refs[:n_stream], s, tile), mods_ref[s], g_ref[0:1]) for s in range(IN_PB)]
    ps = [jnp.dot(h, wn_ref[...], preferred_element_type=F32) for h in hs]
    pts = [_nt(wt_ref[...], h) for h in hs]
    cos, sin = cos_ref[...], sin_ref[...]
    gq, gk_gain = gq_ref[...], gk_ref[...]
    ones = jnp.ones((ONES_ROWS, TM), BF16)
    for s in range(IN_PB):
        p, pt_all = ps[s], pts[s]
        qk_ref[s, :, :GLA_QK] = p[:, :GLA_QK] * GLA_DK ** -0.5
        qk_ref[s, :, GLA_QK:] = p[:, GLA_QK:2 * GLA_QK]
        lr = p[:, 2 * GLA_QK:].astype(BF16)
        gk = jnp.dot(lr, wlr_ref[...], preferred_element_type=F32) + blr_ref[...]
        la_ref[s] = _log_sigmoid(gk) * (1.0 / GLA_NORMALIZER)
        gv_ref[s, 0] = pt_all[:GLA_V].astype(BF16)
        gg_ref[s, 0] = pt_all[GLA_V:2 * GLA_V]
        pt = pt_all[2 * GLA_V:]
        for hh in range(GQA_HEADS):
            blk = pt[hh * 64:(hh + 1) * 64]
            y = blk * lax.rsqrt(jnp.mean(blk * blk, axis=0, keepdims=True) + NORM_EPS) * gq
            o1, o2 = _rope_rows(y, cos, sin)
            aq_ref[s, 0, hh * 64:hh * 64 + 32, :] = (o1 * Q_SCALE).astype(BF16)
            aq_ref[s, 0, hh * 64 + 32:hh * 64 + 64, :] = (o2 * Q_SCALE).astype(BF16)
        kparts, knorms = [], []
        for hh in range(GQA_KV_HEADS):
            blk = pt[512 + hh * 64:512 + (hh + 1) * 64]
            y = blk * lax.rsqrt(jnp.mean(blk * blk, axis=0, keepdims=True) + NORM_EPS) * gk_gain
            o1, o2 = _rope_rows(y, cos, sin)
            kparts += [o1, o2]
            knorms.append(_max_sq_norm_row([o1, o2]))
        ak_ref[s] = jnp.concatenate(kparts, axis=0).astype(BF16).T
        kn_ref[s, 0] = jnp.concatenate(knorms + [jnp.zeros((8 - GQA_KV_HEADS, 128), F32)], axis=0)
        for hh in range(GQA_KV_HEADS):
            av_ref[s, 0, hh * GQA_VROWS:hh * GQA_VROWS + 64, :] = pt[640 + hh * 64:640 + (hh + 1) * 64].astype(BF16)
            av_ref[s, 0, hh * GQA_VROWS + 64:(hh + 1) * GQA_VROWS, :] = ones


def _odd_in_kernel(xs_ref, prev_ref, next_ref, mods_ref, g_ref, wn_ref, wt_ref, cw_ref, cb_ref, cos_ref, sin_ref,
                   v_ref, x1_ref, x2_ref, dq_ref, dk_ref, dv_ref, kn_ref, *, n_tiles):
    i = pl.program_id(1)
    hs = [_normed_input(xs_ref[s], mods_ref[s], g_ref[0:1]) for s in range(IN_PB)]
    h_ext = [jnp.concatenate([hs[s], _normed_input(prev_ref[s], mods_ref[s], g_ref[0:1]),
                              _normed_input(next_ref[s], mods_ref[s], g_ref[0:1])], axis=0) for s in range(IN_PB)]
    us = [jnp.dot(h, wn_ref[...], preferred_element_type=F32) for h in h_ext]
    pts = [_nt(wt_ref[...], h) for h in hs]
    cos, sin = cos_ref[...], sin_ref[...]
    ones = jnp.ones((ONES_ROWS, TM), BF16)
    nv = 2 * HEAD_DIM
    cw = cw_ref[...]
    row_id = lax.broadcasted_iota(jnp.int32, (TM, HY_COLS), 0)
    no_prev = (i == 0) | (i == 1)
    no_next = (i == 0) | (i == n_tiles - 1)
    for s in range(IN_PB):
        pt = pts[s]
        u = us[s][:TM]
        prev_row = jnp.where(no_prev, 0.0, us[s][TM + 7:TM + 8])
        next_row = jnp.where(no_next, 0.0, us[s][TM + 8:TM + 9])
        up = jnp.where(row_id == 0, prev_row, pltpu.roll(u, 1, axis=0))
        dn = jnp.where(row_id == TM - 1, next_row, pltpu.roll(u, TM - 1, axis=0))
        conv = up * cw[0:1] + u * cw[1:2] + dn * cw[2:3] + cb_ref[...]
        v_ref[s] = conv[:, :HY_CH]
        x1_ref[s] = conv[:, HY_CH:2 * HY_CH]
        x2_ref[s] = conv[:, 2 * HY_CH:]
        kparts, knorms = [], []
        for hh in range(2 * DIFF_HEADS):
            o1, o2 = _rope_rows(pt[hh * 64:(hh + 1) * 64], cos, sin)
            dq_ref[s, 0, hh * 64:hh * 64 + 32, :] = (o1 * Q_SCALE).astype(BF16)
            dq_ref[s, 0, hh * 64 + 32:hh * 64 + 64, :] = (o2 * Q_SCALE).astype(BF16)
            k1, k2 = _rope_rows(pt[DIFF_QK + hh * 64:DIFF_QK + (hh + 1) * 64], cos, sin)
            kparts += [k1, k2]
            knorms.append(_max_sq_norm_row([k1, k2]))
        dk_ref[s] = jnp.concatenate(kparts, axis=0).astype(BF16).T
        kn_ref[s, 0] = jnp.concatenate(knorms, axis=0)
        for hh in range(DIFF_HEADS):
            rows = pt[2 * DIFF_QK + hh * nv:2 * DIFF_QK + (hh + 1) * nv]
            dv_ref[s, 0, hh * DIFF_VROWS:hh * DIFF_VROWS + nv, :] = rows.astype(BF16)
            dv_ref[s, 0, hh * DIFF_VROWS + nv:(hh + 1) * DIFF_VROWS, :] = ones


def _pair_perm(n_heads):
    base = np.concatenate([np.arange(0, HEAD_DIM, 2), np.arange(1, HEAD_DIM, 2)])
    return np.concatenate([h * HEAD_DIM + base for h in range(n_heads)])


def _even_in(xs, mods_l, g0, w_in, w_lr, b_lr, qk_g, cos_t, sin_t):
    b, t, d = _stream_shape(xs)
    nt = t // TM
    x_specs, x_args = _stream_specs(xs, IN_PB)
    c_v = 2 * GLA_QK
    c_g = c_v + GLA_V
    c_lr0 = c_g + GLA_V
    c_lr = c_lr0 + 2 * GLA_RANK
    wn = jnp.concatenate([w_in[:, :c_v], w_in[:, c_lr0:c_lr], jnp.zeros((d, 128 - 2 * GLA_RANK), F32)],
                         axis=1).astype(BF16)
    cq = c_lr + GQA_HEADS * HEAD_DIM
    ck = cq + GQA_KV_HEADS * HEAD_DIM
    wq = w_in[:, c_lr:cq][:, _pair_perm(GQA_HEADS)]
    wk = w_in[:, cq:ck][:, _pair_perm(GQA_KV_HEADS)]
    wt = jnp.concatenate([w_in[:, c_v:c_lr0], wq, wk, w_in[:, ck:]], axis=1).T.astype(BF16)
    wlr = jnp.zeros((128, 2 * GLA_QK), F32)
    wlr = wlr.at[:GLA_RANK, :GLA_QK].set(w_lr[0]).at[GLA_RANK:2 * GLA_RANK, GLA_QK:].set(w_lr[1]).astype(BF16)
    blr = b_lr.reshape(1, 2 * GLA_QK)
    perm = _pair_perm(1)
    gq = qk_g[0][perm].reshape(HEAD_DIM, 1)
    gk = qk_g[1][perm].reshape(HEAD_DIM, 1)
    tok = lambda n: pl.BlockSpec((IN_PB, TM, n), lambda bb, i: (bb, i, 0))
    tr = lambda n: pl.BlockSpec((IN_PB, 1, n, TM), lambda bb, i: (bb, i, 0, 0))
    return pl.pallas_call(
        functools.partial(_even_in_kernel, n_stream=len(x_args)),
        grid=(b // IN_PB, nt),
        in_specs=x_specs + [
                  pl.BlockSpec((IN_PB, N_MOD, d), lambda bb, i: (_mod_block(bb, i, IN_PB), 0, 0)),
                  _const_spec((1, d)),
                  _const_spec(wn.shape), _const_spec(wt.shape), _const_spec(wlr.shape), _const_spec(blr.shape),
                  _const_spec(gq.shape), _const_spec(gk.shape),
                  pl.BlockSpec((32, TM), lambda bb, i: (0, i)),
                  pl.BlockSpec((32, TM), lambda bb, i: (0, i))],
        out_specs=[tok(2 * GLA_QK), tok(2 * GLA_QK), tr(GLA_V), tr(GLA_V), tr(512), tok(128),
                   tr(GQA_KV_HEADS * GQA_VROWS), pl.BlockSpec((IN_PB, 1, 8, 128), lambda bb, i: (bb, i, 0, 0))],
        out_shape=[jax.ShapeDtypeStruct((b, t, 2 * GLA_QK), F32),
                   jax.ShapeDtypeStruct((b, t, 2 * GLA_QK), F32),
                   jax.ShapeDtypeStruct((b, nt, GLA_V, TM), BF16),
                   jax.ShapeDtypeStruct((b, nt, GLA_V, TM), F32),
                   jax.ShapeDtypeStruct((b, nt, 512, TM), BF16),
                   jax.ShapeDtypeStruct((b, t, 128), BF16),
                   jax.ShapeDtypeStruct((b, nt, GQA_KV_HEADS * GQA_VROWS, TM), BF16),
                   jax.ShapeDtypeStruct((b, nt, 8, 128), F32)],
        compiler_params=_params("parallel", "parallel"),
        name="even_in",
    )(*x_args, mods_l, g0, wn, wt, wlr, blr, gq, gk, cos_t, sin_t)


def _odd_in(xs, mods_l, g0, w_in, conv_w, conv_b, cos_t, sin_t):
    b, t, d = xs.shape
    nt = t // TM
    r8 = TM // 8
    last_blk = t // 8 - 1
    wn = w_in[:, :HY_COLS].astype(BF16)
    perm = _pair_perm(2 * DIFF_HEADS)
    wq = w_in[:, HY_COLS:HY_COLS + DIFF_QK][:, perm]
    wk = w_in[:, HY_COLS + DIFF_QK:HY_COLS + 2 * DIFF_QK][:, perm]
    wt = jnp.concatenate([wq, wk, w_in[:, HY_COLS + 2 * DIFF_QK:]], axis=1).T.astype(BF16)
    tok = lambda n: pl.BlockSpec((IN_PB, TM, n), lambda bb, i: (bb, i, 0))
    tr = lambda n: pl.BlockSpec((IN_PB, 1, n, TM), lambda bb, i: (bb, i, 0, 0))
    hy = jax.ShapeDtypeStruct((b, t, HY_CH), F32)
    return pl.pallas_call(
        functools.partial(_odd_in_kernel, n_tiles=nt),
        grid=(b // IN_PB, nt),
        in_specs=[tok(d),
                  pl.BlockSpec((IN_PB, 8, d), lambda bb, i: (bb, jnp.maximum(i * r8 - 1, 0), 0)),
                  pl.BlockSpec((IN_PB, 8, d), lambda bb, i: (bb, jnp.minimum((i + 1) * r8, last_blk), 0)),
                  pl.BlockSpec((IN_PB, N_MOD, d), lambda bb, i: (_mod_block(bb, i, IN_PB), 0, 0)),
                  _const_spec((1, d)),
                  _const_spec(wn.shape), _const_spec(wt.shape),
                  _const_spec(conv_w.shape), _const_spec((1, HY_COLS)),
                  pl.BlockSpec((32, TM), lambda bb, i: (0, i)),
                  pl.BlockSpec((32, TM), lambda bb, i: (0, i))],
        out_specs=[tok(HY_CH), tok(HY_CH), tok(HY_CH), tr(DIFF_QK), tok(DIFF_QK), tr(DIFF_HEADS * DIFF_VROWS),
                   pl.BlockSpec((IN_PB, 1, 8, 128), lambda bb, i: (bb, i, 0, 0))],
        out_shape=[hy, hy, hy,
                   jax.ShapeDtypeStruct((b, nt, DIFF_QK, TM), BF16),
                   jax.ShapeDtypeStruct((b, t, DIFF_QK), BF16),
                   jax.ShapeDtypeStruct((b, nt, DIFF_HEADS * DIFF_VROWS, TM), BF16),
                   jax.ShapeDtypeStruct((b, nt, 8, 128), F32)],
        compiler_params=_params("parallel", "parallel"),
        name="odd_in",
    )(xs, xs, xs, mods_l, g0, wn, wt, conv_w, conv_b.reshape(1, HY_COLS), cos_t, sin_t)


def _attn_core(q_ref, k_ref, v_ref, kmax_ref, scratch, *, n_kv, k_col, k_row, q_slot, v_row, v_rows, scores_between,
               vpu_denominator):
    qpad_ref, bound_ref, m_ref, acc_ref, s0_ref, s1_ref, c0_ref, c1_ref, p0_ref, p1_ref = scratch
    s_refs, c_refs, p_refs = (s0_ref, s1_ref), (c0_ref, c1_ref), (p0_ref, p1_ref)
    i = pl.program_id(1)
    zeros = jnp.zeros((HEAD_DIM, TM), BF16)
    bound_max = jnp.zeros((1, TM), F32)
    for hh in range(N_SCORE_HEADS):
        qh = q_ref[0, 0, hh * HEAD_DIM:(hh + 1) * HEAD_DIM, :]
        qpad_ref[hh] = jnp.concatenate([qh, zeros] if q_slot(hh) == 0 else [zeros, qh], axis=0)
        qf = qh.astype(F32)
        q_norm2 = jnp.sum(qf * qf, axis=0, keepdims=True)
        k_norm2 = kmax_ref[0, k_row(hh):k_row(hh) + 1, 0:1]
        bound = jnp.sqrt(q_norm2 * k_norm2) * SCORE_BOUND_SLACK
        bound_ref[hh] = bound
        bound_max = jnp.maximum(bound_max, bound)
    acc_ref[...] = jnp.zeros(acc_ref.shape, F32)
    n = jnp.where(i == 0, 1, n_kv)
    fixed_reference = jnp.max(bound_max) <= SCORE_BOUND_LIMIT

    def produce(j, hh, slot):
        kc = k_ref[0, pl.ds(pl.multiple_of(j * TM, TM), TM), k_col(hh):k_col(hh) + 128]
        s = jnp.dot(kc, qpad_ref[hh], preferred_element_type=F32)
        p = jnp.exp2(s - bound_ref[hh])
        if vpu_denominator:
            m_ref[hh] = m_ref[hh] + jnp.sum(p, axis=0, keepdims=True)
        p_refs[slot][hh] = p.astype(BF16)

    n_val = v_rows - ONES_ROWS if vpu_denominator else v_rows

    def reduce(j, hh, slot):
        vc = v_ref[0, j, v_row(hh):v_row(hh) + n_val, :]
        acc_ref[hh, :n_val] = acc_ref[hh, :n_val] + jnp.dot(vc, p_refs[slot][hh], preferred_element_type=F32)

    unroll = math.gcd(n_kv - 1, FAST_UNROLL_MAX)

    @pl.when(fixed_reference)
    def _():
        m_ref[...] = jnp.zeros(m_ref.shape, F32)
        for hh in range(N_SCORE_HEADS):
            produce(0, hh, 0)

        def fast_step(j, slot):
            for hh in range(N_SCORE_HEADS):
                produce(j + 1, hh, 1 - slot)
                reduce(j, hh, slot)

        def fast_body(jj, carry):
            for u in range(unroll):
                fast_step(unroll * jj + u, u % 2)
            return carry

        lax.fori_loop(0, (n - 1) // unroll, fast_body, 0)
        for hh in range(N_SCORE_HEADS):
            reduce(n - 1, hh, 0)
            if vpu_denominator:
                acc_ref[hh, n_val:, :] = jnp.broadcast_to(m_ref[hh], (ONES_ROWS, TM))

    @pl.when(jnp.logical_not(fixed_reference))
    def _():
        _attn_running_max(k_ref, v_ref, qpad_ref, m_ref, acc_ref, s_refs, c_refs, n=n, k_col=k_col, v_row=v_row,
                          v_rows=v_rows, scores_between=scores_between)


def _attn_running_max(k_ref, v_ref, qpad_ref, m_ref, acc_ref, s_refs, c_refs, *, n, k_col, v_row, v_rows,
                      scores_between):
    m_ref[...] = jnp.full(m_ref.shape, -jnp.inf, F32)

    def scores(j, hh, slot):
        kc = k_ref[0, pl.ds(pl.multiple_of(j * TM, TM), TM), k_col(hh):k_col(hh) + 128]
        s = jnp.dot(kc, qpad_ref[hh], preferred_element_type=F32)
        s_refs[slot][hh] = s
        c_refs[slot][hh] = jnp.max(s, axis=0, keepdims=True)

    def probs(hh, slot):
        m_old = m_ref[hh]
        m_new = jnp.maximum(m_old, c_refs[slot][hh])
        return m_old, m_new, jnp.exp2(s_refs[slot][hh] - m_new).astype(BF16)

    def accumulate(j, hh, state):
        m_old, m_new, p = state
        vc = v_ref[0, j, v_row(hh):v_row(hh) + v_rows, :]
        alpha = jnp.exp2(m_old - m_new)
        acc_ref[hh] = acc_ref[hh] * alpha + jnp.dot(vc, p, preferred_element_type=F32)
        m_ref[hh] = m_new

    for hh in range(N_SCORE_HEADS):
        scores(0, hh, 0)

    def step(j, slot):
        for hh in range(N_SCORE_HEADS):
            state = probs(hh, slot)
            if scores_between:
                scores(j + 1, hh, 1 - slot)
            accumulate(j, hh, state)
            if not scores_between:
                scores(j + 1, hh, 1 - slot)

    def body(jj, carry):
        step(2 * jj, 0)
        step(2 * jj + 1, 1)
        return carry

    lax.fori_loop(0, (n - 1) // 2, body, 0)
    for hh in range(N_SCORE_HEADS):
        accumulate(n - 1, hh, probs(hh, 0))


def _attn_scratch(v_rows):
    stat = pltpu.VMEM((N_SCORE_HEADS, 1, TM), F32)
    return [pltpu.VMEM((N_SCORE_HEADS, 128, TM), BF16),
            stat, stat,
            pltpu.VMEM((N_SCORE_HEADS, v_rows, TM), F32),
            pltpu.VMEM((N_SCORE_HEADS, TM, TM), F32), pltpu.VMEM((N_SCORE_HEADS, TM, TM), F32),
            stat, stat,
            pltpu.VMEM((N_SCORE_HEADS, TM, TM), BF16), pltpu.VMEM((N_SCORE_HEADS, TM, TM), BF16)]


def _gqa_kernel(q_ref, k_ref, v_ref, kmax_ref, o_ref, *scratch, n_kv):
    group = GQA_HEADS // GQA_KV_HEADS
    _attn_core(q_ref, k_ref, v_ref, kmax_ref, scratch, n_kv=n_kv,
               k_col=lambda h: 0, k_row=lambda h: h // group, q_slot=lambda h: h // group,
               v_row=lambda h: (h // group) * GQA_VROWS, v_rows=GQA_VROWS, scores_between=True,
               vpu_denominator=True)
    acc_ref = scratch[3]
    for hh in range(GQA_HEADS):
        a = acc_ref[hh]
        o_ref[0, 0, hh * HEAD_DIM:(hh + 1) * HEAD_DIM, :] = (a[:HEAD_DIM] / a[HEAD_DIM:HEAD_DIM + 1]).astype(o_ref.dtype)


def _gqa_attention(aq, ak, av, kmax):
    b, nt, nq, _ = aq.shape
    t = nt * TM
    assert nt % 2 == 1 and nt > 1
    return pl.pallas_call(
        functools.partial(_gqa_kernel, n_kv=nt),
        grid=(b, nt),
        in_specs=[pl.BlockSpec((1, 1, nq, TM), lambda bb, i: (bb, i, 0, 0)),
                  pl.BlockSpec((1, t, 128), lambda bb, i: (bb, 0, 0)),
                  pl.BlockSpec((1, nt, GQA_KV_HEADS * GQA_VROWS, TM), lambda bb, i: (bb, 0, 0, 0)),
                  pl.BlockSpec((1, 8, 128), lambda bb, i: (bb, 0, 0))],
        out_specs=pl.BlockSpec((1, 1, nq, TM), lambda bb, i: (bb, i, 0, 0)),
        out_shape=jax.ShapeDtypeStruct((b, nt, nq, TM), BF16),
        scratch_shapes=_attn_scratch(GQA_VROWS),
        compiler_params=_params("parallel", "arbitrary"),
        name="gqa_attn",
    )(aq, ak, av, kmax)


def _diff_kernel(q_ref, k_ref, v_ref, kmax_ref, lam_ref, gain_ref, o_ref, *scratch, n_kv, lam_init):
    _attn_core(q_ref, k_ref, v_ref, kmax_ref, scratch, n_kv=n_kv,
               k_col=lambda hc: (hc // 2) * 128, k_row=lambda hc: hc, q_slot=lambda hc: hc % 2,
               v_row=lambda hc: (hc // 2) * DIFF_VROWS, v_rows=DIFF_VROWS, scores_between=False,
               vpu_denominator=True)
    acc_ref = scratch[3]
    lp = lam_ref[...]
    lam = (jnp.exp(jnp.sum(lp[0:1] * lp[1:2], axis=1, keepdims=True))
           - jnp.exp(jnp.sum(lp[2:3] * lp[3:4], axis=1, keepdims=True)) + lam_init)
    gain = gain_ref[...]
    nv = 2 * HEAD_DIM
    for hh in range(DIFF_HEADS):
        a0, a1 = acc_ref[2 * hh], acc_ref[2 * hh + 1]
        o = a0[:nv] / a0[nv:nv + 1] - lam * (a1[:nv] / a1[nv:nv + 1])
        y = o * lax.rsqrt(jnp.mean(o * o, axis=0, keepdims=True) + NORM_EPS) * gain
        o_ref[0, 0, hh * nv:(hh + 1) * nv, :] = (y * (1.0 - lam_init)).astype(o_ref.dtype)


def _diff_attention(dq, dk, dv, kmax, lam_p, gain, layer_idx):
    b, nt, nq, _ = dq.shape
    t = nt * TM
    assert nt % 2 == 1 and nt > 1
    lam_init = 0.8 - 0.6 * math.exp(-0.3 * layer_idx)
    return pl.pallas_call(
        functools.partial(_diff_kernel, n_kv=nt, lam_init=lam_init),
        grid=(b, nt),
        in_specs=[pl.BlockSpec((1, 1, nq, TM), lambda bb, i: (bb, i, 0, 0)),
                  pl.BlockSpec((1, t, DIFF_QK), lambda bb, i: (bb, 0, 0)),
                  pl.BlockSpec((1, nt, DIFF_HEADS * DIFF_VROWS, TM), lambda bb, i: (bb, 0, 0, 0)),
                  pl.BlockSpec((1, 8, 128), lambda bb, i: (bb, 0, 0)),
                  _const_spec(lam_p.shape),
                  _const_spec((2 * HEAD_DIM, 1))],
        out_specs=pl.BlockSpec((1, 1, DIFF_V, TM), lambda bb, i: (bb, i, 0, 0)),
        out_shape=jax.ShapeDtypeStruct((b, nt, DIFF_V, TM), BF16),
        scratch_shapes=_attn_scratch(DIFF_VROWS),
        compiler_params=_params("parallel", "arbitrary"),
        name="diff_attn",
    )(dq, dk, dv, kmax, lam_p, gain.reshape(2 * HEAD_DIM, 1))


def _split_bf16(x):
    hi = x.astype(BF16)
    return hi, (x - hi.astype(F32)).astype(BF16)


def _gla_kernel(qkf_ref, laf_ref, vtf_ref, qkb_ref, lab_ref, vtb_ref, of_ref, ob_ref, stf_ref, stb_ref):
    @pl.when(pl.program_id(1) == 0)
    def _():
        stf_ref[...] = jnp.zeros(stf_ref.shape, F32)
        stb_ref[...] = jnp.zeros(stb_ref.shape, F32)

    _gla_tile(qkf_ref, laf_ref, vtf_ref, of_ref, stf_ref, backward=False)
    _gla_tile(qkb_ref, lab_ref, vtb_ref, ob_ref, stb_ref, backward=True)


def _gla_tile(qk_ref, la_ref, vt_ref, o_ref, st_ref, *, backward):
    n_chunk = TM // GLA_CHUNK
    r = lax.broadcasted_iota(jnp.int32, (TM, TM), 0)
    c = lax.broadcasted_iota(jnp.int32, (TM, TM), 1)
    same = (r // GLA_CHUNK) == (c // GLA_CHUNK)
    causal = (c >= r) if backward else (c <= r)
    tri = jnp.where(same & causal, 1.0, 0.0).astype(BF16)
    blk = jnp.where(same, 1.0, 0.0).astype(BF16)
    lane = lax.broadcasted_iota(jnp.int32, (TM, 128), 1)
    row = lax.broadcasted_iota(jnp.int32, (TM, 128), 0)
    keep = same & ((r >= c) if backward else (r <= c))
    order = range(n_chunk - 1, -1, -1) if backward else range(n_chunk)
    units = [(s, h) for s in range(GLA_PB) for h in range(GLA_HEADS)]
    pair = lambda h: slice((h // 2) * 128, (h // 2 + 1) * 128)
    q_in, k_in, k_end, dec = [], [], [], []
    for s in range(GLA_PB):
        la_hi, la_lo = _split_bf16(la_ref[s])
        cum = jnp.dot(tri, la_hi, preferred_element_type=F32) + jnp.dot(tri, la_lo, preferred_element_type=F32)
        tot = jnp.dot(blk, la_hi, preferred_element_type=F32) + jnp.dot(blk, la_lo, preferred_element_type=F32)
        q = qk_ref[s, :, :GLA_QK]
        k = qk_ref[s, :, GLA_QK:]
        q_in.append((q * jnp.exp(cum)).astype(BF16))
        k_in.append(k * jnp.exp(-cum))
        k_end.append(k * jnp.exp(tot - cum))
        dec.append(jnp.exp(tot))
    head_lanes = {h: (lane // GLA_DK) == (h % 2) for h in range(GLA_HEADS)}
    q_pair = {(s, h): q_in[s][:, pair(h)] for s, h in units}
    k_in_h = {(s, h): jnp.where(head_lanes[h], k_in[s][:, pair(h)], 0.0).astype(BF16) for s, h in units}
    k_end_h = {(s, h): jnp.where(head_lanes[h], k_end[s][:, pair(h)], 0.0) for s, h in units}
    vts = {(s, h): vt_ref[s, 0, h * GLA_DV:(h + 1) * GLA_DV, :] for s, h in units}
    a_ts = {u: jnp.where(keep, _nt(k_in_h[u], q_pair[u]), 0.0).astype(BF16) for u in units}
    o_ts = {u: jnp.dot(vts[u], a_ts[u], preferred_element_type=F32) for u in units}
    s_ts = {(s, h): st_ref[s, h] for s, h in units}
    for ci in order:
        in_chunk = (row // GLA_CHUNK) == ci
        for u in units:
            s, h = u
            q_c = jnp.where(in_chunk, q_pair[u], jnp.zeros_like(q_pair[u]))
            o_ts[u] = o_ts[u] + _nt(s_ts[u].astype(BF16), q_c)
            k_c = jnp.where(in_chunk, k_end_h[u], 0.0).astype(BF16)
            s_ts[u] = (s_ts[u] * dec[s][ci * GLA_CHUNK:ci * GLA_CHUNK + 1, pair(h)]
                       + jnp.dot(vts[u], k_c, preferred_element_type=F32))
    for s, h in units:
        st_ref[s, h] = s_ts[(s, h)]
        o_ref[s, 0, h * GLA_DV:(h + 1) * GLA_DV, :] = o_ts[(s, h)]


def _gla(qk, la, vt):
    b, t, _ = qk.shape
    nt = t // TM
    fwd = lambda i: i
    bwd = lambda i: jnp.where(i == 0, 0, nt - i)
    tok = lambda n, tile, col: pl.BlockSpec((GLA_PB, TM, n), lambda bb, i: (bb, tile(i), col))
    tr = lambda tile: pl.BlockSpec((GLA_PB, 1, GLA_V, TM), lambda bb, i: (bb, tile(i), 0, 0))
    out = jax.ShapeDtypeStruct((b, nt, GLA_V, TM), F32)
    state = pltpu.VMEM((GLA_PB, GLA_HEADS, GLA_DV, 128), F32)
    return pl.pallas_call(
        _gla_kernel,
        grid=(b // GLA_PB, nt),
        in_specs=[tok(2 * GLA_QK, fwd, 0), tok(GLA_QK, fwd, 0), tr(fwd),
                  tok(2 * GLA_QK, bwd, 0), tok(GLA_QK, bwd, 1), tr(bwd)],
        out_specs=[tr(fwd), tr(bwd)],
        out_shape=[out, out],
        scratch_shapes=[state, state],
        compiler_params=_params("parallel", "arbitrary"),
        name="gla",
    )(qk, la, vt, qk, la, vt)


def _post_phases(xs, mods, g, yas, ybs, woa_ref, wob_ref, w1_ref, w2_ref, ff_chunk):
    n = len(xs)
    ys = [jnp.dot(yas[s], woa_ref[...], preferred_element_type=F32)
          + jnp.dot(ybs[s].T, wob_ref[...], preferred_element_type=F32) for s in range(n)]
    x1s = [xs[s] + mods[s][2:3] * _rms(ys[s], g[1:2]) for s in range(n)]
    hs = [(_rms(x1s[s], g[2:3]) * (1.0 + mods[s][4:5]) + mods[s][3:4]).astype(BF16) for s in range(n)]
    d_ff = w1_ref.shape[1]
    accs = [jnp.zeros(xs[0].shape, F32) for _ in range(n)]
    for c in range(d_ff // ff_chunk):
        cols = slice(c * ff_chunk, (c + 1) * ff_chunk)
        for s in range(n):
            u = jnp.dot(hs[s], w1_ref[:, cols], preferred_element_type=F32)
            u = jnp.square(jnp.maximum(u, 0.0)).astype(BF16)
            accs[s] = accs[s] + jnp.dot(u, w2_ref[cols, :], preferred_element_type=F32)
    return [x1s[s] + mods[s][5:6] * _rms(accs[s], g[3:4]) for s in range(n)]


def _post_gla_kernel(*refs, ff_chunk, n_stream, off):
    of_ref, ob_ref, gg_ref, gain_ref, yb_ref, mods_ref, g_ref, woa_ref, wob_ref, w1_ref, w2_ref, o_ref = refs[n_stream:]
    tile = pl.program_id(1) + off
    gain = gain_ref[...]
    yas = []
    for s in range(POST_PB):
        parts = []
        for hh in range(GLA_HEADS):
            rows = slice(hh * GLA_DV, (hh + 1) * GLA_DV)
            o = of_ref[s, 0, rows, :] + ob_ref[s, 0, rows, :]
            gt = gg_ref[s, 0, rows, :]
            y = o * lax.rsqrt(jnp.mean(o * o, axis=0, keepdims=True) + NORM_EPS) * gain
            parts.append(y * (gt / (1.0 + jnp.exp(-gt))))
        yas.append(jnp.concatenate(parts, axis=0).astype(BF16).T)
    outs = _post_phases([_stream_tile(refs[:n_stream], s, tile) for s in range(POST_PB)],
                        [mods_ref[s] for s in range(POST_PB)], g_ref[...], yas,
                        [yb_ref[s, 0] for s in range(POST_PB)], woa_ref, wob_ref, w1_ref, w2_ref, ff_chunk)
    for s in range(POST_PB):
        o_ref[s] = outs[s]


def _post_kernel(*refs, ff_chunk, n_stream, off):
    ya_ref, yb_ref, mods_ref, g_ref, woa_ref, wob_ref, w1_ref, w2_ref, o_ref = refs[n_stream:]
    tile = pl.program_id(1) + off
    outs = _post_phases([_stream_tile(refs[:n_stream], s, tile) for s in range(POST_PB)],
                        [mods_ref[s] for s in range(POST_PB)], g_ref[...],
                        [ya_ref[s].astype(BF16) for s in range(POST_PB)], [yb_ref[s, 0] for s in range(POST_PB)],
                        woa_ref, wob_ref, w1_ref, w2_ref, ff_chunk)
    for s in range(POST_PB):
        o_ref[s] = outs[s]


def _post(xs, ya, yb_t, mods_l, g, w_out, w1, w2, skip_ctx):
    b, t, d = _stream_shape(xs)
    nt = t // TM
    pb = POST_PB
    half = w_out.shape[0] // 2
    woa = w_out[:half].astype(BF16)
    wob = w_out[half:].astype(BF16)
    w1 = w1.astype(BF16)
    w2 = w2.astype(BF16)
    off = 1 if skip_ctx else 0
    x_specs, x_args = _stream_specs(xs, pb, off)
    tok = lambda n: pl.BlockSpec((pb, TM, n), lambda bb, i: (bb, i + off, 0))
    tr = lambda n: pl.BlockSpec((pb, 1, n, TM), lambda bb, i: (bb, i + off, 0, 0))
    if isinstance(ya, tuple):
        o_f, o_b, gg, gain = ya
        body, name = _post_gla_kernel, "post_gla"
        a_specs = [tr(half), tr(half), tr(half), _const_spec((GLA_DV, 1))]
        a_args = [o_f, o_b, gg, gain.reshape(GLA_DV, 1)]
    else:
        body, name = _post_kernel, "post"
        a_specs, a_args = [tok(half)], [ya]
    return pl.pallas_call(
        functools.partial(body, ff_chunk=1024, n_stream=len(x_args), off=off),
        grid=(b // pb, nt - off),
        in_specs=x_specs + a_specs + [
                  tr(half),
                  pl.BlockSpec((pb, N_MOD, d), lambda bb, i: (_mod_block(bb, i + off, pb), 0, 0)),
                  _const_spec(g.shape),
                  _const_spec(woa.shape), _const_spec(wob.shape), _const_spec(w1.shape), _const_spec(w2.shape)],
        out_specs=pl.BlockSpec((pb, TM, d), lambda bb, i: (bb, i, 0)),
        out_shape=jax.ShapeDtypeStruct((b, t - off * TM, d), F32),
        compiler_params=_params("parallel", "parallel"),
        name=name,
    )(*x_args, *a_args, yb_t, mods_l, g, woa, wob, w1, w2)


def _dot3(a, b):
    ah, al = _split_bf16(a)
    bh, bl = _split_bf16(b)
    d = lambda x, y: jnp.dot(x, y, preferred_element_type=F32)
    return d(ah, bh) + (d(al, bh) + d(ah, bl))


def _filter_kernel(w1t_ref, w1c_ref, w1s_ref, b1_ref, w2_ref, b2_ref, w3_ref, delta_ref, o_ref, *, seq, rb):
    tap = pl.program_id(0) * rb + lax.broadcasted_iota(jnp.int32, (rb, 1), 0)
    d = tap.astype(F32)
    t = d / max(seq - 1, 1)
    w = (2 * math.pi / seq) * d
    band = lax.broadcasted_iota(jnp.int32, (1, HY_BANDS), 1).astype(F32)
    bands = 1e-4 + band * ((HY_BANDS - 1 - 1e-4) / (HY_BANDS - 1))
    ang = w * bands
    pre = t * w1t_ref[...] + _dot3(jnp.cos(ang), w1c_ref[...]) - _dot3(jnp.sin(ang), w1s_ref[...]) + b1_ref[...]
    hid = jnp.sin(pre)
    hid = jnp.sin(_dot3(hid, w2_ref[...]) + b2_ref[...])
    decay = jnp.exp(-t * delta_ref[...])
    decay_bwd = jnp.where(tap == 0, 0.0, decay)
    for o in range(HY_ORDER):
        o_ref[2 * o] = (_dot3(hid, w3_ref[2 * o]) * decay).astype(o_ref.dtype)
        o_ref[2 * o + 1] = (_dot3(hid, w3_ref[2 * o + 1]) * decay_bwd).astype(o_ref.dtype)


def _hyena_kernels(seq, f_w1, f_b1, f_w2, f_b2, f_w3):
    rb = min(512, seq)
    ffn = f_w2.shape[0]
    w3 = f_w3.reshape(ffn, 2 * HY_ORDER, HY_CH).transpose(1, 0, 2)
    deltas = jnp.abs(jnp.linspace(math.log(HY_TARGET) / HY_FAST_PCT, math.log(HY_TARGET) / HY_SLOW_PCT,
                                  HY_CH, dtype=F32)).reshape(1, HY_CH)
    return pl.pallas_call(
        functools.partial(_filter_kernel, seq=seq, rb=rb),
        grid=(seq // rb,),
        in_specs=[_const_spec((1, ffn)), _const_spec((HY_BANDS, ffn)), _const_spec((HY_BANDS, ffn)),
                  _const_spec((1, ffn)), _const_spec((ffn, ffn)), _const_spec((1, ffn)),
                  _const_spec(w3.shape), _const_spec((1, HY_CH))],
        out_specs=pl.BlockSpec((2 * HY_ORDER, rb, HY_CH), lambda j: (0, j, 0)),
        out_shape=jax.ShapeDtypeStruct((2 * HY_ORDER, seq, HY_CH), BF16),
        compiler_params=_params("parallel"),
        name="hy_filter",
    )(f_w1[0:1], f_w1[1:1 + HY_BANDS], f_w1[1 + HY_BANDS:], f_b1.reshape(1, ffn), f_w2, f_b2.reshape(1, ffn), w3,
      deltas)


def _dft_tables(seq):
    n = 2 * seq
    r = math.isqrt(n)
    assert r * r == n
    r1 = (r // 2 + 1 + 7) // 8 * 8
    idx = jnp.arange(r, dtype=jnp.int32)
    phi = (2 * math.pi / r) * ((idx[:r1, None] * idx[None, :]) % r).astype(F32)
    f1 = jnp.concatenate([jnp.cos(phi), -jnp.sin(phi)], axis=0).astype(BF16)
    m = (idx[:r1, None, None] + r * idx[None, :, None]) * idx[None, None, :]
    th = (2 * math.pi / n) * (m % n).astype(F32)
    c, s = jnp.cos(th), jnp.sin(th)
    blockm = lambda re, im: jnp.concatenate([jnp.concatenate([re, -im], axis=-1),
                                             jnp.concatenate([im, re], axis=-1)], axis=-2)
    g = blockm(c, -s).astype(BF16)
    ginv = blockm(jnp.swapaxes(c, 1, 2), jnp.swapaxes(s, 1, 2)).astype(BF16)
    k1 = idx[:r1]
    weight = jnp.where((k1 == 0) | (k1 == r // 2), 1.0, jnp.where(k1 < r // 2, 2.0, 0.0)) / n
    phi_t = phi.T[:r // 2]
    finv = jnp.concatenate([jnp.cos(phi_t) * weight, -jnp.sin(phi_t) * weight], axis=1).astype(BF16)
    return r, f1, g, ginv, finv


DFT_BB = 32


def _dft1_kernel(f_ref, x_ref, o_ref, *, a0):
    f = f_ref[...]
    r = o_ref.shape[2]
    x = pltpu.einshape("abc->bac", x_ref[0, a0:].astype(BF16))
    ys = [jnp.dot(f, x[j], preferred_element_type=F32) for j in range(DFT_BB)]
    y = pltpu.einshape("jkc->kjc", jnp.stack([v.astype(o_ref.dtype) for v in ys], axis=0))
    o_ref[0, 0] = y[:r]
    o_ref[0, 1] = y[r:]


def _dft_stage1(f, x4, a0=0):
    b, rows, r, c = x4.shape
    r1 = f.shape[0] // 2
    return pl.pallas_call(
        functools.partial(_dft1_kernel, a0=a0),
        grid=(b, r // DFT_BB),
        in_specs=[_const_spec(f.shape), pl.BlockSpec((1, rows, DFT_BB, c), lambda bi, j: (bi, 0, j, 0))],
        out_specs=pl.BlockSpec((1, 2, r1, DFT_BB, c), lambda bi, j: (bi, 0, 0, j, 0)),
        out_shape=jax.ShapeDtypeStruct((b, 2, r1, r, c), BF16),
        compiler_params=_params("parallel", "parallel"),
        name="hy_dft1",
    )(f, x4)


def _fspec_kernel(af_ref, ab_ref, g_ref, o_ref, *, kb, r):
    for kk in range(kb):
        xf = jnp.dot(g_ref[kk], af_ref[0, :, kk].reshape(2 * r, af_ref.shape[-1]), preferred_element_type=F32)
        xb = jnp.dot(g_ref[kk], ab_ref[0, :, kk].reshape(2 * r, ab_ref.shape[-1]), preferred_element_type=F32)
        o_ref[0, kk, :r] = xf[:r] + xb[:r]
        o_ref[0, kk, r:] = xf[r:] - xb[r:]


def _filter_spectrum(a, g, kb):
    o2, _, r1, r, c = a.shape
    blk = lambda d: pl.BlockSpec((1, 2, kb, r, c), lambda oo, j: (2 * oo + d, 0, j, 0, 0))
    return pl.pallas_call(
        functools.partial(_fspec_kernel, kb=kb, r=r),
        grid=(o2 // 2, r1 // kb),
        in_specs=[blk(0), blk(1), pl.BlockSpec((kb, 2 * r, 2 * r), lambda oo, j: (j, 0, 0))],
        out_specs=pl.BlockSpec((1, kb, 2 * r, c), lambda oo, j: (oo, j, 0, 0)),
        out_shape=jax.ShapeDtypeStruct((o2 // 2, r1, 2 * r, c), F32),
        compiler_params=_params("parallel", "parallel"),
        name="hy_fspec",
    )(a, a, g)


def _dft2_kernel(a_ref, g_ref, gi_ref, k_ref, o_ref, *, kb, r):
    xs = [jnp.dot(g_ref[kk], a_ref[0, :, kk].reshape(2 * r, a_ref.shape[-1]), preferred_element_type=F32)
          for kk in range(kb)]
    ys = []
    for kk in range(kb):
        x, ks = xs[kk], k_ref[0, kk]
        xr, xi, kr, ki = x[:r], x[r:], ks[:r], ks[r:]
        ys.append(jnp.concatenate([xr * kr - xi * ki, xr * ki + xi * kr], axis=0).astype(BF16))
    bqs = [jnp.dot(gi_ref[kk], ys[kk], preferred_element_type=F32) for kk in range(kb)]
    for kk in range(kb):
        o_ref[0, 0, kk] = bqs[kk][:r].astype(o_ref.dtype)
        o_ref[0, 1, kk] = bqs[kk][r:].astype(o_ref.dtype)


def _dft_stage2(a, g, ginv, kspec, order, kb):
    b, _, r1, r, c = a.shape
    blk = pl.BlockSpec((1, 2, kb, r, c), lambda j, bb: (bb, 0, j, 0, 0))
    tab = pl.BlockSpec((kb, 2 * r, 2 * r), lambda j, bb: (j, 0, 0))
    return pl.pallas_call(
        functools.partial(_dft2_kernel, kb=kb, r=r),
        grid=(r1 // kb, b),
        in_specs=[blk, tab, tab, pl.BlockSpec((1, kb, 2 * r, c), lambda j, bb: (order, j, 0, 0))],
        out_specs=blk,
        out_shape=jax.ShapeDtypeStruct(a.shape, BF16),
        compiler_params=_params("parallel", "arbitrary"),
        name="hy_dft2",
    )(a, g, ginv, kspec)


def _dft3_kernel(f_ref, b_ref, gate_ref, z_ref, bias_ref, head_ref, o_ref, *, a0):
    f = f_ref[...]
    bq = jnp.concatenate([b_ref[0, 0], b_ref[0, 1]], axis=0)
    bq = pltpu.einshape("kjc->jkc", bq)
    ys = [jnp.dot(f, bq[j], preferred_element_type=F32) for j in range(DFT_BB)]
    y = pltpu.einshape("jac->ajc", jnp.stack(ys, axis=0))
    o_ref[0, :a0] = head_ref[0]
    o_ref[0, a0:] = gate_ref[0, a0:] * (y + z_ref[0, a0:] * bias_ref[...])


def _dft_stage3(finv, bq5, gate4, z4, bias, head4):
    b, _, r1, r, c = bq5.shape
    rows = z4.shape[1]
    a0 = head4.shape[1]
    io = pl.BlockSpec((1, rows, DFT_BB, c), lambda bi, j: (bi, 0, j, 0))
    return pl.pallas_call(
        functools.partial(_dft3_kernel, a0=a0),
        grid=(b, r // DFT_BB),
        in_specs=[_const_spec(finv.shape), pl.BlockSpec((1, 2, r1, DFT_BB, c), lambda bi, j: (bi, 0, 0, j, 0)), io, io,
                  _const_spec((1, 1, c)), pl.BlockSpec((1, a0, DFT_BB, c), lambda bi, j: (bi, 0, j, 0))],
        out_specs=io,
        out_shape=jax.ShapeDtypeStruct((b, rows, r, c), F32),
        compiler_params=_params("parallel", "parallel"),
        name="hy_dft3",
    )(finv, bq5, gate4, z4, bias, head4)


def _long_conv_gated(v, gates, kern, hy_bias, ctx_out, lc):
    b, t, c = v.shape
    r, f1, g, ginv, finv = _dft_tables(t - lc)
    assert lc % r == 0
    a0 = lc // r
    kb = 8
    kspec = _filter_spectrum(_dft_stage1(f1[:, :r // 2], kern.reshape(2 * HY_ORDER, r // 2, r, c)), g, kb)
    as_blocks = lambda x: x.reshape(b, t // r, r, c)
    z = as_blocks(v)
    heads = [jnp.zeros((b, a0, r, c), F32), ctx_out.reshape(b, a0, r, c)]
    for o in range(HY_ORDER):
        a = _dft_stage1(f1[:, :r // 2], z, a0)
        bq = _dft_stage2(a, g, ginv, kspec, o, kb)
        z = _dft_stage3(finv, bq, as_blocks(gates[o]), z, hy_bias[o].reshape(1, 1, c), heads[o])
    return z.reshape(b, t, c)


def _ctx_conv_kernel(v_ref, x1_ref, x2_ref, k_ref, ff_ref, fi_ref, bias_ref, o_ref):
    n = ff_ref.shape[1]
    z = v_ref[0]
    for o, gate_ref in enumerate((x1_ref, x2_ref)):
        kf = jnp.dot(ff_ref[:, :TM], k_ref[2 * o], preferred_element_type=F32)
        kb = jnp.dot(ff_ref[:, :TM], k_ref[2 * o + 1], preferred_element_type=F32)
        x = jnp.dot(ff_ref[:, :TM], z.astype(BF16), preferred_element_type=F32)
        xr, xi, kr, ki = x[:n], x[n:], kf[:n] + kb[:n], kf[n:] - kb[n:]
        y = jnp.concatenate([xr * kr - xi * ki, xr * ki + xi * kr], axis=0).astype(BF16)
        conv = jnp.dot(fi_ref[...], y, preferred_element_type=F32)
        z = gate_ref[0] * (conv + z * bias_ref[o:o + 1])
    o_ref[0] = z


def _ctx_conv_gated(v, x1, x2, kern, hy_bias):
    b, _, c = v.shape
    seq = TM
    n = 2 * seq
    idx = jnp.arange(n, dtype=jnp.int32)
    th = (2 * math.pi / n) * ((idx[:, None] * idx[None, :]) % n).astype(F32)
    ff = jnp.concatenate([jnp.cos(th), -jnp.sin(th)], axis=0).astype(BF16)
    fi = (jnp.concatenate([jnp.cos(th), -jnp.sin(th)], axis=1)[:seq] / n).astype(BF16)
    io = pl.BlockSpec((1, seq, c), lambda bb: (bb, 0, 0))
    return pl.pallas_call(
        _ctx_conv_kernel,
        grid=(b,),
        in_specs=[io, io, io, _const_spec(kern.shape), _const_spec(ff.shape), _const_spec(fi.shape),
                  _const_spec(hy_bias.shape)],
        out_specs=io,
        out_shape=jax.ShapeDtypeStruct((b, seq, c), F32),
        compiler_params=_params("parallel"),
        name="hy_ctx",
    )(v, x1, x2, kern, ff, fi, hy_bias)


def _hyena(v, x1, x2, lc, f_w1, f_b1, f_w2, f_b2, f_w3, hy_bias):
    seq = v.shape[1] - lc
    yc = _ctx_conv_gated(v, x1, x2, _hyena_kernels(lc, f_w1, f_b1, f_w2, f_b2, f_w3), hy_bias)
    return _long_conv_gated(v, (x1, x2), _hyena_kernels(seq, f_w1, f_b1, f_w2, f_b2, f_w3), hy_bias, yc, lc)


def _rope_tables_t(seq, lc):
    rows = seq // GRID_W
    r = jnp.broadcast_to(jnp.arange(rows, dtype=F32)[:, None], (rows, GRID_W)).reshape(-1)
    col = jnp.broadcast_to(jnp.arange(GRID_W, dtype=F32)[None, :], (rows, GRID_W)).reshape(-1)
    n_pairs = HEAD_DIM // 4
    inv = ROPE_THETA ** (-jnp.arange(n_pairs, dtype=F32) / n_pairs)
    ang = jnp.concatenate([r[:, None] * inv, col[:, None] * inv], axis=-1)
    cos = jnp.concatenate([jnp.ones((lc, 2 * n_pairs), F32), jnp.cos(ang)], axis=0)
    sin = jnp.concatenate([jnp.zeros((lc, 2 * n_pairs), F32), jnp.sin(ang)], axis=0)
    return cos.T, sin.T


def kernel(x, c, ctx, c_ctx, w_ada, b_ada, norm_g, w_mlp_in, w_mlp_out, ev_w_in, ev_w_lr, ev_b_lr, ev_gla_g,
           ev_qk_g, ev_w_out, od_w_in, od_conv_w, od_conv_b, od_f_w1, od_f_b1, od_f_w2, od_f_b2, od_f_w3,
           od_hy_bias, od_lam, od_diff_g, od_w_out):
    bsz, seq, d = x.shape
    lc = ctx.shape[1]
    depth = w_ada.shape[0]
    assert lc == TM and seq % TM == 0 and bsz <= MOD_CTX_ROW and bsz % POST_PB == 0 and bsz % IN_PB == 0 and bsz % GLA_PB == 0
    cos_t, sin_t = _rope_tables_t(seq, lc)
    cc = jnp.zeros((8, d), F32).at[:bsz].set(c).at[MOD_CTX_ROW:MOD_CTX_ROW + max(POST_PB, IN_PB)].set(c_ctx)
    mods = _mods(cc, w_ada, b_ada).reshape(depth, 8, N_MOD, d)
    xs = (x, ctx)
    for i in range(depth):
        last = i == depth - 1
        j = i // 2
        g = norm_g[i]
        if i % 2 == 0:
            qk, la, gv, gg, aq, ak, av, kn = _even_in(xs, mods[i], g[0:1], ev_w_in[j], ev_w_lr[j], ev_b_lr[j],
                                                  ev_qk_g[j], cos_t, sin_t)
            ya = (*_gla(qk, la, gv), gg, ev_gla_g[j])
            yb = _gqa_attention(aq, ak, av, jnp.max(kn, axis=1))
            w_out = ev_w_out[j]
        else:
            hv, hx1, hx2, dq, dk, dv, kn = _odd_in(xs, mods[i], g[0:1], od_w_in[j], od_conv_w[j], od_conv_b[j],
                                                   cos_t, sin_t)
            ya = _hyena(hv, hx1, hx2, lc, od_f_w1[j], od_f_b1[j], od_f_w2[j], od_f_b2[j], od_f_w3[j], od_hy_bias[j])
            yb = _diff_attention(dq, dk, dv, jnp.max(kn, axis=1), od_lam[j], od_diff_g[j], i)
            w_out = od_w_out[j]
        xs = _post(xs, ya, yb, mods[i], g, w_out, w_mlp_in[i], w_mlp_out[i], skip_ctx=last)
    return xs
```
